```python
import jax, jax.numpy as jnp
from jax import lax
from jax.lax import linalg as lax_linalg
import numpy as np

D_MODEL = 2048
BATCH = 16
SEQ = 2048
DEPTH = 2

MIX_WIDTH = D_MODEL
CONV_CH = MIX_WIDTH // 2
CONV_GROUPS = 8
CONV_K = 31
GDN_HEAD_DIM = 128
GDN_V_HEADS = (MIX_WIDTH // 2) // GDN_HEAD_DIM
GDN_QK_HEADS = GDN_V_HEADS // 2
GDN_K_WIDTH = GDN_QK_HEADS * GDN_HEAD_DIM
GDN_V_WIDTH = GDN_V_HEADS * GDN_HEAD_DIM
SHORT_CONV_K = 4
CHUNK = 64
D_FF = ((8 * D_MODEL // 3 + 255) // 256) * 256
FFN_CONV_K = 3
EPS = 1e-6
IN_WIDTH = 2 * CONV_CH + 2 * GDN_K_WIDTH + 2 * GDN_V_WIDTH + 2 * GDN_V_HEADS

kernel_name = "hymba_style_conformer_conv_gated_deltanet_convffn"


def rms_norm(x, g):
    xf = x.astype(jnp.float32)
    y = xf * lax.rsqrt(jnp.mean(xf * xf, axis=-1, keepdims=True) + EPS)
    return (y * g.astype(jnp.float32)).astype(x.dtype)


def group_layer_norm(x, g, b, groups):
    B, S, C = x.shape
    xf = x.astype(jnp.float32).reshape(B, S, groups, C // groups)
    mu = jnp.mean(xf, axis=-1, keepdims=True)
    xc = xf - mu
    var = jnp.mean(xc * xc, axis=-1, keepdims=True)
    y = (xc * lax.rsqrt(var + EPS)).reshape(B, S, C)
    return (y * g.astype(jnp.float32) + b.astype(jnp.float32)).astype(x.dtype)


def causal_depthwise_conv(x, w):
    K, C = w.shape
    return lax.conv_general_dilated(
        x, w[:, None, :].astype(x.dtype), window_strides=(1,), padding=[(K - 1, 0)],
        dimension_numbers=("NWC", "WIO", "NWC"), feature_group_count=C)


def l2_normalize(t):
    return t * lax.rsqrt(jnp.sum(t * t, axis=-1, keepdims=True) + EPS)


def chunk_gated_delta_rule(q, k, v, g, beta):
    B, S, H, dk = q.shape
    dv = v.shape[-1]
    N = S // CHUNK
    q = q * (dk ** -0.5)

    def to_chunks(t):
        return t.reshape(B, N, CHUNK, H, t.shape[-1]).transpose(0, 3, 1, 2, 4)

    q, k, v = to_chunks(q), to_chunks(k), to_chunks(v)
    g = g.reshape(B, N, CHUNK, H).transpose(0, 3, 1, 2)
    beta = beta.reshape(B, N, CHUNK, H).transpose(0, 3, 1, 2)
    g = jnp.cumsum(g, axis=-1)

    causal = jnp.tril(jnp.ones((CHUNK, CHUNK), dtype=bool))
    strict = jnp.tril(jnp.ones((CHUNK, CHUNK), dtype=bool), -1)
    decay = jnp.exp(jnp.where(causal, g[..., :, None] - g[..., None, :], -jnp.inf))

    k_beta = k * beta[..., None]
    v_beta = v * beta[..., None]
    L = jnp.where(strict, jnp.einsum("bhnid,bhnjd->bhnij", k_beta, k) * decay, 0.0)
    A = L + jnp.eye(CHUNK, dtype=L.dtype)
    rhs = jnp.concatenate([v_beta, k_beta * jnp.exp(g)[..., None]], axis=-1)
    sol = lax_linalg.triangular_solve(A, rhs, left_side=True, lower=True, unit_diagonal=True)
    u, w = sol[..., :dv], sol[..., dv:]
    qk = jnp.einsum("bhnid,bhnjd->bhnij", q, k) * decay

    def step(state, inp):
        q_c, k_c, u_c, w_c, g_c, qk_c = inp
        v_new = u_c - jnp.einsum("bhck,bhkv->bhcv", w_c, state)
        o_c = jnp.einsum("bhck,bhkv->bhcv", q_c * jnp.exp(g_c)[..., None], state) + \
            jnp.einsum("bhij,bhjv->bhiv", qk_c, v_new)
        g_last = g_c[..., -1]
        k_dec = k_c * jnp.exp(g_last[..., None] - g_c)[..., None]
        state = state * jnp.exp(g_last)[..., None, None] + jnp.einsum("bhck,bhcv->bhkv", k_dec, v_new)
        return state, o_c

    xs = tuple(jnp.moveaxis(t, 2, 0) for t in (q, k, u, w, g, qk))
    state0 = jnp.zeros((B, H, dk, dv), jnp.float32)
    _, o = lax.scan(step, state0, xs)
    return o.transpose(1, 0, 3, 2, 4).reshape(B, S, H, dv)


def conformer_conv_group(a_val, a_gate, dw_w, dw_b, ln_g, ln_b, pw_w, pw_b):
    u = a_val * jax.nn.sigmoid(a_gate)
    u = causal_depthwise_conv(u, dw_w) + dw_b
    u = group_layer_norm(u, ln_g, ln_b, CONV_GROUPS)
    u = jax.nn.silu(u)
    return u @ pw_w + pw_b


def gated_deltanet_group(q, k, v, z, b_raw, a_raw, conv_w, a_log, dt_bias, norm_g):
    B, S, _ = q.shape
    dtype = q.dtype
    qkv = jax.nn.silu(causal_depthwise_conv(jnp.concatenate([q, k, v], axis=-1), conv_w))
    q, k, v = jnp.split(qkv, [GDN_K_WIDTH, 2 * GDN_K_WIDTH], axis=-1)
    rep = GDN_V_HEADS // GDN_QK_HEADS
    q = jnp.repeat(q.reshape(B, S, GDN_QK_HEADS, GDN_HEAD_DIM), rep, axis=2).astype(jnp.float32)
    k = jnp.repeat(k.reshape(B, S, GDN_QK_HEADS, GDN_HEAD_DIM), rep, axis=2).astype(jnp.float32)
    v = v.reshape(B, S, GDN_V_HEADS, GDN_HEAD_DIM).astype(jnp.float32)
    q, k = l2_normalize(q), l2_normalize(k)
    beta = jax.nn.sigmoid(b_raw.astype(jnp.float32))
    g = -jnp.exp(a_log.astype(jnp.float32)) * jax.nn.softplus(
        a_raw.astype(jnp.float32) + dt_bias.astype(jnp.float32))
    o = chunk_gated_delta_rule(q, k, v, g, beta)
    o = o * lax.rsqrt(jnp.mean(o * o, axis=-1, keepdims=True) + EPS) * norm_g.astype(jnp.float32)
    o = o * jax.nn.silu(z.reshape(B, S, GDN_V_HEADS, GDN_HEAD_DIM).astype(jnp.float32))
    return o.reshape(B, S, GDN_V_WIDTH).astype(dtype)


def _fwd_setup_inputs(seed: int = 0) -> dict:
    key = jax.random.key(seed)
    ks = jax.random.split(key, 24)
    L = DEPTH

    def nrm(k, shape, scale):
        return jax.random.normal(k, shape, jnp.float32) * scale

    dt = jnp.exp(jax.random.uniform(ks[10], (L, GDN_V_HEADS), jnp.float32,
                                    np.log(1e-3).astype(np.float32), np.log(1e-1).astype(np.float32)))
    return {
        "x": nrm(ks[0], (BATCH, SEQ, D_MODEL), 1.0),
        "mix_norm_g": 1.0 + nrm(ks[1], (L, D_MODEL), 0.02),
        "w_in": nrm(ks[2], (L, D_MODEL, IN_WIDTH), D_MODEL ** -0.5),
        "conv_dw_w": nrm(ks[3], (L, CONV_K, CONV_CH), CONV_K ** -0.5),
        "conv_dw_b": nrm(ks[4], (L, CONV_CH), 0.02),
        "conv_ln_g": 1.0 + nrm(ks[5], (L, CONV_CH), 0.02),
        "conv_ln_b": nrm(ks[6], (L, CONV_CH), 0.02),
        "conv_pw_w": nrm(ks[7], (L, CONV_CH, CONV_CH), CONV_CH ** -0.5),
        "conv_pw_b": nrm(ks[8], (L, CONV_CH), 0.02),
        "gdn_conv_w": nrm(ks[9], (L, SHORT_CONV_K, 2 * GDN_K_WIDTH + GDN_V_WIDTH), SHORT_CONV_K ** -0.5),
        "gdn_a_log": jnp.log(jax.random.uniform(ks[11], (L, GDN_V_HEADS), jnp.float32, 1.0, 16.0)),
        "gdn_dt_bias": dt + jnp.log(-jnp.expm1(-dt)),
        "gdn_norm_g": 1.0 + nrm(ks[12], (L, GDN_HEAD_DIM), 0.02),
        "w_out": nrm(ks[13], (L, MIX_WIDTH, D_MODEL), MIX_WIDTH ** -0.5),
        "ffn_norm_g": 1.0 + nrm(ks[14], (L, D_MODEL), 0.02),
        "w_up": nrm(ks[15], (L, D_MODEL, 2 * D_FF), D_MODEL ** -0.5),
        "ffn_conv_w": nrm(ks[16], (L, FFN_CONV_K, D_FF), FFN_CONV_K ** -0.5),
        "ffn_conv_b": nrm(ks[17], (L, D_FF), 0.02),
        "w_down": nrm(ks[18], (L, D_FF, D_MODEL), D_FF ** -0.5),
        "final_norm_g": 1.0 + nrm(ks[19], (D_MODEL,), 0.02),
    }


def _fwd_reference(x, mix_norm_g, w_in, conv_dw_w, conv_dw_b, conv_ln_g, conv_ln_b, conv_pw_w, conv_pw_b,
              gdn_conv_w, gdn_a_log, gdn_dt_bias, gdn_norm_g, w_out, ffn_norm_g, w_up,
              ffn_conv_w, ffn_conv_b, w_down, final_norm_g):
    splits = np.cumsum([CONV_CH, CONV_CH, GDN_K_WIDTH, GDN_K_WIDTH, GDN_V_WIDTH, GDN_V_WIDTH,
                        GDN_V_HEADS]).tolist()
    for l in range(DEPTH):
        h = rms_norm(x, mix_norm_g[l])
        p = h @ w_in[l]
        a_val, a_gate, q, k, v, z, b_raw, a_raw = jnp.split(p, splits, axis=-1)
        out_a = conformer_conv_group(a_val, a_gate, conv_dw_w[l], conv_dw_b[l], conv_ln_g[l],
                                     conv_ln_b[l], conv_pw_w[l], conv_pw_b[l])
        out_b = gated_deltanet_group(q, k, v, z, b_raw, a_raw, gdn_conv_w[l], gdn_a_log[l],
                                     gdn_dt_bias[l], gdn_norm_g[l])
        x = x + jnp.concatenate([out_a, out_b], axis=-1) @ w_out[l]
        h = rms_norm(x, ffn_norm_g[l])
        gate, up = jnp.split(h @ w_up[l], 2, axis=-1)
        gate = causal_depthwise_conv(gate, ffn_conv_w[l]) + ffn_conv_b[l]
        x = x + (jax.nn.silu(gate) * up) @ w_down[l]
    return rms_norm(x, final_norm_g)


import jax as _jax
import jax.numpy as _jnp

TWIN_FORMAT = 'train_step'
FWD_PARAMS = ['x', 'mix_norm_g', 'w_in', 'conv_dw_w', 'conv_dw_b', 'conv_ln_g', 'conv_ln_b', 'conv_pw_w', 'conv_pw_b', 'gdn_conv_w', 'gdn_a_log', 'gdn_dt_bias', 'gdn_norm_g', 'w_out', 'ffn_norm_g', 'w_up', 'ffn_conv_w', 'ffn_conv_b', 'w_down', 'final_norm_g']
TWIN_WEIGHTS = ['mix_norm_g', 'w_in', 'conv_dw_w', 'conv_dw_b', 'conv_ln_g', 'conv_ln_b', 'conv_pw_w', 'conv_pw_b', 'gdn_conv_w', 'gdn_a_log', 'gdn_dt_bias', 'gdn_norm_g', 'w_out', 'ffn_norm_g', 'w_up', 'ffn_conv_w', 'ffn_conv_b', 'w_down', 'final_norm_g']
TWIN_DIFF_INPUT = 'x'
TWIN_INPUTS = ['x', 'mix_norm_g', 'w_in', 'conv_dw_w', 'conv_dw_b', 'conv_ln_g', 'conv_ln_b', 'conv_pw_w', 'conv_pw_b', 'gdn_conv_w', 'gdn_a_log', 'gdn_dt_bias', 'gdn_norm_g', 'w_out', 'ffn_norm_g', 'w_up', 'ffn_conv_w', 'ffn_conv_b', 'w_down', 'final_norm_g', 'loss_target', 'm_mix_norm_g', 'm_w_in', 'm_conv_dw_w', 'm_conv_dw_b', 'm_conv_ln_g', 'm_conv_ln_b', 'm_conv_pw_w', 'm_conv_pw_b', 'm_gdn_conv_w', 'm_gdn_a_log', 'm_gdn_dt_bias', 'm_gdn_norm_g', 'm_w_out', 'm_ffn_norm_g', 'm_w_up', 'm_ffn_conv_w', 'm_ffn_conv_b', 'm_w_down', 'm_final_norm_g', 'v_mix_norm_g', 'v_w_in', 'v_conv_dw_w', 'v_conv_dw_b', 'v_conv_ln_g', 'v_conv_ln_b', 'v_conv_pw_w', 'v_conv_pw_b', 'v_gdn_conv_w', 'v_gdn_a_log', 'v_gdn_dt_bias', 'v_gdn_norm_g', 'v_w_out', 'v_ffn_norm_g', 'v_w_up', 'v_ffn_conv_w', 'v_ffn_conv_b', 'v_w_down', 'v_final_norm_g']
TWIN_OUTPUTS = ['loss', 'grad_x', 'grad_mix_norm_g', 'grad_w_in', 'grad_conv_dw_w', 'grad_conv_dw_b', 'grad_conv_ln_g', 'grad_conv_ln_b', 'grad_conv_pw_w', 'grad_conv_pw_b', 'grad_gdn_conv_w', 'grad_gdn_a_log', 'grad_gdn_dt_bias', 'grad_gdn_norm_g', 'grad_w_out', 'grad_ffn_norm_g', 'grad_w_up', 'grad_ffn_conv_w', 'grad_ffn_conv_b', 'grad_w_down', 'grad_final_norm_g', 'delta_mix_norm_g', 'delta_w_in', 'delta_conv_dw_w', 'delta_conv_dw_b', 'delta_conv_ln_g', 'delta_conv_ln_b', 'delta_conv_pw_w', 'delta_conv_pw_b', 'delta_gdn_conv_w', 'delta_gdn_a_log', 'delta_gdn_dt_bias', 'delta_gdn_norm_g', 'delta_w_out', 'delta_ffn_norm_g', 'delta_w_up', 'delta_ffn_conv_w', 'delta_ffn_conv_b', 'delta_w_down', 'delta_final_norm_g', 'new_m_mix_norm_g', 'new_m_w_in', 'new_m_conv_dw_w', 'new_m_conv_dw_b', 'new_m_conv_ln_g', 'new_m_conv_ln_b', 'new_m_conv_pw_w', 'new_m_conv_pw_b', 'new_m_gdn_conv_w', 'new_m_gdn_a_log', 'new_m_gdn_dt_bias', 'new_m_gdn_norm_g', 'new_m_w_out', 'new_m_ffn_norm_g', 'new_m_w_up', 'new_m_ffn_conv_w', 'new_m_ffn_conv_b', 'new_m_w_down', 'new_m_final_norm_g', 'new_v_mix_norm_g', 'new_v_w_in', 'new_v_conv_dw_w', 'new_v_conv_dw_b', 'new_v_conv_ln_g', 'new_v_conv_ln_b', 'new_v_conv_pw_w', 'new_v_conv_pw_b', 'new_v_gdn_conv_w', 'new_v_gdn_a_log', 'new_v_gdn_dt_bias', 'new_v_gdn_norm_g', 'new_v_w_out', 'new_v_ffn_norm_g', 'new_v_w_up', 'new_v_ffn_conv_w', 'new_v_ffn_conv_b', 'new_v_w_down', 'new_v_final_norm_g']
TWIN_LEAF_KINDS = {'loss': 'loss', 'grad_x': 'grad_x', 'grad_mix_norm_g': 'grad_w', 'grad_w_in': 'grad_w', 'grad_conv_dw_w': 'grad_w', 'grad_conv_dw_b': 'grad_w', 'grad_conv_ln_g': 'grad_w', 'grad_conv_ln_b': 'grad_w', 'grad_conv_pw_w': 'grad_w', 'grad_conv_pw_b': 'grad_w', 'grad_gdn_conv_w': 'grad_w', 'grad_gdn_a_log': 'grad_w', 'grad_gdn_dt_bias': 'grad_w', 'grad_gdn_norm_g': 'grad_w', 'grad_w_out': 'grad_w', 'grad_ffn_norm_g': 'grad_w', 'grad_w_up': 'grad_w', 'grad_ffn_conv_w': 'grad_w', 'grad_ffn_conv_b': 'grad_w', 'grad_w_down': 'grad_w', 'grad_final_norm_g': 'grad_w', 'delta_mix_norm_g': 'delta_w', 'delta_w_in': 'delta_w', 'delta_conv_dw_w': 'delta_w', 'delta_conv_dw_b': 'delta_w', 'delta_conv_ln_g': 'delta_w', 'delta_conv_ln_b': 'delta_w', 'delta_conv_pw_w': 'delta_w', 'delta_conv_pw_b': 'delta_w', 'delta_gdn_conv_w': 'delta_w', 'delta_gdn_a_log': 'delta_w', 'delta_gdn_dt_bias': 'delta_w', 'delta_gdn_norm_g': 'delta_w', 'delta_w_out': 'delta_w', 'delta_ffn_norm_g': 'delta_w', 'delta_w_up': 'delta_w', 'delta_ffn_conv_w': 'delta_w', 'delta_ffn_conv_b': 'delta_w', 'delta_w_down': 'delta_w', 'delta_final_norm_g': 'delta_w', 'new_m_mix_norm_g': 'new_m', 'new_m_w_in': 'new_m', 'new_m_conv_dw_w': 'new_m', 'new_m_conv_dw_b': 'new_m', 'new_m_conv_ln_g': 'new_m', 'new_m_conv_ln_b': 'new_m', 'new_m_conv_pw_w': 'new_m', 'new_m_conv_pw_b': 'new_m', 'new_m_gdn_conv_w': 'new_m', 'new_m_gdn_a_log': 'new_m', 'new_m_gdn_dt_bias': 'new_m', 'new_m_gdn_norm_g': 'new_m', 'new_m_w_out': 'new_m', 'new_m_ffn_norm_g': 'new_m', 'new_m_w_up': 'new_m', 'new_m_ffn_conv_w': 'new_m', 'new_m_ffn_conv_b': 'new_m', 'new_m_w_down': 'new_m', 'new_m_final_norm_g': 'new_m', 'new_v_mix_norm_g': 'new_v', 'new_v_w_in': 'new_v', 'new_v_conv_dw_w': 'new_v', 'new_v_conv_dw_b': 'new_v', 'new_v_conv_ln_g': 'new_v', 'new_v_conv_ln_b': 'new_v', 'new_v_conv_pw_w': 'new_v', 'new_v_conv_pw_b': 'new_v', 'new_v_gdn_conv_w': 'new_v', 'new_v_gdn_a_log': 'new_v', 'new_v_gdn_dt_bias': 'new_v', 'new_v_gdn_norm_g': 'new_v', 'new_v_w_out': 'new_v', 'new_v_ffn_norm_g': 'new_v', 'new_v_w_up': 'new_v', 'new_v_ffn_conv_w': 'new_v', 'new_v_ffn_conv_b': 'new_v', 'new_v_w_down': 'new_v', 'new_v_final_norm_g': 'new_v'}


def _forward(args):
    return _fwd_reference(*[args[k] for k in FWD_PARAMS])


def _output_shape():
    out = _jax.eval_shape(lambda: _forward(_fwd_setup_inputs(0)))
    return out.shape, out.dtype

N_MICROBATCH = 1
ADAM_LR = 0.001
ADAM_B1 = 0.9
ADAM_B2 = 0.999
ADAM_EPS = 1e-08
ADAM_WD = 0.01
ADAM_STEP = 10
PER_EXAMPLE_BATCH_AXIS = {'x': 0, 'loss_target': 0}
SHARED_INPUTS = []
_WEIGHT_DTYPES = {'mix_norm_g': _jnp.float32, 'w_in': _jnp.float32, 'conv_dw_w': _jnp.float32, 'conv_dw_b': _jnp.float32, 'conv_ln_g': _jnp.float32, 'conv_ln_b': _jnp.float32, 'conv_pw_w': _jnp.float32, 'conv_pw_b': _jnp.float32, 'gdn_conv_w': _jnp.float32, 'gdn_a_log': _jnp.float32, 'gdn_dt_bias': _jnp.float32, 'gdn_norm_g': _jnp.float32, 'w_out': _jnp.float32, 'ffn_norm_g': _jnp.float32, 'w_up': _jnp.float32, 'ffn_conv_w': _jnp.float32, 'ffn_conv_b': _jnp.float32, 'w_down': _jnp.float32, 'final_norm_g': _jnp.float32}
MOMENT_SCALE = {'mix_norm_g': 7.098349e-02, 'w_in': 4.458806e-02, 'conv_dw_w': 5.267172e-02, 'conv_dw_b': 1.170029e-01, 'conv_ln_g': 6.114939e-02, 'conv_ln_b': 5.591200e-02, 'conv_pw_w': 5.093796e-02, 'conv_pw_b': 9.354805e-02, 'gdn_conv_w': 4.649406e-02, 'gdn_a_log': 3.571071e-01, 'gdn_dt_bias': 3.639508e-01, 'gdn_norm_g': 1.644744e-01, 'w_out': 5.029514e-02, 'ffn_norm_g': 6.082978e-02, 'w_up': 2.563263e-02, 'ffn_conv_w': 2.614480e-02, 'ffn_conv_b': 2.499340e-02, 'w_down': 4.203304e-02, 'final_norm_g': 1.599892e+01}


def _to_microbatches(a, axis):
    t = _jnp.moveaxis(a, axis, 0)
    t = t.reshape((N_MICROBATCH, t.shape[0] // N_MICROBATCH) + t.shape[1:])
    return _jnp.moveaxis(t, 1, axis + 1)


def setup_inputs(seed: int = 0) -> dict:
    inp = _fwd_setup_inputs(seed)
    key = _jax.random.fold_in(_jax.random.key(seed), 7919)
    shape, _ = _output_shape()
    out = dict(inp)
    out["loss_target"] = _jax.random.normal(_jax.random.fold_in(key, 0), shape, _jnp.float32)
    for i, name in enumerate(TWIN_WEIGHTS):
        w = inp[name].astype(_jnp.float32)
        if MOMENT_SCALE is None:
            s = _jnp.sqrt(_jnp.mean(_jnp.square(w)) + 1e-30)
        else:
            s = MOMENT_SCALE[name]
        km, kv = _jax.random.split(_jax.random.fold_in(key, i + 1))
        out[name] = w
        out["m_" + name] = s * _jax.random.normal(km, w.shape, _jnp.float32)
        out["v_" + name] = (s * s) * _jax.random.uniform(kv, w.shape, _jnp.float32, 0.5, 1.5)
    if N_MICROBATCH > 1:
        for name, axis in PER_EXAMPLE_BATCH_AXIS.items():
            out[name] = _to_microbatches(out[name], axis)
    return {'x': out['x'], 'mix_norm_g': out['mix_norm_g'], 'w_in': out['w_in'], 'conv_dw_w': out['conv_dw_w'], 'conv_dw_b': out['conv_dw_b'], 'conv_ln_g': out['conv_ln_g'], 'conv_ln_b': out['conv_ln_b'], 'conv_pw_w': out['conv_pw_w'], 'conv_pw_b': out['conv_pw_b'], 'gdn_conv_w': out['gdn_conv_w'], 'gdn_a_log': out['gdn_a_log'], 'gdn_dt_bias': out['gdn_dt_bias'], 'gdn_norm_g': out['gdn_norm_g'], 'w_out': out['w_out'], 'ffn_norm_g': out['ffn_norm_g'], 'w_up': out['w_up'], 'ffn_conv_w': out['ffn_conv_w'], 'ffn_conv_b': out['ffn_conv_b'], 'w_down': out['w_down'], 'final_norm_g': out['final_norm_g'], 'loss_target': out['loss_target'], 'm_mix_norm_g': out['m_mix_norm_g'], 'm_w_in': out['m_w_in'], 'm_conv_dw_w': out['m_conv_dw_w'], 'm_conv_dw_b': out['m_conv_dw_b'], 'm_conv_ln_g': out['m_conv_ln_g'], 'm_conv_ln_b': out['m_conv_ln_b'], 'm_conv_pw_w': out['m_conv_pw_w'], 'm_conv_pw_b': out['m_conv_pw_b'], 'm_gdn_conv_w': out['m_gdn_conv_w'], 'm_gdn_a_log': out['m_gdn_a_log'], 'm_gdn_dt_bias': out['m_gdn_dt_bias'], 'm_gdn_norm_g': out['m_gdn_norm_g'], 'm_w_out': out['m_w_out'], 'm_ffn_norm_g': out['m_ffn_norm_g'], 'm_w_up': out['m_w_up'], 'm_ffn_conv_w': out['m_ffn_conv_w'], 'm_ffn_conv_b': out['m_ffn_conv_b'], 'm_w_down': out['m_w_down'], 'm_final_norm_g': out['m_final_norm_g'], 'v_mix_norm_g': out['v_mix_norm_g'], 'v_w_in': out['v_w_in'], 'v_conv_dw_w': out['v_conv_dw_w'], 'v_conv_dw_b': out['v_conv_dw_b'], 'v_conv_ln_g': out['v_conv_ln_g'], 'v_conv_ln_b': out['v_conv_ln_b'], 'v_conv_pw_w': out['v_conv_pw_w'], 'v_conv_pw_b': out['v_conv_pw_b'], 'v_gdn_conv_w': out['v_gdn_conv_w'], 'v_gdn_a_log': out['v_gdn_a_log'], 'v_gdn_dt_bias': out['v_gdn_dt_bias'], 'v_gdn_norm_g': out['v_gdn_norm_g'], 'v_w_out': out['v_w_out'], 'v_ffn_norm_g': out['v_ffn_norm_g'], 'v_w_up': out['v_w_up'], 'v_ffn_conv_w': out['v_ffn_conv_w'], 'v_ffn_conv_b': out['v_ffn_conv_b'], 'v_w_down': out['v_w_down'], 'v_final_norm_g': out['v_final_norm_g']}


def _loss(weights, diff, rest, loss_target):
    with _jax.named_scope("forward"):
        args = {**rest, TWIN_DIFF_INPUT: diff, **{k: w.astype(_WEIGHT_DTYPES[k]) for k, w in weights.items()}}
        y = _forward(args)
    with _jax.named_scope("loss_head"):
        err = _jnp.square(y.astype(_jnp.float32) - loss_target)
        return 0.5 * _jnp.sum(_jnp.mean(err, axis=-1)) if err.ndim else 0.5 * err


def _adamw(w, g, m, v):
    m = ADAM_B1 * m + (1.0 - ADAM_B1) * g
    v = ADAM_B2 * v + (1.0 - ADAM_B2) * _jnp.square(g)
    m_hat = m / (1.0 - ADAM_B1 ** ADAM_STEP)
    v_hat = v / (1.0 - ADAM_B2 ** ADAM_STEP)
    delta = -ADAM_LR * (m_hat / (_jnp.sqrt(v_hat) + ADAM_EPS) + ADAM_WD * w)
    return delta, m, v


def reference(x, mix_norm_g, w_in, conv_dw_w, conv_dw_b, conv_ln_g, conv_ln_b, conv_pw_w, conv_pw_b, gdn_conv_w, gdn_a_log, gdn_dt_bias, gdn_norm_g, w_out, ffn_norm_g, w_up, ffn_conv_w, ffn_conv_b, w_down, final_norm_g, loss_target, m_mix_norm_g, m_w_in, m_conv_dw_w, m_conv_dw_b, m_conv_ln_g, m_conv_ln_b, m_conv_pw_w, m_conv_pw_b, m_gdn_conv_w, m_gdn_a_log, m_gdn_dt_bias, m_gdn_norm_g, m_w_out, m_ffn_norm_g, m_w_up, m_ffn_conv_w, m_ffn_conv_b, m_w_down, m_final_norm_g, v_mix_norm_g, v_w_in, v_conv_dw_w, v_conv_dw_b, v_conv_ln_g, v_conv_ln_b, v_conv_pw_w, v_conv_pw_b, v_gdn_conv_w, v_gdn_a_log, v_gdn_dt_bias, v_gdn_norm_g, v_w_out, v_ffn_norm_g, v_w_up, v_ffn_conv_w, v_ffn_conv_b, v_w_down, v_final_norm_g):
    given = dict(x=x, mix_norm_g=mix_norm_g, w_in=w_in, conv_dw_w=conv_dw_w, conv_dw_b=conv_dw_b, conv_ln_g=conv_ln_g, conv_ln_b=conv_ln_b, conv_pw_w=conv_pw_w, conv_pw_b=conv_pw_b, gdn_conv_w=gdn_conv_w, gdn_a_log=gdn_a_log, gdn_dt_bias=gdn_dt_bias, gdn_norm_g=gdn_norm_g, w_out=w_out, ffn_norm_g=ffn_norm_g, w_up=w_up, ffn_conv_w=ffn_conv_w, ffn_conv_b=ffn_conv_b, w_down=w_down, final_norm_g=final_norm_g, loss_target=loss_target, m_mix_norm_g=m_mix_norm_g, m_w_in=m_w_in, m_conv_dw_w=m_conv_dw_w, m_conv_dw_b=m_conv_dw_b, m_conv_ln_g=m_conv_ln_g, m_conv_ln_b=m_conv_ln_b, m_conv_pw_w=m_conv_pw_w, m_conv_pw_b=m_conv_pw_b, m_gdn_conv_w=m_gdn_conv_w, m_gdn_a_log=m_gdn_a_log, m_gdn_dt_bias=m_gdn_dt_bias, m_gdn_norm_g=m_gdn_norm_g, m_w_out=m_w_out, m_ffn_norm_g=m_ffn_norm_g, m_w_up=m_w_up, m_ffn_conv_w=m_ffn_conv_w, m_ffn_conv_b=m_ffn_conv_b, m_w_down=m_w_down, m_final_norm_g=m_final_norm_g, v_mix_norm_g=v_mix_norm_g, v_w_in=v_w_in, v_conv_dw_w=v_conv_dw_w, v_conv_dw_b=v_conv_dw_b, v_conv_ln_g=v_conv_ln_g, v_conv_ln_b=v_conv_ln_b, v_conv_pw_w=v_conv_pw_w, v_conv_pw_b=v_conv_pw_b, v_gdn_conv_w=v_gdn_conv_w, v_gdn_a_log=v_gdn_a_log, v_gdn_dt_bias=v_gdn_dt_bias, v_gdn_norm_g=v_gdn_norm_g, v_w_out=v_w_out, v_ffn_norm_g=v_ffn_norm_g, v_w_up=v_w_up, v_ffn_conv_w=v_ffn_conv_w, v_ffn_conv_b=v_ffn_conv_b, v_w_down=v_w_down, v_final_norm_g=v_final_norm_g)
    weights = {n: given[n] for n in TWIN_WEIGHTS}
    shared = {n: given[n] for n in SHARED_INPUTS}
    per_example = {n: given[n] for n in ['x']}
    grad_fn = _jax.value_and_grad(_loss, argnums=(0, 1))

    def one_microbatch(ex, loss_target):
        ex = dict(ex)
        diff = ex.pop(TWIN_DIFF_INPUT)
        return grad_fn(weights, diff, {**shared, **ex}, loss_target)

    if N_MICROBATCH == 1:
        loss, (grad_w, grad_x) = one_microbatch(per_example, given["loss_target"])
    else:
        def body(carry, xs):
            loss_sum, grad_sum = carry
            l_k, (gw_k, gx_k) = one_microbatch(xs[0], xs[1])
            with _jax.named_scope("update"):
                return (loss_sum + l_k, _jax.tree.map(_jnp.add, grad_sum, gw_k)), gx_k

        init = (_jnp.zeros((), _jnp.float32), _jax.tree.map(_jnp.zeros_like, weights))
        (loss, grad_w), grad_x = _jax.lax.scan(body, init, (per_example, given["loss_target"]))
    with _jax.named_scope("update"):
        delta_w, new_m, new_v = {}, {}, {}
        for n in TWIN_WEIGHTS:
            delta_w[n], new_m[n], new_v[n] = _adamw(weights[n], grad_w[n], given["m_" + n], given["v_" + n])
    return (loss, grad_x, *[grad_w[n] for n in TWIN_WEIGHTS], *[delta_w[n] for n in TWIN_WEIGHTS],
            *[new_m[n] for n in TWIN_WEIGHTS], *[new_v[n] for n in TWIN_WEIGHTS])
```

```python
import functools
import math

import jax
import jax.numpy as jnp
from jax import lax
from jax.experimental import pallas as pl
from jax.experimental.pallas import tpu as pltpu

F32 = jnp.float32
BF16 = jnp.bfloat16
HIGHEST = lax.Precision.HIGHEST
MESH = pl.DeviceIdType.MESH

EPS = 1e-6
LANES = 128
CHUNK = 128
CONV_K = 31
SHORT_CONV_K = 4
FFN_CONV_K = 3
N_DEV = 8
VMEM_LIMIT = 56 * 1024 * 1024

ADAM_LR = 0.001
ADAM_B1 = 0.9
ADAM_B2 = 0.999
ADAM_EPS = 1e-08
ADAM_WD = 0.01
ADAM_STEP = 10


def _cparams(sem):
    return pltpu.CompilerParams(dimension_semantics=sem, vmem_limit_bytes=VMEM_LIMIT)


def _sig(x):
    return 1.0 / (1.0 + jnp.exp(-x))


def _silu(x):
    return x * _sig(x)


def _dsilu(x):
    s = _sig(x)
    return s * (1.0 + x * (1.0 - s))


def _softplus(x):
    return jnp.maximum(x, 0.0) + jnp.log1p(jnp.exp(-jnp.abs(x)))


def _dot(a, b, precision=None):
    return jnp.dot(a, b, preferred_element_type=F32, precision=precision)


def _dot_nt(a, b):
    return lax.dot_general(a, b, (((1,), (1,)), ((), ())), preferred_element_type=F32)


def _dot_tn(a, b):
    return lax.dot_general(a, b, (((0,), (0,)), ((), ())), preferred_element_type=F32)


def _bf(x):
    return x.astype(BF16)


def _shift_down(u, s):
    if s == 0:
        return u
    row = lax.broadcasted_iota(jnp.int32, u.shape, 0)
    return jnp.where(row >= s, pltpu.roll(u, s, 0), 0.0)


def _shift_up(u, s):
    if s == 0:
        return u
    n = u.shape[0]
    row = lax.broadcasted_iota(jnp.int32, u.shape, 0)
    return jnp.where(row < n - s, pltpu.roll(u, n - s, 0), 0.0)


def _conv_fwd(u, w_ref, K):
    acc = None
    for k in range(K):
        term = w_ref[k:k + 1, :] * _shift_down(u, K - 1 - k)
        acc = term if acc is None else acc + term
    return acc


def _conv_bwd_in(do, w_ref, K):
    acc = None
    for k in range(K):
        term = w_ref[k:k + 1, :] * _shift_up(do, K - 1 - k)
        acc = term if acc is None else acc + term
    return acc


def _conv_bwd_w(do, u, dw_ref, K, first):
    for k in range(K):
        row = jnp.sum(do * _shift_down(u, K - 1 - k), axis=0, keepdims=True)
        _acc_row(dw_ref, k, row, first)


def _acc_row(ref, k, row, first):
    @pl.when(first)
    def _():
        ref[k:k + 1, :] = row

    @pl.when(jnp.logical_not(first))
    def _():
        ref[k:k + 1, :] += row


def _logical(arr):
    if arr.ndim == 2:
        return arr.shape
    return (arr.shape[1], arr.shape[0] * arr.shape[2])


def _tile(dim, pref, *col_widths):
    if dim % LANES:
        assert not col_widths
        return dim
    t = (min(pref, dim) // LANES) * LANES
    while t > LANES and (dim % t or any(c % t for c in col_widths)):
        t -= LANES
    assert dim % t == 0 and all(c % t == 0 for c in col_widths), (dim, pref, col_widths)
    return t


def _spec(shape, rt, ct, rfn, cfn):
    if len(shape) == 2:
        return pl.BlockSpec((rt, ct), lambda i, j, k: (rfn(i, j, k), cfn(i, j, k)))
    per = shape[2] // ct
    return pl.BlockSpec((None, rt, ct),
                        lambda i, j, k: (cfn(i, j, k) // per, rfn(i, j, k), cfn(i, j, k) % per))


def _mm(a, b, *, name, ta=False, tb=False, out_dtype=F32, out_blocks=None, bias=None, res=None,
        tm=1024, tn=1024, tk=1024):
    ra, ca = _logical(a)
    rb, cb = _logical(b)
    M, K = (ca, ra) if ta else (ra, ca)
    N, K2 = (rb, cb) if tb else (cb, rb)
    assert K == K2, (a.shape, b.shape, ta, tb)
    out_shape = (M, N) if out_blocks is None else (out_blocks, M, N // out_blocks)
    cw = lambda arr: [arr.shape[2]] if arr.ndim == 3 else []
    m_c = cw(a) if ta else []
    k_c = (cw(a) if not ta else []) + (cw(b) if tb else [])
    n_c = (cw(b) if not tb else []) + ([out_shape[2]] if out_blocks else []) + (cw(res) if res is not None else [])
    tm, tn, tk = _tile(M, tm, *m_c), _tile(N, tn, *n_c), _tile(K, tk, *k_c)
    nk = K // tk
    im, jn, kk = (lambda i, j, k: i), (lambda i, j, k: j), (lambda i, j, k: k)
    in_specs = [
        _spec(a.shape, tk, tm, kk, im) if ta else _spec(a.shape, tm, tk, im, kk),
        _spec(b.shape, tn, tk, jn, kk) if tb else _spec(b.shape, tk, tn, kk, jn),
    ]
    args = [a, b]
    if bias is not None:
        in_specs.append(pl.BlockSpec((1, tn), lambda i, j, k: (0, j)))
        args.append(bias.reshape(1, N).astype(F32))
    if res is not None:
        in_specs.append(_spec(res.shape, tm, tn, im, jn))
        args.append(res)
    dn = (((0 if ta else 1,), (1 if tb else 0,)), ((), ()))

    def body(*refs):
        a_ref, b_ref = refs[0], refs[1]
        pos = 2
        bias_ref = res_ref = None
        if bias is not None:
            bias_ref = refs[pos]
            pos += 1
        if res is not None:
            res_ref = refs[pos]
            pos += 1
        o_ref, acc_ref = refs[pos], refs[pos + 1]
        k = pl.program_id(2)
        part = lax.dot_general(_bf(a_ref[...]), _bf(b_ref[...]), dn, preferred_element_type=F32)

        @pl.when(k == 0)
        def _():
            acc_ref[...] = part

        @pl.when(k > 0)
        def _():
            acc_ref[...] += part

        @pl.when(k == nk - 1)
        def _():
            r = acc_ref[...]
            if bias_ref is not None:
                r = r + bias_ref[...]
            if res_ref is not None:
                r = r + res_ref[...].astype(F32)
            o_ref[...] = r.astype(out_dtype)

    return pl.pallas_call(
        body, name=name,
        grid=(M // tm, N // tn, nk),
        in_specs=in_specs,
        out_specs=_spec(out_shape, tm, tn, im, jn),
        out_shape=jax.ShapeDtypeStruct(out_shape, out_dtype),
        scratch_shapes=[pltpu.VMEM((tm, tn), F32)],
        compiler_params=_cparams(("parallel", "parallel", "arbitrary")),
    )(*args)


def _rms_fwd(x, g, *, name, tr=512):
    T, D = x.shape
    tr = min(tr, T)

    def body(x_ref, g_ref, h_ref):
        xv = x_ref[...]
        r = lax.rsqrt(jnp.mean(xv * xv, axis=-1, keepdims=True) + EPS)
        h_ref[...] = (xv * r * g_ref[...]).astype(BF16)

    return pl.pallas_call(
        body, name=name, grid=(T // tr,),
        in_specs=[pl.BlockSpec((tr, D), lambda i: (i, 0)), pl.BlockSpec((1, D), lambda i: (0, 0))],
        out_specs=pl.BlockSpec((tr, D), lambda i: (i, 0)),
        out_shape=jax.ShapeDtypeStruct((T, D), BF16),
        compiler_params=_cparams(("parallel",)),
    )(x, g.reshape(1, D))


def _rms_bwd(x, g, dh, dres, *, name, tr=512):
    T, D = x.shape
    tr = min(tr, T)

    def body(x_ref, g_ref, dh_ref, dres_ref, dx_ref, dg_ref):
        i = pl.program_id(0)
        xv = x_ref[...]
        dy = dh_ref[...].astype(F32)
        r = lax.rsqrt(jnp.mean(xv * xv, axis=-1, keepdims=True) + EPS)
        dyg = dy * g_ref[...]
        dot = jnp.mean(dyg * xv, axis=-1, keepdims=True)
        dx_ref[...] = dres_ref[...] + r * dyg - xv * (r * r * r) * dot
        part = jnp.sum(dy * xv * r, axis=0, keepdims=True)
        _acc_row(dg_ref, 0, part, i == 0)

    return pl.pallas_call(
        body, name=name, grid=(T // tr,),
        in_specs=[pl.BlockSpec((tr, D), lambda i: (i, 0)), pl.BlockSpec((1, D), lambda i: (0, 0)),
                  pl.BlockSpec((tr, D), lambda i: (i, 0)), pl.BlockSpec((tr, D), lambda i: (i, 0))],
        out_specs=[pl.BlockSpec((tr, D), lambda i: (i, 0)), pl.BlockSpec((1, D), lambda i: (0, 0))],
        out_shape=[jax.ShapeDtypeStruct((T, D), F32), jax.ShapeDtypeStruct((1, D), F32)],
        compiler_params=_cparams(("arbitrary",)),
    )(x, g.reshape(1, D), dh, dres)


def _loss_head(x, g, target, *, name, tr=512):
    T, D = x.shape
    tr = min(tr, T)

    def body(x_ref, g_ref, t_ref, loss_ref, dx_ref, dg_ref):
        i = pl.program_id(0)
        xv = x_ref[...]
        gv = g_ref[...]
        r = lax.rsqrt(jnp.mean(xv * xv, axis=-1, keepdims=True) + EPS)
        y = xv * r * gv
        err = y - t_ref[...]
        lpart = 0.5 * jnp.sum(jnp.mean(err * err, axis=-1, keepdims=True), axis=0, keepdims=True)
        dy = err * (1.0 / D)
        dyg = dy * gv
        dot = jnp.mean(dyg * xv, axis=-1, keepdims=True)
        dx_ref[...] = r * dyg - xv * (r * r * r) * dot
        _acc_row(dg_ref, 0, jnp.sum(dy * xv * r, axis=0, keepdims=True), i == 0)
        _acc_row(loss_ref, 0, jnp.broadcast_to(lpart, (1, LANES)), i == 0)

    return pl.pallas_call(
        body, name=name, grid=(T // tr,),
        in_specs=[pl.BlockSpec((tr, D), lambda i: (i, 0)), pl.BlockSpec((1, D), lambda i: (0, 0)),
                  pl.BlockSpec((tr, D), lambda i: (i, 0))],
        out_specs=[pl.BlockSpec((1, LANES), lambda i: (0, 0)), pl.BlockSpec((tr, D), lambda i: (i, 0)),
                   pl.BlockSpec((1, D), lambda i: (0, 0))],
        out_shape=[jax.ShapeDtypeStruct((1, LANES), F32), jax.ShapeDtypeStruct((T, D), F32),
                   jax.ShapeDtypeStruct((1, D), F32)],
        compiler_params=_cparams(("arbitrary",)),
    )(x, g.reshape(1, D), target)


def _conf_chain(av, ag, w_ref, b_ref, lg_ref, lb_ref):
    sg = _sig(ag)
    u0 = av * sg
    u1 = _conv_fwd(u0, w_ref, CONV_K) + b_ref[...]
    mu = jnp.mean(u1, axis=-1, keepdims=True)
    xc = u1 - mu
    r = lax.rsqrt(jnp.mean(xc * xc, axis=-1, keepdims=True) + EPS)
    n = xc * r
    u2 = n * lg_ref[...] + lb_ref[...]
    return sg, u0, r, n, u2


def _conf_fwd(p, dw_w, dw_b, ln_g, ln_b, *, Bl, S, CC, name):
    G = CC // LANES

    def body(av_ref, ag_ref, w_ref, b_ref, lg_ref, lb_ref, o_ref):
        _, _, _, _, u2 = _conf_chain(av_ref[...], ag_ref[...], w_ref, b_ref, lg_ref, lb_ref)
        o_ref[...] = _silu(u2).astype(BF16)

    vec = pl.BlockSpec((1, LANES), lambda b, j: (0, j))
    return pl.pallas_call(
        body, name=name, grid=(Bl, G),
        in_specs=[pl.BlockSpec((S, LANES), lambda b, j: (b, j)),
                  pl.BlockSpec((S, LANES), lambda b, j: (b, G + j)),
                  pl.BlockSpec((CONV_K, LANES), lambda b, j: (0, j)), vec, vec, vec],
        out_specs=pl.BlockSpec((S, LANES), lambda b, j: (b, j)),
        out_shape=jax.ShapeDtypeStruct((Bl * S, CC), BF16),
        compiler_params=_cparams(("parallel", "parallel")),
    )(p, p, dw_w, dw_b.reshape(1, CC), ln_g.reshape(1, CC), ln_b.reshape(1, CC))


def _conf_bwd(p, dw_w, dw_b, ln_g, ln_b, du3, *, Bl, S, CC, name):
    G = CC // LANES

    def body(av_ref, ag_ref, w_ref, b_ref, lg_ref, lb_ref, du3_ref,
             dav_ref, dag_ref, dw_ref, db_ref, dlg_ref, dlb_ref):
        first = pl.program_id(1) == 0
        av = av_ref[...]
        sg, u0, r, n, u2 = _conf_chain(av, ag_ref[...], w_ref, b_ref, lg_ref, lb_ref)
        du2 = du3_ref[...] * _dsilu(u2)
        _acc_row(dlg_ref, 0, jnp.sum(du2 * n, axis=0, keepdims=True), first)
        _acc_row(dlb_ref, 0, jnp.sum(du2, axis=0, keepdims=True), first)
        dn = du2 * lg_ref[...]
        du1 = r * (dn - jnp.mean(dn, axis=-1, keepdims=True) - n * jnp.mean(dn * n, axis=-1, keepdims=True))
        _acc_row(db_ref, 0, jnp.sum(du1, axis=0, keepdims=True), first)
        _conv_bwd_w(du1, u0, dw_ref, CONV_K, first)
        du0 = _conv_bwd_in(du1, w_ref, CONV_K)
        dav_ref[...] = (du0 * sg).astype(BF16)
        dag_ref[...] = (du0 * av * sg * (1.0 - sg)).astype(BF16)

    vec = pl.BlockSpec((1, LANES), lambda j, b: (0, j))
    seq = pl.BlockSpec((S, LANES), lambda j, b: (b, j))
    return pl.pallas_call(
        body, name=name, grid=(G, Bl),
        in_specs=[seq, pl.BlockSpec((S, LANES), lambda j, b: (b, G + j)),
                  pl.BlockSpec((CONV_K, LANES), lambda j, b: (0, j)), vec, vec, vec, seq],
        out_specs=[seq, seq, pl.BlockSpec((CONV_K, LANES), lambda j, b: (0, j)), vec, vec, vec],
        out_shape=[jax.ShapeDtypeStruct((Bl * S, CC), BF16), jax.ShapeDtypeStruct((Bl * S, CC), BF16),
                   jax.ShapeDtypeStruct((CONV_K, CC), F32), jax.ShapeDtypeStruct((1, CC), F32),
                   jax.ShapeDtypeStruct((1, CC), F32), jax.ShapeDtypeStruct((1, CC), F32)],
        compiler_params=_cparams(("parallel", "arbitrary")),
    )(p, p, dw_w, dw_b.reshape(1, CC), ln_g.reshape(1, CC), ln_b.reshape(1, CC), du3)


def _gdn_pre_fwd(p, conv_w, *, Bl, S, CC, KW, VW, name):
    NQK = 2 * KW // LANES
    NB = NQK + VW // LANES
    off = 2 * CC // LANES

    def body(x_ref, w_ref, o_ref):
        j = pl.program_id(1)
        s = _silu(_conv_fwd(x_ref[...], w_ref, SHORT_CONV_K))
        r = lax.rsqrt(jnp.sum(s * s, axis=-1, keepdims=True) + EPS)
        o_ref[...] = jnp.where(j < NQK, s * r, s)

    return pl.pallas_call(
        body, name=name, grid=(Bl, NB),
        in_specs=[pl.BlockSpec((S, LANES), lambda b, j: (b, off + j)),
                  pl.BlockSpec((SHORT_CONV_K, LANES), lambda b, j: (0, j))],
        out_specs=pl.BlockSpec((S, LANES), lambda b, j: (b, j)),
        out_shape=jax.ShapeDtypeStruct((Bl * S, NB * LANES), F32),
        compiler_params=_cparams(("parallel", "parallel")),
    )(p, conv_w)


def _gdn_pre_bwd(p, conv_w, dq, dk, dv, *, Bl, S, CC, KW, VW, name):
    HQ = KW // LANES
    H = VW // LANES
    NQK = 2 * HQ
    NB = NQK + H
    off = 2 * CC // LANES

    def body(x_ref, w_ref, dq_ref, dk_ref, dv_ref, dx_ref, dw_ref):
        j = pl.program_id(0)
        first = pl.program_id(1) == 0
        xv = x_ref[...]
        c = _conv_fwd(xv, w_ref, SHORT_CONV_K)
        s = _silu(c)
        r = lax.rsqrt(jnp.sum(s * s, axis=-1, keepdims=True) + EPS)
        dy = jnp.where(j < HQ, dq_ref[...], jnp.where(j < NQK, dk_ref[...], dv_ref[...]))
        ds_norm = r * dy - s * (r * r * r) * jnp.sum(s * dy, axis=-1, keepdims=True)
        ds = jnp.where(j < NQK, ds_norm, dy)
        dc = ds * _dsilu(c)
        _conv_bwd_w(dc, xv, dw_ref, SHORT_CONV_K, first)
        dx_ref[...] = _conv_bwd_in(dc, w_ref, SHORT_CONV_K).astype(BF16)

    return pl.pallas_call(
        body, name=name, grid=(NB, Bl),
        in_specs=[pl.BlockSpec((S, LANES), lambda j, b: (b, off + j)),
                  pl.BlockSpec((SHORT_CONV_K, LANES), lambda j, b: (0, j)),
                  pl.BlockSpec((S, LANES), lambda j, b: (b, jnp.minimum(j, HQ - 1))),
                  pl.BlockSpec((S, LANES), lambda j, b: (b, jnp.clip(j - HQ, 0, HQ - 1))),
                  pl.BlockSpec((S, LANES), lambda j, b: (b, jnp.clip(j - NQK, 0, H - 1)))],
        out_specs=[pl.BlockSpec((S, LANES), lambda j, b: (b, j)),
                   pl.BlockSpec((SHORT_CONV_K, LANES), lambda j, b: (0, j))],
        out_shape=[jax.ShapeDtypeStruct((Bl * S, NB * LANES), BF16),
                   jax.ShapeDtypeStruct((SHORT_CONV_K, NB * LANES), F32)],
        compiler_params=_cparams(("parallel", "arbitrary")),
    )(p, conv_w, dq, dk, dv)


def _lane_pick(h):
    row = lax.broadcasted_iota(jnp.int32, (LANES, LANES), 0)
    return (row == h).astype(F32)


def _gdn_gate_fwd(pba, a_log, dt_bias, *, Bl, S, H, name):
    def body(alog_ref, dtb_ref, x_ref, g_ref, beta_ref):
        xv = x_ref[...]
        for h in range(H):
            b_raw = _dot(xv, _lane_pick(h), HIGHEST)
            a_raw = _dot(xv, _lane_pick(H + h), HIGHEST)
            beta_ref[h] = _sig(b_raw)
            ea = jnp.exp(jnp.zeros((1, LANES), F32) + alog_ref[h])
            g_ref[h] = -ea * _softplus(a_raw + dtb_ref[h])

    smem = pl.BlockSpec(memory_space=pltpu.SMEM)
    rep = pl.BlockSpec((H, S, LANES), lambda b: (0, b, 0))
    return pl.pallas_call(
        body, name=name, grid=(Bl,),
        in_specs=[smem, smem, pl.BlockSpec((S, LANES), lambda b: (b, 0))],
        out_specs=[rep, rep],
        out_shape=[jax.ShapeDtypeStruct((H, Bl * S, LANES), F32)] * 2,
        compiler_params=_cparams(("parallel",)),
    )(a_log, dt_bias, pba)


def _gdn_gate_bwd(pba, a_log, dt_bias, dg, dbeta, *, Bl, S, H, name):
    HP = 8 * ((H + 7) // 8)

    def body(alog_ref, dtb_ref, x_ref, dg_ref, dbeta_ref, dx_ref, dalog_ref, ddtb_ref):
        first = pl.program_id(0) == 0
        xv = x_ref[...]
        lane = lax.broadcasted_iota(jnp.int32, (S, LANES), 1)
        acc = jnp.zeros((S, LANES), F32)

        @pl.when(first)
        def _():
            dalog_ref[...] = jnp.zeros_like(dalog_ref)
            ddtb_ref[...] = jnp.zeros_like(ddtb_ref)

        for h in range(H):
            b_raw = _dot(xv, _lane_pick(h), HIGHEST)
            a_raw = _dot(xv, _lane_pick(H + h), HIGHEST)
            beta = _sig(b_raw)
            db_raw = dbeta_ref[h] * beta * (1.0 - beta)
            z = a_raw + dtb_ref[h]
            ea = jnp.exp(jnp.zeros((1, LANES), F32) + alog_ref[h])
            dgv = dg_ref[h]
            da_raw = dgv * (-ea) * _sig(z)
            g = -ea * _softplus(z)
            dalog_ref[h:h + 1, :] += jnp.sum(dgv * g, axis=0, keepdims=True)
            ddtb_ref[h:h + 1, :] += jnp.sum(da_raw, axis=0, keepdims=True)
            acc = acc + jnp.where(lane == h, db_raw, 0.0) + jnp.where(lane == H + h, da_raw, 0.0)
        dx_ref[...] = acc.astype(BF16)

    smem = pl.BlockSpec(memory_space=pltpu.SMEM)
    rep = pl.BlockSpec((H, S, LANES), lambda b: (0, b, 0))
    small = pl.BlockSpec((HP, LANES), lambda b: (0, 0))
    return pl.pallas_call(
        body, name=name, grid=(Bl,),
        in_specs=[smem, smem, pl.BlockSpec((S, LANES), lambda b: (b, 0)), rep, rep],
        out_specs=[pl.BlockSpec((S, LANES), lambda b: (b, 0)), small, small],
        out_shape=[jax.ShapeDtypeStruct((Bl * S, LANES), BF16),
                   jax.ShapeDtypeStruct((HP, LANES), F32), jax.ShapeDtypeStruct((HP, LANES), F32)],
        compiler_params=_cparams(("arbitrary",)),
    )(a_log, dt_bias, pba, dg, dbeta)


def _tri_masks():
    ri = lax.broadcasted_iota(jnp.int32, (CHUNK, CHUNK), 0)
    ci = lax.broadcasted_iota(jnp.int32, (CHUNK, CHUNK), 1)
    return ri >= ci, ri > ci, ri == CHUNK - 1


def _tri_inv(L):
    ri = lax.broadcasted_iota(jnp.int32, (CHUNK, CHUNK), 0)
    ci = lax.broadcasted_iota(jnp.int32, (CHUNK, CHUNK), 1)
    T = jnp.where(ri == ci, 1.0, 0.0) - L
    P = L
    for _ in range(int(math.log2(CHUNK)) - 1):
        Pb = _bf(P)
        P = _dot(Pb, Pb)
        T = T + _dot(_bf(T), _bf(P))
    return T


def _chunk_local(q, k, v, beta, g):
    ge, gt, last = _tri_masks()
    gam = _dot(ge.astype(F32), g, HIGHEST)
    D = jnp.where(ge, jnp.exp(jnp.where(ge, gam - gam.T, 0.0)), 0.0)
    kb = k * beta
    vb = v * beta
    M = _dot_nt(_bf(kb), _bf(k))
    L = jnp.where(gt, M * D, 0.0)
    eg = jnp.exp(gam)
    kbg = kb * eg
    P = _dot_nt(_bf(q), _bf(k))
    QK = jnp.where(ge, P * D, 0.0)
    gl = jnp.sum(jnp.where(last, gam, 0.0), axis=0, keepdims=True)
    el = jnp.exp(gl - gam)
    return dict(ge=ge, gt=gt, last=last, gam=gam, D=D, kb=kb, vb=vb, L=L, eg=eg, kbg=kbg, QK=QK, gl=gl,
                el=el, kd=k * el, qg=q * eg)


def _chunk_fwd(q, k, v, beta, g, S):
    c = _chunk_local(q, k, v, beta, g)
    Tb = _bf(_tri_inv(c["L"]))
    u = _dot(Tb, _bf(c["vb"]))
    w = _dot(Tb, _bf(c["kbg"]))
    Sb = _bf(S)
    vn = u - _dot(_bf(w), Sb)
    o = _dot(_bf(c["qg"]), Sb) + _dot(_bf(c["QK"]), _bf(vn))
    S2 = S * jnp.exp(c["gl"]) + _dot_tn(_bf(c["kd"]), _bf(vn))
    return o, S2, Tb


def _rowsum(x):
    return jnp.sum(x, axis=-1, keepdims=True)


def _chunk_bwd(q, k, v, beta, g, S, Tb, do, dS2):
    c = _chunk_local(q, k, v, beta, g)
    ge, gt, last = c["ge"], c["gt"], c["last"]
    u = _dot(Tb, _bf(c["vb"]))
    w = _dot(Tb, _bf(c["kbg"]))
    Sb = _bf(S)
    vn = u - _dot(_bf(w), Sb)
    dob, vnb, dS2b = _bf(do), _bf(vn), _bf(dS2)
    e_last = jnp.exp(c["gl"])
    dqg = _dot_nt(dob, Sb)
    dS = _dot_tn(_bf(c["qg"]), dob)
    dQK = jnp.where(ge, _dot_nt(dob, vnb), 0.0)
    dvn = _dot_tn(_bf(c["QK"]), dob)
    dS = dS + dS2 * e_last
    de_last = jnp.sum(jnp.sum(dS2 * S, axis=0, keepdims=True), axis=1, keepdims=True)
    dkd = _dot_nt(vnb, dS2b)
    dvn = dvn + _dot(_bf(c["kd"]), dS2b)
    dvnb = _bf(dvn)
    dw = -_dot_nt(dvnb, Sb)
    dS = dS - _dot_tn(_bf(w), dvnb)
    dvb = _dot_tn(Tb, dvnb)
    dkbg = _dot_tn(Tb, _bf(dw))
    dA = -(_dot_nt(_bf(dvb), _bf(u)) + _dot_nt(_bf(dkbg), _bf(w)))
    dL = jnp.where(gt, dA, 0.0)
    dM = dL * c["D"]
    dP = dQK * c["D"]
    E = dL * c["L"] + dQK * c["QK"]
    kbf = _bf(k)
    dkb = _dot(_bf(dM), kbf) + dkbg * c["eg"]
    dk = _dot_tn(_bf(dM), _bf(c["kb"])) + _dot_tn(_bf(dP), _bf(q)) + dkd * c["el"] + dkb * beta
    dq = _dot(_bf(dP), kbf) + dqg * c["eg"]
    s_kd = _rowsum(dkd * c["kd"])
    dgam = (_rowsum(E) - _rowsum(E.T) + _rowsum(dqg * c["qg"]) - s_kd + _rowsum(dkbg * c["kbg"]))
    dgl = jnp.sum(s_kd, axis=0, keepdims=True) + de_last * e_last
    dgam_rep = jnp.broadcast_to(dgam, (CHUNK, LANES)) + jnp.where(last, jnp.broadcast_to(dgl, (CHUNK, LANES)), 0.0)
    dg_rep = lax.dot_general(ge.astype(F32), dgam_rep, (((0,), (0,)), ((), ())),
                             preferred_element_type=F32, precision=HIGHEST)
    dbeta = _rowsum(dkb * k) + _rowsum(dvb * v)
    dv = dvb * beta
    return dq, dk, dv, jnp.broadcast_to(dbeta, (CHUNK, LANES)), dg_rep, dS


def _gdn_core_fwd(qkv, g, beta, *, Bl, S, KW, VW, name):
    HQ = KW // LANES
    H = VW // LANES
    NC = S // CHUNK
    scale = float(LANES) ** -0.5

    def body(q_ref, k_ref, v_ref, g_ref, beta_ref, o_ref, st_ref, t_ref):
        def step(n, states):
            rows = pl.ds(pl.multiple_of(n * CHUNK, CHUNK), CHUNK)
            q = q_ref[rows, :] * scale
            k = k_ref[rows, :]
            out = []
            for e in range(2):
                st_ref[e, n] = states[e]
                o, S2, Tb = _chunk_fwd(q, k, v_ref[rows, e * LANES:(e + 1) * LANES], beta_ref[e, rows, :],
                                       g_ref[e, rows, :], states[e])
                o_ref[rows, e * LANES:(e + 1) * LANES] = o
                t_ref[e, rows, :] = Tb
                out.append(S2)
            return tuple(out)

        z = jnp.zeros((LANES, LANES), F32)
        lax.fori_loop(0, NC, step, (z, z))

    rep = pl.BlockSpec((2, S, LANES), lambda b, h: (h, b, 0))
    return pl.pallas_call(
        body, name=name, grid=(Bl, HQ),
        in_specs=[pl.BlockSpec((S, LANES), lambda b, h: (b, h)),
                  pl.BlockSpec((S, LANES), lambda b, h: (b, HQ + h)),
                  pl.BlockSpec((S, 2 * LANES), lambda b, h: (b, HQ + h)), rep, rep],
        out_specs=[pl.BlockSpec((S, 2 * LANES), lambda b, h: (b, h)),
                   pl.BlockSpec((None, 2, NC, LANES, LANES), lambda b, h: (b, h, 0, 0, 0)), rep],
        out_shape=[jax.ShapeDtypeStruct((Bl * S, VW), F32),
                   jax.ShapeDtypeStruct((Bl, H, NC, LANES, LANES), F32),
                   jax.ShapeDtypeStruct((H, Bl * S, LANES), BF16)],
        compiler_params=_cparams(("parallel", "parallel")),
    )(qkv, qkv, qkv, g, beta)


def _gdn_core_bwd(qkv, g, beta, states, tinv, do, *, Bl, S, KW, VW, name):
    HQ = KW // LANES
    H = VW // LANES
    NC = S // CHUNK
    scale = float(LANES) ** -0.5

    def body(q_ref, k_ref, v_ref, g_ref, beta_ref, st_ref, t_ref, do_ref,
             dq_ref, dk_ref, dv_ref, dg_ref, dbeta_ref):
        def step(i, dstates):
            n = NC - 1 - i
            rows = pl.ds(pl.multiple_of(n * CHUNK, CHUNK), CHUNK)
            q = q_ref[rows, :] * scale
            k = k_ref[rows, :]
            out = []
            dq_sum = dk_sum = None
            for e in range(2):
                cols = slice(e * LANES, (e + 1) * LANES)
                dq, dk, dv, dbeta, dg, dS = _chunk_bwd(q, k, v_ref[rows, cols], beta_ref[e, rows, :],
                                                       g_ref[e, rows, :], st_ref[e, n], t_ref[e, rows, :],
                                                       do_ref[rows, cols], dstates[e])
                dv_ref[rows, cols] = dv
                dg_ref[e, rows, :] = dg
                dbeta_ref[e, rows, :] = dbeta
                dq_sum = dq if dq_sum is None else dq_sum + dq
                dk_sum = dk if dk_sum is None else dk_sum + dk
                out.append(dS)
            dq_ref[rows, :] = dq_sum * scale
            dk_ref[rows, :] = dk_sum
            return tuple(out)

        z = jnp.zeros((LANES, LANES), F32)
        lax.fori_loop(0, NC, step, (z, z))

    rep = pl.BlockSpec((2, S, LANES), lambda b, h: (h, b, 0))
    seq = pl.BlockSpec((S, LANES), lambda b, h: (b, h))
    seq2 = pl.BlockSpec((S, 2 * LANES), lambda b, h: (b, h))
    return pl.pallas_call(
        body, name=name, grid=(Bl, HQ),
        in_specs=[seq, pl.BlockSpec((S, LANES), lambda b, h: (b, HQ + h)),
                  pl.BlockSpec((S, 2 * LANES), lambda b, h: (b, HQ + h)), rep, rep,
                  pl.BlockSpec((None, 2, NC, LANES, LANES), lambda b, h: (b, h, 0, 0, 0)), rep, seq2],
        out_specs=[seq, seq, seq2, rep, rep],
        out_shape=[jax.ShapeDtypeStruct((Bl * S, KW), F32), jax.ShapeDtypeStruct((Bl * S, KW), F32),
                   jax.ShapeDtypeStruct((Bl * S, VW), F32),
                   jax.ShapeDtypeStruct((H, Bl * S, LANES), F32), jax.ShapeDtypeStruct((H, Bl * S, LANES), F32)],
        compiler_params=_cparams(("parallel", "parallel")),
    )(qkv, qkv, qkv, g, beta, states, tinv, do)


def _gdn_out_fwd(o, p, norm_g, out_a, *, CC, VW, name, tr=256):
    T = o.shape[0]
    tr = min(tr, T)
    H = VW // LANES
    zoff = p.shape[1] // VW - 1

    def body(o_ref, z_ref, ng_ref, a_ref, mix_ref):
        mix_ref[:, :CC] = a_ref[...]
        for h in range(H):
            cols = slice(h * LANES, (h + 1) * LANES)
            ov = o_ref[:, cols]
            r = lax.rsqrt(jnp.mean(ov * ov, axis=-1, keepdims=True) + EPS)
            mix_ref[:, CC + h * LANES:CC + (h + 1) * LANES] = (ov * r * ng_ref[...] * _silu(z_ref[:, cols])).astype(BF16)

    return pl.pallas_call(
        body, name=name, grid=(T // tr,),
        in_specs=[pl.BlockSpec((tr, VW), lambda i: (i, 0)), pl.BlockSpec((tr, VW), lambda i: (i, zoff)),
                  pl.BlockSpec((1, LANES), lambda i: (0, 0)), pl.BlockSpec((tr, CC), lambda i: (i, 0))],
        out_specs=pl.BlockSpec((tr, CC + VW), lambda i: (i, 0)),
        out_shape=jax.ShapeDtypeStruct((T, CC + VW), BF16),
        compiler_params=_cparams(("parallel",)),
    )(o, p, norm_g.reshape(1, LANES), out_a)


def _gdn_out_bwd(o, p, norm_g, dmix, *, CC, VW, name, tr=256):
    T = o.shape[0]
    tr = min(tr, T)
    H = VW // LANES
    zoff = p.shape[1] // VW - 1

    def body(o_ref, z_ref, ng_ref, dmix_ref, do_ref, dz_ref, da_ref, dng_ref, dpb_ref):
        first = pl.program_id(0) == 0
        da = dmix_ref[:, :CC]
        da_ref[...] = da.astype(BF16)
        _acc_row(dpb_ref, 0, jnp.sum(da, axis=0, keepdims=True), first)
        ng = ng_ref[...]
        dng = jnp.zeros((1, LANES), F32)
        for h in range(H):
            cols = slice(h * LANES, (h + 1) * LANES)
            ov = o_ref[:, cols]
            zv = z_ref[:, cols]
            dout = dmix_ref[:, CC + h * LANES:CC + (h + 1) * LANES]
            r = lax.rsqrt(jnp.mean(ov * ov, axis=-1, keepdims=True) + EPS)
            on = ov * r * ng
            don = dout * _silu(zv)
            dz_ref[:, cols] = (dout * on * _dsilu(zv)).astype(BF16)
            dng = dng + jnp.sum(don * ov * r, axis=0, keepdims=True)
            dong = don * ng
            do_ref[:, cols] = r * dong - ov * (r * r * r) * jnp.mean(dong * ov, axis=-1, keepdims=True)
        _acc_row(dng_ref, 0, dng, first)

    return pl.pallas_call(
        body, name=name, grid=(T // tr,),
        in_specs=[pl.BlockSpec((tr, VW), lambda i: (i, 0)), pl.BlockSpec((tr, VW), lambda i: (i, zoff)),
                  pl.BlockSpec((1, LANES), lambda i: (0, 0)), pl.BlockSpec((tr, CC + VW), lambda i: (i, 0))],
        out_specs=[pl.BlockSpec((tr, VW), lambda i: (i, 0)), pl.BlockSpec((tr, VW), lambda i: (i, 0)),
                   pl.BlockSpec((tr, CC), lambda i: (i, 0)), pl.BlockSpec((1, LANES), lambda i: (0, 0)),
                   pl.BlockSpec((1, CC), lambda i: (0, 0))],
        out_shape=[jax.ShapeDtypeStruct((T, VW), F32), jax.ShapeDtypeStruct((T, VW), BF16),
                   jax.ShapeDtypeStruct((T, CC), BF16), jax.ShapeDtypeStruct((1, LANES), F32),
                   jax.ShapeDtypeStruct((1, CC), F32)],
        compiler_params=_cparams(("arbitrary",)),
    )(o, p, norm_g.reshape(1, LANES), dmix)


FFN_CW = 256


def _ffn_act_fwd(gu, conv_w, conv_b, *, Bl, S, name):
    FF = gu.shape[2]
    cw = min(FFN_CW, FF)

    def body(g_ref, u_ref, w_ref, b_ref, a_ref):
        gc = _conv_fwd(g_ref[...], w_ref, FFN_CONV_K) + b_ref[...]
        a_ref[...] = (_silu(gc) * u_ref[...]).astype(BF16)

    return pl.pallas_call(
        body, name=name, grid=(Bl, FF // cw),
        in_specs=[pl.BlockSpec((None, S, cw), lambda b, j: (0, b, j)),
                  pl.BlockSpec((None, S, cw), lambda b, j: (1, b, j)),
                  pl.BlockSpec((FFN_CONV_K, cw), lambda b, j: (0, j)),
                  pl.BlockSpec((1, cw), lambda b, j: (0, j))],
        out_specs=pl.BlockSpec((S, cw), lambda b, j: (b, j)),
        out_shape=jax.ShapeDtypeStruct((Bl * S, FF), BF16),
        compiler_params=_cparams(("parallel", "parallel")),
    )(gu, gu, conv_w, conv_b.reshape(1, FF))


def _ffn_act_bwd(gu, conv_w, conv_b, da, *, Bl, S, name):
    FF = gu.shape[2]
    cw = min(FFN_CW, FF)

    def body(g_ref, u_ref, w_ref, b_ref, da_ref, dgu_ref, dw_ref, db_ref):
        first = pl.program_id(1) == 0
        gate = g_ref[...]
        gc = _conv_fwd(gate, w_ref, FFN_CONV_K) + b_ref[...]
        dav = da_ref[...]
        dgu_ref[1] = (dav * _silu(gc)).astype(BF16)
        dgc = dav * u_ref[...] * _dsilu(gc)
        _acc_row(db_ref, 0, jnp.sum(dgc, axis=0, keepdims=True), first)
        _conv_bwd_w(dgc, gate, dw_ref, FFN_CONV_K, first)
        dgu_ref[0] = _conv_bwd_in(dgc, w_ref, FFN_CONV_K).astype(BF16)

    return pl.pallas_call(
        body, name=name, grid=(FF // cw, Bl),
        in_specs=[pl.BlockSpec((None, S, cw), lambda j, b: (0, b, j)),
                  pl.BlockSpec((None, S, cw), lambda j, b: (1, b, j)),
                  pl.BlockSpec((FFN_CONV_K, cw), lambda j, b: (0, j)),
                  pl.BlockSpec((1, cw), lambda j, b: (0, j)),
                  pl.BlockSpec((S, cw), lambda j, b: (b, j))],
        out_specs=[pl.BlockSpec((2, S, cw), lambda j, b: (0, b, j)),
                   pl.BlockSpec((FFN_CONV_K, cw), lambda j, b: (0, j)),
                   pl.BlockSpec((1, cw), lambda j, b: (0, j))],
        out_shape=[jax.ShapeDtypeStruct((2, Bl * S, FF), BF16),
                   jax.ShapeDtypeStruct((FFN_CONV_K, FF), F32), jax.ShapeDtypeStruct((1, FF), F32)],
        compiler_params=_cparams(("parallel", "arbitrary")),
    )(gu, gu, conv_w, conv_b.reshape(1, FF), da)


def _layer_dims(W):
    CC = W["conv_pw_w"].shape[0]
    VW = W["w_out"].shape[0] - CC
    KW = (W["gdn_conv_w"].shape[1] - VW) // 2
    return CC, KW, VW


def _layer_fwd(l, x, W, Bl, S):
    CC, KW, VW = _layer_dims(W)
    H = VW // LANES
    nm = lambda s: s
    h1 = _rms_fwd(x, W["mix_norm_g"], name=nm("rms1_fwd"))
    p = _mm(h1, W["w_in_main"], name=nm("mm_in"))
    pba = _mm(h1, W["w_in_ba"], name=nm("mm_in_ba"))
    u3 = _conf_fwd(p, W["conv_dw_w"], W["conv_dw_b"], W["conv_ln_g"], W["conv_ln_b"], Bl=Bl, S=S, CC=CC,
                   name=nm("conf_fwd"))
    out_a = _mm(u3, W["conv_pw_w"], bias=W["conv_pw_b"], out_dtype=BF16, name=nm("mm_pw"))
    qkv = _gdn_pre_fwd(p, W["gdn_conv_w"], Bl=Bl, S=S, CC=CC, KW=KW, VW=VW, name=nm("gdn_pre_fwd"))
    g, beta = _gdn_gate_fwd(pba, W["gdn_a_log"], W["gdn_dt_bias"], Bl=Bl, S=S, H=H, name=nm("gdn_gate_fwd"))
    o, states, tinv = _gdn_core_fwd(qkv, g, beta, Bl=Bl, S=S, KW=KW, VW=VW, name=nm("gdn_core_fwd"))
    mix = _gdn_out_fwd(o, p, W["gdn_norm_g"], out_a, CC=CC, VW=VW, name=nm("gdn_out_fwd"))
    x1 = _mm(mix, W["w_out"], res=x, name=nm("mm_out"))
    h2 = _rms_fwd(x1, W["ffn_norm_g"], name=nm("rms2_fwd"))
    upw = W["w_up"].shape[2]
    gu = _mm(h2, W["w_up"], out_blocks=2, tn=upw, name=nm("mm_up"))
    a = _ffn_act_fwd(gu, W["ffn_conv_w"], W["ffn_conv_b"], Bl=Bl, S=S, name=nm("ffn_act_fwd"))
    x2 = _mm(a, W["w_down"], res=x1, name=nm("mm_down"))
    saved = dict(x=x, h1=h1, p=p, pba=pba, u3=u3, qkv=qkv, g=g, beta=beta, o=o, states=states, tinv=tinv,
                 mix=mix, x1=x1, h2=h2, gu=gu, a=a)
    return x2, saved


def _layer_bwd(l, dx2, W, A, Bl, S):
    CC, KW, VW = _layer_dims(W)
    H = VW // LANES
    nm = lambda s: s
    G = {}
    da = _mm(dx2, W["w_down"], tb=True, name=nm("mm_down_dx"))
    G["w_down"] = _mm(A["a"], dx2, ta=True, out_dtype=BF16, name=nm("mm_down_dw"))
    dgu, G["ffn_conv_w"], G["ffn_conv_b"] = _ffn_act_bwd(A["gu"], W["ffn_conv_w"], W["ffn_conv_b"], da,
                                                         Bl=Bl, S=S, name=nm("ffn_act_bwd"))
    upw = W["w_up"].shape[2]
    dh2 = _mm(dgu, W["w_up"], tb=True, tk=upw, name=nm("mm_up_dx"))
    G["w_up"] = _mm(A["h2"], dgu, ta=True, out_dtype=BF16, out_blocks=N_DEV, tn=upw, name=nm("mm_up_dw"))
    dx1, G["ffn_norm_g"] = _rms_bwd(A["x1"], W["ffn_norm_g"], dh2, dx2, name=nm("rms2_bwd"))
    dmix = _mm(dx1, W["w_out"], tb=True, name=nm("mm_out_dx"))
    G["w_out"] = _mm(A["mix"], dx1, ta=True, out_dtype=BF16, name=nm("mm_out_dw"))
    do, dz, dout_a, G["gdn_norm_g"], G["conv_pw_b"] = _gdn_out_bwd(A["o"], A["p"], W["gdn_norm_g"], dmix,
                                                                   CC=CC, VW=VW, name=nm("gdn_out_bwd"))
    dq, dk, dv, dg, dbeta = _gdn_core_bwd(A["qkv"], A["g"], A["beta"], A["states"], A["tinv"], do,
                                          Bl=Bl, S=S, KW=KW, VW=VW, name=nm("gdn_core_bwd"))
    dpba, dalog, ddtb = _gdn_gate_bwd(A["pba"], W["gdn_a_log"], W["gdn_dt_bias"], dg, dbeta, Bl=Bl, S=S, H=H,
                                      name=nm("gdn_gate_bwd"))
    G["gdn_a_log"], G["gdn_dt_bias"] = dalog[:H, 0], ddtb[:H, 0]
    dqkv, G["gdn_conv_w"] = _gdn_pre_bwd(A["p"], W["gdn_conv_w"], dq, dk, dv, Bl=Bl, S=S, CC=CC, KW=KW, VW=VW,
                                         name=nm("gdn_pre_bwd"))
    du3 = _mm(dout_a, W["conv_pw_w"], tb=True, name=nm("mm_pw_dx"))
    G["conv_pw_w"] = _mm(A["u3"], dout_a, ta=True, out_dtype=BF16, name=nm("mm_pw_dw"))
    dav, dag, G["conv_dw_w"], G["conv_dw_b"], G["conv_ln_g"], G["conv_ln_b"] = _conf_bwd(
        A["p"], W["conv_dw_w"], W["conv_dw_b"], W["conv_ln_g"], W["conv_ln_b"], du3, Bl=Bl, S=S, CC=CC,
        name=nm("conf_bwd"))
    dp = jnp.concatenate([dav, dag, dqkv, dz], axis=1)
    dh1 = _mm(dpba, W["w_in_ba"], tb=True, name=nm("mm_in_ba_dx"))
    dh1 = _mm(dp, W["w_in_main"], tb=True, res=dh1, name=nm("mm_in_dx"))
    G["w_in_main"] = _mm(A["h1"], dp, ta=True, out_dtype=BF16, name=nm("mm_in_dw"))
    G["w_in_ba"] = _mm(A["h1"], dpba, ta=True, out_dtype=BF16, name=nm("mm_in_ba_dw"))
    dx, G["mix_norm_g"] = _rms_bwd(A["x"], W["mix_norm_g"], dh1, dx1, name=nm("rms1_bwd"))
    return dx, G


def _local_step(x, target, Ws, final_norm_g):
    Bl, S, D = x.shape
    xt = x.reshape(Bl * S, D)
    acts = []
    for l, W in enumerate(Ws):
        xt, A = _layer_fwd(l, xt, W, Bl, S)
        acts.append(A)
    loss, dx, dgf = _loss_head(xt, final_norm_g, target.reshape(Bl * S, D), name="loss_head")
    grads = [None] * len(Ws)
    for l in reversed(range(len(Ws))):
        dx, grads[l] = _layer_bwd(l, dx, Ws[l], acts[l], Bl, S)
    return loss[0, 0], dx.reshape(Bl, S, D), grads, dgf.reshape(D)


def _mesh_pos():
    return lax.axis_index("x"), lax.axis_index("y"), lax.axis_index("c")


def _dev_index(px, py, pc):
    return 4 * px + 2 * py + pc


_ANY = pl.BlockSpec(memory_space=pl.ANY)


def _all_gather(arrs, *, name):
    n = len(arrs)

    def body(*refs):
        ins, outs = refs[:n], refs[n:2 * n]
        send_sems, recv_sems, local_sems = refs[2 * n:]
        x, y, c = _mesh_pos()
        me, sibling = (x, y, c), (x, y, 1 - c)
        chips = [(1 - x, y), (x, 1 - y), (1 - x, 1 - y)]

        def copy(a, k, block, to, src=None):
            dst = outs[a].at[_dev_index(*block)]
            return pltpu.make_async_remote_copy(
                src_ref=dst if src is None else src, dst_ref=dst,
                send_sem=send_sems.at[a, k], recv_sem=recv_sems.at[a, k],
                device_id=to, device_id_type=MESH)

        mine = [pltpu.make_async_copy(ins[a], outs[a].at[_dev_index(*me)], local_sems.at[a]) for a in range(n)]
        for cp in mine:
            cp.start()
        first = []
        for a in range(n):
            first.append(copy(a, 0, me, sibling, src=ins[a]))
            first += [copy(a, 1 + j, me, (*chip, c), src=ins[a]) for j, chip in enumerate(chips)]
        for cp in first:
            cp.start()
        passed = []
        for a in range(n):
            for j, chip in enumerate(chips):
                copy(a, 1 + j, (*chip, c), me).wait_recv()
                fwd = copy(a, 4 + j, (*chip, c), sibling)
                fwd.start()
                passed.append(fwd)
        for a in range(n):
            copy(a, 0, sibling, me).wait_recv()
            for j, chip in enumerate(chips):
                copy(a, 4 + j, (*chip, 1 - c), me).wait_recv()
        for cp in first + passed:
            cp.wait_send()
        for cp in mine:
            cp.wait()

    return pl.pallas_call(
        body, name=name,
        in_specs=[_ANY] * n, out_specs=[_ANY] * n,
        out_shape=[jax.ShapeDtypeStruct((N_DEV,) + a.shape, a.dtype) for a in arrs],
        scratch_shapes=[pltpu.SemaphoreType.DMA((n, N_DEV - 1)), pltpu.SemaphoreType.DMA((n, N_DEV - 1)),
                        pltpu.SemaphoreType.DMA((n,))],
    )(*arrs)


def _scatter_exchange(parts, *, name):
    n = len(parts)

    def body(*refs):
        ins, outs = refs[:n], refs[n:2 * n]
        send_sems, recv_sems, local_sems = refs[2 * n:]
        x, y, c = _mesh_pos()
        me = _dev_index(x, y, c)
        flip = lambda v, f: 1 - v if f else v
        peers = [(flip(x, p & 4), flip(y, p & 2), flip(c, p & 1)) for p in range(1, N_DEV)]

        def copy(a, k, peer):
            return pltpu.make_async_remote_copy(
                src_ref=ins[a].at[_dev_index(*peer)], dst_ref=outs[a].at[me],
                send_sem=send_sems.at[a, k], recv_sem=recv_sems.at[a, k],
                device_id=peer, device_id_type=MESH)

        def arrival(a, k, peer):
            return pltpu.make_async_remote_copy(
                src_ref=ins[a].at[me], dst_ref=outs[a].at[_dev_index(*peer)],
                send_sem=send_sems.at[a, k], recv_sem=recv_sems.at[a, k],
                device_id=peer, device_id_type=MESH)

        mine = [pltpu.make_async_copy(ins[a].at[me], outs[a].at[me], local_sems.at[a]) for a in range(n)]
        for cp in mine:
            cp.start()
        sends = [copy(a, k, peer) for a in range(n) for k, peer in enumerate(peers)]
        for cp in sends:
            cp.start()
        for a in range(n):
            for k, peer in enumerate(peers):
                arrival(a, k, peer).wait_recv()
        for cp in sends:
            cp.wait_send()
        for cp in mine:
            cp.wait()

    return pl.pallas_call(
        body, name=name,
        in_specs=[_ANY] * n, out_specs=[_ANY] * n,
        out_shape=[jax.ShapeDtypeStruct(a.shape, a.dtype) for a in parts],
        scratch_shapes=[pltpu.SemaphoreType.DMA((n, N_DEV - 1)), pltpu.SemaphoreType.DMA((n, N_DEV - 1)),
                        pltpu.SemaphoreType.DMA((n,))],
    )(*parts)


def _adamw_math(w, g, m, v):
    m2 = ADAM_B1 * m + (1.0 - ADAM_B1) * g
    v2 = ADAM_B2 * v + (1.0 - ADAM_B2) * (g * g)
    m_hat = m2 / (1.0 - ADAM_B1 ** ADAM_STEP)
    v_hat = v2 / (1.0 - ADAM_B2 ** ADAM_STEP)
    delta = -ADAM_LR * (m_hat / (jnp.sqrt(v_hat) + ADAM_EPS) + ADAM_WD * w)
    return delta, m2, v2


def _adamw_big(l, w, m, v, recv, *, name, tr=128):
    _, R, C = w.shape
    tr = next(t for t in range(min(tr, R), 0, -16) if R % t == 0)

    def body(w_ref, m_ref, v_ref, r_ref, g_ref, d_ref, m2_ref, v2_ref):
        g = r_ref[0].astype(F32)
        for s in range(1, N_DEV):
            g = g + r_ref[s].astype(F32)
        g_ref[...] = g
        d_ref[...], m2_ref[...], v2_ref[...] = _adamw_math(w_ref[...], g, m_ref[...], v_ref[...])

    wspec = pl.BlockSpec((None, tr, C), lambda i: (l, i, 0))
    ospec = pl.BlockSpec((tr, C), lambda i: (i, 0))
    return pl.pallas_call(
        body, name=name, grid=(R // tr,),
        in_specs=[wspec, wspec, wspec, pl.BlockSpec((N_DEV, tr, C), lambda i: (0, i, 0))],
        out_specs=[ospec] * 4,
        out_shape=[jax.ShapeDtypeStruct((R, C), F32)] * 4,
        compiler_params=_cparams(("parallel",)),
    )(w, m, v, recv.reshape(N_DEV, R, C))


def _sum_slots(gathered, *, name):
    _, R, C = gathered.shape

    def body(r_ref, o_ref):
        g = r_ref[0]
        for s in range(1, N_DEV):
            g = g + r_ref[s]
        o_ref[...] = g

    return pl.pallas_call(body, name=name, out_shape=jax.ShapeDtypeStruct((R, C), F32))(gathered)


def _adamw_small(w, g, m, v, *, name):
    def body(w_ref, g_ref, m_ref, v_ref, d_ref, m2_ref, v2_ref):
        d_ref[...], m2_ref[...], v2_ref[...] = _adamw_math(w_ref[...], g_ref[...], m_ref[...], v_ref[...])

    return pl.pallas_call(body, name=name, out_shape=[jax.ShapeDtypeStruct(w.shape, F32)] * 3)(w, g, m, v)


def _pack(arrs):
    flat = []
    for a in arrs:
        a = a.reshape(-1).astype(F32)
        flat.append(jnp.pad(a, (0, (-a.shape[0]) % LANES)))
    out = jnp.concatenate(flat)
    out = jnp.pad(out, (0, (-out.shape[0]) % (8 * LANES)))
    return out.reshape(-1, LANES)


def _unpack(packed, shapes):
    flat = packed.reshape(-1)
    out, pos = [], 0
    for s in shapes:
        size = math.prod(s)
        out.append(flat[pos:pos + size].reshape(s))
        pos += size + (-size) % LANES
    return out


BIG = ("w_in", "conv_pw_w", "w_out", "w_up", "w_down")
SMALL_SHARDED = ("conv_dw_w", "gdn_conv_w", "ffn_conv_w")
SMALL_REPLICATED = ("mix_norm_g", "conv_dw_b", "conv_ln_g", "conv_ln_b", "conv_pw_b", "gdn_a_log", "gdn_dt_bias",
                    "gdn_norm_g", "ffn_norm_g", "ffn_conv_b")
WEIGHTS = ("mix_norm_g", "w_in", "conv_dw_w", "conv_dw_b", "conv_ln_g", "conv_ln_b", "conv_pw_w", "conv_pw_b",
           "gdn_conv_w", "gdn_a_log", "gdn_dt_bias", "gdn_norm_g", "w_out", "ffn_norm_g", "w_up", "ffn_conv_w",
           "ffn_conv_b", "w_down", "final_norm_g")


def _train_step(x, target, w, m, v):
    L = w["w_in"].shape[0]
    D = x.shape[-1]
    xi, yi, ci = _mesh_pos()
    me = _dev_index(xi, yi, ci)

    send = [w[n][l].astype(BF16) for l in range(L) for n in BIG]
    send += [w[n] for n in SMALL_SHARDED]
    got = _all_gather(send, name="all_gather_weights")
    big = {(n, l): got[l * len(BIG) + i] for l in range(L) for i, n in enumerate(BIG)}
    small_full = {}
    for i, n in enumerate(SMALL_SHARDED):
        g_ = got[L * len(BIG) + i]
        small_full[n] = jnp.moveaxis(g_, 0, 2).reshape(L, g_.shape[2], N_DEV * g_.shape[3])
    Ws = []
    for l in range(L):
        W = {n: w[n][l] for n in SMALL_REPLICATED}
        W.update({n: small_full[n][l] for n in SMALL_SHARDED})
        w_in = jnp.moveaxis(big["w_in", l], 0, 1).reshape(D, -1)
        n_main = (w_in.shape[1] // LANES) * LANES
        W["w_in_main"] = w_in[:, :n_main]
        W["w_in_ba"] = jnp.pad(w_in[:, n_main:], ((0, 0), (0, LANES - (w_in.shape[1] - n_main))))
        W["w_up"] = big["w_up", l]
        for n in ("conv_pw_w", "w_out", "w_down"):
            g_ = big[n, l]
            W[n] = g_.reshape(g_.shape[0] * g_.shape[1], g_.shape[2])
        Ws.append(W)

    loss, grad_x, G, d_final = _local_step(x, target, Ws, w["final_norm_g"])
    loss = lax.psum(loss, ("x", "y", "c"))

    parts = []
    for l in range(L):
        g_in = jnp.concatenate([G[l]["w_in_main"], G[l]["w_in_ba"][:, :w["w_in"].shape[2] * N_DEV - G[l]["w_in_main"].shape[1]]],
                               axis=1)
        g_in = jnp.moveaxis(g_in.reshape(D, N_DEV, -1), 1, 0)
        per = {"w_in": g_in, "w_up": G[l]["w_up"]}
        for n in ("conv_pw_w", "w_out", "w_down"):
            per[n] = G[l][n].reshape(N_DEV, -1, G[l][n].shape[1])
        parts += [per[n] for n in BIG]
    recv = _scatter_exchange(parts, name="scatter_grads")
    out = {k: {} for k in ("grad", "delta", "new_m", "new_v")}
    for i, n in enumerate(BIG):
        res = [_adamw_big(l, w[n], m[n], v[n], recv[l * len(BIG) + i], name=f"adamw_{n}") for l in range(L)]
        for j, k in enumerate(("grad", "delta", "new_m", "new_v")):
            out[k][n] = jnp.stack([r[j] for r in res])

    small_names = [n for n in WEIGHTS if n not in BIG]
    partial = []
    for n in small_names:
        if n == "final_norm_g":
            partial.append(d_final)
        else:
            partial.append(jnp.stack([G[l][n].reshape(Ws[l][n].shape) for l in range(L)]))
    shapes = [p_.shape for p_ in partial]
    gathered = _all_gather([_pack(partial)], name="all_gather_small_grads")[0]
    full = dict(zip(small_names, _unpack(_sum_slots(gathered, name="sum_small_grads"), shapes)))
    for n in SMALL_SHARDED:
        width = w[n].shape[-1]
        full[n] = lax.dynamic_slice_in_dim(full[n], me * width, width, axis=2)
    loc_shapes = [w[n].shape for n in small_names]
    g_pack = _pack([full[n] for n in small_names])
    res = _adamw_small(_pack([w[n] for n in small_names]), g_pack, _pack([m[n] for n in small_names]),
                       _pack([v[n] for n in small_names]), name="adamw_small")
    for k, packed in zip(("grad", "delta", "new_m", "new_v"), (g_pack,) + tuple(res)):
        out[k].update(dict(zip(small_names, _unpack(packed, loc_shapes))))
    return loss, grad_x, out


def kernel(x, mix_norm_g, w_in, conv_dw_w, conv_dw_b, conv_ln_g, conv_ln_b, conv_pw_w, conv_pw_b, gdn_conv_w, gdn_a_log, gdn_dt_bias, gdn_norm_g, w_out, ffn_norm_g, w_up, ffn_conv_w, ffn_conv_b, w_down, final_norm_g, loss_target, m_mix_norm_g, m_w_in, m_conv_dw_w, m_conv_dw_b, m_conv_ln_g, m_conv_ln_b, m_conv_pw_w, m_conv_pw_b, m_gdn_conv_w, m_gdn_a_log, m_gdn_dt_bias, m_gdn_norm_g, m_w_out, m_ffn_norm_g, m_w_up, m_ffn_conv_w, m_ffn_conv_b, m_w_down, m_final_norm_g, v_mix_norm_g, v_w_in, v_conv_dw_w, v_conv_dw_b, v_conv_ln_g, v_conv_ln_b, v_conv_pw_w, v_conv_pw_b, v_gdn_conv_w, v_gdn_a_log, v_gdn_dt_bias, v_gdn_norm_g, v_w_out, v_ffn_norm_g, v_w_up, v_ffn_conv_w, v_ffn_conv_b, v_w_down, v_final_norm_g):
    w = dict(zip(WEIGHTS, (mix_norm_g, w_in, conv_dw_w, conv_dw_b, conv_ln_g, conv_ln_b, conv_pw_w, conv_pw_b, gdn_conv_w,
                           gdn_a_log, gdn_dt_bias, gdn_norm_g, w_out, ffn_norm_g, w_up, ffn_conv_w, ffn_conv_b, w_down,
                           final_norm_g)))
    m = dict(zip(WEIGHTS, (m_mix_norm_g, m_w_in, m_conv_dw_w, m_conv_dw_b, m_conv_ln_g, m_conv_ln_b, m_conv_pw_w,
                           m_conv_pw_b, m_gdn_conv_w, m_gdn_a_log, m_gdn_dt_bias, m_gdn_norm_g, m_w_out, m_ffn_norm_g,
                           m_w_up, m_ffn_conv_w, m_ffn_conv_b, m_w_down, m_final_norm_g)))
    v = dict(zip(WEIGHTS, (v_mix_norm_g, v_w_in, v_conv_dw_w, v_conv_dw_b, v_conv_ln_g, v_conv_ln_b, v_conv_pw_w,
                           v_conv_pw_b, v_gdn_conv_w, v_gdn_a_log, v_gdn_dt_bias, v_gdn_norm_g, v_w_out, v_ffn_norm_g,
                           v_w_up, v_ffn_conv_w, v_ffn_conv_b, v_w_down, v_final_norm_g)))
    loss, grad_x, out = _train_step(x, loss_target, w, m, v)
    return (loss, grad_x, *[out["grad"][n] for n in WEIGHTS], *[out["delta"][n] for n in WEIGHTS],
            *[out["new_m"][n] for n in WEIGHTS], *[out["new_v"][n] for n in WEIGHTS])
```

```python
import functools
import math

import jax
import jax.numpy as jnp
from jax import lax
from jax.experimental import pallas as pl
from jax.experimental.pallas import tpu as pltpu

F32 = jnp.float32
BF16 = jnp.bfloat16
HIGHEST = lax.Precision.HIGHEST
MESH = pl.DeviceIdType.MESH

EPS = 1e-6
LANES = 128
CHUNK = 128
CONV_K = 31
SHORT_CONV_K = 4
FFN_CONV_K = 3
N_DEV = 8
VMEM_LIMIT = 56 * 1024 * 1024

ADAM_LR = 0.001
ADAM_B1 = 0.9
ADAM_B2 = 0.999
ADAM_EPS = 1e-08
ADAM_WD = 0.01
ADAM_STEP = 10


def _cparams(sem):
    return pltpu.CompilerParams(dimension_semantics=sem, vmem_limit_bytes=VMEM_LIMIT)


def _sig(x):
    return 1.0 / (1.0 + jnp.exp(-x))


def _silu(x):
    return x * _sig(x)


def _dsilu(x):
    s = _sig(x)
    return s * (1.0 + x * (1.0 - s))


def _softplus(x):
    return jnp.maximum(x, 0.0) + jnp.log1p(jnp.exp(-jnp.abs(x)))


def _dot(a, b, precision=None):
    return jnp.dot(a, b, preferred_element_type=F32, precision=precision)


def _dot_nt(a, b):
    return lax.dot_general(a, b, (((1,), (1,)), ((), ())), preferred_element_type=F32)


def _dot_tn(a, b):
    return lax.dot_general(a, b, (((0,), (0,)), ((), ())), preferred_element_type=F32)


def _bf(x):
    return x.astype(BF16)


def _shift_down(u, s):
    if s == 0:
        return u
    row = lax.broadcasted_iota(jnp.int32, u.shape, 0)
    return jnp.where(row >= s, pltpu.roll(u, s, 0), 0.0)


def _shift_up(u, s):
    if s == 0:
        return u
    n = u.shape[0]
    row = lax.broadcasted_iota(jnp.int32, u.shape, 0)
    return jnp.where(row < n - s, pltpu.roll(u, n - s, 0), 0.0)


def _conv_fwd(u, w_ref, K):
    acc = None
    for k in range(K):
        term = w_ref[k:k + 1, :] * _shift_down(u, K - 1 - k)
        acc = term if acc is None else acc + term
    return acc


def _conv_bwd_in(do, w_ref, K):
    acc = None
    for k in range(K):
        term = w_ref[k:k + 1, :] * _shift_up(do, K - 1 - k)
        acc = term if acc is None else acc + term
    return acc


def _conv_bwd_w(do, u, dw_ref, K, first):
    for k in range(K):
        row = jnp.sum(do * _shift_down(u, K - 1 - k), axis=0, keepdims=True)
        _acc_row(dw_ref, k, row, first)


def _acc_row(ref, k, row, first):
    @pl.when(first)
    def _():
        ref[k:k + 1, :] = row

    @pl.when(jnp.logical_not(first))
    def _():
        ref[k:k + 1, :] += row


def _logical(arr):
    if arr.ndim == 2:
        return arr.shape
    return (arr.shape[1], arr.shape[0] * arr.shape[2])


def _tile(dim, pref, *col_widths):
    if dim % LANES:
        assert not col_widths
        return dim
    t = (min(pref, dim) // LANES) * LANES
    while t > LANES and (dim % t or any(c % t for c in col_widths)):
        t -= LANES
    assert dim % t == 0 and all(c % t == 0 for c in col_widths), (dim, pref, col_widths)
    return t


def _spec(shape, rt, ct, rfn, cfn):
    if len(shape) == 2:
        return pl.BlockSpec((rt, ct), lambda i, j, k: (rfn(i, j, k), cfn(i, j, k)))
    per = shape[2] // ct
    return pl.BlockSpec((None, rt, ct),
                        lambda i, j, k: (cfn(i, j, k) // per, rfn(i, j, k), cfn(i, j, k) % per))


def _mm(a, b, *, name, ta=False, tb=False, out_dtype=F32, out_blocks=None, bias=None, res=None, dep=None,
        tm=1024, tn=1024, tk=1024):
    ra, ca = _logical(a)
    rb, cb = _logical(b)
    M, K = (ca, ra) if ta else (ra, ca)
    N, K2 = (rb, cb) if tb else (cb, rb)
    assert K == K2, (a.shape, b.shape, ta, tb)
    out_shape = (M, N) if out_blocks is None else (out_blocks, M, N // out_blocks)
    cw = lambda arr: [arr.shape[2]] if arr.ndim == 3 else []
    m_c = cw(a) if ta else []
    k_c = (cw(a) if not ta else []) + (cw(b) if tb else [])
    n_c = (cw(b) if not tb else []) + ([out_shape[2]] if out_blocks else []) + (cw(res) if res is not None else [])
    tm, tn, tk = _tile(M, tm, *m_c), _tile(N, tn, *n_c), _tile(K, tk, *k_c)
    nk = K // tk
    im, jn, kk = (lambda i, j, k: i), (lambda i, j, k: j), (lambda i, j, k: k)
    in_specs = [
        _spec(a.shape, tk, tm, kk, im) if ta else _spec(a.shape, tm, tk, im, kk),
        _spec(b.shape, tn, tk, jn, kk) if tb else _spec(b.shape, tk, tn, kk, jn),
    ]
    args = [a, b]
    if bias is not None:
        in_specs.append(pl.BlockSpec((1, tn), lambda i, j, k: (0, j)))
        args.append(bias.reshape(1, N).astype(F32))
    if res is not None:
        in_specs.append(_spec(res.shape, tm, tn, im, jn))
        args.append(res)
    if dep is not None:
        in_specs.append(pl.BlockSpec(memory_space=pl.ANY))
        args.append(dep)
    dn = (((0 if ta else 1,), (1 if tb else 0,)), ((), ()))

    def body(*refs):
        a_ref, b_ref = refs[0], refs[1]
        pos = 2
        bias_ref = res_ref = None
        if bias is not None:
            bias_ref = refs[pos]
            pos += 1
        if res is not None:
            res_ref = refs[pos]
            pos += 1
        if dep is not None:
            pos += 1
        o_ref, acc_ref = refs[pos], refs[pos + 1]
        k = pl.program_id(2)
        part = lax.dot_general(_bf(a_ref[...]), _bf(b_ref[...]), dn, preferred_element_type=F32)

        @pl.when(k == 0)
        def _():
            acc_ref[...] = part

        @pl.when(k > 0)
        def _():
            acc_ref[...] += part

        @pl.when(k == nk - 1)
        def _():
            r = acc_ref[...]
            if bias_ref is not None:
                r = r + bias_ref[...]
            if res_ref is not None:
                r = r + res_ref[...].astype(F32)
            o_ref[...] = r.astype(out_dtype)

    return pl.pallas_call(
        body, name=name,
        grid=(M // tm, N // tn, nk),
        in_specs=in_specs,
        out_specs=_spec(out_shape, tm, tn, im, jn),
        out_shape=jax.ShapeDtypeStruct(out_shape, out_dtype),
        scratch_shapes=[pltpu.VMEM((tm, tn), F32)],
        compiler_params=_cparams(("parallel", "parallel", "arbitrary")),
    )(*args)


def _rms_fwd(x, g, *, name, tr=512):
    T, D = x.shape
    tr = min(tr, T)

    def body(x_ref, g_ref, h_ref):
        xv = x_ref[...]
        r = lax.rsqrt(jnp.mean(xv * xv, axis=-1, keepdims=True) + EPS)
        h_ref[...] = (xv * r * g_ref[...]).astype(BF16)

    return pl.pallas_call(
        body, name=name, grid=(T // tr,),
        in_specs=[pl.BlockSpec((tr, D), lambda i: (i, 0)), pl.BlockSpec((1, D), lambda i: (0, 0))],
        out_specs=pl.BlockSpec((tr, D), lambda i: (i, 0)),
        out_shape=jax.ShapeDtypeStruct((T, D), BF16),
        compiler_params=_cparams(("parallel",)),
    )(x, g.reshape(1, D))


def _rms_bwd(x, g, dh, dres, *, name, tr=512):
    T, D = x.shape
    tr = min(tr, T)

    def body(x_ref, g_ref, dh_ref, dres_ref, dx_ref, dg_ref):
        i = pl.program_id(0)
        xv = x_ref[...]
        dy = dh_ref[...].astype(F32)
        r = lax.rsqrt(jnp.mean(xv * xv, axis=-1, keepdims=True) + EPS)
        dyg = dy * g_ref[...]
        dot = jnp.mean(dyg * xv, axis=-1, keepdims=True)
        dx_ref[...] = dres_ref[...] + r * dyg - xv * (r * r * r) * dot
        part = jnp.sum(dy * xv * r, axis=0, keepdims=True)
        _acc_row(dg_ref, 0, part, i == 0)

    return pl.pallas_call(
        body, name=name, grid=(T // tr,),
        in_specs=[pl.BlockSpec((tr, D), lambda i: (i, 0)), pl.BlockSpec((1, D), lambda i: (0, 0)),
                  pl.BlockSpec((tr, D), lambda i: (i, 0)), pl.BlockSpec((tr, D), lambda i: (i, 0))],
        out_specs=[pl.BlockSpec((tr, D), lambda i: (i, 0)), pl.BlockSpec((1, D), lambda i: (0, 0))],
        out_shape=[jax.ShapeDtypeStruct((T, D), F32), jax.ShapeDtypeStruct((1, D), F32)],
        compiler_params=_cparams(("arbitrary",)),
    )(x, g.reshape(1, D), dh, dres)


def _loss_head(x, g, target, *, name, tr=512):
    T, D = x.shape
    tr = min(tr, T)

    def body(x_ref, g_ref, t_ref, loss_ref, dx_ref, dg_ref):
        i = pl.program_id(0)
        xv = x_ref[...]
        gv = g_ref[...]
        r = lax.rsqrt(jnp.mean(xv * xv, axis=-1, keepdims=True) + EPS)
        y = xv * r * gv
        err = y - t_ref[...]
        lpart = 0.5 * jnp.sum(jnp.mean(err * err, axis=-1, keepdims=True), axis=0, keepdims=True)
        dy = err * (1.0 / D)
        dyg = dy * gv
        dot = jnp.mean(dyg * xv, axis=-1, keepdims=True)
        dx_ref[...] = r * dyg - xv * (r * r * r) * dot
        _acc_row(dg_ref, 0, jnp.sum(dy * xv * r, axis=0, keepdims=True), i == 0)
        _acc_row(loss_ref, 0, jnp.broadcast_to(lpart, (1, LANES)), i == 0)

    return pl.pallas_call(
        body, name=name, grid=(T // tr,),
        in_specs=[pl.BlockSpec((tr, D), lambda i: (i, 0)), pl.BlockSpec((1, D), lambda i: (0, 0)),
                  pl.BlockSpec((tr, D), lambda i: (i, 0))],
        out_specs=[pl.BlockSpec((1, LANES), lambda i: (0, 0)), pl.BlockSpec((tr, D), lambda i: (i, 0)),
                   pl.BlockSpec((1, D), lambda i: (0, 0))],
        out_shape=[jax.ShapeDtypeStruct((1, LANES), F32), jax.ShapeDtypeStruct((T, D), F32),
                   jax.ShapeDtypeStruct((1, D), F32)],
        compiler_params=_cparams(("arbitrary",)),
    )(x, g.reshape(1, D), target)


def _conf_chain(av, ag, w_ref, b_ref, lg_ref, lb_ref):
    sg = _sig(ag)
    u0 = av * sg
    u1 = _conv_fwd(u0, w_ref, CONV_K) + b_ref[...]
    mu = jnp.mean(u1, axis=-1, keepdims=True)
    xc = u1 - mu
    r = lax.rsqrt(jnp.mean(xc * xc, axis=-1, keepdims=True) + EPS)
    n = xc * r
    u2 = n * lg_ref[...] + lb_ref[...]
    return sg, u0, r, n, u2


def _conf_fwd(p, dw_w, dw_b, ln_g, ln_b, *, Bl, S, CC, name):
    G = CC // LANES

    def body(av_ref, ag_ref, w_ref, b_ref, lg_ref, lb_ref, o_ref):
        _, _, _, _, u2 = _conf_chain(av_ref[...], ag_ref[...], w_ref, b_ref, lg_ref, lb_ref)
        o_ref[...] = _silu(u2).astype(BF16)

    vec = pl.BlockSpec((1, LANES), lambda b, j: (0, j))
    return pl.pallas_call(
        body, name=name, grid=(Bl, G),
        in_specs=[pl.BlockSpec((S, LANES), lambda b, j: (b, j)),
                  pl.BlockSpec((S, LANES), lambda b, j: (b, G + j)),
                  pl.BlockSpec((CONV_K, LANES), lambda b, j: (0, j)), vec, vec, vec],
        out_specs=pl.BlockSpec((S, LANES), lambda b, j: (b, j)),
        out_shape=jax.ShapeDtypeStruct((Bl * S, CC), BF16),
        compiler_params=_cparams(("parallel", "parallel")),
    )(p, p, dw_w, dw_b.reshape(1, CC), ln_g.reshape(1, CC), ln_b.reshape(1, CC))


def _conf_bwd(p, dw_w, dw_b, ln_g, ln_b, du3, *, Bl, S, CC, name):
    G = CC // LANES

    def body(av_ref, ag_ref, w_ref, b_ref, lg_ref, lb_ref, du3_ref,
             dav_ref, dag_ref, dw_ref, db_ref, dlg_ref, dlb_ref):
        first = pl.program_id(1) == 0
        av = av_ref[...]
        sg, u0, r, n, u2 = _conf_chain(av, ag_ref[...], w_ref, b_ref, lg_ref, lb_ref)
        du2 = du3_ref[...] * _dsilu(u2)
        _acc_row(dlg_ref, 0, jnp.sum(du2 * n, axis=0, keepdims=True), first)
        _acc_row(dlb_ref, 0, jnp.sum(du2, axis=0, keepdims=True), first)
        dn = du2 * lg_ref[...]
        du1 = r * (dn - jnp.mean(dn, axis=-1, keepdims=True) - n * jnp.mean(dn * n, axis=-1, keepdims=True))
        _acc_row(db_ref, 0, jnp.sum(du1, axis=0, keepdims=True), first)
        _conv_bwd_w(du1, u0, dw_ref, CONV_K, first)
        du0 = _conv_bwd_in(du1, w_ref, CONV_K)
        dav_ref[...] = (du0 * sg).astype(BF16)
        dag_ref[...] = (du0 * av * sg * (1.0 - sg)).astype(BF16)

    vec = pl.BlockSpec((1, LANES), lambda j, b: (0, j))
    seq = pl.BlockSpec((S, LANES), lambda j, b: (b, j))
    return pl.pallas_call(
        body, name=name, grid=(G, Bl),
        in_specs=[seq, pl.BlockSpec((S, LANES), lambda j, b: (b, G + j)),
                  pl.BlockSpec((CONV_K, LANES), lambda j, b: (0, j)), vec, vec, vec, seq],
        out_specs=[seq, seq, pl.BlockSpec((CONV_K, LANES), lambda j, b: (0, j)), vec, vec, vec],
        out_shape=[jax.ShapeDtypeStruct((Bl * S, CC), BF16), jax.ShapeDtypeStruct((Bl * S, CC), BF16),
                   jax.ShapeDtypeStruct((CONV_K, CC), F32), jax.ShapeDtypeStruct((1, CC), F32),
                   jax.ShapeDtypeStruct((1, CC), F32), jax.ShapeDtypeStruct((1, CC), F32)],
        compiler_params=_cparams(("parallel", "arbitrary")),
    )(p, p, dw_w, dw_b.reshape(1, CC), ln_g.reshape(1, CC), ln_b.reshape(1, CC), du3)


def _gdn_pre_fwd(p, conv_w, *, Bl, S, CC, KW, VW, name):
    NQK = 2 * KW // LANES
    NB = NQK + VW // LANES
    off = 2 * CC // LANES

    def body(x_ref, w_ref, o_ref):
        j = pl.program_id(1)
        s = _silu(_conv_fwd(x_ref[...], w_ref, SHORT_CONV_K))
        r = lax.rsqrt(jnp.sum(s * s, axis=-1, keepdims=True) + EPS)
        o_ref[...] = jnp.where(j < NQK, s * r, s)

    return pl.pallas_call(
        body, name=name, grid=(Bl, NB),
        in_specs=[pl.BlockSpec((S, LANES), lambda b, j: (b, off + j)),
                  pl.BlockSpec((SHORT_CONV_K, LANES), lambda b, j: (0, j))],
        out_specs=pl.BlockSpec((S, LANES), lambda b, j: (b, j)),
        out_shape=jax.ShapeDtypeStruct((Bl * S, NB * LANES), F32),
        compiler_params=_cparams(("parallel", "parallel")),
    )(p, conv_w)


def _gdn_pre_bwd(p, conv_w, dq, dk, dv, *, Bl, S, CC, KW, VW, name):
    HQ = KW // LANES
    H = VW // LANES
    NQK = 2 * HQ
    NB = NQK + H
    off = 2 * CC // LANES

    def body(x_ref, w_ref, dq_ref, dk_ref, dv_ref, dx_ref, dw_ref):
        j = pl.program_id(0)
        first = pl.program_id(1) == 0
        xv = x_ref[...]
        c = _conv_fwd(xv, w_ref, SHORT_CONV_K)
        s = _silu(c)
        r = lax.rsqrt(jnp.sum(s * s, axis=-1, keepdims=True) + EPS)
        dy = jnp.where(j < HQ, dq_ref[...], jnp.where(j < NQK, dk_ref[...], dv_ref[...]))
        ds_norm = r * dy - s * (r * r * r) * jnp.sum(s * dy, axis=-1, keepdims=True)
        ds = jnp.where(j < NQK, ds_norm, dy)
        dc = ds * _dsilu(c)
        _conv_bwd_w(dc, xv, dw_ref, SHORT_CONV_K, first)
        dx_ref[...] = _conv_bwd_in(dc, w_ref, SHORT_CONV_K).astype(BF16)

    return pl.pallas_call(
        body, name=name, grid=(NB, Bl),
        in_specs=[pl.BlockSpec((S, LANES), lambda j, b: (b, off + j)),
                  pl.BlockSpec((SHORT_CONV_K, LANES), lambda j, b: (0, j)),
                  pl.BlockSpec((S, LANES), lambda j, b: (b, jnp.minimum(j, HQ - 1))),
                  pl.BlockSpec((S, LANES), lambda j, b: (b, jnp.clip(j - HQ, 0, HQ - 1))),
                  pl.BlockSpec((S, LANES), lambda j, b: (b, jnp.clip(j - NQK, 0, H - 1)))],
        out_specs=[pl.BlockSpec((S, LANES), lambda j, b: (b, j)),
                   pl.BlockSpec((SHORT_CONV_K, LANES), lambda j, b: (0, j))],
        out_shape=[jax.ShapeDtypeStruct((Bl * S, NB * LANES), BF16),
                   jax.ShapeDtypeStruct((SHORT_CONV_K, NB * LANES), F32)],
        compiler_params=_cparams(("parallel", "arbitrary")),
    )(p, conv_w, dq, dk, dv)


def _lane_pick(h):
    row = lax.broadcasted_iota(jnp.int32, (LANES, LANES), 0)
    return (row == h).astype(F32)


def _gdn_gate_fwd(pba, a_log, dt_bias, *, Bl, S, H, name):
    def body(alog_ref, dtb_ref, x_ref, g_ref, beta_ref):
        xv = x_ref[...]
        for h in range(H):
            b_raw = _dot(xv, _lane_pick(h), HIGHEST)
            a_raw = _dot(xv, _lane_pick(H + h), HIGHEST)
            beta_ref[h] = _sig(b_raw)
            ea = jnp.exp(jnp.zeros((1, LANES), F32) + alog_ref[h])
            g_ref[h] = -ea * _softplus(a_raw + dtb_ref[h])

    smem = pl.BlockSpec(memory_space=pltpu.SMEM)
    rep = pl.BlockSpec((H, S, LANES), lambda b: (0, b, 0))
    return pl.pallas_call(
        body, name=name, grid=(Bl,),
        in_specs=[smem, smem, pl.BlockSpec((S, LANES), lambda b: (b, 0))],
        out_specs=[rep, rep],
        out_shape=[jax.ShapeDtypeStruct((H, Bl * S, LANES), F32)] * 2,
        compiler_params=_cparams(("parallel",)),
    )(a_log, dt_bias, pba)


def _gdn_gate_bwd(pba, a_log, dt_bias, dg, dbeta, *, Bl, S, H, name):
    HP = 8 * ((H + 7) // 8)

    def body(alog_ref, dtb_ref, x_ref, dg_ref, dbeta_ref, dx_ref, dalog_ref, ddtb_ref):
        first = pl.program_id(0) == 0
        xv = x_ref[...]
        lane = lax.broadcasted_iota(jnp.int32, (S, LANES), 1)
        acc = jnp.zeros((S, LANES), F32)

        @pl.when(first)
        def _():
            dalog_ref[...] = jnp.zeros_like(dalog_ref)
            ddtb_ref[...] = jnp.zeros_like(ddtb_ref)

        for h in range(H):
            b_raw = _dot(xv, _lane_pick(h), HIGHEST)
            a_raw = _dot(xv, _lane_pick(H + h), HIGHEST)
            beta = _sig(b_raw)
            db_raw = dbeta_ref[h] * beta * (1.0 - beta)
            z = a_raw + dtb_ref[h]
            ea = jnp.exp(jnp.zeros((1, LANES), F32) + alog_ref[h])
            dgv = dg_ref[h]
            da_raw = dgv * (-ea) * _sig(z)
            g = -ea * _softplus(z)
            dalog_ref[h:h + 1, :] += jnp.sum(dgv * g, axis=0, keepdims=True)
            ddtb_ref[h:h + 1, :] += jnp.sum(da_raw, axis=0, keepdims=True)
            acc = acc + jnp.where(lane == h, db_raw, 0.0) + jnp.where(lane == H + h, da_raw, 0.0)
        dx_ref[...] = acc.astype(BF16)

    smem = pl.BlockSpec(memory_space=pltpu.SMEM)
    rep = pl.BlockSpec((H, S, LANES), lambda b: (0, b, 0))
    small = pl.BlockSpec((HP, LANES), lambda b: (0, 0))
    return pl.pallas_call(
        body, name=name, grid=(Bl,),
        in_specs=[smem, smem, pl.BlockSpec((S, LANES), lambda b: (b, 0)), rep, rep],
        out_specs=[pl.BlockSpec((S, LANES), lambda b: (b, 0)), small, small],
        out_shape=[jax.ShapeDtypeStruct((Bl * S, LANES), BF16),
                   jax.ShapeDtypeStruct((HP, LANES), F32), jax.ShapeDtypeStruct((HP, LANES), F32)],
        compiler_params=_cparams(("arbitrary",)),
    )(a_log, dt_bias, pba, dg, dbeta)


def _tri_masks():
    ri = lax.broadcasted_iota(jnp.int32, (CHUNK, CHUNK), 0)
    ci = lax.broadcasted_iota(jnp.int32, (CHUNK, CHUNK), 1)
    return ri >= ci, ri > ci, ri == CHUNK - 1


def _tri_inv(L):
    ri = lax.broadcasted_iota(jnp.int32, (CHUNK, CHUNK), 0)
    ci = lax.broadcasted_iota(jnp.int32, (CHUNK, CHUNK), 1)
    T = jnp.where(ri == ci, 1.0, 0.0) - jnp.where((ri >> 1) == (ci >> 1), L, 0.0)
    for lv in range(2, int(math.log2(CHUNK)) + 1):
        O = jnp.where(((ri >> lv) == (ci >> lv)) & ((ri >> (lv - 1)) != (ci >> (lv - 1))), L, 0.0)
        T = T - _dot(T, _dot(O, T, HIGHEST), HIGHEST)
    return T


def _chunk_local(q, k, v, beta, g):
    ge, gt, last = _tri_masks()
    gam = _dot(ge.astype(F32), g, HIGHEST)
    D = jnp.where(ge, jnp.exp(jnp.where(ge, gam - gam.T, 0.0)), 0.0)
    kb = k * beta
    vb = v * beta
    M = _dot_nt(_bf(kb), _bf(k))
    L = jnp.where(gt, M * D, 0.0)
    eg = jnp.exp(gam)
    kbg = kb * eg
    P = _dot_nt(_bf(q), _bf(k))
    QK = jnp.where(ge, P * D, 0.0)
    gl = jnp.sum(jnp.where(last, gam, 0.0), axis=0, keepdims=True)
    el = jnp.exp(gl - gam)
    return dict(ge=ge, gt=gt, last=last, gam=gam, D=D, kb=kb, vb=vb, L=L, eg=eg, kbg=kbg, QK=QK, gl=gl,
                el=el, kd=k * el, qg=q * eg)


def _chunk_fwd(q, k, v, beta, g, S):
    c = _chunk_local(q, k, v, beta, g)
    T = _tri_inv(c["L"])
    u = _dot(T, c["vb"], HIGHEST)
    w = _dot(T, c["kbg"], HIGHEST)
    Sb = _bf(S)
    vn = u - _dot(_bf(w), Sb)
    o = _dot(_bf(c["qg"]), Sb) + _dot(_bf(c["QK"]), _bf(vn))
    S2 = S * jnp.exp(c["gl"]) + _dot_tn(_bf(c["kd"]), _bf(vn))
    return o, S2, T


def _rowsum(x):
    return jnp.sum(x, axis=-1, keepdims=True)


def _dot_tn_hi(a, b):
    return lax.dot_general(a, b, (((0,), (0,)), ((), ())), preferred_element_type=F32, precision=HIGHEST)


def _chunk_bwd(q, k, v, beta, g, S, T, do, dS2):
    c = _chunk_local(q, k, v, beta, g)
    ge, gt, last = c["ge"], c["gt"], c["last"]
    u = _dot(T, c["vb"], HIGHEST)
    w = _dot(T, c["kbg"], HIGHEST)
    Sb = _bf(S)
    vn = u - _dot(_bf(w), Sb)
    dob, vnb, dS2b = _bf(do), _bf(vn), _bf(dS2)
    e_last = jnp.exp(c["gl"])
    dqg = _dot_nt(dob, Sb)
    dS = _dot_tn(_bf(c["qg"]), dob)
    dQK = jnp.where(ge, _dot_nt(dob, vnb), 0.0)
    dvn = _dot_tn(_bf(c["QK"]), dob)
    dS = dS + dS2 * e_last
    de_last = jnp.sum(jnp.sum(dS2 * S, axis=0, keepdims=True), axis=1, keepdims=True)
    dkd = _dot_nt(vnb, dS2b)
    dvn = dvn + _dot(_bf(c["kd"]), dS2b)
    dvnb = _bf(dvn)
    dw = -_dot_nt(dvnb, Sb)
    dS = dS - _dot_tn(_bf(w), dvnb)
    dvb = _dot_tn_hi(T, dvn)
    dkbg = _dot_tn_hi(T, dw)
    dA = -(_dot_nt(_bf(dvb), _bf(u)) + _dot_nt(_bf(dkbg), _bf(w)))
    dL = jnp.where(gt, dA, 0.0)
    dM = dL * c["D"]
    dP = dQK * c["D"]
    E = dL * c["L"] + dQK * c["QK"]
    kbf = _bf(k)
    dkb = _dot(_bf(dM), kbf) + dkbg * c["eg"]
    dk = _dot_tn(_bf(dM), _bf(c["kb"])) + _dot_tn(_bf(dP), _bf(q)) + dkd * c["el"] + dkb * beta
    dq = _dot(_bf(dP), kbf) + dqg * c["eg"]
    s_kd = _rowsum(dkd * c["kd"])
    dgam = (_rowsum(E) - _rowsum(E.T) + _rowsum(dqg * c["qg"]) - s_kd + _rowsum(dkbg * c["kbg"]))
    dgl = jnp.sum(s_kd, axis=0, keepdims=True) + de_last * e_last
    dgam_rep = jnp.broadcast_to(dgam, (CHUNK, LANES)) + jnp.where(last, jnp.broadcast_to(dgl, (CHUNK, LANES)), 0.0)
    dg_rep = lax.dot_general(ge.astype(F32), dgam_rep, (((0,), (0,)), ((), ())),
                             preferred_element_type=F32, precision=HIGHEST)
    dbeta = _rowsum(dkb * k) + _rowsum(dvb * v)
    dv = dvb * beta
    return dq, dk, dv, jnp.broadcast_to(dbeta, (CHUNK, LANES)), dg_rep, dS


def _gdn_core_fwd(qkv, g, beta, *, Bl, S, KW, VW, name):
    HQ = KW // LANES
    H = VW // LANES
    NC = S // CHUNK
    scale = float(LANES) ** -0.5

    def body(q_ref, k_ref, v_ref, g_ref, beta_ref, o_ref, st_ref, t_ref):
        def step(n, states):
            rows = pl.ds(pl.multiple_of(n * CHUNK, CHUNK), CHUNK)
            q = q_ref[rows, :] * scale
            k = k_ref[rows, :]
            out = []
            for e in range(2):
                st_ref[e, n] = states[e]
                o, S2, Tb = _chunk_fwd(q, k, v_ref[rows, e * LANES:(e + 1) * LANES], beta_ref[e, rows, :],
                                       g_ref[e, rows, :], states[e])
                o_ref[rows, e * LANES:(e + 1) * LANES] = o
                t_ref[e, rows, :] = Tb
                out.append(S2)
            return tuple(out)

        z = jnp.zeros((LANES, LANES), F32)
        lax.fori_loop(0, NC, step, (z, z))

    rep = pl.BlockSpec((2, S, LANES), lambda b, h: (h, b, 0))
    return pl.pallas_call(
        body, name=name, grid=(Bl, HQ),
        in_specs=[pl.BlockSpec((S, LANES), lambda b, h: (b, h)),
                  pl.BlockSpec((S, LANES), lambda b, h: (b, HQ + h)),
                  pl.BlockSpec((S, 2 * LANES), lambda b, h: (b, HQ + h)), rep, rep],
        out_specs=[pl.BlockSpec((S, 2 * LANES), lambda b, h: (b, h)),
                   pl.BlockSpec((None, 2, NC, LANES, LANES), lambda b, h: (b, h, 0, 0, 0)), rep],
        out_shape=[jax.ShapeDtypeStruct((Bl * S, VW), F32),
                   jax.ShapeDtypeStruct((Bl, H, NC, LANES, LANES), F32),
                   jax.ShapeDtypeStruct((H, Bl * S, LANES), F32)],
        compiler_params=_cparams(("parallel", "parallel")),
    )(qkv, qkv, qkv, g, beta)


def _gdn_core_bwd(qkv, g, beta, states, tinv, do, *, Bl, S, KW, VW, name):
    HQ = KW // LANES
    H = VW // LANES
    NC = S // CHUNK
    scale = float(LANES) ** -0.5

    def body(q_ref, k_ref, v_ref, g_ref, beta_ref, st_ref, t_ref, do_ref,
             dq_ref, dk_ref, dv_ref, dg_ref, dbeta_ref):
        def step(i, dstates):
            n = NC - 1 - i
            rows = pl.ds(pl.multiple_of(n * CHUNK, CHUNK), CHUNK)
            q = q_ref[rows, :] * scale
            k = k_ref[rows, :]
            out = []
            dq_sum = dk_sum = None
            for e in range(2):
                cols = slice(e * LANES, (e + 1) * LANES)
                dq, dk, dv, dbeta, dg, dS = _chunk_bwd(q, k, v_ref[rows, cols], beta_ref[e, rows, :],
                                                       g_ref[e, rows, :], st_ref[e, n], t_ref[e, rows, :],
                                                       do_ref[rows, cols], dstates[e])
                dv_ref[rows, cols] = dv
                dg_ref[e, rows, :] = dg
                dbeta_ref[e, rows, :] = dbeta
                dq_sum = dq if dq_sum is None else dq_sum + dq
                dk_sum = dk if dk_sum is None else dk_sum + dk
                out.append(dS)
            dq_ref[rows, :] = dq_sum * scale
            dk_ref[rows, :] = dk_sum
            return tuple(out)

        z = jnp.zeros((LANES, LANES), F32)
        lax.fori_loop(0, NC, step, (z, z))

    rep = pl.BlockSpec((2, S, LANES), lambda b, h: (h, b, 0))
    seq = pl.BlockSpec((S, LANES), lambda b, h: (b, h))
    seq2 = pl.BlockSpec((S, 2 * LANES), lambda b, h: (b, h))
    return pl.pallas_call(
        body, name=name, grid=(Bl, HQ),
        in_specs=[seq, pl.BlockSpec((S, LANES), lambda b, h: (b, HQ + h)),
                  pl.BlockSpec((S, 2 * LANES), lambda b, h: (b, HQ + h)), rep, rep,
                  pl.BlockSpec((None, 2, NC, LANES, LANES), lambda b, h: (b, h, 0, 0, 0)), rep, seq2],
        out_specs=[seq, seq, seq2, rep, rep],
        out_shape=[jax.ShapeDtypeStruct((Bl * S, KW), F32), jax.ShapeDtypeStruct((Bl * S, KW), F32),
                   jax.ShapeDtypeStruct((Bl * S, VW), F32),
                   jax.ShapeDtypeStruct((H, Bl * S, LANES), F32), jax.ShapeDtypeStruct((H, Bl * S, LANES), F32)],
        compiler_params=_cparams(("parallel", "parallel")),
    )(qkv, qkv, qkv, g, beta, states, tinv, do)


def _gdn_out_fwd(o, p, norm_g, out_a, *, CC, VW, name, tr=256):
    T = o.shape[0]
    tr = min(tr, T)
    H = VW // LANES
    zoff = p.shape[1] // VW - 1

    def body(o_ref, z_ref, ng_ref, a_ref, mix_ref):
        mix_ref[:, :CC] = a_ref[...]
        for h in range(H):
            cols = slice(h * LANES, (h + 1) * LANES)
            ov = o_ref[:, cols]
            r = lax.rsqrt(jnp.mean(ov * ov, axis=-1, keepdims=True) + EPS)
            mix_ref[:, CC + h * LANES:CC + (h + 1) * LANES] = (ov * r * ng_ref[...] * _silu(z_ref[:, cols])).astype(BF16)

    return pl.pallas_call(
        body, name=name, grid=(T // tr,),
        in_specs=[pl.BlockSpec((tr, VW), lambda i: (i, 0)), pl.BlockSpec((tr, VW), lambda i: (i, zoff)),
                  pl.BlockSpec((1, LANES), lambda i: (0, 0)), pl.BlockSpec((tr, CC), lambda i: (i, 0))],
        out_specs=pl.BlockSpec((tr, CC + VW), lambda i: (i, 0)),
        out_shape=jax.ShapeDtypeStruct((T, CC + VW), BF16),
        compiler_params=_cparams(("parallel",)),
    )(o, p, norm_g.reshape(1, LANES), out_a)


def _gdn_out_bwd(o, p, norm_g, dmix, *, CC, VW, name, tr=256):
    T = o.shape[0]
    tr = min(tr, T)
    H = VW // LANES
    zoff = p.shape[1] // VW - 1

    def body(o_ref, z_ref, ng_ref, dmix_ref, do_ref, dz_ref, da_ref, dng_ref, dpb_ref):
        first = pl.program_id(0) == 0
        da = dmix_ref[:, :CC]
        da_ref[...] = da.astype(BF16)
        _acc_row(dpb_ref, 0, jnp.sum(da, axis=0, keepdims=True), first)
        ng = ng_ref[...]
        dng = jnp.zeros((1, LANES), F32)
        for h in range(H):
            cols = slice(h * LANES, (h + 1) * LANES)
            ov = o_ref[:, cols]
            zv = z_ref[:, cols]
            dout = dmix_ref[:, CC + h * LANES:CC + (h + 1) * LANES]
            r = lax.rsqrt(jnp.mean(ov * ov, axis=-1, keepdims=True) + EPS)
            on = ov * r * ng
            don = dout * _silu(zv)
            dz_ref[:, cols] = (dout * on * _dsilu(zv)).astype(BF16)
            dng = dng + jnp.sum(don * ov * r, axis=0, keepdims=True)
            dong = don * ng
            do_ref[:, cols] = r * dong - ov * (r * r * r) * jnp.mean(dong * ov, axis=-1, keepdims=True)
        _acc_row(dng_ref, 0, dng, first)

    return pl.pallas_call(
        body, name=name, grid=(T // tr,),
        in_specs=[pl.BlockSpec((tr, VW), lambda i: (i, 0)), pl.BlockSpec((tr, VW), lambda i: (i, zoff)),
                  pl.BlockSpec((1, LANES), lambda i: (0, 0)), pl.BlockSpec((tr, CC + VW), lambda i: (i, 0))],
        out_specs=[pl.BlockSpec((tr, VW), lambda i: (i, 0)), pl.BlockSpec((tr, VW), lambda i: (i, 0)),
                   pl.BlockSpec((tr, CC), lambda i: (i, 0)), pl.BlockSpec((1, LANES), lambda i: (0, 0)),
                   pl.BlockSpec((1, CC), lambda i: (0, 0))],
        out_shape=[jax.ShapeDtypeStruct((T, VW), F32), jax.ShapeDtypeStruct((T, VW), BF16),
                   jax.ShapeDtypeStruct((T, CC), BF16), jax.ShapeDtypeStruct((1, LANES), F32),
                   jax.ShapeDtypeStruct((1, CC), F32)],
        compiler_params=_cparams(("arbitrary",)),
    )(o, p, norm_g.reshape(1, LANES), dmix)


FFN_CW = 256


def _ffn_act_fwd(gu, conv_w, conv_b, *, Bl, S, name):
    FF = gu.shape[2]
    cw = min(FFN_CW, FF)

    def body(g_ref, u_ref, w_ref, b_ref, a_ref):
        gc = _conv_fwd(g_ref[...], w_ref, FFN_CONV_K) + b_ref[...]
        a_ref[...] = (_silu(gc) * u_ref[...]).astype(BF16)

    return pl.pallas_call(
        body, name=name, grid=(Bl, FF // cw),
        in_specs=[pl.BlockSpec((None, S, cw), lambda b, j: (0, b, j)),
                  pl.BlockSpec((None, S, cw), lambda b, j: (1, b, j)),
                  pl.BlockSpec((FFN_CONV_K, cw), lambda b, j: (0, j)),
                  pl.BlockSpec((1, cw), lambda b, j: (0, j))],
        out_specs=pl.BlockSpec((S, cw), lambda b, j: (b, j)),
        out_shape=jax.ShapeDtypeStruct((Bl * S, FF), BF16),
        compiler_params=_cparams(("parallel", "parallel")),
    )(gu, gu, conv_w, conv_b.reshape(1, FF))


def _ffn_act_bwd(gu, conv_w, conv_b, da, *, Bl, S, name):
    FF = gu.shape[2]
    cw = min(FFN_CW, FF)

    def body(g_ref, u_ref, w_ref, b_ref, da_ref, dgu_ref, dw_ref, db_ref):
        first = pl.program_id(1) == 0
        gate = g_ref[...]
        gc = _conv_fwd(gate, w_ref, FFN_CONV_K) + b_ref[...]
        dav = da_ref[...]
        dgu_ref[1] = (dav * _silu(gc)).astype(BF16)
        dgc = dav * u_ref[...] * _dsilu(gc)
        _acc_row(db_ref, 0, jnp.sum(dgc, axis=0, keepdims=True), first)
        _conv_bwd_w(dgc, gate, dw_ref, FFN_CONV_K, first)
        dgu_ref[0] = _conv_bwd_in(dgc, w_ref, FFN_CONV_K).astype(BF16)

    return pl.pallas_call(
        body, name=name, grid=(FF // cw, Bl),
        in_specs=[pl.BlockSpec((None, S, cw), lambda j, b: (0, b, j)),
                  pl.BlockSpec((None, S, cw), lambda j, b: (1, b, j)),
                  pl.BlockSpec((FFN_CONV_K, cw), lambda j, b: (0, j)),
                  pl.BlockSpec((1, cw), lambda j, b: (0, j)),
                  pl.BlockSpec((S, cw), lambda j, b: (b, j))],
        out_specs=[pl.BlockSpec((2, S, cw), lambda j, b: (0, b, j)),
                   pl.BlockSpec((FFN_CONV_K, cw), lambda j, b: (0, j)),
                   pl.BlockSpec((1, cw), lambda j, b: (0, j))],
        out_shape=[jax.ShapeDtypeStruct((2, Bl * S, FF), BF16),
                   jax.ShapeDtypeStruct((FFN_CONV_K, FF), F32), jax.ShapeDtypeStruct((1, FF), F32)],
        compiler_params=_cparams(("parallel", "arbitrary")),
    )(gu, gu, conv_w, conv_b.reshape(1, FF), da)


def _layer_dims(W):
    CC = W["conv_pw_b"].shape[0]
    VW = W["mix_norm_g"].shape[0] - CC
    KW = (W["gdn_conv_w"].shape[1] - VW) // 2
    return CC, KW, VW


def _layer_fwd(l, x, W, Bl, S, fetch):
    CC, KW, VW = _layer_dims(W)
    H = VW // LANES
    w_in_main, w_in_ba = fetch(l, "w_in", x)
    h1 = _rms_fwd(x, W["mix_norm_g"], name="rms1_fwd")
    p = _mm(h1, w_in_main, name="mm_in")
    pba = _mm(h1, w_in_ba, name="mm_in_ba")
    u3 = _conf_fwd(p, W["conv_dw_w"], W["conv_dw_b"], W["conv_ln_g"], W["conv_ln_b"], Bl=Bl, S=S, CC=CC,
                   name="conf_fwd")
    conv_pw_w = fetch(l, "conv_pw_w", u3)
    out_a = _mm(u3, conv_pw_w, bias=W["conv_pw_b"], out_dtype=BF16, name="mm_pw")
    qkv = _gdn_pre_fwd(p, W["gdn_conv_w"], Bl=Bl, S=S, CC=CC, KW=KW, VW=VW, name="gdn_pre_fwd")
    g, beta = _gdn_gate_fwd(pba, W["gdn_a_log"], W["gdn_dt_bias"], Bl=Bl, S=S, H=H, name="gdn_gate_fwd")
    o, states, tinv = _gdn_core_fwd(qkv, g, beta, Bl=Bl, S=S, KW=KW, VW=VW, name="gdn_core_fwd")
    mix = _gdn_out_fwd(o, p, W["gdn_norm_g"], out_a, CC=CC, VW=VW, name="gdn_out_fwd")
    w_out = fetch(l, "w_out", mix)
    x1 = _mm(mix, w_out, res=x, name="mm_out")
    h2 = _rms_fwd(x1, W["ffn_norm_g"], name="rms2_fwd")
    w_up = fetch(l, "w_up", h2)
    gu = _mm(h2, w_up, out_blocks=2, tn=w_up.shape[2], name="mm_up")
    a = _ffn_act_fwd(gu, W["ffn_conv_w"], W["ffn_conv_b"], Bl=Bl, S=S, name="ffn_act_fwd")
    w_down = fetch(l, "w_down", a)
    x2 = _mm(a, w_down, res=x1, name="mm_down")
    saved = dict(x=x, h1=h1, p=p, pba=pba, u3=u3, qkv=qkv, g=g, beta=beta, o=o, states=states, tinv=tinv,
                 mix=mix, x1=x1, h2=h2, gu=gu, a=a, w_in_main=w_in_main, w_in_ba=w_in_ba, conv_pw_w=conv_pw_w,
                 w_out=w_out, w_up=w_up, w_down=w_down)
    return x2, saved


def _layer_bwd(l, dx2, W, A, Bl, S, sink, tok):
    CC, KW, VW = _layer_dims(W)
    H = VW // LANES
    G = {}
    da = _mm(dx2, A["w_down"], tb=True, dep=tok, name="mm_down_dx")
    tok = sink(l, "w_down", _mm(A["a"], dx2, ta=True, out_dtype=BF16, name="mm_down_dw"))
    dgu, G["ffn_conv_w"], G["ffn_conv_b"] = _ffn_act_bwd(A["gu"], W["ffn_conv_w"], W["ffn_conv_b"], da,
                                                         Bl=Bl, S=S, name="ffn_act_bwd")
    upw = A["w_up"].shape[2]
    dh2 = _mm(dgu, A["w_up"], tb=True, tk=upw, dep=tok, name="mm_up_dx")
    tok = sink(l, "w_up", _mm(A["h2"], dgu, ta=True, out_dtype=BF16, out_blocks=N_DEV, tn=upw, name="mm_up_dw"))
    dx1, G["ffn_norm_g"] = _rms_bwd(A["x1"], W["ffn_norm_g"], dh2, dx2, name="rms2_bwd")
    dmix = _mm(dx1, A["w_out"], tb=True, dep=tok, name="mm_out_dx")
    tok = sink(l, "w_out", _mm(A["mix"], dx1, ta=True, out_dtype=BF16, name="mm_out_dw"))
    do, dz, dout_a, G["gdn_norm_g"], G["conv_pw_b"] = _gdn_out_bwd(A["o"], A["p"], W["gdn_norm_g"], dmix,
                                                                   CC=CC, VW=VW, name="gdn_out_bwd")
    dq, dk, dv, dg, dbeta = _gdn_core_bwd(A["qkv"], A["g"], A["beta"], A["states"], A["tinv"], do,
                                          Bl=Bl, S=S, KW=KW, VW=VW, name="gdn_core_bwd")
    dpba, dalog, ddtb = _gdn_gate_bwd(A["pba"], W["gdn_a_log"], W["gdn_dt_bias"], dg, dbeta, Bl=Bl, S=S, H=H,
                                      name="gdn_gate_bwd")
    G["gdn_a_log"], G["gdn_dt_bias"] = dalog[:H, 0], ddtb[:H, 0]
    dqkv, G["gdn_conv_w"] = _gdn_pre_bwd(A["p"], W["gdn_conv_w"], dq, dk, dv, Bl=Bl, S=S, CC=CC, KW=KW, VW=VW,
                                         name="gdn_pre_bwd")
    du3 = _mm(dout_a, A["conv_pw_w"], tb=True, dep=tok, name="mm_pw_dx")
    tok = sink(l, "conv_pw_w", _mm(A["u3"], dout_a, ta=True, out_dtype=BF16, name="mm_pw_dw"))
    dav, dag, G["conv_dw_w"], G["conv_dw_b"], G["conv_ln_g"], G["conv_ln_b"] = _conf_bwd(
        A["p"], W["conv_dw_w"], W["conv_dw_b"], W["conv_ln_g"], W["conv_ln_b"], du3, Bl=Bl, S=S, CC=CC,
        name="conf_bwd")
    dp = jnp.concatenate([dav, dag, dqkv, dz], axis=1)
    dh1 = _mm(dpba, A["w_in_ba"], tb=True, dep=tok, name="mm_in_ba_dx")
    dh1 = _mm(dp, A["w_in_main"], tb=True, res=dh1, name="mm_in_dx")
    tok = sink(l, "w_in", (_mm(A["h1"], dp, ta=True, out_dtype=BF16, name="mm_in_dw"),
                           _mm(A["h1"], dpba, ta=True, out_dtype=BF16, name="mm_in_ba_dw")))
    dx, G["mix_norm_g"] = _rms_bwd(A["x"], W["mix_norm_g"], dh1, dx1, name="rms1_bwd")
    return dx, G, tok


def _local_step(x, target, Ws, final_norm_g, fetch, sink):
    Bl, S, D = x.shape
    xt = x.reshape(Bl * S, D)
    acts = []
    for l, W in enumerate(Ws):
        xt, A = _layer_fwd(l, xt, W, Bl, S, fetch)
        acts.append(A)
    loss, dx, dgf = _loss_head(xt, final_norm_g, target.reshape(Bl * S, D), name="loss_head")
    grads = [None] * len(Ws)
    tok = None
    for l in reversed(range(len(Ws))):
        dx, grads[l], tok = _layer_bwd(l, dx, Ws[l], acts[l], Bl, S, sink, tok)
    return loss[0, 0], dx.reshape(Bl, S, D), grads, dgf.reshape(D)


def _mesh_pos():
    return lax.axis_index("x"), lax.axis_index("y"), lax.axis_index("c")


def _dev_index(px, py, pc):
    return 4 * px + 2 * py + pc


_ANY = pl.BlockSpec(memory_space=pl.ANY)


def _all_gather(arrs, *, name):
    n = len(arrs)

    def body(*refs):
        ins, outs = refs[:n], refs[n:2 * n]
        send_sems, recv_sems, local_sems = refs[2 * n:]
        x, y, c = _mesh_pos()
        me, sibling = (x, y, c), (x, y, 1 - c)
        chips = [(1 - x, y), (x, 1 - y), (1 - x, 1 - y)]

        def copy(a, k, block, to, src=None):
            dst = outs[a].at[_dev_index(*block)]
            return pltpu.make_async_remote_copy(
                src_ref=dst if src is None else src, dst_ref=dst,
                send_sem=send_sems.at[a, k], recv_sem=recv_sems.at[a, k],
                device_id=to, device_id_type=MESH)

        mine = [pltpu.make_async_copy(ins[a], outs[a].at[_dev_index(*me)], local_sems.at[a]) for a in range(n)]
        for cp in mine:
            cp.start()
        first = []
        for a in range(n):
            first.append(copy(a, 0, me, sibling, src=ins[a]))
            first += [copy(a, 1 + j, me, (*chip, c), src=ins[a]) for j, chip in enumerate(chips)]
        for cp in first:
            cp.start()
        passed = []
        for a in range(n):
            for j, chip in enumerate(chips):
                copy(a, 1 + j, (*chip, c), me).wait_recv()
                fwd = copy(a, 4 + j, (*chip, c), sibling)
                fwd.start()
                passed.append(fwd)
        for a in range(n):
            copy(a, 0, sibling, me).wait_recv()
            for j, chip in enumerate(chips):
                copy(a, 4 + j, (*chip, 1 - c), me).wait_recv()
        for cp in first + passed:
            cp.wait_send()
        for cp in mine:
            cp.wait()

    return pl.pallas_call(
        body, name=name,
        in_specs=[_ANY] * n, out_specs=[_ANY] * n,
        out_shape=[jax.ShapeDtypeStruct((N_DEV,) + a.shape, a.dtype) for a in arrs],
        scratch_shapes=[pltpu.SemaphoreType.DMA((n, N_DEV - 1)), pltpu.SemaphoreType.DMA((n, N_DEV - 1)),
                        pltpu.SemaphoreType.DMA((n,))],
    )(*arrs)


_HBM = pl.BlockSpec(memory_space=pltpu.HBM)
_SEM = pl.BlockSpec(memory_space=pltpu.SEMAPHORE)
_DATAFLOW = pltpu.SideEffectType.DATAFLOW_SIDE_EFFECTING


def _peers(x, y, c):
    flip = lambda v, f: 1 - v if f else v
    return [(flip(x, p & 4), flip(y, p & 2), flip(c, p & 1)) for p in range(1, N_DEV)]


def _exchange_start(srcs, *, scatter, name):
    n = len(srcs)
    zones = [lax.empty(s.shape if scatter else (N_DEV,) + s.shape, s.dtype) for s in srcs]

    def body(*refs):
        src, zone = refs[:n], refs[n:2 * n]
        send_sems, recv_sems = refs[2 * n:3 * n], refs[3 * n:4 * n]
        token, local_sems = refs[6 * n], refs[6 * n + 1]
        x, y, c = _mesh_pos()
        me = _dev_index(x, y, c)
        own = [pltpu.make_async_copy(src[a].at[me] if scatter else src[a], zone[a].at[me], local_sems.at[a])
               for a in range(n)]
        for cp in own:
            cp.start()
        for a in range(n):
            for k, peer in enumerate(_peers(x, y, c)):
                pltpu.make_async_remote_copy(
                    src_ref=src[a].at[_dev_index(*peer)] if scatter else src[a], dst_ref=zone[a].at[me],
                    send_sem=send_sems[a].at[k], recv_sem=recv_sems[a].at[k],
                    device_id=peer, device_id_type=MESH).start()
        for cp in own:
            cp.wait()
        token[...] = jnp.zeros_like(token)

    sems = [pltpu.SemaphoreType.DMA((N_DEV - 1,))] * (2 * n)
    out = pl.pallas_call(
        body, name=name,
        in_specs=[_HBM] * (2 * n),
        out_specs=[_SEM] * (2 * n) + [_HBM] * (2 * n) + [pl.BlockSpec(memory_space=pltpu.VMEM)],
        out_shape=sems + [pltpu.HBM(s.shape, s.dtype) for s in srcs] + [pltpu.HBM(z.shape, z.dtype) for z in zones]
        + [jax.ShapeDtypeStruct((8, LANES), F32)],
        input_output_aliases={i: 2 * n + i for i in range(2 * n)},
        scratch_shapes=[pltpu.SemaphoreType.DMA((n,))],
        compiler_params=pltpu.CompilerParams(has_side_effects=_DATAFLOW),
    )(*[pltpu.with_memory_space_constraint(t, pltpu.HBM) for t in list(srcs) + zones])
    handles = [(out[2 * n + a], out[3 * n + a], out[a], out[n + a]) for a in range(n)]
    return handles, out[4 * n]


def _exchange_wait(handle, after, *, scatter, name):
    src, zone, send_sems, recv_sems = handle

    def body(src_ref, zone_ref, ssem, rsem, after_ref, src_out, zone_out):
        x, y, c = _mesh_pos()
        me = _dev_index(x, y, c)
        for k, peer in enumerate(_peers(x, y, c)):
            cp = pltpu.make_async_remote_copy(
                src_ref=src_ref.at[me] if scatter else src_ref, dst_ref=zone_ref.at[_dev_index(*peer)],
                send_sem=ssem.at[k], recv_sem=rsem.at[k], device_id=peer, device_id_type=MESH)
            cp.wait_send()
            cp.wait_recv()

    return pl.pallas_call(
        body, name=name,
        in_specs=[_HBM, _HBM, _SEM, _SEM, _ANY], out_specs=[_HBM, _HBM],
        out_shape=[pltpu.HBM(src.shape, src.dtype), pltpu.HBM(zone.shape, zone.dtype)],
        input_output_aliases={0: 0, 1: 1},
        compiler_params=pltpu.CompilerParams(has_side_effects=_DATAFLOW),
    )(src, zone, send_sems, recv_sems, after)[1]


def _adamw_math(w, g, m, v):
    m2 = ADAM_B1 * m + (1.0 - ADAM_B1) * g
    v2 = ADAM_B2 * v + (1.0 - ADAM_B2) * (g * g)
    m_hat = m2 / (1.0 - ADAM_B1 ** ADAM_STEP)
    v_hat = v2 / (1.0 - ADAM_B2 ** ADAM_STEP)
    delta = -ADAM_LR * (m_hat / (jnp.sqrt(v_hat) + ADAM_EPS) + ADAM_WD * w)
    return delta, m2, v2


def _adamw_big(l, w, m, v, recv, *, name, tr=128):
    _, R, C = w.shape
    tr = next(t for t in range(min(tr, R), 0, -16) if R % t == 0)

    def body(w_ref, m_ref, v_ref, r_ref, g_ref, d_ref, m2_ref, v2_ref):
        g = r_ref[0].astype(F32)
        for s in range(1, N_DEV):
            g = g + r_ref[s].astype(F32)
        g_ref[...] = g
        d_ref[...], m2_ref[...], v2_ref[...] = _adamw_math(w_ref[...], g, m_ref[...], v_ref[...])

    wspec = pl.BlockSpec((None, tr, C), lambda i: (l, i, 0))
    ospec = pl.BlockSpec((tr, C), lambda i: (i, 0))
    return pl.pallas_call(
        body, name=name, grid=(R // tr,),
        in_specs=[wspec, wspec, wspec, pl.BlockSpec((N_DEV, tr, C), lambda i: (0, i, 0))],
        out_specs=[ospec] * 4,
        out_shape=[jax.ShapeDtypeStruct((R, C), F32)] * 4,
        compiler_params=_cparams(("parallel",)),
    )(w, m, v, recv.reshape(N_DEV, R, C))


def _sum_slots(gathered, *, name):
    _, R, C = gathered.shape

    def body(r_ref, o_ref):
        g = r_ref[0]
        for s in range(1, N_DEV):
            g = g + r_ref[s]
        o_ref[...] = g

    return pl.pallas_call(body, name=name, out_shape=jax.ShapeDtypeStruct((R, C), F32))(gathered)


def _adamw_small(w, g, m, v, *, name):
    def body(w_ref, g_ref, m_ref, v_ref, d_ref, m2_ref, v2_ref):
        d_ref[...], m2_ref[...], v2_ref[...] = _adamw_math(w_ref[...], g_ref[...], m_ref[...], v_ref[...])

    return pl.pallas_call(body, name=name, out_shape=[jax.ShapeDtypeStruct(w.shape, F32)] * 3)(w, g, m, v)


def _pack(arrs):
    flat = []
    for a in arrs:
        a = a.reshape(-1).astype(F32)
        flat.append(jnp.pad(a, (0, (-a.shape[0]) % LANES)))
    out = jnp.concatenate(flat)
    out = jnp.pad(out, (0, (-out.shape[0]) % (8 * LANES)))
    return out.reshape(-1, LANES)


def _unpack(packed, shapes):
    flat = packed.reshape(-1)
    out, pos = [], 0
    for s in shapes:
        size = math.prod(s)
        out.append(flat[pos:pos + size].reshape(s))
        pos += size + (-size) % LANES
    return out


BIG = ("w_in", "conv_pw_w", "w_out", "w_up", "w_down")
SMALL_SHARDED = ("conv_dw_w", "gdn_conv_w", "ffn_conv_w")
SMALL_REPLICATED = ("mix_norm_g", "conv_dw_b", "conv_ln_g", "conv_ln_b", "conv_pw_b", "gdn_a_log", "gdn_dt_bias",
                    "gdn_norm_g", "ffn_norm_g", "ffn_conv_b")
WEIGHTS = ("mix_norm_g", "w_in", "conv_dw_w", "conv_dw_b", "conv_ln_g", "conv_ln_b", "conv_pw_w", "conv_pw_b",
           "gdn_conv_w", "gdn_a_log", "gdn_dt_bias", "gdn_norm_g", "w_out", "ffn_norm_g", "w_up", "ffn_conv_w",
           "ffn_conv_b", "w_down", "final_norm_g")


def _train_step(x, target, w, m, v):
    L = w["w_in"].shape[0]
    D = x.shape[-1]
    xi, yi, ci = _mesh_pos()
    me = _dev_index(xi, yi, ci)

    keys = [(n, l) for l in range(L) for n in BIG]
    send = [w[n][l].astype(BF16) for n, l in keys] + [w[n] for n in SMALL_SHARDED]
    handles, _ = _exchange_start(send, scatter=False, name="gather_weights_start")
    pending = dict(zip(keys, handles))
    small_full = {}
    for n, handle in zip(SMALL_SHARDED, handles[len(keys):]):
        g_ = _exchange_wait(handle, x, scatter=False, name=f"gather_wait_{n}")
        small_full[n] = jnp.moveaxis(g_, 0, 2).reshape(L, g_.shape[2], N_DEV * g_.shape[3])
    Ws = []
    for l in range(L):
        W = {n: w[n][l] for n in SMALL_REPLICATED}
        W.update({n: small_full[n][l] for n in SMALL_SHARDED})
        Ws.append(W)

    def fetch(l, n, after):
        g_ = _exchange_wait(pending[n, l], after, scatter=False, name=f"gather_wait_{n}_{l}")
        if n == "w_up":
            return g_
        if n == "w_in":
            w_in = jnp.moveaxis(g_, 0, 1).reshape(D, -1)
            n_main = (w_in.shape[1] // LANES) * LANES
            return w_in[:, :n_main], jnp.pad(w_in[:, n_main:], ((0, 0), (0, LANES - (w_in.shape[1] - n_main))))
        return g_.reshape(g_.shape[0] * g_.shape[1], g_.shape[2])

    started = []

    def sink(l, n, g_):
        if n == "w_in":
            g_main, g_ba = g_
            g_ = jnp.concatenate([g_main, g_ba[:, :w["w_in"].shape[2] * N_DEV - g_main.shape[1]]], axis=1)
            part = jnp.moveaxis(g_.reshape(D, N_DEV, -1), 1, 0)
        elif n == "w_up":
            part = g_
        else:
            part = g_.reshape(N_DEV, -1, g_.shape[1])
        (handle,), tok = _exchange_start([part], scatter=True, name=f"scatter_start_{n}_{l}")
        started.append((n, l, handle))
        return tok

    loss, grad_x, G, d_final = _local_step(x, target, Ws, w["final_norm_g"], fetch, sink)
    loss = lax.psum(loss, ("x", "y", "c"))

    out = {k: {} for k in ("grad", "delta", "new_m", "new_v")}
    res = {}
    for n, l, handle in started:
        recv = _exchange_wait(handle, grad_x, scatter=True, name=f"scatter_wait_{n}_{l}")
        res[n, l] = _adamw_big(l, w[n], m[n], v[n], recv, name=f"adamw_{n}")
    for n in BIG:
        for j, k in enumerate(("grad", "delta", "new_m", "new_v")):
            out[k][n] = jnp.stack([res[n, l][j] for l in range(L)])

    small_names = [n for n in WEIGHTS if n not in BIG]
    partial = []
    for n in small_names:
        if n == "final_norm_g":
            partial.append(d_final)
        else:
            partial.append(jnp.stack([G[l][n].reshape(Ws[l][n].shape) for l in range(L)]))
    shapes = [p_.shape for p_ in partial]
    gathered = _all_gather([_pack(partial)], name="all_gather_small_grads")[0]
    full = dict(zip(small_names, _unpack(_sum_slots(gathered, name="sum_small_grads"), shapes)))
    for n in SMALL_SHARDED:
        width = w[n].shape[-1]
        full[n] = lax.dynamic_slice_in_dim(full[n], me * width, width, axis=2)
    loc_shapes = [w[n].shape for n in small_names]
    g_pack = _pack([full[n] for n in small_names])
    res = _adamw_small(_pack([w[n] for n in small_names]), g_pack, _pack([m[n] for n in small_names]),
                       _pack([v[n] for n in small_names]), name="adamw_small")
    for k, packed in zip(("grad", "delta", "new_m", "new_v"), (g_pack,) + tuple(res)):
        out[k].update(dict(zip(small_names, _unpack(packed, loc_shapes))))
    return loss, grad_x, out


def kernel(x, mix_norm_g, w_in, conv_dw_w, conv_dw_b, conv_ln_g, conv_ln_b, conv_pw_w, conv_pw_b, gdn_conv_w, gdn_a_log, gdn_dt_bias, gdn_norm_g, w_out, ffn_norm_g, w_up, ffn_conv_w, ffn_conv_b, w_down, final_norm_g, loss_target, m_mix_norm_g, m_w_in, m_conv_dw_w, m_conv_dw_b, m_conv_ln_g, m_conv_ln_b, m_conv_pw_w, m_conv_pw_b, m_gdn_conv_w, m_gdn_a_log, m_gdn_dt_bias, m_gdn_norm_g, m_w_out, m_ffn_norm_g, m_w_up, m_ffn_conv_w, m_ffn_conv_b, m_w_down, m_final_norm_g, v_mix_norm_g, v_w_in, v_conv_dw_w, v_conv_dw_b, v_conv_ln_g, v_conv_ln_b, v_conv_pw_w, v_conv_pw_b, v_gdn_conv_w, v_gdn_a_log, v_gdn_dt_bias, v_gdn_norm_g, v_w_out, v_ffn_norm_g, v_w_up, v_ffn_conv_w, v_ffn_conv_b, v_w_down, v_final_norm_g):
    w = dict(zip(WEIGHTS, (mix_norm_g, w_in, conv_dw_w, conv_dw_b, conv_ln_g, conv_ln_b, conv_pw_w, conv_pw_b, gdn_conv_w,
                           gdn_a_log, gdn_dt_bias, gdn_norm_g, w_out, ffn_norm_g, w_up, ffn_conv_w, ffn_conv_b, w_down,
                           final_norm_g)))
    m = dict(zip(WEIGHTS, (m_mix_norm_g, m_w_in, m_conv_dw_w, m_conv_dw_b, m_conv_ln_g, m_conv_ln_b, m_conv_pw_w,
                           m_conv_pw_b, m_gdn_conv_w, m_gdn_a_log, m_gdn_dt_bias, m_gdn_norm_g, m_w_out, m_ffn_norm_g,
                           m_w_up, m_ffn_conv_w, m_ffn_conv_b, m_w_down, m_final_norm_g)))
    v = dict(zip(WEIGHTS, (v_mix_norm_g, v_w_in, v_conv_dw_w, v_conv_dw_b, v_conv_ln_g, v_conv_ln_b, v_conv_pw_w,
                           v_conv_pw_b, v_gdn_conv_w, v_gdn_a_log, v_gdn_dt_bias, v_gdn_norm_g, v_w_out, v_ffn_norm_g,
                           v_w_up, v_ffn_conv_w, v_ffn_conv_b, v_w_down, v_final_norm_g)))
    loss, grad_x, out = _train_step(x, loss_target, w, m, v)
    return (loss, grad_x, *[out["grad"][n] for n in WEIGHTS], *[out["delta"][n] for n in WEIGHTS],
            *[out["new_m"][n] for n in WEIGHTS], *[out["new_v"][n] for n in WEIGHTS])
```

```python
import functools
import math

import jax
import jax.numpy as jnp
from jax import lax
from jax.experimental import pallas as pl
from jax.experimental.pallas import tpu as pltpu
from jax.experimental.pallas import tpu_sc as plsc

F32 = jnp.float32
BF16 = jnp.bfloat16
HIGHEST = lax.Precision.HIGHEST
MESH = pl.DeviceIdType.MESH

EPS = 1e-6
LANES = 128
CHUNK = 128
CONV_K = 31
SHORT_CONV_K = 4
FFN_CONV_K = 3
N_DEV = 8
VMEM_LIMIT = 56 * 1024 * 1024

ADAM_LR = 0.001
ADAM_B1 = 0.9
ADAM_B2 = 0.999
ADAM_EPS = 1e-08
ADAM_WD = 0.01
ADAM_STEP = 10


def _cparams(sem):
    return pltpu.CompilerParams(dimension_semantics=sem, vmem_limit_bytes=VMEM_LIMIT)


def _sig(x):
    return 1.0 / (1.0 + jnp.exp(-x))


def _silu(x):
    return x * _sig(x)


def _dsilu(x):
    s = _sig(x)
    return s * (1.0 + x * (1.0 - s))


def _softplus(x):
    return jnp.maximum(x, 0.0) + jnp.log1p(jnp.exp(-jnp.abs(x)))


def _dot(a, b, precision=None):
    return jnp.dot(a, b, preferred_element_type=F32, precision=precision)


def _dot_nt(a, b):
    return lax.dot_general(a, b, (((1,), (1,)), ((), ())), preferred_element_type=F32)


def _dot_tn(a, b):
    return lax.dot_general(a, b, (((0,), (0,)), ((), ())), preferred_element_type=F32)


def _bf(x):
    return x.astype(BF16)


def _shift_down(u, s):
    if s == 0:
        return u
    row = lax.broadcasted_iota(jnp.int32, u.shape, 0)
    return jnp.where(row >= s, pltpu.roll(u, s, 0), 0.0)


def _shift_up(u, s):
    if s == 0:
        return u
    n = u.shape[0]
    row = lax.broadcasted_iota(jnp.int32, u.shape, 0)
    return jnp.where(row < n - s, pltpu.roll(u, n - s, 0), 0.0)


def _conv_fwd(u, w_ref, K):
    acc = None
    for k in range(K):
        term = w_ref[k:k + 1, :] * _shift_down(u, K - 1 - k)
        acc = term if acc is None else acc + term
    return acc


def _conv_bwd_in(do, w_ref, K):
    acc = None
    for k in range(K):
        term = w_ref[k:k + 1, :] * _shift_up(do, K - 1 - k)
        acc = term if acc is None else acc + term
    return acc


def _conv_bwd_w(do, u, dw_ref, K, first):
    for k in range(K):
        row = jnp.sum(do * _shift_down(u, K - 1 - k), axis=0, keepdims=True)
        _acc_row(dw_ref, k, row, first)


def _acc_row(ref, k, row, first):
    @pl.when(first)
    def _():
        ref[k:k + 1, :] = row

    @pl.when(jnp.logical_not(first))
    def _():
        ref[k:k + 1, :] += row


def _logical(arr):
    if arr.ndim == 2:
        return arr.shape
    return (arr.shape[1], arr.shape[0] * arr.shape[2])


def _tile(dim, pref, *col_widths):
    if dim % LANES:
        assert not col_widths
        return dim
    t = (min(pref, dim) // LANES) * LANES
    while t > LANES and (dim % t or any(c % t for c in col_widths)):
        t -= LANES
    assert dim % t == 0 and all(c % t == 0 for c in col_widths), (dim, pref, col_widths)
    return t


def _spec(shape, rt, ct, rfn, cfn):
    if len(shape) == 2:
        return pl.BlockSpec((rt, ct), lambda i, j, k: (rfn(i, j, k), cfn(i, j, k)))
    per = shape[2] // ct
    return pl.BlockSpec((None, rt, ct),
                        lambda i, j, k: (cfn(i, j, k) // per, rfn(i, j, k), cfn(i, j, k) % per))


def _mm(a, b, *, name, ta=False, tb=False, out_dtype=F32, out_blocks=None, bias=None, res=None,
        tm=1024, tn=1024, tk=1024):
    ra, ca = _logical(a)
    rb, cb = _logical(b)
    M, K = (ca, ra) if ta else (ra, ca)
    N, K2 = (rb, cb) if tb else (cb, rb)
    assert K == K2, (a.shape, b.shape, ta, tb)
    out_shape = (M, N) if out_blocks is None else (out_blocks, M, N // out_blocks)
    cw = lambda arr: [arr.shape[2]] if arr.ndim == 3 else []
    m_c = cw(a) if ta else []
    k_c = (cw(a) if not ta else []) + (cw(b) if tb else [])
    n_c = (cw(b) if not tb else []) + ([out_shape[2]] if out_blocks else []) + (cw(res) if res is not None else [])
    tm, tn, tk = _tile(M, tm, *m_c), _tile(N, tn, *n_c), _tile(K, tk, *k_c)
    nk = K // tk
    im, jn, kk = (lambda i, j, k: i), (lambda i, j, k: j), (lambda i, j, k: k)
    in_specs = [
        _spec(a.shape, tk, tm, kk, im) if ta else _spec(a.shape, tm, tk, im, kk),
        _spec(b.shape, tn, tk, jn, kk) if tb else _spec(b.shape, tk, tn, kk, jn),
    ]
    args = [a, b]
    if bias is not None:
        in_specs.append(pl.BlockSpec((1, tn), lambda i, j, k: (0, j)))
        args.append(bias.reshape(1, N).astype(F32))
    if res is not None:
        in_specs.append(_spec(res.shape, tm, tn, im, jn))
        args.append(res)
    dn = (((0 if ta else 1,), (1 if tb else 0,)), ((), ()))

    def body(*refs):
        a_ref, b_ref = refs[0], refs[1]
        pos = 2
        bias_ref = res_ref = None
        if bias is not None:
            bias_ref = refs[pos]
            pos += 1
        if res is not None:
            res_ref = refs[pos]
            pos += 1
        o_ref, acc_ref = refs[pos], refs[pos + 1]
        k = pl.program_id(2)
        part = lax.dot_general(_bf(a_ref[...]), _bf(b_ref[...]), dn, preferred_element_type=F32)

        @pl.when(k == 0)
        def _():
            acc_ref[...] = part

        @pl.when(k > 0)
        def _():
            acc_ref[...] += part

        @pl.when(k == nk - 1)
        def _():
            r = acc_ref[...]
            if bias_ref is not None:
                r = r + bias_ref[...]
            if res_ref is not None:
                r = r + res_ref[...].astype(F32)
            o_ref[...] = r.astype(out_dtype)

    return pl.pallas_call(
        body, name=name,
        grid=(M // tm, N // tn, nk),
        in_specs=in_specs,
        out_specs=_spec(out_shape, tm, tn, im, jn),
        out_shape=jax.ShapeDtypeStruct(out_shape, out_dtype),
        scratch_shapes=[pltpu.VMEM((tm, tn), F32)],
        compiler_params=_cparams(("parallel", "parallel", "arbitrary")),
    )(*args)


def _rms_fwd(x, g, *, name, tr=512):
    T, D = x.shape
    tr = min(tr, T)

    def body(x_ref, g_ref, h_ref):
        xv = x_ref[...]
        r = lax.rsqrt(jnp.mean(xv * xv, axis=-1, keepdims=True) + EPS)
        h_ref[...] = (xv * r * g_ref[...]).astype(BF16)

    return pl.pallas_call(
        body, name=name, grid=(T // tr,),
        in_specs=[pl.BlockSpec((tr, D), lambda i: (i, 0)), pl.BlockSpec((1, D), lambda i: (0, 0))],
        out_specs=pl.BlockSpec((tr, D), lambda i: (i, 0)),
        out_shape=jax.ShapeDtypeStruct((T, D), BF16),
        compiler_params=_cparams(("parallel",)),
    )(x, g.reshape(1, D))


def _rms_bwd(x, g, dh, dres, *, name, tr=512):
    T, D = x.shape
    tr = min(tr, T)

    def body(x_ref, g_ref, dh_ref, dres_ref, dx_ref, dg_ref):
        i = pl.program_id(0)
        xv = x_ref[...]
        dy = dh_ref[...].astype(F32)
        r = lax.rsqrt(jnp.mean(xv * xv, axis=-1, keepdims=True) + EPS)
        dyg = dy * g_ref[...]
        dot = jnp.mean(dyg * xv, axis=-1, keepdims=True)
        dx_ref[...] = dres_ref[...] + r * dyg - xv * (r * r * r) * dot
        part = jnp.sum(dy * xv * r, axis=0, keepdims=True)
        _acc_row(dg_ref, 0, part, i == 0)

    return pl.pallas_call(
        body, name=name, grid=(T // tr,),
        in_specs=[pl.BlockSpec((tr, D), lambda i: (i, 0)), pl.BlockSpec((1, D), lambda i: (0, 0)),
                  pl.BlockSpec((tr, D), lambda i: (i, 0)), pl.BlockSpec((tr, D), lambda i: (i, 0))],
        out_specs=[pl.BlockSpec((tr, D), lambda i: (i, 0)), pl.BlockSpec((1, D), lambda i: (0, 0))],
        out_shape=[jax.ShapeDtypeStruct((T, D), F32), jax.ShapeDtypeStruct((1, D), F32)],
        compiler_params=_cparams(("arbitrary",)),
    )(x, g.reshape(1, D), dh, dres)


def _loss_head(x, g, target, *, name, tr=512):
    T, D = x.shape
    tr = min(tr, T)

    def body(x_ref, g_ref, t_ref, loss_ref, dx_ref, dg_ref):
        i = pl.program_id(0)
        xv = x_ref[...]
        gv = g_ref[...]
        r = lax.rsqrt(jnp.mean(xv * xv, axis=-1, keepdims=True) + EPS)
        y = xv * r * gv
        err = y - t_ref[...]
        lpart = 0.5 * jnp.sum(jnp.mean(err * err, axis=-1, keepdims=True), axis=0, keepdims=True)
        dy = err * (1.0 / D)
        dyg = dy * gv
        dot = jnp.mean(dyg * xv, axis=-1, keepdims=True)
        dx_ref[...] = r * dyg - xv * (r * r * r) * dot
        _acc_row(dg_ref, 0, jnp.sum(dy * xv * r, axis=0, keepdims=True), i == 0)
        _acc_row(loss_ref, 0, jnp.broadcast_to(lpart, (1, LANES)), i == 0)

    return pl.pallas_call(
        body, name=name, grid=(T // tr,),
        in_specs=[pl.BlockSpec((tr, D), lambda i: (i, 0)), pl.BlockSpec((1, D), lambda i: (0, 0)),
                  pl.BlockSpec((tr, D), lambda i: (i, 0))],
        out_specs=[pl.BlockSpec((1, LANES), lambda i: (0, 0)), pl.BlockSpec((tr, D), lambda i: (i, 0)),
                   pl.BlockSpec((1, D), lambda i: (0, 0))],
        out_shape=[jax.ShapeDtypeStruct((1, LANES), F32), jax.ShapeDtypeStruct((T, D), F32),
                   jax.ShapeDtypeStruct((1, D), F32)],
        compiler_params=_cparams(("arbitrary",)),
    )(x, g.reshape(1, D), target)


def _conf_chain(av, ag, w_ref, b_ref, lg_ref, lb_ref):
    sg = _sig(ag)
    u0 = av * sg
    u1 = _conv_fwd(u0, w_ref, CONV_K) + b_ref[...]
    mu = jnp.mean(u1, axis=-1, keepdims=True)
    xc = u1 - mu
    r = lax.rsqrt(jnp.mean(xc * xc, axis=-1, keepdims=True) + EPS)
    n = xc * r
    u2 = n * lg_ref[...] + lb_ref[...]
    return sg, u0, r, n, u2


def _conf_fwd(p, dw_w, dw_b, ln_g, ln_b, *, Bl, S, CC, name):
    G = CC // LANES

    def body(av_ref, ag_ref, w_ref, b_ref, lg_ref, lb_ref, o_ref):
        _, _, _, _, u2 = _conf_chain(av_ref[...], ag_ref[...], w_ref, b_ref, lg_ref, lb_ref)
        o_ref[...] = _silu(u2).astype(BF16)

    vec = pl.BlockSpec((1, LANES), lambda b, j: (0, j))
    return pl.pallas_call(
        body, name=name, grid=(Bl, G),
        in_specs=[pl.BlockSpec((S, LANES), lambda b, j: (b, j)),
                  pl.BlockSpec((S, LANES), lambda b, j: (b, G + j)),
                  pl.BlockSpec((CONV_K, LANES), lambda b, j: (0, j)), vec, vec, vec],
        out_specs=pl.BlockSpec((S, LANES), lambda b, j: (b, j)),
        out_shape=jax.ShapeDtypeStruct((Bl * S, CC), BF16),
        compiler_params=_cparams(("parallel", "parallel")),
    )(p, p, dw_w, dw_b.reshape(1, CC), ln_g.reshape(1, CC), ln_b.reshape(1, CC))


def _conf_bwd(p, dw_w, dw_b, ln_g, ln_b, du3, *, Bl, S, CC, name):
    G = CC // LANES

    def body(av_ref, ag_ref, w_ref, b_ref, lg_ref, lb_ref, du3_ref,
             dav_ref, dag_ref, dw_ref, db_ref, dlg_ref, dlb_ref):
        first = pl.program_id(1) == 0
        av = av_ref[...]
        sg, u0, r, n, u2 = _conf_chain(av, ag_ref[...], w_ref, b_ref, lg_ref, lb_ref)
        du2 = du3_ref[...] * _dsilu(u2)
        _acc_row(dlg_ref, 0, jnp.sum(du2 * n, axis=0, keepdims=True), first)
        _acc_row(dlb_ref, 0, jnp.sum(du2, axis=0, keepdims=True), first)
        dn = du2 * lg_ref[...]
        du1 = r * (dn - jnp.mean(dn, axis=-1, keepdims=True) - n * jnp.mean(dn * n, axis=-1, keepdims=True))
        _acc_row(db_ref, 0, jnp.sum(du1, axis=0, keepdims=True), first)
        _conv_bwd_w(du1, u0, dw_ref, CONV_K, first)
        du0 = _conv_bwd_in(du1, w_ref, CONV_K)
        dav_ref[...] = (du0 * sg).astype(BF16)
        dag_ref[...] = (du0 * av * sg * (1.0 - sg)).astype(BF16)

    vec = pl.BlockSpec((1, LANES), lambda j, b: (0, j))
    seq = pl.BlockSpec((S, LANES), lambda j, b: (b, j))
    return pl.pallas_call(
        body, name=name, grid=(G, Bl),
        in_specs=[seq, pl.BlockSpec((S, LANES), lambda j, b: (b, G + j)),
                  pl.BlockSpec((CONV_K, LANES), lambda j, b: (0, j)), vec, vec, vec, seq],
        out_specs=[seq, seq, pl.BlockSpec((CONV_K, LANES), lambda j, b: (0, j)), vec, vec, vec],
        out_shape=[jax.ShapeDtypeStruct((Bl * S, CC), BF16), jax.ShapeDtypeStruct((Bl * S, CC), BF16),
                   jax.ShapeDtypeStruct((CONV_K, CC), F32), jax.ShapeDtypeStruct((1, CC), F32),
                   jax.ShapeDtypeStruct((1, CC), F32), jax.ShapeDtypeStruct((1, CC), F32)],
        compiler_params=_cparams(("parallel", "arbitrary")),
    )(p, p, dw_w, dw_b.reshape(1, CC), ln_g.reshape(1, CC), ln_b.reshape(1, CC), du3)


def _gdn_pre_fwd(p, conv_w, *, Bl, S, CC, KW, VW, name):
    NQK = 2 * KW // LANES
    NB = NQK + VW // LANES
    off = 2 * CC // LANES

    def body(x_ref, w_ref, o_ref):
        j = pl.program_id(1)
        s = _silu(_conv_fwd(x_ref[...], w_ref, SHORT_CONV_K))
        r = lax.rsqrt(jnp.sum(s * s, axis=-1, keepdims=True) + EPS)
        o_ref[...] = jnp.where(j < NQK, s * r, s)

    return pl.pallas_call(
        body, name=name, grid=(Bl, NB),
        in_specs=[pl.BlockSpec((S, LANES), lambda b, j: (b, off + j)),
                  pl.BlockSpec((SHORT_CONV_K, LANES), lambda b, j: (0, j))],
        out_specs=pl.BlockSpec((S, LANES), lambda b, j: (b, j)),
        out_shape=jax.ShapeDtypeStruct((Bl * S, NB * LANES), F32),
        compiler_params=_cparams(("parallel", "parallel")),
    )(p, conv_w)


def _gdn_pre_bwd(p, conv_w, dq, dk, dv, *, Bl, S, CC, KW, VW, name):
    HQ = KW // LANES
    H = VW // LANES
    NQK = 2 * HQ
    NB = NQK + H
    off = 2 * CC // LANES

    def body(x_ref, w_ref, dq_ref, dk_ref, dv_ref, dx_ref, dw_ref):
        j = pl.program_id(0)
        first = pl.program_id(1) == 0
        xv = x_ref[...]
        c = _conv_fwd(xv, w_ref, SHORT_CONV_K)
        s = _silu(c)
        r = lax.rsqrt(jnp.sum(s * s, axis=-1, keepdims=True) + EPS)
        dy = jnp.where(j < HQ, dq_ref[...], jnp.where(j < NQK, dk_ref[...], dv_ref[...]))
        ds_norm = r * dy - s * (r * r * r) * jnp.sum(s * dy, axis=-1, keepdims=True)
        ds = jnp.where(j < NQK, ds_norm, dy)
        dc = ds * _dsilu(c)
        _conv_bwd_w(dc, xv, dw_ref, SHORT_CONV_K, first)
        dx_ref[...] = _conv_bwd_in(dc, w_ref, SHORT_CONV_K).astype(BF16)

    return pl.pallas_call(
        body, name=name, grid=(NB, Bl),
        in_specs=[pl.BlockSpec((S, LANES), lambda j, b: (b, off + j)),
                  pl.BlockSpec((SHORT_CONV_K, LANES), lambda j, b: (0, j)),
                  pl.BlockSpec((S, LANES), lambda j, b: (b, jnp.minimum(j, HQ - 1))),
                  pl.BlockSpec((S, LANES), lambda j, b: (b, jnp.clip(j - HQ, 0, HQ - 1))),
                  pl.BlockSpec((S, LANES), lambda j, b: (b, jnp.clip(j - NQK, 0, H - 1)))],
        out_specs=[pl.BlockSpec((S, LANES), lambda j, b: (b, j)),
                   pl.BlockSpec((SHORT_CONV_K, LANES), lambda j, b: (0, j))],
        out_shape=[jax.ShapeDtypeStruct((Bl * S, NB * LANES), BF16),
                   jax.ShapeDtypeStruct((SHORT_CONV_K, NB * LANES), F32)],
        compiler_params=_cparams(("parallel", "arbitrary")),
    )(p, conv_w, dq, dk, dv)


def _lane_pick(h):
    row = lax.broadcasted_iota(jnp.int32, (LANES, LANES), 0)
    return (row == h).astype(F32)


def _gdn_gate_fwd(pba, a_log, dt_bias, *, Bl, S, H, name):
    def body(alog_ref, dtb_ref, x_ref, g_ref, beta_ref):
        xv = x_ref[...]
        for h in range(H):
            b_raw = _dot(xv, _lane_pick(h), HIGHEST)
            a_raw = _dot(xv, _lane_pick(H + h), HIGHEST)
            beta_ref[h] = _sig(b_raw)
            ea = jnp.exp(jnp.zeros((1, LANES), F32) + alog_ref[h])
            g_ref[h] = -ea * _softplus(a_raw + dtb_ref[h])

    smem = pl.BlockSpec(memory_space=pltpu.SMEM)
    rep = pl.BlockSpec((H, S, LANES), lambda b: (0, b, 0))
    return pl.pallas_call(
        body, name=name, grid=(Bl,),
        in_specs=[smem, smem, pl.BlockSpec((S, LANES), lambda b: (b, 0))],
        out_specs=[rep, rep],
        out_shape=[jax.ShapeDtypeStruct((H, Bl * S, LANES), F32)] * 2,
        compiler_params=_cparams(("parallel",)),
    )(a_log, dt_bias, pba)


def _gdn_gate_bwd(pba, a_log, dt_bias, dg, dbeta, *, Bl, S, H, name):
    HP = 8 * ((H + 7) // 8)

    def body(alog_ref, dtb_ref, x_ref, dg_ref, dbeta_ref, dx_ref, dalog_ref, ddtb_ref):
        first = pl.program_id(0) == 0
        xv = x_ref[...]
        lane = lax.broadcasted_iota(jnp.int32, (S, LANES), 1)
        acc = jnp.zeros((S, LANES), F32)

        @pl.when(first)
        def _():
            dalog_ref[...] = jnp.zeros_like(dalog_ref)
            ddtb_ref[...] = jnp.zeros_like(ddtb_ref)

        for h in range(H):
            b_raw = _dot(xv, _lane_pick(h), HIGHEST)
            a_raw = _dot(xv, _lane_pick(H + h), HIGHEST)
            beta = _sig(b_raw)
            db_raw = dbeta_ref[h] * beta * (1.0 - beta)
            z = a_raw + dtb_ref[h]
            ea = jnp.exp(jnp.zeros((1, LANES), F32) + alog_ref[h])
            dgv = dg_ref[h]
            da_raw = dgv * (-ea) * _sig(z)
            g = -ea * _softplus(z)
            dalog_ref[h:h + 1, :] += jnp.sum(dgv * g, axis=0, keepdims=True)
            ddtb_ref[h:h + 1, :] += jnp.sum(da_raw, axis=0, keepdims=True)
            acc = acc + jnp.where(lane == h, db_raw, 0.0) + jnp.where(lane == H + h, da_raw, 0.0)
        dx_ref[...] = acc.astype(BF16)

    smem = pl.BlockSpec(memory_space=pltpu.SMEM)
    rep = pl.BlockSpec((H, S, LANES), lambda b: (0, b, 0))
    small = pl.BlockSpec((HP, LANES), lambda b: (0, 0))
    return pl.pallas_call(
        body, name=name, grid=(Bl,),
        in_specs=[smem, smem, pl.BlockSpec((S, LANES), lambda b: (b, 0)), rep, rep],
        out_specs=[pl.BlockSpec((S, LANES), lambda b: (b, 0)), small, small],
        out_shape=[jax.ShapeDtypeStruct((Bl * S, LANES), BF16),
                   jax.ShapeDtypeStruct((HP, LANES), F32), jax.ShapeDtypeStruct((HP, LANES), F32)],
        compiler_params=_cparams(("arbitrary",)),
    )(a_log, dt_bias, pba, dg, dbeta)


def _tri_masks():
    ri = lax.broadcasted_iota(jnp.int32, (CHUNK, CHUNK), 0)
    ci = lax.broadcasted_iota(jnp.int32, (CHUNK, CHUNK), 1)
    return ri >= ci, ri > ci, ri == CHUNK - 1


def _tri_inv(L):
    ri = lax.broadcasted_iota(jnp.int32, (CHUNK, CHUNK), 0)
    ci = lax.broadcasted_iota(jnp.int32, (CHUNK, CHUNK), 1)
    T = jnp.where(ri == ci, 1.0, 0.0) - jnp.where((ri >> 1) == (ci >> 1), L, 0.0)
    for lv in range(2, int(math.log2(CHUNK)) + 1):
        O = jnp.where(((ri >> lv) == (ci >> lv)) & ((ri >> (lv - 1)) != (ci >> (lv - 1))), L, 0.0)
        T = T - _dot(T, _dot(O, T, HIGHEST), HIGHEST)
    return T


def _chunk_local(q, k, v, beta, g):
    ge, gt, last = _tri_masks()
    gam = _dot(ge.astype(F32), g, HIGHEST)
    D = jnp.where(ge, jnp.exp(jnp.where(ge, gam - gam.T, 0.0)), 0.0)
    kb = k * beta
    vb = v * beta
    M = _dot_nt(_bf(kb), _bf(k))
    L = jnp.where(gt, M * D, 0.0)
    eg = jnp.exp(gam)
    kbg = kb * eg
    P = _dot_nt(_bf(q), _bf(k))
    QK = jnp.where(ge, P * D, 0.0)
    gl = jnp.sum(jnp.where(last, gam, 0.0), axis=0, keepdims=True)
    el = jnp.exp(gl - gam)
    return dict(ge=ge, gt=gt, last=last, gam=gam, D=D, kb=kb, vb=vb, L=L, eg=eg, kbg=kbg, QK=QK, gl=gl,
                el=el, kd=k * el, qg=q * eg)


def _chunk_fwd(q, k, v, beta, g, S):
    c = _chunk_local(q, k, v, beta, g)
    T = _tri_inv(c["L"])
    u = _dot(T, c["vb"], HIGHEST)
    w = _dot(T, c["kbg"], HIGHEST)
    Sb = _bf(S)
    vn = u - _dot(_bf(w), Sb)
    o = _dot(_bf(c["qg"]), Sb) + _dot(_bf(c["QK"]), _bf(vn))
    S2 = S * jnp.exp(c["gl"]) + _dot_tn(_bf(c["kd"]), _bf(vn))
    return o, S2, T


def _rowsum(x):
    return jnp.sum(x, axis=-1, keepdims=True)


def _dot_tn_hi(a, b):
    return lax.dot_general(a, b, (((0,), (0,)), ((), ())), preferred_element_type=F32, precision=HIGHEST)


def _chunk_bwd(q, k, v, beta, g, S, T, do, dS2):
    c = _chunk_local(q, k, v, beta, g)
    ge, gt, last = c["ge"], c["gt"], c["last"]
    u = _dot(T, c["vb"], HIGHEST)
    w = _dot(T, c["kbg"], HIGHEST)
    Sb = _bf(S)
    vn = u - _dot(_bf(w), Sb)
    dob, vnb, dS2b = _bf(do), _bf(vn), _bf(dS2)
    e_last = jnp.exp(c["gl"])
    dqg = _dot_nt(dob, Sb)
    dS = _dot_tn(_bf(c["qg"]), dob)
    dQK = jnp.where(ge, _dot_nt(dob, vnb), 0.0)
    dvn = _dot_tn(_bf(c["QK"]), dob)
    dS = dS + dS2 * e_last
    de_last = jnp.sum(jnp.sum(dS2 * S, axis=0, keepdims=True), axis=1, keepdims=True)
    dkd = _dot_nt(vnb, dS2b)
    dvn = dvn + _dot(_bf(c["kd"]), dS2b)
    dvnb = _bf(dvn)
    dw = -_dot_nt(dvnb, Sb)
    dS = dS - _dot_tn(_bf(w), dvnb)
    dvb = _dot_tn_hi(T, dvn)
    dkbg = _dot_tn_hi(T, dw)
    dA = -(_dot_nt(_bf(dvb), _bf(u)) + _dot_nt(_bf(dkbg), _bf(w)))
    dL = jnp.where(gt, dA, 0.0)
    dM = dL * c["D"]
    dP = dQK * c["D"]
    E = dL * c["L"] + dQK * c["QK"]
    kbf = _bf(k)
    dkb = _dot(_bf(dM), kbf) + dkbg * c["eg"]
    dk = _dot_tn(_bf(dM), _bf(c["kb"])) + _dot_tn(_bf(dP), _bf(q)) + dkd * c["el"] + dkb * beta
    dq = _dot(_bf(dP), kbf) + dqg * c["eg"]
    s_kd = _rowsum(dkd * c["kd"])
    dgam = (_rowsum(E) - _rowsum(E.T) + _rowsum(dqg * c["qg"]) - s_kd + _rowsum(dkbg * c["kbg"]))
    dgl = jnp.sum(s_kd, axis=0, keepdims=True) + de_last * e_last
    dgam_rep = jnp.broadcast_to(dgam, (CHUNK, LANES)) + jnp.where(last, jnp.broadcast_to(dgl, (CHUNK, LANES)), 0.0)
    dg_rep = lax.dot_general(ge.astype(F32), dgam_rep, (((0,), (0,)), ((), ())),
                             preferred_element_type=F32, precision=HIGHEST)
    dbeta = _rowsum(dkb * k) + _rowsum(dvb * v)
    dv = dvb * beta
    return dq, dk, dv, jnp.broadcast_to(dbeta, (CHUNK, LANES)), dg_rep, dS


def _gdn_core_fwd(qkv, g, beta, *, Bl, S, KW, VW, name):
    HQ = KW // LANES
    H = VW // LANES
    NC = S // CHUNK
    scale = float(LANES) ** -0.5

    def body(q_ref, k_ref, v_ref, g_ref, beta_ref, o_ref, st_ref, t_ref):
        def step(n, states):
            rows = pl.ds(pl.multiple_of(n * CHUNK, CHUNK), CHUNK)
            q = q_ref[rows, :] * scale
            k = k_ref[rows, :]
            out = []
            for e in range(2):
                st_ref[e, n] = states[e]
                o, S2, Tb = _chunk_fwd(q, k, v_ref[rows, e * LANES:(e + 1) * LANES], beta_ref[e, rows, :],
                                       g_ref[e, rows, :], states[e])
                o_ref[rows, e * LANES:(e + 1) * LANES] = o
                t_ref[e, rows, :] = Tb
                out.append(S2)
            return tuple(out)

        z = jnp.zeros((LANES, LANES), F32)
        lax.fori_loop(0, NC, step, (z, z))

    rep = pl.BlockSpec((2, S, LANES), lambda b, h: (h, b, 0))
    return pl.pallas_call(
        body, name=name, grid=(Bl, HQ),
        in_specs=[pl.BlockSpec((S, LANES), lambda b, h: (b, h)),
                  pl.BlockSpec((S, LANES), lambda b, h: (b, HQ + h)),
                  pl.BlockSpec((S, 2 * LANES), lambda b, h: (b, HQ + h)), rep, rep],
        out_specs=[pl.BlockSpec((S, 2 * LANES), lambda b, h: (b, h)),
                   pl.BlockSpec((None, 2, NC, LANES, LANES), lambda b, h: (b, h, 0, 0, 0)), rep],
        out_shape=[jax.ShapeDtypeStruct((Bl * S, VW), F32),
                   jax.ShapeDtypeStruct((Bl, H, NC, LANES, LANES), F32),
                   jax.ShapeDtypeStruct((H, Bl * S, LANES), F32)],
        compiler_params=_cparams(("parallel", "parallel")),
    )(qkv, qkv, qkv, g, beta)


def _gdn_core_bwd(qkv, g, beta, states, tinv, do, *, Bl, S, KW, VW, name):
    HQ = KW // LANES
    H = VW // LANES
    NC = S // CHUNK
    scale = float(LANES) ** -0.5

    def body(q_ref, k_ref, v_ref, g_ref, beta_ref, st_ref, t_ref, do_ref,
             dq_ref, dk_ref, dv_ref, dg_ref, dbeta_ref):
        def step(i, dstates):
            n = NC - 1 - i
            rows = pl.ds(pl.multiple_of(n * CHUNK, CHUNK), CHUNK)
            q = q_ref[rows, :] * scale
            k = k_ref[rows, :]
            out = []
            dq_sum = dk_sum = None
            for e in range(2):
                cols = slice(e * LANES, (e + 1) * LANES)
                dq, dk, dv, dbeta, dg, dS = _chunk_bwd(q, k, v_ref[rows, cols], beta_ref[e, rows, :],
                                                       g_ref[e, rows, :], st_ref[e, n], t_ref[e, rows, :],
                                                       do_ref[rows, cols], dstates[e])
                dv_ref[rows, cols] = dv
                dg_ref[e, rows, :] = dg
                dbeta_ref[e, rows, :] = dbeta
                dq_sum = dq if dq_sum is None else dq_sum + dq
                dk_sum = dk if dk_sum is None else dk_sum + dk
                out.append(dS)
            dq_ref[rows, :] = dq_sum * scale
            dk_ref[rows, :] = dk_sum
            return tuple(out)

        z = jnp.zeros((LANES, LANES), F32)
        lax.fori_loop(0, NC, step, (z, z))

    rep = pl.BlockSpec((2, S, LANES), lambda b, h: (h, b, 0))
    seq = pl.BlockSpec((S, LANES), lambda b, h: (b, h))
    seq2 = pl.BlockSpec((S, 2 * LANES), lambda b, h: (b, h))
    return pl.pallas_call(
        body, name=name, grid=(Bl, HQ),
        in_specs=[seq, pl.BlockSpec((S, LANES), lambda b, h: (b, HQ + h)),
                  pl.BlockSpec((S, 2 * LANES), lambda b, h: (b, HQ + h)), rep, rep,
                  pl.BlockSpec((None, 2, NC, LANES, LANES), lambda b, h: (b, h, 0, 0, 0)), rep, seq2],
        out_specs=[seq, seq, seq2, rep, rep],
        out_shape=[jax.ShapeDtypeStruct((Bl * S, KW), F32), jax.ShapeDtypeStruct((Bl * S, KW), F32),
                   jax.ShapeDtypeStruct((Bl * S, VW), F32),
                   jax.ShapeDtypeStruct((H, Bl * S, LANES), F32), jax.ShapeDtypeStruct((H, Bl * S, LANES), F32)],
        compiler_params=_cparams(("parallel", "parallel")),
    )(qkv, qkv, qkv, g, beta, states, tinv, do)


def _gdn_out_fwd(o, p, norm_g, out_a, *, CC, VW, name, tr=256):
    T = o.shape[0]
    tr = min(tr, T)
    H = VW // LANES
    zoff = p.shape[1] // VW - 1

    def body(o_ref, z_ref, ng_ref, a_ref, mix_ref):
        mix_ref[:, :CC] = a_ref[...]
        for h in range(H):
            cols = slice(h * LANES, (h + 1) * LANES)
            ov = o_ref[:, cols]
            r = lax.rsqrt(jnp.mean(ov * ov, axis=-1, keepdims=True) + EPS)
            mix_ref[:, CC + h * LANES:CC + (h + 1) * LANES] = (ov * r * ng_ref[...] * _silu(z_ref[:, cols])).astype(BF16)

    return pl.pallas_call(
        body, name=name, grid=(T // tr,),
        in_specs=[pl.BlockSpec((tr, VW), lambda i: (i, 0)), pl.BlockSpec((tr, VW), lambda i: (i, zoff)),
                  pl.BlockSpec((1, LANES), lambda i: (0, 0)), pl.BlockSpec((tr, CC), lambda i: (i, 0))],
        out_specs=pl.BlockSpec((tr, CC + VW), lambda i: (i, 0)),
        out_shape=jax.ShapeDtypeStruct((T, CC + VW), BF16),
        compiler_params=_cparams(("parallel",)),
    )(o, p, norm_g.reshape(1, LANES), out_a)


def _gdn_out_bwd(o, p, norm_g, dmix, *, CC, VW, name, tr=256):
    T = o.shape[0]
    tr = min(tr, T)
    H = VW // LANES
    zoff = p.shape[1] // VW - 1

    def body(o_ref, z_ref, ng_ref, dmix_ref, do_ref, dz_ref, da_ref, dng_ref, dpb_ref):
        first = pl.program_id(0) == 0
        da = dmix_ref[:, :CC]
        da_ref[...] = da.astype(BF16)
        _acc_row(dpb_ref, 0, jnp.sum(da, axis=0, keepdims=True), first)
        ng = ng_ref[...]
        dng = jnp.zeros((1, LANES), F32)
        for h in range(H):
            cols = slice(h * LANES, (h + 1) * LANES)
            ov = o_ref[:, cols]
            zv = z_ref[:, cols]
            dout = dmix_ref[:, CC + h * LANES:CC + (h + 1) * LANES]
            r = lax.rsqrt(jnp.mean(ov * ov, axis=-1, keepdims=True) + EPS)
            on = ov * r * ng
            don = dout * _silu(zv)
            dz_ref[:, cols] = (dout * on * _dsilu(zv)).astype(BF16)
            dng = dng + jnp.sum(don * ov * r, axis=0, keepdims=True)
            dong = don * ng
            do_ref[:, cols] = r * dong - ov * (r * r * r) * jnp.mean(dong * ov, axis=-1, keepdims=True)
        _acc_row(dng_ref, 0, dng, first)

    return pl.pallas_call(
        body, name=name, grid=(T // tr,),
        in_specs=[pl.BlockSpec((tr, VW), lambda i: (i, 0)), pl.BlockSpec((tr, VW), lambda i: (i, zoff)),
                  pl.BlockSpec((1, LANES), lambda i: (0, 0)), pl.BlockSpec((tr, CC + VW), lambda i: (i, 0))],
        out_specs=[pl.BlockSpec((tr, VW), lambda i: (i, 0)), pl.BlockSpec((tr, VW), lambda i: (i, 0)),
                   pl.BlockSpec((tr, CC), lambda i: (i, 0)), pl.BlockSpec((1, LANES), lambda i: (0, 0)),
                   pl.BlockSpec((1, CC), lambda i: (0, 0))],
        out_shape=[jax.ShapeDtypeStruct((T, VW), F32), jax.ShapeDtypeStruct((T, VW), BF16),
                   jax.ShapeDtypeStruct((T, CC), BF16), jax.ShapeDtypeStruct((1, LANES), F32),
                   jax.ShapeDtypeStruct((1, CC), F32)],
        compiler_params=_cparams(("arbitrary",)),
    )(o, p, norm_g.reshape(1, LANES), dmix)


FFN_CW = 256


def _ffn_act_fwd(gu, conv_w, conv_b, *, Bl, S, name):
    FF = gu.shape[2]
    cw = min(FFN_CW, FF)

    def body(g_ref, u_ref, w_ref, b_ref, a_ref):
        gc = _conv_fwd(g_ref[...], w_ref, FFN_CONV_K) + b_ref[...]
        a_ref[...] = (_silu(gc) * u_ref[...]).astype(BF16)

    return pl.pallas_call(
        body, name=name, grid=(Bl, FF // cw),
        in_specs=[pl.BlockSpec((None, S, cw), lambda b, j: (0, b, j)),
                  pl.BlockSpec((None, S, cw), lambda b, j: (1, b, j)),
                  pl.BlockSpec((FFN_CONV_K, cw), lambda b, j: (0, j)),
                  pl.BlockSpec((1, cw), lambda b, j: (0, j))],
        out_specs=pl.BlockSpec((S, cw), lambda b, j: (b, j)),
        out_shape=jax.ShapeDtypeStruct((Bl * S, FF), BF16),
        compiler_params=_cparams(("parallel", "parallel")),
    )(gu, gu, conv_w, conv_b.reshape(1, FF))


def _ffn_act_bwd(gu, conv_w, conv_b, da, *, Bl, S, name):
    FF = gu.shape[2]
    cw = min(FFN_CW, FF)

    def body(g_ref, u_ref, w_ref, b_ref, da_ref, dgu_ref, dw_ref, db_ref):
        first = pl.program_id(1) == 0
        gate = g_ref[...]
        gc = _conv_fwd(gate, w_ref, FFN_CONV_K) + b_ref[...]
        dav = da_ref[...]
        dgu_ref[1] = (dav * _silu(gc)).astype(BF16)
        dgc = dav * u_ref[...] * _dsilu(gc)
        _acc_row(db_ref, 0, jnp.sum(dgc, axis=0, keepdims=True), first)
        _conv_bwd_w(dgc, gate, dw_ref, FFN_CONV_K, first)
        dgu_ref[0] = _conv_bwd_in(dgc, w_ref, FFN_CONV_K).astype(BF16)

    return pl.pallas_call(
        body, name=name, grid=(FF // cw, Bl),
        in_specs=[pl.BlockSpec((None, S, cw), lambda j, b: (0, b, j)),
                  pl.BlockSpec((None, S, cw), lambda j, b: (1, b, j)),
                  pl.BlockSpec((FFN_CONV_K, cw), lambda j, b: (0, j)),
                  pl.BlockSpec((1, cw), lambda j, b: (0, j)),
                  pl.BlockSpec((S, cw), lambda j, b: (b, j))],
        out_specs=[pl.BlockSpec((2, S, cw), lambda j, b: (0, b, j)),
                   pl.BlockSpec((FFN_CONV_K, cw), lambda j, b: (0, j)),
                   pl.BlockSpec((1, cw), lambda j, b: (0, j))],
        out_shape=[jax.ShapeDtypeStruct((2, Bl * S, FF), BF16),
                   jax.ShapeDtypeStruct((FFN_CONV_K, FF), F32), jax.ShapeDtypeStruct((1, FF), F32)],
        compiler_params=_cparams(("parallel", "arbitrary")),
    )(gu, gu, conv_w, conv_b.reshape(1, FF), da)


def _layer_dims(W):
    CC = W["conv_pw_b"].shape[0]
    VW = W["mix_norm_g"].shape[0] - CC
    KW = (W["gdn_conv_w"].shape[1] - VW) // 2
    return CC, KW, VW


def _layer_fwd(l, x, W, Bl, S, fetch):
    CC, KW, VW = _layer_dims(W)
    H = VW // LANES
    w_in_main, w_in_ba = fetch(l, "w_in", x)
    h1 = _rms_fwd(x, W["mix_norm_g"], name="rms1_fwd")
    p = _mm(h1, w_in_main, name="mm_in")
    pba = _mm(h1, w_in_ba, name="mm_in_ba")
    u3 = _conf_fwd(p, W["conv_dw_w"], W["conv_dw_b"], W["conv_ln_g"], W["conv_ln_b"], Bl=Bl, S=S, CC=CC,
                   name="conf_fwd")
    conv_pw_w = fetch(l, "conv_pw_w", u3)
    out_a = _mm(u3, conv_pw_w, bias=W["conv_pw_b"], out_dtype=BF16, name="mm_pw")
    qkv = _gdn_pre_fwd(p, W["gdn_conv_w"], Bl=Bl, S=S, CC=CC, KW=KW, VW=VW, name="gdn_pre_fwd")
    g, beta = _gdn_gate_fwd(pba, W["gdn_a_log"], W["gdn_dt_bias"], Bl=Bl, S=S, H=H, name="gdn_gate_fwd")
    o, states, tinv = _gdn_core_fwd(qkv, g, beta, Bl=Bl, S=S, KW=KW, VW=VW, name="gdn_core_fwd")
    mix = _gdn_out_fwd(o, p, W["gdn_norm_g"], out_a, CC=CC, VW=VW, name="gdn_out_fwd")
    w_out = fetch(l, "w_out", mix)
    x1 = _mm(mix, w_out, res=x, name="mm_out")
    h2 = _rms_fwd(x1, W["ffn_norm_g"], name="rms2_fwd")
    w_up = fetch(l, "w_up", h2)
    gu = _mm(h2, w_up, out_blocks=2, tn=w_up.shape[2], name="mm_up")
    a = _ffn_act_fwd(gu, W["ffn_conv_w"], W["ffn_conv_b"], Bl=Bl, S=S, name="ffn_act_fwd")
    w_down = fetch(l, "w_down", a)
    x2 = _mm(a, w_down, res=x1, name="mm_down")
    saved = dict(x=x, h1=h1, p=p, pba=pba, u3=u3, qkv=qkv, g=g, beta=beta, o=o, states=states, tinv=tinv,
                 mix=mix, x1=x1, h2=h2, gu=gu, a=a, w_in_main=w_in_main, w_in_ba=w_in_ba, conv_pw_w=conv_pw_w,
                 w_out=w_out, w_up=w_up, w_down=w_down)
    return x2, saved


def _layer_bwd(l, dx2, W, A, Bl, S, sink):
    CC, KW, VW = _layer_dims(W)
    H = VW // LANES
    G = {}
    da = _mm(dx2, A["w_down"], tb=True, name="mm_down_dx")
    da = sink(l, "w_down", _mm(A["a"], dx2, ta=True, out_dtype=BF16, name="mm_down_dw"), da)
    dgu, G["ffn_conv_w"], G["ffn_conv_b"] = _ffn_act_bwd(A["gu"], W["ffn_conv_w"], W["ffn_conv_b"], da,
                                                         Bl=Bl, S=S, name="ffn_act_bwd")
    upw = A["w_up"].shape[2]
    dh2 = _mm(dgu, A["w_up"], tb=True, tk=upw, name="mm_up_dx")
    dh2 = sink(l, "w_up", _mm(A["h2"], dgu, ta=True, out_dtype=BF16, out_blocks=N_DEV, tn=upw, name="mm_up_dw"),
               dh2)
    dx1, G["ffn_norm_g"] = _rms_bwd(A["x1"], W["ffn_norm_g"], dh2, dx2, name="rms2_bwd")
    dmix = _mm(dx1, A["w_out"], tb=True, name="mm_out_dx")
    dmix = sink(l, "w_out", _mm(A["mix"], dx1, ta=True, out_dtype=BF16, name="mm_out_dw"), dmix)
    do, dz, dout_a, G["gdn_norm_g"], G["conv_pw_b"] = _gdn_out_bwd(A["o"], A["p"], W["gdn_norm_g"], dmix,
                                                                   CC=CC, VW=VW, name="gdn_out_bwd")
    dq, dk, dv, dg, dbeta = _gdn_core_bwd(A["qkv"], A["g"], A["beta"], A["states"], A["tinv"], do,
                                          Bl=Bl, S=S, KW=KW, VW=VW, name="gdn_core_bwd")
    dpba, dalog, ddtb = _gdn_gate_bwd(A["pba"], W["gdn_a_log"], W["gdn_dt_bias"], dg, dbeta, Bl=Bl, S=S, H=H,
                                      name="gdn_gate_bwd")
    G["gdn_a_log"], G["gdn_dt_bias"] = dalog[:H, 0], ddtb[:H, 0]
    dqkv, G["gdn_conv_w"] = _gdn_pre_bwd(A["p"], W["gdn_conv_w"], dq, dk, dv, Bl=Bl, S=S, CC=CC, KW=KW, VW=VW,
                                         name="gdn_pre_bwd")
    du3 = _mm(dout_a, A["conv_pw_w"], tb=True, name="mm_pw_dx")
    du3 = sink(l, "conv_pw_w", _mm(A["u3"], dout_a, ta=True, out_dtype=BF16, name="mm_pw_dw"), du3)
    dav, dag, G["conv_dw_w"], G["conv_dw_b"], G["conv_ln_g"], G["conv_ln_b"] = _conf_bwd(
        A["p"], W["conv_dw_w"], W["conv_dw_b"], W["conv_ln_g"], W["conv_ln_b"], du3, Bl=Bl, S=S, CC=CC,
        name="conf_bwd")
    dp = jnp.concatenate([dav, dag, dqkv, dz], axis=1)
    dh1 = _mm(dpba, A["w_in_ba"], tb=True, name="mm_in_ba_dx")
    dh1 = _mm(dp, A["w_in_main"], tb=True, res=dh1, name="mm_in_dx")
    dh1 = sink(l, "w_in", (_mm(A["h1"], dp, ta=True, out_dtype=BF16, name="mm_in_dw"),
                           _mm(A["h1"], dpba, ta=True, out_dtype=BF16, name="mm_in_ba_dw")), dh1)
    dx, G["mix_norm_g"] = _rms_bwd(A["x"], W["mix_norm_g"], dh1, dx1, name="rms1_bwd")
    return dx, G


def _local_step(x, target, Ws, final_norm_g, fetch, sink):
    Bl, S, D = x.shape
    xt = x.reshape(Bl * S, D)
    acts = []
    for l, W in enumerate(Ws):
        xt, A = _layer_fwd(l, xt, W, Bl, S, fetch)
        acts.append(A)
    loss, dx, dgf = _loss_head(xt, final_norm_g, target.reshape(Bl * S, D), name="loss_head")
    grads = [None] * len(Ws)
    for l in reversed(range(len(Ws))):
        dx, grads[l] = _layer_bwd(l, dx, Ws[l], acts[l], Bl, S, sink)
    return loss[0, 0], dx.reshape(Bl, S, D), grads, dgf.reshape(D)


def _mesh_pos():
    return lax.axis_index("x"), lax.axis_index("y"), lax.axis_index("c")


def _dev_index(px, py, pc):
    return 4 * px + 2 * py + pc


_ANY = pl.BlockSpec(memory_space=pl.ANY)


def _all_gather(arrs, *, name):
    n = len(arrs)

    def body(*refs):
        ins, outs = refs[:n], refs[n:2 * n]
        send_sems, recv_sems, local_sems = refs[2 * n:]
        x, y, c = _mesh_pos()
        me, sibling = (x, y, c), (x, y, 1 - c)
        chips = [(1 - x, y), (x, 1 - y), (1 - x, 1 - y)]

        def copy(a, k, block, to, src=None):
            dst = outs[a].at[_dev_index(*block)]
            return pltpu.make_async_remote_copy(
                src_ref=dst if src is None else src, dst_ref=dst,
                send_sem=send_sems.at[a, k], recv_sem=recv_sems.at[a, k],
                device_id=to, device_id_type=MESH)

        mine = [pltpu.make_async_copy(ins[a], outs[a].at[_dev_index(*me)], local_sems.at[a]) for a in range(n)]
        for cp in mine:
            cp.start()
        first = []
        for a in range(n):
            first.append(copy(a, 0, me, sibling, src=ins[a]))
            first += [copy(a, 1 + j, me, (*chip, c), src=ins[a]) for j, chip in enumerate(chips)]
        for cp in first:
            cp.start()
        passed = []
        for a in range(n):
            for j, chip in enumerate(chips):
                copy(a, 1 + j, (*chip, c), me).wait_recv()
                fwd = copy(a, 4 + j, (*chip, c), sibling)
                fwd.start()
                passed.append(fwd)
        for a in range(n):
            copy(a, 0, sibling, me).wait_recv()
            for j, chip in enumerate(chips):
                copy(a, 4 + j, (*chip, 1 - c), me).wait_recv()
        for cp in first + passed:
            cp.wait_send()
        for cp in mine:
            cp.wait()

    return pl.pallas_call(
        body, name=name,
        in_specs=[_ANY] * n, out_specs=[_ANY] * n,
        out_shape=[jax.ShapeDtypeStruct((N_DEV,) + a.shape, a.dtype) for a in arrs],
        scratch_shapes=[pltpu.SemaphoreType.DMA((n, N_DEV - 1)), pltpu.SemaphoreType.DMA((n, N_DEV - 1)),
                        pltpu.SemaphoreType.DMA((n,))],
    )(*arrs)


def _peers(x, y, c):
    flip = lambda v, f: 1 - v if f else v
    return [(flip(x, p & 4), flip(y, p & 2), flip(c, p & 1)) for p in range(1, N_DEV)]


GATHER_ID, SCATTER_ID = 1, 2
_SEQUENCER = dict(axis_name="sequencer", num_cores=1)


def _handshake(peers):
    barrier = pltpu.get_barrier_semaphore()
    for peer in peers:
        pl.semaphore_signal(barrier, inc=1, device_id=peer, device_id_type=MESH)
    pl.semaphore_wait(barrier, len(peers))


def _sc_gather(src, *, name):
    def body(src_ref, zone_ref, send_sems, recv_sems, local_sem):
        x, y, c = _mesh_pos()
        me, sibling = (x, y, c), (x, y, 1 - c)
        chips = [(1 - x, y), (x, 1 - y), (1 - x, 1 - y)]
        _handshake([sibling] + [(*chip, c) for chip in chips])

        def copy(k, block, to, from_src=False):
            dst = zone_ref.at[_dev_index(*block)]
            return pltpu.make_async_remote_copy(
                src_ref=src_ref if from_src else dst, dst_ref=dst, send_sem=send_sems.at[k], recv_sem=recv_sems.at[k],
                device_id=to, device_id_type=MESH)

        mine = pltpu.make_async_copy(src_ref, zone_ref.at[_dev_index(*me)], local_sem)
        mine.start()
        first = [copy(1 + j, me, (*chip, c), from_src=True) for j, chip in enumerate(chips)]
        first.append(copy(0, me, sibling, from_src=True))
        for cp in first:
            cp.start()
        passed = []
        for j, chip in enumerate(chips):
            copy(1 + j, (*chip, c), me).wait_recv()
            fwd = copy(4 + j, (*chip, c), sibling)
            fwd.start()
            passed.append(fwd)
        copy(0, sibling, me).wait_recv()
        for j, chip in enumerate(chips):
            copy(4 + j, (*chip, 1 - c), me).wait_recv()
        for cp in first + passed:
            cp.wait_send()
        mine.wait()

    return pl.kernel(
        body, name=name,
        out_type=jax.ShapeDtypeStruct((N_DEV,) + src.shape, src.dtype),
        mesh=plsc.ScalarSubcoreMesh(**_SEQUENCER),
        scratch_types=[pltpu.SemaphoreType.DMA((N_DEV - 1,)), pltpu.SemaphoreType.DMA((N_DEV - 1,)),
                       pltpu.SemaphoreType.DMA],
        compiler_params=pltpu.CompilerParams(collective_id=GATHER_ID),
    )(src)


def _sc_scatter(part, *, name):
    def body(src_ref, zone_ref, send_sems, recv_sems, local_sem):
        x, y, c = _mesh_pos()
        me = _dev_index(x, y, c)
        peers = _peers(x, y, c)
        _handshake(peers)
        mine = pltpu.make_async_copy(src_ref.at[me], zone_ref.at[me], local_sem)
        mine.start()
        sends = [pltpu.make_async_remote_copy(
            src_ref=src_ref.at[_dev_index(*peer)], dst_ref=zone_ref.at[me], send_sem=send_sems.at[k],
            recv_sem=recv_sems.at[k], device_id=peer, device_id_type=MESH) for k, peer in enumerate(peers)]
        for cp in sends:
            cp.start()
        for k, peer in enumerate(peers):
            pltpu.make_async_remote_copy(
                src_ref=src_ref.at[me], dst_ref=zone_ref.at[_dev_index(*peer)], send_sem=send_sems.at[k],
                recv_sem=recv_sems.at[k], device_id=peer, device_id_type=MESH).wait_recv()
        for cp in sends:
            cp.wait_send()
        mine.wait()

    return pl.kernel(
        body, name=name,
        out_type=jax.ShapeDtypeStruct(part.shape, part.dtype),
        mesh=plsc.ScalarSubcoreMesh(**_SEQUENCER),
        scratch_types=[pltpu.SemaphoreType.DMA((N_DEV - 1,)), pltpu.SemaphoreType.DMA((N_DEV - 1,)),
                       pltpu.SemaphoreType.DMA],
        compiler_params=pltpu.CompilerParams(collective_id=SCATTER_ID),
    )(part)


def _adamw_math(w, g, m, v):
    m2 = ADAM_B1 * m + (1.0 - ADAM_B1) * g
    v2 = ADAM_B2 * v + (1.0 - ADAM_B2) * (g * g)
    m_hat = m2 / (1.0 - ADAM_B1 ** ADAM_STEP)
    v_hat = v2 / (1.0 - ADAM_B2 ** ADAM_STEP)
    delta = -ADAM_LR * (m_hat / (jnp.sqrt(v_hat) + ADAM_EPS) + ADAM_WD * w)
    return delta, m2, v2


def _adamw_big(l, w, m, v, recv, *, name, tr=128):
    _, R, C = w.shape
    tr = next(t for t in range(min(tr, R), 0, -16) if R % t == 0)

    def body(w_ref, m_ref, v_ref, r_ref, g_ref, d_ref, m2_ref, v2_ref):
        g = r_ref[0].astype(F32)
        for s in range(1, N_DEV):
            g = g + r_ref[s].astype(F32)
        g_ref[...] = g
        d_ref[...], m2_ref[...], v2_ref[...] = _adamw_math(w_ref[...], g, m_ref[...], v_ref[...])

    wspec = pl.BlockSpec((None, tr, C), lambda i: (l, i, 0))
    ospec = pl.BlockSpec((tr, C), lambda i: (i, 0))
    return pl.pallas_call(
        body, name=name, grid=(R // tr,),
        in_specs=[wspec, wspec, wspec, pl.BlockSpec((N_DEV, tr, C), lambda i: (0, i, 0))],
        out_specs=[ospec] * 4,
        out_shape=[jax.ShapeDtypeStruct((R, C), F32)] * 4,
        compiler_params=_cparams(("parallel",)),
    )(w, m, v, recv.reshape(N_DEV, R, C))


def _sum_slots(gathered, *, name):
    _, R, C = gathered.shape

    def body(r_ref, o_ref):
        g = r_ref[0]
        for s in range(1, N_DEV):
            g = g + r_ref[s]
        o_ref[...] = g

    return pl.pallas_call(body, name=name, out_shape=jax.ShapeDtypeStruct((R, C), F32))(gathered)


def _adamw_small(w, g, m, v, *, name):
    def body(w_ref, g_ref, m_ref, v_ref, d_ref, m2_ref, v2_ref):
        d_ref[...], m2_ref[...], v2_ref[...] = _adamw_math(w_ref[...], g_ref[...], m_ref[...], v_ref[...])

    return pl.pallas_call(body, name=name, out_shape=[jax.ShapeDtypeStruct(w.shape, F32)] * 3)(w, g, m, v)


def _pack(arrs):
    flat = []
    for a in arrs:
        a = a.reshape(-1).astype(F32)
        flat.append(jnp.pad(a, (0, (-a.shape[0]) % LANES)))
    out = jnp.concatenate(flat)
    out = jnp.pad(out, (0, (-out.shape[0]) % (8 * LANES)))
    return out.reshape(-1, LANES)


def _unpack(packed, shapes):
    flat = packed.reshape(-1)
    out, pos = [], 0
    for s in shapes:
        size = math.prod(s)
        out.append(flat[pos:pos + size].reshape(s))
        pos += size + (-size) % LANES
    return out


BIG = ("w_in", "conv_pw_w", "w_out", "w_up", "w_down")
SMALL_SHARDED = ("conv_dw_w", "gdn_conv_w", "ffn_conv_w")
SMALL_REPLICATED = ("mix_norm_g", "conv_dw_b", "conv_ln_g", "conv_ln_b", "conv_pw_b", "gdn_a_log", "gdn_dt_bias",
                    "gdn_norm_g", "ffn_norm_g", "ffn_conv_b")
WEIGHTS = ("mix_norm_g", "w_in", "conv_dw_w", "conv_dw_b", "conv_ln_g", "conv_ln_b", "conv_pw_w", "conv_pw_b",
           "gdn_conv_w", "gdn_a_log", "gdn_dt_bias", "gdn_norm_g", "w_out", "ffn_norm_g", "w_up", "ffn_conv_w",
           "ffn_conv_b", "w_down", "final_norm_g")


def _train_step(x, target, w, m, v):
    L = w["w_in"].shape[0]
    D = x.shape[-1]
    xi, yi, ci = _mesh_pos()
    me = _dev_index(xi, yi, ci)

    small_full = {}
    for n in SMALL_SHARDED:
        g_ = _sc_gather(w[n], name=f"gather_{n}")
        small_full[n] = jnp.moveaxis(g_, 0, 2).reshape(L, g_.shape[2], N_DEV * g_.shape[3])
    gathered = {}

    def launch(l, after=None):
        for n in BIG:
            src = w[n][l].astype(BF16)
            if after is not None:
                src = lax.optimization_barrier((src, after))[0]
            gathered[n, l] = _sc_gather(src, name=f"gather_{n}_{l}")

    launch(0)
    Ws = []
    for l in range(L):
        W = {n: w[n][l] for n in SMALL_REPLICATED}
        W.update({n: small_full[n][l] for n in SMALL_SHARDED})
        Ws.append(W)

    def fetch(l, n, after):
        if n == "conv_pw_w" and l + 1 < L:
            launch(l + 1, after)
        g_ = lax.optimization_barrier((gathered[n, l], after))[0]
        if n == "w_up":
            return g_
        if n == "w_in":
            w_in = jnp.moveaxis(g_, 0, 1).reshape(D, -1)
            n_main = (w_in.shape[1] // LANES) * LANES
            return w_in[:, :n_main], jnp.pad(w_in[:, n_main:], ((0, 0), (0, LANES - (w_in.shape[1] - n_main))))
        return g_.reshape(g_.shape[0] * g_.shape[1], g_.shape[2])

    started = []
    res = {}
    SCATTERS_IN_FLIGHT = 2

    def consume(chain):
        n, l, recv = started.pop(0)
        res[n, l] = _adamw_big(l, w[n], m[n], v[n], recv, name=f"adamw_{n}")
        if chain is None:
            return None
        tied = lax.optimization_barrier((chain, *res[n, l]))
        res[n, l] = list(tied[1:])
        return tied[0]

    def sink(l, n, g_, chain):
        g_, chain = lax.optimization_barrier((g_, chain))
        if len(started) >= SCATTERS_IN_FLIGHT:
            chain = consume(chain)
        if n == "w_in":
            g_main, g_ba = g_
            g_ = jnp.concatenate([g_main, g_ba[:, :w["w_in"].shape[2] * N_DEV - g_main.shape[1]]], axis=1)
            part = jnp.moveaxis(g_.reshape(D, N_DEV, -1), 1, 0)
        elif n == "w_up":
            part = g_
        else:
            part = g_.reshape(N_DEV, -1, g_.shape[1])
        started.append((n, l, _sc_scatter(part, name=f"scatter_{n}_{l}")))
        return chain

    loss, grad_x, G, d_final = _local_step(x, target, Ws, w["final_norm_g"], fetch, sink)
    loss = lax.psum(loss, ("x", "y", "c"))

    out = {k: {} for k in ("grad", "delta", "new_m", "new_v")}
    while started:
        consume(None)
    for n in BIG:
        for j, k in enumerate(("grad", "delta", "new_m", "new_v")):
            out[k][n] = jnp.stack([res[n, l][j] for l in range(L)])

    small_names = [n for n in WEIGHTS if n not in BIG]
    partial = []
    for n in small_names:
        if n == "final_norm_g":
            partial.append(d_final)
        else:
            partial.append(jnp.stack([G[l][n].reshape(Ws[l][n].shape) for l in range(L)]))
    shapes = [p_.shape for p_ in partial]
    gathered = _all_gather([_pack(partial)], name="all_gather_small_grads")[0]
    full = dict(zip(small_names, _unpack(_sum_slots(gathered, name="sum_small_grads"), shapes)))
    for n in SMALL_SHARDED:
        width = w[n].shape[-1]
        full[n] = lax.dynamic_slice_in_dim(full[n], me * width, width, axis=2)
    loc_shapes = [w[n].shape for n in small_names]
    g_pack = _pack([full[n] for n in small_names])
    res = _adamw_small(_pack([w[n] for n in small_names]), g_pack, _pack([m[n] for n in small_names]),
                       _pack([v[n] for n in small_names]), name="adamw_small")
    for k, packed in zip(("grad", "delta", "new_m", "new_v"), (g_pack,) + tuple(res)):
        out[k].update(dict(zip(small_names, _unpack(packed, loc_shapes))))
    return loss, grad_x, out


def kernel(x, mix_norm_g, w_in, conv_dw_w, conv_dw_b, conv_ln_g, conv_ln_b, conv_pw_w, conv_pw_b, gdn_conv_w, gdn_a_log, gdn_dt_bias, gdn_norm_g, w_out, ffn_norm_g, w_up, ffn_conv_w, ffn_conv_b, w_down, final_norm_g, loss_target, m_mix_norm_g, m_w_in, m_conv_dw_w, m_conv_dw_b, m_conv_ln_g, m_conv_ln_b, m_conv_pw_w, m_conv_pw_b, m_gdn_conv_w, m_gdn_a_log, m_gdn_dt_bias, m_gdn_norm_g, m_w_out, m_ffn_norm_g, m_w_up, m_ffn_conv_w, m_ffn_conv_b, m_w_down, m_final_norm_g, v_mix_norm_g, v_w_in, v_conv_dw_w, v_conv_dw_b, v_conv_ln_g, v_conv_ln_b, v_conv_pw_w, v_conv_pw_b, v_gdn_conv_w, v_gdn_a_log, v_gdn_dt_bias, v_gdn_norm_g, v_w_out, v_ffn_norm_g, v_w_up, v_ffn_conv_w, v_ffn_conv_b, v_w_down, v_final_norm_g):
    w = dict(zip(WEIGHTS, (mix_norm_g, w_in, conv_dw_w, conv_dw_b, conv_ln_g, conv_ln_b, conv_pw_w, conv_pw_b, gdn_conv_w,
                           gdn_a_log, gdn_dt_bias, gdn_norm_g, w_out, ffn_norm_g, w_up, ffn_conv_w, ffn_conv_b, w_down,
                           final_norm_g)))
    m = dict(zip(WEIGHTS, (m_mix_norm_g, m_w_in, m_conv_dw_w, m_conv_dw_b, m_conv_ln_g, m_conv_ln_b, m_conv_pw_w,
                           m_conv_pw_b, m_gdn_conv_w, m_gdn_a_log, m_gdn_dt_bias, m_gdn_norm_g, m_w_out, m_ffn_norm_g,
                           m_w_up, m_ffn_conv_w, m_ffn_conv_b, m_w_down, m_final_norm_g)))
    v = dict(zip(WEIGHTS, (v_mix_norm_g, v_w_in, v_conv_dw_w, v_conv_dw_b, v_conv_ln_g, v_conv_ln_b, v_conv_pw_w,
                           v_conv_pw_b, v_gdn_conv_w, v_gdn_a_log, v_gdn_dt_bias, v_gdn_norm_g, v_w_out, v_ffn_norm_g,
                           v_w_up, v_ffn_conv_w, v_ffn_conv_b, v_w_down, v_final_norm_g)))
    loss, grad_x, out = _train_step(x, loss_target, w, m, v)
    return (loss, grad_x, *[out["grad"][n] for n in WEIGHTS], *[out["delta"][n] for n in WEIGHTS],
            *[out["new_m"][n] for n in WEIGHTS], *[out["new_v"][n] for n in WEIGHTS])
```

```python
import functools
import math

import jax
import jax.numpy as jnp
from jax import lax
from jax.experimental import pallas as pl
from jax.experimental.pallas import tpu as pltpu
from jax.experimental.pallas import tpu_sc as plsc

F32 = jnp.float32
BF16 = jnp.bfloat16
HIGHEST = lax.Precision.HIGHEST
MESH = pl.DeviceIdType.MESH

EPS = 1e-6
LANES = 128
CHUNK = 128
CONV_K = 31
SHORT_CONV_K = 4
FFN_CONV_K = 3
N_DEV = 8
VMEM_LIMIT = 56 * 1024 * 1024

ADAM_LR = 0.001
ADAM_B1 = 0.9
ADAM_B2 = 0.999
ADAM_EPS = 1e-08
ADAM_WD = 0.01
ADAM_STEP = 10


def _cparams(sem):
    return pltpu.CompilerParams(dimension_semantics=sem, vmem_limit_bytes=VMEM_LIMIT)


def _sig(x):
    return 1.0 / (1.0 + jnp.exp(-x))


def _silu(x):
    return x * _sig(x)


def _dsilu(x):
    s = _sig(x)
    return s * (1.0 + x * (1.0 - s))


def _softplus(x):
    return jnp.maximum(x, 0.0) + jnp.log1p(jnp.exp(-jnp.abs(x)))


def _dot(a, b, precision=None):
    return jnp.dot(a, b, preferred_element_type=F32, precision=precision)


def _dot_nt(a, b):
    return lax.dot_general(a, b, (((1,), (1,)), ((), ())), preferred_element_type=F32)


def _dot_tn(a, b):
    return lax.dot_general(a, b, (((0,), (0,)), ((), ())), preferred_element_type=F32)


def _bf(x):
    return x.astype(BF16)


_NN = (((1,), (0,)), ((), ()))
_TN = (((0,), (0,)), ((), ()))


def _split2(x):
    hi = _bf(x)
    return hi, _bf(x - hi.astype(F32))


def _dot_x3(a, b, dn=_NN):
    ah, al = _split2(a)
    bh, bl = _split2(b)
    f = lambda p, q: lax.dot_general(p, q, dn, preferred_element_type=F32)
    return f(ah, bh) + (f(al, bh) + f(ah, bl))


def _dot_mask(mask, x, dn=_NN):
    mb = _bf(mask)
    hi, lo = _split2(x)
    lo2 = _bf(x - hi.astype(F32) - lo.astype(F32))
    f = lambda q: lax.dot_general(mb, q, dn, preferred_element_type=F32)
    return f(hi) + (f(lo) + f(lo2))


def _shift_down(u, s):
    if s == 0:
        return u
    row = lax.broadcasted_iota(jnp.int32, u.shape, 0)
    return jnp.where(row >= s, pltpu.roll(u, s, 0), 0.0)


def _shift_up(u, s):
    if s == 0:
        return u
    n = u.shape[0]
    row = lax.broadcasted_iota(jnp.int32, u.shape, 0)
    return jnp.where(row < n - s, pltpu.roll(u, n - s, 0), 0.0)


def _conv_fwd(u, w_ref, K):
    acc = None
    for k in range(K):
        term = w_ref[k:k + 1, :] * _shift_down(u, K - 1 - k)
        acc = term if acc is None else acc + term
    return acc


def _conv_bwd_in(do, w_ref, K):
    acc = None
    for k in range(K):
        term = w_ref[k:k + 1, :] * _shift_up(do, K - 1 - k)
        acc = term if acc is None else acc + term
    return acc


def _conv_bwd_w(do, u, dw_ref, K, first):
    for k in range(K):
        row = jnp.sum(do * _shift_down(u, K - 1 - k), axis=0, keepdims=True)
        _acc_row(dw_ref, k, row, first)


def _acc_row(ref, k, row, first):
    @pl.when(first)
    def _():
        ref[k:k + 1, :] = row

    @pl.when(jnp.logical_not(first))
    def _():
        ref[k:k + 1, :] += row


def _logical(arr):
    if arr.ndim == 2:
        return arr.shape
    return (arr.shape[1], arr.shape[0] * arr.shape[2])


def _tile(dim, pref, *col_widths):
    if dim % LANES:
        assert not col_widths
        return dim
    t = (min(pref, dim) // LANES) * LANES
    while t > LANES and (dim % t or any(c % t for c in col_widths)):
        t -= LANES
    assert dim % t == 0 and all(c % t == 0 for c in col_widths), (dim, pref, col_widths)
    return t


def _spec(shape, rt, ct, rfn, cfn):
    if len(shape) == 2:
        return pl.BlockSpec((rt, ct), lambda i, j, k: (rfn(i, j, k), cfn(i, j, k)))
    per = shape[2] // ct
    return pl.BlockSpec((None, rt, ct),
                        lambda i, j, k: (cfn(i, j, k) // per, rfn(i, j, k), cfn(i, j, k) % per))


def _mm(a, b, *, name, ta=False, tb=False, out_dtype=F32, out_blocks=None, bias=None, res=None,
        tm=1024, tn=1024, tk=2048):
    ra, ca = _logical(a)
    rb, cb = _logical(b)
    M, K = (ca, ra) if ta else (ra, ca)
    N, K2 = (rb, cb) if tb else (cb, rb)
    assert K == K2, (a.shape, b.shape, ta, tb)
    out_shape = (M, N) if out_blocks is None else (out_blocks, M, N // out_blocks)
    cw = lambda arr: [arr.shape[2]] if arr.ndim == 3 else []
    m_c = cw(a) if ta else []
    k_c = (cw(a) if not ta else []) + (cw(b) if tb else [])
    n_c = (cw(b) if not tb else []) + ([out_shape[2]] if out_blocks else []) + (cw(res) if res is not None else [])
    tm, tn, tk = _tile(M, tm, *m_c), _tile(N, tn, *n_c), _tile(K, tk, *k_c)
    nk = K // tk
    im, jn, kk = (lambda i, j, k: i), (lambda i, j, k: j), (lambda i, j, k: k)
    in_specs = [
        _spec(a.shape, tk, tm, kk, im) if ta else _spec(a.shape, tm, tk, im, kk),
        _spec(b.shape, tn, tk, jn, kk) if tb else _spec(b.shape, tk, tn, kk, jn),
    ]
    args = [a, b]
    if bias is not None:
        in_specs.append(pl.BlockSpec((1, tn), lambda i, j, k: (0, j)))
        args.append(bias.reshape(1, N).astype(F32))
    if res is not None:
        in_specs.append(_spec(res.shape, tm, tn, im, jn))
        args.append(res)
    dn = (((0 if ta else 1,), (1 if tb else 0,)), ((), ()))

    def body(*refs):
        a_ref, b_ref = refs[0], refs[1]
        pos = 2
        bias_ref = res_ref = None
        if bias is not None:
            bias_ref = refs[pos]
            pos += 1
        if res is not None:
            res_ref = refs[pos]
            pos += 1
        o_ref = refs[pos]
        k = pl.program_id(2)
        part = lax.dot_general(_bf(a_ref[...]), _bf(b_ref[...]), dn, preferred_element_type=F32)

        def finish(r):
            if bias_ref is not None:
                r = r + bias_ref[...]
            if res_ref is not None:
                r = r + res_ref[...].astype(F32)
            o_ref[...] = r.astype(out_dtype)

        if nk == 1:
            finish(part)
            return
        acc_ref = refs[pos + 1]

        @pl.when(k == 0)
        def _():
            acc_ref[...] = part

        @pl.when((k > 0) & (k < nk - 1))
        def _():
            acc_ref[...] += part

        @pl.when(k == nk - 1)
        def _():
            finish(acc_ref[...] + part)

    return pl.pallas_call(
        body, name=name,
        grid=(M // tm, N // tn, nk),
        in_specs=in_specs,
        out_specs=_spec(out_shape, tm, tn, im, jn),
        out_shape=jax.ShapeDtypeStruct(out_shape, out_dtype),
        scratch_shapes=[pltpu.VMEM((tm, tn), F32)] if nk > 1 else [],
        compiler_params=_cparams(("parallel", "parallel", "arbitrary")),
    )(*args)


def _rms_fwd(x, g, *, name, tr=512):
    T, D = x.shape
    tr = min(tr, T)

    def body(x_ref, g_ref, h_ref):
        xv = x_ref[...]
        r = lax.rsqrt(jnp.mean(xv * xv, axis=-1, keepdims=True) + EPS)
        h_ref[...] = (xv * r * g_ref[...]).astype(BF16)

    return pl.pallas_call(
        body, name=name, grid=(T // tr,),
        in_specs=[pl.BlockSpec((tr, D), lambda i: (i, 0)), pl.BlockSpec((1, D), lambda i: (0, 0))],
        out_specs=pl.BlockSpec((tr, D), lambda i: (i, 0)),
        out_shape=jax.ShapeDtypeStruct((T, D), BF16),
        compiler_params=_cparams(("parallel",)),
    )(x, g.reshape(1, D))


def _rms_bwd(x, g, dh, dres, *, name, tr=512):
    T, D = x.shape
    tr = min(tr, T)

    def body(x_ref, g_ref, dh_ref, dres_ref, dx_ref, dxb_ref, dg_ref):
        i = pl.program_id(0)
        xv = x_ref[...]
        dy = dh_ref[...].astype(F32)
        r = lax.rsqrt(jnp.mean(xv * xv, axis=-1, keepdims=True) + EPS)
        dyg = dy * g_ref[...]
        dot = jnp.mean(dyg * xv, axis=-1, keepdims=True)
        dx = dres_ref[...] + r * dyg - xv * (r * r * r) * dot
        dx_ref[...] = dx
        dxb_ref[...] = dx.astype(BF16)
        part = jnp.sum(dy * xv * r, axis=0, keepdims=True)
        _acc_row(dg_ref, 0, part, i == 0)

    row = pl.BlockSpec((tr, D), lambda i: (i, 0))
    vec = pl.BlockSpec((1, D), lambda i: (0, 0))
    return pl.pallas_call(
        body, name=name, grid=(T // tr,),
        in_specs=[row, vec, row, row],
        out_specs=[row, row, vec],
        out_shape=[jax.ShapeDtypeStruct((T, D), F32), jax.ShapeDtypeStruct((T, D), BF16),
                   jax.ShapeDtypeStruct((1, D), F32)],
        compiler_params=_cparams(("arbitrary",)),
    )(x, g.reshape(1, D), dh, dres)


def _loss_head(x, g, target, *, name, tr=512):
    T, D = x.shape
    tr = min(tr, T)

    def body(x_ref, g_ref, t_ref, loss_ref, dx_ref, dxb_ref, dg_ref):
        i = pl.program_id(0)
        xv = x_ref[...]
        gv = g_ref[...]
        r = lax.rsqrt(jnp.mean(xv * xv, axis=-1, keepdims=True) + EPS)
        y = xv * r * gv
        err = y - t_ref[...]
        lpart = 0.5 * jnp.sum(jnp.mean(err * err, axis=-1, keepdims=True), axis=0, keepdims=True)
        dy = err * (1.0 / D)
        dyg = dy * gv
        dot = jnp.mean(dyg * xv, axis=-1, keepdims=True)
        dx = r * dyg - xv * (r * r * r) * dot
        dx_ref[...] = dx
        dxb_ref[...] = dx.astype(BF16)
        _acc_row(dg_ref, 0, jnp.sum(dy * xv * r, axis=0, keepdims=True), i == 0)
        _acc_row(loss_ref, 0, jnp.broadcast_to(lpart, (1, LANES)), i == 0)

    row = pl.BlockSpec((tr, D), lambda i: (i, 0))
    return pl.pallas_call(
        body, name=name, grid=(T // tr,),
        in_specs=[row, pl.BlockSpec((1, D), lambda i: (0, 0)), row],
        out_specs=[pl.BlockSpec((1, LANES), lambda i: (0, 0)), row, row, pl.BlockSpec((1, D), lambda i: (0, 0))],
        out_shape=[jax.ShapeDtypeStruct((1, LANES), F32), jax.ShapeDtypeStruct((T, D), F32),
                   jax.ShapeDtypeStruct((T, D), BF16), jax.ShapeDtypeStruct((1, D), F32)],
        compiler_params=_cparams(("arbitrary",)),
    )(x, g.reshape(1, D), target)


def _conf_chain(av, ag, w_ref, b_ref, lg_ref, lb_ref):
    sg = _sig(ag)
    u0 = av * sg
    u1 = _conv_fwd(u0, w_ref, CONV_K) + b_ref[...]
    mu = jnp.mean(u1, axis=-1, keepdims=True)
    xc = u1 - mu
    r = lax.rsqrt(jnp.mean(xc * xc, axis=-1, keepdims=True) + EPS)
    n = xc * r
    u2 = n * lg_ref[...] + lb_ref[...]
    return sg, u0, r, n, u2


def _conf_fwd(p, dw_w, dw_b, ln_g, ln_b, *, Bl, S, CC, name):
    G = CC // LANES

    def body(av_ref, ag_ref, w_ref, b_ref, lg_ref, lb_ref, o_ref):
        _, _, _, _, u2 = _conf_chain(av_ref[...], ag_ref[...], w_ref, b_ref, lg_ref, lb_ref)
        o_ref[...] = _silu(u2).astype(BF16)

    vec = pl.BlockSpec((1, LANES), lambda b, j: (0, j))
    return pl.pallas_call(
        body, name=name, grid=(Bl, G),
        in_specs=[pl.BlockSpec((S, LANES), lambda b, j: (b, j)),
                  pl.BlockSpec((S, LANES), lambda b, j: (b, G + j)),
                  pl.BlockSpec((CONV_K, LANES), lambda b, j: (0, j)), vec, vec, vec],
        out_specs=pl.BlockSpec((S, LANES), lambda b, j: (b, j)),
        out_shape=jax.ShapeDtypeStruct((Bl * S, CC), BF16),
        compiler_params=_cparams(("parallel", "parallel")),
    )(p, p, dw_w, dw_b.reshape(1, CC), ln_g.reshape(1, CC), ln_b.reshape(1, CC))


def _conf_bwd(p, dw_w, dw_b, ln_g, ln_b, du3, *, Bl, S, CC, name):
    G = CC // LANES

    def body(av_ref, ag_ref, w_ref, b_ref, lg_ref, lb_ref, du3_ref,
             dav_ref, dag_ref, dw_ref, db_ref, dlg_ref, dlb_ref):
        first = pl.program_id(1) == 0
        av = av_ref[...]
        sg, u0, r, n, u2 = _conf_chain(av, ag_ref[...], w_ref, b_ref, lg_ref, lb_ref)
        du2 = du3_ref[...] * _dsilu(u2)
        _acc_row(dlg_ref, 0, jnp.sum(du2 * n, axis=0, keepdims=True), first)
        _acc_row(dlb_ref, 0, jnp.sum(du2, axis=0, keepdims=True), first)
        dn = du2 * lg_ref[...]
        du1 = r * (dn - jnp.mean(dn, axis=-1, keepdims=True) - n * jnp.mean(dn * n, axis=-1, keepdims=True))
        _acc_row(db_ref, 0, jnp.sum(du1, axis=0, keepdims=True), first)
        _conv_bwd_w(du1, u0, dw_ref, CONV_K, first)
        du0 = _conv_bwd_in(du1, w_ref, CONV_K)
        dav_ref[...] = (du0 * sg).astype(BF16)
        dag_ref[...] = (du0 * av * sg * (1.0 - sg)).astype(BF16)

    vec = pl.BlockSpec((1, LANES), lambda j, b: (0, j))
    seq = pl.BlockSpec((S, LANES), lambda j, b: (b, j))
    return pl.pallas_call(
        body, name=name, grid=(G, Bl),
        in_specs=[seq, pl.BlockSpec((S, LANES), lambda j, b: (b, G + j)),
                  pl.BlockSpec((CONV_K, LANES), lambda j, b: (0, j)), vec, vec, vec, seq],
        out_specs=[seq, seq, pl.BlockSpec((CONV_K, LANES), lambda j, b: (0, j)), vec, vec, vec],
        out_shape=[jax.ShapeDtypeStruct((Bl * S, CC), BF16), jax.ShapeDtypeStruct((Bl * S, CC), BF16),
                   jax.ShapeDtypeStruct((CONV_K, CC), F32), jax.ShapeDtypeStruct((1, CC), F32),
                   jax.ShapeDtypeStruct((1, CC), F32), jax.ShapeDtypeStruct((1, CC), F32)],
        compiler_params=_cparams(("parallel", "arbitrary")),
    )(p, p, dw_w, dw_b.reshape(1, CC), ln_g.reshape(1, CC), ln_b.reshape(1, CC), du3)


def _gdn_pre_fwd(p, conv_w, *, Bl, S, CC, KW, VW, name):
    NQK = 2 * KW // LANES
    NB = NQK + VW // LANES
    off = 2 * CC // LANES

    def body(x_ref, w_ref, o_ref):
        j = pl.program_id(1)
        s = _silu(_conv_fwd(x_ref[...], w_ref, SHORT_CONV_K))
        r = lax.rsqrt(jnp.sum(s * s, axis=-1, keepdims=True) + EPS)
        o_ref[...] = jnp.where(j < NQK, s * r, s)

    return pl.pallas_call(
        body, name=name, grid=(Bl, NB),
        in_specs=[pl.BlockSpec((S, LANES), lambda b, j: (b, off + j)),
                  pl.BlockSpec((SHORT_CONV_K, LANES), lambda b, j: (0, j))],
        out_specs=pl.BlockSpec((S, LANES), lambda b, j: (b, j)),
        out_shape=jax.ShapeDtypeStruct((Bl * S, NB * LANES), F32),
        compiler_params=_cparams(("parallel", "parallel")),
    )(p, conv_w)


def _gdn_pre_bwd(p, conv_w, dq, dk, dv, *, Bl, S, CC, KW, VW, name):
    HQ = KW // LANES
    H = VW // LANES
    NQK = 2 * HQ
    NB = NQK + H
    off = 2 * CC // LANES

    def body(x_ref, w_ref, dq_ref, dk_ref, dv_ref, dx_ref, dw_ref):
        j = pl.program_id(0)
        first = pl.program_id(1) == 0
        xv = x_ref[...]
        c = _conv_fwd(xv, w_ref, SHORT_CONV_K)
        s = _silu(c)
        r = lax.rsqrt(jnp.sum(s * s, axis=-1, keepdims=True) + EPS)
        dy = jnp.where(j < HQ, dq_ref[...], jnp.where(j < NQK, dk_ref[...], dv_ref[...]))
        ds_norm = r * dy - s * (r * r * r) * jnp.sum(s * dy, axis=-1, keepdims=True)
        ds = jnp.where(j < NQK, ds_norm, dy)
        dc = ds * _dsilu(c)
        _conv_bwd_w(dc, xv, dw_ref, SHORT_CONV_K, first)
        dx_ref[...] = _conv_bwd_in(dc, w_ref, SHORT_CONV_K).astype(BF16)

    return pl.pallas_call(
        body, name=name, grid=(NB, Bl),
        in_specs=[pl.BlockSpec((S, LANES), lambda j, b: (b, off + j)),
                  pl.BlockSpec((SHORT_CONV_K, LANES), lambda j, b: (0, j)),
                  pl.BlockSpec((S, LANES), lambda j, b: (b, jnp.minimum(j, HQ - 1))),
                  pl.BlockSpec((S, LANES), lambda j, b: (b, jnp.clip(j - HQ, 0, HQ - 1))),
                  pl.BlockSpec((S, LANES), lambda j, b: (b, jnp.clip(j - NQK, 0, H - 1)))],
        out_specs=[pl.BlockSpec((S, LANES), lambda j, b: (b, j)),
                   pl.BlockSpec((SHORT_CONV_K, LANES), lambda j, b: (0, j))],
        out_shape=[jax.ShapeDtypeStruct((Bl * S, NB * LANES), BF16),
                   jax.ShapeDtypeStruct((SHORT_CONV_K, NB * LANES), F32)],
        compiler_params=_cparams(("parallel", "arbitrary")),
    )(p, conv_w, dq, dk, dv)


def _lane_pick(h):
    row = lax.broadcasted_iota(jnp.int32, (LANES, LANES), 0)
    return (row == h).astype(F32)


def _gdn_gate_fwd(pba, a_log, dt_bias, *, Bl, S, H, name):
    def body(alog_ref, dtb_ref, x_ref, g_ref, beta_ref):
        xv = x_ref[...]
        for h in range(H):
            b_raw = _dot(xv, _lane_pick(h), HIGHEST)
            a_raw = _dot(xv, _lane_pick(H + h), HIGHEST)
            beta_ref[h] = _sig(b_raw)
            ea = jnp.exp(jnp.zeros((1, LANES), F32) + alog_ref[h])
            g_ref[h] = -ea * _softplus(a_raw + dtb_ref[h])

    smem = pl.BlockSpec(memory_space=pltpu.SMEM)
    rep = pl.BlockSpec((H, S, LANES), lambda b: (0, b, 0))
    return pl.pallas_call(
        body, name=name, grid=(Bl,),
        in_specs=[smem, smem, pl.BlockSpec((S, LANES), lambda b: (b, 0))],
        out_specs=[rep, rep],
        out_shape=[jax.ShapeDtypeStruct((H, Bl * S, LANES), F32)] * 2,
        compiler_params=_cparams(("parallel",)),
    )(a_log, dt_bias, pba)


def _gdn_gate_bwd(pba, a_log, dt_bias, dg, dbeta, *, Bl, S, H, name):
    HP = 8 * ((H + 7) // 8)

    def body(alog_ref, dtb_ref, x_ref, dg_ref, dbeta_ref, dx_ref, dalog_ref, ddtb_ref):
        first = pl.program_id(0) == 0
        xv = x_ref[...]
        lane = lax.broadcasted_iota(jnp.int32, (S, LANES), 1)
        acc = jnp.zeros((S, LANES), F32)

        @pl.when(first)
        def _():
            dalog_ref[...] = jnp.zeros_like(dalog_ref)
            ddtb_ref[...] = jnp.zeros_like(ddtb_ref)

        for h in range(H):
            b_raw = _dot(xv, _lane_pick(h), HIGHEST)
            a_raw = _dot(xv, _lane_pick(H + h), HIGHEST)
            beta = _sig(b_raw)
            db_raw = dbeta_ref[h] * beta * (1.0 - beta)
            z = a_raw + dtb_ref[h]
            ea = jnp.exp(jnp.zeros((1, LANES), F32) + alog_ref[h])
            dgv = dg_ref[h]
            da_raw = dgv * (-ea) * _sig(z)
            g = -ea * _softplus(z)
            dalog_ref[h:h + 1, :] += jnp.sum(dgv * g, axis=0, keepdims=True)
            ddtb_ref[h:h + 1, :] += jnp.sum(da_raw, axis=0, keepdims=True)
            acc = acc + jnp.where(lane == h, db_raw, 0.0) + jnp.where(lane == H + h, da_raw, 0.0)
        dx_ref[...] = acc.astype(BF16)

    smem = pl.BlockSpec(memory_space=pltpu.SMEM)
    rep = pl.BlockSpec((H, S, LANES), lambda b: (0, b, 0))
    small = pl.BlockSpec((HP, LANES), lambda b: (0, 0))
    return pl.pallas_call(
        body, name=name, grid=(Bl,),
        in_specs=[smem, smem, pl.BlockSpec((S, LANES), lambda b: (b, 0)), rep, rep],
        out_specs=[pl.BlockSpec((S, LANES), lambda b: (b, 0)), small, small],
        out_shape=[jax.ShapeDtypeStruct((Bl * S, LANES), BF16),
                   jax.ShapeDtypeStruct((HP, LANES), F32), jax.ShapeDtypeStruct((HP, LANES), F32)],
        compiler_params=_cparams(("arbitrary",)),
    )(a_log, dt_bias, pba, dg, dbeta)


def _tri_masks():
    ri = lax.broadcasted_iota(jnp.int32, (CHUNK, CHUNK), 0)
    ci = lax.broadcasted_iota(jnp.int32, (CHUNK, CHUNK), 1)
    return ri >= ci, ri > ci, ri == CHUNK - 1


def _tri_inv(L):
    ri = lax.broadcasted_iota(jnp.int32, (CHUNK, CHUNK), 0)
    ci = lax.broadcasted_iota(jnp.int32, (CHUNK, CHUNK), 1)
    T = jnp.where(ri == ci, 1.0, 0.0) - jnp.where((ri >> 1) == (ci >> 1), L, 0.0)
    for lv in range(2, int(math.log2(CHUNK)) + 1):
        O = jnp.where(((ri >> lv) == (ci >> lv)) & ((ri >> (lv - 1)) != (ci >> (lv - 1))), L, 0.0)
        T = T - _dot_x3(T, _dot_x3(O, T))
    return T


def _chunk_local(q, k, v, beta, g):
    ge, gt, last = _tri_masks()
    gam = _dot_mask(ge, g)
    D = jnp.where(ge, jnp.exp(jnp.where(ge, gam - gam.T, 0.0)), 0.0)
    kb = k * beta
    vb = v * beta
    M = _dot_nt(_bf(kb), _bf(k))
    L = jnp.where(gt, M * D, 0.0)
    eg = jnp.exp(gam)
    kbg = kb * eg
    P = _dot_nt(_bf(q), _bf(k))
    QK = jnp.where(ge, P * D, 0.0)
    gl = jnp.sum(jnp.where(last, gam, 0.0), axis=0, keepdims=True)
    el = jnp.exp(gl - gam)
    return dict(ge=ge, gt=gt, last=last, gam=gam, D=D, kb=kb, vb=vb, L=L, eg=eg, kbg=kbg, QK=QK, gl=gl,
                el=el, kd=k * el, qg=q * eg)


def _chunk_fwd(q, k, v, beta, g, S):
    c = _chunk_local(q, k, v, beta, g)
    T = _tri_inv(c["L"])
    u = _dot_x3(T, c["vb"])
    w = _dot_x3(T, c["kbg"])
    Sb = _bf(S)
    vn = u - _dot(_bf(w), Sb)
    o = _dot(_bf(c["qg"]), Sb) + _dot(_bf(c["QK"]), _bf(vn))
    S2 = S * jnp.exp(c["gl"]) + _dot_tn(_bf(c["kd"]), _bf(vn))
    return o, S2, T


def _rowsum(x):
    return jnp.sum(x, axis=-1, keepdims=True)


def _chunk_bwd(q, k, v, beta, g, S, T, do, dS2):
    c = _chunk_local(q, k, v, beta, g)
    ge, gt, last = c["ge"], c["gt"], c["last"]
    u = _dot_x3(T, c["vb"])
    w = _dot_x3(T, c["kbg"])
    Sb = _bf(S)
    vn = u - _dot(_bf(w), Sb)
    dob, vnb, dS2b = _bf(do), _bf(vn), _bf(dS2)
    e_last = jnp.exp(c["gl"])
    dqg = _dot_nt(dob, Sb)
    dS = _dot_tn(_bf(c["qg"]), dob)
    dQK = jnp.where(ge, _dot_nt(dob, vnb), 0.0)
    dvn = _dot_tn(_bf(c["QK"]), dob)
    dS = dS + dS2 * e_last
    de_last = jnp.sum(jnp.sum(dS2 * S, axis=0, keepdims=True), axis=1, keepdims=True)
    dkd = _dot_nt(vnb, dS2b)
    dvn = dvn + _dot(_bf(c["kd"]), dS2b)
    dvnb = _bf(dvn)
    dw = -_dot_nt(dvnb, Sb)
    dS = dS - _dot_tn(_bf(w), dvnb)
    dvb = _dot_x3(T, dvn, _TN)
    dkbg = _dot_x3(T, dw, _TN)
    dA = -(_dot_nt(_bf(dvb), _bf(u)) + _dot_nt(_bf(dkbg), _bf(w)))
    dL = jnp.where(gt, dA, 0.0)
    dM = dL * c["D"]
    dP = dQK * c["D"]
    E = dL * c["L"] + dQK * c["QK"]
    kbf = _bf(k)
    dkb = _dot(_bf(dM), kbf) + dkbg * c["eg"]
    dk = _dot_tn(_bf(dM), _bf(c["kb"])) + _dot_tn(_bf(dP), _bf(q)) + dkd * c["el"] + dkb * beta
    dq = _dot(_bf(dP), kbf) + dqg * c["eg"]
    s_kd = _rowsum(dkd * c["kd"])
    dgam = (_rowsum(E) - _rowsum(E.T) + _rowsum(dqg * c["qg"]) - s_kd + _rowsum(dkbg * c["kbg"]))
    dgl = jnp.sum(s_kd, axis=0, keepdims=True) + de_last * e_last
    dgam_rep = jnp.broadcast_to(dgam, (CHUNK, LANES)) + jnp.where(last, jnp.broadcast_to(dgl, (CHUNK, LANES)), 0.0)
    dg_rep = _dot_mask(ge, dgam_rep, _TN)
    dbeta = _rowsum(dkb * k) + _rowsum(dvb * v)
    dv = dvb * beta
    return dq, dk, dv, jnp.broadcast_to(dbeta, (CHUNK, LANES)), dg_rep, dS


def _gdn_core_fwd(qkv, g, beta, *, Bl, S, KW, VW, name):
    HQ = KW // LANES
    H = VW // LANES
    NC = S // CHUNK
    scale = float(LANES) ** -0.5

    def body(q_ref, k_ref, v_ref, g_ref, beta_ref, o_ref, st_ref, t_ref):
        def step(n, states):
            rows = pl.ds(pl.multiple_of(n * CHUNK, CHUNK), CHUNK)
            q = q_ref[rows, :] * scale
            k = k_ref[rows, :]
            out = []
            for e in range(2):
                st_ref[e, n] = states[e]
                o, S2, Tb = _chunk_fwd(q, k, v_ref[rows, e * LANES:(e + 1) * LANES], beta_ref[e, rows, :],
                                       g_ref[e, rows, :], states[e])
                o_ref[rows, e * LANES:(e + 1) * LANES] = o
                t_ref[e, rows, :] = Tb
                out.append(S2)
            return tuple(out)

        z = jnp.zeros((LANES, LANES), F32)
        lax.fori_loop(0, NC, step, (z, z))

    rep = pl.BlockSpec((2, S, LANES), lambda b, h: (h, b, 0))
    return pl.pallas_call(
        body, name=name, grid=(Bl, HQ),
        in_specs=[pl.BlockSpec((S, LANES), lambda b, h: (b, h)),
                  pl.BlockSpec((S, LANES), lambda b, h: (b, HQ + h)),
                  pl.BlockSpec((S, 2 * LANES), lambda b, h: (b, HQ + h)), rep, rep],
        out_specs=[pl.BlockSpec((S, 2 * LANES), lambda b, h: (b, h)),
                   pl.BlockSpec((None, 2, NC, LANES, LANES), lambda b, h: (b, h, 0, 0, 0)), rep],
        out_shape=[jax.ShapeDtypeStruct((Bl * S, VW), F32),
                   jax.ShapeDtypeStruct((Bl, H, NC, LANES, LANES), F32),
                   jax.ShapeDtypeStruct((H, Bl * S, LANES), F32)],
        compiler_params=_cparams(("parallel", "parallel")),
    )(qkv, qkv, qkv, g, beta)


def _gdn_core_bwd(qkv, g, beta, states, tinv, do, *, Bl, S, KW, VW, name):
    HQ = KW // LANES
    H = VW // LANES
    NC = S // CHUNK
    scale = float(LANES) ** -0.5

    def body(q_ref, k_ref, v_ref, g_ref, beta_ref, st_ref, t_ref, do_ref,
             dq_ref, dk_ref, dv_ref, dg_ref, dbeta_ref):
        def step(i, dstates):
            n = NC - 1 - i
            rows = pl.ds(pl.multiple_of(n * CHUNK, CHUNK), CHUNK)
            q = q_ref[rows, :] * scale
            k = k_ref[rows, :]
            out = []
            dq_sum = dk_sum = None
            for e in range(2):
                cols = slice(e * LANES, (e + 1) * LANES)
                dq, dk, dv, dbeta, dg, dS = _chunk_bwd(q, k, v_ref[rows, cols], beta_ref[e, rows, :],
                                                       g_ref[e, rows, :], st_ref[e, n], t_ref[e, rows, :],
                                                       do_ref[rows, cols], dstates[e])
                dv_ref[rows, cols] = dv
                dg_ref[e, rows, :] = dg
                dbeta_ref[e, rows, :] = dbeta
                dq_sum = dq if dq_sum is None else dq_sum + dq
                dk_sum = dk if dk_sum is None else dk_sum + dk
                out.append(dS)
            dq_ref[rows, :] = dq_sum * scale
            dk_ref[rows, :] = dk_sum
            return tuple(out)

        z = jnp.zeros((LANES, LANES), F32)
        lax.fori_loop(0, NC, step, (z, z))

    rep = pl.BlockSpec((2, S, LANES), lambda b, h: (h, b, 0))
    seq = pl.BlockSpec((S, LANES), lambda b, h: (b, h))
    seq2 = pl.BlockSpec((S, 2 * LANES), lambda b, h: (b, h))
    return pl.pallas_call(
        body, name=name, grid=(Bl, HQ),
        in_specs=[seq, pl.BlockSpec((S, LANES), lambda b, h: (b, HQ + h)),
                  pl.BlockSpec((S, 2 * LANES), lambda b, h: (b, HQ + h)), rep, rep,
                  pl.BlockSpec((None, 2, NC, LANES, LANES), lambda b, h: (b, h, 0, 0, 0)), rep, seq2],
        out_specs=[seq, seq, seq2, rep, rep],
        out_shape=[jax.ShapeDtypeStruct((Bl * S, KW), F32), jax.ShapeDtypeStruct((Bl * S, KW), F32),
                   jax.ShapeDtypeStruct((Bl * S, VW), F32),
                   jax.ShapeDtypeStruct((H, Bl * S, LANES), F32), jax.ShapeDtypeStruct((H, Bl * S, LANES), F32)],
        compiler_params=_cparams(("parallel", "parallel")),
    )(qkv, qkv, qkv, g, beta, states, tinv, do)


def _gdn_out_fwd(o, p, norm_g, out_a, *, CC, VW, name, tr=256):
    T = o.shape[0]
    tr = min(tr, T)
    H = VW // LANES
    zoff = p.shape[1] // VW - 1

    def body(o_ref, z_ref, ng_ref, a_ref, mix_ref):
        mix_ref[:, :CC] = a_ref[...]
        for h in range(H):
            cols = slice(h * LANES, (h + 1) * LANES)
            ov = o_ref[:, cols]
            r = lax.rsqrt(jnp.mean(ov * ov, axis=-1, keepdims=True) + EPS)
            mix_ref[:, CC + h * LANES:CC + (h + 1) * LANES] = (ov * r * ng_ref[...] * _silu(z_ref[:, cols])).astype(BF16)

    return pl.pallas_call(
        body, name=name, grid=(T // tr,),
        in_specs=[pl.BlockSpec((tr, VW), lambda i: (i, 0)), pl.BlockSpec((tr, VW), lambda i: (i, zoff)),
                  pl.BlockSpec((1, LANES), lambda i: (0, 0)), pl.BlockSpec((tr, CC), lambda i: (i, 0))],
        out_specs=pl.BlockSpec((tr, CC + VW), lambda i: (i, 0)),
        out_shape=jax.ShapeDtypeStruct((T, CC + VW), BF16),
        compiler_params=_cparams(("parallel",)),
    )(o, p, norm_g.reshape(1, LANES), out_a)


def _gdn_out_bwd(o, p, norm_g, dmix, *, CC, VW, name, tr=256):
    T = o.shape[0]
    tr = min(tr, T)
    H = VW // LANES
    zoff = p.shape[1] // VW - 1

    def body(o_ref, z_ref, ng_ref, dmix_ref, do_ref, dz_ref, da_ref, dng_ref, dpb_ref):
        first = pl.program_id(0) == 0
        da = dmix_ref[:, :CC]
        da_ref[...] = da.astype(BF16)
        _acc_row(dpb_ref, 0, jnp.sum(da, axis=0, keepdims=True), first)
        ng = ng_ref[...]
        dng = jnp.zeros((1, LANES), F32)
        for h in range(H):
            cols = slice(h * LANES, (h + 1) * LANES)
            ov = o_ref[:, cols]
            zv = z_ref[:, cols]
            dout = dmix_ref[:, CC + h * LANES:CC + (h + 1) * LANES]
            r = lax.rsqrt(jnp.mean(ov * ov, axis=-1, keepdims=True) + EPS)
            on = ov * r * ng
            don = dout * _silu(zv)
            dz_ref[:, cols] = (dout * on * _dsilu(zv)).astype(BF16)
            dng = dng + jnp.sum(don * ov * r, axis=0, keepdims=True)
            dong = don * ng
            do_ref[:, cols] = r * dong - ov * (r * r * r) * jnp.mean(dong * ov, axis=-1, keepdims=True)
        _acc_row(dng_ref, 0, dng, first)

    return pl.pallas_call(
        body, name=name, grid=(T // tr,),
        in_specs=[pl.BlockSpec((tr, VW), lambda i: (i, 0)), pl.BlockSpec((tr, VW), lambda i: (i, zoff)),
                  pl.BlockSpec((1, LANES), lambda i: (0, 0)), pl.BlockSpec((tr, CC + VW), lambda i: (i, 0))],
        out_specs=[pl.BlockSpec((tr, VW), lambda i: (i, 0)), pl.BlockSpec((tr, VW), lambda i: (i, 0)),
                   pl.BlockSpec((tr, CC), lambda i: (i, 0)), pl.BlockSpec((1, LANES), lambda i: (0, 0)),
                   pl.BlockSpec((1, CC), lambda i: (0, 0))],
        out_shape=[jax.ShapeDtypeStruct((T, VW), F32), jax.ShapeDtypeStruct((T, VW), BF16),
                   jax.ShapeDtypeStruct((T, CC), BF16), jax.ShapeDtypeStruct((1, LANES), F32),
                   jax.ShapeDtypeStruct((1, CC), F32)],
        compiler_params=_cparams(("arbitrary",)),
    )(o, p, norm_g.reshape(1, LANES), dmix)


FFN_CW = 256


def _ffn_act_fwd(gu, conv_w, conv_b, *, Bl, S, name):
    FF = gu.shape[2]
    cw = min(FFN_CW, FF)

    def body(g_ref, u_ref, w_ref, b_ref, a_ref):
        gc = _conv_fwd(g_ref[...], w_ref, FFN_CONV_K) + b_ref[...]
        a_ref[...] = (_silu(gc) * u_ref[...]).astype(BF16)

    return pl.pallas_call(
        body, name=name, grid=(Bl, FF // cw),
        in_specs=[pl.BlockSpec((None, S, cw), lambda b, j: (0, b, j)),
                  pl.BlockSpec((None, S, cw), lambda b, j: (1, b, j)),
                  pl.BlockSpec((FFN_CONV_K, cw), lambda b, j: (0, j)),
                  pl.BlockSpec((1, cw), lambda b, j: (0, j))],
        out_specs=pl.BlockSpec((S, cw), lambda b, j: (b, j)),
        out_shape=jax.ShapeDtypeStruct((Bl * S, FF), BF16),
        compiler_params=_cparams(("parallel", "parallel")),
    )(gu, gu, conv_w, conv_b.reshape(1, FF))


def _ffn_act_bwd(gu, conv_w, conv_b, da, *, Bl, S, name):
    FF = gu.shape[2]
    cw = min(FFN_CW, FF)

    def body(g_ref, u_ref, w_ref, b_ref, da_ref, dgu_ref, dw_ref, db_ref):
        first = pl.program_id(1) == 0
        gate = g_ref[...]
        gc = _conv_fwd(gate, w_ref, FFN_CONV_K) + b_ref[...]
        dav = da_ref[...]
        dgu_ref[1] = (dav * _silu(gc)).astype(BF16)
        dgc = dav * u_ref[...] * _dsilu(gc)
        _acc_row(db_ref, 0, jnp.sum(dgc, axis=0, keepdims=True), first)
        _conv_bwd_w(dgc, gate, dw_ref, FFN_CONV_K, first)
        dgu_ref[0] = _conv_bwd_in(dgc, w_ref, FFN_CONV_K).astype(BF16)

    return pl.pallas_call(
        body, name=name, grid=(FF // cw, Bl),
        in_specs=[pl.BlockSpec((None, S, cw), lambda j, b: (0, b, j)),
                  pl.BlockSpec((None, S, cw), lambda j, b: (1, b, j)),
                  pl.BlockSpec((FFN_CONV_K, cw), lambda j, b: (0, j)),
                  pl.BlockSpec((1, cw), lambda j, b: (0, j)),
                  pl.BlockSpec((S, cw), lambda j, b: (b, j))],
        out_specs=[pl.BlockSpec((2, S, cw), lambda j, b: (0, b, j)),
                   pl.BlockSpec((FFN_CONV_K, cw), lambda j, b: (0, j)),
                   pl.BlockSpec((1, cw), lambda j, b: (0, j))],
        out_shape=[jax.ShapeDtypeStruct((2, Bl * S, FF), BF16),
                   jax.ShapeDtypeStruct((FFN_CONV_K, FF), F32), jax.ShapeDtypeStruct((1, FF), F32)],
        compiler_params=_cparams(("parallel", "arbitrary")),
    )(gu, gu, conv_w, conv_b.reshape(1, FF), da)


def _layer_dims(W):
    CC = W["conv_pw_b"].shape[0]
    VW = W["mix_norm_g"].shape[0] - CC
    KW = (W["gdn_conv_w"].shape[1] - VW) // 2
    return CC, KW, VW


def _layer_fwd(l, x, W, Bl, S, fetch):
    CC, KW, VW = _layer_dims(W)
    H = VW // LANES
    w_in_main, w_in_ba = fetch(l, "w_in", x)
    h1 = _rms_fwd(x, W["mix_norm_g"], name="rms1_fwd")
    p = _mm(h1, w_in_main, name="mm_in")
    pba = _mm(h1, w_in_ba, name="mm_in_ba")
    u3 = _conf_fwd(p, W["conv_dw_w"], W["conv_dw_b"], W["conv_ln_g"], W["conv_ln_b"], Bl=Bl, S=S, CC=CC,
                   name="conf_fwd")
    conv_pw_w = fetch(l, "conv_pw_w", u3)
    out_a = _mm(u3, conv_pw_w, bias=W["conv_pw_b"], out_dtype=BF16, name="mm_pw")
    qkv = _gdn_pre_fwd(p, W["gdn_conv_w"], Bl=Bl, S=S, CC=CC, KW=KW, VW=VW, name="gdn_pre_fwd")
    g, beta = _gdn_gate_fwd(pba, W["gdn_a_log"], W["gdn_dt_bias"], Bl=Bl, S=S, H=H, name="gdn_gate_fwd")
    o, states, tinv = _gdn_core_fwd(qkv, g, beta, Bl=Bl, S=S, KW=KW, VW=VW, name="gdn_core_fwd")
    mix = _gdn_out_fwd(o, p, W["gdn_norm_g"], out_a, CC=CC, VW=VW, name="gdn_out_fwd")
    w_out = fetch(l, "w_out", mix)
    x1 = _mm(mix, w_out, res=x, name="mm_out")
    h2 = _rms_fwd(x1, W["ffn_norm_g"], name="rms2_fwd")
    w_up = fetch(l, "w_up", h2)
    gu = _mm(h2, w_up, out_blocks=2, tn=w_up.shape[2], name="mm_up")
    a = _ffn_act_fwd(gu, W["ffn_conv_w"], W["ffn_conv_b"], Bl=Bl, S=S, name="ffn_act_fwd")
    w_down = fetch(l, "w_down", a)
    x2 = _mm(a, w_down, res=x1, name="mm_down")
    saved = dict(x=x, h1=h1, p=p, pba=pba, u3=u3, qkv=qkv, g=g, beta=beta, o=o, states=states, tinv=tinv,
                 mix=mix, x1=x1, h2=h2, gu=gu, a=a, w_in_main=w_in_main, w_in_ba=w_in_ba, conv_pw_w=conv_pw_w,
                 w_out=w_out, w_up=w_up, w_down=w_down)
    return x2, saved


def _layer_bwd(l, dx2, dx2b, W, A, Bl, S, sink):
    CC, KW, VW = _layer_dims(W)
    H = VW // LANES
    G = {}
    upw = A["w_up"].shape[2]
    da = _mm(dx2b, A["w_down"], tb=True, tn=upw, name="mm_down_dx")
    da = sink(l, "w_down", _mm(A["a"], dx2b, ta=True, out_dtype=BF16, tm=upw, name="mm_down_dw"), da)
    dgu, G["ffn_conv_w"], G["ffn_conv_b"] = _ffn_act_bwd(A["gu"], W["ffn_conv_w"], W["ffn_conv_b"], da,
                                                         Bl=Bl, S=S, name="ffn_act_bwd")
    dh2 = _mm(dgu, A["w_up"], tb=True, tk=upw, tn=2048, name="mm_up_dx")
    dh2 = sink(l, "w_up", _mm(A["h2"], dgu, ta=True, out_dtype=BF16, out_blocks=N_DEV, tn=upw, name="mm_up_dw"),
               dh2)
    dx1, dx1b, G["ffn_norm_g"] = _rms_bwd(A["x1"], W["ffn_norm_g"], dh2, dx2, name="rms2_bwd")
    dmix = _mm(dx1b, A["w_out"], tb=True, name="mm_out_dx")
    dmix = sink(l, "w_out", _mm(A["mix"], dx1b, ta=True, out_dtype=BF16, name="mm_out_dw"), dmix)
    do, dz, dout_a, G["gdn_norm_g"], G["conv_pw_b"] = _gdn_out_bwd(A["o"], A["p"], W["gdn_norm_g"], dmix,
                                                                   CC=CC, VW=VW, name="gdn_out_bwd")
    dq, dk, dv, dg, dbeta = _gdn_core_bwd(A["qkv"], A["g"], A["beta"], A["states"], A["tinv"], do,
                                          Bl=Bl, S=S, KW=KW, VW=VW, name="gdn_core_bwd")
    dpba, dalog, ddtb = _gdn_gate_bwd(A["pba"], W["gdn_a_log"], W["gdn_dt_bias"], dg, dbeta, Bl=Bl, S=S, H=H,
                                      name="gdn_gate_bwd")
    G["gdn_a_log"], G["gdn_dt_bias"] = dalog[:H, 0], ddtb[:H, 0]
    dqkv, G["gdn_conv_w"] = _gdn_pre_bwd(A["p"], W["gdn_conv_w"], dq, dk, dv, Bl=Bl, S=S, CC=CC, KW=KW, VW=VW,
                                         name="gdn_pre_bwd")
    du3 = _mm(dout_a, A["conv_pw_w"], tb=True, name="mm_pw_dx")
    du3 = sink(l, "conv_pw_w", _mm(A["u3"], dout_a, ta=True, out_dtype=BF16, name="mm_pw_dw"), du3)
    dav, dag, G["conv_dw_w"], G["conv_dw_b"], G["conv_ln_g"], G["conv_ln_b"] = _conf_bwd(
        A["p"], W["conv_dw_w"], W["conv_dw_b"], W["conv_ln_g"], W["conv_ln_b"], du3, Bl=Bl, S=S, CC=CC,
        name="conf_bwd")
    dp = jnp.concatenate([dav, dag, dqkv, dz], axis=1)
    dh1 = _mm(dpba, A["w_in_ba"], tb=True, name="mm_in_ba_dx")
    dh1 = _mm(dp, A["w_in_main"], tb=True, res=dh1, name="mm_in_dx")
    dh1 = sink(l, "w_in", (_mm(A["h1"], dp, ta=True, out_dtype=BF16, name="mm_in_dw"),
                           _mm(A["h1"], dpba, ta=True, out_dtype=BF16, name="mm_in_ba_dw")), dh1)
    dx, dxb, G["mix_norm_g"] = _rms_bwd(A["x"], W["mix_norm_g"], dh1, dx1, name="rms1_bwd")
    return dx, dxb, G


def _local_step(x, target, Ws, final_norm_g, fetch, sink):
    Bl, S, D = x.shape
    xt = x.reshape(Bl * S, D)
    acts = []
    for l, W in enumerate(Ws):
        xt, A = _layer_fwd(l, xt, W, Bl, S, fetch)
        acts.append(A)
    loss, dx, dxb, dgf = _loss_head(xt, final_norm_g, target.reshape(Bl * S, D), name="loss_head")
    grads = [None] * len(Ws)
    for l in reversed(range(len(Ws))):
        dx, dxb, grads[l] = _layer_bwd(l, dx, dxb, Ws[l], acts[l], Bl, S, sink)
    return loss[0, 0], dx.reshape(Bl, S, D), grads, dgf.reshape(D)


def _mesh_pos():
    return lax.axis_index("x"), lax.axis_index("y"), lax.axis_index("c")


def _dev_index(px, py, pc):
    return 4 * px + 2 * py + pc


_ANY = pl.BlockSpec(memory_space=pl.ANY)


def _all_gather(arrs, *, name):
    n = len(arrs)

    def body(*refs):
        ins, outs = refs[:n], refs[n:2 * n]
        send_sems, recv_sems, local_sems = refs[2 * n:]
        x, y, c = _mesh_pos()
        me, sibling = (x, y, c), (x, y, 1 - c)
        chips = [(1 - x, y), (x, 1 - y), (1 - x, 1 - y)]

        def copy(a, k, block, to, src=None):
            dst = outs[a].at[_dev_index(*block)]
            return pltpu.make_async_remote_copy(
                src_ref=dst if src is None else src, dst_ref=dst,
                send_sem=send_sems.at[a, k], recv_sem=recv_sems.at[a, k],
                device_id=to, device_id_type=MESH)

        mine = [pltpu.make_async_copy(ins[a], outs[a].at[_dev_index(*me)], local_sems.at[a]) for a in range(n)]
        for cp in mine:
            cp.start()
        first = []
        for a in range(n):
            first.append(copy(a, 0, me, sibling, src=ins[a]))
            first += [copy(a, 1 + j, me, (*chip, c), src=ins[a]) for j, chip in enumerate(chips)]
        for cp in first:
            cp.start()
        passed = []
        for a in range(n):
            for j, chip in enumerate(chips):
                copy(a, 1 + j, (*chip, c), me).wait_recv()
                fwd = copy(a, 4 + j, (*chip, c), sibling)
                fwd.start()
                passed.append(fwd)
        for a in range(n):
            copy(a, 0, sibling, me).wait_recv()
            for j, chip in enumerate(chips):
                copy(a, 4 + j, (*chip, 1 - c), me).wait_recv()
        for cp in first + passed:
            cp.wait_send()
        for cp in mine:
            cp.wait()

    return pl.pallas_call(
        body, name=name,
        in_specs=[_ANY] * n, out_specs=[_ANY] * n,
        out_shape=[jax.ShapeDtypeStruct((N_DEV,) + a.shape, a.dtype) for a in arrs],
        scratch_shapes=[pltpu.SemaphoreType.DMA((n, N_DEV - 1)), pltpu.SemaphoreType.DMA((n, N_DEV - 1)),
                        pltpu.SemaphoreType.DMA((n,))],
    )(*arrs)


def _peers(x, y, c):
    flip = lambda v, f: 1 - v if f else v
    return [(flip(x, p & 4), flip(y, p & 2), flip(c, p & 1)) for p in range(1, N_DEV)]


GATHER_ID, SCATTER_ID = 1, 2
_SEQUENCER = dict(axis_name="sequencer", num_cores=1)


def _handshake(peers):
    barrier = pltpu.get_barrier_semaphore()
    for peer in peers:
        pl.semaphore_signal(barrier, inc=1, device_id=peer, device_id_type=MESH)
    pl.semaphore_wait(barrier, len(peers))


def _sc_gather(src, *, name):
    def body(src_ref, zone_ref, send_sems, recv_sems, local_sem):
        x, y, c = _mesh_pos()
        me, sibling = (x, y, c), (x, y, 1 - c)
        chips = [(1 - x, y), (x, 1 - y), (1 - x, 1 - y)]
        _handshake([sibling] + [(*chip, c) for chip in chips])

        def copy(k, block, to, from_src=False):
            dst = zone_ref.at[_dev_index(*block)]
            return pltpu.make_async_remote_copy(
                src_ref=src_ref if from_src else dst, dst_ref=dst, send_sem=send_sems.at[k], recv_sem=recv_sems.at[k],
                device_id=to, device_id_type=MESH)

        mine = pltpu.make_async_copy(src_ref, zone_ref.at[_dev_index(*me)], local_sem)
        mine.start()
        first = [copy(1 + j, me, (*chip, c), from_src=True) for j, chip in enumerate(chips)]
        first.append(copy(0, me, sibling, from_src=True))
        for cp in first:
            cp.start()
        passed = []
        for j, chip in enumerate(chips):
            copy(1 + j, (*chip, c), me).wait_recv()
            fwd = copy(4 + j, (*chip, c), sibling)
            fwd.start()
            passed.append(fwd)
        copy(0, sibling, me).wait_recv()
        for j, chip in enumerate(chips):
            copy(4 + j, (*chip, 1 - c), me).wait_recv()
        for cp in first + passed:
            cp.wait_send()
        mine.wait()

    return pl.kernel(
        body, name=name,
        out_type=jax.ShapeDtypeStruct((N_DEV,) + src.shape, src.dtype),
        mesh=plsc.ScalarSubcoreMesh(**_SEQUENCER),
        scratch_types=[pltpu.SemaphoreType.DMA((N_DEV - 1,)), pltpu.SemaphoreType.DMA((N_DEV - 1,)),
                       pltpu.SemaphoreType.DMA],
        compiler_params=pltpu.CompilerParams(collective_id=GATHER_ID),
    )(src)


def _sc_scatter(part, *, name):
    def body(src_ref, zone_ref, send_sems, recv_sems, local_sem):
        x, y, c = _mesh_pos()
        me = _dev_index(x, y, c)
        peers = _peers(x, y, c)
        _handshake(peers)
        mine = pltpu.make_async_copy(src_ref.at[me], zone_ref.at[me], local_sem)
        mine.start()
        sends = [pltpu.make_async_remote_copy(
            src_ref=src_ref.at[_dev_index(*peer)], dst_ref=zone_ref.at[me], send_sem=send_sems.at[k],
            recv_sem=recv_sems.at[k], device_id=peer, device_id_type=MESH) for k, peer in enumerate(peers)]
        for cp in sends:
            cp.start()
        for k, peer in enumerate(peers):
            pltpu.make_async_remote_copy(
                src_ref=src_ref.at[me], dst_ref=zone_ref.at[_dev_index(*peer)], send_sem=send_sems.at[k],
                recv_sem=recv_sems.at[k], device_id=peer, device_id_type=MESH).wait_recv()
        for cp in sends:
            cp.wait_send()
        mine.wait()

    return pl.kernel(
        body, name=name,
        out_type=jax.ShapeDtypeStruct(part.shape, part.dtype),
        mesh=plsc.ScalarSubcoreMesh(**_SEQUENCER),
        scratch_types=[pltpu.SemaphoreType.DMA((N_DEV - 1,)), pltpu.SemaphoreType.DMA((N_DEV - 1,)),
                       pltpu.SemaphoreType.DMA],
        compiler_params=pltpu.CompilerParams(collective_id=SCATTER_ID),
    )(part)


def _adamw_math(w, g, m, v):
    m2 = ADAM_B1 * m + (1.0 - ADAM_B1) * g
    v2 = ADAM_B2 * v + (1.0 - ADAM_B2) * (g * g)
    m_hat = m2 / (1.0 - ADAM_B1 ** ADAM_STEP)
    v_hat = v2 / (1.0 - ADAM_B2 ** ADAM_STEP)
    delta = -ADAM_LR * (m_hat / (jnp.sqrt(v_hat) + ADAM_EPS) + ADAM_WD * w)
    return delta, m2, v2


def _adamw_big(l, w, m, v, recv, prev, *, name, tr=128):
    L, R, C = w.shape
    tr = next(t for t in range(min(tr, R), 0, -16) if R % t == 0)

    def body(w_ref, m_ref, v_ref, r_ref, *rest):
        g_ref, d_ref, m2_ref, v2_ref = rest[-4:]
        g = r_ref[0].astype(F32)
        for s in range(1, N_DEV):
            g = g + r_ref[s].astype(F32)
        g_ref[...] = g
        d_ref[...], m2_ref[...], v2_ref[...] = _adamw_math(w_ref[...], g, m_ref[...], v_ref[...])

    wspec = pl.BlockSpec((None, tr, C), lambda i: (l, i, 0))
    prev = list(prev) if prev is not None else []
    return pl.pallas_call(
        body, name=name, grid=(R // tr,),
        in_specs=[wspec, wspec, wspec, pl.BlockSpec((N_DEV, tr, C), lambda i: (0, i, 0))] + [_ANY] * len(prev),
        out_specs=[wspec] * 4,
        out_shape=[jax.ShapeDtypeStruct((L, R, C), F32)] * 4,
        input_output_aliases={4 + j: j for j in range(len(prev))},
        compiler_params=_cparams(("parallel",)),
    )(w, m, v, recv.reshape(N_DEV, R, C), *prev)


def _sum_slots(gathered, *, name):
    _, R, C = gathered.shape

    def body(r_ref, o_ref):
        g = r_ref[0]
        for s in range(1, N_DEV):
            g = g + r_ref[s]
        o_ref[...] = g

    return pl.pallas_call(body, name=name, out_shape=jax.ShapeDtypeStruct((R, C), F32))(gathered)


def _adamw_small(w, g, m, v, *, name):
    def body(w_ref, g_ref, m_ref, v_ref, d_ref, m2_ref, v2_ref):
        d_ref[...], m2_ref[...], v2_ref[...] = _adamw_math(w_ref[...], g_ref[...], m_ref[...], v_ref[...])

    return pl.pallas_call(body, name=name, out_shape=[jax.ShapeDtypeStruct(w.shape, F32)] * 3)(w, g, m, v)


def _pack(arrs):
    flat = []
    for a in arrs:
        a = a.reshape(-1).astype(F32)
        flat.append(jnp.pad(a, (0, (-a.shape[0]) % LANES)))
    out = jnp.concatenate(flat)
    out = jnp.pad(out, (0, (-out.shape[0]) % (8 * LANES)))
    return out.reshape(-1, LANES)


def _unpack(packed, shapes):
    flat = packed.reshape(-1)
    out, pos = [], 0
    for s in shapes:
        size = math.prod(s)
        out.append(flat[pos:pos + size].reshape(s))
        pos += size + (-size) % LANES
    return out


BIG = ("w_in", "conv_pw_w", "w_out", "w_up", "w_down")
SMALL_SHARDED = ("conv_dw_w", "gdn_conv_w", "ffn_conv_w")
SMALL_REPLICATED = ("mix_norm_g", "conv_dw_b", "conv_ln_g", "conv_ln_b", "conv_pw_b", "gdn_a_log", "gdn_dt_bias",
                    "gdn_norm_g", "ffn_norm_g", "ffn_conv_b")
WEIGHTS = ("mix_norm_g", "w_in", "conv_dw_w", "conv_dw_b", "conv_ln_g", "conv_ln_b", "conv_pw_w", "conv_pw_b",
           "gdn_conv_w", "gdn_a_log", "gdn_dt_bias", "gdn_norm_g", "w_out", "ffn_norm_g", "w_up", "ffn_conv_w",
           "ffn_conv_b", "w_down", "final_norm_g")


def _train_step(x, target, w, m, v):
    L = w["w_in"].shape[0]
    D = x.shape[-1]
    xi, yi, ci = _mesh_pos()
    me = _dev_index(xi, yi, ci)

    small_full = {}
    for n in SMALL_SHARDED:
        g_ = _sc_gather(w[n], name=f"gather_{n}")
        small_full[n] = jnp.moveaxis(g_, 0, 2).reshape(L, g_.shape[2], N_DEV * g_.shape[3])
    gathered = {}

    def launch(l, after=None):
        for n in BIG:
            src = w[n][l].astype(BF16)
            if after is not None:
                src = lax.optimization_barrier((src, after))[0]
            gathered[n, l] = _sc_gather(src, name=f"gather_{n}_{l}")

    launch(0)
    Ws = []
    for l in range(L):
        W = {n: w[n][l] for n in SMALL_REPLICATED}
        W.update({n: small_full[n][l] for n in SMALL_SHARDED})
        Ws.append(W)

    def fetch(l, n, after):
        if n == "conv_pw_w" and l + 1 < L:
            launch(l + 1, after)
        g_ = lax.optimization_barrier((gathered[n, l], after))[0]
        if n == "w_up":
            return g_
        if n == "w_in":
            w_in = jnp.moveaxis(g_, 0, 1).reshape(D, -1)
            n_main = (w_in.shape[1] // LANES) * LANES
            return w_in[:, :n_main], jnp.pad(w_in[:, n_main:], ((0, 0), (0, LANES - (w_in.shape[1] - n_main))))
        return g_.reshape(g_.shape[0] * g_.shape[1], g_.shape[2])

    started = []
    res = {}
    SCATTERS_IN_FLIGHT = 2

    def consume(chain):
        n, l, recv = started.pop(0)
        res[n] = _adamw_big(l, w[n], m[n], v[n], recv, res.get(n), name=f"adamw_{n}")
        if chain is None:
            return None
        tied = lax.optimization_barrier((chain, *res[n]))
        res[n] = list(tied[1:])
        return tied[0]

    def sink(l, n, g_, chain):
        g_, chain = lax.optimization_barrier((g_, chain))
        if len(started) >= SCATTERS_IN_FLIGHT:
            chain = consume(chain)
        if n == "w_in":
            g_main, g_ba = g_
            g_ = jnp.concatenate([g_main, g_ba[:, :w["w_in"].shape[2] * N_DEV - g_main.shape[1]]], axis=1)
            part = jnp.moveaxis(g_.reshape(D, N_DEV, -1), 1, 0)
        elif n == "w_up":
            part = g_
        else:
            part = g_.reshape(N_DEV, -1, g_.shape[1])
        started.append((n, l, _sc_scatter(part, name=f"scatter_{n}_{l}")))
        return chain

    loss, grad_x, G, d_final = _local_step(x, target, Ws, w["final_norm_g"], fetch, sink)
    loss = lax.psum(loss, ("x", "y", "c"))

    out = {k: {} for k in ("grad", "delta", "new_m", "new_v")}
    while started:
        consume(None)
    for n in BIG:
        for j, k in enumerate(("grad", "delta", "new_m", "new_v")):
            out[k][n] = res[n][j]

    small_names = [n for n in WEIGHTS if n not in BIG]
    partial = []
    for n in small_names:
        if n == "final_norm_g":
            partial.append(d_final)
        else:
            partial.append(jnp.stack([G[l][n].reshape(Ws[l][n].shape) for l in range(L)]))
    shapes = [p_.shape for p_ in partial]
    gathered = _all_gather([_pack(partial)], name="all_gather_small_grads")[0]
    full = dict(zip(small_names, _unpack(_sum_slots(gathered, name="sum_small_grads"), shapes)))
    for n in SMALL_SHARDED:
        width = w[n].shape[-1]
        full[n] = lax.dynamic_slice_in_dim(full[n], me * width, width, axis=2)
    loc_shapes = [w[n].shape for n in small_names]
    g_pack = _pack([full[n] for n in small_names])
    res = _adamw_small(_pack([w[n] for n in small_names]), g_pack, _pack([m[n] for n in small_names]),
                       _pack([v[n] for n in small_names]), name="adamw_small")
    for k, packed in zip(("grad", "delta", "new_m", "new_v"), (g_pack,) + tuple(res)):
        out[k].update(dict(zip(small_names, _unpack(packed, loc_shapes))))
    return loss, grad_x, out


def kernel(x, mix_norm_g, w_in, conv_dw_w, conv_dw_b, conv_ln_g, conv_ln_b, conv_pw_w, conv_pw_b, gdn_conv_w, gdn_a_log, gdn_dt_bias, gdn_norm_g, w_out, ffn_norm_g, w_up, ffn_conv_w, ffn_conv_b, w_down, final_norm_g, loss_target, m_mix_norm_g, m_w_in, m_conv_dw_w, m_conv_dw_b, m_conv_ln_g, m_conv_ln_b, m_conv_pw_w, m_conv_pw_b, m_gdn_conv_w, m_gdn_a_log, m_gdn_dt_bias, m_gdn_norm_g, m_w_out, m_ffn_norm_g, m_w_up, m_ffn_conv_w, m_ffn_conv_b, m_w_down, m_final_norm_g, v_mix_norm_g, v_w_in, v_conv_dw_w, v_conv_dw_b, v_conv_ln_g, v_conv_ln_b, v_conv_pw_w, v_conv_pw_b, v_gdn_conv_w, v_gdn_a_log, v_gdn_dt_bias, v_gdn_norm_g, v_w_out, v_ffn_norm_g, v_w_up, v_ffn_conv_w, v_ffn_conv_b, v_w_down, v_final_norm_g):
    w = dict(zip(WEIGHTS, (mix_norm_g, w_in, conv_dw_w, conv_dw_b, conv_ln_g, conv_ln_b, conv_pw_w, conv_pw_b, gdn_conv_w,
                           gdn_a_log, gdn_dt_bias, gdn_norm_g, w_out, ffn_norm_g, w_up, ffn_conv_w, ffn_conv_b, w_down,
                           final_norm_g)))
    m = dict(zip(WEIGHTS, (m_mix_norm_g, m_w_in, m_conv_dw_w, m_conv_dw_b, m_conv_ln_g, m_conv_ln_b, m_conv_pw_w,
                           m_conv_pw_b, m_gdn_conv_w, m_gdn_a_log, m_gdn_dt_bias, m_gdn_norm_g, m_w_out, m_ffn_norm_g,
                           m_w_up, m_ffn_conv_w, m_ffn_conv_b, m_w_down, m_final_norm_g)))
    v = dict(zip(WEIGHTS, (v_mix_norm_g, v_w_in, v_conv_dw_w, v_conv_dw_b, v_conv_ln_g, v_conv_ln_b, v_conv_pw_w,
                           v_conv_pw_b, v_gdn_conv_w, v_gdn_a_log, v_gdn_dt_bias, v_gdn_norm_g, v_w_out, v_ffn_norm_g,
                           v_w_up, v_ffn_conv_w, v_ffn_conv_b, v_w_down, v_final_norm_g)))
    loss, grad_x, out = _train_step(x, loss_target, w, m, v)
    return (loss, grad_x, *[out["grad"][n] for n in WEIGHTS], *[out["delta"][n] for n in WEIGHTS],
            *[out["new_m"][n] for n in WEIGHTS], *[out["new_v"][n] for n in WEIGHTS])
```

```python
import functools
import math

import jax
import jax.numpy as jnp
from jax import lax
from jax.experimental import pallas as pl
from jax.experimental.pallas import tpu as pltpu
from jax.experimental.pallas import tpu_sc as plsc

F32 = jnp.float32
BF16 = jnp.bfloat16
HIGHEST = lax.Precision.HIGHEST
MESH = pl.DeviceIdType.MESH

EPS = 1e-6
LANES = 128
CHUNK = 128
NEAR_BLOCK = 32
CONV_K = 31
SHORT_CONV_K = 4
FFN_CONV_K = 3
N_DEV = 8
VMEM_LIMIT = 56 * 1024 * 1024

ADAM_LR = 0.001
ADAM_B1 = 0.9
ADAM_B2 = 0.999
ADAM_EPS = 1e-08
ADAM_WD = 0.01
ADAM_STEP = 10


def _cparams(sem):
    return pltpu.CompilerParams(dimension_semantics=sem, vmem_limit_bytes=VMEM_LIMIT)


def _sig(x):
    return 1.0 / (1.0 + jnp.exp(-x))


def _silu(x):
    return x * _sig(x)


def _dsilu(x):
    s = _sig(x)
    return s * (1.0 + x * (1.0 - s))


def _softplus(x):
    return jnp.maximum(x, 0.0) + jnp.log1p(jnp.exp(-jnp.abs(x)))


def _dot(a, b, precision=None):
    return jnp.dot(a, b, preferred_element_type=F32, precision=precision)


def _dot_nt(a, b):
    return lax.dot_general(a, b, (((1,), (1,)), ((), ())), preferred_element_type=F32)


def _dot_tn(a, b):
    return lax.dot_general(a, b, (((0,), (0,)), ((), ())), preferred_element_type=F32)


def _bf(x):
    return x.astype(BF16)


_NN = (((1,), (0,)), ((), ()))
_TN = (((0,), (0,)), ((), ()))


def _split2(x):
    hi = _bf(x)
    return hi, _bf(x - hi.astype(F32))


def _dot_x3(a, b, dn=_NN):
    ah, al = _split2(a)
    bh, bl = _split2(b)
    f = lambda p, q: lax.dot_general(p, q, dn, preferred_element_type=F32)
    return f(ah, bh) + (f(al, bh) + f(ah, bl))


def _dot_mask(mask, x, dn=_NN):
    mb = _bf(mask)
    hi, lo = _split2(x)
    lo2 = _bf(x - hi.astype(F32) - lo.astype(F32))
    f = lambda q: lax.dot_general(mb, q, dn, preferred_element_type=F32)
    return f(hi) + (f(lo) + f(lo2))


def _shift_down(u, s):
    if s == 0:
        return u
    row = lax.broadcasted_iota(jnp.int32, u.shape, 0)
    return jnp.where(row >= s, pltpu.roll(u, s, 0), 0.0)


def _shift_up(u, s):
    if s == 0:
        return u
    n = u.shape[0]
    row = lax.broadcasted_iota(jnp.int32, u.shape, 0)
    return jnp.where(row < n - s, pltpu.roll(u, n - s, 0), 0.0)


def _conv_fwd(u, w_ref, K):
    acc = None
    for k in range(K):
        term = w_ref[k:k + 1, :] * _shift_down(u, K - 1 - k)
        acc = term if acc is None else acc + term
    return acc


def _conv_bwd_in(do, w_ref, K):
    acc = None
    for k in range(K):
        term = w_ref[k:k + 1, :] * _shift_up(do, K - 1 - k)
        acc = term if acc is None else acc + term
    return acc


def _conv_bwd_w(do, u, dw_ref, K, first):
    for k in range(K):
        row = jnp.sum(do * _shift_down(u, K - 1 - k), axis=0, keepdims=True)
        _acc_row(dw_ref, k, row, first)


def _acc_row(ref, k, row, first):
    @pl.when(first)
    def _():
        ref[k:k + 1, :] = row

    @pl.when(jnp.logical_not(first))
    def _():
        ref[k:k + 1, :] += row


def _logical(arr):
    if arr.ndim == 2:
        return arr.shape
    return (arr.shape[1], arr.shape[0] * arr.shape[2])


def _tile(dim, pref, *col_widths):
    if dim % LANES:
        assert not col_widths
        return dim
    t = (min(pref, dim) // LANES) * LANES
    while t > LANES and (dim % t or any(c % t for c in col_widths)):
        t -= LANES
    assert dim % t == 0 and all(c % t == 0 for c in col_widths), (dim, pref, col_widths)
    return t


def _spec(shape, rt, ct, rfn, cfn):
    if len(shape) == 2:
        return pl.BlockSpec((rt, ct), lambda i, j, k: (rfn(i, j, k), cfn(i, j, k)))
    per = shape[2] // ct
    return pl.BlockSpec((None, rt, ct),
                        lambda i, j, k: (cfn(i, j, k) // per, rfn(i, j, k), cfn(i, j, k) % per))


def _mm(a, b, *, name, ta=False, tb=False, out_dtype=F32, out_blocks=None, bias=None, res=None,
        tm=1024, tn=1024, tk=2048):
    ra, ca = _logical(a)
    rb, cb = _logical(b)
    M, K = (ca, ra) if ta else (ra, ca)
    N, K2 = (rb, cb) if tb else (cb, rb)
    assert K == K2, (a.shape, b.shape, ta, tb)
    out_shape = (M, N) if out_blocks is None else (out_blocks, M, N // out_blocks)
    cw = lambda arr: [arr.shape[2]] if arr.ndim == 3 else []
    m_c = cw(a) if ta else []
    k_c = (cw(a) if not ta else []) + (cw(b) if tb else [])
    n_c = (cw(b) if not tb else []) + ([out_shape[2]] if out_blocks else []) + (cw(res) if res is not None else [])
    tm, tn, tk = _tile(M, tm, *m_c), _tile(N, tn, *n_c), _tile(K, tk, *k_c)
    nk = K // tk
    im, jn, kk = (lambda i, j, k: i), (lambda i, j, k: j), (lambda i, j, k: k)
    in_specs = [
        _spec(a.shape, tk, tm, kk, im) if ta else _spec(a.shape, tm, tk, im, kk),
        _spec(b.shape, tn, tk, jn, kk) if tb else _spec(b.shape, tk, tn, kk, jn),
    ]
    args = [a, b]
    if bias is not None:
        in_specs.append(pl.BlockSpec((1, tn), lambda i, j, k: (0, j)))
        args.append(bias.reshape(1, N).astype(F32))
    if res is not None:
        in_specs.append(_spec(res.shape, tm, tn, im, jn))
        args.append(res)
    dn = (((0 if ta else 1,), (1 if tb else 0,)), ((), ()))

    def body(*refs):
        a_ref, b_ref = refs[0], refs[1]
        pos = 2
        bias_ref = res_ref = None
        if bias is not None:
            bias_ref = refs[pos]
            pos += 1
        if res is not None:
            res_ref = refs[pos]
            pos += 1
        o_ref = refs[pos]
        k = pl.program_id(2)
        part = lax.dot_general(_bf(a_ref[...]), _bf(b_ref[...]), dn, preferred_element_type=F32)

        def finish(r):
            if bias_ref is not None:
                r = r + bias_ref[...]
            if res_ref is not None:
                r = r + res_ref[...].astype(F32)
            o_ref[...] = r.astype(out_dtype)

        if nk == 1:
            finish(part)
            return
        acc_ref = refs[pos + 1]

        @pl.when(k == 0)
        def _():
            acc_ref[...] = part

        @pl.when((k > 0) & (k < nk - 1))
        def _():
            acc_ref[...] += part

        @pl.when(k == nk - 1)
        def _():
            finish(acc_ref[...] + part)

    return pl.pallas_call(
        body, name=name,
        grid=(M // tm, N // tn, nk),
        in_specs=in_specs,
        out_specs=_spec(out_shape, tm, tn, im, jn),
        out_shape=jax.ShapeDtypeStruct(out_shape, out_dtype),
        scratch_shapes=[pltpu.VMEM((tm, tn), F32)] if nk > 1 else [],
        compiler_params=_cparams(("parallel", "parallel", "arbitrary")),
    )(*args)


def _rms_fwd(x, g, *, name, tr=512):
    T, D = x.shape
    tr = min(tr, T)

    def body(x_ref, g_ref, h_ref):
        xv = x_ref[...]
        r = lax.rsqrt(jnp.mean(xv * xv, axis=-1, keepdims=True) + EPS)
        h_ref[...] = (xv * r * g_ref[...]).astype(BF16)

    return pl.pallas_call(
        body, name=name, grid=(T // tr,),
        in_specs=[pl.BlockSpec((tr, D), lambda i: (i, 0)), pl.BlockSpec((1, D), lambda i: (0, 0))],
        out_specs=pl.BlockSpec((tr, D), lambda i: (i, 0)),
        out_shape=jax.ShapeDtypeStruct((T, D), BF16),
        compiler_params=_cparams(("parallel",)),
    )(x, g.reshape(1, D))


def _rms_bwd(x, g, dh, dres, *, name, tr=512):
    T, D = x.shape
    tr = min(tr, T)

    def body(x_ref, g_ref, dh_ref, dres_ref, dx_ref, dxb_ref, dg_ref):
        i = pl.program_id(0)
        xv = x_ref[...]
        dy = dh_ref[...].astype(F32)
        r = lax.rsqrt(jnp.mean(xv * xv, axis=-1, keepdims=True) + EPS)
        dyg = dy * g_ref[...]
        dot = jnp.mean(dyg * xv, axis=-1, keepdims=True)
        dx = dres_ref[...] + r * dyg - xv * (r * r * r) * dot
        dx_ref[...] = dx
        dxb_ref[...] = dx.astype(BF16)
        part = jnp.sum(dy * xv * r, axis=0, keepdims=True)
        _acc_row(dg_ref, 0, part, i == 0)

    row = pl.BlockSpec((tr, D), lambda i: (i, 0))
    vec = pl.BlockSpec((1, D), lambda i: (0, 0))
    return pl.pallas_call(
        body, name=name, grid=(T // tr,),
        in_specs=[row, vec, row, row],
        out_specs=[row, row, vec],
        out_shape=[jax.ShapeDtypeStruct((T, D), F32), jax.ShapeDtypeStruct((T, D), BF16),
                   jax.ShapeDtypeStruct((1, D), F32)],
        compiler_params=_cparams(("arbitrary",)),
    )(x, g.reshape(1, D), dh, dres)


def _loss_head(x, g, target, *, name, tr=512):
    T, D = x.shape
    tr = min(tr, T)

    def body(x_ref, g_ref, t_ref, loss_ref, dx_ref, dxb_ref, dg_ref):
        i = pl.program_id(0)
        xv = x_ref[...]
        gv = g_ref[...]
        r = lax.rsqrt(jnp.mean(xv * xv, axis=-1, keepdims=True) + EPS)
        y = xv * r * gv
        err = y - t_ref[...]
        lpart = 0.5 * jnp.sum(jnp.mean(err * err, axis=-1, keepdims=True), axis=0, keepdims=True)
        dy = err * (1.0 / D)
        dyg = dy * gv
        dot = jnp.mean(dyg * xv, axis=-1, keepdims=True)
        dx = r * dyg - xv * (r * r * r) * dot
        dx_ref[...] = dx
        dxb_ref[...] = dx.astype(BF16)
        _acc_row(dg_ref, 0, jnp.sum(dy * xv * r, axis=0, keepdims=True), i == 0)
        _acc_row(loss_ref, 0, jnp.broadcast_to(lpart, (1, LANES)), i == 0)

    row = pl.BlockSpec((tr, D), lambda i: (i, 0))
    return pl.pallas_call(
        body, name=name, grid=(T // tr,),
        in_specs=[row, pl.BlockSpec((1, D), lambda i: (0, 0)), row],
        out_specs=[pl.BlockSpec((1, LANES), lambda i: (0, 0)), row, row, pl.BlockSpec((1, D), lambda i: (0, 0))],
        out_shape=[jax.ShapeDtypeStruct((1, LANES), F32), jax.ShapeDtypeStruct((T, D), F32),
                   jax.ShapeDtypeStruct((T, D), BF16), jax.ShapeDtypeStruct((1, D), F32)],
        compiler_params=_cparams(("arbitrary",)),
    )(x, g.reshape(1, D), target)


def _conf_norm(u1, lg_ref, lb_ref):
    mu = jnp.mean(u1, axis=-1, keepdims=True)
    xc = u1 - mu
    r = lax.rsqrt(jnp.mean(xc * xc, axis=-1, keepdims=True) + EPS)
    n = xc * r
    return r, n, n * lg_ref[...] + lb_ref[...]


def _conf_fwd(p, dw_w, dw_b, ln_g, ln_b, *, Bl, S, CC, name):
    G = CC // LANES

    def body(av_ref, ag_ref, w_ref, b_ref, lg_ref, lb_ref, o_ref, u1_ref):
        u0 = av_ref[...] * _sig(ag_ref[...])
        u1 = _conv_fwd(u0, w_ref, CONV_K) + b_ref[...]
        u1_ref[...] = u1
        _, _, u2 = _conf_norm(u1, lg_ref, lb_ref)
        o_ref[...] = _silu(u2).astype(BF16)

    vec = pl.BlockSpec((1, LANES), lambda b, j: (0, j))
    seq = pl.BlockSpec((S, LANES), lambda b, j: (b, j))
    return pl.pallas_call(
        body, name=name, grid=(Bl, G),
        in_specs=[seq, pl.BlockSpec((S, LANES), lambda b, j: (b, G + j)),
                  pl.BlockSpec((CONV_K, LANES), lambda b, j: (0, j)), vec, vec, vec],
        out_specs=[seq, seq],
        out_shape=[jax.ShapeDtypeStruct((Bl * S, CC), BF16), jax.ShapeDtypeStruct((Bl * S, CC), F32)],
        compiler_params=_cparams(("parallel", "parallel")),
    )(p, p, dw_w, dw_b.reshape(1, CC), ln_g.reshape(1, CC), ln_b.reshape(1, CC))


def _conf_bwd(p, u1, dw_w, ln_g, ln_b, du3, *, Bl, S, CC, name):
    G = CC // LANES

    def body(av_ref, ag_ref, u1_ref, w_ref, lg_ref, lb_ref, du3_ref,
             dav_ref, dag_ref, dw_ref, db_ref, dlg_ref, dlb_ref):
        first = pl.program_id(1) == 0
        av = av_ref[...]
        sg = _sig(ag_ref[...])
        u0 = av * sg
        r, n, u2 = _conf_norm(u1_ref[...], lg_ref, lb_ref)
        du2 = du3_ref[...] * _dsilu(u2)
        _acc_row(dlg_ref, 0, jnp.sum(du2 * n, axis=0, keepdims=True), first)
        _acc_row(dlb_ref, 0, jnp.sum(du2, axis=0, keepdims=True), first)
        dn = du2 * lg_ref[...]
        du1 = r * (dn - jnp.mean(dn, axis=-1, keepdims=True) - n * jnp.mean(dn * n, axis=-1, keepdims=True))
        _acc_row(db_ref, 0, jnp.sum(du1, axis=0, keepdims=True), first)
        _conv_bwd_w(du1, u0, dw_ref, CONV_K, first)
        du0 = _conv_bwd_in(du1, w_ref, CONV_K)
        dav_ref[...] = (du0 * sg).astype(BF16)
        dag_ref[...] = (du0 * av * sg * (1.0 - sg)).astype(BF16)

    vec = pl.BlockSpec((1, LANES), lambda j, b: (0, j))
    seq = pl.BlockSpec((S, LANES), lambda j, b: (b, j))
    return pl.pallas_call(
        body, name=name, grid=(G, Bl),
        in_specs=[seq, pl.BlockSpec((S, LANES), lambda j, b: (b, G + j)), seq,
                  pl.BlockSpec((CONV_K, LANES), lambda j, b: (0, j)), vec, vec, seq],
        out_specs=[seq, seq, pl.BlockSpec((CONV_K, LANES), lambda j, b: (0, j)), vec, vec, vec],
        out_shape=[jax.ShapeDtypeStruct((Bl * S, CC), BF16), jax.ShapeDtypeStruct((Bl * S, CC), BF16),
                   jax.ShapeDtypeStruct((CONV_K, CC), F32), jax.ShapeDtypeStruct((1, CC), F32),
                   jax.ShapeDtypeStruct((1, CC), F32), jax.ShapeDtypeStruct((1, CC), F32)],
        compiler_params=_cparams(("parallel", "arbitrary")),
    )(p, p, u1, dw_w, ln_g.reshape(1, CC), ln_b.reshape(1, CC), du3)


def _gdn_pre_fwd(p, conv_w, *, Bl, S, CC, KW, VW, name):
    NQK = 2 * KW // LANES
    NB = NQK + VW // LANES
    off = 2 * CC // LANES

    def body(x_ref, w_ref, o_ref):
        j = pl.program_id(1)
        s = _silu(_conv_fwd(x_ref[...], w_ref, SHORT_CONV_K))
        r = lax.rsqrt(jnp.sum(s * s, axis=-1, keepdims=True) + EPS)
        o_ref[...] = jnp.where(j < NQK, s * r, s)

    return pl.pallas_call(
        body, name=name, grid=(Bl, NB),
        in_specs=[pl.BlockSpec((S, LANES), lambda b, j: (b, off + j)),
                  pl.BlockSpec((SHORT_CONV_K, LANES), lambda b, j: (0, j))],
        out_specs=pl.BlockSpec((S, LANES), lambda b, j: (b, j)),
        out_shape=jax.ShapeDtypeStruct((Bl * S, NB * LANES), F32),
        compiler_params=_cparams(("parallel", "parallel")),
    )(p, conv_w)


def _gdn_pre_bwd(p, conv_w, dq, dk, dv, *, Bl, S, CC, KW, VW, name):
    HQ = KW // LANES
    H = VW // LANES
    NQK = 2 * HQ
    NB = NQK + H
    off = 2 * CC // LANES

    def body(x_ref, w_ref, dq_ref, dk_ref, dv_ref, dx_ref, dw_ref):
        j = pl.program_id(0)
        first = pl.program_id(1) == 0
        xv = x_ref[...]
        c = _conv_fwd(xv, w_ref, SHORT_CONV_K)
        s = _silu(c)
        r = lax.rsqrt(jnp.sum(s * s, axis=-1, keepdims=True) + EPS)
        dy = jnp.where(j < HQ, dq_ref[...], jnp.where(j < NQK, dk_ref[...], dv_ref[...]))
        ds_norm = r * dy - s * (r * r * r) * jnp.sum(s * dy, axis=-1, keepdims=True)
        ds = jnp.where(j < NQK, ds_norm, dy)
        dc = ds * _dsilu(c)
        _conv_bwd_w(dc, xv, dw_ref, SHORT_CONV_K, first)
        dx_ref[...] = _conv_bwd_in(dc, w_ref, SHORT_CONV_K).astype(BF16)

    return pl.pallas_call(
        body, name=name, grid=(NB, Bl),
        in_specs=[pl.BlockSpec((S, LANES), lambda j, b: (b, off + j)),
                  pl.BlockSpec((SHORT_CONV_K, LANES), lambda j, b: (0, j)),
                  pl.BlockSpec((S, LANES), lambda j, b: (b, jnp.minimum(j, HQ - 1))),
                  pl.BlockSpec((S, LANES), lambda j, b: (b, jnp.clip(j - HQ, 0, HQ - 1))),
                  pl.BlockSpec((S, LANES), lambda j, b: (b, jnp.clip(j - NQK, 0, H - 1)))],
        out_specs=[pl.BlockSpec((S, LANES), lambda j, b: (b, j)),
                   pl.BlockSpec((SHORT_CONV_K, LANES), lambda j, b: (0, j))],
        out_shape=[jax.ShapeDtypeStruct((Bl * S, NB * LANES), BF16),
                   jax.ShapeDtypeStruct((SHORT_CONV_K, NB * LANES), F32)],
        compiler_params=_cparams(("parallel", "arbitrary")),
    )(p, conv_w, dq, dk, dv)


def _lane_pick(h):
    row = lax.broadcasted_iota(jnp.int32, (LANES, LANES), 0)
    return (row == h).astype(F32)


def _gdn_gate_fwd(pba, a_log, dt_bias, *, Bl, S, H, name):
    def body(alog_ref, dtb_ref, x_ref, g_ref, beta_ref):
        xv = x_ref[...]
        for h in range(H):
            b_raw = _dot(xv, _lane_pick(h), HIGHEST)
            a_raw = _dot(xv, _lane_pick(H + h), HIGHEST)
            beta_ref[h] = _sig(b_raw)
            ea = jnp.exp(jnp.zeros((1, LANES), F32) + alog_ref[h])
            g_ref[h] = -ea * _softplus(a_raw + dtb_ref[h])

    smem = pl.BlockSpec(memory_space=pltpu.SMEM)
    rep = pl.BlockSpec((H, S, LANES), lambda b: (0, b, 0))
    return pl.pallas_call(
        body, name=name, grid=(Bl,),
        in_specs=[smem, smem, pl.BlockSpec((S, LANES), lambda b: (b, 0))],
        out_specs=[rep, rep],
        out_shape=[jax.ShapeDtypeStruct((H, Bl * S, LANES), F32)] * 2,
        compiler_params=_cparams(("parallel",)),
    )(a_log, dt_bias, pba)


def _gdn_gate_bwd(pba, a_log, dt_bias, dg, dbeta, *, Bl, S, H, name):
    HP = 8 * ((H + 7) // 8)

    def body(alog_ref, dtb_ref, x_ref, dg_ref, dbeta_ref, dx_ref, dalog_ref, ddtb_ref):
        first = pl.program_id(0) == 0
        xv = x_ref[...]
        lane = lax.broadcasted_iota(jnp.int32, (S, LANES), 1)
        acc = jnp.zeros((S, LANES), F32)

        @pl.when(first)
        def _():
            dalog_ref[...] = jnp.zeros_like(dalog_ref)
            ddtb_ref[...] = jnp.zeros_like(ddtb_ref)

        for h in range(H):
            b_raw = _dot(xv, _lane_pick(h), HIGHEST)
            a_raw = _dot(xv, _lane_pick(H + h), HIGHEST)
            beta = _sig(b_raw)
            db_raw = dbeta_ref[h] * beta * (1.0 - beta)
            z = a_raw + dtb_ref[h]
            ea = jnp.exp(jnp.zeros((1, LANES), F32) + alog_ref[h])
            dgv = dg_ref[h]
            da_raw = dgv * (-ea) * _sig(z)
            g = -ea * _softplus(z)
            dalog_ref[h:h + 1, :] += jnp.sum(dgv * g, axis=0, keepdims=True)
            ddtb_ref[h:h + 1, :] += jnp.sum(da_raw, axis=0, keepdims=True)
            acc = acc + jnp.where(lane == h, db_raw, 0.0) + jnp.where(lane == H + h, da_raw, 0.0)
        dx_ref[...] = acc.astype(BF16)

    smem = pl.BlockSpec(memory_space=pltpu.SMEM)
    rep = pl.BlockSpec((H, S, LANES), lambda b: (0, b, 0))
    small = pl.BlockSpec((HP, LANES), lambda b: (0, 0))
    return pl.pallas_call(
        body, name=name, grid=(Bl,),
        in_specs=[smem, smem, pl.BlockSpec((S, LANES), lambda b: (b, 0)), rep, rep],
        out_specs=[pl.BlockSpec((S, LANES), lambda b: (b, 0)), small, small],
        out_shape=[jax.ShapeDtypeStruct((Bl * S, LANES), BF16),
                   jax.ShapeDtypeStruct((HP, LANES), F32), jax.ShapeDtypeStruct((HP, LANES), F32)],
        compiler_params=_cparams(("arbitrary",)),
    )(a_log, dt_bias, pba, dg, dbeta)


def _tri_masks():
    ri = lax.broadcasted_iota(jnp.int32, (CHUNK, CHUNK), 0)
    ci = lax.broadcasted_iota(jnp.int32, (CHUNK, CHUNK), 1)
    return ri >= ci, ri > ci, ri == CHUNK - 1


def _tri_inv(L):
    ri = lax.broadcasted_iota(jnp.int32, (CHUNK, CHUNK), 0)
    ci = lax.broadcasted_iota(jnp.int32, (CHUNK, CHUNK), 1)
    T = jnp.where(ri == ci, 1.0, 0.0) - jnp.where((ri >> 1) == (ci >> 1), L, 0.0)
    for lv in range(2, int(math.log2(CHUNK)) + 1):
        O = jnp.where(((ri >> lv) == (ci >> lv)) & ((ri >> (lv - 1)) != (ci >> (lv - 1))), L, 0.0)
        if (1 << lv) <= NEAR_BLOCK:
            T = T - _dot_x3(T, _dot_x3(O, T))
        else:
            Tb = _bf(T)
            T = T - _dot(Tb, _bf(_dot(_bf(O), Tb)))
    return T


def _chunk_local(q, k, v, beta, g):
    ge, gt, last = _tri_masks()
    gam = _dot_mask(ge, g)
    D = jnp.where(ge, jnp.exp(jnp.where(ge, gam - gam.T, 0.0)), 0.0)
    kb = k * beta
    vb = v * beta
    M = _dot_nt(_bf(kb), _bf(k))
    L = jnp.where(gt, M * D, 0.0)
    eg = jnp.exp(gam)
    kbg = kb * eg
    P = _dot_nt(_bf(q), _bf(k))
    QK = jnp.where(ge, P * D, 0.0)
    gl = jnp.sum(jnp.where(last, gam, 0.0), axis=0, keepdims=True)
    el = jnp.exp(gl - gam)
    return dict(ge=ge, gt=gt, last=last, gam=gam, D=D, kb=kb, vb=vb, L=L, eg=eg, kbg=kbg, QK=QK, gl=gl,
                el=el, kd=k * el, qg=q * eg)


def _rowsum(x):
    return jnp.sum(x, axis=-1, keepdims=True)


def _chunk_bwd(q, k, v, beta, g, S, T, u, w, do, dS2):
    c = _chunk_local(q, k, v, beta, g)
    ge, gt, last = c["ge"], c["gt"], c["last"]
    Sb = _bf(S)
    vn = u - _dot(w, Sb)
    dob, vnb, dS2b = _bf(do), _bf(vn), _bf(dS2)
    e_last = jnp.exp(c["gl"])
    dqg = _dot_nt(dob, Sb)
    dS = _dot_tn(_bf(c["qg"]), dob)
    dQK = jnp.where(ge, _dot_nt(dob, vnb), 0.0)
    dvn = _dot_tn(_bf(c["QK"]), dob)
    dS = dS + dS2 * e_last
    de_last = jnp.sum(jnp.sum(dS2 * S, axis=0, keepdims=True), axis=1, keepdims=True)
    dkd = _dot_nt(vnb, dS2b)
    dvn = dvn + _dot(_bf(c["kd"]), dS2b)
    dvnb = _bf(dvn)
    dw = -_dot_nt(dvnb, Sb)
    dS = dS - _dot_tn(w, dvnb)
    dsol = _dot_x3(T, jnp.concatenate([dvn, dw], axis=1), _TN)
    dvb, dkbg = dsol[:, :LANES], dsol[:, LANES:]
    dA = -(_dot_nt(_bf(dvb), _bf(u)) + _dot_nt(_bf(dkbg), w))
    dL = jnp.where(gt, dA, 0.0)
    dM = dL * c["D"]
    dP = dQK * c["D"]
    E = dL * c["L"] + dQK * c["QK"]
    kbf = _bf(k)
    dkb = _dot(_bf(dM), kbf) + dkbg * c["eg"]
    dk = _dot_tn(_bf(dM), _bf(c["kb"])) + _dot_tn(_bf(dP), _bf(q)) + dkd * c["el"] + dkb * beta
    dq = _dot(_bf(dP), kbf) + dqg * c["eg"]
    s_kd = _rowsum(dkd * c["kd"])
    dgam = (_rowsum(E) - _rowsum(E.T) + _rowsum(dqg * c["qg"]) - s_kd + _rowsum(dkbg * c["kbg"]))
    dgl = jnp.sum(s_kd, axis=0, keepdims=True) + de_last * e_last
    dgam_rep = jnp.broadcast_to(dgam, (CHUNK, LANES)) + jnp.where(last, jnp.broadcast_to(dgl, (CHUNK, LANES)), 0.0)
    dg_rep = _dot_mask(ge, dgam_rep, _TN)
    dbeta = _rowsum(dkb * k) + _rowsum(dvb * v)
    dv = dvb * beta
    return dq, dk, dv, jnp.broadcast_to(dbeta, (CHUNK, LANES)), dg_rep, dS


def _gdn_core_fwd(qkv, g, beta, *, Bl, S, KW, VW, name):
    HQ = KW // LANES
    H = VW // LANES
    NC = S // CHUNK
    scale = float(LANES) ** -0.5

    PAIR = 2 if NC % 2 == 0 else 1

    def body(q_ref, k_ref, v_ref, g_ref, beta_ref, o_ref, st_ref, t_ref, u_s, w_s,
             qk_s, qg_s, kd_s, el_s):
        def local(n2, carry):
            for half in range(PAIR):
                n = n2 * PAIR + half
                rows = pl.ds(pl.multiple_of(n * CHUNK, CHUNK), CHUNK)
                q = q_ref[rows, :] * scale
                k = k_ref[rows, :]
                for e in range(2):
                    c = _chunk_local(q, k, v_ref[rows, e * LANES:(e + 1) * LANES], beta_ref[e, rows, :],
                                     g_ref[e, rows, :])
                    T = _tri_inv(c["L"])
                    t_ref[e, rows, :] = T
                    uw = _dot_x3(T, jnp.concatenate([c["vb"], c["kbg"]], axis=1))
                    u_s[e, rows, :] = uw[:, :LANES]
                    w_s[e, rows, :] = _bf(uw[:, LANES:])
                    qk_s[e, rows, :] = _bf(c["QK"])
                    qg_s[e, rows, :] = _bf(c["qg"])
                    kd_s[e, rows, :] = _bf(c["kd"])
                    el_s[e, pl.ds(pl.multiple_of(n * 8, 8), 8), :] = jnp.broadcast_to(jnp.exp(c["gl"]), (8, LANES))
            return carry

        lax.fori_loop(0, NC // PAIR, local, 0)

        def scan(n, states):
            rows = pl.ds(pl.multiple_of(n * CHUNK, CHUNK), CHUNK)
            out = []
            for e in range(2):
                S_in = states[e]
                st_ref[e, n] = S_in
                Sb = _bf(S_in)
                vn = u_s[e, rows, :] - _dot(w_s[e, rows, :], Sb)
                vnb = _bf(vn)
                o_ref[rows, e * LANES:(e + 1) * LANES] = _dot(qg_s[e, rows, :], Sb) + _dot(qk_s[e, rows, :], vnb)
                e_last = el_s[e, pl.ds(pl.multiple_of(n * 8, 8), 1), :]
                out.append(S_in * e_last + _dot_tn(kd_s[e, rows, :], vnb))
            return tuple(out)

        z = jnp.zeros((LANES, LANES), F32)
        lax.fori_loop(0, NC, scan, (z, z))

    rep = pl.BlockSpec((2, S, LANES), lambda b, h: (h, b, 0))
    return pl.pallas_call(
        body, name=name, grid=(Bl, HQ),
        in_specs=[pl.BlockSpec((S, LANES), lambda b, h: (b, h)),
                  pl.BlockSpec((S, LANES), lambda b, h: (b, HQ + h)),
                  pl.BlockSpec((S, 2 * LANES), lambda b, h: (b, HQ + h)), rep, rep],
        out_specs=[pl.BlockSpec((S, 2 * LANES), lambda b, h: (b, h)),
                   pl.BlockSpec((None, 2, NC, LANES, LANES), lambda b, h: (b, h, 0, 0, 0)), rep, rep, rep],
        out_shape=[jax.ShapeDtypeStruct((Bl * S, VW), F32),
                   jax.ShapeDtypeStruct((Bl, H, NC, LANES, LANES), F32),
                   jax.ShapeDtypeStruct((H, Bl * S, LANES), F32),
                   jax.ShapeDtypeStruct((H, Bl * S, LANES), F32),
                   jax.ShapeDtypeStruct((H, Bl * S, LANES), BF16)],
        scratch_shapes=[pltpu.VMEM((2, S, LANES), BF16)] * 3 + [pltpu.VMEM((2, NC * 8, LANES), F32)],
        compiler_params=_cparams(("parallel", "parallel")),
    )(qkv, qkv, qkv, g, beta)


def _gdn_core_bwd(qkv, g, beta, states, tinv, u, w, do, *, Bl, S, KW, VW, name):
    HQ = KW // LANES
    H = VW // LANES
    NC = S // CHUNK
    scale = float(LANES) ** -0.5

    def body(q_ref, k_ref, v_ref, g_ref, beta_ref, st_ref, t_ref, u_ref, w_ref, do_ref,
             dq_ref, dk_ref, dv_ref, dg_ref, dbeta_ref):
        def step(i, dstates):
            n = NC - 1 - i
            rows = pl.ds(pl.multiple_of(n * CHUNK, CHUNK), CHUNK)
            q = q_ref[rows, :] * scale
            k = k_ref[rows, :]
            out = []
            dq_sum = dk_sum = None
            for e in range(2):
                cols = slice(e * LANES, (e + 1) * LANES)
                dq, dk, dv, dbeta, dg, dS = _chunk_bwd(q, k, v_ref[rows, cols], beta_ref[e, rows, :],
                                                       g_ref[e, rows, :], st_ref[e, n], t_ref[e, rows, :],
                                                       u_ref[e, rows, :], w_ref[e, rows, :],
                                                       do_ref[rows, cols], dstates[e])
                dv_ref[rows, cols] = dv
                dg_ref[e, rows, :] = dg
                dbeta_ref[e, rows, :] = dbeta
                dq_sum = dq if dq_sum is None else dq_sum + dq
                dk_sum = dk if dk_sum is None else dk_sum + dk
                out.append(dS)
            dq_ref[rows, :] = dq_sum * scale
            dk_ref[rows, :] = dk_sum
            return tuple(out)

        z = jnp.zeros((LANES, LANES), F32)
        lax.fori_loop(0, NC, step, (z, z))

    rep = pl.BlockSpec((2, S, LANES), lambda b, h: (h, b, 0))
    seq = pl.BlockSpec((S, LANES), lambda b, h: (b, h))
    seq2 = pl.BlockSpec((S, 2 * LANES), lambda b, h: (b, h))
    return pl.pallas_call(
        body, name=name, grid=(Bl, HQ),
        in_specs=[seq, pl.BlockSpec((S, LANES), lambda b, h: (b, HQ + h)),
                  pl.BlockSpec((S, 2 * LANES), lambda b, h: (b, HQ + h)), rep, rep,
                  pl.BlockSpec((None, 2, NC, LANES, LANES), lambda b, h: (b, h, 0, 0, 0)), rep, rep, rep, seq2],
        out_specs=[seq, seq, seq2, rep, rep],
        out_shape=[jax.ShapeDtypeStruct((Bl * S, KW), F32), jax.ShapeDtypeStruct((Bl * S, KW), F32),
                   jax.ShapeDtypeStruct((Bl * S, VW), F32),
                   jax.ShapeDtypeStruct((H, Bl * S, LANES), F32), jax.ShapeDtypeStruct((H, Bl * S, LANES), F32)],
        compiler_params=_cparams(("parallel", "parallel")),
    )(qkv, qkv, qkv, g, beta, states, tinv, u, w, do)


def _gdn_out_fwd(o, p, norm_g, out_a, *, CC, VW, name, tr=256):
    T = o.shape[0]
    tr = min(tr, T)
    H = VW // LANES
    zoff = p.shape[1] // VW - 1

    def body(o_ref, z_ref, ng_ref, a_ref, mix_ref):
        mix_ref[:, :CC] = a_ref[...]
        for h in range(H):
            cols = slice(h * LANES, (h + 1) * LANES)
            ov = o_ref[:, cols]
            r = lax.rsqrt(jnp.mean(ov * ov, axis=-1, keepdims=True) + EPS)
            mix_ref[:, CC + h * LANES:CC + (h + 1) * LANES] = (ov * r * ng_ref[...] * _silu(z_ref[:, cols])).astype(BF16)

    return pl.pallas_call(
        body, name=name, grid=(T // tr,),
        in_specs=[pl.BlockSpec((tr, VW), lambda i: (i, 0)), pl.BlockSpec((tr, VW), lambda i: (i, zoff)),
                  pl.BlockSpec((1, LANES), lambda i: (0, 0)), pl.BlockSpec((tr, CC), lambda i: (i, 0))],
        out_specs=pl.BlockSpec((tr, CC + VW), lambda i: (i, 0)),
        out_shape=jax.ShapeDtypeStruct((T, CC + VW), BF16),
        compiler_params=_cparams(("parallel",)),
    )(o, p, norm_g.reshape(1, LANES), out_a)


def _gdn_out_bwd(o, p, norm_g, dmix, *, CC, VW, name, tr=256):
    T = o.shape[0]
    tr = min(tr, T)
    H = VW // LANES
    zoff = p.shape[1] // VW - 1

    def body(o_ref, z_ref, ng_ref, dmix_ref, do_ref, dz_ref, da_ref, dng_ref, dpb_ref):
        first = pl.program_id(0) == 0
        da = dmix_ref[:, :CC]
        da_ref[...] = da.astype(BF16)
        _acc_row(dpb_ref, 0, jnp.sum(da, axis=0, keepdims=True), first)
        ng = ng_ref[...]
        dng = jnp.zeros((1, LANES), F32)
        for h in range(H):
            cols = slice(h * LANES, (h + 1) * LANES)
            ov = o_ref[:, cols]
            zv = z_ref[:, cols]
            dout = dmix_ref[:, CC + h * LANES:CC + (h + 1) * LANES]
            r = lax.rsqrt(jnp.mean(ov * ov, axis=-1, keepdims=True) + EPS)
            on = ov * r * ng
            don = dout * _silu(zv)
            dz_ref[:, cols] = (dout * on * _dsilu(zv)).astype(BF16)
            dng = dng + jnp.sum(don * ov * r, axis=0, keepdims=True)
            dong = don * ng
            do_ref[:, cols] = r * dong - ov * (r * r * r) * jnp.mean(dong * ov, axis=-1, keepdims=True)
        _acc_row(dng_ref, 0, dng, first)

    return pl.pallas_call(
        body, name=name, grid=(T // tr,),
        in_specs=[pl.BlockSpec((tr, VW), lambda i: (i, 0)), pl.BlockSpec((tr, VW), lambda i: (i, zoff)),
                  pl.BlockSpec((1, LANES), lambda i: (0, 0)), pl.BlockSpec((tr, CC + VW), lambda i: (i, 0))],
        out_specs=[pl.BlockSpec((tr, VW), lambda i: (i, 0)), pl.BlockSpec((tr, VW), lambda i: (i, 0)),
                   pl.BlockSpec((tr, CC), lambda i: (i, 0)), pl.BlockSpec((1, LANES), lambda i: (0, 0)),
                   pl.BlockSpec((1, CC), lambda i: (0, 0))],
        out_shape=[jax.ShapeDtypeStruct((T, VW), F32), jax.ShapeDtypeStruct((T, VW), BF16),
                   jax.ShapeDtypeStruct((T, CC), BF16), jax.ShapeDtypeStruct((1, LANES), F32),
                   jax.ShapeDtypeStruct((1, CC), F32)],
        compiler_params=_cparams(("arbitrary",)),
    )(o, p, norm_g.reshape(1, LANES), dmix)


FFN_CW = 256


def _ffn_act_fwd(gu, conv_w, conv_b, *, Bl, S, name):
    FF = gu.shape[2]
    cw = min(FFN_CW, FF)

    def body(g_ref, u_ref, w_ref, b_ref, a_ref):
        gc = _conv_fwd(g_ref[...], w_ref, FFN_CONV_K) + b_ref[...]
        a_ref[...] = (_silu(gc) * u_ref[...]).astype(BF16)

    return pl.pallas_call(
        body, name=name, grid=(Bl, FF // cw),
        in_specs=[pl.BlockSpec((None, S, cw), lambda b, j: (0, b, j)),
                  pl.BlockSpec((None, S, cw), lambda b, j: (1, b, j)),
                  pl.BlockSpec((FFN_CONV_K, cw), lambda b, j: (0, j)),
                  pl.BlockSpec((1, cw), lambda b, j: (0, j))],
        out_specs=pl.BlockSpec((S, cw), lambda b, j: (b, j)),
        out_shape=jax.ShapeDtypeStruct((Bl * S, FF), BF16),
        compiler_params=_cparams(("parallel", "parallel")),
    )(gu, gu, conv_w, conv_b.reshape(1, FF))


def _ffn_act_bwd(gu, conv_w, conv_b, da, *, Bl, S, name):
    FF = gu.shape[2]
    cw = min(FFN_CW, FF)

    def body(g_ref, u_ref, w_ref, b_ref, da_ref, dgu_ref, dw_ref, db_ref):
        first = pl.program_id(1) == 0
        gate = g_ref[...]
        gc = _conv_fwd(gate, w_ref, FFN_CONV_K) + b_ref[...]
        dav = da_ref[...]
        dgu_ref[1] = (dav * _silu(gc)).astype(BF16)
        dgc = dav * u_ref[...] * _dsilu(gc)
        _acc_row(db_ref, 0, jnp.sum(dgc, axis=0, keepdims=True), first)
        _conv_bwd_w(dgc, gate, dw_ref, FFN_CONV_K, first)
        dgu_ref[0] = _conv_bwd_in(dgc, w_ref, FFN_CONV_K).astype(BF16)

    return pl.pallas_call(
        body, name=name, grid=(FF // cw, Bl),
        in_specs=[pl.BlockSpec((None, S, cw), lambda j, b: (0, b, j)),
                  pl.BlockSpec((None, S, cw), lambda j, b: (1, b, j)),
                  pl.BlockSpec((FFN_CONV_K, cw), lambda j, b: (0, j)),
                  pl.BlockSpec((1, cw), lambda j, b: (0, j)),
                  pl.BlockSpec((S, cw), lambda j, b: (b, j))],
        out_specs=[pl.BlockSpec((2, S, cw), lambda j, b: (0, b, j)),
                   pl.BlockSpec((FFN_CONV_K, cw), lambda j, b: (0, j)),
                   pl.BlockSpec((1, cw), lambda j, b: (0, j))],
        out_shape=[jax.ShapeDtypeStruct((2, Bl * S, FF), BF16),
                   jax.ShapeDtypeStruct((FFN_CONV_K, FF), F32), jax.ShapeDtypeStruct((1, FF), F32)],
        compiler_params=_cparams(("parallel", "arbitrary")),
    )(gu, gu, conv_w, conv_b.reshape(1, FF), da)


def _layer_dims(W):
    CC = W["conv_pw_b"].shape[0]
    VW = W["mix_norm_g"].shape[0] - CC
    KW = (W["gdn_conv_w"].shape[1] - VW) // 2
    return CC, KW, VW


def _layer_fwd(l, x, W, Bl, S, fetch):
    CC, KW, VW = _layer_dims(W)
    H = VW // LANES
    w_in_main, w_in_ba = fetch(l, "w_in", x)
    h1 = _rms_fwd(x, W["mix_norm_g"], name="rms1_fwd")
    p = _mm(h1, w_in_main, name="mm_in")
    pba = _mm(h1, w_in_ba, name="mm_in_ba")
    u3, u1 = _conf_fwd(p, W["conv_dw_w"], W["conv_dw_b"], W["conv_ln_g"], W["conv_ln_b"], Bl=Bl, S=S, CC=CC,
                       name="conf_fwd")
    conv_pw_w = fetch(l, "conv_pw_w", u3)
    out_a = _mm(u3, conv_pw_w, bias=W["conv_pw_b"], out_dtype=BF16, name="mm_pw")
    qkv = _gdn_pre_fwd(p, W["gdn_conv_w"], Bl=Bl, S=S, CC=CC, KW=KW, VW=VW, name="gdn_pre_fwd")
    g, beta = _gdn_gate_fwd(pba, W["gdn_a_log"], W["gdn_dt_bias"], Bl=Bl, S=S, H=H, name="gdn_gate_fwd")
    o, states, tinv, gdn_u, gdn_w = _gdn_core_fwd(qkv, g, beta, Bl=Bl, S=S, KW=KW, VW=VW, name="gdn_core_fwd")
    mix = _gdn_out_fwd(o, p, W["gdn_norm_g"], out_a, CC=CC, VW=VW, name="gdn_out_fwd")
    w_out = fetch(l, "w_out", mix)
    x1 = _mm(mix, w_out, res=x, name="mm_out")
    h2 = _rms_fwd(x1, W["ffn_norm_g"], name="rms2_fwd")
    w_up = fetch(l, "w_up", h2)
    gu = _mm(h2, w_up, out_blocks=2, tn=w_up.shape[2], name="mm_up")
    a = _ffn_act_fwd(gu, W["ffn_conv_w"], W["ffn_conv_b"], Bl=Bl, S=S, name="ffn_act_fwd")
    w_down = fetch(l, "w_down", a)
    x2 = _mm(a, w_down, res=x1, name="mm_down")
    saved = dict(x=x, h1=h1, p=p, pba=pba, u1=u1, u3=u3, qkv=qkv, g=g, beta=beta, o=o, states=states, tinv=tinv,
                 gdn_u=gdn_u, gdn_w=gdn_w, mix=mix, x1=x1, h2=h2, gu=gu, a=a, w_in_main=w_in_main, w_in_ba=w_in_ba, conv_pw_w=conv_pw_w,
                 w_out=w_out, w_up=w_up, w_down=w_down)
    return x2, saved


def _layer_bwd(l, dx2, dx2b, W, A, Bl, S, sink):
    CC, KW, VW = _layer_dims(W)
    H = VW // LANES
    G = {}
    upw = A["w_up"].shape[2]
    da = _mm(dx2b, A["w_down"], tb=True, tn=upw, name="mm_down_dx")
    da = sink(l, "w_down", _mm(A["a"], dx2b, ta=True, out_dtype=BF16, tm=upw, name="mm_down_dw"), da)
    dgu, G["ffn_conv_w"], G["ffn_conv_b"] = _ffn_act_bwd(A["gu"], W["ffn_conv_w"], W["ffn_conv_b"], da,
                                                         Bl=Bl, S=S, name="ffn_act_bwd")
    dh2 = _mm(dgu, A["w_up"], tb=True, tk=upw, tn=2048, name="mm_up_dx")
    dh2 = sink(l, "w_up", _mm(A["h2"], dgu, ta=True, out_dtype=BF16, out_blocks=N_DEV, tn=upw, name="mm_up_dw"),
               dh2)
    dx1, dx1b, G["ffn_norm_g"] = _rms_bwd(A["x1"], W["ffn_norm_g"], dh2, dx2, name="rms2_bwd")
    dmix = _mm(dx1b, A["w_out"], tb=True, name="mm_out_dx")
    dmix = sink(l, "w_out", _mm(A["mix"], dx1b, ta=True, out_dtype=BF16, name="mm_out_dw"), dmix)
    do, dz, dout_a, G["gdn_norm_g"], G["conv_pw_b"] = _gdn_out_bwd(A["o"], A["p"], W["gdn_norm_g"], dmix,
                                                                   CC=CC, VW=VW, name="gdn_out_bwd")
    dq, dk, dv, dg, dbeta = _gdn_core_bwd(A["qkv"], A["g"], A["beta"], A["states"], A["tinv"], A["gdn_u"],
                                          A["gdn_w"], do, Bl=Bl, S=S, KW=KW, VW=VW, name="gdn_core_bwd")
    dpba, dalog, ddtb = _gdn_gate_bwd(A["pba"], W["gdn_a_log"], W["gdn_dt_bias"], dg, dbeta, Bl=Bl, S=S, H=H,
                                      name="gdn_gate_bwd")
    G["gdn_a_log"], G["gdn_dt_bias"] = dalog[:H, 0], ddtb[:H, 0]
    dqkv, G["gdn_conv_w"] = _gdn_pre_bwd(A["p"], W["gdn_conv_w"], dq, dk, dv, Bl=Bl, S=S, CC=CC, KW=KW, VW=VW,
                                         name="gdn_pre_bwd")
    du3 = _mm(dout_a, A["conv_pw_w"], tb=True, name="mm_pw_dx")
    du3 = sink(l, "conv_pw_w", _mm(A["u3"], dout_a, ta=True, out_dtype=BF16, name="mm_pw_dw"), du3)
    dav, dag, G["conv_dw_w"], G["conv_dw_b"], G["conv_ln_g"], G["conv_ln_b"] = _conf_bwd(
        A["p"], A["u1"], W["conv_dw_w"], W["conv_ln_g"], W["conv_ln_b"], du3, Bl=Bl, S=S, CC=CC, name="conf_bwd")
    dp = jnp.concatenate([dav, dag, dqkv, dz], axis=1)
    dh1 = _mm(dpba, A["w_in_ba"], tb=True, name="mm_in_ba_dx")
    dh1 = _mm(dp, A["w_in_main"], tb=True, res=dh1, name="mm_in_dx")
    dh1 = sink(l, "w_in", (_mm(A["h1"], dp, ta=True, out_dtype=BF16, name="mm_in_dw"),
                           _mm(A["h1"], dpba, ta=True, out_dtype=BF16, name="mm_in_ba_dw")), dh1)
    dx, dxb, G["mix_norm_g"] = _rms_bwd(A["x"], W["mix_norm_g"], dh1, dx1, name="rms1_bwd")
    return dx, dxb, G


def _local_step(x, target, Ws, final_norm_g, fetch, sink):
    Bl, S, D = x.shape
    xt = x.reshape(Bl * S, D)
    acts = []
    for l, W in enumerate(Ws):
        xt, A = _layer_fwd(l, xt, W, Bl, S, fetch)
        acts.append(A)
    loss, dx, dxb, dgf = _loss_head(xt, final_norm_g, target.reshape(Bl * S, D), name="loss_head")
    grads = [None] * len(Ws)
    for l in reversed(range(len(Ws))):
        dx, dxb, grads[l] = _layer_bwd(l, dx, dxb, Ws[l], acts[l], Bl, S, sink)
    return loss[0, 0], dx.reshape(Bl, S, D), grads, dgf.reshape(D)


def _mesh_pos():
    return lax.axis_index("x"), lax.axis_index("y"), lax.axis_index("c")


def _dev_index(px, py, pc):
    return 4 * px + 2 * py + pc


_ANY = pl.BlockSpec(memory_space=pl.ANY)


def _all_gather(arrs, *, name):
    n = len(arrs)

    def body(*refs):
        ins, outs = refs[:n], refs[n:2 * n]
        send_sems, recv_sems, local_sems = refs[2 * n:]
        x, y, c = _mesh_pos()
        me, sibling = (x, y, c), (x, y, 1 - c)
        chips = [(1 - x, y), (x, 1 - y), (1 - x, 1 - y)]

        def copy(a, k, block, to, src=None):
            dst = outs[a].at[_dev_index(*block)]
            return pltpu.make_async_remote_copy(
                src_ref=dst if src is None else src, dst_ref=dst,
                send_sem=send_sems.at[a, k], recv_sem=recv_sems.at[a, k],
                device_id=to, device_id_type=MESH)

        mine = [pltpu.make_async_copy(ins[a], outs[a].at[_dev_index(*me)], local_sems.at[a]) for a in range(n)]
        for cp in mine:
            cp.start()
        first = []
        for a in range(n):
            first.append(copy(a, 0, me, sibling, src=ins[a]))
            first += [copy(a, 1 + j, me, (*chip, c), src=ins[a]) for j, chip in enumerate(chips)]
        for cp in first:
            cp.start()
        passed = []
        for a in range(n):
            for j, chip in enumerate(chips):
                copy(a, 1 + j, (*chip, c), me).wait_recv()
                fwd = copy(a, 4 + j, (*chip, c), sibling)
                fwd.start()
                passed.append(fwd)
        for a in range(n):
            copy(a, 0, sibling, me).wait_recv()
            for j, chip in enumerate(chips):
                copy(a, 4 + j, (*chip, 1 - c), me).wait_recv()
        for cp in first + passed:
            cp.wait_send()
        for cp in mine:
            cp.wait()

    return pl.pallas_call(
        body, name=name,
        in_specs=[_ANY] * n, out_specs=[_ANY] * n,
        out_shape=[jax.ShapeDtypeStruct((N_DEV,) + a.shape, a.dtype) for a in arrs],
        scratch_shapes=[pltpu.SemaphoreType.DMA((n, N_DEV - 1)), pltpu.SemaphoreType.DMA((n, N_DEV - 1)),
                        pltpu.SemaphoreType.DMA((n,))],
    )(*arrs)


def _peers(x, y, c):
    flip = lambda v, f: 1 - v if f else v
    return [(flip(x, p & 4), flip(y, p & 2), flip(c, p & 1)) for p in range(1, N_DEV)]


GATHER_ID, SCATTER_ID = 1, 2
_SEQUENCER = dict(axis_name="sequencer", num_cores=1)


def _handshake(peers):
    barrier = pltpu.get_barrier_semaphore()
    for peer in peers:
        pl.semaphore_signal(barrier, inc=1, device_id=peer, device_id_type=MESH)
    pl.semaphore_wait(barrier, len(peers))


def _sc_gather(src, *, name):
    def body(src_ref, zone_ref, send_sems, recv_sems, local_sem):
        x, y, c = _mesh_pos()
        me, sibling = (x, y, c), (x, y, 1 - c)
        chips = [(1 - x, y), (x, 1 - y), (1 - x, 1 - y)]
        _handshake([sibling] + [(*chip, c) for chip in chips])

        def copy(k, block, to, from_src=False):
            dst = zone_ref.at[_dev_index(*block)]
            return pltpu.make_async_remote_copy(
                src_ref=src_ref if from_src else dst, dst_ref=dst, send_sem=send_sems.at[k], recv_sem=recv_sems.at[k],
                device_id=to, device_id_type=MESH)

        mine = pltpu.make_async_copy(src_ref, zone_ref.at[_dev_index(*me)], local_sem)
        mine.start()
        first = [copy(1 + j, me, (*chip, c), from_src=True) for j, chip in enumerate(chips)]
        first.append(copy(0, me, sibling, from_src=True))
        for cp in first:
            cp.start()
        passed = []
        for j, chip in enumerate(chips):
            copy(1 + j, (*chip, c), me).wait_recv()
            fwd = copy(4 + j, (*chip, c), sibling)
            fwd.start()
            passed.append(fwd)
        copy(0, sibling, me).wait_recv()
        for j, chip in enumerate(chips):
            copy(4 + j, (*chip, 1 - c), me).wait_recv()
        for cp in first + passed:
            cp.wait_send()
        mine.wait()

    return pl.kernel(
        body, name=name,
        out_type=jax.ShapeDtypeStruct((N_DEV,) + src.shape, src.dtype),
        mesh=plsc.ScalarSubcoreMesh(**_SEQUENCER),
        scratch_types=[pltpu.SemaphoreType.DMA((N_DEV - 1,)), pltpu.SemaphoreType.DMA((N_DEV - 1,)),
                       pltpu.SemaphoreType.DMA],
        compiler_params=pltpu.CompilerParams(collective_id=GATHER_ID),
    )(src)


def _sc_scatter(part, *, name):
    def body(src_ref, zone_ref, send_sems, recv_sems, local_sem):
        x, y, c = _mesh_pos()
        me = _dev_index(x, y, c)
        peers = _peers(x, y, c)
        _handshake(peers)
        mine = pltpu.make_async_copy(src_ref.at[me], zone_ref.at[me], local_sem)
        mine.start()
        sends = [pltpu.make_async_remote_copy(
            src_ref=src_ref.at[_dev_index(*peer)], dst_ref=zone_ref.at[me], send_sem=send_sems.at[k],
            recv_sem=recv_sems.at[k], device_id=peer, device_id_type=MESH) for k, peer in enumerate(peers)]
        for cp in sends:
            cp.start()
        for k, peer in enumerate(peers):
            pltpu.make_async_remote_copy(
                src_ref=src_ref.at[me], dst_ref=zone_ref.at[_dev_index(*peer)], send_sem=send_sems.at[k],
                recv_sem=recv_sems.at[k], device_id=peer, device_id_type=MESH).wait_recv()
        for cp in sends:
            cp.wait_send()
        mine.wait()

    return pl.kernel(
        body, name=name,
        out_type=jax.ShapeDtypeStruct(part.shape, part.dtype),
        mesh=plsc.ScalarSubcoreMesh(**_SEQUENCER),
        scratch_types=[pltpu.SemaphoreType.DMA((N_DEV - 1,)), pltpu.SemaphoreType.DMA((N_DEV - 1,)),
                       pltpu.SemaphoreType.DMA],
        compiler_params=pltpu.CompilerParams(collective_id=SCATTER_ID),
    )(part)


def _adamw_math(w, g, m, v):
    m2 = ADAM_B1 * m + (1.0 - ADAM_B1) * g
    v2 = ADAM_B2 * v + (1.0 - ADAM_B2) * (g * g)
    m_hat = m2 / (1.0 - ADAM_B1 ** ADAM_STEP)
    v_hat = v2 / (1.0 - ADAM_B2 ** ADAM_STEP)
    delta = -ADAM_LR * (m_hat / (jnp.sqrt(v_hat) + ADAM_EPS) + ADAM_WD * w)
    return delta, m2, v2


def _adamw_big(l, w, m, v, recv, prev, *, name, tr=128):
    L, R, C = w.shape
    tr = next(t for t in range(min(tr, R), 0, -16) if R % t == 0)

    def body(w_ref, m_ref, v_ref, r_ref, *rest):
        g_ref, d_ref, m2_ref, v2_ref = rest[-4:]
        g = r_ref[0].astype(F32)
        for s in range(1, N_DEV):
            g = g + r_ref[s].astype(F32)
        g_ref[...] = g
        d_ref[...], m2_ref[...], v2_ref[...] = _adamw_math(w_ref[...], g, m_ref[...], v_ref[...])

    wspec = pl.BlockSpec((None, tr, C), lambda i: (l, i, 0))
    prev = list(prev) if prev is not None else []
    return pl.pallas_call(
        body, name=name, grid=(R // tr,),
        in_specs=[wspec, wspec, wspec, pl.BlockSpec((N_DEV, tr, C), lambda i: (0, i, 0))] + [_ANY] * len(prev),
        out_specs=[wspec] * 4,
        out_shape=[jax.ShapeDtypeStruct((L, R, C), F32)] * 4,
        input_output_aliases={4 + j: j for j in range(len(prev))},
        compiler_params=_cparams(("parallel",)),
    )(w, m, v, recv.reshape(N_DEV, R, C), *prev)


def _sum_slots(gathered, *, name):
    _, R, C = gathered.shape

    def body(r_ref, o_ref):
        g = r_ref[0]
        for s in range(1, N_DEV):
            g = g + r_ref[s]
        o_ref[...] = g

    return pl.pallas_call(body, name=name, out_shape=jax.ShapeDtypeStruct((R, C), F32))(gathered)


def _adamw_small(w, g, m, v, *, name):
    def body(w_ref, g_ref, m_ref, v_ref, d_ref, m2_ref, v2_ref):
        d_ref[...], m2_ref[...], v2_ref[...] = _adamw_math(w_ref[...], g_ref[...], m_ref[...], v_ref[...])

    return pl.pallas_call(body, name=name, out_shape=[jax.ShapeDtypeStruct(w.shape, F32)] * 3)(w, g, m, v)


def _pack(arrs):
    flat = []
    for a in arrs:
        a = a.reshape(-1).astype(F32)
        flat.append(jnp.pad(a, (0, (-a.shape[0]) % LANES)))
    out = jnp.concatenate(flat)
    out = jnp.pad(out, (0, (-out.shape[0]) % (8 * LANES)))
    return out.reshape(-1, LANES)


def _unpack(packed, shapes):
    flat = packed.reshape(-1)
    out, pos = [], 0
    for s in shapes:
        size = math.prod(s)
        out.append(flat[pos:pos + size].reshape(s))
        pos += size + (-size) % LANES
    return out


BIG = ("w_in", "conv_pw_w", "w_out", "w_up", "w_down")
SMALL_SHARDED = ("conv_dw_w", "gdn_conv_w", "ffn_conv_w")
SMALL_REPLICATED = ("mix_norm_g", "conv_dw_b", "conv_ln_g", "conv_ln_b", "conv_pw_b", "gdn_a_log", "gdn_dt_bias",
                    "gdn_norm_g", "ffn_norm_g", "ffn_conv_b")
WEIGHTS = ("mix_norm_g", "w_in", "conv_dw_w", "conv_dw_b", "conv_ln_g", "conv_ln_b", "conv_pw_w", "conv_pw_b",
           "gdn_conv_w", "gdn_a_log", "gdn_dt_bias", "gdn_norm_g", "w_out", "ffn_norm_g", "w_up", "ffn_conv_w",
           "ffn_conv_b", "w_down", "final_norm_g")


def _train_step(x, target, w, m, v):
    L = w["w_in"].shape[0]
    D = x.shape[-1]
    xi, yi, ci = _mesh_pos()
    me = _dev_index(xi, yi, ci)

    small_full = {}
    for n in SMALL_SHARDED:
        g_ = _sc_gather(w[n], name=f"gather_{n}")
        small_full[n] = jnp.moveaxis(g_, 0, 2).reshape(L, g_.shape[2], N_DEV * g_.shape[3])
    gathered = {}

    def launch(l, after=None):
        for n in BIG:
            src = w[n][l].astype(BF16)
            if after is not None:
                src = lax.optimization_barrier((src, after))[0]
            gathered[n, l] = _sc_gather(src, name=f"gather_{n}_{l}")

    launch(0)
    Ws = []
    for l in range(L):
        W = {n: w[n][l] for n in SMALL_REPLICATED}
        W.update({n: small_full[n][l] for n in SMALL_SHARDED})
        Ws.append(W)

    def fetch(l, n, after):
        if n == "conv_pw_w" and l + 1 < L:
            launch(l + 1, after)
        g_ = lax.optimization_barrier((gathered[n, l], after))[0]
        if n == "w_up":
            return g_
        if n == "w_in":
            w_in = jnp.moveaxis(g_, 0, 1).reshape(D, -1)
            n_main = (w_in.shape[1] // LANES) * LANES
            return w_in[:, :n_main], jnp.pad(w_in[:, n_main:], ((0, 0), (0, LANES - (w_in.shape[1] - n_main))))
        return g_.reshape(g_.shape[0] * g_.shape[1], g_.shape[2])

    started = []
    res = {}
    SCATTERS_IN_FLIGHT = 2

    def consume(chain):
        n, l, recv = started.pop(0)
        res[n] = _adamw_big(l, w[n], m[n], v[n], recv, res.get(n), name=f"adamw_{n}")
        if chain is None:
            return None
        tied = lax.optimization_barrier((chain, *res[n]))
        res[n] = list(tied[1:])
        return tied[0]

    def sink(l, n, g_, chain):
        g_, chain = lax.optimization_barrier((g_, chain))
        if len(started) >= SCATTERS_IN_FLIGHT:
            chain = consume(chain)
        if n == "w_in":
            g_main, g_ba = g_
            g_ = jnp.concatenate([g_main, g_ba[:, :w["w_in"].shape[2] * N_DEV - g_main.shape[1]]], axis=1)
            part = jnp.moveaxis(g_.reshape(D, N_DEV, -1), 1, 0)
        elif n == "w_up":
            part = g_
        else:
            part = g_.reshape(N_DEV, -1, g_.shape[1])
        started.append((n, l, _sc_scatter(part, name=f"scatter_{n}_{l}")))
        return chain

    loss, grad_x, G, d_final = _local_step(x, target, Ws, w["final_norm_g"], fetch, sink)
    loss = lax.psum(loss, ("x", "y", "c"))

    out = {k: {} for k in ("grad", "delta", "new_m", "new_v")}
    while started:
        consume(None)
    for n in BIG:
        for j, k in enumerate(("grad", "delta", "new_m", "new_v")):
            out[k][n] = res[n][j]

    small_names = [n for n in WEIGHTS if n not in BIG]
    partial = []
    for n in small_names:
        if n == "final_norm_g":
            partial.append(d_final)
        else:
            partial.append(jnp.stack([G[l][n].reshape(Ws[l][n].shape) for l in range(L)]))
    shapes = [p_.shape for p_ in partial]
    gathered = _all_gather([_pack(partial)], name="all_gather_small_grads")[0]
    full = dict(zip(small_names, _unpack(_sum_slots(gathered, name="sum_small_grads"), shapes)))
    for n in SMALL_SHARDED:
        width = w[n].shape[-1]
        full[n] = lax.dynamic_slice_in_dim(full[n], me * width, width, axis=2)
    loc_shapes = [w[n].shape for n in small_names]
    g_pack = _pack([full[n] for n in small_names])
    res = _adamw_small(_pack([w[n] for n in small_names]), g_pack, _pack([m[n] for n in small_names]),
                       _pack([v[n] for n in small_names]), name="adamw_small")
    for k, packed in zip(("grad", "delta", "new_m", "new_v"), (g_pack,) + tuple(res)):
        out[k].update(dict(zip(small_names, _unpack(packed, loc_shapes))))
    return loss, grad_x, out


def kernel(x, mix_norm_g, w_in, conv_dw_w, conv_dw_b, conv_ln_g, conv_ln_b, conv_pw_w, conv_pw_b, gdn_conv_w, gdn_a_log, gdn_dt_bias, gdn_norm_g, w_out, ffn_norm_g, w_up, ffn_conv_w, ffn_conv_b, w_down, final_norm_g, loss_target, m_mix_norm_g, m_w_in, m_conv_dw_w, m_conv_dw_b, m_conv_ln_g, m_conv_ln_b, m_conv_pw_w, m_conv_pw_b, m_gdn_conv_w, m_gdn_a_log, m_gdn_dt_bias, m_gdn_norm_g, m_w_out, m_ffn_norm_g, m_w_up, m_ffn_conv_w, m_ffn_conv_b, m_w_down, m_final_norm_g, v_mix_norm_g, v_w_in, v_conv_dw_w, v_conv_dw_b, v_conv_ln_g, v_conv_ln_b, v_conv_pw_w, v_conv_pw_b, v_gdn_conv_w, v_gdn_a_log, v_gdn_dt_bias, v_gdn_norm_g, v_w_out, v_ffn_norm_g, v_w_up, v_ffn_conv_w, v_ffn_conv_b, v_w_down, v_final_norm_g):
    w = dict(zip(WEIGHTS, (mix_norm_g, w_in, conv_dw_w, conv_dw_b, conv_ln_g, conv_ln_b, conv_pw_w, conv_pw_b, gdn_conv_w,
                           gdn_a_log, gdn_dt_bias, gdn_norm_g, w_out, ffn_norm_g, w_up, ffn_conv_w, ffn_conv_b, w_down,
                           final_norm_g)))
    m = dict(zip(WEIGHTS, (m_mix_norm_g, m_w_in, m_conv_dw_w, m_conv_dw_b, m_conv_ln_g, m_conv_ln_b, m_conv_pw_w,
                           m_conv_pw_b, m_gdn_conv_w, m_gdn_a_log, m_gdn_dt_bias, m_gdn_norm_g, m_w_out, m_ffn_norm_g,
                           m_w_up, m_ffn_conv_w, m_ffn_conv_b, m_w_down, m_final_norm_g)))
    v = dict(zip(WEIGHTS, (v_mix_norm_g, v_w_in, v_conv_dw_w, v_conv_dw_b, v_conv_ln_g, v_conv_ln_b, v_conv_pw_w,
                           v_conv_pw_b, v_gdn_conv_w, v_gdn_a_log, v_gdn_dt_bias, v_gdn_norm_g, v_w_out, v_ffn_norm_g,
                           v_w_up, v_ffn_conv_w, v_ffn_conv_b, v_w_down, v_final_norm_g)))
    loss, grad_x, out = _train_step(x, loss_target, w, m, v)
    return (loss, grad_x, *[out["grad"][n] for n in WEIGHTS], *[out["delta"][n] for n in WEIGHTS],
            *[out["new_m"][n] for n in WEIGHTS], *[out["new_v"][n] for n in WEIGHTS])
```

```python
import functools
import math

import jax
import jax.numpy as jnp
from jax import lax
from jax.experimental import pallas as pl
from jax.experimental.pallas import tpu as pltpu
from jax.experimental.pallas import tpu_sc as plsc

F32 = jnp.float32
BF16 = jnp.bfloat16
HIGHEST = lax.Precision.HIGHEST
MESH = pl.DeviceIdType.MESH

EPS = 1e-6
LANES = 128
CHUNK = 128
NEAR_BLOCK = 32
CONV_K = 31
SHORT_CONV_K = 4
FFN_CONV_K = 3
N_DEV = 8
VMEM_LIMIT = 56 * 1024 * 1024

ADAM_LR = 0.001
ADAM_B1 = 0.9
ADAM_B2 = 0.999
ADAM_EPS = 1e-08
ADAM_WD = 0.01
ADAM_STEP = 10


def _cparams(sem):
    return pltpu.CompilerParams(dimension_semantics=sem, vmem_limit_bytes=VMEM_LIMIT)


def _sig(x):
    return 1.0 / (1.0 + jnp.exp(-x))


def _silu(x):
    return x * _sig(x)


def _dsilu(x):
    s = _sig(x)
    return s * (1.0 + x * (1.0 - s))


def _softplus(x):
    return jnp.maximum(x, 0.0) + jnp.log1p(jnp.exp(-jnp.abs(x)))


def _dot(a, b, precision=None):
    return jnp.dot(a, b, preferred_element_type=F32, precision=precision)


def _dot_nt(a, b):
    return lax.dot_general(a, b, (((1,), (1,)), ((), ())), preferred_element_type=F32)


def _dot_tn(a, b):
    return lax.dot_general(a, b, (((0,), (0,)), ((), ())), preferred_element_type=F32)


def _bf(x):
    return x.astype(BF16)


_NN = (((1,), (0,)), ((), ()))
_TN = (((0,), (0,)), ((), ()))


def _split2(x):
    hi = _bf(x)
    return hi, _bf(x - hi.astype(F32))


def _dot_x3(a, b, dn=_NN):
    ah, al = _split2(a)
    bh, bl = _split2(b)
    f = lambda p, q: lax.dot_general(p, q, dn, preferred_element_type=F32)
    return f(ah, bh) + (f(al, bh) + f(ah, bl))


def _dot_mask(mask, x, dn=_NN):
    mb = _bf(mask)
    hi, lo = _split2(x)
    lo2 = _bf(x - hi.astype(F32) - lo.astype(F32))
    f = lambda q: lax.dot_general(mb, q, dn, preferred_element_type=F32)
    return f(hi) + (f(lo) + f(lo2))


def _shift_down(u, s):
    if s == 0:
        return u
    row = lax.broadcasted_iota(jnp.int32, u.shape, 0)
    return jnp.where(row >= s, pltpu.roll(u, s, 0), 0.0)


def _shift_up(u, s):
    if s == 0:
        return u
    n = u.shape[0]
    row = lax.broadcasted_iota(jnp.int32, u.shape, 0)
    return jnp.where(row < n - s, pltpu.roll(u, n - s, 0), 0.0)


def _conv_fwd(u, w_ref, K):
    acc = None
    for k in range(K):
        term = w_ref[k:k + 1, :] * _shift_down(u, K - 1 - k)
        acc = term if acc is None else acc + term
    return acc


def _conv_bwd_in(do, w_ref, K):
    acc = None
    for k in range(K):
        term = w_ref[k:k + 1, :] * _shift_up(do, K - 1 - k)
        acc = term if acc is None else acc + term
    return acc


def _conv_bwd_w(do, u, dw_ref, K, first):
    for k in range(K):
        row = jnp.sum(do * _shift_down(u, K - 1 - k), axis=0, keepdims=True)
        _acc_row(dw_ref, k, row, first)


def _acc_row(ref, k, row, first):
    @pl.when(first)
    def _():
        ref[k:k + 1, :] = row

    @pl.when(jnp.logical_not(first))
    def _():
        ref[k:k + 1, :] += row


def _logical(arr):
    if arr.ndim == 2:
        return arr.shape
    return (arr.shape[1], arr.shape[0] * arr.shape[2])


def _tile(dim, pref, *col_widths):
    if dim % LANES:
        assert not col_widths
        return dim
    t = (min(pref, dim) // LANES) * LANES
    while t > LANES and (dim % t or any(c % t for c in col_widths)):
        t -= LANES
    assert dim % t == 0 and all(c % t == 0 for c in col_widths), (dim, pref, col_widths)
    return t


def _spec(shape, rt, ct, rfn, cfn):
    if len(shape) == 2:
        return pl.BlockSpec((rt, ct), lambda i, j, k: (rfn(i, j, k), cfn(i, j, k)))
    per = shape[2] // ct
    return pl.BlockSpec((None, rt, ct),
                        lambda i, j, k: (cfn(i, j, k) // per, rfn(i, j, k), cfn(i, j, k) % per))


def _mm(a, b, *, name, ta=False, tb=False, out_dtype=F32, out_blocks=None, bias=None, res=None, b_rows=None,
        tm=1024, tn=1024, tk=2048):
    ra, ca = _logical(a)
    rb, cb = _logical(b)
    if b_rows is not None:
        assert b.ndim == 2 and b_rows <= rb
        rb = b_rows
    M, K = (ca, ra) if ta else (ra, ca)
    N, K2 = (rb, cb) if tb else (cb, rb)
    assert K == K2, (a.shape, b.shape, ta, tb)
    out_shape = (M, N) if out_blocks is None else (out_blocks, M, N // out_blocks)
    cw = lambda arr: [arr.shape[2]] if arr.ndim == 3 else []
    m_c = cw(a) if ta else []
    k_c = (cw(a) if not ta else []) + (cw(b) if tb else [])
    n_c = (cw(b) if not tb else []) + ([out_shape[2]] if out_blocks else []) + (cw(res) if res is not None else [])
    tm, tn, tk = _tile(M, tm, *m_c), _tile(N, tn, *n_c), _tile(K, tk, *k_c)
    nk = K // tk
    im, jn, kk = (lambda i, j, k: i), (lambda i, j, k: j), (lambda i, j, k: k)
    in_specs = [
        _spec(a.shape, tk, tm, kk, im) if ta else _spec(a.shape, tm, tk, im, kk),
        _spec(b.shape, tn, tk, jn, kk) if tb else _spec(b.shape, tk, tn, kk, jn),
    ]
    args = [a, b]
    if bias is not None:
        in_specs.append(pl.BlockSpec((1, tn), lambda i, j, k: (0, j)))
        args.append(bias.reshape(1, N).astype(F32))
    if res is not None:
        in_specs.append(_spec(res.shape, tm, tn, im, jn))
        args.append(res)
    dn = (((0 if ta else 1,), (1 if tb else 0,)), ((), ()))

    def body(*refs):
        a_ref, b_ref = refs[0], refs[1]
        pos = 2
        bias_ref = res_ref = None
        if bias is not None:
            bias_ref = refs[pos]
            pos += 1
        if res is not None:
            res_ref = refs[pos]
            pos += 1
        o_ref = refs[pos]
        k = pl.program_id(2)
        part = lax.dot_general(_bf(a_ref[...]), _bf(b_ref[...]), dn, preferred_element_type=F32)

        def finish(r):
            if bias_ref is not None:
                r = r + bias_ref[...]
            if res_ref is not None:
                r = r + res_ref[...].astype(F32)
            o_ref[...] = r.astype(out_dtype)

        if nk == 1:
            finish(part)
            return
        acc_ref = refs[pos + 1]

        @pl.when(k == 0)
        def _():
            acc_ref[...] = part

        @pl.when((k > 0) & (k < nk - 1))
        def _():
            acc_ref[...] += part

        @pl.when(k == nk - 1)
        def _():
            finish(acc_ref[...] + part)

    return pl.pallas_call(
        body, name=name,
        grid=(M // tm, N // tn, nk),
        in_specs=in_specs,
        out_specs=_spec(out_shape, tm, tn, im, jn),
        out_shape=jax.ShapeDtypeStruct(out_shape, out_dtype),
        scratch_shapes=[pltpu.VMEM((tm, tn), F32)] if nk > 1 else [],
        compiler_params=_cparams(("parallel", "parallel", "arbitrary")),
    )(*args)


def _rms_fwd(x, g, *, name, tr=512):
    T, D = x.shape
    tr = min(tr, T)

    def body(x_ref, g_ref, h_ref):
        xv = x_ref[...]
        r = lax.rsqrt(jnp.mean(xv * xv, axis=-1, keepdims=True) + EPS)
        h_ref[...] = (xv * r * g_ref[...]).astype(BF16)

    return pl.pallas_call(
        body, name=name, grid=(T // tr,),
        in_specs=[pl.BlockSpec((tr, D), lambda i: (i, 0)), pl.BlockSpec((1, D), lambda i: (0, 0))],
        out_specs=pl.BlockSpec((tr, D), lambda i: (i, 0)),
        out_shape=jax.ShapeDtypeStruct((T, D), BF16),
        compiler_params=_cparams(("parallel",)),
    )(x, g.reshape(1, D))


def _rms_bwd(x, g, dh, dres, *, name, tr=512):
    T, D = x.shape
    tr = min(tr, T)

    def body(x_ref, g_ref, dh_ref, dres_ref, dx_ref, dxb_ref, dg_ref):
        i = pl.program_id(0)
        xv = x_ref[...]
        dy = dh_ref[...].astype(F32)
        r = lax.rsqrt(jnp.mean(xv * xv, axis=-1, keepdims=True) + EPS)
        dyg = dy * g_ref[...]
        dot = jnp.mean(dyg * xv, axis=-1, keepdims=True)
        dx = dres_ref[...] + r * dyg - xv * (r * r * r) * dot
        dx_ref[...] = dx
        dxb_ref[...] = dx.astype(BF16)
        part = jnp.sum(dy * xv * r, axis=0, keepdims=True)
        _acc_row(dg_ref, 0, part, i == 0)

    row = pl.BlockSpec((tr, D), lambda i: (i, 0))
    vec = pl.BlockSpec((1, D), lambda i: (0, 0))
    return pl.pallas_call(
        body, name=name, grid=(T // tr,),
        in_specs=[row, vec, row, row],
        out_specs=[row, row, vec],
        out_shape=[jax.ShapeDtypeStruct((T, D), F32), jax.ShapeDtypeStruct((T, D), BF16),
                   jax.ShapeDtypeStruct((1, D), F32)],
        compiler_params=_cparams(("arbitrary",)),
    )(x, g.reshape(1, D), dh, dres)


def _loss_head(x, g, target, *, name, tr=512):
    T, D = x.shape
    tr = min(tr, T)

    def body(x_ref, g_ref, t_ref, loss_ref, dx_ref, dxb_ref, dg_ref):
        i = pl.program_id(0)
        xv = x_ref[...]
        gv = g_ref[...]
        r = lax.rsqrt(jnp.mean(xv * xv, axis=-1, keepdims=True) + EPS)
        y = xv * r * gv
        err = y - t_ref[...]
        lpart = 0.5 * jnp.sum(jnp.mean(err * err, axis=-1, keepdims=True), axis=0, keepdims=True)
        dy = err * (1.0 / D)
        dyg = dy * gv
        dot = jnp.mean(dyg * xv, axis=-1, keepdims=True)
        dx = r * dyg - xv * (r * r * r) * dot
        dx_ref[...] = dx
        dxb_ref[...] = dx.astype(BF16)
        _acc_row(dg_ref, 0, jnp.sum(dy * xv * r, axis=0, keepdims=True), i == 0)
        _acc_row(loss_ref, 0, jnp.broadcast_to(lpart, (1, LANES)), i == 0)

    row = pl.BlockSpec((tr, D), lambda i: (i, 0))
    return pl.pallas_call(
        body, name=name, grid=(T // tr,),
        in_specs=[row, pl.BlockSpec((1, D), lambda i: (0, 0)), row],
        out_specs=[pl.BlockSpec((1, LANES), lambda i: (0, 0)), row, row, pl.BlockSpec((1, D), lambda i: (0, 0))],
        out_shape=[jax.ShapeDtypeStruct((1, LANES), F32), jax.ShapeDtypeStruct((T, D), F32),
                   jax.ShapeDtypeStruct((T, D), BF16), jax.ShapeDtypeStruct((1, D), F32)],
        compiler_params=_cparams(("arbitrary",)),
    )(x, g.reshape(1, D), target)


def _conf_norm(u1, lg_ref, lb_ref):
    mu = jnp.mean(u1, axis=-1, keepdims=True)
    xc = u1 - mu
    r = lax.rsqrt(jnp.mean(xc * xc, axis=-1, keepdims=True) + EPS)
    n = xc * r
    return r, n, n * lg_ref[...] + lb_ref[...]


def _conf_fwd(p, dw_w, dw_b, ln_g, ln_b, *, Bl, S, CC, name):
    G = CC // LANES

    def body(av_ref, ag_ref, w_ref, b_ref, lg_ref, lb_ref, o_ref, u1_ref):
        u0 = av_ref[...] * _sig(ag_ref[...])
        u1 = _conv_fwd(u0, w_ref, CONV_K) + b_ref[...]
        u1_ref[...] = u1
        _, _, u2 = _conf_norm(u1, lg_ref, lb_ref)
        o_ref[...] = _silu(u2).astype(BF16)

    vec = pl.BlockSpec((1, LANES), lambda b, j: (0, j))
    seq = pl.BlockSpec((S, LANES), lambda b, j: (b, j))
    return pl.pallas_call(
        body, name=name, grid=(Bl, G),
        in_specs=[seq, pl.BlockSpec((S, LANES), lambda b, j: (b, G + j)),
                  pl.BlockSpec((CONV_K, LANES), lambda b, j: (0, j)), vec, vec, vec],
        out_specs=[seq, seq],
        out_shape=[jax.ShapeDtypeStruct((Bl * S, CC), BF16), jax.ShapeDtypeStruct((Bl * S, CC), F32)],
        compiler_params=_cparams(("parallel", "parallel")),
    )(p, p, dw_w, dw_b.reshape(1, CC), ln_g.reshape(1, CC), ln_b.reshape(1, CC))


def _conf_bwd(p, u1, dw_w, ln_g, ln_b, du3, *, Bl, S, CC, name):
    G = CC // LANES

    def body(av_ref, ag_ref, u1_ref, w_ref, lg_ref, lb_ref, du3_ref,
             dav_ref, dag_ref, dw_ref, db_ref, dlg_ref, dlb_ref):
        first = pl.program_id(1) == 0
        av = av_ref[...]
        sg = _sig(ag_ref[...])
        u0 = av * sg
        r, n, u2 = _conf_norm(u1_ref[...], lg_ref, lb_ref)
        du2 = du3_ref[...] * _dsilu(u2)
        _acc_row(dlg_ref, 0, jnp.sum(du2 * n, axis=0, keepdims=True), first)
        _acc_row(dlb_ref, 0, jnp.sum(du2, axis=0, keepdims=True), first)
        dn = du2 * lg_ref[...]
        du1 = r * (dn - jnp.mean(dn, axis=-1, keepdims=True) - n * jnp.mean(dn * n, axis=-1, keepdims=True))
        _acc_row(db_ref, 0, jnp.sum(du1, axis=0, keepdims=True), first)
        _conv_bwd_w(du1, u0, dw_ref, CONV_K, first)
        du0 = _conv_bwd_in(du1, w_ref, CONV_K)
        dav_ref[...] = (du0 * sg).astype(BF16)
        dag_ref[...] = (du0 * av * sg * (1.0 - sg)).astype(BF16)

    vec = pl.BlockSpec((1, LANES), lambda j, b: (0, j))
    seq = pl.BlockSpec((S, LANES), lambda j, b: (b, j))
    return pl.pallas_call(
        body, name=name, grid=(G, Bl),
        in_specs=[seq, pl.BlockSpec((S, LANES), lambda j, b: (b, G + j)), seq,
                  pl.BlockSpec((CONV_K, LANES), lambda j, b: (0, j)), vec, vec, seq],
        out_specs=[seq, seq, pl.BlockSpec((CONV_K, LANES), lambda j, b: (0, j)), vec, vec, vec],
        out_shape=[jax.ShapeDtypeStruct((Bl * S, CC), BF16), jax.ShapeDtypeStruct((Bl * S, CC), BF16),
                   jax.ShapeDtypeStruct((CONV_K, CC), F32), jax.ShapeDtypeStruct((1, CC), F32),
                   jax.ShapeDtypeStruct((1, CC), F32), jax.ShapeDtypeStruct((1, CC), F32)],
        compiler_params=_cparams(("parallel", "arbitrary")),
    )(p, p, u1, dw_w, ln_g.reshape(1, CC), ln_b.reshape(1, CC), du3)


def _gdn_pre_fwd(p, conv_w, *, Bl, S, CC, KW, VW, name):
    NQK = 2 * KW // LANES
    NB = NQK + VW // LANES
    off = 2 * CC // LANES

    def body(x_ref, w_ref, o_ref):
        j = pl.program_id(1)
        s = _silu(_conv_fwd(x_ref[...], w_ref, SHORT_CONV_K))
        r = lax.rsqrt(jnp.sum(s * s, axis=-1, keepdims=True) + EPS)
        o_ref[...] = jnp.where(j < NQK, s * r, s)

    return pl.pallas_call(
        body, name=name, grid=(Bl, NB),
        in_specs=[pl.BlockSpec((S, LANES), lambda b, j: (b, off + j)),
                  pl.BlockSpec((SHORT_CONV_K, LANES), lambda b, j: (0, j))],
        out_specs=pl.BlockSpec((S, LANES), lambda b, j: (b, j)),
        out_shape=jax.ShapeDtypeStruct((Bl * S, NB * LANES), F32),
        compiler_params=_cparams(("parallel", "parallel")),
    )(p, conv_w)


def _gdn_pre_bwd(p, conv_w, dq, dk, dv, *, Bl, S, CC, KW, VW, name):
    HQ = KW // LANES
    H = VW // LANES
    NQK = 2 * HQ
    NB = NQK + H
    off = 2 * CC // LANES

    def body(x_ref, w_ref, dq_ref, dk_ref, dv_ref, dx_ref, dw_ref):
        j = pl.program_id(0)
        first = pl.program_id(1) == 0
        xv = x_ref[...]
        c = _conv_fwd(xv, w_ref, SHORT_CONV_K)
        s = _silu(c)
        r = lax.rsqrt(jnp.sum(s * s, axis=-1, keepdims=True) + EPS)
        dy = jnp.where(j < HQ, dq_ref[...], jnp.where(j < NQK, dk_ref[...], dv_ref[...]))
        ds_norm = r * dy - s * (r * r * r) * jnp.sum(s * dy, axis=-1, keepdims=True)
        ds = jnp.where(j < NQK, ds_norm, dy)
        dc = ds * _dsilu(c)
        _conv_bwd_w(dc, xv, dw_ref, SHORT_CONV_K, first)
        dx_ref[...] = _conv_bwd_in(dc, w_ref, SHORT_CONV_K).astype(BF16)

    return pl.pallas_call(
        body, name=name, grid=(NB, Bl),
        in_specs=[pl.BlockSpec((S, LANES), lambda j, b: (b, off + j)),
                  pl.BlockSpec((SHORT_CONV_K, LANES), lambda j, b: (0, j)),
                  pl.BlockSpec((S, LANES), lambda j, b: (b, jnp.minimum(j, HQ - 1))),
                  pl.BlockSpec((S, LANES), lambda j, b: (b, jnp.clip(j - HQ, 0, HQ - 1))),
                  pl.BlockSpec((S, LANES), lambda j, b: (b, jnp.clip(j - NQK, 0, H - 1)))],
        out_specs=[pl.BlockSpec((S, LANES), lambda j, b: (b, j)),
                   pl.BlockSpec((SHORT_CONV_K, LANES), lambda j, b: (0, j))],
        out_shape=[jax.ShapeDtypeStruct((Bl * S, NB * LANES), BF16),
                   jax.ShapeDtypeStruct((SHORT_CONV_K, NB * LANES), F32)],
        compiler_params=_cparams(("parallel", "arbitrary")),
    )(p, conv_w, dq, dk, dv)


def _lane_pick(h):
    row = lax.broadcasted_iota(jnp.int32, (LANES, LANES), 0)
    return (row == h).astype(F32)


def _gdn_gate_fwd(pba, a_log, dt_bias, *, Bl, S, H, name):
    def body(alog_ref, dtb_ref, x_ref, g_ref, beta_ref):
        xv = x_ref[...]
        for h in range(H):
            b_raw = _dot(xv, _lane_pick(h), HIGHEST)
            a_raw = _dot(xv, _lane_pick(H + h), HIGHEST)
            beta_ref[h] = _sig(b_raw)
            ea = jnp.exp(jnp.zeros((1, LANES), F32) + alog_ref[h])
            g_ref[h] = -ea * _softplus(a_raw + dtb_ref[h])

    smem = pl.BlockSpec(memory_space=pltpu.SMEM)
    rep = pl.BlockSpec((H, S, LANES), lambda b: (0, b, 0))
    return pl.pallas_call(
        body, name=name, grid=(Bl,),
        in_specs=[smem, smem, pl.BlockSpec((S, LANES), lambda b: (b, 0))],
        out_specs=[rep, rep],
        out_shape=[jax.ShapeDtypeStruct((H, Bl * S, LANES), F32)] * 2,
        compiler_params=_cparams(("parallel",)),
    )(a_log, dt_bias, pba)


def _gdn_gate_bwd(pba, a_log, dt_bias, dg, dbeta, *, Bl, S, H, name):
    HP = 8 * ((H + 7) // 8)

    def body(alog_ref, dtb_ref, x_ref, dg_ref, dbeta_ref, dx_ref, dalog_ref, ddtb_ref):
        first = pl.program_id(0) == 0
        xv = x_ref[...]
        lane = lax.broadcasted_iota(jnp.int32, (S, LANES), 1)
        acc = jnp.zeros((S, LANES), F32)

        @pl.when(first)
        def _():
            dalog_ref[...] = jnp.zeros_like(dalog_ref)
            ddtb_ref[...] = jnp.zeros_like(ddtb_ref)

        for h in range(H):
            b_raw = _dot(xv, _lane_pick(h), HIGHEST)
            a_raw = _dot(xv, _lane_pick(H + h), HIGHEST)
            beta = _sig(b_raw)
            db_raw = dbeta_ref[h] * beta * (1.0 - beta)
            z = a_raw + dtb_ref[h]
            ea = jnp.exp(jnp.zeros((1, LANES), F32) + alog_ref[h])
            dgv = dg_ref[h]
            da_raw = dgv * (-ea) * _sig(z)
            g = -ea * _softplus(z)
            dalog_ref[h:h + 1, :] += jnp.sum(dgv * g, axis=0, keepdims=True)
            ddtb_ref[h:h + 1, :] += jnp.sum(da_raw, axis=0, keepdims=True)
            acc = acc + jnp.where(lane == h, db_raw, 0.0) + jnp.where(lane == H + h, da_raw, 0.0)
        dx_ref[...] = acc.astype(BF16)

    smem = pl.BlockSpec(memory_space=pltpu.SMEM)
    rep = pl.BlockSpec((H, S, LANES), lambda b: (0, b, 0))
    small = pl.BlockSpec((HP, LANES), lambda b: (0, 0))
    return pl.pallas_call(
        body, name=name, grid=(Bl,),
        in_specs=[smem, smem, pl.BlockSpec((S, LANES), lambda b: (b, 0)), rep, rep],
        out_specs=[pl.BlockSpec((S, LANES), lambda b: (b, 0)), small, small],
        out_shape=[jax.ShapeDtypeStruct((Bl * S, LANES), BF16),
                   jax.ShapeDtypeStruct((HP, LANES), F32), jax.ShapeDtypeStruct((HP, LANES), F32)],
        compiler_params=_cparams(("arbitrary",)),
    )(a_log, dt_bias, pba, dg, dbeta)


def _tri_masks():
    ri = lax.broadcasted_iota(jnp.int32, (CHUNK, CHUNK), 0)
    ci = lax.broadcasted_iota(jnp.int32, (CHUNK, CHUNK), 1)
    return ri >= ci, ri > ci, ri == CHUNK - 1


def _tri_inv(L):
    ri = lax.broadcasted_iota(jnp.int32, (CHUNK, CHUNK), 0)
    ci = lax.broadcasted_iota(jnp.int32, (CHUNK, CHUNK), 1)
    T = jnp.where(ri == ci, 1.0, 0.0) - jnp.where((ri >> 1) == (ci >> 1), L, 0.0)
    for lv in range(2, int(math.log2(CHUNK)) + 1):
        O = jnp.where(((ri >> lv) == (ci >> lv)) & ((ri >> (lv - 1)) != (ci >> (lv - 1))), L, 0.0)
        if (1 << lv) <= NEAR_BLOCK:
            T = T - _dot_x3(T, _dot_x3(O, T))
        else:
            Tb = _bf(T)
            T = T - _dot(Tb, _bf(_dot(_bf(O), Tb)))
    return T


def _chunk_local(q, k, v, beta, g):
    ge, gt, last = _tri_masks()
    gam = _dot_mask(ge, g)
    D = jnp.where(ge, jnp.exp(jnp.where(ge, gam - gam.T, 0.0)), 0.0)
    kb = k * beta
    vb = v * beta
    M = _dot_nt(_bf(kb), _bf(k))
    L = jnp.where(gt, M * D, 0.0)
    eg = jnp.exp(gam)
    kbg = kb * eg
    P = _dot_nt(_bf(q), _bf(k))
    QK = jnp.where(ge, P * D, 0.0)
    gl = jnp.sum(jnp.where(last, gam, 0.0), axis=0, keepdims=True)
    el = jnp.exp(gl - gam)
    return dict(ge=ge, gt=gt, last=last, gam=gam, D=D, kb=kb, vb=vb, L=L, eg=eg, kbg=kbg, QK=QK, gl=gl,
                el=el, kd=k * el, qg=q * eg)


def _rowsum(x):
    return jnp.sum(x, axis=-1, keepdims=True)


def _chunk_bwd(q, k, v, beta, g, S, T, u, w, do, dS2):
    c = _chunk_local(q, k, v, beta, g)
    ge, gt, last = c["ge"], c["gt"], c["last"]
    Sb = _bf(S)
    vn = u - _dot(w, Sb)
    dob, vnb, dS2b = _bf(do), _bf(vn), _bf(dS2)
    e_last = jnp.exp(c["gl"])
    dqg = _dot_nt(dob, Sb)
    dS = _dot_tn(_bf(c["qg"]), dob)
    dQK = jnp.where(ge, _dot_nt(dob, vnb), 0.0)
    dvn = _dot_tn(_bf(c["QK"]), dob)
    dS = dS + dS2 * e_last
    de_last = jnp.sum(jnp.sum(dS2 * S, axis=0, keepdims=True), axis=1, keepdims=True)
    dkd = _dot_nt(vnb, dS2b)
    dvn = dvn + _dot(_bf(c["kd"]), dS2b)
    dvnb = _bf(dvn)
    dw = -_dot_nt(dvnb, Sb)
    dS = dS - _dot_tn(w, dvnb)
    dsol = _dot_x3(T, jnp.concatenate([dvn, dw], axis=1), _TN)
    dvb, dkbg = dsol[:, :LANES], dsol[:, LANES:]
    dA = -(_dot_nt(_bf(dvb), _bf(u)) + _dot_nt(_bf(dkbg), w))
    dL = jnp.where(gt, dA, 0.0)
    dM = dL * c["D"]
    dP = dQK * c["D"]
    E = dL * c["L"] + dQK * c["QK"]
    kbf = _bf(k)
    dkb = _dot(_bf(dM), kbf) + dkbg * c["eg"]
    dk = _dot_tn(_bf(dM), _bf(c["kb"])) + _dot_tn(_bf(dP), _bf(q)) + dkd * c["el"] + dkb * beta
    dq = _dot(_bf(dP), kbf) + dqg * c["eg"]
    s_kd = _rowsum(dkd * c["kd"])
    dgam = (_rowsum(E) - _rowsum(E.T) + _rowsum(dqg * c["qg"]) - s_kd + _rowsum(dkbg * c["kbg"]))
    dgl = jnp.sum(s_kd, axis=0, keepdims=True) + de_last * e_last
    dgam_rep = jnp.broadcast_to(dgam, (CHUNK, LANES)) + jnp.where(last, jnp.broadcast_to(dgl, (CHUNK, LANES)), 0.0)
    dg_rep = _dot_mask(ge, dgam_rep, _TN)
    dbeta = _rowsum(dkb * k) + _rowsum(dvb * v)
    dv = dvb * beta
    return dq, dk, dv, jnp.broadcast_to(dbeta, (CHUNK, LANES)), dg_rep, dS


def _gdn_core_fwd(qkv, g, beta, *, Bl, S, KW, VW, name):
    HQ = KW // LANES
    H = VW // LANES
    NC = S // CHUNK
    scale = float(LANES) ** -0.5

    PAIR = 2 if NC % 2 == 0 else 1

    def body(q_ref, k_ref, v_ref, g_ref, beta_ref, o_ref, st_ref, t_ref, u_s, w_s,
             qk_s, qg_s, kd_s, el_s):
        def local(n2, carry):
            for half in range(PAIR):
                n = n2 * PAIR + half
                rows = pl.ds(pl.multiple_of(n * CHUNK, CHUNK), CHUNK)
                q = q_ref[rows, :] * scale
                k = k_ref[rows, :]
                for e in range(2):
                    c = _chunk_local(q, k, v_ref[rows, e * LANES:(e + 1) * LANES], beta_ref[e, rows, :],
                                     g_ref[e, rows, :])
                    T = _tri_inv(c["L"])
                    t_ref[e, rows, :] = T
                    uw = _dot_x3(T, jnp.concatenate([c["vb"], c["kbg"]], axis=1))
                    u_s[e, rows, :] = uw[:, :LANES]
                    w_s[e, rows, :] = _bf(uw[:, LANES:])
                    qk_s[e, rows, :] = _bf(c["QK"])
                    qg_s[e, rows, :] = _bf(c["qg"])
                    kd_s[e, rows, :] = _bf(c["kd"])
                    el_s[e, pl.ds(pl.multiple_of(n * 8, 8), 8), :] = jnp.broadcast_to(jnp.exp(c["gl"]), (8, LANES))
            return carry

        lax.fori_loop(0, NC // PAIR, local, 0)

        def scan(n, states):
            rows = pl.ds(pl.multiple_of(n * CHUNK, CHUNK), CHUNK)
            out = []
            for e in range(2):
                S_in = states[e]
                st_ref[e, n] = S_in
                Sb = _bf(S_in)
                vn = u_s[e, rows, :] - _dot(w_s[e, rows, :], Sb)
                vnb = _bf(vn)
                o_ref[rows, e * LANES:(e + 1) * LANES] = _dot(qg_s[e, rows, :], Sb) + _dot(qk_s[e, rows, :], vnb)
                e_last = el_s[e, pl.ds(pl.multiple_of(n * 8, 8), 1), :]
                out.append(S_in * e_last + _dot_tn(kd_s[e, rows, :], vnb))
            return tuple(out)

        z = jnp.zeros((LANES, LANES), F32)
        lax.fori_loop(0, NC, scan, (z, z))

    rep = pl.BlockSpec((2, S, LANES), lambda b, h: (h, b, 0))
    return pl.pallas_call(
        body, name=name, grid=(Bl, HQ),
        in_specs=[pl.BlockSpec((S, LANES), lambda b, h: (b, h)),
                  pl.BlockSpec((S, LANES), lambda b, h: (b, HQ + h)),
                  pl.BlockSpec((S, 2 * LANES), lambda b, h: (b, HQ + h)), rep, rep],
        out_specs=[pl.BlockSpec((S, 2 * LANES), lambda b, h: (b, h)),
                   pl.BlockSpec((None, 2, NC, LANES, LANES), lambda b, h: (b, h, 0, 0, 0)), rep, rep, rep],
        out_shape=[jax.ShapeDtypeStruct((Bl * S, VW), F32),
                   jax.ShapeDtypeStruct((Bl, H, NC, LANES, LANES), F32),
                   jax.ShapeDtypeStruct((H, Bl * S, LANES), F32),
                   jax.ShapeDtypeStruct((H, Bl * S, LANES), F32),
                   jax.ShapeDtypeStruct((H, Bl * S, LANES), BF16)],
        scratch_shapes=[pltpu.VMEM((2, S, LANES), BF16)] * 3 + [pltpu.VMEM((2, NC * 8, LANES), F32)],
        compiler_params=_cparams(("parallel", "parallel")),
    )(qkv, qkv, qkv, g, beta)


def _gdn_core_bwd(qkv, g, beta, states, tinv, u, w, do, *, Bl, S, KW, VW, name):
    HQ = KW // LANES
    H = VW // LANES
    NC = S // CHUNK
    scale = float(LANES) ** -0.5

    def body(q_ref, k_ref, v_ref, g_ref, beta_ref, st_ref, t_ref, u_ref, w_ref, do_ref,
             dq_ref, dk_ref, dv_ref, dg_ref, dbeta_ref):
        def step(i, dstates):
            n = NC - 1 - i
            rows = pl.ds(pl.multiple_of(n * CHUNK, CHUNK), CHUNK)
            q = q_ref[rows, :] * scale
            k = k_ref[rows, :]
            out = []
            dq_sum = dk_sum = None
            for e in range(2):
                cols = slice(e * LANES, (e + 1) * LANES)
                dq, dk, dv, dbeta, dg, dS = _chunk_bwd(q, k, v_ref[rows, cols], beta_ref[e, rows, :],
                                                       g_ref[e, rows, :], st_ref[e, n], t_ref[e, rows, :],
                                                       u_ref[e, rows, :], w_ref[e, rows, :],
                                                       do_ref[rows, cols], dstates[e])
                dv_ref[rows, cols] = dv
                dg_ref[e, rows, :] = dg
                dbeta_ref[e, rows, :] = dbeta
                dq_sum = dq if dq_sum is None else dq_sum + dq
                dk_sum = dk if dk_sum is None else dk_sum + dk
                out.append(dS)
            dq_ref[rows, :] = dq_sum * scale
            dk_ref[rows, :] = dk_sum
            return tuple(out)

        z = jnp.zeros((LANES, LANES), F32)
        lax.fori_loop(0, NC, step, (z, z))

    rep = pl.BlockSpec((2, S, LANES), lambda b, h: (h, b, 0))
    seq = pl.BlockSpec((S, LANES), lambda b, h: (b, h))
    seq2 = pl.BlockSpec((S, 2 * LANES), lambda b, h: (b, h))
    return pl.pallas_call(
        body, name=name, grid=(Bl, HQ),
        in_specs=[seq, pl.BlockSpec((S, LANES), lambda b, h: (b, HQ + h)),
                  pl.BlockSpec((S, 2 * LANES), lambda b, h: (b, HQ + h)), rep, rep,
                  pl.BlockSpec((None, 2, NC, LANES, LANES), lambda b, h: (b, h, 0, 0, 0)), rep, rep, rep, seq2],
        out_specs=[seq, seq, seq2, rep, rep],
        out_shape=[jax.ShapeDtypeStruct((Bl * S, KW), F32), jax.ShapeDtypeStruct((Bl * S, KW), F32),
                   jax.ShapeDtypeStruct((Bl * S, VW), F32),
                   jax.ShapeDtypeStruct((H, Bl * S, LANES), F32), jax.ShapeDtypeStruct((H, Bl * S, LANES), F32)],
        compiler_params=_cparams(("parallel", "parallel")),
    )(qkv, qkv, qkv, g, beta, states, tinv, u, w, do)


def _gdn_out_fwd(o, p, norm_g, out_a, *, CC, VW, name, tr=256):
    T = o.shape[0]
    tr = min(tr, T)
    H = VW // LANES
    zoff = p.shape[1] // VW - 1

    def body(o_ref, z_ref, ng_ref, a_ref, mix_ref):
        mix_ref[:, :CC] = a_ref[...]
        for h in range(H):
            cols = slice(h * LANES, (h + 1) * LANES)
            ov = o_ref[:, cols]
            r = lax.rsqrt(jnp.mean(ov * ov, axis=-1, keepdims=True) + EPS)
            mix_ref[:, CC + h * LANES:CC + (h + 1) * LANES] = (ov * r * ng_ref[...] * _silu(z_ref[:, cols])).astype(BF16)

    return pl.pallas_call(
        body, name=name, grid=(T // tr,),
        in_specs=[pl.BlockSpec((tr, VW), lambda i: (i, 0)), pl.BlockSpec((tr, VW), lambda i: (i, zoff)),
                  pl.BlockSpec((1, LANES), lambda i: (0, 0)), pl.BlockSpec((tr, CC), lambda i: (i, 0))],
        out_specs=pl.BlockSpec((tr, CC + VW), lambda i: (i, 0)),
        out_shape=jax.ShapeDtypeStruct((T, CC + VW), BF16),
        compiler_params=_cparams(("parallel",)),
    )(o, p, norm_g.reshape(1, LANES), out_a)


def _gdn_out_bwd(o, p, norm_g, dmix, *, CC, VW, name, tr=256):
    T = o.shape[0]
    tr = min(tr, T)
    H = VW // LANES
    zoff = p.shape[1] // VW - 1

    def body(o_ref, z_ref, ng_ref, dmix_ref, do_ref, dz_ref, da_ref, dng_ref, dpb_ref):
        first = pl.program_id(0) == 0
        da = dmix_ref[:, :CC]
        da_ref[...] = da.astype(BF16)
        _acc_row(dpb_ref, 0, jnp.sum(da, axis=0, keepdims=True), first)
        ng = ng_ref[...]
        dng = jnp.zeros((1, LANES), F32)
        for h in range(H):
            cols = slice(h * LANES, (h + 1) * LANES)
            ov = o_ref[:, cols]
            zv = z_ref[:, cols]
            dout = dmix_ref[:, CC + h * LANES:CC + (h + 1) * LANES]
            r = lax.rsqrt(jnp.mean(ov * ov, axis=-1, keepdims=True) + EPS)
            on = ov * r * ng
            don = dout * _silu(zv)
            dz_ref[:, cols] = (dout * on * _dsilu(zv)).astype(BF16)
            dng = dng + jnp.sum(don * ov * r, axis=0, keepdims=True)
            dong = don * ng
            do_ref[:, cols] = r * dong - ov * (r * r * r) * jnp.mean(dong * ov, axis=-1, keepdims=True)
        _acc_row(dng_ref, 0, dng, first)

    return pl.pallas_call(
        body, name=name, grid=(T // tr,),
        in_specs=[pl.BlockSpec((tr, VW), lambda i: (i, 0)), pl.BlockSpec((tr, VW), lambda i: (i, zoff)),
                  pl.BlockSpec((1, LANES), lambda i: (0, 0)), pl.BlockSpec((tr, CC + VW), lambda i: (i, 0))],
        out_specs=[pl.BlockSpec((tr, VW), lambda i: (i, 0)), pl.BlockSpec((tr, VW), lambda i: (i, 0)),
                   pl.BlockSpec((tr, CC), lambda i: (i, 0)), pl.BlockSpec((1, LANES), lambda i: (0, 0)),
                   pl.BlockSpec((1, CC), lambda i: (0, 0))],
        out_shape=[jax.ShapeDtypeStruct((T, VW), F32), jax.ShapeDtypeStruct((T, VW), BF16),
                   jax.ShapeDtypeStruct((T, CC), BF16), jax.ShapeDtypeStruct((1, LANES), F32),
                   jax.ShapeDtypeStruct((1, CC), F32)],
        compiler_params=_cparams(("arbitrary",)),
    )(o, p, norm_g.reshape(1, LANES), dmix)


FFN_CW = 256


def _ffn_act_fwd(gu, conv_w, conv_b, *, Bl, S, name):
    FF = gu.shape[2]
    cw = min(FFN_CW, FF)

    def body(g_ref, u_ref, w_ref, b_ref, a_ref):
        gc = _conv_fwd(g_ref[...], w_ref, FFN_CONV_K) + b_ref[...]
        a_ref[...] = (_silu(gc) * u_ref[...]).astype(BF16)

    return pl.pallas_call(
        body, name=name, grid=(Bl, FF // cw),
        in_specs=[pl.BlockSpec((None, S, cw), lambda b, j: (0, b, j)),
                  pl.BlockSpec((None, S, cw), lambda b, j: (1, b, j)),
                  pl.BlockSpec((FFN_CONV_K, cw), lambda b, j: (0, j)),
                  pl.BlockSpec((1, cw), lambda b, j: (0, j))],
        out_specs=pl.BlockSpec((S, cw), lambda b, j: (b, j)),
        out_shape=jax.ShapeDtypeStruct((Bl * S, FF), BF16),
        compiler_params=_cparams(("parallel", "parallel")),
    )(gu, gu, conv_w, conv_b.reshape(1, FF))


def _ffn_act_bwd(gu, conv_w, conv_b, da, *, Bl, S, name):
    FF = gu.shape[2]
    cw = min(FFN_CW, FF)

    def body(g_ref, u_ref, w_ref, b_ref, da_ref, dgu_ref, dw_ref, db_ref):
        first = pl.program_id(1) == 0
        gate = g_ref[...]
        gc = _conv_fwd(gate, w_ref, FFN_CONV_K) + b_ref[...]
        dav = da_ref[...]
        dgu_ref[1] = (dav * _silu(gc)).astype(BF16)
        dgc = dav * u_ref[...] * _dsilu(gc)
        _acc_row(db_ref, 0, jnp.sum(dgc, axis=0, keepdims=True), first)
        _conv_bwd_w(dgc, gate, dw_ref, FFN_CONV_K, first)
        dgu_ref[0] = _conv_bwd_in(dgc, w_ref, FFN_CONV_K).astype(BF16)

    return pl.pallas_call(
        body, name=name, grid=(FF // cw, Bl),
        in_specs=[pl.BlockSpec((None, S, cw), lambda j, b: (0, b, j)),
                  pl.BlockSpec((None, S, cw), lambda j, b: (1, b, j)),
                  pl.BlockSpec((FFN_CONV_K, cw), lambda j, b: (0, j)),
                  pl.BlockSpec((1, cw), lambda j, b: (0, j)),
                  pl.BlockSpec((S, cw), lambda j, b: (b, j))],
        out_specs=[pl.BlockSpec((2, S, cw), lambda j, b: (0, b, j)),
                   pl.BlockSpec((FFN_CONV_K, cw), lambda j, b: (0, j)),
                   pl.BlockSpec((1, cw), lambda j, b: (0, j))],
        out_shape=[jax.ShapeDtypeStruct((2, Bl * S, FF), BF16),
                   jax.ShapeDtypeStruct((FFN_CONV_K, FF), F32), jax.ShapeDtypeStruct((1, FF), F32)],
        compiler_params=_cparams(("parallel", "arbitrary")),
    )(gu, gu, conv_w, conv_b.reshape(1, FF), da)


def _layer_dims(W):
    CC = W["conv_pw_b"].shape[0]
    VW = W["mix_norm_g"].shape[0] - CC
    KW = (W["gdn_conv_w"].shape[1] - VW) // 2
    return CC, KW, VW


def _layer_fwd(l, x, W, Bl, S, fetch):
    CC, KW, VW = _layer_dims(W)
    H = VW // LANES
    w_in_t, w_in_ba = fetch(l, "w_in", x)
    n_main = (w_in_t.shape[0] // LANES) * LANES
    h1 = _rms_fwd(x, W["mix_norm_g"], name="rms1_fwd")
    p = _mm(h1, w_in_t, tb=True, b_rows=n_main, name="mm_in")
    pba = _mm(h1, w_in_ba, tb=True, name="mm_in_ba")
    u3, u1 = _conf_fwd(p, W["conv_dw_w"], W["conv_dw_b"], W["conv_ln_g"], W["conv_ln_b"], Bl=Bl, S=S, CC=CC,
                       name="conf_fwd")
    conv_pw_w = fetch(l, "conv_pw_w", u3)
    out_a = _mm(u3, conv_pw_w, bias=W["conv_pw_b"], out_dtype=BF16, name="mm_pw")
    qkv = _gdn_pre_fwd(p, W["gdn_conv_w"], Bl=Bl, S=S, CC=CC, KW=KW, VW=VW, name="gdn_pre_fwd")
    g, beta = _gdn_gate_fwd(pba, W["gdn_a_log"], W["gdn_dt_bias"], Bl=Bl, S=S, H=H, name="gdn_gate_fwd")
    o, states, tinv, gdn_u, gdn_w = _gdn_core_fwd(qkv, g, beta, Bl=Bl, S=S, KW=KW, VW=VW, name="gdn_core_fwd")
    mix = _gdn_out_fwd(o, p, W["gdn_norm_g"], out_a, CC=CC, VW=VW, name="gdn_out_fwd")
    w_out = fetch(l, "w_out", mix)
    x1 = _mm(mix, w_out, res=x, name="mm_out")
    h2 = _rms_fwd(x1, W["ffn_norm_g"], name="rms2_fwd")
    w_up = fetch(l, "w_up", h2)
    gu = _mm(h2, w_up, out_blocks=2, tn=w_up.shape[2], name="mm_up")
    a = _ffn_act_fwd(gu, W["ffn_conv_w"], W["ffn_conv_b"], Bl=Bl, S=S, name="ffn_act_fwd")
    w_down = fetch(l, "w_down", a)
    x2 = _mm(a, w_down, res=x1, name="mm_down")
    saved = dict(x=x, h1=h1, p=p, pba=pba, u1=u1, u3=u3, qkv=qkv, g=g, beta=beta, o=o, states=states, tinv=tinv,
                 gdn_u=gdn_u, gdn_w=gdn_w, mix=mix, x1=x1, h2=h2, gu=gu, a=a, w_in_t=w_in_t, w_in_ba=w_in_ba, conv_pw_w=conv_pw_w,
                 w_out=w_out, w_up=w_up, w_down=w_down)
    return x2, saved


def _layer_bwd(l, dx2, dx2b, W, A, Bl, S, sink):
    CC, KW, VW = _layer_dims(W)
    H = VW // LANES
    G = {}
    upw = A["w_up"].shape[2]
    da = _mm(dx2b, A["w_down"], tb=True, tn=upw, name="mm_down_dx")
    da = sink(l, "w_down", _mm(A["a"], dx2b, ta=True, out_dtype=BF16, tm=upw, name="mm_down_dw"), da)
    dgu, G["ffn_conv_w"], G["ffn_conv_b"] = _ffn_act_bwd(A["gu"], W["ffn_conv_w"], W["ffn_conv_b"], da,
                                                         Bl=Bl, S=S, name="ffn_act_bwd")
    dh2 = _mm(dgu, A["w_up"], tb=True, tk=upw, tn=2048, name="mm_up_dx")
    dh2 = sink(l, "w_up", _mm(A["h2"], dgu, ta=True, out_dtype=BF16, out_blocks=N_DEV, tn=upw, name="mm_up_dw"),
               dh2)
    dx1, dx1b, G["ffn_norm_g"] = _rms_bwd(A["x1"], W["ffn_norm_g"], dh2, dx2, name="rms2_bwd")
    dmix = _mm(dx1b, A["w_out"], tb=True, name="mm_out_dx")
    dmix = sink(l, "w_out", _mm(A["mix"], dx1b, ta=True, out_dtype=BF16, name="mm_out_dw"), dmix)
    do, dz, dout_a, G["gdn_norm_g"], G["conv_pw_b"] = _gdn_out_bwd(A["o"], A["p"], W["gdn_norm_g"], dmix,
                                                                   CC=CC, VW=VW, name="gdn_out_bwd")
    dq, dk, dv, dg, dbeta = _gdn_core_bwd(A["qkv"], A["g"], A["beta"], A["states"], A["tinv"], A["gdn_u"],
                                          A["gdn_w"], do, Bl=Bl, S=S, KW=KW, VW=VW, name="gdn_core_bwd")
    dpba, dalog, ddtb = _gdn_gate_bwd(A["pba"], W["gdn_a_log"], W["gdn_dt_bias"], dg, dbeta, Bl=Bl, S=S, H=H,
                                      name="gdn_gate_bwd")
    G["gdn_a_log"], G["gdn_dt_bias"] = dalog[:H, 0], ddtb[:H, 0]
    dqkv, G["gdn_conv_w"] = _gdn_pre_bwd(A["p"], W["gdn_conv_w"], dq, dk, dv, Bl=Bl, S=S, CC=CC, KW=KW, VW=VW,
                                         name="gdn_pre_bwd")
    du3 = _mm(dout_a, A["conv_pw_w"], tb=True, name="mm_pw_dx")
    du3 = sink(l, "conv_pw_w", _mm(A["u3"], dout_a, ta=True, out_dtype=BF16, name="mm_pw_dw"), du3)
    dav, dag, G["conv_dw_w"], G["conv_dw_b"], G["conv_ln_g"], G["conv_ln_b"] = _conf_bwd(
        A["p"], A["u1"], W["conv_dw_w"], W["conv_ln_g"], W["conv_ln_b"], du3, Bl=Bl, S=S, CC=CC, name="conf_bwd")
    dp = jnp.concatenate([dav, dag, dqkv, dz], axis=1)
    dh1 = _mm(dpba, A["w_in_ba"], name="mm_in_ba_dx")
    dh1 = _mm(dp, A["w_in_t"], b_rows=dp.shape[1], res=dh1, name="mm_in_dx")
    dh1 = sink(l, "w_in", (_mm(dp, A["h1"], ta=True, out_dtype=BF16, name="mm_in_dw"),
                           _mm(dpba, A["h1"], ta=True, out_dtype=BF16, name="mm_in_ba_dw")), dh1)
    dx, dxb, G["mix_norm_g"] = _rms_bwd(A["x"], W["mix_norm_g"], dh1, dx1, name="rms1_bwd")
    return dx, dxb, G


def _local_step(x, target, Ws, final_norm_g, fetch, sink):
    Bl, S, D = x.shape
    xt = x.reshape(Bl * S, D)
    acts = []
    for l, W in enumerate(Ws):
        xt, A = _layer_fwd(l, xt, W, Bl, S, fetch)
        acts.append(A)
    loss, dx, dxb, dgf = _loss_head(xt, final_norm_g, target.reshape(Bl * S, D), name="loss_head")
    grads = [None] * len(Ws)
    for l in reversed(range(len(Ws))):
        dx, dxb, grads[l] = _layer_bwd(l, dx, dxb, Ws[l], acts[l], Bl, S, sink)
    return loss[0, 0], dx.reshape(Bl, S, D), grads, dgf.reshape(D)


def _mesh_pos():
    return lax.axis_index("x"), lax.axis_index("y"), lax.axis_index("c")


def _dev_index(px, py, pc):
    return 4 * px + 2 * py + pc


_ANY = pl.BlockSpec(memory_space=pl.ANY)


def _all_gather(arrs, *, name):
    n = len(arrs)

    def body(*refs):
        ins, outs = refs[:n], refs[n:2 * n]
        send_sems, recv_sems, local_sems = refs[2 * n:]
        x, y, c = _mesh_pos()
        me, sibling = (x, y, c), (x, y, 1 - c)
        chips = [(1 - x, y), (x, 1 - y), (1 - x, 1 - y)]

        def copy(a, k, block, to, src=None):
            dst = outs[a].at[_dev_index(*block)]
            return pltpu.make_async_remote_copy(
                src_ref=dst if src is None else src, dst_ref=dst,
                send_sem=send_sems.at[a, k], recv_sem=recv_sems.at[a, k],
                device_id=to, device_id_type=MESH)

        mine = [pltpu.make_async_copy(ins[a], outs[a].at[_dev_index(*me)], local_sems.at[a]) for a in range(n)]
        for cp in mine:
            cp.start()
        first = []
        for a in range(n):
            first.append(copy(a, 0, me, sibling, src=ins[a]))
            first += [copy(a, 1 + j, me, (*chip, c), src=ins[a]) for j, chip in enumerate(chips)]
        for cp in first:
            cp.start()
        passed = []
        for a in range(n):
            for j, chip in enumerate(chips):
                copy(a, 1 + j, (*chip, c), me).wait_recv()
                fwd = copy(a, 4 + j, (*chip, c), sibling)
                fwd.start()
                passed.append(fwd)
        for a in range(n):
            copy(a, 0, sibling, me).wait_recv()
            for j, chip in enumerate(chips):
                copy(a, 4 + j, (*chip, 1 - c), me).wait_recv()
        for cp in first + passed:
            cp.wait_send()
        for cp in mine:
            cp.wait()

    return pl.pallas_call(
        body, name=name,
        in_specs=[_ANY] * n, out_specs=[_ANY] * n,
        out_shape=[jax.ShapeDtypeStruct((N_DEV,) + a.shape, a.dtype) for a in arrs],
        scratch_shapes=[pltpu.SemaphoreType.DMA((n, N_DEV - 1)), pltpu.SemaphoreType.DMA((n, N_DEV - 1)),
                        pltpu.SemaphoreType.DMA((n,))],
    )(*arrs)


def _peers(x, y, c):
    flip = lambda v, f: 1 - v if f else v
    return [(flip(x, p & 4), flip(y, p & 2), flip(c, p & 1)) for p in range(1, N_DEV)]


GATHER_ID, SCATTER_ID = 1, 2
_SEQUENCER = dict(axis_name="sequencer", num_cores=1)


def _handshake(peers):
    barrier = pltpu.get_barrier_semaphore()
    for peer in peers:
        pl.semaphore_signal(barrier, inc=1, device_id=peer, device_id_type=MESH)
    pl.semaphore_wait(barrier, len(peers))


def _sc_gather(src, *, name):
    def body(src_ref, zone_ref, send_sems, recv_sems, local_sem):
        x, y, c = _mesh_pos()
        me, sibling = (x, y, c), (x, y, 1 - c)
        chips = [(1 - x, y), (x, 1 - y), (1 - x, 1 - y)]
        _handshake([sibling] + [(*chip, c) for chip in chips])

        def copy(k, block, to, from_src=False):
            dst = zone_ref.at[_dev_index(*block)]
            return pltpu.make_async_remote_copy(
                src_ref=src_ref if from_src else dst, dst_ref=dst, send_sem=send_sems.at[k], recv_sem=recv_sems.at[k],
                device_id=to, device_id_type=MESH)

        mine = pltpu.make_async_copy(src_ref, zone_ref.at[_dev_index(*me)], local_sem)
        mine.start()
        first = [copy(1 + j, me, (*chip, c), from_src=True) for j, chip in enumerate(chips)]
        first.append(copy(0, me, sibling, from_src=True))
        for cp in first:
            cp.start()
        passed = []
        for j, chip in enumerate(chips):
            copy(1 + j, (*chip, c), me).wait_recv()
            fwd = copy(4 + j, (*chip, c), sibling)
            fwd.start()
            passed.append(fwd)
        copy(0, sibling, me).wait_recv()
        for j, chip in enumerate(chips):
            copy(4 + j, (*chip, 1 - c), me).wait_recv()
        for cp in first + passed:
            cp.wait_send()
        mine.wait()

    return pl.kernel(
        body, name=name,
        out_type=jax.ShapeDtypeStruct((N_DEV,) + src.shape, src.dtype),
        mesh=plsc.ScalarSubcoreMesh(**_SEQUENCER),
        scratch_types=[pltpu.SemaphoreType.DMA((N_DEV - 1,)), pltpu.SemaphoreType.DMA((N_DEV - 1,)),
                       pltpu.SemaphoreType.DMA],
        compiler_params=pltpu.CompilerParams(collective_id=GATHER_ID),
    )(src)


def _sc_scatter(part, *, name):
    def body(src_ref, zone_ref, send_sems, recv_sems, local_sem):
        x, y, c = _mesh_pos()
        me = _dev_index(x, y, c)
        peers = _peers(x, y, c)
        _handshake(peers)
        mine = pltpu.make_async_copy(src_ref.at[me], zone_ref.at[me], local_sem)
        mine.start()
        sends = [pltpu.make_async_remote_copy(
            src_ref=src_ref.at[_dev_index(*peer)], dst_ref=zone_ref.at[me], send_sem=send_sems.at[k],
            recv_sem=recv_sems.at[k], device_id=peer, device_id_type=MESH) for k, peer in enumerate(peers)]
        for cp in sends:
            cp.start()
        for k, peer in enumerate(peers):
            pltpu.make_async_remote_copy(
                src_ref=src_ref.at[me], dst_ref=zone_ref.at[_dev_index(*peer)], send_sem=send_sems.at[k],
                recv_sem=recv_sems.at[k], device_id=peer, device_id_type=MESH).wait_recv()
        for cp in sends:
            cp.wait_send()
        mine.wait()

    return pl.kernel(
        body, name=name,
        out_type=jax.ShapeDtypeStruct(part.shape, part.dtype),
        mesh=plsc.ScalarSubcoreMesh(**_SEQUENCER),
        scratch_types=[pltpu.SemaphoreType.DMA((N_DEV - 1,)), pltpu.SemaphoreType.DMA((N_DEV - 1,)),
                       pltpu.SemaphoreType.DMA],
        compiler_params=pltpu.CompilerParams(collective_id=SCATTER_ID),
    )(part)


def _adamw_math(w, g, m, v):
    m2 = ADAM_B1 * m + (1.0 - ADAM_B1) * g
    v2 = ADAM_B2 * v + (1.0 - ADAM_B2) * (g * g)
    m_hat = m2 / (1.0 - ADAM_B1 ** ADAM_STEP)
    v_hat = v2 / (1.0 - ADAM_B2 ** ADAM_STEP)
    delta = -ADAM_LR * (m_hat / (jnp.sqrt(v_hat) + ADAM_EPS) + ADAM_WD * w)
    return delta, m2, v2


def _adamw_big(l, w, m, v, recv, prev, *, summed=False, name, tr=128):
    L, R, C = w.shape
    tr = next(t for t in range(min(tr, R), 0, -16) if R % t == 0)

    def body(w_ref, m_ref, v_ref, r_ref, *rest):
        g_ref, d_ref, m2_ref, v2_ref = rest[-4:]
        if summed:
            g = r_ref[...]
        else:
            g = r_ref[0].astype(F32)
            for s in range(1, N_DEV):
                g = g + r_ref[s].astype(F32)
        g_ref[...] = g
        d_ref[...], m2_ref[...], v2_ref[...] = _adamw_math(w_ref[...], g, m_ref[...], v_ref[...])

    wspec = pl.BlockSpec((None, tr, C), lambda i: (l, i, 0))
    rspec = pl.BlockSpec((tr, C), lambda i: (i, 0)) if summed else pl.BlockSpec((N_DEV, tr, C), lambda i: (0, i, 0))
    prev = list(prev) if prev is not None else []
    return pl.pallas_call(
        body, name=name, grid=(R // tr,),
        in_specs=[wspec, wspec, wspec, rspec] + [_ANY] * len(prev),
        out_specs=[wspec] * 4,
        out_shape=[jax.ShapeDtypeStruct((L, R, C), F32)] * 4,
        input_output_aliases={4 + j: j for j in range(len(prev))},
        compiler_params=_cparams(("parallel",)),
    )(w, m, v, recv if summed else recv.reshape(N_DEV, R, C), *prev)


def _sum_slots_wide(recv, *, name, tc=512):
    _, R, C = recv.shape
    tc = _tile(C, tc)

    def body(r_ref, o_ref):
        g = r_ref[0].astype(F32)
        for s in range(1, N_DEV):
            g = g + r_ref[s].astype(F32)
        o_ref[...] = g

    return pl.pallas_call(
        body, name=name, grid=(C // tc,),
        in_specs=[pl.BlockSpec((N_DEV, R, tc), lambda j: (0, 0, j))],
        out_specs=pl.BlockSpec((R, tc), lambda j: (0, j)),
        out_shape=jax.ShapeDtypeStruct((R, C), F32),
        compiler_params=_cparams(("parallel",)),
    )(recv)


def _sum_slots(gathered, *, name):
    _, R, C = gathered.shape

    def body(r_ref, o_ref):
        g = r_ref[0]
        for s in range(1, N_DEV):
            g = g + r_ref[s]
        o_ref[...] = g

    return pl.pallas_call(body, name=name, out_shape=jax.ShapeDtypeStruct((R, C), F32))(gathered)


def _adamw_small(w, g, m, v, *, name):
    def body(w_ref, g_ref, m_ref, v_ref, d_ref, m2_ref, v2_ref):
        d_ref[...], m2_ref[...], v2_ref[...] = _adamw_math(w_ref[...], g_ref[...], m_ref[...], v_ref[...])

    return pl.pallas_call(body, name=name, out_shape=[jax.ShapeDtypeStruct(w.shape, F32)] * 3)(w, g, m, v)


def _pack(arrs):
    flat = []
    for a in arrs:
        a = a.reshape(-1).astype(F32)
        flat.append(jnp.pad(a, (0, (-a.shape[0]) % LANES)))
    out = jnp.concatenate(flat)
    out = jnp.pad(out, (0, (-out.shape[0]) % (8 * LANES)))
    return out.reshape(-1, LANES)


def _unpack(packed, shapes):
    flat = packed.reshape(-1)
    out, pos = [], 0
    for s in shapes:
        size = math.prod(s)
        out.append(flat[pos:pos + size].reshape(s))
        pos += size + (-size) % LANES
    return out


BIG = ("w_in", "conv_pw_w", "w_out", "w_up", "w_down")
SMALL_SHARDED = ("conv_dw_w", "gdn_conv_w", "ffn_conv_w")
SMALL_REPLICATED = ("mix_norm_g", "conv_dw_b", "conv_ln_g", "conv_ln_b", "conv_pw_b", "gdn_a_log", "gdn_dt_bias",
                    "gdn_norm_g", "ffn_norm_g", "ffn_conv_b")
WEIGHTS = ("mix_norm_g", "w_in", "conv_dw_w", "conv_dw_b", "conv_ln_g", "conv_ln_b", "conv_pw_w", "conv_pw_b",
           "gdn_conv_w", "gdn_a_log", "gdn_dt_bias", "gdn_norm_g", "w_out", "ffn_norm_g", "w_up", "ffn_conv_w",
           "ffn_conv_b", "w_down", "final_norm_g")


def _train_step(x, target, w, m, v):
    L = w["w_in"].shape[0]
    D = x.shape[-1]
    xi, yi, ci = _mesh_pos()
    me = _dev_index(xi, yi, ci)

    small_full = {}
    for n in SMALL_SHARDED:
        g_ = _sc_gather(w[n], name=f"gather_{n}")
        small_full[n] = jnp.moveaxis(g_, 0, 2).reshape(L, g_.shape[2], N_DEV * g_.shape[3])
    gathered = {}

    def launch(l, after=None):
        for n in BIG:
            src = (w[n][l].T if n == "w_in" else w[n][l]).astype(BF16)
            if after is not None:
                src = lax.optimization_barrier((src, after))[0]
            gathered[n, l] = _sc_gather(src, name=f"gather_{n}_{l}")

    launch(0)
    Ws = []
    for l in range(L):
        W = {n: w[n][l] for n in SMALL_REPLICATED}
        W.update({n: small_full[n][l] for n in SMALL_SHARDED})
        Ws.append(W)

    def fetch(l, n, after):
        if n == "conv_pw_w" and l + 1 < L:
            launch(l + 1, after)
        g_ = lax.optimization_barrier((gathered[n, l], after))[0]
        if n == "w_up":
            return g_
        g_ = g_.reshape(g_.shape[0] * g_.shape[1], g_.shape[2])
        if n == "w_in":
            n_main = (g_.shape[0] // LANES) * LANES
            return g_, jnp.pad(g_[n_main:], ((0, LANES - (g_.shape[0] - n_main)), (0, 0)))
        return g_

    started = []
    res = {}
    SCATTERS_IN_FLIGHT = 2

    def consume(chain):
        n, l, recv = started.pop(0)
        if n == "w_in":
            recv = _sum_slots_wide(recv, name="sum_w_in_grad").T
        res[n] = _adamw_big(l, w[n], m[n], v[n], recv, res.get(n), summed=(n == "w_in"), name=f"adamw_{n}")
        if chain is None:
            return None
        tied = lax.optimization_barrier((chain, *res[n]))
        res[n] = list(tied[1:])
        return tied[0]

    def sink(l, n, g_, chain):
        g_, chain = lax.optimization_barrier((g_, chain))
        if len(started) >= SCATTERS_IN_FLIGHT:
            chain = consume(chain)
        if n == "w_in":
            g_main, g_ba = g_
            g_ = jnp.concatenate([g_main, g_ba[:w["w_in"].shape[2] * N_DEV - g_main.shape[0]]], axis=0)
            part = g_.reshape(N_DEV, -1, D)
        elif n == "w_up":
            part = g_
        else:
            part = g_.reshape(N_DEV, -1, g_.shape[1])
        started.append((n, l, _sc_scatter(part, name=f"scatter_{n}_{l}")))
        return chain

    loss, grad_x, G, d_final = _local_step(x, target, Ws, w["final_norm_g"], fetch, sink)
    loss = lax.psum(loss, ("x", "y", "c"))

    out = {k: {} for k in ("grad", "delta", "new_m", "new_v")}
    while started:
        consume(None)
    for n in BIG:
        for j, k in enumerate(("grad", "delta", "new_m", "new_v")):
            out[k][n] = res[n][j]

    small_names = [n for n in WEIGHTS if n not in BIG]
    partial = []
    for n in small_names:
        if n == "final_norm_g":
            partial.append(d_final)
        else:
            partial.append(jnp.stack([G[l][n].reshape(Ws[l][n].shape) for l in range(L)]))
    shapes = [p_.shape for p_ in partial]
    gathered = _all_gather([_pack(partial)], name="all_gather_small_grads")[0]
    full = dict(zip(small_names, _unpack(_sum_slots(gathered, name="sum_small_grads"), shapes)))
    for n in SMALL_SHARDED:
        width = w[n].shape[-1]
        full[n] = lax.dynamic_slice_in_dim(full[n], me * width, width, axis=2)
    loc_shapes = [w[n].shape for n in small_names]
    g_pack = _pack([full[n] for n in small_names])
    res = _adamw_small(_pack([w[n] for n in small_names]), g_pack, _pack([m[n] for n in small_names]),
                       _pack([v[n] for n in small_names]), name="adamw_small")
    for k, packed in zip(("grad", "delta", "new_m", "new_v"), (g_pack,) + tuple(res)):
        out[k].update(dict(zip(small_names, _unpack(packed, loc_shapes))))
    return loss, grad_x, out


def kernel(x, mix_norm_g, w_in, conv_dw_w, conv_dw_b, conv_ln_g, conv_ln_b, conv_pw_w, conv_pw_b, gdn_conv_w, gdn_a_log, gdn_dt_bias, gdn_norm_g, w_out, ffn_norm_g, w_up, ffn_conv_w, ffn_conv_b, w_down, final_norm_g, loss_target, m_mix_norm_g, m_w_in, m_conv_dw_w, m_conv_dw_b, m_conv_ln_g, m_conv_ln_b, m_conv_pw_w, m_conv_pw_b, m_gdn_conv_w, m_gdn_a_log, m_gdn_dt_bias, m_gdn_norm_g, m_w_out, m_ffn_norm_g, m_w_up, m_ffn_conv_w, m_ffn_conv_b, m_w_down, m_final_norm_g, v_mix_norm_g, v_w_in, v_conv_dw_w, v_conv_dw_b, v_conv_ln_g, v_conv_ln_b, v_conv_pw_w, v_conv_pw_b, v_gdn_conv_w, v_gdn_a_log, v_gdn_dt_bias, v_gdn_norm_g, v_w_out, v_ffn_norm_g, v_w_up, v_ffn_conv_w, v_ffn_conv_b, v_w_down, v_final_norm_g):
    w = dict(zip(WEIGHTS, (mix_norm_g, w_in, conv_dw_w, conv_dw_b, conv_ln_g, conv_ln_b, conv_pw_w, conv_pw_b, gdn_conv_w,
                           gdn_a_log, gdn_dt_bias, gdn_norm_g, w_out, ffn_norm_g, w_up, ffn_conv_w, ffn_conv_b, w_down,
                           final_norm_g)))
    m = dict(zip(WEIGHTS, (m_mix_norm_g, m_w_in, m_conv_dw_w, m_conv_dw_b, m_conv_ln_g, m_conv_ln_b, m_conv_pw_w,
                           m_conv_pw_b, m_gdn_conv_w, m_gdn_a_log, m_gdn_dt_bias, m_gdn_norm_g, m_w_out, m_ffn_norm_g,
                           m_w_up, m_ffn_conv_w, m_ffn_conv_b, m_w_down, m_final_norm_g)))
    v = dict(zip(WEIGHTS, (v_mix_norm_g, v_w_in, v_conv_dw_w, v_conv_dw_b, v_conv_ln_g, v_conv_ln_b, v_conv_pw_w,
                           v_conv_pw_b, v_gdn_conv_w, v_gdn_a_log, v_gdn_dt_bias, v_gdn_norm_g, v_w_out, v_ffn_norm_g,
                           v_w_up, v_ffn_conv_w, v_ffn_conv_b, v_w_down, v_final_norm_g)))
    loss, grad_x, out = _train_step(x, loss_target, w, m, v)
    return (loss, grad_x, *[out["grad"][n] for n in WEIGHTS], *[out["delta"][n] for n in WEIGHTS],
            *[out["new_m"][n] for n in WEIGHTS], *[out["new_v"][n] for n in WEIGHTS])
```

```python
import functools
import math

import jax
import jax.numpy as jnp
from jax import lax
from jax.experimental import pallas as pl
from jax.experimental.pallas import tpu as pltpu
from jax.experimental.pallas import tpu_sc as plsc

F32 = jnp.float32
BF16 = jnp.bfloat16
MESH = pl.DeviceIdType.MESH

EPS = 1e-6
LANES = 128
CHUNK = 128
NEAR_BLOCK = 32
CONV_K = 31
SHORT_CONV_K = 4
FFN_CONV_K = 3
N_DEV = 8
VMEM_LIMIT = 56 * 1024 * 1024

ADAM_LR = 0.001
ADAM_B1 = 0.9
ADAM_B2 = 0.999
ADAM_EPS = 1e-08
ADAM_WD = 0.01
ADAM_STEP = 10


def _cparams(sem):
    return pltpu.CompilerParams(dimension_semantics=sem, vmem_limit_bytes=VMEM_LIMIT)


def _sig(x):
    return 1.0 / (1.0 + jnp.exp(-x))


def _silu(x):
    return x * _sig(x)


def _dsilu(x):
    s = _sig(x)
    return s * (1.0 + x * (1.0 - s))


def _softplus(x):
    return jnp.maximum(x, 0.0) + jnp.log1p(jnp.exp(-jnp.abs(x)))


def _dot(a, b):
    return jnp.dot(a, b, preferred_element_type=F32)


def _dot_nt(a, b):
    return lax.dot_general(a, b, (((1,), (1,)), ((), ())), preferred_element_type=F32)


def _dot_tn(a, b):
    return lax.dot_general(a, b, (((0,), (0,)), ((), ())), preferred_element_type=F32)


def _bf(x):
    return x.astype(BF16)


_NN = (((1,), (0,)), ((), ()))
_TN = (((0,), (0,)), ((), ()))


def _split2(x):
    hi = _bf(x)
    return hi, _bf(x - hi.astype(F32))


def _dot_x3(a, b, dn=_NN):
    ah, al = _split2(a)
    bh, bl = _split2(b)
    f = lambda p, q: lax.dot_general(p, q, dn, preferred_element_type=F32)
    return f(ah, bh) + (f(al, bh) + f(ah, bl))


def _dot_mask(mask, x, dn=_NN):
    mb = _bf(mask)
    hi, lo = _split2(x)
    lo2 = _bf(x - hi.astype(F32) - lo.astype(F32))
    f = lambda q: lax.dot_general(mb, q, dn, preferred_element_type=F32)
    return f(hi) + (f(lo) + f(lo2))


def _shift_down(u, s):
    if s == 0:
        return u
    row = lax.broadcasted_iota(jnp.int32, u.shape, 0)
    return jnp.where(row >= s, pltpu.roll(u, s, 0), 0.0)


def _shift_up(u, s):
    if s == 0:
        return u
    n = u.shape[0]
    row = lax.broadcasted_iota(jnp.int32, u.shape, 0)
    return jnp.where(row < n - s, pltpu.roll(u, n - s, 0), 0.0)


def _conv_fwd(u, w_ref, K):
    acc = None
    for k in range(K):
        term = w_ref[k:k + 1, :] * _shift_down(u, K - 1 - k)
        acc = term if acc is None else acc + term
    return acc


def _conv_bwd_in(do, w_ref, K):
    acc = None
    for k in range(K):
        term = w_ref[k:k + 1, :] * _shift_up(do, K - 1 - k)
        acc = term if acc is None else acc + term
    return acc


def _conv_bwd_w(do, u, dw_ref, K, first):
    for k in range(K):
        row = jnp.sum(do * _shift_down(u, K - 1 - k), axis=0, keepdims=True)
        _acc_row(dw_ref, k, row, first)


def _acc_row(ref, k, row, first):
    @pl.when(first)
    def _():
        ref[k:k + 1, :] = row

    @pl.when(jnp.logical_not(first))
    def _():
        ref[k:k + 1, :] += row


def _logical(arr):
    if arr.ndim == 2:
        return arr.shape
    return (arr.shape[1], arr.shape[0] * arr.shape[2])


def _tile(dim, pref, *col_widths):
    if dim % LANES:
        assert not col_widths
        return dim
    t = (min(pref, dim) // LANES) * LANES
    while t > LANES and (dim % t or any(c % t for c in col_widths)):
        t -= LANES
    assert dim % t == 0 and all(c % t == 0 for c in col_widths), (dim, pref, col_widths)
    return t


def _spec(shape, rt, ct, rfn, cfn):
    if len(shape) == 2:
        return pl.BlockSpec((rt, ct), lambda i, j, k: (rfn(i, j, k), cfn(i, j, k)))
    per = shape[2] // ct
    return pl.BlockSpec((None, rt, ct),
                        lambda i, j, k: (cfn(i, j, k) // per, rfn(i, j, k), cfn(i, j, k) % per))


def _mm(a, b, *, name, ta=False, tb=False, out_dtype=F32, out_blocks=None, bias=None, res=None, b_rows=None,
        tm=1024, tn=1024, tk=2048):
    ra, ca = _logical(a)
    rb, cb = _logical(b)
    if b_rows is not None:
        assert b.ndim == 2 and b_rows <= rb
        rb = b_rows
    M, K = (ca, ra) if ta else (ra, ca)
    N, K2 = (rb, cb) if tb else (cb, rb)
    assert K == K2, (a.shape, b.shape, ta, tb)
    out_shape = (M, N) if out_blocks is None else (out_blocks, M, N // out_blocks)
    cw = lambda arr: [arr.shape[2]] if arr.ndim == 3 else []
    m_c = cw(a) if ta else []
    k_c = (cw(a) if not ta else []) + (cw(b) if tb else [])
    n_c = (cw(b) if not tb else []) + ([out_shape[2]] if out_blocks else []) + (cw(res) if res is not None else [])
    tm, tn, tk = _tile(M, tm, *m_c), _tile(N, tn, *n_c), _tile(K, tk, *k_c)
    nk = K // tk
    im, jn, kk = (lambda i, j, k: i), (lambda i, j, k: j), (lambda i, j, k: k)
    in_specs = [
        _spec(a.shape, tk, tm, kk, im) if ta else _spec(a.shape, tm, tk, im, kk),
        _spec(b.shape, tn, tk, jn, kk) if tb else _spec(b.shape, tk, tn, kk, jn),
    ]
    args = [a, b]
    if bias is not None:
        in_specs.append(pl.BlockSpec((1, tn), lambda i, j, k: (0, j)))
        args.append(bias.reshape(1, N).astype(F32))
    if res is not None:
        in_specs.append(_spec(res.shape, tm, tn, im, jn))
        args.append(res)
    dn = (((0 if ta else 1,), (1 if tb else 0,)), ((), ()))

    def body(*refs):
        a_ref, b_ref = refs[0], refs[1]
        pos = 2
        bias_ref = res_ref = None
        if bias is not None:
            bias_ref = refs[pos]
            pos += 1
        if res is not None:
            res_ref = refs[pos]
            pos += 1
        o_ref = refs[pos]
        k = pl.program_id(2)
        part = lax.dot_general(_bf(a_ref[...]), _bf(b_ref[...]), dn, preferred_element_type=F32)

        def finish(r):
            if bias_ref is not None:
                r = r + bias_ref[...]
            if res_ref is not None:
                r = r + res_ref[...].astype(F32)
            o_ref[...] = r.astype(out_dtype)

        if nk == 1:
            finish(part)
            return
        acc_ref = refs[pos + 1]

        @pl.when(k == 0)
        def _():
            acc_ref[...] = part

        @pl.when((k > 0) & (k < nk - 1))
        def _():
            acc_ref[...] += part

        @pl.when(k == nk - 1)
        def _():
            finish(acc_ref[...] + part)

    return pl.pallas_call(
        body, name=name,
        grid=(M // tm, N // tn, nk),
        in_specs=in_specs,
        out_specs=_spec(out_shape, tm, tn, im, jn),
        out_shape=jax.ShapeDtypeStruct(out_shape, out_dtype),
        scratch_shapes=[pltpu.VMEM((tm, tn), F32)] if nk > 1 else [],
        compiler_params=_cparams(("parallel", "parallel", "arbitrary")),
    )(*args)


def _rms_fwd(x, g, *, name, tr=512):
    T, D = x.shape
    tr = min(tr, T)

    def body(x_ref, g_ref, h_ref):
        xv = x_ref[...]
        r = lax.rsqrt(jnp.mean(xv * xv, axis=-1, keepdims=True) + EPS)
        h_ref[...] = (xv * r * g_ref[...]).astype(BF16)

    return pl.pallas_call(
        body, name=name, grid=(T // tr,),
        in_specs=[pl.BlockSpec((tr, D), lambda i: (i, 0)), pl.BlockSpec((1, D), lambda i: (0, 0))],
        out_specs=pl.BlockSpec((tr, D), lambda i: (i, 0)),
        out_shape=jax.ShapeDtypeStruct((T, D), BF16),
        compiler_params=_cparams(("parallel",)),
    )(x, g.reshape(1, D))


def _rms_bwd(x, g, dh, dres, *, name, tr=512):
    T, D = x.shape
    tr = min(tr, T)

    def body(x_ref, g_ref, dh_ref, dres_ref, dx_ref, dxb_ref, dg_ref):
        i = pl.program_id(0)
        xv = x_ref[...]
        dy = dh_ref[...].astype(F32)
        r = lax.rsqrt(jnp.mean(xv * xv, axis=-1, keepdims=True) + EPS)
        dyg = dy * g_ref[...]
        dot = jnp.mean(dyg * xv, axis=-1, keepdims=True)
        dx = dres_ref[...] + r * dyg - xv * (r * r * r) * dot
        dx_ref[...] = dx
        dxb_ref[...] = dx.astype(BF16)
        part = jnp.sum(dy * xv * r, axis=0, keepdims=True)
        _acc_row(dg_ref, 0, part, i == 0)

    row = pl.BlockSpec((tr, D), lambda i: (i, 0))
    vec = pl.BlockSpec((1, D), lambda i: (0, 0))
    return pl.pallas_call(
        body, name=name, grid=(T // tr,),
        in_specs=[row, vec, row, row],
        out_specs=[row, row, vec],
        out_shape=[jax.ShapeDtypeStruct((T, D), F32), jax.ShapeDtypeStruct((T, D), BF16),
                   jax.ShapeDtypeStruct((1, D), F32)],
        compiler_params=_cparams(("arbitrary",)),
    )(x, g.reshape(1, D), dh, dres)


def _loss_head(x, g, target, *, name, tr=512):
    T, D = x.shape
    tr = min(tr, T)

    def body(x_ref, g_ref, t_ref, loss_ref, dx_ref, dxb_ref, dg_ref):
        i = pl.program_id(0)
        xv = x_ref[...]
        gv = g_ref[...]
        r = lax.rsqrt(jnp.mean(xv * xv, axis=-1, keepdims=True) + EPS)
        y = xv * r * gv
        err = y - t_ref[...]
        lpart = 0.5 * jnp.sum(jnp.mean(err * err, axis=-1, keepdims=True), axis=0, keepdims=True)
        dy = err * (1.0 / D)
        dyg = dy * gv
        dot = jnp.mean(dyg * xv, axis=-1, keepdims=True)
        dx = r * dyg - xv * (r * r * r) * dot
        dx_ref[...] = dx
        dxb_ref[...] = dx.astype(BF16)
        _acc_row(dg_ref, 0, jnp.sum(dy * xv * r, axis=0, keepdims=True), i == 0)
        _acc_row(loss_ref, 0, jnp.broadcast_to(lpart, (1, LANES)), i == 0)

    row = pl.BlockSpec((tr, D), lambda i: (i, 0))
    return pl.pallas_call(
        body, name=name, grid=(T // tr,),
        in_specs=[row, pl.BlockSpec((1, D), lambda i: (0, 0)), row],
        out_specs=[pl.BlockSpec((1, LANES), lambda i: (0, 0)), row, row, pl.BlockSpec((1, D), lambda i: (0, 0))],
        out_shape=[jax.ShapeDtypeStruct((1, LANES), F32), jax.ShapeDtypeStruct((T, D), F32),
                   jax.ShapeDtypeStruct((T, D), BF16), jax.ShapeDtypeStruct((1, D), F32)],
        compiler_params=_cparams(("arbitrary",)),
    )(x, g.reshape(1, D), target)


def _conf_norm(u1, lg_ref, lb_ref):
    mu = jnp.mean(u1, axis=-1, keepdims=True)
    xc = u1 - mu
    r = lax.rsqrt(jnp.mean(xc * xc, axis=-1, keepdims=True) + EPS)
    n = xc * r
    return r, n, n * lg_ref[...] + lb_ref[...]


def _conf_fwd(p, dw_w, dw_b, ln_g, ln_b, *, Bl, S, CC, name):
    G = CC // LANES

    def body(av_ref, ag_ref, w_ref, b_ref, lg_ref, lb_ref, o_ref, u1_ref):
        u0 = av_ref[...] * _sig(ag_ref[...])
        u1 = _conv_fwd(u0, w_ref, CONV_K) + b_ref[...]
        u1_ref[...] = u1
        _, _, u2 = _conf_norm(u1, lg_ref, lb_ref)
        o_ref[...] = _silu(u2).astype(BF16)

    vec = pl.BlockSpec((1, LANES), lambda b, j: (0, j))
    seq = pl.BlockSpec((S, LANES), lambda b, j: (b, j))
    return pl.pallas_call(
        body, name=name, grid=(Bl, G),
        in_specs=[seq, pl.BlockSpec((S, LANES), lambda b, j: (b, G + j)),
                  pl.BlockSpec((CONV_K, LANES), lambda b, j: (0, j)), vec, vec, vec],
        out_specs=[seq, seq],
        out_shape=[jax.ShapeDtypeStruct((Bl * S, CC), BF16), jax.ShapeDtypeStruct((Bl * S, CC), F32)],
        compiler_params=_cparams(("parallel", "parallel")),
    )(p, p, dw_w, dw_b.reshape(1, CC), ln_g.reshape(1, CC), ln_b.reshape(1, CC))


def _conf_bwd(p, u1, dw_w, ln_g, ln_b, du3, *, Bl, S, CC, name):
    G = CC // LANES

    def body(av_ref, ag_ref, u1_ref, w_ref, lg_ref, lb_ref, du3_ref,
             dav_ref, dag_ref, dw_ref, db_ref, dlg_ref, dlb_ref):
        first = pl.program_id(1) == 0
        av = av_ref[...]
        sg = _sig(ag_ref[...])
        u0 = av * sg
        r, n, u2 = _conf_norm(u1_ref[...], lg_ref, lb_ref)
        du2 = du3_ref[...] * _dsilu(u2)
        _acc_row(dlg_ref, 0, jnp.sum(du2 * n, axis=0, keepdims=True), first)
        _acc_row(dlb_ref, 0, jnp.sum(du2, axis=0, keepdims=True), first)
        dn = du2 * lg_ref[...]
        du1 = r * (dn - jnp.mean(dn, axis=-1, keepdims=True) - n * jnp.mean(dn * n, axis=-1, keepdims=True))
        _acc_row(db_ref, 0, jnp.sum(du1, axis=0, keepdims=True), first)
        _conv_bwd_w(du1, u0, dw_ref, CONV_K, first)
        du0 = _conv_bwd_in(du1, w_ref, CONV_K)
        dav_ref[...] = (du0 * sg).astype(BF16)
        dag_ref[...] = (du0 * av * sg * (1.0 - sg)).astype(BF16)

    vec = pl.BlockSpec((1, LANES), lambda j, b: (0, j))
    seq = pl.BlockSpec((S, LANES), lambda j, b: (b, j))
    return pl.pallas_call(
        body, name=name, grid=(G, Bl),
        in_specs=[seq, pl.BlockSpec((S, LANES), lambda j, b: (b, G + j)), seq,
                  pl.BlockSpec((CONV_K, LANES), lambda j, b: (0, j)), vec, vec, seq],
        out_specs=[seq, seq, pl.BlockSpec((CONV_K, LANES), lambda j, b: (0, j)), vec, vec, vec],
        out_shape=[jax.ShapeDtypeStruct((Bl * S, CC), BF16), jax.ShapeDtypeStruct((Bl * S, CC), BF16),
                   jax.ShapeDtypeStruct((CONV_K, CC), F32), jax.ShapeDtypeStruct((1, CC), F32),
                   jax.ShapeDtypeStruct((1, CC), F32), jax.ShapeDtypeStruct((1, CC), F32)],
        compiler_params=_cparams(("parallel", "arbitrary")),
    )(p, p, u1, dw_w, ln_g.reshape(1, CC), ln_b.reshape(1, CC), du3)


def _gdn_pre_fwd(p, conv_w, *, Bl, S, CC, KW, VW, name):
    NQK = 2 * KW // LANES
    NB = NQK + VW // LANES
    off = 2 * CC // LANES

    def body(x_ref, w_ref, o_ref):
        j = pl.program_id(1)
        s = _silu(_conv_fwd(x_ref[...], w_ref, SHORT_CONV_K))
        r = lax.rsqrt(jnp.sum(s * s, axis=-1, keepdims=True) + EPS)
        o_ref[...] = jnp.where(j < NQK, s * r, s)

    return pl.pallas_call(
        body, name=name, grid=(Bl, NB),
        in_specs=[pl.BlockSpec((S, LANES), lambda b, j: (b, off + j)),
                  pl.BlockSpec((SHORT_CONV_K, LANES), lambda b, j: (0, j))],
        out_specs=pl.BlockSpec((S, LANES), lambda b, j: (b, j)),
        out_shape=jax.ShapeDtypeStruct((Bl * S, NB * LANES), F32),
        compiler_params=_cparams(("parallel", "parallel")),
    )(p, conv_w)


def _gdn_pre_bwd(p, conv_w, dq, dk, dv, *, Bl, S, CC, KW, VW, name):
    HQ = KW // LANES
    H = VW // LANES
    NQK = 2 * HQ
    NB = NQK + H
    off = 2 * CC // LANES

    def body(x_ref, w_ref, dq_ref, dk_ref, dv_ref, dx_ref, dw_ref):
        j = pl.program_id(0)
        first = pl.program_id(1) == 0
        xv = x_ref[...]
        c = _conv_fwd(xv, w_ref, SHORT_CONV_K)
        s = _silu(c)
        r = lax.rsqrt(jnp.sum(s * s, axis=-1, keepdims=True) + EPS)
        dy = jnp.where(j < HQ, dq_ref[...], jnp.where(j < NQK, dk_ref[...], dv_ref[...]))
        ds_norm = r * dy - s * (r * r * r) * jnp.sum(s * dy, axis=-1, keepdims=True)
        ds = jnp.where(j < NQK, ds_norm, dy)
        dc = ds * _dsilu(c)
        _conv_bwd_w(dc, xv, dw_ref, SHORT_CONV_K, first)
        dx_ref[...] = _conv_bwd_in(dc, w_ref, SHORT_CONV_K).astype(BF16)

    return pl.pallas_call(
        body, name=name, grid=(NB, Bl),
        in_specs=[pl.BlockSpec((S, LANES), lambda j, b: (b, off + j)),
                  pl.BlockSpec((SHORT_CONV_K, LANES), lambda j, b: (0, j)),
                  pl.BlockSpec((S, LANES), lambda j, b: (b, jnp.minimum(j, HQ - 1))),
                  pl.BlockSpec((S, LANES), lambda j, b: (b, jnp.clip(j - HQ, 0, HQ - 1))),
                  pl.BlockSpec((S, LANES), lambda j, b: (b, jnp.clip(j - NQK, 0, H - 1)))],
        out_specs=[pl.BlockSpec((S, LANES), lambda j, b: (b, j)),
                   pl.BlockSpec((SHORT_CONV_K, LANES), lambda j, b: (0, j))],
        out_shape=[jax.ShapeDtypeStruct((Bl * S, NB * LANES), BF16),
                   jax.ShapeDtypeStruct((SHORT_CONV_K, NB * LANES), F32)],
        compiler_params=_cparams(("parallel", "arbitrary")),
    )(p, conv_w, dq, dk, dv)


def _split3(x):
    hi, lo = _split2(x)
    return hi, lo, _bf(x - hi.astype(F32) - lo.astype(F32))


def _lane_replicate(parts, h):
    row = lax.broadcasted_iota(jnp.int32, (LANES, LANES), 0)
    E = jnp.where(row == h, 1.0, 0.0).astype(BF16)
    return _dot(parts[0], E) + (_dot(parts[1], E) + _dot(parts[2], E))


def _gdn_gate_fwd(pba, a_log, dt_bias, *, Bl, S, H, name):
    def body(alog_ref, dtb_ref, x_ref, g_ref, beta_ref):
        parts = _split3(x_ref[...])
        for h in range(H):
            b_raw = _lane_replicate(parts, h)
            a_raw = _lane_replicate(parts, H + h)
            beta_ref[h] = _sig(b_raw)
            ea = jnp.exp(jnp.zeros((1, LANES), F32) + alog_ref[h])
            g_ref[h] = -ea * _softplus(a_raw + dtb_ref[h])

    smem = pl.BlockSpec(memory_space=pltpu.SMEM)
    rep = pl.BlockSpec((H, S, LANES), lambda b: (0, b, 0))
    return pl.pallas_call(
        body, name=name, grid=(Bl,),
        in_specs=[smem, smem, pl.BlockSpec((S, LANES), lambda b: (b, 0))],
        out_specs=[rep, rep],
        out_shape=[jax.ShapeDtypeStruct((H, Bl * S, LANES), F32)] * 2,
        compiler_params=_cparams(("parallel",)),
    )(a_log, dt_bias, pba)


def _gdn_gate_bwd(pba, a_log, dt_bias, dg, dbeta, *, Bl, S, H, name):
    HP = 8 * ((H + 7) // 8)

    def body(alog_ref, dtb_ref, x_ref, dg_ref, dbeta_ref, dx_ref, dalog_ref, ddtb_ref):
        first = pl.program_id(0) == 0
        parts = _split3(x_ref[...])
        lane = lax.broadcasted_iota(jnp.int32, (S, LANES), 1)
        acc = jnp.zeros((S, LANES), F32)

        @pl.when(first)
        def _():
            dalog_ref[...] = jnp.zeros_like(dalog_ref)
            ddtb_ref[...] = jnp.zeros_like(ddtb_ref)

        for h in range(H):
            b_raw = _lane_replicate(parts, h)
            a_raw = _lane_replicate(parts, H + h)
            beta = _sig(b_raw)
            db_raw = dbeta_ref[h] * beta * (1.0 - beta)
            z = a_raw + dtb_ref[h]
            ea = jnp.exp(jnp.zeros((1, LANES), F32) + alog_ref[h])
            dgv = dg_ref[h]
            da_raw = dgv * (-ea) * _sig(z)
            g = -ea * _softplus(z)
            dalog_ref[h:h + 1, :] += jnp.sum(dgv * g, axis=0, keepdims=True)
            ddtb_ref[h:h + 1, :] += jnp.sum(da_raw, axis=0, keepdims=True)
            acc = acc + jnp.where(lane == h, db_raw, 0.0) + jnp.where(lane == H + h, da_raw, 0.0)
        dx_ref[...] = acc.astype(BF16)

    smem = pl.BlockSpec(memory_space=pltpu.SMEM)
    rep = pl.BlockSpec((H, S, LANES), lambda b: (0, b, 0))
    small = pl.BlockSpec((HP, LANES), lambda b: (0, 0))
    return pl.pallas_call(
        body, name=name, grid=(Bl,),
        in_specs=[smem, smem, pl.BlockSpec((S, LANES), lambda b: (b, 0)), rep, rep],
        out_specs=[pl.BlockSpec((S, LANES), lambda b: (b, 0)), small, small],
        out_shape=[jax.ShapeDtypeStruct((Bl * S, LANES), BF16),
                   jax.ShapeDtypeStruct((HP, LANES), F32), jax.ShapeDtypeStruct((HP, LANES), F32)],
        compiler_params=_cparams(("arbitrary",)),
    )(a_log, dt_bias, pba, dg, dbeta)


def _tri_masks():
    ri = lax.broadcasted_iota(jnp.int32, (CHUNK, CHUNK), 0)
    ci = lax.broadcasted_iota(jnp.int32, (CHUNK, CHUNK), 1)
    return ri >= ci, ri > ci, ri == CHUNK - 1


def _tri_inv(L):
    ri = lax.broadcasted_iota(jnp.int32, (CHUNK, CHUNK), 0)
    ci = lax.broadcasted_iota(jnp.int32, (CHUNK, CHUNK), 1)
    T = jnp.where(ri == ci, 1.0, 0.0) - jnp.where((ri >> 1) == (ci >> 1), L, 0.0)
    for lv in range(2, int(math.log2(CHUNK)) + 1):
        O = jnp.where(((ri >> lv) == (ci >> lv)) & ((ri >> (lv - 1)) != (ci >> (lv - 1))), L, 0.0)
        if (1 << lv) <= NEAR_BLOCK:
            T = T - _dot_x3(T, _dot_x3(O, T))
        else:
            Tb = _bf(T)
            T = T - _dot(Tb, _bf(_dot(_bf(O), Tb)))
    return T


def _chunk_local(q, k, v, beta, g):
    ge, gt, last = _tri_masks()
    gam = _dot_mask(ge, g)
    D = jnp.where(ge, jnp.exp(jnp.where(ge, gam - gam.T, 0.0)), 0.0)
    kb = k * beta
    vb = v * beta
    M = _dot_nt(_bf(kb), _bf(k))
    L = jnp.where(gt, M * D, 0.0)
    eg = jnp.exp(gam)
    kbg = kb * eg
    P = _dot_nt(_bf(q), _bf(k))
    QK = jnp.where(ge, P * D, 0.0)
    gl = jnp.sum(jnp.where(last, gam, 0.0), axis=0, keepdims=True)
    el = jnp.exp(gl - gam)
    return dict(ge=ge, gt=gt, last=last, gam=gam, D=D, kb=kb, vb=vb, L=L, eg=eg, kbg=kbg, QK=QK, gl=gl,
                el=el, kd=k * el, qg=q * eg)


def _rowsum(x):
    return jnp.sum(x, axis=-1, keepdims=True)


def _chunk_bwd(q, k, v, beta, g, S, T, u, w, do, dS2):
    c = _chunk_local(q, k, v, beta, g)
    ge, gt, last = c["ge"], c["gt"], c["last"]
    Sb = _bf(S)
    vn = u - _dot(w, Sb)
    dob, vnb, dS2b = _bf(do), _bf(vn), _bf(dS2)
    e_last = jnp.exp(c["gl"])
    dqg = _dot_nt(dob, Sb)
    dS = _dot_tn(_bf(c["qg"]), dob)
    dQK = jnp.where(ge, _dot_nt(dob, vnb), 0.0)
    dvn = _dot_tn(_bf(c["QK"]), dob)
    dS = dS + dS2 * e_last
    de_last = jnp.sum(jnp.sum(dS2 * S, axis=0, keepdims=True), axis=1, keepdims=True)
    dkd = _dot_nt(vnb, dS2b)
    dvn = dvn + _dot(_bf(c["kd"]), dS2b)
    dvnb = _bf(dvn)
    dw = -_dot_nt(dvnb, Sb)
    dS = dS - _dot_tn(w, dvnb)
    dsol = _dot_x3(T, jnp.concatenate([dvn, dw], axis=1), _TN)
    dvb, dkbg = dsol[:, :LANES], dsol[:, LANES:]
    dA = -(_dot_nt(_bf(dvb), _bf(u)) + _dot_nt(_bf(dkbg), w))
    dL = jnp.where(gt, dA, 0.0)
    dM = dL * c["D"]
    dP = dQK * c["D"]
    E = dL * c["L"] + dQK * c["QK"]
    kbf = _bf(k)
    dkb = _dot(_bf(dM), kbf) + dkbg * c["eg"]
    dk = _dot_tn(_bf(dM), _bf(c["kb"])) + _dot_tn(_bf(dP), _bf(q)) + dkd * c["el"] + dkb * beta
    dq = _dot(_bf(dP), kbf) + dqg * c["eg"]
    s_kd = _rowsum(dkd * c["kd"])
    dgam = (_rowsum(E) - _rowsum(E.T) + _rowsum(dqg * c["qg"]) - s_kd + _rowsum(dkbg * c["kbg"]))
    dgl = jnp.sum(s_kd, axis=0, keepdims=True) + de_last * e_last
    dgam_rep = jnp.broadcast_to(dgam, (CHUNK, LANES)) + jnp.where(last, jnp.broadcast_to(dgl, (CHUNK, LANES)), 0.0)
    dg_rep = _dot_mask(ge, dgam_rep, _TN)
    dbeta = _rowsum(dkb * k) + _rowsum(dvb * v)
    dv = dvb * beta
    return dq, dk, dv, jnp.broadcast_to(dbeta, (CHUNK, LANES)), dg_rep, dS


def _gdn_core_fwd(qkv, g, beta, *, Bl, S, KW, VW, name):
    HQ = KW // LANES
    H = VW // LANES
    NC = S // CHUNK
    scale = float(LANES) ** -0.5

    PAIR = 2 if NC % 2 == 0 else 1

    def body(q_ref, k_ref, v_ref, g_ref, beta_ref, o_ref, st_ref, t_ref, u_s, w_s,
             qk_s, qg_s, kd_s, el_s):
        def local(n2, carry):
            for half in range(PAIR):
                n = n2 * PAIR + half
                rows = pl.ds(pl.multiple_of(n * CHUNK, CHUNK), CHUNK)
                q = q_ref[rows, :] * scale
                k = k_ref[rows, :]
                for e in range(2):
                    c = _chunk_local(q, k, v_ref[rows, e * LANES:(e + 1) * LANES], beta_ref[e, rows, :],
                                     g_ref[e, rows, :])
                    T = _tri_inv(c["L"])
                    t_ref[e, rows, :] = T
                    uw = _dot_x3(T, jnp.concatenate([c["vb"], c["kbg"]], axis=1))
                    u_s[e, rows, :] = uw[:, :LANES]
                    w_s[e, rows, :] = _bf(uw[:, LANES:])
                    qk_s[e, rows, :] = _bf(c["QK"])
                    qg_s[e, rows, :] = _bf(c["qg"])
                    kd_s[e, rows, :] = _bf(c["kd"])
                    el_s[e, pl.ds(pl.multiple_of(n * 8, 8), 8), :] = jnp.broadcast_to(jnp.exp(c["gl"]), (8, LANES))
            return carry

        lax.fori_loop(0, NC // PAIR, local, 0)

        def scan(n, states):
            rows = pl.ds(pl.multiple_of(n * CHUNK, CHUNK), CHUNK)
            out = []
            for e in range(2):
                S_in = states[e]
                st_ref[e, n] = S_in
                Sb = _bf(S_in)
                vn = u_s[e, rows, :] - _dot(w_s[e, rows, :], Sb)
                vnb = _bf(vn)
                o_ref[rows, e * LANES:(e + 1) * LANES] = _dot(qg_s[e, rows, :], Sb) + _dot(qk_s[e, rows, :], vnb)
                e_last = el_s[e, pl.ds(pl.multiple_of(n * 8, 8), 1), :]
                out.append(S_in * e_last + _dot_tn(kd_s[e, rows, :], vnb))
            return tuple(out)

        z = jnp.zeros((LANES, LANES), F32)
        lax.fori_loop(0, NC, scan, (z, z))

    rep = pl.BlockSpec((2, S, LANES), lambda b, h: (h, b, 0))
    return pl.pallas_call(
        body, name=name, grid=(Bl, HQ),
        in_specs=[pl.BlockSpec((S, LANES), lambda b, h: (b, h)),
                  pl.BlockSpec((S, LANES), lambda b, h: (b, HQ + h)),
                  pl.BlockSpec((S, 2 * LANES), lambda b, h: (b, HQ + h)), rep, rep],
        out_specs=[pl.BlockSpec((S, 2 * LANES), lambda b, h: (b, h)),
                   pl.BlockSpec((None, 2, NC, LANES, LANES), lambda b, h: (b, h, 0, 0, 0)), rep, rep, rep],
        out_shape=[jax.ShapeDtypeStruct((Bl * S, VW), F32),
                   jax.ShapeDtypeStruct((Bl, H, NC, LANES, LANES), F32),
                   jax.ShapeDtypeStruct((H, Bl * S, LANES), F32),
                   jax.ShapeDtypeStruct((H, Bl * S, LANES), F32),
                   jax.ShapeDtypeStruct((H, Bl * S, LANES), BF16)],
        scratch_shapes=[pltpu.VMEM((2, S, LANES), BF16)] * 3 + [pltpu.VMEM((2, NC * 8, LANES), F32)],
        compiler_params=_cparams(("parallel", "parallel")),
    )(qkv, qkv, qkv, g, beta)


def _gdn_core_bwd(qkv, g, beta, states, tinv, u, w, do, *, Bl, S, KW, VW, name):
    HQ = KW // LANES
    H = VW // LANES
    NC = S // CHUNK
    scale = float(LANES) ** -0.5

    def body(q_ref, k_ref, v_ref, g_ref, beta_ref, st_ref, t_ref, u_ref, w_ref, do_ref,
             dq_ref, dk_ref, dv_ref, dg_ref, dbeta_ref):
        def step(i, dstates):
            n = NC - 1 - i
            rows = pl.ds(pl.multiple_of(n * CHUNK, CHUNK), CHUNK)
            q = q_ref[rows, :] * scale
            k = k_ref[rows, :]
            out = []
            dq_sum = dk_sum = None
            for e in range(2):
                cols = slice(e * LANES, (e + 1) * LANES)
                dq, dk, dv, dbeta, dg, dS = _chunk_bwd(q, k, v_ref[rows, cols], beta_ref[e, rows, :],
                                                       g_ref[e, rows, :], st_ref[e, n], t_ref[e, rows, :],
                                                       u_ref[e, rows, :], w_ref[e, rows, :],
                                                       do_ref[rows, cols], dstates[e])
                dv_ref[rows, cols] = dv
                dg_ref[e, rows, :] = dg
                dbeta_ref[e, rows, :] = dbeta
                dq_sum = dq if dq_sum is None else dq_sum + dq
                dk_sum = dk if dk_sum is None else dk_sum + dk
                out.append(dS)
            dq_ref[rows, :] = dq_sum * scale
            dk_ref[rows, :] = dk_sum
            return tuple(out)

        z = jnp.zeros((LANES, LANES), F32)
        lax.fori_loop(0, NC, step, (z, z))

    rep = pl.BlockSpec((2, S, LANES), lambda b, h: (h, b, 0))
    seq = pl.BlockSpec((S, LANES), lambda b, h: (b, h))
    seq2 = pl.BlockSpec((S, 2 * LANES), lambda b, h: (b, h))
    return pl.pallas_call(
        body, name=name, grid=(Bl, HQ),
        in_specs=[seq, pl.BlockSpec((S, LANES), lambda b, h: (b, HQ + h)),
                  pl.BlockSpec((S, 2 * LANES), lambda b, h: (b, HQ + h)), rep, rep,
                  pl.BlockSpec((None, 2, NC, LANES, LANES), lambda b, h: (b, h, 0, 0, 0)), rep, rep, rep, seq2],
        out_specs=[seq, seq, seq2, rep, rep],
        out_shape=[jax.ShapeDtypeStruct((Bl * S, KW), F32), jax.ShapeDtypeStruct((Bl * S, KW), F32),
                   jax.ShapeDtypeStruct((Bl * S, VW), F32),
                   jax.ShapeDtypeStruct((H, Bl * S, LANES), F32), jax.ShapeDtypeStruct((H, Bl * S, LANES), F32)],
        compiler_params=_cparams(("parallel", "parallel")),
    )(qkv, qkv, qkv, g, beta, states, tinv, u, w, do)


def _gdn_out_fwd(o, p, norm_g, out_a, *, CC, VW, name, tr=256):
    T = o.shape[0]
    tr = min(tr, T)
    H = VW // LANES
    zoff = p.shape[1] // VW - 1

    def body(o_ref, z_ref, ng_ref, a_ref, mix_ref):
        mix_ref[:, :CC] = a_ref[...]
        for h in range(H):
            cols = slice(h * LANES, (h + 1) * LANES)
            ov = o_ref[:, cols]
            r = lax.rsqrt(jnp.mean(ov * ov, axis=-1, keepdims=True) + EPS)
            mix_ref[:, CC + h * LANES:CC + (h + 1) * LANES] = (ov * r * ng_ref[...] * _silu(z_ref[:, cols])).astype(BF16)

    return pl.pallas_call(
        body, name=name, grid=(T // tr,),
        in_specs=[pl.BlockSpec((tr, VW), lambda i: (i, 0)), pl.BlockSpec((tr, VW), lambda i: (i, zoff)),
                  pl.BlockSpec((1, LANES), lambda i: (0, 0)), pl.BlockSpec((tr, CC), lambda i: (i, 0))],
        out_specs=pl.BlockSpec((tr, CC + VW), lambda i: (i, 0)),
        out_shape=jax.ShapeDtypeStruct((T, CC + VW), BF16),
        compiler_params=_cparams(("parallel",)),
    )(o, p, norm_g.reshape(1, LANES), out_a)


def _gdn_out_bwd(o, p, norm_g, dmix, *, CC, VW, name, tr=256):
    T = o.shape[0]
    tr = min(tr, T)
    H = VW // LANES
    zoff = p.shape[1] // VW - 1

    def body(o_ref, z_ref, ng_ref, dmix_ref, do_ref, dz_ref, da_ref, dng_ref, dpb_ref):
        first = pl.program_id(0) == 0
        da = dmix_ref[:, :CC]
        da_ref[...] = da.astype(BF16)
        _acc_row(dpb_ref, 0, jnp.sum(da, axis=0, keepdims=True), first)
        ng = ng_ref[...]
        dng = jnp.zeros((1, LANES), F32)
        for h in range(H):
            cols = slice(h * LANES, (h + 1) * LANES)
            ov = o_ref[:, cols]
            zv = z_ref[:, cols]
            dout = dmix_ref[:, CC + h * LANES:CC + (h + 1) * LANES]
            r = lax.rsqrt(jnp.mean(ov * ov, axis=-1, keepdims=True) + EPS)
            on = ov * r * ng
            don = dout * _silu(zv)
            dz_ref[:, cols] = (dout * on * _dsilu(zv)).astype(BF16)
            dng = dng + jnp.sum(don * ov * r, axis=0, keepdims=True)
            dong = don * ng
            do_ref[:, cols] = r * dong - ov * (r * r * r) * jnp.mean(dong * ov, axis=-1, keepdims=True)
        _acc_row(dng_ref, 0, dng, first)

    return pl.pallas_call(
        body, name=name, grid=(T // tr,),
        in_specs=[pl.BlockSpec((tr, VW), lambda i: (i, 0)), pl.BlockSpec((tr, VW), lambda i: (i, zoff)),
                  pl.BlockSpec((1, LANES), lambda i: (0, 0)), pl.BlockSpec((tr, CC + VW), lambda i: (i, 0))],
        out_specs=[pl.BlockSpec((tr, VW), lambda i: (i, 0)), pl.BlockSpec((tr, VW), lambda i: (i, 0)),
                   pl.BlockSpec((tr, CC), lambda i: (i, 0)), pl.BlockSpec((1, LANES), lambda i: (0, 0)),
                   pl.BlockSpec((1, CC), lambda i: (0, 0))],
        out_shape=[jax.ShapeDtypeStruct((T, VW), F32), jax.ShapeDtypeStruct((T, VW), BF16),
                   jax.ShapeDtypeStruct((T, CC), BF16), jax.ShapeDtypeStruct((1, LANES), F32),
                   jax.ShapeDtypeStruct((1, CC), F32)],
        compiler_params=_cparams(("arbitrary",)),
    )(o, p, norm_g.reshape(1, LANES), dmix)


FFN_CW = 256


def _ffn_act_fwd(gu, conv_w, conv_b, *, Bl, S, name):
    FF = gu.shape[2]
    cw = min(FFN_CW, FF)

    def body(g_ref, u_ref, w_ref, b_ref, a_ref):
        gc = _conv_fwd(g_ref[...], w_ref, FFN_CONV_K) + b_ref[...]
        a_ref[...] = (_silu(gc) * u_ref[...]).astype(BF16)

    return pl.pallas_call(
        body, name=name, grid=(Bl, FF // cw),
        in_specs=[pl.BlockSpec((None, S, cw), lambda b, j: (0, b, j)),
                  pl.BlockSpec((None, S, cw), lambda b, j: (1, b, j)),
                  pl.BlockSpec((FFN_CONV_K, cw), lambda b, j: (0, j)),
                  pl.BlockSpec((1, cw), lambda b, j: (0, j))],
        out_specs=pl.BlockSpec((S, cw), lambda b, j: (b, j)),
        out_shape=jax.ShapeDtypeStruct((Bl * S, FF), BF16),
        compiler_params=_cparams(("parallel", "parallel")),
    )(gu, gu, conv_w, conv_b.reshape(1, FF))


def _ffn_act_bwd(gu, conv_w, conv_b, da, *, Bl, S, name):
    FF = gu.shape[2]
    cw = min(FFN_CW, FF)

    def body(g_ref, u_ref, w_ref, b_ref, da_ref, dgu_ref, dw_ref, db_ref):
        first = pl.program_id(1) == 0
        gate = g_ref[...]
        gc = _conv_fwd(gate, w_ref, FFN_CONV_K) + b_ref[...]
        dav = da_ref[...]
        dgu_ref[1] = (dav * _silu(gc)).astype(BF16)
        dgc = dav * u_ref[...] * _dsilu(gc)
        _acc_row(db_ref, 0, jnp.sum(dgc, axis=0, keepdims=True), first)
        _conv_bwd_w(dgc, gate, dw_ref, FFN_CONV_K, first)
        dgu_ref[0] = _conv_bwd_in(dgc, w_ref, FFN_CONV_K).astype(BF16)

    return pl.pallas_call(
        body, name=name, grid=(FF // cw, Bl),
        in_specs=[pl.BlockSpec((None, S, cw), lambda j, b: (0, b, j)),
                  pl.BlockSpec((None, S, cw), lambda j, b: (1, b, j)),
                  pl.BlockSpec((FFN_CONV_K, cw), lambda j, b: (0, j)),
                  pl.BlockSpec((1, cw), lambda j, b: (0, j)),
                  pl.BlockSpec((S, cw), lambda j, b: (b, j))],
        out_specs=[pl.BlockSpec((2, S, cw), lambda j, b: (0, b, j)),
                   pl.BlockSpec((FFN_CONV_K, cw), lambda j, b: (0, j)),
                   pl.BlockSpec((1, cw), lambda j, b: (0, j))],
        out_shape=[jax.ShapeDtypeStruct((2, Bl * S, FF), BF16),
                   jax.ShapeDtypeStruct((FFN_CONV_K, FF), F32), jax.ShapeDtypeStruct((1, FF), F32)],
        compiler_params=_cparams(("parallel", "arbitrary")),
    )(gu, gu, conv_w, conv_b.reshape(1, FF), da)


def _layer_dims(W):
    CC = W["conv_pw_b"].shape[0]
    VW = W["mix_norm_g"].shape[0] - CC
    KW = (W["gdn_conv_w"].shape[1] - VW) // 2
    return CC, KW, VW


def _layer_fwd(l, x, W, Bl, S, fetch):
    CC, KW, VW = _layer_dims(W)
    H = VW // LANES
    w_in_t, w_in_ba = fetch(l, "w_in", x)
    n_main = (w_in_t.shape[0] // LANES) * LANES
    h1 = _rms_fwd(x, W["mix_norm_g"], name="rms1_fwd")
    p = _mm(h1, w_in_t, tb=True, b_rows=n_main, name="mm_in")
    pba = _mm(h1, w_in_ba, tb=True, name="mm_in_ba")
    u3, u1 = _conf_fwd(p, W["conv_dw_w"], W["conv_dw_b"], W["conv_ln_g"], W["conv_ln_b"], Bl=Bl, S=S, CC=CC,
                       name="conf_fwd")
    conv_pw_w = fetch(l, "conv_pw_w", u3)
    out_a = _mm(u3, conv_pw_w, bias=W["conv_pw_b"], out_dtype=BF16, name="mm_pw")
    qkv = _gdn_pre_fwd(p, W["gdn_conv_w"], Bl=Bl, S=S, CC=CC, KW=KW, VW=VW, name="gdn_pre_fwd")
    g, beta = _gdn_gate_fwd(pba, W["gdn_a_log"], W["gdn_dt_bias"], Bl=Bl, S=S, H=H, name="gdn_gate_fwd")
    o, states, tinv, gdn_u, gdn_w = _gdn_core_fwd(qkv, g, beta, Bl=Bl, S=S, KW=KW, VW=VW, name="gdn_core_fwd")
    mix = _gdn_out_fwd(o, p, W["gdn_norm_g"], out_a, CC=CC, VW=VW, name="gdn_out_fwd")
    w_out = fetch(l, "w_out", mix)
    x1 = _mm(mix, w_out, res=x, name="mm_out")
    h2 = _rms_fwd(x1, W["ffn_norm_g"], name="rms2_fwd")
    w_up = fetch(l, "w_up", h2)
    gu = _mm(h2, w_up, out_blocks=2, tn=w_up.shape[2], name="mm_up")
    a = _ffn_act_fwd(gu, W["ffn_conv_w"], W["ffn_conv_b"], Bl=Bl, S=S, name="ffn_act_fwd")
    w_down = fetch(l, "w_down", a)
    x2 = _mm(a, w_down, res=x1, name="mm_down")
    saved = dict(x=x, h1=h1, p=p, pba=pba, u1=u1, u3=u3, qkv=qkv, g=g, beta=beta, o=o, states=states, tinv=tinv,
                 gdn_u=gdn_u, gdn_w=gdn_w, mix=mix, x1=x1, h2=h2, gu=gu, a=a, w_in_t=w_in_t, w_in_ba=w_in_ba, conv_pw_w=conv_pw_w,
                 w_out=w_out, w_up=w_up, w_down=w_down)
    return x2, saved


def _layer_bwd(l, dx2, dx2b, W, A, Bl, S, sink):
    CC, KW, VW = _layer_dims(W)
    H = VW // LANES
    G = {}
    upw = A["w_up"].shape[2]
    da = _mm(dx2b, A["w_down"], tb=True, tn=upw, name="mm_down_dx")
    da = sink(l, "w_down", _mm(A["a"], dx2b, ta=True, out_dtype=BF16, tm=upw, name="mm_down_dw"), da)
    dgu, G["ffn_conv_w"], G["ffn_conv_b"] = _ffn_act_bwd(A["gu"], W["ffn_conv_w"], W["ffn_conv_b"], da,
                                                         Bl=Bl, S=S, name="ffn_act_bwd")
    dh2 = _mm(dgu, A["w_up"], tb=True, tk=upw, tn=2048, name="mm_up_dx")
    dh2 = sink(l, "w_up", _mm(A["h2"], dgu, ta=True, out_dtype=BF16, out_blocks=N_DEV, tn=upw, name="mm_up_dw"),
               dh2)
    dx1, dx1b, G["ffn_norm_g"] = _rms_bwd(A["x1"], W["ffn_norm_g"], dh2, dx2, name="rms2_bwd")
    dmix = _mm(dx1b, A["w_out"], tb=True, name="mm_out_dx")
    dmix = sink(l, "w_out", _mm(A["mix"], dx1b, ta=True, out_dtype=BF16, name="mm_out_dw"), dmix)
    do, dz, dout_a, G["gdn_norm_g"], G["conv_pw_b"] = _gdn_out_bwd(A["o"], A["p"], W["gdn_norm_g"], dmix,
                                                                   CC=CC, VW=VW, name="gdn_out_bwd")
    dq, dk, dv, dg, dbeta = _gdn_core_bwd(A["qkv"], A["g"], A["beta"], A["states"], A["tinv"], A["gdn_u"],
                                          A["gdn_w"], do, Bl=Bl, S=S, KW=KW, VW=VW, name="gdn_core_bwd")
    dpba, dalog, ddtb = _gdn_gate_bwd(A["pba"], W["gdn_a_log"], W["gdn_dt_bias"], dg, dbeta, Bl=Bl, S=S, H=H,
                                      name="gdn_gate_bwd")
    G["gdn_a_log"], G["gdn_dt_bias"] = dalog[:H, 0], ddtb[:H, 0]
    dqkv, G["gdn_conv_w"] = _gdn_pre_bwd(A["p"], W["gdn_conv_w"], dq, dk, dv, Bl=Bl, S=S, CC=CC, KW=KW, VW=VW,
                                         name="gdn_pre_bwd")
    du3 = _mm(dout_a, A["conv_pw_w"], tb=True, name="mm_pw_dx")
    du3 = sink(l, "conv_pw_w", _mm(A["u3"], dout_a, ta=True, out_dtype=BF16, name="mm_pw_dw"), du3)
    dav, dag, G["conv_dw_w"], G["conv_dw_b"], G["conv_ln_g"], G["conv_ln_b"] = _conf_bwd(
        A["p"], A["u1"], W["conv_dw_w"], W["conv_ln_g"], W["conv_ln_b"], du3, Bl=Bl, S=S, CC=CC, name="conf_bwd")
    dp = jnp.concatenate([dav, dag, dqkv, dz], axis=1)
    dh1 = _mm(dpba, A["w_in_ba"], name="mm_in_ba_dx")
    dh1 = _mm(dp, A["w_in_t"], b_rows=dp.shape[1], res=dh1, name="mm_in_dx")
    dh1 = sink(l, "w_in", (_mm(dp, A["h1"], ta=True, out_dtype=BF16, name="mm_in_dw"),
                           _mm(dpba, A["h1"], ta=True, out_dtype=BF16, name="mm_in_ba_dw")), dh1)
    dx, dxb, G["mix_norm_g"] = _rms_bwd(A["x"], W["mix_norm_g"], dh1, dx1, name="rms1_bwd")
    return dx, dxb, G


def _local_step(x, target, Ws, final_norm_g, fetch, sink):
    Bl, S, D = x.shape
    xt = x.reshape(Bl * S, D)
    acts = []
    for l, W in enumerate(Ws):
        xt, A = _layer_fwd(l, xt, W, Bl, S, fetch)
        acts.append(A)
    loss, dx, dxb, dgf = _loss_head(xt, final_norm_g, target.reshape(Bl * S, D), name="loss_head")
    grads = [None] * len(Ws)
    for l in reversed(range(len(Ws))):
        dx, dxb, grads[l] = _layer_bwd(l, dx, dxb, Ws[l], acts[l], Bl, S, sink)
    return loss[0, 0], dx.reshape(Bl, S, D), grads, dgf.reshape(D)


def _mesh_pos():
    return lax.axis_index("x"), lax.axis_index("y"), lax.axis_index("c")


def _dev_index(px, py, pc):
    return 4 * px + 2 * py + pc


_ANY = pl.BlockSpec(memory_space=pl.ANY)


def _all_gather(arrs, *, name):
    n = len(arrs)

    def body(*refs):
        ins, outs = refs[:n], refs[n:2 * n]
        send_sems, recv_sems, local_sems = refs[2 * n:]
        x, y, c = _mesh_pos()
        me, sibling = (x, y, c), (x, y, 1 - c)
        chips = [(1 - x, y), (x, 1 - y), (1 - x, 1 - y)]

        def copy(a, k, block, to, src=None):
            dst = outs[a].at[_dev_index(*block)]
            return pltpu.make_async_remote_copy(
                src_ref=dst if src is None else src, dst_ref=dst,
                send_sem=send_sems.at[a, k], recv_sem=recv_sems.at[a, k],
                device_id=to, device_id_type=MESH)

        mine = [pltpu.make_async_copy(ins[a], outs[a].at[_dev_index(*me)], local_sems.at[a]) for a in range(n)]
        for cp in mine:
            cp.start()
        first = []
        for a in range(n):
            first.append(copy(a, 0, me, sibling, src=ins[a]))
            first += [copy(a, 1 + j, me, (*chip, c), src=ins[a]) for j, chip in enumerate(chips)]
        for cp in first:
            cp.start()
        passed = []
        for a in range(n):
            for j, chip in enumerate(chips):
                copy(a, 1 + j, (*chip, c), me).wait_recv()
                fwd = copy(a, 4 + j, (*chip, c), sibling)
                fwd.start()
                passed.append(fwd)
        for a in range(n):
            copy(a, 0, sibling, me).wait_recv()
            for j, chip in enumerate(chips):
                copy(a, 4 + j, (*chip, 1 - c), me).wait_recv()
        for cp in first + passed:
            cp.wait_send()
        for cp in mine:
            cp.wait()

    return pl.pallas_call(
        body, name=name,
        in_specs=[_ANY] * n, out_specs=[_ANY] * n,
        out_shape=[jax.ShapeDtypeStruct((N_DEV,) + a.shape, a.dtype) for a in arrs],
        scratch_shapes=[pltpu.SemaphoreType.DMA((n, N_DEV - 1)), pltpu.SemaphoreType.DMA((n, N_DEV - 1)),
                        pltpu.SemaphoreType.DMA((n,))],
    )(*arrs)


def _peers(x, y, c):
    flip = lambda v, f: 1 - v if f else v
    return [(flip(x, p & 4), flip(y, p & 2), flip(c, p & 1)) for p in range(1, N_DEV)]


GATHER_ID, SCATTER_ID = 1, 2
_SEQUENCER = dict(axis_name="sequencer", num_cores=1)


def _handshake(peers):
    barrier = pltpu.get_barrier_semaphore()
    for peer in peers:
        pl.semaphore_signal(barrier, inc=1, device_id=peer, device_id_type=MESH)
    pl.semaphore_wait(barrier, len(peers))


def _sc_gather(src, *, name):
    def body(src_ref, zone_ref, send_sems, recv_sems, local_sem):
        x, y, c = _mesh_pos()
        me, sibling = (x, y, c), (x, y, 1 - c)
        chips = [(1 - x, y), (x, 1 - y), (1 - x, 1 - y)]
        _handshake([sibling] + [(*chip, c) for chip in chips])

        def copy(k, block, to, from_src=False):
            dst = zone_ref.at[_dev_index(*block)]
            return pltpu.make_async_remote_copy(
                src_ref=src_ref if from_src else dst, dst_ref=dst, send_sem=send_sems.at[k], recv_sem=recv_sems.at[k],
                device_id=to, device_id_type=MESH)

        mine = pltpu.make_async_copy(src_ref, zone_ref.at[_dev_index(*me)], local_sem)
        mine.start()
        first = [copy(1 + j, me, (*chip, c), from_src=True) for j, chip in enumerate(chips)]
        first.append(copy(0, me, sibling, from_src=True))
        for cp in first:
            cp.start()
        passed = []
        for j, chip in enumerate(chips):
            copy(1 + j, (*chip, c), me).wait_recv()
            fwd = copy(4 + j, (*chip, c), sibling)
            fwd.start()
            passed.append(fwd)
        copy(0, sibling, me).wait_recv()
        for j, chip in enumerate(chips):
            copy(4 + j, (*chip, 1 - c), me).wait_recv()
        for cp in first + passed:
            cp.wait_send()
        mine.wait()

    return pl.kernel(
        body, name=name,
        out_type=jax.ShapeDtypeStruct((N_DEV,) + src.shape, src.dtype),
        mesh=plsc.ScalarSubcoreMesh(**_SEQUENCER),
        scratch_types=[pltpu.SemaphoreType.DMA((N_DEV - 1,)), pltpu.SemaphoreType.DMA((N_DEV - 1,)),
                       pltpu.SemaphoreType.DMA],
        compiler_params=pltpu.CompilerParams(collective_id=GATHER_ID),
    )(src)


def _sc_scatter(part, *, name):
    def body(src_ref, zone_ref, send_sems, recv_sems, local_sem):
        x, y, c = _mesh_pos()
        me = _dev_index(x, y, c)
        peers = _peers(x, y, c)
        _handshake(peers)
        mine = pltpu.make_async_copy(src_ref.at[me], zone_ref.at[me], local_sem)
        mine.start()
        sends = [pltpu.make_async_remote_copy(
            src_ref=src_ref.at[_dev_index(*peer)], dst_ref=zone_ref.at[me], send_sem=send_sems.at[k],
            recv_sem=recv_sems.at[k], device_id=peer, device_id_type=MESH) for k, peer in enumerate(peers)]
        for cp in sends:
            cp.start()
        for k, peer in enumerate(peers):
            pltpu.make_async_remote_copy(
                src_ref=src_ref.at[me], dst_ref=zone_ref.at[_dev_index(*peer)], send_sem=send_sems.at[k],
                recv_sem=recv_sems.at[k], device_id=peer, device_id_type=MESH).wait_recv()
        for cp in sends:
            cp.wait_send()
        mine.wait()

    return pl.kernel(
        body, name=name,
        out_type=jax.ShapeDtypeStruct(part.shape, part.dtype),
        mesh=plsc.ScalarSubcoreMesh(**_SEQUENCER),
        scratch_types=[pltpu.SemaphoreType.DMA((N_DEV - 1,)), pltpu.SemaphoreType.DMA((N_DEV - 1,)),
                       pltpu.SemaphoreType.DMA],
        compiler_params=pltpu.CompilerParams(collective_id=SCATTER_ID),
    )(part)


def _adamw_math(w, g, m, v):
    m2 = ADAM_B1 * m + (1.0 - ADAM_B1) * g
    v2 = ADAM_B2 * v + (1.0 - ADAM_B2) * (g * g)
    m_hat = m2 / (1.0 - ADAM_B1 ** ADAM_STEP)
    v_hat = v2 / (1.0 - ADAM_B2 ** ADAM_STEP)
    delta = -ADAM_LR * (m_hat / (jnp.sqrt(v_hat) + ADAM_EPS) + ADAM_WD * w)
    return delta, m2, v2


def _adamw_big(l, w, m, v, recv, prev, *, summed=False, name, tr=128):
    L, R, C = w.shape
    tr = next(t for t in range(min(tr, R), 0, -16) if R % t == 0)

    def body(w_ref, m_ref, v_ref, r_ref, *rest):
        g_ref, d_ref, m2_ref, v2_ref = rest[-4:]
        if summed:
            g = r_ref[...]
        else:
            g = r_ref[0].astype(F32)
            for s in range(1, N_DEV):
                g = g + r_ref[s].astype(F32)
        g_ref[...] = g
        d_ref[...], m2_ref[...], v2_ref[...] = _adamw_math(w_ref[...], g, m_ref[...], v_ref[...])

    wspec = pl.BlockSpec((None, tr, C), lambda i: (l, i, 0))
    rspec = pl.BlockSpec((tr, C), lambda i: (i, 0)) if summed else pl.BlockSpec((N_DEV, tr, C), lambda i: (0, i, 0))
    prev = list(prev) if prev is not None else []
    return pl.pallas_call(
        body, name=name, grid=(R // tr,),
        in_specs=[wspec, wspec, wspec, rspec] + [_ANY] * len(prev),
        out_specs=[wspec] * 4,
        out_shape=[jax.ShapeDtypeStruct((L, R, C), F32)] * 4,
        input_output_aliases={4 + j: j for j in range(len(prev))},
        compiler_params=_cparams(("parallel",)),
    )(w, m, v, recv if summed else recv.reshape(N_DEV, R, C), *prev)


def _sum_slots_wide(recv, *, name, tc=512):
    _, R, C = recv.shape
    tc = _tile(C, tc)

    def body(r_ref, o_ref):
        g = r_ref[0].astype(F32)
        for s in range(1, N_DEV):
            g = g + r_ref[s].astype(F32)
        o_ref[...] = g

    return pl.pallas_call(
        body, name=name, grid=(C // tc,),
        in_specs=[pl.BlockSpec((N_DEV, R, tc), lambda j: (0, 0, j))],
        out_specs=pl.BlockSpec((R, tc), lambda j: (0, j)),
        out_shape=jax.ShapeDtypeStruct((R, C), F32),
        compiler_params=_cparams(("parallel",)),
    )(recv)


def _sum_slots(gathered, *, name):
    _, R, C = gathered.shape

    def body(r_ref, o_ref):
        g = r_ref[0]
        for s in range(1, N_DEV):
            g = g + r_ref[s]
        o_ref[...] = g

    return pl.pallas_call(body, name=name, out_shape=jax.ShapeDtypeStruct((R, C), F32))(gathered)


def _adamw_small(w, g, m, v, *, name):
    def body(w_ref, g_ref, m_ref, v_ref, d_ref, m2_ref, v2_ref):
        d_ref[...], m2_ref[...], v2_ref[...] = _adamw_math(w_ref[...], g_ref[...], m_ref[...], v_ref[...])

    return pl.pallas_call(body, name=name, out_shape=[jax.ShapeDtypeStruct(w.shape, F32)] * 3)(w, g, m, v)


def _pack(arrs):
    flat = []
    for a in arrs:
        a = a.reshape(-1).astype(F32)
        flat.append(jnp.pad(a, (0, (-a.shape[0]) % LANES)))
    out = jnp.concatenate(flat)
    out = jnp.pad(out, (0, (-out.shape[0]) % (8 * LANES)))
    return out.reshape(-1, LANES)


def _unpack(packed, shapes):
    flat = packed.reshape(-1)
    out, pos = [], 0
    for s in shapes:
        size = math.prod(s)
        out.append(flat[pos:pos + size].reshape(s))
        pos += size + (-size) % LANES
    return out


BIG = ("w_in", "conv_pw_w", "w_out", "w_up", "w_down")
SMALL_SHARDED = ("conv_dw_w", "gdn_conv_w", "ffn_conv_w")
SMALL_REPLICATED = ("mix_norm_g", "conv_dw_b", "conv_ln_g", "conv_ln_b", "conv_pw_b", "gdn_a_log", "gdn_dt_bias",
                    "gdn_norm_g", "ffn_norm_g", "ffn_conv_b")
WEIGHTS = ("mix_norm_g", "w_in", "conv_dw_w", "conv_dw_b", "conv_ln_g", "conv_ln_b", "conv_pw_w", "conv_pw_b",
           "gdn_conv_w", "gdn_a_log", "gdn_dt_bias", "gdn_norm_g", "w_out", "ffn_norm_g", "w_up", "ffn_conv_w",
           "ffn_conv_b", "w_down", "final_norm_g")


def _train_step(x, target, w, m, v):
    L = w["w_in"].shape[0]
    D = x.shape[-1]
    xi, yi, ci = _mesh_pos()
    me = _dev_index(xi, yi, ci)

    gathered = {}

    def launch(l, after=None):
        for n in BIG:
            src = (w[n][l].T if n == "w_in" else w[n][l]).astype(BF16)
            if after is not None:
                src = lax.optimization_barrier((src, after))[0]
            gathered[n, l] = _sc_gather(src, name=f"gather_{n}_{l}")

    launch(0)
    small_full = {}
    for n, g_ in zip(SMALL_SHARDED, _all_gather([w[n] for n in SMALL_SHARDED], name="all_gather_conv_taps")):
        small_full[n] = jnp.moveaxis(g_, 0, 2).reshape(L, g_.shape[2], N_DEV * g_.shape[3])
    Ws = []
    for l in range(L):
        W = {n: w[n][l] for n in SMALL_REPLICATED}
        W.update({n: small_full[n][l] for n in SMALL_SHARDED})
        Ws.append(W)

    def fetch(l, n, after):
        if n == "conv_pw_w" and l + 1 < L:
            launch(l + 1, after)
        g_ = lax.optimization_barrier((gathered[n, l], after))[0]
        if n == "w_up":
            return g_
        g_ = g_.reshape(g_.shape[0] * g_.shape[1], g_.shape[2])
        if n == "w_in":
            n_main = (g_.shape[0] // LANES) * LANES
            return g_, jnp.pad(g_[n_main:], ((0, LANES - (g_.shape[0] - n_main)), (0, 0)))
        return g_

    started = []
    res = {}
    SCATTERS_IN_FLIGHT = 2

    def consume(chain):
        n, l, recv = started.pop(0)
        if n == "w_in":
            recv = _sum_slots_wide(recv, name="sum_w_in_grad").T
        res[n] = _adamw_big(l, w[n], m[n], v[n], recv, res.get(n), summed=(n == "w_in"), name=f"adamw_{n}")
        if chain is None:
            return None
        tied = lax.optimization_barrier((chain, *res[n]))
        res[n] = list(tied[1:])
        return tied[0]

    def sink(l, n, g_, chain):
        g_, chain = lax.optimization_barrier((g_, chain))
        if len(started) >= SCATTERS_IN_FLIGHT:
            chain = consume(chain)
        if n == "w_in":
            g_main, g_ba = g_
            g_ = jnp.concatenate([g_main, g_ba[:w["w_in"].shape[2] * N_DEV - g_main.shape[0]]], axis=0)
            part = g_.reshape(N_DEV, -1, D)
        elif n == "w_up":
            part = g_
        else:
            part = g_.reshape(N_DEV, -1, g_.shape[1])
        started.append((n, l, _sc_scatter(part, name=f"scatter_{n}_{l}")))
        return chain

    loss, grad_x, G, d_final = _local_step(x, target, Ws, w["final_norm_g"], fetch, sink)

    small_names = [n for n in WEIGHTS if n not in BIG]
    partial = []
    for n in small_names:
        if n == "final_norm_g":
            partial.append(d_final)
        else:
            partial.append(jnp.stack([G[l][n].reshape(Ws[l][n].shape) for l in range(L)]))
    partial.append(loss.reshape(1))
    small_gathered = _sc_gather(_pack(partial), name="gather_small_grads")

    out = {k: {} for k in ("grad", "delta", "new_m", "new_v")}
    while started:
        consume(None)
    for n in BIG:
        for j, k in enumerate(("grad", "delta", "new_m", "new_v")):
            out[k][n] = res[n][j]

    summed = _unpack(_sum_slots(small_gathered, name="sum_small_grads"), [p_.shape for p_ in partial])
    full = dict(zip(small_names, summed))
    loss = summed[-1][0]
    for n in SMALL_SHARDED:
        width = w[n].shape[-1]
        full[n] = lax.dynamic_slice_in_dim(full[n], me * width, width, axis=2)
    loc_shapes = [w[n].shape for n in small_names]
    g_pack = _pack([full[n] for n in small_names])
    res = _adamw_small(_pack([w[n] for n in small_names]), g_pack, _pack([m[n] for n in small_names]),
                       _pack([v[n] for n in small_names]), name="adamw_small")
    for k, packed in zip(("grad", "delta", "new_m", "new_v"), (g_pack,) + tuple(res)):
        out[k].update(dict(zip(small_names, _unpack(packed, loc_shapes))))
    return loss, grad_x, out


def kernel(x, mix_norm_g, w_in, conv_dw_w, conv_dw_b, conv_ln_g, conv_ln_b, conv_pw_w, conv_pw_b, gdn_conv_w, gdn_a_log, gdn_dt_bias, gdn_norm_g, w_out, ffn_norm_g, w_up, ffn_conv_w, ffn_conv_b, w_down, final_norm_g, loss_target, m_mix_norm_g, m_w_in, m_conv_dw_w, m_conv_dw_b, m_conv_ln_g, m_conv_ln_b, m_conv_pw_w, m_conv_pw_b, m_gdn_conv_w, m_gdn_a_log, m_gdn_dt_bias, m_gdn_norm_g, m_w_out, m_ffn_norm_g, m_w_up, m_ffn_conv_w, m_ffn_conv_b, m_w_down, m_final_norm_g, v_mix_norm_g, v_w_in, v_conv_dw_w, v_conv_dw_b, v_conv_ln_g, v_conv_ln_b, v_conv_pw_w, v_conv_pw_b, v_gdn_conv_w, v_gdn_a_log, v_gdn_dt_bias, v_gdn_norm_g, v_w_out, v_ffn_norm_g, v_w_up, v_ffn_conv_w, v_ffn_conv_b, v_w_down, v_final_norm_g):
    w = dict(zip(WEIGHTS, (mix_norm_g, w_in, conv_dw_w, conv_dw_b, conv_ln_g, conv_ln_b, conv_pw_w, conv_pw_b, gdn_conv_w,
                           gdn_a_log, gdn_dt_bias, gdn_norm_g, w_out, ffn_norm_g, w_up, ffn_conv_w, ffn_conv_b, w_down,
                           final_norm_g)))
    m = dict(zip(WEIGHTS, (m_mix_norm_g, m_w_in, m_conv_dw_w, m_conv_dw_b, m_conv_ln_g, m_conv_ln_b, m_conv_pw_w,
                           m_conv_pw_b, m_gdn_conv_w, m_gdn_a_log, m_gdn_dt_bias, m_gdn_norm_g, m_w_out, m_ffn_norm_g,
                           m_w_up, m_ffn_conv_w, m_ffn_conv_b, m_w_down, m_final_norm_g)))
    v = dict(zip(WEIGHTS, (v_mix_norm_g, v_w_in, v_conv_dw_w, v_conv_dw_b, v_conv_ln_g, v_conv_ln_b, v_conv_pw_w,
                           v_conv_pw_b, v_gdn_conv_w, v_gdn_a_log, v_gdn_dt_bias, v_gdn_norm_g, v_w_out, v_ffn_norm_g,
                           v_w_up, v_ffn_conv_w, v_ffn_conv_b, v_w_down, v_final_norm_g)))
    loss, grad_x, out = _train_step(x, loss_target, w, m, v)
    return (loss, grad_x, *[out["grad"][n] for n in WEIGHTS], *[out["delta"][n] for n in WEIGHTS],
            *[out["new_m"][n] for n in WEIGHTS], *[out["new_v"][n] for n in WEIGHTS])
```

```python
import functools
import math

import jax
import jax.numpy as jnp
from jax import lax
from jax.experimental import pallas as pl
from jax.experimental.pallas import tpu as pltpu
from jax.experimental.pallas import tpu_sc as plsc

F32 = jnp.float32
BF16 = jnp.bfloat16
MESH = pl.DeviceIdType.MESH

EPS = 1e-6
LANES = 128
CHUNK = 128
NEAR_BLOCK = 32
CONV_K = 31
SHORT_CONV_K = 4
FFN_CONV_K = 3
N_DEV = 8
VMEM_LIMIT = 56 * 1024 * 1024

ADAM_LR = 0.001
ADAM_B1 = 0.9
ADAM_B2 = 0.999
ADAM_EPS = 1e-08
ADAM_WD = 0.01
ADAM_STEP = 10


def _cparams(sem):
    return pltpu.CompilerParams(dimension_semantics=sem, vmem_limit_bytes=VMEM_LIMIT)


def _sig(x):
    return 1.0 / (1.0 + jnp.exp(-x))


def _silu(x):
    return x * _sig(x)


def _dsilu(x):
    s = _sig(x)
    return s * (1.0 + x * (1.0 - s))


def _softplus(x):
    return jnp.maximum(x, 0.0) + jnp.log1p(jnp.exp(-jnp.abs(x)))


def _dot(a, b):
    return jnp.dot(a, b, preferred_element_type=F32)


def _dot_nt(a, b):
    return lax.dot_general(a, b, (((1,), (1,)), ((), ())), preferred_element_type=F32)


def _dot_tn(a, b):
    return lax.dot_general(a, b, (((0,), (0,)), ((), ())), preferred_element_type=F32)


def _bf(x):
    return x.astype(BF16)


_NN = (((1,), (0,)), ((), ()))
_TN = (((0,), (0,)), ((), ()))


def _split2(x):
    hi = _bf(x)
    return hi, _bf(x - hi.astype(F32))


def _dot_x3(a, b, dn=_NN):
    ah, al = _split2(a)
    bh, bl = _split2(b)
    f = lambda p, q: lax.dot_general(p, q, dn, preferred_element_type=F32)
    return f(ah, bh) + (f(al, bh) + f(ah, bl))


def _dot_mask(mask, x, dn=_NN):
    mb = _bf(mask)
    hi, lo = _split2(x)
    lo2 = _bf(x - hi.astype(F32) - lo.astype(F32))
    f = lambda q: lax.dot_general(mb, q, dn, preferred_element_type=F32)
    return f(hi) + (f(lo) + f(lo2))


def _shift_down(u, s):
    if s == 0:
        return u
    row = lax.broadcasted_iota(jnp.int32, u.shape, 0)
    return jnp.where(row >= s, pltpu.roll(u, s, 0), 0.0)


def _shift_up(u, s):
    if s == 0:
        return u
    n = u.shape[0]
    row = lax.broadcasted_iota(jnp.int32, u.shape, 0)
    return jnp.where(row < n - s, pltpu.roll(u, n - s, 0), 0.0)


def _conv_fwd(u, w_ref, K):
    acc = None
    for k in range(K):
        term = w_ref[k:k + 1, :] * _shift_down(u, K - 1 - k)
        acc = term if acc is None else acc + term
    return acc


def _conv_bwd_in(do, w_ref, K):
    acc = None
    for k in range(K):
        term = w_ref[k:k + 1, :] * _shift_up(do, K - 1 - k)
        acc = term if acc is None else acc + term
    return acc


def _conv_bwd_w(do, u, dw_ref, K, first):
    for k in range(K):
        row = jnp.sum(do * _shift_down(u, K - 1 - k), axis=0, keepdims=True)
        _acc_row(dw_ref, k, row, first)


def _acc_row(ref, k, row, first):
    @pl.when(first)
    def _():
        ref[k:k + 1, :] = row

    @pl.when(jnp.logical_not(first))
    def _():
        ref[k:k + 1, :] += row


def _logical(arr):
    if arr.ndim == 2:
        return arr.shape
    return (arr.shape[1], arr.shape[0] * arr.shape[2])


def _tile(dim, pref, *col_widths):
    if dim % LANES:
        assert not col_widths
        return dim
    t = (min(pref, dim) // LANES) * LANES
    while t > LANES and (dim % t or any(c % t for c in col_widths)):
        t -= LANES
    assert dim % t == 0 and all(c % t == 0 for c in col_widths), (dim, pref, col_widths)
    return t


def _spec(shape, rt, ct, rfn, cfn):
    if len(shape) == 2:
        return pl.BlockSpec((rt, ct), lambda i, j, k: (rfn(i, j, k), cfn(i, j, k)))
    per = shape[2] // ct
    return pl.BlockSpec((None, rt, ct),
                        lambda i, j, k: (cfn(i, j, k) // per, rfn(i, j, k), cfn(i, j, k) % per))


def _mm(a, b, *, name, ta=False, tb=False, out_dtype=F32, out_blocks=None, bias=None, res=None, b_rows=None,
        tm=1024, tn=1024, tk=2048):
    ra, ca = _logical(a)
    rb, cb = _logical(b)
    if b_rows is not None:
        assert b.ndim == 2 and b_rows <= rb
        rb = b_rows
    M, K = (ca, ra) if ta else (ra, ca)
    N, K2 = (rb, cb) if tb else (cb, rb)
    assert K == K2, (a.shape, b.shape, ta, tb)
    out_shape = (M, N) if out_blocks is None else (out_blocks, M, N // out_blocks)
    cw = lambda arr: [arr.shape[2]] if arr.ndim == 3 else []
    m_c = cw(a) if ta else []
    k_c = (cw(a) if not ta else []) + (cw(b) if tb else [])
    n_c = (cw(b) if not tb else []) + ([out_shape[2]] if out_blocks else []) + (cw(res) if res is not None else [])
    tm, tn, tk = _tile(M, tm, *m_c), _tile(N, tn, *n_c), _tile(K, tk, *k_c)
    nk = K // tk
    im, jn, kk = (lambda i, j, k: i), (lambda i, j, k: j), (lambda i, j, k: k)
    in_specs = [
        _spec(a.shape, tk, tm, kk, im) if ta else _spec(a.shape, tm, tk, im, kk),
        _spec(b.shape, tn, tk, jn, kk) if tb else _spec(b.shape, tk, tn, kk, jn),
    ]
    args = [a, b]
    if bias is not None:
        in_specs.append(pl.BlockSpec((1, tn), lambda i, j, k: (0, j)))
        args.append(bias.reshape(1, N).astype(F32))
    if res is not None:
        in_specs.append(_spec(res.shape, tm, tn, im, jn))
        args.append(res)
    dn = (((0 if ta else 1,), (1 if tb else 0,)), ((), ()))

    def body(*refs):
        a_ref, b_ref = refs[0], refs[1]
        pos = 2
        bias_ref = res_ref = None
        if bias is not None:
            bias_ref = refs[pos]
            pos += 1
        if res is not None:
            res_ref = refs[pos]
            pos += 1
        o_ref = refs[pos]
        k = pl.program_id(2)
        part = lax.dot_general(_bf(a_ref[...]), _bf(b_ref[...]), dn, preferred_element_type=F32)

        def finish(r):
            if bias_ref is not None:
                r = r + bias_ref[...]
            if res_ref is not None:
                r = r + res_ref[...].astype(F32)
            o_ref[...] = r.astype(out_dtype)

        if nk == 1:
            finish(part)
            return
        acc_ref = refs[pos + 1]

        @pl.when(k == 0)
        def _():
            acc_ref[...] = part

        @pl.when((k > 0) & (k < nk - 1))
        def _():
            acc_ref[...] += part

        @pl.when(k == nk - 1)
        def _():
            finish(acc_ref[...] + part)

    return pl.pallas_call(
        body, name=name,
        grid=(M // tm, N // tn, nk),
        in_specs=in_specs,
        out_specs=_spec(out_shape, tm, tn, im, jn),
        out_shape=jax.ShapeDtypeStruct(out_shape, out_dtype),
        scratch_shapes=[pltpu.VMEM((tm, tn), F32)] if nk > 1 else [],
        compiler_params=_cparams(("parallel", "parallel", "arbitrary")),
    )(*args)


def _rms_fwd(x, g, *, name, tr=512):
    T, D = x.shape
    tr = min(tr, T)

    def body(x_ref, g_ref, h_ref):
        xv = x_ref[...]
        r = lax.rsqrt(jnp.mean(xv * xv, axis=-1, keepdims=True) + EPS)
        h_ref[...] = (xv * r * g_ref[...]).astype(BF16)

    return pl.pallas_call(
        body, name=name, grid=(T // tr,),
        in_specs=[pl.BlockSpec((tr, D), lambda i: (i, 0)), pl.BlockSpec((1, D), lambda i: (0, 0))],
        out_specs=pl.BlockSpec((tr, D), lambda i: (i, 0)),
        out_shape=jax.ShapeDtypeStruct((T, D), BF16),
        compiler_params=_cparams(("parallel",)),
    )(x, g.reshape(1, D))


def _rms_bwd(x, g, dh, dres, *, name, tr=512):
    T, D = x.shape
    tr = min(tr, T)

    def body(x_ref, g_ref, dh_ref, dres_ref, dx_ref, dxb_ref, dg_ref):
        i = pl.program_id(0)
        xv = x_ref[...]
        dy = dh_ref[...].astype(F32)
        r = lax.rsqrt(jnp.mean(xv * xv, axis=-1, keepdims=True) + EPS)
        dyg = dy * g_ref[...]
        dot = jnp.mean(dyg * xv, axis=-1, keepdims=True)
        dx = dres_ref[...] + r * dyg - xv * (r * r * r) * dot
        dx_ref[...] = dx
        dxb_ref[...] = dx.astype(BF16)
        part = jnp.sum(dy * xv * r, axis=0, keepdims=True)
        _acc_row(dg_ref, 0, part, i == 0)

    row = pl.BlockSpec((tr, D), lambda i: (i, 0))
    vec = pl.BlockSpec((1, D), lambda i: (0, 0))
    return pl.pallas_call(
        body, name=name, grid=(T // tr,),
        in_specs=[row, vec, row, row],
        out_specs=[row, row, vec],
        out_shape=[jax.ShapeDtypeStruct((T, D), F32), jax.ShapeDtypeStruct((T, D), BF16),
                   jax.ShapeDtypeStruct((1, D), F32)],
        compiler_params=_cparams(("arbitrary",)),
    )(x, g.reshape(1, D), dh, dres)


def _loss_head(x, g, target, *, name, tr=512):
    T, D = x.shape
    tr = min(tr, T)

    def body(x_ref, g_ref, t_ref, loss_ref, dx_ref, dxb_ref, dg_ref):
        i = pl.program_id(0)
        xv = x_ref[...]
        gv = g_ref[...]
        r = lax.rsqrt(jnp.mean(xv * xv, axis=-1, keepdims=True) + EPS)
        y = xv * r * gv
        err = y - t_ref[...]
        lpart = 0.5 * jnp.sum(jnp.mean(err * err, axis=-1, keepdims=True), axis=0, keepdims=True)
        dy = err * (1.0 / D)
        dyg = dy * gv
        dot = jnp.mean(dyg * xv, axis=-1, keepdims=True)
        dx = r * dyg - xv * (r * r * r) * dot
        dx_ref[...] = dx
        dxb_ref[...] = dx.astype(BF16)
        _acc_row(dg_ref, 0, jnp.sum(dy * xv * r, axis=0, keepdims=True), i == 0)
        _acc_row(loss_ref, 0, jnp.broadcast_to(lpart, (1, LANES)), i == 0)

    row = pl.BlockSpec((tr, D), lambda i: (i, 0))
    return pl.pallas_call(
        body, name=name, grid=(T // tr,),
        in_specs=[row, pl.BlockSpec((1, D), lambda i: (0, 0)), row],
        out_specs=[pl.BlockSpec((1, LANES), lambda i: (0, 0)), row, row, pl.BlockSpec((1, D), lambda i: (0, 0))],
        out_shape=[jax.ShapeDtypeStruct((1, LANES), F32), jax.ShapeDtypeStruct((T, D), F32),
                   jax.ShapeDtypeStruct((T, D), BF16), jax.ShapeDtypeStruct((1, D), F32)],
        compiler_params=_cparams(("arbitrary",)),
    )(x, g.reshape(1, D), target)


HALO = 32
SUBLANES = 8


def _conv_dw_blocks(do, u, dw_ref, K, first, u_s, do_p):
    S, C = u.shape
    zeros = jnp.zeros((HALO, C), F32)
    for r in range(SUBLANES):
        u_s[r, 0:HALO, :] = zeros
        u_s[r, HALO:HALO + S, :] = _shift_down(u, r)
    do_p[0:HALO, :] = zeros
    do_p[HALO:HALO + S, :] = do
    do_p[HALO + S:2 * HALO + S, :] = zeros
    n_a = (K - 1) // SUBLANES + 1

    def block(i, accs):
        i0 = pl.multiple_of(i * SUBLANES, SUBLANES)
        us = [u_s[r, pl.ds(i0, SUBLANES), :] for r in range(SUBLANES)]
        ds = [do_p[pl.ds(i0 + SUBLANES * a, SUBLANES), :] for a in range(n_a)]
        out = list(accs)
        for a in range(n_a):
            for r in range(SUBLANES):
                s = SUBLANES * a + r
                if s < K:
                    out[K - 1 - s] = out[K - 1 - s] + ds[a] * us[r]
        return tuple(out)

    accs = lax.fori_loop(HALO // SUBLANES, (S + HALO) // SUBLANES, block, (jnp.zeros((SUBLANES, C), F32),) * K)
    for k in range(K):
        _acc_row(dw_ref, k, jnp.sum(accs[k], axis=0, keepdims=True), first)


def _conf_norm(u1, lg_ref, lb_ref):
    mu = jnp.mean(u1, axis=-1, keepdims=True)
    xc = u1 - mu
    r = lax.rsqrt(jnp.mean(xc * xc, axis=-1, keepdims=True) + EPS)
    n = xc * r
    return r, n, n * lg_ref[...] + lb_ref[...]


def _conf_fwd(p, dw_w, dw_b, ln_g, ln_b, *, Bl, S, CC, name):
    G = CC // LANES

    def body(av_ref, ag_ref, w_ref, b_ref, lg_ref, lb_ref, o_ref, u1_ref):
        u0 = av_ref[...] * _sig(ag_ref[...])
        u1 = _conv_fwd(u0, w_ref, CONV_K) + b_ref[...]
        u1_ref[...] = u1
        _, _, u2 = _conf_norm(u1, lg_ref, lb_ref)
        o_ref[...] = _silu(u2).astype(BF16)

    vec = pl.BlockSpec((1, LANES), lambda b, j: (0, j))
    seq = pl.BlockSpec((S, LANES), lambda b, j: (b, j))
    return pl.pallas_call(
        body, name=name, grid=(Bl, G),
        in_specs=[seq, pl.BlockSpec((S, LANES), lambda b, j: (b, G + j)),
                  pl.BlockSpec((CONV_K, LANES), lambda b, j: (0, j)), vec, vec, vec],
        out_specs=[seq, seq],
        out_shape=[jax.ShapeDtypeStruct((Bl * S, CC), BF16), jax.ShapeDtypeStruct((Bl * S, CC), F32)],
        compiler_params=_cparams(("parallel", "parallel")),
    )(p, p, dw_w, dw_b.reshape(1, CC), ln_g.reshape(1, CC), ln_b.reshape(1, CC))


def _conf_bwd(p, u1, dw_w, ln_g, ln_b, du3, *, Bl, S, CC, name):
    G = CC // LANES

    def body(av_ref, ag_ref, u1_ref, w_ref, lg_ref, lb_ref, du3_ref,
             dav_ref, dag_ref, dw_ref, db_ref, dlg_ref, dlb_ref, u_s, do_p):
        first = pl.program_id(1) == 0
        av = av_ref[...]
        sg = _sig(ag_ref[...])
        r, n, u2 = _conf_norm(u1_ref[...], lg_ref, lb_ref)
        du2 = du3_ref[...] * _dsilu(u2)
        _acc_row(dlg_ref, 0, jnp.sum(du2 * n, axis=0, keepdims=True), first)
        _acc_row(dlb_ref, 0, jnp.sum(du2, axis=0, keepdims=True), first)
        dn = du2 * lg_ref[...]
        du1 = r * (dn - jnp.mean(dn, axis=-1, keepdims=True) - n * jnp.mean(dn * n, axis=-1, keepdims=True))
        _acc_row(db_ref, 0, jnp.sum(du1, axis=0, keepdims=True), first)
        _conv_dw_blocks(du1, av * sg, dw_ref, CONV_K, first, u_s, do_p)
        du0 = _conv_bwd_in(du1, w_ref, CONV_K)
        dav_ref[...] = (du0 * sg).astype(BF16)
        dag_ref[...] = (du0 * av * sg * (1.0 - sg)).astype(BF16)

    vec = pl.BlockSpec((1, LANES), lambda j, b: (0, j))
    seq = pl.BlockSpec((S, LANES), lambda j, b: (b, j))
    return pl.pallas_call(
        body, name=name, grid=(G, Bl),
        in_specs=[seq, pl.BlockSpec((S, LANES), lambda j, b: (b, G + j)), seq,
                  pl.BlockSpec((CONV_K, LANES), lambda j, b: (0, j)), vec, vec, seq],
        out_specs=[seq, seq, pl.BlockSpec((CONV_K, LANES), lambda j, b: (0, j)), vec, vec, vec],
        out_shape=[jax.ShapeDtypeStruct((Bl * S, CC), BF16), jax.ShapeDtypeStruct((Bl * S, CC), BF16),
                   jax.ShapeDtypeStruct((CONV_K, CC), F32), jax.ShapeDtypeStruct((1, CC), F32),
                   jax.ShapeDtypeStruct((1, CC), F32), jax.ShapeDtypeStruct((1, CC), F32)],
        scratch_shapes=[pltpu.VMEM((SUBLANES, S + HALO, LANES), F32), pltpu.VMEM((S + 2 * HALO, LANES), F32)],
        compiler_params=_cparams(("parallel", "arbitrary")),
    )(p, p, u1, dw_w, ln_g.reshape(1, CC), ln_b.reshape(1, CC), du3)


def _gdn_pre_fwd(p, conv_w, *, Bl, S, CC, KW, VW, name):
    NQK = 2 * KW // LANES
    NB = NQK + VW // LANES
    off = 2 * CC // LANES

    def body(x_ref, w_ref, o_ref):
        j = pl.program_id(1)
        s = _silu(_conv_fwd(x_ref[...], w_ref, SHORT_CONV_K))
        r = lax.rsqrt(jnp.sum(s * s, axis=-1, keepdims=True) + EPS)
        o_ref[...] = jnp.where(j < NQK, s * r, s)

    return pl.pallas_call(
        body, name=name, grid=(Bl, NB),
        in_specs=[pl.BlockSpec((S, LANES), lambda b, j: (b, off + j)),
                  pl.BlockSpec((SHORT_CONV_K, LANES), lambda b, j: (0, j))],
        out_specs=pl.BlockSpec((S, LANES), lambda b, j: (b, j)),
        out_shape=jax.ShapeDtypeStruct((Bl * S, NB * LANES), F32),
        compiler_params=_cparams(("parallel", "parallel")),
    )(p, conv_w)


def _gdn_pre_bwd(p, conv_w, dq, dk, dv, *, Bl, S, CC, KW, VW, name):
    HQ = KW // LANES
    H = VW // LANES
    NQK = 2 * HQ
    NB = NQK + H
    off = 2 * CC // LANES

    def body(x_ref, w_ref, dq_ref, dk_ref, dv_ref, dx_ref, dw_ref):
        j = pl.program_id(0)
        first = pl.program_id(1) == 0
        xv = x_ref[...]
        c = _conv_fwd(xv, w_ref, SHORT_CONV_K)
        s = _silu(c)
        r = lax.rsqrt(jnp.sum(s * s, axis=-1, keepdims=True) + EPS)
        dy = jnp.where(j < HQ, dq_ref[...], jnp.where(j < NQK, dk_ref[...], dv_ref[...]))
        ds_norm = r * dy - s * (r * r * r) * jnp.sum(s * dy, axis=-1, keepdims=True)
        ds = jnp.where(j < NQK, ds_norm, dy)
        dc = ds * _dsilu(c)
        _conv_bwd_w(dc, xv, dw_ref, SHORT_CONV_K, first)
        dx_ref[...] = _conv_bwd_in(dc, w_ref, SHORT_CONV_K).astype(BF16)

    return pl.pallas_call(
        body, name=name, grid=(NB, Bl),
        in_specs=[pl.BlockSpec((S, LANES), lambda j, b: (b, off + j)),
                  pl.BlockSpec((SHORT_CONV_K, LANES), lambda j, b: (0, j)),
                  pl.BlockSpec((S, LANES), lambda j, b: (b, jnp.minimum(j, HQ - 1))),
                  pl.BlockSpec((S, LANES), lambda j, b: (b, jnp.clip(j - HQ, 0, HQ - 1))),
                  pl.BlockSpec((S, LANES), lambda j, b: (b, jnp.clip(j - NQK, 0, H - 1)))],
        out_specs=[pl.BlockSpec((S, LANES), lambda j, b: (b, j)),
                   pl.BlockSpec((SHORT_CONV_K, LANES), lambda j, b: (0, j))],
        out_shape=[jax.ShapeDtypeStruct((Bl * S, NB * LANES), BF16),
                   jax.ShapeDtypeStruct((SHORT_CONV_K, NB * LANES), F32)],
        compiler_params=_cparams(("parallel", "arbitrary")),
    )(p, conv_w, dq, dk, dv)


def _split3(x):
    hi, lo = _split2(x)
    return hi, lo, _bf(x - hi.astype(F32) - lo.astype(F32))


def _lane_replicate(parts, h):
    row = lax.broadcasted_iota(jnp.int32, (LANES, LANES), 0)
    E = jnp.where(row == h, 1.0, 0.0).astype(BF16)
    return _dot(parts[0], E) + (_dot(parts[1], E) + _dot(parts[2], E))


def _gdn_gate_fwd(pba, a_log, dt_bias, *, Bl, S, H, name):
    def body(alog_ref, dtb_ref, x_ref, g_ref, beta_ref):
        parts = _split3(x_ref[...])
        for h in range(H):
            b_raw = _lane_replicate(parts, h)
            a_raw = _lane_replicate(parts, H + h)
            beta_ref[h] = _sig(b_raw)
            ea = jnp.exp(jnp.zeros((1, LANES), F32) + alog_ref[h])
            g_ref[h] = -ea * _softplus(a_raw + dtb_ref[h])

    smem = pl.BlockSpec(memory_space=pltpu.SMEM)
    rep = pl.BlockSpec((H, S, LANES), lambda b: (0, b, 0))
    return pl.pallas_call(
        body, name=name, grid=(Bl,),
        in_specs=[smem, smem, pl.BlockSpec((S, LANES), lambda b: (b, 0))],
        out_specs=[rep, rep],
        out_shape=[jax.ShapeDtypeStruct((H, Bl * S, LANES), F32)] * 2,
        compiler_params=_cparams(("parallel",)),
    )(a_log, dt_bias, pba)


def _gdn_gate_bwd(pba, a_log, dt_bias, dg, dbeta, *, Bl, S, H, name):
    HP = 8 * ((H + 7) // 8)

    def body(alog_ref, dtb_ref, x_ref, dg_ref, dbeta_ref, dx_ref, dalog_ref, ddtb_ref):
        first = pl.program_id(0) == 0
        parts = _split3(x_ref[...])
        lane = lax.broadcasted_iota(jnp.int32, (S, LANES), 1)
        acc = jnp.zeros((S, LANES), F32)

        @pl.when(first)
        def _():
            dalog_ref[...] = jnp.zeros_like(dalog_ref)
            ddtb_ref[...] = jnp.zeros_like(ddtb_ref)

        for h in range(H):
            b_raw = _lane_replicate(parts, h)
            a_raw = _lane_replicate(parts, H + h)
            beta = _sig(b_raw)
            db_raw = dbeta_ref[h] * beta * (1.0 - beta)
            z = a_raw + dtb_ref[h]
            ea = jnp.exp(jnp.zeros((1, LANES), F32) + alog_ref[h])
            dgv = dg_ref[h]
            da_raw = dgv * (-ea) * _sig(z)
            g = -ea * _softplus(z)
            dalog_ref[h:h + 1, :] += jnp.sum(dgv * g, axis=0, keepdims=True)
            ddtb_ref[h:h + 1, :] += jnp.sum(da_raw, axis=0, keepdims=True)
            acc = acc + jnp.where(lane == h, db_raw, 0.0) + jnp.where(lane == H + h, da_raw, 0.0)
        dx_ref[...] = acc.astype(BF16)

    smem = pl.BlockSpec(memory_space=pltpu.SMEM)
    rep = pl.BlockSpec((H, S, LANES), lambda b: (0, b, 0))
    small = pl.BlockSpec((HP, LANES), lambda b: (0, 0))
    return pl.pallas_call(
        body, name=name, grid=(Bl,),
        in_specs=[smem, smem, pl.BlockSpec((S, LANES), lambda b: (b, 0)), rep, rep],
        out_specs=[pl.BlockSpec((S, LANES), lambda b: (b, 0)), small, small],
        out_shape=[jax.ShapeDtypeStruct((Bl * S, LANES), BF16),
                   jax.ShapeDtypeStruct((HP, LANES), F32), jax.ShapeDtypeStruct((HP, LANES), F32)],
        compiler_params=_cparams(("arbitrary",)),
    )(a_log, dt_bias, pba, dg, dbeta)


def _tri_masks():
    ri = lax.broadcasted_iota(jnp.int32, (CHUNK, CHUNK), 0)
    ci = lax.broadcasted_iota(jnp.int32, (CHUNK, CHUNK), 1)
    return ri >= ci, ri > ci, ri == CHUNK - 1


def _tri_inv(L):
    ri = lax.broadcasted_iota(jnp.int32, (CHUNK, CHUNK), 0)
    ci = lax.broadcasted_iota(jnp.int32, (CHUNK, CHUNK), 1)
    T = jnp.where(ri == ci, 1.0, 0.0) - jnp.where((ri >> 1) == (ci >> 1), L, 0.0)
    for lv in range(2, int(math.log2(CHUNK)) + 1):
        O = jnp.where(((ri >> lv) == (ci >> lv)) & ((ri >> (lv - 1)) != (ci >> (lv - 1))), L, 0.0)
        if (1 << lv) <= NEAR_BLOCK:
            T = T - _dot_x3(T, _dot_x3(O, T))
        else:
            Tb = _bf(T)
            T = T - _dot(Tb, _bf(_dot(_bf(O), Tb)))
    return T


def _chunk_local(q, k, v, beta, g):
    ge, gt, last = _tri_masks()
    gam = _dot_mask(ge, g)
    D = jnp.where(ge, jnp.exp(jnp.where(ge, gam - gam.T, 0.0)), 0.0)
    kb = k * beta
    vb = v * beta
    M = _dot_nt(_bf(kb), _bf(k))
    L = jnp.where(gt, M * D, 0.0)
    eg = jnp.exp(gam)
    kbg = kb * eg
    P = _dot_nt(_bf(q), _bf(k))
    QK = jnp.where(ge, P * D, 0.0)
    gl = jnp.sum(jnp.where(last, gam, 0.0), axis=0, keepdims=True)
    el = jnp.exp(gl - gam)
    return dict(ge=ge, gt=gt, last=last, gam=gam, D=D, kb=kb, vb=vb, L=L, eg=eg, kbg=kbg, QK=QK, gl=gl,
                el=el, kd=k * el, qg=q * eg)


def _rowsum(x):
    return jnp.sum(x, axis=-1, keepdims=True)


def _chunk_bwd(q, k, v, beta, g, S, T, u, w, do, dS2):
    c = _chunk_local(q, k, v, beta, g)
    ge, gt, last = c["ge"], c["gt"], c["last"]
    Sb = _bf(S)
    vn = u - _dot(w, Sb)
    dob, vnb, dS2b = _bf(do), _bf(vn), _bf(dS2)
    e_last = jnp.exp(c["gl"])
    dqg = _dot_nt(dob, Sb)
    dS = _dot_tn(_bf(c["qg"]), dob)
    dQK = jnp.where(ge, _dot_nt(dob, vnb), 0.0)
    dvn = _dot_tn(_bf(c["QK"]), dob)
    dS = dS + dS2 * e_last
    de_last = jnp.sum(jnp.sum(dS2 * S, axis=0, keepdims=True), axis=1, keepdims=True)
    dkd = _dot_nt(vnb, dS2b)
    dvn = dvn + _dot(_bf(c["kd"]), dS2b)
    dvnb = _bf(dvn)
    dw = -_dot_nt(dvnb, Sb)
    dS = dS - _dot_tn(w, dvnb)
    dsol = _dot_x3(T, jnp.concatenate([dvn, dw], axis=1), _TN)
    dvb, dkbg = dsol[:, :LANES], dsol[:, LANES:]
    dA = -(_dot_nt(_bf(dvb), _bf(u)) + _dot_nt(_bf(dkbg), w))
    dL = jnp.where(gt, dA, 0.0)
    dM = dL * c["D"]
    dP = dQK * c["D"]
    E = dL * c["L"] + dQK * c["QK"]
    kbf = _bf(k)
    dkb = _dot(_bf(dM), kbf) + dkbg * c["eg"]
    dk = _dot_tn(_bf(dM), _bf(c["kb"])) + _dot_tn(_bf(dP), _bf(q)) + dkd * c["el"] + dkb * beta
    dq = _dot(_bf(dP), kbf) + dqg * c["eg"]
    s_kd = _rowsum(dkd * c["kd"])
    dgam = (_rowsum(E) - _rowsum(E.T) + _rowsum(dqg * c["qg"]) - s_kd + _rowsum(dkbg * c["kbg"]))
    dgl = jnp.sum(s_kd, axis=0, keepdims=True) + de_last * e_last
    dgam_rep = jnp.broadcast_to(dgam, (CHUNK, LANES)) + jnp.where(last, jnp.broadcast_to(dgl, (CHUNK, LANES)), 0.0)
    dg_rep = _dot_mask(ge, dgam_rep, _TN)
    dbeta = _rowsum(dkb * k) + _rowsum(dvb * v)
    dv = dvb * beta
    return dq, dk, dv, jnp.broadcast_to(dbeta, (CHUNK, LANES)), dg_rep, dS


def _gdn_core_fwd(qkv, g, beta, *, Bl, S, KW, VW, name):
    HQ = KW // LANES
    H = VW // LANES
    NC = S // CHUNK
    scale = float(LANES) ** -0.5

    PAIR = 2 if NC % 2 == 0 else 1

    def body(q_ref, k_ref, v_ref, g_ref, beta_ref, o_ref, st_ref, t_ref, u_s, w_s,
             qk_s, qg_s, kd_s, el_s):
        def local(n2, carry):
            for half in range(PAIR):
                n = n2 * PAIR + half
                rows = pl.ds(pl.multiple_of(n * CHUNK, CHUNK), CHUNK)
                q = q_ref[rows, :] * scale
                k = k_ref[rows, :]
                for e in range(2):
                    c = _chunk_local(q, k, v_ref[rows, e * LANES:(e + 1) * LANES], beta_ref[e, rows, :],
                                     g_ref[e, rows, :])
                    T = _tri_inv(c["L"])
                    t_ref[e, rows, :] = T
                    uw = _dot_x3(T, jnp.concatenate([c["vb"], c["kbg"]], axis=1))
                    u_s[e, rows, :] = uw[:, :LANES]
                    w_s[e, rows, :] = _bf(uw[:, LANES:])
                    qk_s[e, rows, :] = _bf(c["QK"])
                    qg_s[e, rows, :] = _bf(c["qg"])
                    kd_s[e, rows, :] = _bf(c["kd"])
                    el_s[e, pl.ds(pl.multiple_of(n * 8, 8), 8), :] = jnp.broadcast_to(jnp.exp(c["gl"]), (8, LANES))
            return carry

        lax.fori_loop(0, NC // PAIR, local, 0)

        def scan(n, states):
            rows = pl.ds(pl.multiple_of(n * CHUNK, CHUNK), CHUNK)
            out = []
            for e in range(2):
                S_in = states[e]
                st_ref[e, n] = S_in
                Sb = _bf(S_in)
                vn = u_s[e, rows, :] - _dot(w_s[e, rows, :], Sb)
                vnb = _bf(vn)
                o_ref[rows, e * LANES:(e + 1) * LANES] = _dot(qg_s[e, rows, :], Sb) + _dot(qk_s[e, rows, :], vnb)
                e_last = el_s[e, pl.ds(pl.multiple_of(n * 8, 8), 1), :]
                out.append(S_in * e_last + _dot_tn(kd_s[e, rows, :], vnb))
            return tuple(out)

        z = jnp.zeros((LANES, LANES), F32)
        lax.fori_loop(0, NC, scan, (z, z))

    rep = pl.BlockSpec((2, S, LANES), lambda b, h: (h, b, 0))
    return pl.pallas_call(
        body, name=name, grid=(Bl, HQ),
        in_specs=[pl.BlockSpec((S, LANES), lambda b, h: (b, h)),
                  pl.BlockSpec((S, LANES), lambda b, h: (b, HQ + h)),
                  pl.BlockSpec((S, 2 * LANES), lambda b, h: (b, HQ + h)), rep, rep],
        out_specs=[pl.BlockSpec((S, 2 * LANES), lambda b, h: (b, h)),
                   pl.BlockSpec((None, 2, NC, LANES, LANES), lambda b, h: (b, h, 0, 0, 0)), rep, rep, rep],
        out_shape=[jax.ShapeDtypeStruct((Bl * S, VW), F32),
                   jax.ShapeDtypeStruct((Bl, H, NC, LANES, LANES), F32),
                   jax.ShapeDtypeStruct((H, Bl * S, LANES), F32),
                   jax.ShapeDtypeStruct((H, Bl * S, LANES), F32),
                   jax.ShapeDtypeStruct((H, Bl * S, LANES), BF16)],
        scratch_shapes=[pltpu.VMEM((2, S, LANES), BF16)] * 3 + [pltpu.VMEM((2, NC * 8, LANES), F32)],
        compiler_params=_cparams(("parallel", "parallel")),
    )(qkv, qkv, qkv, g, beta)


def _gdn_core_bwd(qkv, g, beta, states, tinv, u, w, do, *, Bl, S, KW, VW, name):
    HQ = KW // LANES
    H = VW // LANES
    NC = S // CHUNK
    scale = float(LANES) ** -0.5

    def body(q_ref, k_ref, v_ref, g_ref, beta_ref, st_ref, t_ref, u_ref, w_ref, do_ref,
             dq_ref, dk_ref, dv_ref, dg_ref, dbeta_ref):
        def step(i, dstates):
            n = NC - 1 - i
            rows = pl.ds(pl.multiple_of(n * CHUNK, CHUNK), CHUNK)
            q = q_ref[rows, :] * scale
            k = k_ref[rows, :]
            out = []
            dq_sum = dk_sum = None
            for e in range(2):
                cols = slice(e * LANES, (e + 1) * LANES)
                dq, dk, dv, dbeta, dg, dS = _chunk_bwd(q, k, v_ref[rows, cols], beta_ref[e, rows, :],
                                                       g_ref[e, rows, :], st_ref[e, n], t_ref[e, rows, :],
                                                       u_ref[e, rows, :], w_ref[e, rows, :],
                                                       do_ref[rows, cols], dstates[e])
                dv_ref[rows, cols] = dv
                dg_ref[e, rows, :] = dg
                dbeta_ref[e, rows, :] = dbeta
                dq_sum = dq if dq_sum is None else dq_sum + dq
                dk_sum = dk if dk_sum is None else dk_sum + dk
                out.append(dS)
            dq_ref[rows, :] = dq_sum * scale
            dk_ref[rows, :] = dk_sum
            return tuple(out)

        z = jnp.zeros((LANES, LANES), F32)
        lax.fori_loop(0, NC, step, (z, z))

    rep = pl.BlockSpec((2, S, LANES), lambda b, h: (h, b, 0))
    seq = pl.BlockSpec((S, LANES), lambda b, h: (b, h))
    seq2 = pl.BlockSpec((S, 2 * LANES), lambda b, h: (b, h))
    return pl.pallas_call(
        body, name=name, grid=(Bl, HQ),
        in_specs=[seq, pl.BlockSpec((S, LANES), lambda b, h: (b, HQ + h)),
                  pl.BlockSpec((S, 2 * LANES), lambda b, h: (b, HQ + h)), rep, rep,
                  pl.BlockSpec((None, 2, NC, LANES, LANES), lambda b, h: (b, h, 0, 0, 0)), rep, rep, rep, seq2],
        out_specs=[seq, seq, seq2, rep, rep],
        out_shape=[jax.ShapeDtypeStruct((Bl * S, KW), F32), jax.ShapeDtypeStruct((Bl * S, KW), F32),
                   jax.ShapeDtypeStruct((Bl * S, VW), F32),
                   jax.ShapeDtypeStruct((H, Bl * S, LANES), F32), jax.ShapeDtypeStruct((H, Bl * S, LANES), F32)],
        compiler_params=_cparams(("parallel", "parallel")),
    )(qkv, qkv, qkv, g, beta, states, tinv, u, w, do)


def _gdn_out_fwd(o, p, norm_g, out_a, *, CC, VW, name, tr=256):
    T = o.shape[0]
    tr = min(tr, T)
    H = VW // LANES
    zoff = p.shape[1] // VW - 1

    def body(o_ref, z_ref, ng_ref, a_ref, mix_ref):
        mix_ref[:, :CC] = a_ref[...]
        for h in range(H):
            cols = slice(h * LANES, (h + 1) * LANES)
            ov = o_ref[:, cols]
            r = lax.rsqrt(jnp.mean(ov * ov, axis=-1, keepdims=True) + EPS)
            mix_ref[:, CC + h * LANES:CC + (h + 1) * LANES] = (ov * r * ng_ref[...] * _silu(z_ref[:, cols])).astype(BF16)

    return pl.pallas_call(
        body, name=name, grid=(T // tr,),
        in_specs=[pl.BlockSpec((tr, VW), lambda i: (i, 0)), pl.BlockSpec((tr, VW), lambda i: (i, zoff)),
                  pl.BlockSpec((1, LANES), lambda i: (0, 0)), pl.BlockSpec((tr, CC), lambda i: (i, 0))],
        out_specs=pl.BlockSpec((tr, CC + VW), lambda i: (i, 0)),
        out_shape=jax.ShapeDtypeStruct((T, CC + VW), BF16),
        compiler_params=_cparams(("parallel",)),
    )(o, p, norm_g.reshape(1, LANES), out_a)


def _gdn_out_bwd(o, p, norm_g, dmix, *, CC, VW, name, tr=256):
    T = o.shape[0]
    tr = min(tr, T)
    H = VW // LANES
    zoff = p.shape[1] // VW - 1

    def body(o_ref, z_ref, ng_ref, dmix_ref, do_ref, dz_ref, da_ref, dng_ref, dpb_ref):
        first = pl.program_id(0) == 0
        da = dmix_ref[:, :CC]
        da_ref[...] = da.astype(BF16)
        _acc_row(dpb_ref, 0, jnp.sum(da, axis=0, keepdims=True), first)
        ng = ng_ref[...]
        dng = jnp.zeros((1, LANES), F32)
        for h in range(H):
            cols = slice(h * LANES, (h + 1) * LANES)
            ov = o_ref[:, cols]
            zv = z_ref[:, cols]
            dout = dmix_ref[:, CC + h * LANES:CC + (h + 1) * LANES]
            r = lax.rsqrt(jnp.mean(ov * ov, axis=-1, keepdims=True) + EPS)
            on = ov * r * ng
            don = dout * _silu(zv)
            dz_ref[:, cols] = (dout * on * _dsilu(zv)).astype(BF16)
            dng = dng + jnp.sum(don * ov * r, axis=0, keepdims=True)
            dong = don * ng
            do_ref[:, cols] = r * dong - ov * (r * r * r) * jnp.mean(dong * ov, axis=-1, keepdims=True)
        _acc_row(dng_ref, 0, dng, first)

    return pl.pallas_call(
        body, name=name, grid=(T // tr,),
        in_specs=[pl.BlockSpec((tr, VW), lambda i: (i, 0)), pl.BlockSpec((tr, VW), lambda i: (i, zoff)),
                  pl.BlockSpec((1, LANES), lambda i: (0, 0)), pl.BlockSpec((tr, CC + VW), lambda i: (i, 0))],
        out_specs=[pl.BlockSpec((tr, VW), lambda i: (i, 0)), pl.BlockSpec((tr, VW), lambda i: (i, 0)),
                   pl.BlockSpec((tr, CC), lambda i: (i, 0)), pl.BlockSpec((1, LANES), lambda i: (0, 0)),
                   pl.BlockSpec((1, CC), lambda i: (0, 0))],
        out_shape=[jax.ShapeDtypeStruct((T, VW), F32), jax.ShapeDtypeStruct((T, VW), BF16),
                   jax.ShapeDtypeStruct((T, CC), BF16), jax.ShapeDtypeStruct((1, LANES), F32),
                   jax.ShapeDtypeStruct((1, CC), F32)],
        compiler_params=_cparams(("arbitrary",)),
    )(o, p, norm_g.reshape(1, LANES), dmix)


FFN_CW = 256


def _ffn_act_fwd(gu, conv_w, conv_b, *, Bl, S, name):
    FF = gu.shape[2]
    cw = min(FFN_CW, FF)

    def body(g_ref, u_ref, w_ref, b_ref, a_ref):
        gc = _conv_fwd(g_ref[...], w_ref, FFN_CONV_K) + b_ref[...]
        a_ref[...] = (_silu(gc) * u_ref[...]).astype(BF16)

    return pl.pallas_call(
        body, name=name, grid=(Bl, FF // cw),
        in_specs=[pl.BlockSpec((None, S, cw), lambda b, j: (0, b, j)),
                  pl.BlockSpec((None, S, cw), lambda b, j: (1, b, j)),
                  pl.BlockSpec((FFN_CONV_K, cw), lambda b, j: (0, j)),
                  pl.BlockSpec((1, cw), lambda b, j: (0, j))],
        out_specs=pl.BlockSpec((S, cw), lambda b, j: (b, j)),
        out_shape=jax.ShapeDtypeStruct((Bl * S, FF), BF16),
        compiler_params=_cparams(("parallel", "parallel")),
    )(gu, gu, conv_w, conv_b.reshape(1, FF))


def _ffn_act_bwd(gu, conv_w, conv_b, da, *, Bl, S, name):
    FF = gu.shape[2]
    cw = min(FFN_CW, FF)

    def body(g_ref, u_ref, w_ref, b_ref, da_ref, dgu_ref, dw_ref, db_ref):
        first = pl.program_id(1) == 0
        gate = g_ref[...]
        gc = _conv_fwd(gate, w_ref, FFN_CONV_K) + b_ref[...]
        dav = da_ref[...]
        dgu_ref[1] = (dav * _silu(gc)).astype(BF16)
        dgc = dav * u_ref[...] * _dsilu(gc)
        _acc_row(db_ref, 0, jnp.sum(dgc, axis=0, keepdims=True), first)
        _conv_bwd_w(dgc, gate, dw_ref, FFN_CONV_K, first)
        dgu_ref[0] = _conv_bwd_in(dgc, w_ref, FFN_CONV_K).astype(BF16)

    return pl.pallas_call(
        body, name=name, grid=(FF // cw, Bl),
        in_specs=[pl.BlockSpec((None, S, cw), lambda j, b: (0, b, j)),
                  pl.BlockSpec((None, S, cw), lambda j, b: (1, b, j)),
                  pl.BlockSpec((FFN_CONV_K, cw), lambda j, b: (0, j)),
                  pl.BlockSpec((1, cw), lambda j, b: (0, j)),
                  pl.BlockSpec((S, cw), lambda j, b: (b, j))],
        out_specs=[pl.BlockSpec((2, S, cw), lambda j, b: (0, b, j)),
                   pl.BlockSpec((FFN_CONV_K, cw), lambda j, b: (0, j)),
                   pl.BlockSpec((1, cw), lambda j, b: (0, j))],
        out_shape=[jax.ShapeDtypeStruct((2, Bl * S, FF), BF16),
                   jax.ShapeDtypeStruct((FFN_CONV_K, FF), F32), jax.ShapeDtypeStruct((1, FF), F32)],
        compiler_params=_cparams(("parallel", "arbitrary")),
    )(gu, gu, conv_w, conv_b.reshape(1, FF), da)


def _layer_dims(W):
    CC = W["conv_pw_b"].shape[0]
    VW = W["mix_norm_g"].shape[0] - CC
    KW = (W["gdn_conv_w"].shape[1] - VW) // 2
    return CC, KW, VW


def _layer_fwd(l, x, W, Bl, S, fetch):
    CC, KW, VW = _layer_dims(W)
    H = VW // LANES
    w_in_t, w_in_ba = fetch(l, "w_in", x)
    n_main = (w_in_t.shape[0] // LANES) * LANES
    h1 = _rms_fwd(x, W["mix_norm_g"], name="rms1_fwd")
    p = _mm(h1, w_in_t, tb=True, b_rows=n_main, name="mm_in")
    pba = _mm(h1, w_in_ba, tb=True, name="mm_in_ba")
    u3, u1 = _conf_fwd(p, W["conv_dw_w"], W["conv_dw_b"], W["conv_ln_g"], W["conv_ln_b"], Bl=Bl, S=S, CC=CC,
                       name="conf_fwd")
    conv_pw_w = fetch(l, "conv_pw_w", u3)
    out_a = _mm(u3, conv_pw_w, bias=W["conv_pw_b"], out_dtype=BF16, name="mm_pw")
    qkv = _gdn_pre_fwd(p, W["gdn_conv_w"], Bl=Bl, S=S, CC=CC, KW=KW, VW=VW, name="gdn_pre_fwd")
    g, beta = _gdn_gate_fwd(pba, W["gdn_a_log"], W["gdn_dt_bias"], Bl=Bl, S=S, H=H, name="gdn_gate_fwd")
    o, states, tinv, gdn_u, gdn_w = _gdn_core_fwd(qkv, g, beta, Bl=Bl, S=S, KW=KW, VW=VW, name="gdn_core_fwd")
    mix = _gdn_out_fwd(o, p, W["gdn_norm_g"], out_a, CC=CC, VW=VW, name="gdn_out_fwd")
    w_out = fetch(l, "w_out", mix)
    x1 = _mm(mix, w_out, res=x, name="mm_out")
    h2 = _rms_fwd(x1, W["ffn_norm_g"], name="rms2_fwd")
    w_up = fetch(l, "w_up", h2)
    gu = _mm(h2, w_up, out_blocks=2, tn=w_up.shape[2], name="mm_up")
    a = _ffn_act_fwd(gu, W["ffn_conv_w"], W["ffn_conv_b"], Bl=Bl, S=S, name="ffn_act_fwd")
    w_down = fetch(l, "w_down", a)
    x2 = _mm(a, w_down, res=x1, name="mm_down")
    saved = dict(x=x, h1=h1, p=p, pba=pba, u1=u1, u3=u3, qkv=qkv, g=g, beta=beta, o=o, states=states, tinv=tinv,
                 gdn_u=gdn_u, gdn_w=gdn_w, mix=mix, x1=x1, h2=h2, gu=gu, a=a, w_in_t=w_in_t, w_in_ba=w_in_ba, conv_pw_w=conv_pw_w,
                 w_out=w_out, w_up=w_up, w_down=w_down)
    return x2, saved


def _layer_bwd(l, dx2, dx2b, W, A, Bl, S, sink):
    CC, KW, VW = _layer_dims(W)
    H = VW // LANES
    G = {}
    upw = A["w_up"].shape[2]
    da = _mm(dx2b, A["w_down"], tb=True, tn=upw, name="mm_down_dx")
    da = sink(l, "w_down", _mm(A["a"], dx2b, ta=True, out_dtype=BF16, tm=upw, name="mm_down_dw"), da)
    dgu, G["ffn_conv_w"], G["ffn_conv_b"] = _ffn_act_bwd(A["gu"], W["ffn_conv_w"], W["ffn_conv_b"], da,
                                                         Bl=Bl, S=S, name="ffn_act_bwd")
    dh2 = _mm(dgu, A["w_up"], tb=True, tk=upw, tn=2048, name="mm_up_dx")
    dh2 = sink(l, "w_up", _mm(A["h2"], dgu, ta=True, out_dtype=BF16, out_blocks=N_DEV, tn=upw, name="mm_up_dw"),
               dh2)
    dx1, dx1b, G["ffn_norm_g"] = _rms_bwd(A["x1"], W["ffn_norm_g"], dh2, dx2, name="rms2_bwd")
    dmix = _mm(dx1b, A["w_out"], tb=True, name="mm_out_dx")
    dmix = sink(l, "w_out", _mm(A["mix"], dx1b, ta=True, out_dtype=BF16, name="mm_out_dw"), dmix)
    do, dz, dout_a, G["gdn_norm_g"], G["conv_pw_b"] = _gdn_out_bwd(A["o"], A["p"], W["gdn_norm_g"], dmix,
                                                                   CC=CC, VW=VW, name="gdn_out_bwd")
    dq, dk, dv, dg, dbeta = _gdn_core_bwd(A["qkv"], A["g"], A["beta"], A["states"], A["tinv"], A["gdn_u"],
                                          A["gdn_w"], do, Bl=Bl, S=S, KW=KW, VW=VW, name="gdn_core_bwd")
    dpba, dalog, ddtb = _gdn_gate_bwd(A["pba"], W["gdn_a_log"], W["gdn_dt_bias"], dg, dbeta, Bl=Bl, S=S, H=H,
                                      name="gdn_gate_bwd")
    G["gdn_a_log"], G["gdn_dt_bias"] = dalog[:H, 0], ddtb[:H, 0]
    dqkv, G["gdn_conv_w"] = _gdn_pre_bwd(A["p"], W["gdn_conv_w"], dq, dk, dv, Bl=Bl, S=S, CC=CC, KW=KW, VW=VW,
                                         name="gdn_pre_bwd")
    du3 = _mm(dout_a, A["conv_pw_w"], tb=True, name="mm_pw_dx")
    du3 = sink(l, "conv_pw_w", _mm(A["u3"], dout_a, ta=True, out_dtype=BF16, name="mm_pw_dw"), du3)
    dav, dag, G["conv_dw_w"], G["conv_dw_b"], G["conv_ln_g"], G["conv_ln_b"] = _conf_bwd(
        A["p"], A["u1"], W["conv_dw_w"], W["conv_ln_g"], W["conv_ln_b"], du3, Bl=Bl, S=S, CC=CC, name="conf_bwd")
    dp = jnp.concatenate([dav, dag, dqkv, dz], axis=1)
    dp = sink(l, "w_in", (_mm(dp, A["h1"], ta=True, out_dtype=BF16, name="mm_in_dw"),
                          _mm(dpba, A["h1"], ta=True, out_dtype=BF16, name="mm_in_ba_dw")), dp)
    dh1 = _mm(dpba, A["w_in_ba"], name="mm_in_ba_dx")
    dh1 = _mm(dp, A["w_in_t"], b_rows=dp.shape[1], res=dh1, name="mm_in_dx")
    dx, dxb, G["mix_norm_g"] = _rms_bwd(A["x"], W["mix_norm_g"], dh1, dx1, name="rms1_bwd")
    return dx, dxb, G


def _local_step(x, target, Ws, final_norm_g, fetch, sink):
    Bl, S, D = x.shape
    xt = x.reshape(Bl * S, D)
    acts = []
    for l, W in enumerate(Ws):
        xt, A = _layer_fwd(l, xt, W, Bl, S, fetch)
        acts.append(A)
    loss, dx, dxb, dgf = _loss_head(xt, final_norm_g, target.reshape(Bl * S, D), name="loss_head")
    grads = [None] * len(Ws)
    for l in reversed(range(len(Ws))):
        dx, dxb, grads[l] = _layer_bwd(l, dx, dxb, Ws[l], acts[l], Bl, S, sink)
    return loss[0, 0], dx.reshape(Bl, S, D), grads, dgf.reshape(D)


def _mesh_pos():
    return lax.axis_index("x"), lax.axis_index("y"), lax.axis_index("c")


def _dev_index(px, py, pc):
    return 4 * px + 2 * py + pc


_ANY = pl.BlockSpec(memory_space=pl.ANY)


def _all_gather(arrs, *, name):
    n = len(arrs)

    def body(*refs):
        ins, outs = refs[:n], refs[n:2 * n]
        send_sems, recv_sems, local_sems = refs[2 * n:]
        x, y, c = _mesh_pos()
        me, sibling = (x, y, c), (x, y, 1 - c)
        chips = [(1 - x, y), (x, 1 - y), (1 - x, 1 - y)]

        def copy(a, k, block, to, src=None):
            dst = outs[a].at[_dev_index(*block)]
            return pltpu.make_async_remote_copy(
                src_ref=dst if src is None else src, dst_ref=dst,
                send_sem=send_sems.at[a, k], recv_sem=recv_sems.at[a, k],
                device_id=to, device_id_type=MESH)

        mine = [pltpu.make_async_copy(ins[a], outs[a].at[_dev_index(*me)], local_sems.at[a]) for a in range(n)]
        for cp in mine:
            cp.start()
        first = []
        for a in range(n):
            first.append(copy(a, 0, me, sibling, src=ins[a]))
            first += [copy(a, 1 + j, me, (*chip, c), src=ins[a]) for j, chip in enumerate(chips)]
        for cp in first:
            cp.start()
        passed = []
        for a in range(n):
            for j, chip in enumerate(chips):
                copy(a, 1 + j, (*chip, c), me).wait_recv()
                fwd = copy(a, 4 + j, (*chip, c), sibling)
                fwd.start()
                passed.append(fwd)
        for a in range(n):
            copy(a, 0, sibling, me).wait_recv()
            for j, chip in enumerate(chips):
                copy(a, 4 + j, (*chip, 1 - c), me).wait_recv()
        for cp in first + passed:
            cp.wait_send()
        for cp in mine:
            cp.wait()

    return pl.pallas_call(
        body, name=name,
        in_specs=[_ANY] * n, out_specs=[_ANY] * n,
        out_shape=[jax.ShapeDtypeStruct((N_DEV,) + a.shape, a.dtype) for a in arrs],
        scratch_shapes=[pltpu.SemaphoreType.DMA((n, N_DEV - 1)), pltpu.SemaphoreType.DMA((n, N_DEV - 1)),
                        pltpu.SemaphoreType.DMA((n,))],
    )(*arrs)


def _peers(x, y, c):
    flip = lambda v, f: 1 - v if f else v
    return [(flip(x, p & 4), flip(y, p & 2), flip(c, p & 1)) for p in range(1, N_DEV)]


GATHER_ID, SCATTER_ID = 1, 2
_SEQUENCER = dict(axis_name="sequencer", num_cores=1)


def _handshake(peers):
    barrier = pltpu.get_barrier_semaphore()
    for peer in peers:
        pl.semaphore_signal(barrier, inc=1, device_id=peer, device_id_type=MESH)
    pl.semaphore_wait(barrier, len(peers))


def _sc_gather(src, *, name):
    def body(src_ref, zone_ref, send_sems, recv_sems, local_sem):
        x, y, c = _mesh_pos()
        me, sibling = (x, y, c), (x, y, 1 - c)
        chips = [(1 - x, y), (x, 1 - y), (1 - x, 1 - y)]
        _handshake([sibling] + [(*chip, c) for chip in chips])

        def copy(k, block, to, from_src=False):
            dst = zone_ref.at[_dev_index(*block)]
            return pltpu.make_async_remote_copy(
                src_ref=src_ref if from_src else dst, dst_ref=dst, send_sem=send_sems.at[k], recv_sem=recv_sems.at[k],
                device_id=to, device_id_type=MESH)

        mine = pltpu.make_async_copy(src_ref, zone_ref.at[_dev_index(*me)], local_sem)
        mine.start()
        first = [copy(1 + j, me, (*chip, c), from_src=True) for j, chip in enumerate(chips)]
        first.append(copy(0, me, sibling, from_src=True))
        for cp in first:
            cp.start()
        passed = []
        for j, chip in enumerate(chips):
            copy(1 + j, (*chip, c), me).wait_recv()
            fwd = copy(4 + j, (*chip, c), sibling)
            fwd.start()
            passed.append(fwd)
        copy(0, sibling, me).wait_recv()
        for j, chip in enumerate(chips):
            copy(4 + j, (*chip, 1 - c), me).wait_recv()
        for cp in first + passed:
            cp.wait_send()
        mine.wait()

    return pl.kernel(
        body, name=name,
        out_type=jax.ShapeDtypeStruct((N_DEV,) + src.shape, src.dtype),
        mesh=plsc.ScalarSubcoreMesh(**_SEQUENCER),
        scratch_types=[pltpu.SemaphoreType.DMA((N_DEV - 1,)), pltpu.SemaphoreType.DMA((N_DEV - 1,)),
                       pltpu.SemaphoreType.DMA],
        compiler_params=pltpu.CompilerParams(collective_id=GATHER_ID),
    )(src)


def _sc_scatter(part, *, name):
    def body(src_ref, zone_ref, send_sems, recv_sems, local_sem):
        x, y, c = _mesh_pos()
        me = _dev_index(x, y, c)
        peers = _peers(x, y, c)
        _handshake(peers)
        mine = pltpu.make_async_copy(src_ref.at[me], zone_ref.at[me], local_sem)
        mine.start()
        sends = [pltpu.make_async_remote_copy(
            src_ref=src_ref.at[_dev_index(*peer)], dst_ref=zone_ref.at[me], send_sem=send_sems.at[k],
            recv_sem=recv_sems.at[k], device_id=peer, device_id_type=MESH) for k, peer in enumerate(peers)]
        for cp in sends:
            cp.start()
        for k, peer in enumerate(peers):
            pltpu.make_async_remote_copy(
                src_ref=src_ref.at[me], dst_ref=zone_ref.at[_dev_index(*peer)], send_sem=send_sems.at[k],
                recv_sem=recv_sems.at[k], device_id=peer, device_id_type=MESH).wait_recv()
        for cp in sends:
            cp.wait_send()
        mine.wait()

    return pl.kernel(
        body, name=name,
        out_type=jax.ShapeDtypeStruct(part.shape, part.dtype),
        mesh=plsc.ScalarSubcoreMesh(**_SEQUENCER),
        scratch_types=[pltpu.SemaphoreType.DMA((N_DEV - 1,)), pltpu.SemaphoreType.DMA((N_DEV - 1,)),
                       pltpu.SemaphoreType.DMA],
        compiler_params=pltpu.CompilerParams(collective_id=SCATTER_ID),
    )(part)


def _adamw_math(w, g, m, v):
    m2 = ADAM_B1 * m + (1.0 - ADAM_B1) * g
    v2 = ADAM_B2 * v + (1.0 - ADAM_B2) * (g * g)
    m_hat = m2 / (1.0 - ADAM_B1 ** ADAM_STEP)
    v_hat = v2 / (1.0 - ADAM_B2 ** ADAM_STEP)
    delta = -ADAM_LR * (m_hat / (jnp.sqrt(v_hat) + ADAM_EPS) + ADAM_WD * w)
    return delta, m2, v2


def _adamw_big(l, w, m, v, recv, prev, *, summed=False, name, tr=128):
    L, R, C = w.shape
    tr = next(t for t in range(min(tr, R), 0, -16) if R % t == 0)

    def body(w_ref, m_ref, v_ref, r_ref, *rest):
        g_ref, d_ref, m2_ref, v2_ref = rest[-4:]
        if summed:
            g = r_ref[...]
        else:
            g = r_ref[0].astype(F32)
            for s in range(1, N_DEV):
                g = g + r_ref[s].astype(F32)
        g_ref[...] = g
        d_ref[...], m2_ref[...], v2_ref[...] = _adamw_math(w_ref[...], g, m_ref[...], v_ref[...])

    wspec = pl.BlockSpec((None, tr, C), lambda i: (l, i, 0))
    rspec = pl.BlockSpec((tr, C), lambda i: (i, 0)) if summed else pl.BlockSpec((N_DEV, tr, C), lambda i: (0, i, 0))
    prev = list(prev) if prev is not None else []
    return pl.pallas_call(
        body, name=name, grid=(R // tr,),
        in_specs=[wspec, wspec, wspec, rspec] + [_ANY] * len(prev),
        out_specs=[wspec] * 4,
        out_shape=[jax.ShapeDtypeStruct((L, R, C), F32)] * 4,
        input_output_aliases={4 + j: j for j in range(len(prev))},
        compiler_params=_cparams(("parallel",)),
    )(w, m, v, recv if summed else recv.reshape(N_DEV, R, C), *prev)


def _sum_slots_wide(recv, *, name, tc=512):
    _, R, C = recv.shape
    tc = _tile(C, tc)

    def body(r_ref, o_ref):
        g = r_ref[0].astype(F32)
        for s in range(1, N_DEV):
            g = g + r_ref[s].astype(F32)
        o_ref[...] = g

    return pl.pallas_call(
        body, name=name, grid=(C // tc,),
        in_specs=[pl.BlockSpec((N_DEV, R, tc), lambda j: (0, 0, j))],
        out_specs=pl.BlockSpec((R, tc), lambda j: (0, j)),
        out_shape=jax.ShapeDtypeStruct((R, C), F32),
        compiler_params=_cparams(("parallel",)),
    )(recv)


def _sum_slots(gathered, *, name):
    _, R, C = gathered.shape

    def body(r_ref, o_ref):
        g = r_ref[0]
        for s in range(1, N_DEV):
            g = g + r_ref[s]
        o_ref[...] = g

    return pl.pallas_call(body, name=name, out_shape=jax.ShapeDtypeStruct((R, C), F32))(gathered)


def _adamw_small(w, g, m, v, *, name):
    def body(w_ref, g_ref, m_ref, v_ref, d_ref, m2_ref, v2_ref):
        d_ref[...], m2_ref[...], v2_ref[...] = _adamw_math(w_ref[...], g_ref[...], m_ref[...], v_ref[...])

    return pl.pallas_call(body, name=name, out_shape=[jax.ShapeDtypeStruct(w.shape, F32)] * 3)(w, g, m, v)


def _pack(arrs):
    flat = []
    for a in arrs:
        a = a.reshape(-1).astype(F32)
        flat.append(jnp.pad(a, (0, (-a.shape[0]) % LANES)))
    out = jnp.concatenate(flat)
    out = jnp.pad(out, (0, (-out.shape[0]) % (8 * LANES)))
    return out.reshape(-1, LANES)


def _unpack(packed, shapes):
    flat = packed.reshape(-1)
    out, pos = [], 0
    for s in shapes:
        size = math.prod(s)
        out.append(flat[pos:pos + size].reshape(s))
        pos += size + (-size) % LANES
    return out


BIG = ("w_in", "conv_pw_w", "w_out", "w_up", "w_down")
SMALL_SHARDED = ("conv_dw_w", "gdn_conv_w", "ffn_conv_w")
SMALL_REPLICATED = ("mix_norm_g", "conv_dw_b", "conv_ln_g", "conv_ln_b", "conv_pw_b", "gdn_a_log", "gdn_dt_bias",
                    "gdn_norm_g", "ffn_norm_g", "ffn_conv_b")
WEIGHTS = ("mix_norm_g", "w_in", "conv_dw_w", "conv_dw_b", "conv_ln_g", "conv_ln_b", "conv_pw_w", "conv_pw_b",
           "gdn_conv_w", "gdn_a_log", "gdn_dt_bias", "gdn_norm_g", "w_out", "ffn_norm_g", "w_up", "ffn_conv_w",
           "ffn_conv_b", "w_down", "final_norm_g")


def _train_step(x, target, w, m, v):
    L = w["w_in"].shape[0]
    D = x.shape[-1]
    xi, yi, ci = _mesh_pos()
    me = _dev_index(xi, yi, ci)

    gathered = {}

    def launch(l, after=None):
        for n in BIG:
            src = (w[n][l].T if n == "w_in" else w[n][l]).astype(BF16)
            if after is not None:
                src = lax.optimization_barrier((src, after))[0]
            gathered[n, l] = _sc_gather(src, name=f"gather_{n}_{l}")

    launch(0)
    small_full = {}
    for n, g_ in zip(SMALL_SHARDED, _all_gather([w[n] for n in SMALL_SHARDED], name="all_gather_conv_taps")):
        small_full[n] = jnp.moveaxis(g_, 0, 2).reshape(L, g_.shape[2], N_DEV * g_.shape[3])
    Ws = []
    for l in range(L):
        W = {n: w[n][l] for n in SMALL_REPLICATED}
        W.update({n: small_full[n][l] for n in SMALL_SHARDED})
        Ws.append(W)

    def fetch(l, n, after):
        if n == "conv_pw_w" and l + 1 < L:
            launch(l + 1, after)
        g_ = lax.optimization_barrier((gathered[n, l], after))[0]
        if n == "w_up":
            return g_
        g_ = g_.reshape(g_.shape[0] * g_.shape[1], g_.shape[2])
        if n == "w_in":
            n_main = (g_.shape[0] // LANES) * LANES
            return g_, jnp.pad(g_[n_main:], ((0, LANES - (g_.shape[0] - n_main)), (0, 0)))
        return g_

    started = []
    res = {}
    SCATTERS_IN_FLIGHT = 2

    def consume(chain):
        n, l, recv = started.pop(0)
        if n == "w_in":
            recv = _sum_slots_wide(recv, name="sum_w_in_grad").T
        res[n] = _adamw_big(l, w[n], m[n], v[n], recv, res.get(n), summed=(n == "w_in"), name=f"adamw_{n}")
        if chain is None:
            return None
        tied = lax.optimization_barrier((chain, *res[n]))
        res[n] = list(tied[1:])
        return tied[0]

    def sink(l, n, g_, chain):
        g_, chain = lax.optimization_barrier((g_, chain))
        if len(started) >= SCATTERS_IN_FLIGHT:
            chain = consume(chain)
        if n == "w_in":
            g_main, g_ba = g_
            g_ = jnp.concatenate([g_main, g_ba[:w["w_in"].shape[2] * N_DEV - g_main.shape[0]]], axis=0)
            part = g_.reshape(N_DEV, -1, D)
        elif n == "w_up":
            part = g_
        else:
            part = g_.reshape(N_DEV, -1, g_.shape[1])
        started.append((n, l, _sc_scatter(part, name=f"scatter_{n}_{l}")))
        return chain

    loss, grad_x, G, d_final = _local_step(x, target, Ws, w["final_norm_g"], fetch, sink)

    small_names = [n for n in WEIGHTS if n not in BIG]
    partial = []
    for n in small_names:
        if n == "final_norm_g":
            partial.append(d_final)
        else:
            partial.append(jnp.stack([G[l][n].reshape(Ws[l][n].shape) for l in range(L)]))
    partial.append(loss.reshape(1))
    small_gathered = _sc_gather(_pack(partial), name="gather_small_grads")

    out = {k: {} for k in ("grad", "delta", "new_m", "new_v")}
    while started:
        consume(None)
    for n in BIG:
        for j, k in enumerate(("grad", "delta", "new_m", "new_v")):
            out[k][n] = res[n][j]

    summed = _unpack(_sum_slots(small_gathered, name="sum_small_grads"), [p_.shape for p_ in partial])
    full = dict(zip(small_names, summed))
    loss = summed[-1][0]
    for n in SMALL_SHARDED:
        width = w[n].shape[-1]
        full[n] = lax.dynamic_slice_in_dim(full[n], me * width, width, axis=2)
    loc_shapes = [w[n].shape for n in small_names]
    g_pack = _pack([full[n] for n in small_names])
    res = _adamw_small(_pack([w[n] for n in small_names]), g_pack, _pack([m[n] for n in small_names]),
                       _pack([v[n] for n in small_names]), name="adamw_small")
    for k, packed in zip(("grad", "delta", "new_m", "new_v"), (g_pack,) + tuple(res)):
        out[k].update(dict(zip(small_names, _unpack(packed, loc_shapes))))
    return loss, grad_x, out


def kernel(x, mix_norm_g, w_in, conv_dw_w, conv_dw_b, conv_ln_g, conv_ln_b, conv_pw_w, conv_pw_b, gdn_conv_w, gdn_a_log, gdn_dt_bias, gdn_norm_g, w_out, ffn_norm_g, w_up, ffn_conv_w, ffn_conv_b, w_down, final_norm_g, loss_target, m_mix_norm_g, m_w_in, m_conv_dw_w, m_conv_dw_b, m_conv_ln_g, m_conv_ln_b, m_conv_pw_w, m_conv_pw_b, m_gdn_conv_w, m_gdn_a_log, m_gdn_dt_bias, m_gdn_norm_g, m_w_out, m_ffn_norm_g, m_w_up, m_ffn_conv_w, m_ffn_conv_b, m_w_down, m_final_norm_g, v_mix_norm_g, v_w_in, v_conv_dw_w, v_conv_dw_b, v_conv_ln_g, v_conv_ln_b, v_conv_pw_w, v_conv_pw_b, v_gdn_conv_w, v_gdn_a_log, v_gdn_dt_bias, v_gdn_norm_g, v_w_out, v_ffn_norm_g, v_w_up, v_ffn_conv_w, v_ffn_conv_b, v_w_down, v_final_norm_g):
    w = dict(zip(WEIGHTS, (mix_norm_g, w_in, conv_dw_w, conv_dw_b, conv_ln_g, conv_ln_b, conv_pw_w, conv_pw_b, gdn_conv_w,
                           gdn_a_log, gdn_dt_bias, gdn_norm_g, w_out, ffn_norm_g, w_up, ffn_conv_w, ffn_conv_b, w_down,
                           final_norm_g)))
    m = dict(zip(WEIGHTS, (m_mix_norm_g, m_w_in, m_conv_dw_w, m_conv_dw_b, m_conv_ln_g, m_conv_ln_b, m_conv_pw_w,
                           m_conv_pw_b, m_gdn_conv_w, m_gdn_a_log, m_gdn_dt_bias, m_gdn_norm_g, m_w_out, m_ffn_norm_g,
                           m_w_up, m_ffn_conv_w, m_ffn_conv_b, m_w_down, m_final_norm_g)))
    v = dict(zip(WEIGHTS, (v_mix_norm_g, v_w_in, v_conv_dw_w, v_conv_dw_b, v_conv_ln_g, v_conv_ln_b, v_conv_pw_w,
                           v_conv_pw_b, v_gdn_conv_w, v_gdn_a_log, v_gdn_dt_bias, v_gdn_norm_g, v_w_out, v_ffn_norm_g,
                           v_w_up, v_ffn_conv_w, v_ffn_conv_b, v_w_down, v_final_norm_g)))
    loss, grad_x, out = _train_step(x, loss_target, w, m, v)
    return (loss, grad_x, *[out["grad"][n] for n in WEIGHTS], *[out["delta"][n] for n in WEIGHTS],
            *[out["new_m"][n] for n in WEIGHTS], *[out["new_v"][n] for n in WEIGHTS])
```

```python
import functools
import math

import jax
import jax.numpy as jnp
from jax import lax
from jax.experimental import pallas as pl
from jax.experimental.pallas import tpu as pltpu
from jax.experimental.pallas import tpu_sc as plsc

F32 = jnp.float32
BF16 = jnp.bfloat16
MESH = pl.DeviceIdType.MESH

EPS = 1e-6
LANES = 128
CHUNK = 128
NEAR_BLOCK = 32
CONV_K = 31
SHORT_CONV_K = 4
FFN_CONV_K = 3
N_DEV = 8
VMEM_LIMIT = 56 * 1024 * 1024

ADAM_LR = 0.001
ADAM_B1 = 0.9
ADAM_B2 = 0.999
ADAM_EPS = 1e-08
ADAM_WD = 0.01
ADAM_STEP = 10


def _cparams(sem):
    return pltpu.CompilerParams(dimension_semantics=sem, vmem_limit_bytes=VMEM_LIMIT)


def _sig(x):
    return 1.0 / (1.0 + jnp.exp(-x))


def _silu(x):
    return x * _sig(x)


def _dsilu(x):
    s = _sig(x)
    return s * (1.0 + x * (1.0 - s))


def _softplus(x):
    return jnp.maximum(x, 0.0) + jnp.log1p(jnp.exp(-jnp.abs(x)))


def _dot(a, b):
    return jnp.dot(a, b, preferred_element_type=F32)


def _dot_nt(a, b):
    return lax.dot_general(a, b, (((1,), (1,)), ((), ())), preferred_element_type=F32)


def _dot_tn(a, b):
    return lax.dot_general(a, b, (((0,), (0,)), ((), ())), preferred_element_type=F32)


def _bf(x):
    return x.astype(BF16)


_NN = (((1,), (0,)), ((), ()))
_TN = (((0,), (0,)), ((), ()))


def _split2(x):
    hi = _bf(x)
    return hi, _bf(x - hi.astype(F32))


def _dot_x3(a, b, dn=_NN):
    ah, al = _split2(a)
    bh, bl = _split2(b)
    f = lambda p, q: lax.dot_general(p, q, dn, preferred_element_type=F32)
    return f(ah, bh) + (f(al, bh) + f(ah, bl))


def _dot_mask(mask, x, dn=_NN):
    mb = _bf(mask)
    hi, lo = _split2(x)
    lo2 = _bf(x - hi.astype(F32) - lo.astype(F32))
    f = lambda q: lax.dot_general(mb, q, dn, preferred_element_type=F32)
    return f(hi) + (f(lo) + f(lo2))


def _shift_down(u, s):
    if s == 0:
        return u
    row = lax.broadcasted_iota(jnp.int32, u.shape, 0)
    return jnp.where(row >= s, pltpu.roll(u, s, 0), 0.0)


def _shift_up(u, s):
    if s == 0:
        return u
    n = u.shape[0]
    row = lax.broadcasted_iota(jnp.int32, u.shape, 0)
    return jnp.where(row < n - s, pltpu.roll(u, n - s, 0), 0.0)


def _conv_fwd(u, w_ref, K):
    acc = None
    for k in range(K):
        term = w_ref[k:k + 1, :] * _shift_down(u, K - 1 - k)
        acc = term if acc is None else acc + term
    return acc


def _conv_bwd_in(do, w_ref, K):
    acc = None
    for k in range(K):
        term = w_ref[k:k + 1, :] * _shift_up(do, K - 1 - k)
        acc = term if acc is None else acc + term
    return acc


def _conv_bwd_w(do, u, dw_ref, K, first):
    for k in range(K):
        row = jnp.sum(do * _shift_down(u, K - 1 - k), axis=0, keepdims=True)
        _acc_row(dw_ref, k, row, first)


def _acc_row(ref, k, row, first):
    @pl.when(first)
    def _():
        ref[k:k + 1, :] = row

    @pl.when(jnp.logical_not(first))
    def _():
        ref[k:k + 1, :] += row


def _logical(arr):
    if arr.ndim == 2:
        return arr.shape
    return (arr.shape[1], arr.shape[0] * arr.shape[2])


def _tile(dim, pref, *col_widths):
    if dim % LANES:
        assert not col_widths
        return dim
    t = (min(pref, dim) // LANES) * LANES
    while t > LANES and (dim % t or any(c % t for c in col_widths)):
        t -= LANES
    assert dim % t == 0 and all(c % t == 0 for c in col_widths), (dim, pref, col_widths)
    return t


def _spec(shape, rt, ct, rfn, cfn):
    if len(shape) == 2:
        return pl.BlockSpec((rt, ct), lambda i, j, k: (rfn(i, j, k), cfn(i, j, k)))
    per = shape[2] // ct
    return pl.BlockSpec((None, rt, ct),
                        lambda i, j, k: (cfn(i, j, k) // per, rfn(i, j, k), cfn(i, j, k) % per))


def _mm(a, b, *, name, ta=False, tb=False, out_dtype=F32, out_blocks=None, bias=None, res=None, b_rows=None,
        tm=1024, tn=1024, tk=2048):
    ra, ca = _logical(a)
    rb, cb = _logical(b)
    if b_rows is not None:
        assert b.ndim == 2 and b_rows <= rb
        rb = b_rows
    M, K = (ca, ra) if ta else (ra, ca)
    N, K2 = (rb, cb) if tb else (cb, rb)
    assert K == K2, (a.shape, b.shape, ta, tb)
    out_shape = (M, N) if out_blocks is None else (out_blocks, M, N // out_blocks)
    cw = lambda arr: [arr.shape[2]] if arr.ndim == 3 else []
    m_c = cw(a) if ta else []
    k_c = (cw(a) if not ta else []) + (cw(b) if tb else [])
    n_c = (cw(b) if not tb else []) + ([out_shape[2]] if out_blocks else []) + (cw(res) if res is not None else [])
    tm, tn, tk = _tile(M, tm, *m_c), _tile(N, tn, *n_c), _tile(K, tk, *k_c)
    nk = K // tk
    im, jn, kk = (lambda i, j, k: i), (lambda i, j, k: j), (lambda i, j, k: k)
    in_specs = [
        _spec(a.shape, tk, tm, kk, im) if ta else _spec(a.shape, tm, tk, im, kk),
        _spec(b.shape, tn, tk, jn, kk) if tb else _spec(b.shape, tk, tn, kk, jn),
    ]
    args = [a, b]
    if bias is not None:
        in_specs.append(pl.BlockSpec((1, tn), lambda i, j, k: (0, j)))
        args.append(bias.reshape(1, N).astype(F32))
    if res is not None:
        in_specs.append(_spec(res.shape, tm, tn, im, jn))
        args.append(res)
    dn = (((0 if ta else 1,), (1 if tb else 0,)), ((), ()))

    def body(*refs):
        a_ref, b_ref = refs[0], refs[1]
        pos = 2
        bias_ref = res_ref = None
        if bias is not None:
            bias_ref = refs[pos]
            pos += 1
        if res is not None:
            res_ref = refs[pos]
            pos += 1
        o_ref = refs[pos]
        k = pl.program_id(2)
        part = lax.dot_general(_bf(a_ref[...]), _bf(b_ref[...]), dn, preferred_element_type=F32)

        def finish(r):
            if bias_ref is not None:
                r = r + bias_ref[...]
            if res_ref is not None:
                r = r + res_ref[...].astype(F32)
            o_ref[...] = r.astype(out_dtype)

        if nk == 1:
            finish(part)
            return
        acc_ref = refs[pos + 1]

        @pl.when(k == 0)
        def _():
            acc_ref[...] = part

        @pl.when((k > 0) & (k < nk - 1))
        def _():
            acc_ref[...] += part

        @pl.when(k == nk - 1)
        def _():
            finish(acc_ref[...] + part)

    return pl.pallas_call(
        body, name=name,
        grid=(M // tm, N // tn, nk),
        in_specs=in_specs,
        out_specs=_spec(out_shape, tm, tn, im, jn),
        out_shape=jax.ShapeDtypeStruct(out_shape, out_dtype),
        scratch_shapes=[pltpu.VMEM((tm, tn), F32)] if nk > 1 else [],
        compiler_params=_cparams(("parallel", "parallel", "arbitrary")),
    )(*args)


def _rms_fwd(x, g, *, name, tr=512):
    T, D = x.shape
    tr = min(tr, T)

    def body(x_ref, g_ref, h_ref):
        xv = x_ref[...]
        r = lax.rsqrt(jnp.mean(xv * xv, axis=-1, keepdims=True) + EPS)
        h_ref[...] = (xv * r * g_ref[...]).astype(BF16)

    return pl.pallas_call(
        body, name=name, grid=(T // tr,),
        in_specs=[pl.BlockSpec((tr, D), lambda i: (i, 0)), pl.BlockSpec((1, D), lambda i: (0, 0))],
        out_specs=pl.BlockSpec((tr, D), lambda i: (i, 0)),
        out_shape=jax.ShapeDtypeStruct((T, D), BF16),
        compiler_params=_cparams(("parallel",)),
    )(x, g.reshape(1, D))


def _rms_bwd(x, g, dh, dres, *, name, tr=512):
    T, D = x.shape
    tr = min(tr, T)

    def body(x_ref, g_ref, dh_ref, dres_ref, dx_ref, dxb_ref, dg_ref):
        i = pl.program_id(0)
        xv = x_ref[...]
        dy = dh_ref[...].astype(F32)
        r = lax.rsqrt(jnp.mean(xv * xv, axis=-1, keepdims=True) + EPS)
        dyg = dy * g_ref[...]
        dot = jnp.mean(dyg * xv, axis=-1, keepdims=True)
        dx = dres_ref[...] + r * dyg - xv * (r * r * r) * dot
        dx_ref[...] = dx
        dxb_ref[...] = dx.astype(BF16)
        part = jnp.sum(dy * xv * r, axis=0, keepdims=True)
        _acc_row(dg_ref, 0, part, i == 0)

    row = pl.BlockSpec((tr, D), lambda i: (i, 0))
    vec = pl.BlockSpec((1, D), lambda i: (0, 0))
    return pl.pallas_call(
        body, name=name, grid=(T // tr,),
        in_specs=[row, vec, row, row],
        out_specs=[row, row, vec],
        out_shape=[jax.ShapeDtypeStruct((T, D), F32), jax.ShapeDtypeStruct((T, D), BF16),
                   jax.ShapeDtypeStruct((1, D), F32)],
        compiler_params=_cparams(("arbitrary",)),
    )(x, g.reshape(1, D), dh, dres)


def _loss_head(x, g, target, *, name, tr=512):
    T, D = x.shape
    tr = min(tr, T)

    def body(x_ref, g_ref, t_ref, loss_ref, dx_ref, dxb_ref, dg_ref):
        i = pl.program_id(0)
        xv = x_ref[...]
        gv = g_ref[...]
        r = lax.rsqrt(jnp.mean(xv * xv, axis=-1, keepdims=True) + EPS)
        y = xv * r * gv
        err = y - t_ref[...]
        lpart = 0.5 * jnp.sum(jnp.mean(err * err, axis=-1, keepdims=True), axis=0, keepdims=True)
        dy = err * (1.0 / D)
        dyg = dy * gv
        dot = jnp.mean(dyg * xv, axis=-1, keepdims=True)
        dx = r * dyg - xv * (r * r * r) * dot
        dx_ref[...] = dx
        dxb_ref[...] = dx.astype(BF16)
        _acc_row(dg_ref, 0, jnp.sum(dy * xv * r, axis=0, keepdims=True), i == 0)
        _acc_row(loss_ref, 0, jnp.broadcast_to(lpart, (1, LANES)), i == 0)

    row = pl.BlockSpec((tr, D), lambda i: (i, 0))
    return pl.pallas_call(
        body, name=name, grid=(T // tr,),
        in_specs=[row, pl.BlockSpec((1, D), lambda i: (0, 0)), row],
        out_specs=[pl.BlockSpec((1, LANES), lambda i: (0, 0)), row, row, pl.BlockSpec((1, D), lambda i: (0, 0))],
        out_shape=[jax.ShapeDtypeStruct((1, LANES), F32), jax.ShapeDtypeStruct((T, D), F32),
                   jax.ShapeDtypeStruct((T, D), BF16), jax.ShapeDtypeStruct((1, D), F32)],
        compiler_params=_cparams(("arbitrary",)),
    )(x, g.reshape(1, D), target)


HALO = 32
SUBLANES = 8


def _conv_dw_blocks(do, u, dw_ref, K, first, u_s, do_p):
    S, C = u.shape
    zeros = jnp.zeros((HALO, C), F32)
    for r in range(SUBLANES):
        u_s[r, 0:HALO, :] = zeros
        u_s[r, HALO:HALO + S, :] = _shift_down(u, r)
    do_p[0:HALO, :] = zeros
    do_p[HALO:HALO + S, :] = do
    do_p[HALO + S:2 * HALO + S, :] = zeros
    n_a = (K - 1) // SUBLANES + 1

    def block(i, accs):
        i0 = pl.multiple_of(i * SUBLANES, SUBLANES)
        us = [u_s[r, pl.ds(i0, SUBLANES), :] for r in range(SUBLANES)]
        ds = [do_p[pl.ds(i0 + SUBLANES * a, SUBLANES), :] for a in range(n_a)]
        out = list(accs)
        for a in range(n_a):
            for r in range(SUBLANES):
                s = SUBLANES * a + r
                if s < K:
                    out[K - 1 - s] = out[K - 1 - s] + ds[a] * us[r]
        return tuple(out)

    accs = lax.fori_loop(HALO // SUBLANES, (S + HALO) // SUBLANES, block, (jnp.zeros((SUBLANES, C), F32),) * K)
    for k in range(K):
        _acc_row(dw_ref, k, jnp.sum(accs[k], axis=0, keepdims=True), first)


def _conf_norm(u1, lg_ref, lb_ref):
    mu = jnp.mean(u1, axis=-1, keepdims=True)
    xc = u1 - mu
    r = lax.rsqrt(jnp.mean(xc * xc, axis=-1, keepdims=True) + EPS)
    n = xc * r
    return r, n, n * lg_ref[...] + lb_ref[...]


def _conf_fwd(p, dw_w, dw_b, ln_g, ln_b, *, Bl, S, CC, name):
    G = CC // LANES

    def body(av_ref, ag_ref, w_ref, b_ref, lg_ref, lb_ref, o_ref, u1_ref):
        u0 = av_ref[...] * _sig(ag_ref[...])
        u1 = _conv_fwd(u0, w_ref, CONV_K) + b_ref[...]
        u1_ref[...] = u1
        _, _, u2 = _conf_norm(u1, lg_ref, lb_ref)
        o_ref[...] = _silu(u2).astype(BF16)

    vec = pl.BlockSpec((1, LANES), lambda b, j: (0, j))
    seq = pl.BlockSpec((S, LANES), lambda b, j: (b, j))
    return pl.pallas_call(
        body, name=name, grid=(Bl, G),
        in_specs=[seq, pl.BlockSpec((S, LANES), lambda b, j: (b, G + j)),
                  pl.BlockSpec((CONV_K, LANES), lambda b, j: (0, j)), vec, vec, vec],
        out_specs=[seq, seq],
        out_shape=[jax.ShapeDtypeStruct((Bl * S, CC), BF16), jax.ShapeDtypeStruct((Bl * S, CC), F32)],
        compiler_params=_cparams(("parallel", "parallel")),
    )(p, p, dw_w, dw_b.reshape(1, CC), ln_g.reshape(1, CC), ln_b.reshape(1, CC))


def _conf_bwd(p, u1, dw_w, ln_g, ln_b, du3, *, Bl, S, CC, name):
    G = CC // LANES

    def body(av_ref, ag_ref, u1_ref, w_ref, lg_ref, lb_ref, du3_ref,
             dav_ref, dag_ref, dw_ref, db_ref, dlg_ref, dlb_ref, u_s, do_p):
        first = pl.program_id(1) == 0
        av = av_ref[...]
        sg = _sig(ag_ref[...])
        r, n, u2 = _conf_norm(u1_ref[...], lg_ref, lb_ref)
        du2 = du3_ref[...] * _dsilu(u2)
        _acc_row(dlg_ref, 0, jnp.sum(du2 * n, axis=0, keepdims=True), first)
        _acc_row(dlb_ref, 0, jnp.sum(du2, axis=0, keepdims=True), first)
        dn = du2 * lg_ref[...]
        du1 = r * (dn - jnp.mean(dn, axis=-1, keepdims=True) - n * jnp.mean(dn * n, axis=-1, keepdims=True))
        _acc_row(db_ref, 0, jnp.sum(du1, axis=0, keepdims=True), first)
        _conv_dw_blocks(du1, av * sg, dw_ref, CONV_K, first, u_s, do_p)
        du0 = _conv_bwd_in(du1, w_ref, CONV_K)
        dav_ref[...] = (du0 * sg).astype(BF16)
        dag_ref[...] = (du0 * av * sg * (1.0 - sg)).astype(BF16)

    vec = pl.BlockSpec((1, LANES), lambda j, b: (0, j))
    seq = pl.BlockSpec((S, LANES), lambda j, b: (b, j))
    return pl.pallas_call(
        body, name=name, grid=(G, Bl),
        in_specs=[seq, pl.BlockSpec((S, LANES), lambda j, b: (b, G + j)), seq,
                  pl.BlockSpec((CONV_K, LANES), lambda j, b: (0, j)), vec, vec, seq],
        out_specs=[seq, seq, pl.BlockSpec((CONV_K, LANES), lambda j, b: (0, j)), vec, vec, vec],
        out_shape=[jax.ShapeDtypeStruct((Bl * S, CC), BF16), jax.ShapeDtypeStruct((Bl * S, CC), BF16),
                   jax.ShapeDtypeStruct((CONV_K, CC), F32), jax.ShapeDtypeStruct((1, CC), F32),
                   jax.ShapeDtypeStruct((1, CC), F32), jax.ShapeDtypeStruct((1, CC), F32)],
        scratch_shapes=[pltpu.VMEM((SUBLANES, S + HALO, LANES), F32), pltpu.VMEM((S + 2 * HALO, LANES), F32)],
        compiler_params=_cparams(("parallel", "arbitrary")),
    )(p, p, u1, dw_w, ln_g.reshape(1, CC), ln_b.reshape(1, CC), du3)


def _gdn_pre_fwd(p, conv_w, *, Bl, S, CC, KW, VW, name):
    NQK = 2 * KW // LANES
    NB = NQK + VW // LANES
    off = 2 * CC // LANES

    def body(x_ref, w_ref, o_ref):
        j = pl.program_id(1)
        s = _silu(_conv_fwd(x_ref[...], w_ref, SHORT_CONV_K))
        r = lax.rsqrt(jnp.sum(s * s, axis=-1, keepdims=True) + EPS)
        o_ref[...] = jnp.where(j < NQK, s * r, s)

    return pl.pallas_call(
        body, name=name, grid=(Bl, NB),
        in_specs=[pl.BlockSpec((S, LANES), lambda b, j: (b, off + j)),
                  pl.BlockSpec((SHORT_CONV_K, LANES), lambda b, j: (0, j))],
        out_specs=pl.BlockSpec((S, LANES), lambda b, j: (b, j)),
        out_shape=jax.ShapeDtypeStruct((Bl * S, NB * LANES), F32),
        compiler_params=_cparams(("parallel", "parallel")),
    )(p, conv_w)


def _gdn_pre_bwd(p, conv_w, dq, dk, dv, *, Bl, S, CC, KW, VW, name):
    HQ = KW // LANES
    H = VW // LANES
    NQK = 2 * HQ
    NB = NQK + H
    off = 2 * CC // LANES

    def body(x_ref, w_ref, dq_ref, dk_ref, dv_ref, dx_ref, dw_ref):
        j = pl.program_id(0)
        first = pl.program_id(1) == 0
        xv = x_ref[...]
        c = _conv_fwd(xv, w_ref, SHORT_CONV_K)
        s = _silu(c)
        r = lax.rsqrt(jnp.sum(s * s, axis=-1, keepdims=True) + EPS)
        dy = jnp.where(j < HQ, dq_ref[...], jnp.where(j < NQK, dk_ref[...], dv_ref[...]))
        ds_norm = r * dy - s * (r * r * r) * jnp.sum(s * dy, axis=-1, keepdims=True)
        ds = jnp.where(j < NQK, ds_norm, dy)
        dc = ds * _dsilu(c)
        _conv_bwd_w(dc, xv, dw_ref, SHORT_CONV_K, first)
        dx_ref[...] = _conv_bwd_in(dc, w_ref, SHORT_CONV_K).astype(BF16)

    return pl.pallas_call(
        body, name=name, grid=(NB, Bl),
        in_specs=[pl.BlockSpec((S, LANES), lambda j, b: (b, off + j)),
                  pl.BlockSpec((SHORT_CONV_K, LANES), lambda j, b: (0, j)),
                  pl.BlockSpec((S, LANES), lambda j, b: (b, jnp.minimum(j, HQ - 1))),
                  pl.BlockSpec((S, LANES), lambda j, b: (b, jnp.clip(j - HQ, 0, HQ - 1))),
                  pl.BlockSpec((S, LANES), lambda j, b: (b, jnp.clip(j - NQK, 0, H - 1)))],
        out_specs=[pl.BlockSpec((S, LANES), lambda j, b: (b, j)),
                   pl.BlockSpec((SHORT_CONV_K, LANES), lambda j, b: (0, j))],
        out_shape=[jax.ShapeDtypeStruct((Bl * S, NB * LANES), BF16),
                   jax.ShapeDtypeStruct((SHORT_CONV_K, NB * LANES), F32)],
        compiler_params=_cparams(("parallel", "arbitrary")),
    )(p, conv_w, dq, dk, dv)


def _split3(x):
    hi, lo = _split2(x)
    return hi, lo, _bf(x - hi.astype(F32) - lo.astype(F32))


def _lane_replicate(parts, h):
    row = lax.broadcasted_iota(jnp.int32, (LANES, LANES), 0)
    E = jnp.where(row == h, 1.0, 0.0).astype(BF16)
    return _dot(parts[0], E) + (_dot(parts[1], E) + _dot(parts[2], E))


def _gdn_gate_fwd(pba, a_log, dt_bias, *, Bl, S, H, name):
    def body(alog_ref, dtb_ref, x_ref, g_ref, beta_ref):
        parts = _split3(x_ref[...])
        for h in range(H):
            b_raw = _lane_replicate(parts, h)
            a_raw = _lane_replicate(parts, H + h)
            beta_ref[h] = _sig(b_raw)
            ea = jnp.exp(jnp.zeros((1, LANES), F32) + alog_ref[h])
            g_ref[h] = -ea * _softplus(a_raw + dtb_ref[h])

    smem = pl.BlockSpec(memory_space=pltpu.SMEM)
    rep = pl.BlockSpec((H, S, LANES), lambda b: (0, b, 0))
    return pl.pallas_call(
        body, name=name, grid=(Bl,),
        in_specs=[smem, smem, pl.BlockSpec((S, LANES), lambda b: (b, 0))],
        out_specs=[rep, rep],
        out_shape=[jax.ShapeDtypeStruct((H, Bl * S, LANES), F32)] * 2,
        compiler_params=_cparams(("parallel",)),
    )(a_log, dt_bias, pba)


def _gdn_gate_bwd(pba, a_log, dt_bias, dg, dbeta, *, Bl, S, H, name):
    HP = 8 * ((H + 7) // 8)

    def body(alog_ref, dtb_ref, x_ref, dg_ref, dbeta_ref, dx_ref, dalog_ref, ddtb_ref):
        first = pl.program_id(0) == 0
        parts = _split3(x_ref[...])
        lane = lax.broadcasted_iota(jnp.int32, (S, LANES), 1)
        acc = jnp.zeros((S, LANES), F32)

        @pl.when(first)
        def _():
            dalog_ref[...] = jnp.zeros_like(dalog_ref)
            ddtb_ref[...] = jnp.zeros_like(ddtb_ref)

        for h in range(H):
            b_raw = _lane_replicate(parts, h)
            a_raw = _lane_replicate(parts, H + h)
            beta = _sig(b_raw)
            db_raw = dbeta_ref[h] * beta * (1.0 - beta)
            z = a_raw + dtb_ref[h]
            ea = jnp.exp(jnp.zeros((1, LANES), F32) + alog_ref[h])
            dgv = dg_ref[h]
            da_raw = dgv * (-ea) * _sig(z)
            g = -ea * _softplus(z)
            dalog_ref[h:h + 1, :] += jnp.sum(dgv * g, axis=0, keepdims=True)
            ddtb_ref[h:h + 1, :] += jnp.sum(da_raw, axis=0, keepdims=True)
            acc = acc + jnp.where(lane == h, db_raw, 0.0) + jnp.where(lane == H + h, da_raw, 0.0)
        dx_ref[...] = acc.astype(BF16)

    smem = pl.BlockSpec(memory_space=pltpu.SMEM)
    rep = pl.BlockSpec((H, S, LANES), lambda b: (0, b, 0))
    small = pl.BlockSpec((HP, LANES), lambda b: (0, 0))
    return pl.pallas_call(
        body, name=name, grid=(Bl,),
        in_specs=[smem, smem, pl.BlockSpec((S, LANES), lambda b: (b, 0)), rep, rep],
        out_specs=[pl.BlockSpec((S, LANES), lambda b: (b, 0)), small, small],
        out_shape=[jax.ShapeDtypeStruct((Bl * S, LANES), BF16),
                   jax.ShapeDtypeStruct((HP, LANES), F32), jax.ShapeDtypeStruct((HP, LANES), F32)],
        compiler_params=_cparams(("arbitrary",)),
    )(a_log, dt_bias, pba, dg, dbeta)


def _tri_masks():
    ri = lax.broadcasted_iota(jnp.int32, (CHUNK, CHUNK), 0)
    ci = lax.broadcasted_iota(jnp.int32, (CHUNK, CHUNK), 1)
    return ri >= ci, ri > ci, ri == CHUNK - 1


def _tri_inv(L):
    ri = lax.broadcasted_iota(jnp.int32, (CHUNK, CHUNK), 0)
    ci = lax.broadcasted_iota(jnp.int32, (CHUNK, CHUNK), 1)
    T = jnp.where(ri == ci, 1.0, 0.0) - jnp.where((ri >> 1) == (ci >> 1), L, 0.0)
    for lv in range(2, int(math.log2(CHUNK)) + 1):
        O = jnp.where(((ri >> lv) == (ci >> lv)) & ((ri >> (lv - 1)) != (ci >> (lv - 1))), L, 0.0)
        if (1 << lv) <= NEAR_BLOCK:
            T = T - _dot_x3(T, _dot_x3(O, T))
        else:
            Tb = _bf(T)
            T = T - _dot(Tb, _bf(_dot(_bf(O), Tb)))
    return T


def _chunk_local(q, k, v, beta, g):
    ge, gt, last = _tri_masks()
    gam = _dot_mask(ge, g)
    D = jnp.where(ge, jnp.exp(jnp.where(ge, gam - gam.T, 0.0)), 0.0)
    kb = k * beta
    vb = v * beta
    M = _dot_nt(_bf(kb), _bf(k))
    L = jnp.where(gt, M * D, 0.0)
    eg = jnp.exp(gam)
    kbg = kb * eg
    P = _dot_nt(_bf(q), _bf(k))
    QK = jnp.where(ge, P * D, 0.0)
    gl = jnp.sum(jnp.where(last, gam, 0.0), axis=0, keepdims=True)
    el = jnp.exp(gl - gam)
    return dict(ge=ge, gt=gt, last=last, gam=gam, D=D, kb=kb, vb=vb, L=L, eg=eg, kbg=kbg, QK=QK, gl=gl,
                el=el, kd=k * el, qg=q * eg)


def _rowsum(x):
    return jnp.sum(x, axis=-1, keepdims=True)


def _chunk_bwd(q, k, v, beta, g, S, T, u, w, do, dS2):
    c = _chunk_local(q, k, v, beta, g)
    ge, gt, last = c["ge"], c["gt"], c["last"]
    Sb = _bf(S)
    vn = u - _dot(w, Sb)
    dob, vnb, dS2b = _bf(do), _bf(vn), _bf(dS2)
    e_last = jnp.exp(c["gl"])
    dqg = _dot_nt(dob, Sb)
    dS = _dot_tn(_bf(c["qg"]), dob)
    dQK = jnp.where(ge, _dot_nt(dob, vnb), 0.0)
    dvn = _dot_tn(_bf(c["QK"]), dob)
    dS = dS + dS2 * e_last
    de_last = jnp.sum(jnp.sum(dS2 * S, axis=0, keepdims=True), axis=1, keepdims=True)
    dkd = _dot_nt(vnb, dS2b)
    dvn = dvn + _dot(_bf(c["kd"]), dS2b)
    dvnb = _bf(dvn)
    dw = -_dot_nt(dvnb, Sb)
    dS = dS - _dot_tn(w, dvnb)
    dsol = _dot_x3(T, jnp.concatenate([dvn, dw], axis=1), _TN)
    dvb, dkbg = dsol[:, :LANES], dsol[:, LANES:]
    dA = -(_dot_nt(_bf(dvb), _bf(u)) + _dot_nt(_bf(dkbg), w))
    dL = jnp.where(gt, dA, 0.0)
    dM = dL * c["D"]
    dP = dQK * c["D"]
    E = dL * c["L"] + dQK * c["QK"]
    kbf = _bf(k)
    dkb = _dot(_bf(dM), kbf) + dkbg * c["eg"]
    dk = _dot_tn(_bf(dM), _bf(c["kb"])) + _dot_tn(_bf(dP), _bf(q)) + dkd * c["el"] + dkb * beta
    dq = _dot(_bf(dP), kbf) + dqg * c["eg"]
    s_kd = _rowsum(dkd * c["kd"])
    dgam = (_rowsum(E) - _rowsum(E.T) + _rowsum(dqg * c["qg"]) - s_kd + _rowsum(dkbg * c["kbg"]))
    dgl = jnp.sum(s_kd, axis=0, keepdims=True) + de_last * e_last
    dgam_rep = jnp.broadcast_to(dgam, (CHUNK, LANES)) + jnp.where(last, jnp.broadcast_to(dgl, (CHUNK, LANES)), 0.0)
    dg_rep = _dot_mask(ge, dgam_rep, _TN)
    dbeta = _rowsum(dkb * k) + _rowsum(dvb * v)
    dv = dvb * beta
    return dq, dk, dv, jnp.broadcast_to(dbeta, (CHUNK, LANES)), dg_rep, dS


def _gdn_core_fwd(qkv, g, beta, *, Bl, S, KW, VW, name):
    HQ = KW // LANES
    H = VW // LANES
    NC = S // CHUNK
    scale = float(LANES) ** -0.5

    PAIR = 2 if NC % 2 == 0 else 1

    def body(q_ref, k_ref, v_ref, g_ref, beta_ref, o_ref, st_ref, t_ref, u_s, w_s,
             qk_s, qg_s, kd_s, el_s):
        def local(n2, carry):
            for half in range(PAIR):
                n = n2 * PAIR + half
                rows = pl.ds(pl.multiple_of(n * CHUNK, CHUNK), CHUNK)
                q = q_ref[rows, :] * scale
                k = k_ref[rows, :]
                for e in range(2):
                    c = _chunk_local(q, k, v_ref[rows, e * LANES:(e + 1) * LANES], beta_ref[e, rows, :],
                                     g_ref[e, rows, :])
                    T = _tri_inv(c["L"])
                    t_ref[e, rows, :] = T
                    uw = _dot_x3(T, jnp.concatenate([c["vb"], c["kbg"]], axis=1))
                    u_s[e, rows, :] = uw[:, :LANES]
                    w_s[e, rows, :] = _bf(uw[:, LANES:])
                    qk_s[e, rows, :] = _bf(c["QK"])
                    qg_s[e, rows, :] = _bf(c["qg"])
                    kd_s[e, rows, :] = _bf(c["kd"])
                    el_s[e, pl.ds(pl.multiple_of(n * 8, 8), 8), :] = jnp.broadcast_to(jnp.exp(c["gl"]), (8, LANES))
            return carry

        lax.fori_loop(0, NC // PAIR, local, 0)

        def scan(n, states):
            rows = pl.ds(pl.multiple_of(n * CHUNK, CHUNK), CHUNK)
            out = []
            for e in range(2):
                S_in = states[e]
                st_ref[e, n] = S_in
                Sb = _bf(S_in)
                vn = u_s[e, rows, :] - _dot(w_s[e, rows, :], Sb)
                vnb = _bf(vn)
                o_ref[rows, e * LANES:(e + 1) * LANES] = _dot(qg_s[e, rows, :], Sb) + _dot(qk_s[e, rows, :], vnb)
                e_last = el_s[e, pl.ds(pl.multiple_of(n * 8, 8), 1), :]
                out.append(S_in * e_last + _dot_tn(kd_s[e, rows, :], vnb))
            return tuple(out)

        z = jnp.zeros((LANES, LANES), F32)
        lax.fori_loop(0, NC, scan, (z, z))

    rep = pl.BlockSpec((2, S, LANES), lambda b, h: (h, b, 0))
    return pl.pallas_call(
        body, name=name, grid=(Bl, HQ),
        in_specs=[pl.BlockSpec((S, LANES), lambda b, h: (b, h)),
                  pl.BlockSpec((S, LANES), lambda b, h: (b, HQ + h)),
                  pl.BlockSpec((S, 2 * LANES), lambda b, h: (b, HQ + h)), rep, rep],
        out_specs=[pl.BlockSpec((S, 2 * LANES), lambda b, h: (b, h)),
                   pl.BlockSpec((None, 2, NC, LANES, LANES), lambda b, h: (b, h, 0, 0, 0)), rep, rep, rep],
        out_shape=[jax.ShapeDtypeStruct((Bl * S, VW), F32),
                   jax.ShapeDtypeStruct((Bl, H, NC, LANES, LANES), F32),
                   jax.ShapeDtypeStruct((H, Bl * S, LANES), F32),
                   jax.ShapeDtypeStruct((H, Bl * S, LANES), F32),
                   jax.ShapeDtypeStruct((H, Bl * S, LANES), BF16)],
        scratch_shapes=[pltpu.VMEM((2, S, LANES), BF16)] * 3 + [pltpu.VMEM((2, NC * 8, LANES), F32)],
        compiler_params=_cparams(("parallel", "parallel")),
    )(qkv, qkv, qkv, g, beta)


def _gdn_core_bwd(qkv, g, beta, states, tinv, u, w, do, *, Bl, S, KW, VW, name):
    HQ = KW // LANES
    H = VW // LANES
    NC = S // CHUNK
    scale = float(LANES) ** -0.5

    def body(q_ref, k_ref, v_ref, g_ref, beta_ref, st_ref, t_ref, u_ref, w_ref, do_ref,
             dq_ref, dk_ref, dv_ref, dg_ref, dbeta_ref):
        def step(i, dstates):
            n = NC - 1 - i
            rows = pl.ds(pl.multiple_of(n * CHUNK, CHUNK), CHUNK)
            q = q_ref[rows, :] * scale
            k = k_ref[rows, :]
            out = []
            dq_sum = dk_sum = None
            for e in range(2):
                cols = slice(e * LANES, (e + 1) * LANES)
                dq, dk, dv, dbeta, dg, dS = _chunk_bwd(q, k, v_ref[rows, cols], beta_ref[e, rows, :],
                                                       g_ref[e, rows, :], st_ref[e, n], t_ref[e, rows, :],
                                                       u_ref[e, rows, :], w_ref[e, rows, :],
                                                       do_ref[rows, cols], dstates[e])
                dv_ref[rows, cols] = dv
                dg_ref[e, rows, :] = dg
                dbeta_ref[e, rows, :] = dbeta
                dq_sum = dq if dq_sum is None else dq_sum + dq
                dk_sum = dk if dk_sum is None else dk_sum + dk
                out.append(dS)
            dq_ref[rows, :] = dq_sum * scale
            dk_ref[rows, :] = dk_sum
            return tuple(out)

        z = jnp.zeros((LANES, LANES), F32)
        lax.fori_loop(0, NC, step, (z, z))

    rep = pl.BlockSpec((2, S, LANES), lambda b, h: (h, b, 0))
    seq = pl.BlockSpec((S, LANES), lambda b, h: (b, h))
    seq2 = pl.BlockSpec((S, 2 * LANES), lambda b, h: (b, h))
    return pl.pallas_call(
        body, name=name, grid=(Bl, HQ),
        in_specs=[seq, pl.BlockSpec((S, LANES), lambda b, h: (b, HQ + h)),
                  pl.BlockSpec((S, 2 * LANES), lambda b, h: (b, HQ + h)), rep, rep,
                  pl.BlockSpec((None, 2, NC, LANES, LANES), lambda b, h: (b, h, 0, 0, 0)), rep, rep, rep, seq2],
        out_specs=[seq, seq, seq2, rep, rep],
        out_shape=[jax.ShapeDtypeStruct((Bl * S, KW), F32), jax.ShapeDtypeStruct((Bl * S, KW), F32),
                   jax.ShapeDtypeStruct((Bl * S, VW), F32),
                   jax.ShapeDtypeStruct((H, Bl * S, LANES), F32), jax.ShapeDtypeStruct((H, Bl * S, LANES), F32)],
        compiler_params=_cparams(("parallel", "parallel")),
    )(qkv, qkv, qkv, g, beta, states, tinv, u, w, do)


def _gdn_out_fwd(o, p, norm_g, out_a, *, CC, VW, name, tr=256):
    T = o.shape[0]
    tr = min(tr, T)
    H = VW // LANES
    zoff = p.shape[1] // VW - 1

    def body(o_ref, z_ref, ng_ref, a_ref, mix_ref):
        mix_ref[:, :CC] = a_ref[...]
        for h in range(H):
            cols = slice(h * LANES, (h + 1) * LANES)
            ov = o_ref[:, cols]
            r = lax.rsqrt(jnp.mean(ov * ov, axis=-1, keepdims=True) + EPS)
            mix_ref[:, CC + h * LANES:CC + (h + 1) * LANES] = (ov * r * ng_ref[...] * _silu(z_ref[:, cols])).astype(BF16)

    return pl.pallas_call(
        body, name=name, grid=(T // tr,),
        in_specs=[pl.BlockSpec((tr, VW), lambda i: (i, 0)), pl.BlockSpec((tr, VW), lambda i: (i, zoff)),
                  pl.BlockSpec((1, LANES), lambda i: (0, 0)), pl.BlockSpec((tr, CC), lambda i: (i, 0))],
        out_specs=pl.BlockSpec((tr, CC + VW), lambda i: (i, 0)),
        out_shape=jax.ShapeDtypeStruct((T, CC + VW), BF16),
        compiler_params=_cparams(("parallel",)),
    )(o, p, norm_g.reshape(1, LANES), out_a)


def _gdn_out_bwd(o, p, norm_g, dmix, *, CC, VW, name, tr=256):
    T = o.shape[0]
    tr = min(tr, T)
    H = VW // LANES
    zoff = p.shape[1] // VW - 1

    def body(o_ref, z_ref, ng_ref, dmix_ref, do_ref, dz_ref, da_ref, dng_ref, dpb_ref):
        first = pl.program_id(0) == 0
        da = dmix_ref[:, :CC]
        da_ref[...] = da.astype(BF16)
        _acc_row(dpb_ref, 0, jnp.sum(da, axis=0, keepdims=True), first)
        ng = ng_ref[...]
        dng = jnp.zeros((1, LANES), F32)
        for h in range(H):
            cols = slice(h * LANES, (h + 1) * LANES)
            ov = o_ref[:, cols]
            zv = z_ref[:, cols]
            dout = dmix_ref[:, CC + h * LANES:CC + (h + 1) * LANES]
            r = lax.rsqrt(jnp.mean(ov * ov, axis=-1, keepdims=True) + EPS)
            on = ov * r * ng
            don = dout * _silu(zv)
            dz_ref[:, cols] = (dout * on * _dsilu(zv)).astype(BF16)
            dng = dng + jnp.sum(don * ov * r, axis=0, keepdims=True)
            dong = don * ng
            do_ref[:, cols] = r * dong - ov * (r * r * r) * jnp.mean(dong * ov, axis=-1, keepdims=True)
        _acc_row(dng_ref, 0, dng, first)

    return pl.pallas_call(
        body, name=name, grid=(T // tr,),
        in_specs=[pl.BlockSpec((tr, VW), lambda i: (i, 0)), pl.BlockSpec((tr, VW), lambda i: (i, zoff)),
                  pl.BlockSpec((1, LANES), lambda i: (0, 0)), pl.BlockSpec((tr, CC + VW), lambda i: (i, 0))],
        out_specs=[pl.BlockSpec((tr, VW), lambda i: (i, 0)), pl.BlockSpec((tr, VW), lambda i: (i, 0)),
                   pl.BlockSpec((tr, CC), lambda i: (i, 0)), pl.BlockSpec((1, LANES), lambda i: (0, 0)),
                   pl.BlockSpec((1, CC), lambda i: (0, 0))],
        out_shape=[jax.ShapeDtypeStruct((T, VW), F32), jax.ShapeDtypeStruct((T, VW), BF16),
                   jax.ShapeDtypeStruct((T, CC), BF16), jax.ShapeDtypeStruct((1, LANES), F32),
                   jax.ShapeDtypeStruct((1, CC), F32)],
        compiler_params=_cparams(("arbitrary",)),
    )(o, p, norm_g.reshape(1, LANES), dmix)


FFN_CW = 256


def _ffn_act_fwd(gu, conv_w, conv_b, *, Bl, S, name):
    FF = gu.shape[2]
    cw = min(FFN_CW, FF)

    def body(g_ref, u_ref, w_ref, b_ref, a_ref):
        gc = _conv_fwd(g_ref[...], w_ref, FFN_CONV_K) + b_ref[...]
        a_ref[...] = (_silu(gc) * u_ref[...]).astype(BF16)

    return pl.pallas_call(
        body, name=name, grid=(Bl, FF // cw),
        in_specs=[pl.BlockSpec((None, S, cw), lambda b, j: (0, b, j)),
                  pl.BlockSpec((None, S, cw), lambda b, j: (1, b, j)),
                  pl.BlockSpec((FFN_CONV_K, cw), lambda b, j: (0, j)),
                  pl.BlockSpec((1, cw), lambda b, j: (0, j))],
        out_specs=pl.BlockSpec((S, cw), lambda b, j: (b, j)),
        out_shape=jax.ShapeDtypeStruct((Bl * S, FF), BF16),
        compiler_params=_cparams(("parallel", "parallel")),
    )(gu, gu, conv_w, conv_b.reshape(1, FF))


def _ffn_act_bwd(gu, conv_w, conv_b, da, *, Bl, S, name):
    FF = gu.shape[2]
    cw = min(FFN_CW, FF)

    def body(g_ref, u_ref, w_ref, b_ref, da_ref, dgu_ref, dw_ref, db_ref):
        first = pl.program_id(1) == 0
        gate = g_ref[...]
        gc = _conv_fwd(gate, w_ref, FFN_CONV_K) + b_ref[...]
        dav = da_ref[...]
        dgu_ref[1] = (dav * _silu(gc)).astype(BF16)
        dgc = dav * u_ref[...] * _dsilu(gc)
        _acc_row(db_ref, 0, jnp.sum(dgc, axis=0, keepdims=True), first)
        _conv_bwd_w(dgc, gate, dw_ref, FFN_CONV_K, first)
        dgu_ref[0] = _conv_bwd_in(dgc, w_ref, FFN_CONV_K).astype(BF16)

    return pl.pallas_call(
        body, name=name, grid=(FF // cw, Bl),
        in_specs=[pl.BlockSpec((None, S, cw), lambda j, b: (0, b, j)),
                  pl.BlockSpec((None, S, cw), lambda j, b: (1, b, j)),
                  pl.BlockSpec((FFN_CONV_K, cw), lambda j, b: (0, j)),
                  pl.BlockSpec((1, cw), lambda j, b: (0, j)),
                  pl.BlockSpec((S, cw), lambda j, b: (b, j))],
        out_specs=[pl.BlockSpec((2, S, cw), lambda j, b: (0, b, j)),
                   pl.BlockSpec((FFN_CONV_K, cw), lambda j, b: (0, j)),
                   pl.BlockSpec((1, cw), lambda j, b: (0, j))],
        out_shape=[jax.ShapeDtypeStruct((2, Bl * S, FF), BF16),
                   jax.ShapeDtypeStruct((FFN_CONV_K, FF), F32), jax.ShapeDtypeStruct((1, FF), F32)],
        compiler_params=_cparams(("parallel", "arbitrary")),
    )(gu, gu, conv_w, conv_b.reshape(1, FF), da)


def _layer_dims(W):
    CC = W["conv_pw_b"].shape[0]
    VW = W["mix_norm_g"].shape[0] - CC
    KW = (W["gdn_conv_w"].shape[1] - VW) // 2
    return CC, KW, VW


def _layer_fwd(l, x, W, Bl, S, fetch):
    CC, KW, VW = _layer_dims(W)
    H = VW // LANES
    w_in_t, w_in_ba = fetch(l, "w_in", x)
    n_main = (w_in_t.shape[0] // LANES) * LANES
    h1 = _rms_fwd(x, W["mix_norm_g"], name="rms1_fwd")
    p = _mm(h1, w_in_t, tb=True, b_rows=n_main, name="mm_in")
    pba = _mm(h1, w_in_ba, tb=True, name="mm_in_ba")
    u3, u1 = _conf_fwd(p, W["conv_dw_w"], W["conv_dw_b"], W["conv_ln_g"], W["conv_ln_b"], Bl=Bl, S=S, CC=CC,
                       name="conf_fwd")
    conv_pw_w = fetch(l, "conv_pw_w", u3)
    out_a = _mm(u3, conv_pw_w, bias=W["conv_pw_b"], out_dtype=BF16, name="mm_pw")
    qkv = _gdn_pre_fwd(p, W["gdn_conv_w"], Bl=Bl, S=S, CC=CC, KW=KW, VW=VW, name="gdn_pre_fwd")
    g, beta = _gdn_gate_fwd(pba, W["gdn_a_log"], W["gdn_dt_bias"], Bl=Bl, S=S, H=H, name="gdn_gate_fwd")
    o, states, tinv, gdn_u, gdn_w = _gdn_core_fwd(qkv, g, beta, Bl=Bl, S=S, KW=KW, VW=VW, name="gdn_core_fwd")
    mix = _gdn_out_fwd(o, p, W["gdn_norm_g"], out_a, CC=CC, VW=VW, name="gdn_out_fwd")
    w_out = fetch(l, "w_out", mix)
    x1 = _mm(mix, w_out, res=x, name="mm_out")
    h2 = _rms_fwd(x1, W["ffn_norm_g"], name="rms2_fwd")
    w_up = fetch(l, "w_up", h2)
    gu = _mm(h2, w_up, out_blocks=2, tn=w_up.shape[2], name="mm_up")
    a = _ffn_act_fwd(gu, W["ffn_conv_w"], W["ffn_conv_b"], Bl=Bl, S=S, name="ffn_act_fwd")
    w_down = fetch(l, "w_down", a)
    x2 = _mm(a, w_down, res=x1, name="mm_down")
    saved = dict(x=x, h1=h1, p=p, pba=pba, u1=u1, u3=u3, qkv=qkv, g=g, beta=beta, o=o, states=states, tinv=tinv,
                 gdn_u=gdn_u, gdn_w=gdn_w, mix=mix, x1=x1, h2=h2, gu=gu, a=a, w_in_t=w_in_t, w_in_ba=w_in_ba, conv_pw_w=conv_pw_w,
                 w_out=w_out, w_up=w_up, w_down=w_down)
    return x2, saved


def _layer_bwd(l, dx2, dx2b, W, A, Bl, S, sink):
    CC, KW, VW = _layer_dims(W)
    H = VW // LANES
    G = {}
    upw = A["w_up"].shape[2]
    da = _mm(dx2b, A["w_down"], tb=True, tn=upw, name="mm_down_dx")
    da = sink(l, "w_down", _mm(A["a"], dx2b, ta=True, out_dtype=BF16, tm=upw, name="mm_down_dw"), da)
    dgu, G["ffn_conv_w"], G["ffn_conv_b"] = _ffn_act_bwd(A["gu"], W["ffn_conv_w"], W["ffn_conv_b"], da,
                                                         Bl=Bl, S=S, name="ffn_act_bwd")
    dh2 = _mm(dgu, A["w_up"], tb=True, tk=upw, tn=2048, name="mm_up_dx")
    dh2 = sink(l, "w_up", _mm(A["h2"], dgu, ta=True, out_dtype=BF16, out_blocks=N_DEV, tn=upw, name="mm_up_dw"),
               dh2)
    dx1, dx1b, G["ffn_norm_g"] = _rms_bwd(A["x1"], W["ffn_norm_g"], dh2, dx2, name="rms2_bwd")
    dmix = _mm(dx1b, A["w_out"], tb=True, name="mm_out_dx")
    dmix = sink(l, "w_out", _mm(A["mix"], dx1b, ta=True, out_dtype=BF16, name="mm_out_dw"), dmix)
    do, dz, dout_a, G["gdn_norm_g"], G["conv_pw_b"] = _gdn_out_bwd(A["o"], A["p"], W["gdn_norm_g"], dmix,
                                                                   CC=CC, VW=VW, name="gdn_out_bwd")
    dq, dk, dv, dg, dbeta = _gdn_core_bwd(A["qkv"], A["g"], A["beta"], A["states"], A["tinv"], A["gdn_u"],
                                          A["gdn_w"], do, Bl=Bl, S=S, KW=KW, VW=VW, name="gdn_core_bwd")
    dpba, dalog, ddtb = _gdn_gate_bwd(A["pba"], W["gdn_a_log"], W["gdn_dt_bias"], dg, dbeta, Bl=Bl, S=S, H=H,
                                      name="gdn_gate_bwd")
    G["gdn_a_log"], G["gdn_dt_bias"] = dalog[:H, 0], ddtb[:H, 0]
    dqkv, G["gdn_conv_w"] = _gdn_pre_bwd(A["p"], W["gdn_conv_w"], dq, dk, dv, Bl=Bl, S=S, CC=CC, KW=KW, VW=VW,
                                         name="gdn_pre_bwd")
    du3 = _mm(dout_a, A["conv_pw_w"], tb=True, name="mm_pw_dx")
    du3 = sink(l, "conv_pw_w", _mm(A["u3"], dout_a, ta=True, out_dtype=BF16, name="mm_pw_dw"), du3)
    dav, dag, G["conv_dw_w"], G["conv_dw_b"], G["conv_ln_g"], G["conv_ln_b"] = _conf_bwd(
        A["p"], A["u1"], W["conv_dw_w"], W["conv_ln_g"], W["conv_ln_b"], du3, Bl=Bl, S=S, CC=CC, name="conf_bwd")
    dp = jnp.concatenate([dav, dag, dqkv, dz], axis=1)
    dp = sink(l, "w_in", (_mm(dp, A["h1"], ta=True, out_dtype=BF16, name="mm_in_dw"),
                          _mm(dpba, A["h1"], ta=True, out_dtype=BF16, name="mm_in_ba_dw")), dp)
    dh1 = _mm(dpba, A["w_in_ba"], name="mm_in_ba_dx")
    dh1 = _mm(dp, A["w_in_t"], b_rows=dp.shape[1], res=dh1, name="mm_in_dx")
    dx, dxb, G["mix_norm_g"] = _rms_bwd(A["x"], W["mix_norm_g"], dh1, dx1, name="rms1_bwd")
    return dx, dxb, G


def _local_step(x, target, Ws, final_norm_g, fetch, sink):
    Bl, S, D = x.shape
    xt = x.reshape(Bl * S, D)
    acts = []
    for l, W in enumerate(Ws):
        xt, A = _layer_fwd(l, xt, W, Bl, S, fetch)
        acts.append(A)
    loss, dx, dxb, dgf = _loss_head(xt, final_norm_g, target.reshape(Bl * S, D), name="loss_head")
    grads = [None] * len(Ws)
    for l in reversed(range(len(Ws))):
        dx, dxb, grads[l] = _layer_bwd(l, dx, dxb, Ws[l], acts[l], Bl, S, sink)
    return loss[0, 0], dx.reshape(Bl, S, D), grads, dgf.reshape(D)


def _mesh_pos():
    return lax.axis_index("x"), lax.axis_index("y"), lax.axis_index("c")


def _dev_index(px, py, pc):
    return 4 * px + 2 * py + pc


_ANY = pl.BlockSpec(memory_space=pl.ANY)


def _all_gather(arrs, *, name):
    n = len(arrs)

    def body(*refs):
        ins, outs = refs[:n], refs[n:2 * n]
        send_sems, recv_sems, local_sems = refs[2 * n:]
        x, y, c = _mesh_pos()
        me, sibling = (x, y, c), (x, y, 1 - c)
        chips = [(1 - x, y), (x, 1 - y), (1 - x, 1 - y)]

        def copy(a, k, block, to, src=None):
            dst = outs[a].at[_dev_index(*block)]
            return pltpu.make_async_remote_copy(
                src_ref=dst if src is None else src, dst_ref=dst,
                send_sem=send_sems.at[a, k], recv_sem=recv_sems.at[a, k],
                device_id=to, device_id_type=MESH)

        mine = [pltpu.make_async_copy(ins[a], outs[a].at[_dev_index(*me)], local_sems.at[a]) for a in range(n)]
        for cp in mine:
            cp.start()
        first = []
        for a in range(n):
            first.append(copy(a, 0, me, sibling, src=ins[a]))
            first += [copy(a, 1 + j, me, (*chip, c), src=ins[a]) for j, chip in enumerate(chips)]
        for cp in first:
            cp.start()
        passed = []
        for a in range(n):
            for j, chip in enumerate(chips):
                copy(a, 1 + j, (*chip, c), me).wait_recv()
                fwd = copy(a, 4 + j, (*chip, c), sibling)
                fwd.start()
                passed.append(fwd)
        for a in range(n):
            copy(a, 0, sibling, me).wait_recv()
            for j, chip in enumerate(chips):
                copy(a, 4 + j, (*chip, 1 - c), me).wait_recv()
        for cp in first + passed:
            cp.wait_send()
        for cp in mine:
            cp.wait()

    return pl.pallas_call(
        body, name=name,
        in_specs=[_ANY] * n, out_specs=[_ANY] * n,
        out_shape=[jax.ShapeDtypeStruct((N_DEV,) + a.shape, a.dtype) for a in arrs],
        scratch_shapes=[pltpu.SemaphoreType.DMA((n, N_DEV - 1)), pltpu.SemaphoreType.DMA((n, N_DEV - 1)),
                        pltpu.SemaphoreType.DMA((n,))],
    )(*arrs)


def _peers(x, y, c):
    flip = lambda v, f: 1 - v if f else v
    return [(flip(x, p & 4), flip(y, p & 2), flip(c, p & 1)) for p in range(1, N_DEV)]


GATHER_ID, SCATTER_ID = 1, 2
_SEQUENCER = dict(axis_name="sequencer", num_cores=1)


def _handshake(peers):
    barrier = pltpu.get_barrier_semaphore()
    for peer in peers:
        pl.semaphore_signal(barrier, inc=1, device_id=peer, device_id_type=MESH)
    pl.semaphore_wait(barrier, len(peers))


def _sc_gather(src, *, name):
    def body(src_ref, zone_ref, send_sems, recv_sems, local_sem):
        x, y, c = _mesh_pos()
        me, sibling = (x, y, c), (x, y, 1 - c)
        chips = [(1 - x, y), (x, 1 - y), (1 - x, 1 - y)]
        _handshake([sibling] + [(*chip, c) for chip in chips])

        def copy(k, block, to, from_src=False):
            dst = zone_ref.at[_dev_index(*block)]
            return pltpu.make_async_remote_copy(
                src_ref=src_ref if from_src else dst, dst_ref=dst, send_sem=send_sems.at[k], recv_sem=recv_sems.at[k],
                device_id=to, device_id_type=MESH)

        mine = pltpu.make_async_copy(src_ref, zone_ref.at[_dev_index(*me)], local_sem)
        mine.start()
        first = [copy(1 + j, me, (*chip, c), from_src=True) for j, chip in enumerate(chips)]
        first.append(copy(0, me, sibling, from_src=True))
        for cp in first:
            cp.start()
        passed = []
        for j, chip in enumerate(chips):
            copy(1 + j, (*chip, c), me).wait_recv()
            fwd = copy(4 + j, (*chip, c), sibling)
            fwd.start()
            passed.append(fwd)
        copy(0, sibling, me).wait_recv()
        for j, chip in enumerate(chips):
            copy(4 + j, (*chip, 1 - c), me).wait_recv()
        for cp in first + passed:
            cp.wait_send()
        mine.wait()

    return pl.kernel(
        body, name=name,
        out_type=jax.ShapeDtypeStruct((N_DEV,) + src.shape, src.dtype),
        mesh=plsc.ScalarSubcoreMesh(**_SEQUENCER),
        scratch_types=[pltpu.SemaphoreType.DMA((N_DEV - 1,)), pltpu.SemaphoreType.DMA((N_DEV - 1,)),
                       pltpu.SemaphoreType.DMA],
        compiler_params=pltpu.CompilerParams(collective_id=GATHER_ID),
    )(src)


def _sc_scatter(part, *, name):
    def body(src_ref, zone_ref, send_sems, recv_sems, local_sem):
        x, y, c = _mesh_pos()
        me = _dev_index(x, y, c)
        peers = _peers(x, y, c)
        _handshake(peers)
        mine = pltpu.make_async_copy(src_ref.at[me], zone_ref.at[me], local_sem)
        mine.start()
        sends = [pltpu.make_async_remote_copy(
            src_ref=src_ref.at[_dev_index(*peer)], dst_ref=zone_ref.at[me], send_sem=send_sems.at[k],
            recv_sem=recv_sems.at[k], device_id=peer, device_id_type=MESH) for k, peer in enumerate(peers)]
        for cp in sends:
            cp.start()
        for k, peer in enumerate(peers):
            pltpu.make_async_remote_copy(
                src_ref=src_ref.at[me], dst_ref=zone_ref.at[_dev_index(*peer)], send_sem=send_sems.at[k],
                recv_sem=recv_sems.at[k], device_id=peer, device_id_type=MESH).wait_recv()
        for cp in sends:
            cp.wait_send()
        mine.wait()

    return pl.kernel(
        body, name=name,
        out_type=jax.ShapeDtypeStruct(part.shape, part.dtype),
        mesh=plsc.ScalarSubcoreMesh(**_SEQUENCER),
        scratch_types=[pltpu.SemaphoreType.DMA((N_DEV - 1,)), pltpu.SemaphoreType.DMA((N_DEV - 1,)),
                       pltpu.SemaphoreType.DMA],
        compiler_params=pltpu.CompilerParams(collective_id=SCATTER_ID),
    )(part)


def _adamw_math(w, g, m, v):
    m2 = ADAM_B1 * m + (1.0 - ADAM_B1) * g
    v2 = ADAM_B2 * v + (1.0 - ADAM_B2) * (g * g)
    m_hat = m2 / (1.0 - ADAM_B1 ** ADAM_STEP)
    v_hat = v2 / (1.0 - ADAM_B2 ** ADAM_STEP)
    delta = -ADAM_LR * (m_hat / (jnp.sqrt(v_hat) + ADAM_EPS) + ADAM_WD * w)
    return delta, m2, v2


def _adamw_big(l, w, m, v, recv, prev, *, summed=False, name, tr=128):
    L, R, C = w.shape
    tr = next(t for t in range(min(tr, R), 0, -16) if R % t == 0)

    def body(w_ref, m_ref, v_ref, r_ref, *rest):
        g_ref, d_ref, m2_ref, v2_ref = rest[-4:]
        if summed:
            g = r_ref[...]
        else:
            g = r_ref[0].astype(F32)
            for s in range(1, N_DEV):
                g = g + r_ref[s].astype(F32)
        g_ref[...] = g
        d_ref[...], m2_ref[...], v2_ref[...] = _adamw_math(w_ref[...], g, m_ref[...], v_ref[...])

    wspec = pl.BlockSpec((None, tr, C), lambda i: (l, i, 0))
    rspec = pl.BlockSpec((tr, C), lambda i: (i, 0)) if summed else pl.BlockSpec((N_DEV, tr, C), lambda i: (0, i, 0))
    prev = list(prev) if prev is not None else []
    return pl.pallas_call(
        body, name=name, grid=(R // tr,),
        in_specs=[wspec, wspec, wspec, rspec] + [_ANY] * len(prev),
        out_specs=[wspec] * 4,
        out_shape=[jax.ShapeDtypeStruct((L, R, C), F32)] * 4,
        input_output_aliases={4 + j: j for j in range(len(prev))},
        compiler_params=_cparams(("parallel",)),
    )(w, m, v, recv if summed else recv.reshape(N_DEV, R, C), *prev)


def _sum_slots_wide(recv, *, name, tc=512):
    _, R, C = recv.shape
    tc = _tile(C, tc)

    def body(r_ref, o_ref):
        g = r_ref[0].astype(F32)
        for s in range(1, N_DEV):
            g = g + r_ref[s].astype(F32)
        o_ref[...] = g

    return pl.pallas_call(
        body, name=name, grid=(C // tc,),
        in_specs=[pl.BlockSpec((N_DEV, R, tc), lambda j: (0, 0, j))],
        out_specs=pl.BlockSpec((R, tc), lambda j: (0, j)),
        out_shape=jax.ShapeDtypeStruct((R, C), F32),
        compiler_params=_cparams(("parallel",)),
    )(recv)


def _sum_slots(gathered, *, name):
    _, R, C = gathered.shape

    def body(r_ref, o_ref):
        g = r_ref[0]
        for s in range(1, N_DEV):
            g = g + r_ref[s]
        o_ref[...] = g

    return pl.pallas_call(body, name=name, out_shape=jax.ShapeDtypeStruct((R, C), F32))(gathered)


def _adamw_small(w, g, m, v, *, name):
    def body(w_ref, g_ref, m_ref, v_ref, d_ref, m2_ref, v2_ref):
        d_ref[...], m2_ref[...], v2_ref[...] = _adamw_math(w_ref[...], g_ref[...], m_ref[...], v_ref[...])

    return pl.pallas_call(body, name=name, out_shape=[jax.ShapeDtypeStruct(w.shape, F32)] * 3)(w, g, m, v)


def _pack(arrs):
    flat = []
    for a in arrs:
        a = a.reshape(-1).astype(F32)
        flat.append(jnp.pad(a, (0, (-a.shape[0]) % LANES)))
    out = jnp.concatenate(flat)
    out = jnp.pad(out, (0, (-out.shape[0]) % (8 * LANES)))
    return out.reshape(-1, LANES)


def _unpack(packed, shapes):
    flat = packed.reshape(-1)
    out, pos = [], 0
    for s in shapes:
        size = math.prod(s)
        out.append(flat[pos:pos + size].reshape(s))
        pos += size + (-size) % LANES
    return out


BIG = ("w_in", "conv_pw_w", "w_out", "w_up", "w_down")
SMALL_SHARDED = ("conv_dw_w", "gdn_conv_w", "ffn_conv_w")
SMALL_REPLICATED = ("mix_norm_g", "conv_dw_b", "conv_ln_g", "conv_ln_b", "conv_pw_b", "gdn_a_log", "gdn_dt_bias",
                    "gdn_norm_g", "ffn_norm_g", "ffn_conv_b")
WEIGHTS = ("mix_norm_g", "w_in", "conv_dw_w", "conv_dw_b", "conv_ln_g", "conv_ln_b", "conv_pw_w", "conv_pw_b",
           "gdn_conv_w", "gdn_a_log", "gdn_dt_bias", "gdn_norm_g", "w_out", "ffn_norm_g", "w_up", "ffn_conv_w",
           "ffn_conv_b", "w_down", "final_norm_g")


def _train_step(x, target, w, m, v):
    L = w["w_in"].shape[0]
    D = x.shape[-1]
    xi, yi, ci = _mesh_pos()
    me = _dev_index(xi, yi, ci)

    gathered = {}

    def launch(l, after=None):
        for n in BIG:
            src = (w[n][l].T if n == "w_in" else w[n][l]).astype(BF16)
            if after is not None:
                src = lax.optimization_barrier((src, after))[0]
            gathered[n, l] = _sc_gather(src, name=f"gather_{n}_{l}")

    launch(0)
    small_full = {}
    for n, g_ in zip(SMALL_SHARDED, _all_gather([w[n] for n in SMALL_SHARDED], name="all_gather_conv_taps")):
        small_full[n] = jnp.moveaxis(g_, 0, 2).reshape(L, g_.shape[2], N_DEV * g_.shape[3])
    Ws = []
    for l in range(L):
        W = {n: w[n][l] for n in SMALL_REPLICATED}
        W.update({n: small_full[n][l] for n in SMALL_SHARDED})
        Ws.append(W)

    def fetch(l, n, after):
        if n == "conv_pw_w" and l + 1 < L:
            launch(l + 1, after)
        g_ = lax.optimization_barrier((gathered[n, l], after))[0]
        if n == "w_up":
            return g_
        g_ = g_.reshape(g_.shape[0] * g_.shape[1], g_.shape[2])
        if n == "w_in":
            n_main = (g_.shape[0] // LANES) * LANES
            return g_, jnp.pad(g_[n_main:], ((0, LANES - (g_.shape[0] - n_main)), (0, 0)))
        return g_

    started = []
    res = {}
    SCATTERS_IN_FLIGHT = 2

    def consume(chain):
        n, l, recv = started.pop(0)
        if chain is not None:
            recv, chain = lax.optimization_barrier((recv, chain))
        if n == "w_in":
            recv = _sum_slots_wide(recv, name="sum_w_in_grad").T
        res[n] = _adamw_big(l, w[n], m[n], v[n], recv, res.get(n), summed=(n == "w_in"), name=f"adamw_{n}")
        if chain is None:
            return None
        tied = lax.optimization_barrier((chain, *res[n]))
        res[n] = list(tied[1:])
        return tied[0]

    def sink(l, n, g_, chain):
        g_, chain = lax.optimization_barrier((g_, chain))
        if len(started) >= SCATTERS_IN_FLIGHT:
            chain = consume(chain)
        if n == "w_in":
            g_main, g_ba = g_
            g_ = jnp.concatenate([g_main, g_ba[:w["w_in"].shape[2] * N_DEV - g_main.shape[0]]], axis=0)
            part = g_.reshape(N_DEV, -1, D)
        elif n == "w_up":
            part = g_
        else:
            part = g_.reshape(N_DEV, -1, g_.shape[1])
        started.append((n, l, _sc_scatter(part, name=f"scatter_{n}_{l}")))
        return chain

    loss, grad_x, G, d_final = _local_step(x, target, Ws, w["final_norm_g"], fetch, sink)

    small_names = [n for n in WEIGHTS if n not in BIG]
    partial = []
    for n in small_names:
        if n == "final_norm_g":
            partial.append(d_final)
        else:
            partial.append(jnp.stack([G[l][n].reshape(Ws[l][n].shape) for l in range(L)]))
    partial.append(loss.reshape(1))
    small_gathered = _sc_gather(_pack(partial), name="gather_small_grads")

    out = {k: {} for k in ("grad", "delta", "new_m", "new_v")}
    while started:
        grad_x = consume(grad_x)
    for n in BIG:
        for j, k in enumerate(("grad", "delta", "new_m", "new_v")):
            out[k][n] = res[n][j]

    summed = _unpack(_sum_slots(small_gathered, name="sum_small_grads"), [p_.shape for p_ in partial])
    full = dict(zip(small_names, summed))
    loss = summed[-1][0]
    for n in SMALL_SHARDED:
        width = w[n].shape[-1]
        full[n] = lax.dynamic_slice_in_dim(full[n], me * width, width, axis=2)
    loc_shapes = [w[n].shape for n in small_names]
    g_pack = _pack([full[n] for n in small_names])
    res = _adamw_small(_pack([w[n] for n in small_names]), g_pack, _pack([m[n] for n in small_names]),
                       _pack([v[n] for n in small_names]), name="adamw_small")
    for k, packed in zip(("grad", "delta", "new_m", "new_v"), (g_pack,) + tuple(res)):
        out[k].update(dict(zip(small_names, _unpack(packed, loc_shapes))))
    return loss, grad_x, out


def kernel(x, mix_norm_g, w_in, conv_dw_w, conv_dw_b, conv_ln_g, conv_ln_b, conv_pw_w, conv_pw_b, gdn_conv_w, gdn_a_log, gdn_dt_bias, gdn_norm_g, w_out, ffn_norm_g, w_up, ffn_conv_w, ffn_conv_b, w_down, final_norm_g, loss_target, m_mix_norm_g, m_w_in, m_conv_dw_w, m_conv_dw_b, m_conv_ln_g, m_conv_ln_b, m_conv_pw_w, m_conv_pw_b, m_gdn_conv_w, m_gdn_a_log, m_gdn_dt_bias, m_gdn_norm_g, m_w_out, m_ffn_norm_g, m_w_up, m_ffn_conv_w, m_ffn_conv_b, m_w_down, m_final_norm_g, v_mix_norm_g, v_w_in, v_conv_dw_w, v_conv_dw_b, v_conv_ln_g, v_conv_ln_b, v_conv_pw_w, v_conv_pw_b, v_gdn_conv_w, v_gdn_a_log, v_gdn_dt_bias, v_gdn_norm_g, v_w_out, v_ffn_norm_g, v_w_up, v_ffn_conv_w, v_ffn_conv_b, v_w_down, v_final_norm_g):
    w = dict(zip(WEIGHTS, (mix_norm_g, w_in, conv_dw_w, conv_dw_b, conv_ln_g, conv_ln_b, conv_pw_w, conv_pw_b, gdn_conv_w,
                           gdn_a_log, gdn_dt_bias, gdn_norm_g, w_out, ffn_norm_g, w_up, ffn_conv_w, ffn_conv_b, w_down,
                           final_norm_g)))
    m = dict(zip(WEIGHTS, (m_mix_norm_g, m_w_in, m_conv_dw_w, m_conv_dw_b, m_conv_ln_g, m_conv_ln_b, m_conv_pw_w,
                           m_conv_pw_b, m_gdn_conv_w, m_gdn_a_log, m_gdn_dt_bias, m_gdn_norm_g, m_w_out, m_ffn_norm_g,
                           m_w_up, m_ffn_conv_w, m_ffn_conv_b, m_w_down, m_final_norm_g)))
    v = dict(zip(WEIGHTS, (v_mix_norm_g, v_w_in, v_conv_dw_w, v_conv_dw_b, v_conv_ln_g, v_conv_ln_b, v_conv_pw_w,
                           v_conv_pw_b, v_gdn_conv_w, v_gdn_a_log, v_gdn_dt_bias, v_gdn_norm_g, v_w_out, v_ffn_norm_g,
                           v_w_up, v_ffn_conv_w, v_ffn_conv_b, v_w_down, v_final_norm_g)))
    loss, grad_x, out = _train_step(x, loss_target, w, m, v)
    return (loss, grad_x, *[out["grad"][n] for n in WEIGHTS], *[out["delta"][n] for n in WEIGHTS],
            *[out["new_m"][n] for n in WEIGHTS], *[out["new_v"][n] for n in WEIGHTS])
```

```python
import functools
import math

import jax
import jax.numpy as jnp
from jax import lax
from jax.experimental import pallas as pl
from jax.experimental.pallas import tpu as pltpu
from jax.experimental.pallas import tpu_sc as plsc

F32 = jnp.float32
BF16 = jnp.bfloat16
MESH = pl.DeviceIdType.MESH

EPS = 1e-6
LANES = 128
CHUNK = 128
NEAR_BLOCK = 32
CONV_K = 31
SHORT_CONV_K = 4
FFN_CONV_K = 3
N_DEV = 8
VMEM_LIMIT = 56 * 1024 * 1024

ADAM_LR = 0.001
ADAM_B1 = 0.9
ADAM_B2 = 0.999
ADAM_EPS = 1e-08
ADAM_WD = 0.01
ADAM_STEP = 10


def _cparams(sem):
    return pltpu.CompilerParams(dimension_semantics=sem, vmem_limit_bytes=VMEM_LIMIT)


def _sig(x):
    return 1.0 / (1.0 + jnp.exp(-x))


def _silu(x):
    return x * _sig(x)


def _dsilu(x):
    s = _sig(x)
    return s * (1.0 + x * (1.0 - s))


def _softplus(x):
    return jnp.maximum(x, 0.0) + jnp.log1p(jnp.exp(-jnp.abs(x)))


def _dot(a, b):
    return jnp.dot(a, b, preferred_element_type=F32)


def _dot_nt(a, b):
    return lax.dot_general(a, b, (((1,), (1,)), ((), ())), preferred_element_type=F32)


def _dot_tn(a, b):
    return lax.dot_general(a, b, (((0,), (0,)), ((), ())), preferred_element_type=F32)


def _bf(x):
    return x.astype(BF16)


_NN = (((1,), (0,)), ((), ()))
_TN = (((0,), (0,)), ((), ()))


def _split2(x):
    hi = _bf(x)
    return hi, _bf(x - hi.astype(F32))


def _dot_x3(a, b, dn=_NN):
    ah, al = _split2(a)
    bh, bl = _split2(b)
    f = lambda p, q: lax.dot_general(p, q, dn, preferred_element_type=F32)
    return f(ah, bh) + (f(al, bh) + f(ah, bl))


def _dot_mask(mask, x, dn=_NN):
    mb = _bf(mask)
    hi, lo = _split2(x)
    lo2 = _bf(x - hi.astype(F32) - lo.astype(F32))
    f = lambda q: lax.dot_general(mb, q, dn, preferred_element_type=F32)
    return f(hi) + (f(lo) + f(lo2))


def _shift_down(u, s):
    if s == 0:
        return u
    row = lax.broadcasted_iota(jnp.int32, u.shape, 0)
    return jnp.where(row >= s, pltpu.roll(u, s, 0), 0.0)


def _shift_up(u, s):
    if s == 0:
        return u
    n = u.shape[0]
    row = lax.broadcasted_iota(jnp.int32, u.shape, 0)
    return jnp.where(row < n - s, pltpu.roll(u, n - s, 0), 0.0)


def _conv_fwd(u, w_ref, K):
    acc = None
    for k in range(K):
        term = w_ref[k:k + 1, :] * _shift_down(u, K - 1 - k)
        acc = term if acc is None else acc + term
    return acc


def _conv_bwd_in(do, w_ref, K):
    acc = None
    for k in range(K):
        term = w_ref[k:k + 1, :] * _shift_up(do, K - 1 - k)
        acc = term if acc is None else acc + term
    return acc


def _conv_bwd_w(do, u, dw_ref, K, first):
    for k in range(K):
        row = jnp.sum(do * _shift_down(u, K - 1 - k), axis=0, keepdims=True)
        _acc_row(dw_ref, k, row, first)


def _acc_row(ref, k, row, first):
    @pl.when(first)
    def _():
        ref[k:k + 1, :] = row

    @pl.when(jnp.logical_not(first))
    def _():
        ref[k:k + 1, :] += row


def _logical(arr):
    if arr.ndim == 2:
        return arr.shape
    return (arr.shape[1], arr.shape[0] * arr.shape[2])


def _tile(dim, pref, *col_widths):
    if dim % LANES:
        assert not col_widths
        return dim
    t = (min(pref, dim) // LANES) * LANES
    while t > LANES and (dim % t or any(c % t for c in col_widths)):
        t -= LANES
    assert dim % t == 0 and all(c % t == 0 for c in col_widths), (dim, pref, col_widths)
    return t


def _spec(shape, rt, ct, rfn, cfn):
    if len(shape) == 2:
        return pl.BlockSpec((rt, ct), lambda i, j, k: (rfn(i, j, k), cfn(i, j, k)))
    per = shape[2] // ct
    return pl.BlockSpec((None, rt, ct),
                        lambda i, j, k: (cfn(i, j, k) // per, rfn(i, j, k), cfn(i, j, k) % per))


def _mm(a, b, *, name, ta=False, tb=False, out_dtype=F32, out_blocks=None, bias=None, res=None, b_rows=None,
        tm=1024, tn=1024, tk=2048):
    ra, ca = _logical(a)
    rb, cb = _logical(b)
    if b_rows is not None:
        assert b.ndim == 2 and b_rows <= rb
        rb = b_rows
    M, K = (ca, ra) if ta else (ra, ca)
    N, K2 = (rb, cb) if tb else (cb, rb)
    assert K == K2, (a.shape, b.shape, ta, tb)
    out_shape = (M, N) if out_blocks is None else (out_blocks, M, N // out_blocks)
    cw = lambda arr: [arr.shape[2]] if arr.ndim == 3 else []
    m_c = cw(a) if ta else []
    k_c = (cw(a) if not ta else []) + (cw(b) if tb else [])
    n_c = (cw(b) if not tb else []) + ([out_shape[2]] if out_blocks else []) + (cw(res) if res is not None else [])
    tm, tn, tk = _tile(M, tm, *m_c), _tile(N, tn, *n_c), _tile(K, tk, *k_c)
    nk = K // tk
    im, jn, kk = (lambda i, j, k: i), (lambda i, j, k: j), (lambda i, j, k: k)
    in_specs = [
        _spec(a.shape, tk, tm, kk, im) if ta else _spec(a.shape, tm, tk, im, kk),
        _spec(b.shape, tn, tk, jn, kk) if tb else _spec(b.shape, tk, tn, kk, jn),
    ]
    args = [a, b]
    if bias is not None:
        in_specs.append(pl.BlockSpec((1, tn), lambda i, j, k: (0, j)))
        args.append(bias.reshape(1, N).astype(F32))
    if res is not None:
        in_specs.append(_spec(res.shape, tm, tn, im, jn))
        args.append(res)
    dn = (((0 if ta else 1,), (1 if tb else 0,)), ((), ()))

    def body(*refs):
        a_ref, b_ref = refs[0], refs[1]
        pos = 2
        bias_ref = res_ref = None
        if bias is not None:
            bias_ref = refs[pos]
            pos += 1
        if res is not None:
            res_ref = refs[pos]
            pos += 1
        o_ref = refs[pos]
        k = pl.program_id(2)
        part = lax.dot_general(_bf(a_ref[...]), _bf(b_ref[...]), dn, preferred_element_type=F32)

        def finish(r):
            if bias_ref is not None:
                r = r + bias_ref[...]
            if res_ref is not None:
                r = r + res_ref[...].astype(F32)
            o_ref[...] = r.astype(out_dtype)

        if nk == 1:
            finish(part)
            return
        acc_ref = refs[pos + 1]

        @pl.when(k == 0)
        def _():
            acc_ref[...] = part

        @pl.when((k > 0) & (k < nk - 1))
        def _():
            acc_ref[...] += part

        @pl.when(k == nk - 1)
        def _():
            finish(acc_ref[...] + part)

    return pl.pallas_call(
        body, name=name,
        grid=(M // tm, N // tn, nk),
        in_specs=in_specs,
        out_specs=_spec(out_shape, tm, tn, im, jn),
        out_shape=jax.ShapeDtypeStruct(out_shape, out_dtype),
        scratch_shapes=[pltpu.VMEM((tm, tn), F32)] if nk > 1 else [],
        compiler_params=_cparams(("parallel", "parallel", "arbitrary")),
    )(*args)


def _rms_fwd(x, g, *, name, tr=512):
    T, D = x.shape
    tr = min(tr, T)

    def body(x_ref, g_ref, h_ref):
        xv = x_ref[...]
        r = lax.rsqrt(jnp.mean(xv * xv, axis=-1, keepdims=True) + EPS)
        h_ref[...] = (xv * r * g_ref[...]).astype(BF16)

    return pl.pallas_call(
        body, name=name, grid=(T // tr,),
        in_specs=[pl.BlockSpec((tr, D), lambda i: (i, 0)), pl.BlockSpec((1, D), lambda i: (0, 0))],
        out_specs=pl.BlockSpec((tr, D), lambda i: (i, 0)),
        out_shape=jax.ShapeDtypeStruct((T, D), BF16),
        compiler_params=_cparams(("parallel",)),
    )(x, g.reshape(1, D))


def _rms_bwd(x, g, dh, dres, *, name, tr=512):
    T, D = x.shape
    tr = min(tr, T)

    def body(x_ref, g_ref, dh_ref, dres_ref, dx_ref, dxb_ref, dg_ref):
        i = pl.program_id(0)
        xv = x_ref[...]
        dy = dh_ref[...].astype(F32)
        r = lax.rsqrt(jnp.mean(xv * xv, axis=-1, keepdims=True) + EPS)
        dyg = dy * g_ref[...]
        dot = jnp.mean(dyg * xv, axis=-1, keepdims=True)
        dx = dres_ref[...] + r * dyg - xv * (r * r * r) * dot
        dx_ref[...] = dx
        dxb_ref[...] = dx.astype(BF16)
        part = jnp.sum(dy * xv * r, axis=0, keepdims=True)
        _acc_row(dg_ref, 0, part, i == 0)

    row = pl.BlockSpec((tr, D), lambda i: (i, 0))
    vec = pl.BlockSpec((1, D), lambda i: (0, 0))
    return pl.pallas_call(
        body, name=name, grid=(T // tr,),
        in_specs=[row, vec, row, row],
        out_specs=[row, row, vec],
        out_shape=[jax.ShapeDtypeStruct((T, D), F32), jax.ShapeDtypeStruct((T, D), BF16),
                   jax.ShapeDtypeStruct((1, D), F32)],
        compiler_params=_cparams(("arbitrary",)),
    )(x, g.reshape(1, D), dh, dres)


def _loss_head(x, g, target, *, name, tr=512):
    T, D = x.shape
    tr = min(tr, T)

    def body(x_ref, g_ref, t_ref, loss_ref, dx_ref, dxb_ref, dg_ref):
        i = pl.program_id(0)
        xv = x_ref[...]
        gv = g_ref[...]
        r = lax.rsqrt(jnp.mean(xv * xv, axis=-1, keepdims=True) + EPS)
        y = xv * r * gv
        err = y - t_ref[...]
        lpart = 0.5 * jnp.sum(jnp.mean(err * err, axis=-1, keepdims=True), axis=0, keepdims=True)
        dy = err * (1.0 / D)
        dyg = dy * gv
        dot = jnp.mean(dyg * xv, axis=-1, keepdims=True)
        dx = r * dyg - xv * (r * r * r) * dot
        dx_ref[...] = dx
        dxb_ref[...] = dx.astype(BF16)
        _acc_row(dg_ref, 0, jnp.sum(dy * xv * r, axis=0, keepdims=True), i == 0)
        _acc_row(loss_ref, 0, jnp.broadcast_to(lpart, (1, LANES)), i == 0)

    row = pl.BlockSpec((tr, D), lambda i: (i, 0))
    return pl.pallas_call(
        body, name=name, grid=(T // tr,),
        in_specs=[row, pl.BlockSpec((1, D), lambda i: (0, 0)), row],
        out_specs=[pl.BlockSpec((1, LANES), lambda i: (0, 0)), row, row, pl.BlockSpec((1, D), lambda i: (0, 0))],
        out_shape=[jax.ShapeDtypeStruct((1, LANES), F32), jax.ShapeDtypeStruct((T, D), F32),
                   jax.ShapeDtypeStruct((T, D), BF16), jax.ShapeDtypeStruct((1, D), F32)],
        compiler_params=_cparams(("arbitrary",)),
    )(x, g.reshape(1, D), target)


HALO = 32
SUBLANES = 8


def _conv_dw_blocks(do, u, dw_ref, K, first, u_s, do_p):
    S, C = u.shape
    zeros = jnp.zeros((HALO, C), F32)
    for r in range(SUBLANES):
        u_s[r, 0:HALO, :] = zeros
        u_s[r, HALO:HALO + S, :] = _shift_down(u, r)
    do_p[0:HALO, :] = zeros
    do_p[HALO:HALO + S, :] = do
    do_p[HALO + S:2 * HALO + S, :] = zeros
    n_a = (K - 1) // SUBLANES + 1

    def block(i, accs):
        i0 = pl.multiple_of(i * SUBLANES, SUBLANES)
        us = [u_s[r, pl.ds(i0, SUBLANES), :] for r in range(SUBLANES)]
        ds = [do_p[pl.ds(i0 + SUBLANES * a, SUBLANES), :] for a in range(n_a)]
        out = list(accs)
        for a in range(n_a):
            for r in range(SUBLANES):
                s = SUBLANES * a + r
                if s < K:
                    out[K - 1 - s] = out[K - 1 - s] + ds[a] * us[r]
        return tuple(out)

    accs = lax.fori_loop(HALO // SUBLANES, (S + HALO) // SUBLANES, block, (jnp.zeros((SUBLANES, C), F32),) * K)
    for k in range(K):
        _acc_row(dw_ref, k, jnp.sum(accs[k], axis=0, keepdims=True), first)


def _conf_norm(u1, lg_ref, lb_ref):
    mu = jnp.mean(u1, axis=-1, keepdims=True)
    xc = u1 - mu
    r = lax.rsqrt(jnp.mean(xc * xc, axis=-1, keepdims=True) + EPS)
    n = xc * r
    return r, n, n * lg_ref[...] + lb_ref[...]


def _conf_fwd(p, dw_w, dw_b, ln_g, ln_b, *, Bl, S, CC, name):
    G = CC // LANES

    def body(av_ref, ag_ref, w_ref, b_ref, lg_ref, lb_ref, o_ref, u1_ref):
        u0 = av_ref[...] * _sig(ag_ref[...])
        u1 = _conv_fwd(u0, w_ref, CONV_K) + b_ref[...]
        u1_ref[...] = u1
        _, _, u2 = _conf_norm(u1, lg_ref, lb_ref)
        o_ref[...] = _silu(u2).astype(BF16)

    vec = pl.BlockSpec((1, LANES), lambda b, j: (0, j))
    seq = pl.BlockSpec((S, LANES), lambda b, j: (b, j))
    return pl.pallas_call(
        body, name=name, grid=(Bl, G),
        in_specs=[seq, pl.BlockSpec((S, LANES), lambda b, j: (b, G + j)),
                  pl.BlockSpec((CONV_K, LANES), lambda b, j: (0, j)), vec, vec, vec],
        out_specs=[seq, seq],
        out_shape=[jax.ShapeDtypeStruct((Bl * S, CC), BF16), jax.ShapeDtypeStruct((Bl * S, CC), F32)],
        compiler_params=_cparams(("parallel", "parallel")),
    )(p, p, dw_w, dw_b.reshape(1, CC), ln_g.reshape(1, CC), ln_b.reshape(1, CC))


def _conf_bwd(p, u1, dw_w, ln_g, ln_b, du3, *, Bl, S, CC, name):
    G = CC // LANES

    def body(av_ref, ag_ref, u1_ref, w_ref, lg_ref, lb_ref, du3_ref,
             dav_ref, dag_ref, dw_ref, db_ref, dlg_ref, dlb_ref, u_s, do_p):
        first = pl.program_id(1) == 0
        av = av_ref[...]
        sg = _sig(ag_ref[...])
        r, n, u2 = _conf_norm(u1_ref[...], lg_ref, lb_ref)
        du2 = du3_ref[...] * _dsilu(u2)
        _acc_row(dlg_ref, 0, jnp.sum(du2 * n, axis=0, keepdims=True), first)
        _acc_row(dlb_ref, 0, jnp.sum(du2, axis=0, keepdims=True), first)
        dn = du2 * lg_ref[...]
        du1 = r * (dn - jnp.mean(dn, axis=-1, keepdims=True) - n * jnp.mean(dn * n, axis=-1, keepdims=True))
        _acc_row(db_ref, 0, jnp.sum(du1, axis=0, keepdims=True), first)
        _conv_dw_blocks(du1, av * sg, dw_ref, CONV_K, first, u_s, do_p)
        du0 = _conv_bwd_in(du1, w_ref, CONV_K)
        dav_ref[...] = (du0 * sg).astype(BF16)
        dag_ref[...] = (du0 * av * sg * (1.0 - sg)).astype(BF16)

    vec = pl.BlockSpec((1, LANES), lambda j, b: (0, j))
    seq = pl.BlockSpec((S, LANES), lambda j, b: (b, j))
    return pl.pallas_call(
        body, name=name, grid=(G, Bl),
        in_specs=[seq, pl.BlockSpec((S, LANES), lambda j, b: (b, G + j)), seq,
                  pl.BlockSpec((CONV_K, LANES), lambda j, b: (0, j)), vec, vec, seq],
        out_specs=[seq, seq, pl.BlockSpec((CONV_K, LANES), lambda j, b: (0, j)), vec, vec, vec],
        out_shape=[jax.ShapeDtypeStruct((Bl * S, CC), BF16), jax.ShapeDtypeStruct((Bl * S, CC), BF16),
                   jax.ShapeDtypeStruct((CONV_K, CC), F32), jax.ShapeDtypeStruct((1, CC), F32),
                   jax.ShapeDtypeStruct((1, CC), F32), jax.ShapeDtypeStruct((1, CC), F32)],
        scratch_shapes=[pltpu.VMEM((SUBLANES, S + HALO, LANES), F32), pltpu.VMEM((S + 2 * HALO, LANES), F32)],
        compiler_params=_cparams(("parallel", "arbitrary")),
    )(p, p, u1, dw_w, ln_g.reshape(1, CC), ln_b.reshape(1, CC), du3)


def _gdn_pre_fwd(p, conv_w, *, Bl, S, CC, KW, VW, name):
    NQK = 2 * KW // LANES
    NB = NQK + VW // LANES
    off = 2 * CC // LANES

    def body(x_ref, w_ref, o_ref):
        j = pl.program_id(1)
        s = _silu(_conv_fwd(x_ref[...], w_ref, SHORT_CONV_K))
        r = lax.rsqrt(jnp.sum(s * s, axis=-1, keepdims=True) + EPS)
        o_ref[...] = jnp.where(j < NQK, s * r, s)

    return pl.pallas_call(
        body, name=name, grid=(Bl, NB),
        in_specs=[pl.BlockSpec((S, LANES), lambda b, j: (b, off + j)),
                  pl.BlockSpec((SHORT_CONV_K, LANES), lambda b, j: (0, j))],
        out_specs=pl.BlockSpec((S, LANES), lambda b, j: (b, j)),
        out_shape=jax.ShapeDtypeStruct((Bl * S, NB * LANES), F32),
        compiler_params=_cparams(("parallel", "parallel")),
    )(p, conv_w)


def _gdn_pre_bwd(p, conv_w, dq, dk, dv, *, Bl, S, CC, KW, VW, name):
    HQ = KW // LANES
    H = VW // LANES
    NQK = 2 * HQ
    NB = NQK + H
    off = 2 * CC // LANES

    def body(x_ref, w_ref, dq_ref, dk_ref, dv_ref, dx_ref, dw_ref):
        j = pl.program_id(0)
        first = pl.program_id(1) == 0
        xv = x_ref[...]
        c = _conv_fwd(xv, w_ref, SHORT_CONV_K)
        s = _silu(c)
        r = lax.rsqrt(jnp.sum(s * s, axis=-1, keepdims=True) + EPS)
        dy = jnp.where(j < HQ, dq_ref[...], jnp.where(j < NQK, dk_ref[...], dv_ref[...]))
        ds_norm = r * dy - s * (r * r * r) * jnp.sum(s * dy, axis=-1, keepdims=True)
        ds = jnp.where(j < NQK, ds_norm, dy)
        dc = ds * _dsilu(c)
        _conv_bwd_w(dc, xv, dw_ref, SHORT_CONV_K, first)
        dx_ref[...] = _conv_bwd_in(dc, w_ref, SHORT_CONV_K).astype(BF16)

    return pl.pallas_call(
        body, name=name, grid=(NB, Bl),
        in_specs=[pl.BlockSpec((S, LANES), lambda j, b: (b, off + j)),
                  pl.BlockSpec((SHORT_CONV_K, LANES), lambda j, b: (0, j)),
                  pl.BlockSpec((S, LANES), lambda j, b: (b, jnp.minimum(j, HQ - 1))),
                  pl.BlockSpec((S, LANES), lambda j, b: (b, jnp.clip(j - HQ, 0, HQ - 1))),
                  pl.BlockSpec((S, LANES), lambda j, b: (b, jnp.clip(j - NQK, 0, H - 1)))],
        out_specs=[pl.BlockSpec((S, LANES), lambda j, b: (b, j)),
                   pl.BlockSpec((SHORT_CONV_K, LANES), lambda j, b: (0, j))],
        out_shape=[jax.ShapeDtypeStruct((Bl * S, NB * LANES), BF16),
                   jax.ShapeDtypeStruct((SHORT_CONV_K, NB * LANES), F32)],
        compiler_params=_cparams(("parallel", "arbitrary")),
    )(p, conv_w, dq, dk, dv)


def _split3(x):
    hi, lo = _split2(x)
    return hi, lo, _bf(x - hi.astype(F32) - lo.astype(F32))


def _lane_replicate(parts, h):
    row = lax.broadcasted_iota(jnp.int32, (LANES, LANES), 0)
    E = jnp.where(row == h, 1.0, 0.0).astype(BF16)
    return _dot(parts[0], E) + (_dot(parts[1], E) + _dot(parts[2], E))


def _gdn_gate_fwd(pba, a_log, dt_bias, *, Bl, S, H, name):
    def body(alog_ref, dtb_ref, x_ref, g_ref, beta_ref):
        parts = _split3(x_ref[...])
        for h in range(H):
            b_raw = _lane_replicate(parts, h)
            a_raw = _lane_replicate(parts, H + h)
            beta_ref[h] = _sig(b_raw)
            ea = jnp.exp(jnp.zeros((1, LANES), F32) + alog_ref[h])
            g_ref[h] = -ea * _softplus(a_raw + dtb_ref[h])

    smem = pl.BlockSpec(memory_space=pltpu.SMEM)
    rep = pl.BlockSpec((H, S, LANES), lambda b: (0, b, 0))
    return pl.pallas_call(
        body, name=name, grid=(Bl,),
        in_specs=[smem, smem, pl.BlockSpec((S, LANES), lambda b: (b, 0))],
        out_specs=[rep, rep],
        out_shape=[jax.ShapeDtypeStruct((H, Bl * S, LANES), F32)] * 2,
        compiler_params=_cparams(("parallel",)),
    )(a_log, dt_bias, pba)


def _gdn_gate_bwd(pba, a_log, dt_bias, dg, dbeta, *, Bl, S, H, name):
    HP = 8 * ((H + 7) // 8)

    def body(alog_ref, dtb_ref, x_ref, dg_ref, dbeta_ref, dx_ref, dalog_ref, ddtb_ref):
        first = pl.program_id(0) == 0
        parts = _split3(x_ref[...])
        lane = lax.broadcasted_iota(jnp.int32, (S, LANES), 1)
        acc = jnp.zeros((S, LANES), F32)

        @pl.when(first)
        def _():
            dalog_ref[...] = jnp.zeros_like(dalog_ref)
            ddtb_ref[...] = jnp.zeros_like(ddtb_ref)

        for h in range(H):
            b_raw = _lane_replicate(parts, h)
            a_raw = _lane_replicate(parts, H + h)
            beta = _sig(b_raw)
            db_raw = dbeta_ref[h] * beta * (1.0 - beta)
            z = a_raw + dtb_ref[h]
            ea = jnp.exp(jnp.zeros((1, LANES), F32) + alog_ref[h])
            dgv = dg_ref[h]
            da_raw = dgv * (-ea) * _sig(z)
            g = -ea * _softplus(z)
            dalog_ref[h:h + 1, :] += jnp.sum(dgv * g, axis=0, keepdims=True)
            ddtb_ref[h:h + 1, :] += jnp.sum(da_raw, axis=0, keepdims=True)
            acc = acc + jnp.where(lane == h, db_raw, 0.0) + jnp.where(lane == H + h, da_raw, 0.0)
        dx_ref[...] = acc.astype(BF16)

    smem = pl.BlockSpec(memory_space=pltpu.SMEM)
    rep = pl.BlockSpec((H, S, LANES), lambda b: (0, b, 0))
    small = pl.BlockSpec((HP, LANES), lambda b: (0, 0))
    return pl.pallas_call(
        body, name=name, grid=(Bl,),
        in_specs=[smem, smem, pl.BlockSpec((S, LANES), lambda b: (b, 0)), rep, rep],
        out_specs=[pl.BlockSpec((S, LANES), lambda b: (b, 0)), small, small],
        out_shape=[jax.ShapeDtypeStruct((Bl * S, LANES), BF16),
                   jax.ShapeDtypeStruct((HP, LANES), F32), jax.ShapeDtypeStruct((HP, LANES), F32)],
        compiler_params=_cparams(("arbitrary",)),
    )(a_log, dt_bias, pba, dg, dbeta)


def _tri_masks():
    ri = lax.broadcasted_iota(jnp.int32, (CHUNK, CHUNK), 0)
    ci = lax.broadcasted_iota(jnp.int32, (CHUNK, CHUNK), 1)
    return ri >= ci, ri > ci, ri == CHUNK - 1


def _tri_inv(L):
    ri = lax.broadcasted_iota(jnp.int32, (CHUNK, CHUNK), 0)
    ci = lax.broadcasted_iota(jnp.int32, (CHUNK, CHUNK), 1)
    T = jnp.where(ri == ci, 1.0, 0.0) - jnp.where((ri >> 1) == (ci >> 1), L, 0.0)
    for lv in range(2, int(math.log2(CHUNK)) + 1):
        O = jnp.where(((ri >> lv) == (ci >> lv)) & ((ri >> (lv - 1)) != (ci >> (lv - 1))), L, 0.0)
        if (1 << lv) <= NEAR_BLOCK:
            T = T - _dot_x3(T, _dot_x3(O, T))
        else:
            Tb = _bf(T)
            T = T - _dot(Tb, _bf(_dot(_bf(O), Tb)))
    return T


def _chunk_local(q, k, v, beta, g):
    ge, gt, last = _tri_masks()
    gam = _dot_mask(ge, g)
    D = jnp.where(ge, jnp.exp(jnp.where(ge, gam - gam.T, 0.0)), 0.0)
    kb = k * beta
    vb = v * beta
    M = _dot_nt(_bf(kb), _bf(k))
    L = jnp.where(gt, M * D, 0.0)
    eg = jnp.exp(gam)
    kbg = kb * eg
    P = _dot_nt(_bf(q), _bf(k))
    QK = jnp.where(ge, P * D, 0.0)
    gl = jnp.sum(jnp.where(last, gam, 0.0), axis=0, keepdims=True)
    el = jnp.exp(gl - gam)
    return dict(ge=ge, gt=gt, last=last, gam=gam, D=D, kb=kb, vb=vb, L=L, eg=eg, kbg=kbg, QK=QK, gl=gl,
                el=el, kd=k * el, qg=q * eg)


def _rowsum(x):
    return jnp.sum(x, axis=-1, keepdims=True)


def _chunk_bwd(q, k, v, beta, g, S, T, u, w, do, dS2):
    c = _chunk_local(q, k, v, beta, g)
    ge, gt, last = c["ge"], c["gt"], c["last"]
    Sb = _bf(S)
    vn = u - _dot(w, Sb)
    dob, vnb, dS2b = _bf(do), _bf(vn), _bf(dS2)
    e_last = jnp.exp(c["gl"])
    dqg = _dot_nt(dob, Sb)
    dS = _dot_tn(_bf(c["qg"]), dob)
    dQK = jnp.where(ge, _dot_nt(dob, vnb), 0.0)
    dvn = _dot_tn(_bf(c["QK"]), dob)
    dS = dS + dS2 * e_last
    de_last = jnp.sum(jnp.sum(dS2 * S, axis=0, keepdims=True), axis=1, keepdims=True)
    dkd = _dot_nt(vnb, dS2b)
    dvn = dvn + _dot(_bf(c["kd"]), dS2b)
    dvnb = _bf(dvn)
    dw = -_dot_nt(dvnb, Sb)
    dS = dS - _dot_tn(w, dvnb)
    dsol = _dot_x3(T, jnp.concatenate([dvn, dw], axis=1), _TN)
    dvb, dkbg = dsol[:, :LANES], dsol[:, LANES:]
    dA = -(_dot_nt(_bf(dvb), _bf(u)) + _dot_nt(_bf(dkbg), w))
    dL = jnp.where(gt, dA, 0.0)
    dM = dL * c["D"]
    dP = dQK * c["D"]
    E = dL * c["L"] + dQK * c["QK"]
    kbf = _bf(k)
    dkb = _dot(_bf(dM), kbf) + dkbg * c["eg"]
    dk = _dot_tn(_bf(dM), _bf(c["kb"])) + _dot_tn(_bf(dP), _bf(q)) + dkd * c["el"] + dkb * beta
    dq = _dot(_bf(dP), kbf) + dqg * c["eg"]
    s_kd = _rowsum(dkd * c["kd"])
    dgam = (_rowsum(E) - _rowsum(E.T) + _rowsum(dqg * c["qg"]) - s_kd + _rowsum(dkbg * c["kbg"]))
    dgl = jnp.sum(s_kd, axis=0, keepdims=True) + de_last * e_last
    dgam_rep = jnp.broadcast_to(dgam, (CHUNK, LANES)) + jnp.where(last, jnp.broadcast_to(dgl, (CHUNK, LANES)), 0.0)
    dg_rep = _dot_mask(ge, dgam_rep, _TN)
    dbeta = _rowsum(dkb * k) + _rowsum(dvb * v)
    dv = dvb * beta
    return dq, dk, dv, jnp.broadcast_to(dbeta, (CHUNK, LANES)), dg_rep, dS


def _gdn_core_fwd(qkv, g, beta, *, Bl, S, KW, VW, name):
    HQ = KW // LANES
    H = VW // LANES
    NC = S // CHUNK
    scale = float(LANES) ** -0.5

    PAIR = 2 if NC % 2 == 0 else 1

    def body(q_ref, k_ref, v_ref, g_ref, beta_ref, o_ref, st_ref, t_ref, u_s, w_s,
             qk_s, qg_s, kd_s, el_s):
        def local(n2, carry):
            for half in range(PAIR):
                n = n2 * PAIR + half
                rows = pl.ds(pl.multiple_of(n * CHUNK, CHUNK), CHUNK)
                q = q_ref[rows, :] * scale
                k = k_ref[rows, :]
                for e in range(2):
                    c = _chunk_local(q, k, v_ref[rows, e * LANES:(e + 1) * LANES], beta_ref[e, rows, :],
                                     g_ref[e, rows, :])
                    T = _tri_inv(c["L"])
                    t_ref[e, rows, :] = T
                    uw = _dot_x3(T, jnp.concatenate([c["vb"], c["kbg"]], axis=1))
                    u_s[e, rows, :] = uw[:, :LANES]
                    w_s[e, rows, :] = _bf(uw[:, LANES:])
                    qk_s[e, rows, :] = _bf(c["QK"])
                    qg_s[e, rows, :] = _bf(c["qg"])
                    kd_s[e, rows, :] = _bf(c["kd"])
                    el_s[e, pl.ds(pl.multiple_of(n * 8, 8), 8), :] = jnp.broadcast_to(jnp.exp(c["gl"]), (8, LANES))
            return carry

        lax.fori_loop(0, NC // PAIR, local, 0)

        def scan(n, states):
            rows = pl.ds(pl.multiple_of(n * CHUNK, CHUNK), CHUNK)
            out = []
            for e in range(2):
                S_in = states[e]
                st_ref[e, n] = S_in
                Sb = _bf(S_in)
                vn = u_s[e, rows, :] - _dot(w_s[e, rows, :], Sb)
                vnb = _bf(vn)
                o_ref[rows, e * LANES:(e + 1) * LANES] = _dot(qg_s[e, rows, :], Sb) + _dot(qk_s[e, rows, :], vnb)
                e_last = el_s[e, pl.ds(pl.multiple_of(n * 8, 8), 1), :]
                out.append(S_in * e_last + _dot_tn(kd_s[e, rows, :], vnb))
            return tuple(out)

        z = jnp.zeros((LANES, LANES), F32)
        lax.fori_loop(0, NC, scan, (z, z))

    rep = pl.BlockSpec((2, S, LANES), lambda b, h: (h, b, 0))
    return pl.pallas_call(
        body, name=name, grid=(Bl, HQ),
        in_specs=[pl.BlockSpec((S, LANES), lambda b, h: (b, h)),
                  pl.BlockSpec((S, LANES), lambda b, h: (b, HQ + h)),
                  pl.BlockSpec((S, 2 * LANES), lambda b, h: (b, HQ + h)), rep, rep],
        out_specs=[pl.BlockSpec((S, 2 * LANES), lambda b, h: (b, h)),
                   pl.BlockSpec((None, 2, NC, LANES, LANES), lambda b, h: (b, h, 0, 0, 0)), rep, rep, rep],
        out_shape=[jax.ShapeDtypeStruct((Bl * S, VW), F32),
                   jax.ShapeDtypeStruct((Bl, H, NC, LANES, LANES), F32),
                   jax.ShapeDtypeStruct((H, Bl * S, LANES), F32),
                   jax.ShapeDtypeStruct((H, Bl * S, LANES), F32),
                   jax.ShapeDtypeStruct((H, Bl * S, LANES), BF16)],
        scratch_shapes=[pltpu.VMEM((2, S, LANES), BF16)] * 3 + [pltpu.VMEM((2, NC * 8, LANES), F32)],
        compiler_params=_cparams(("parallel", "parallel")),
    )(qkv, qkv, qkv, g, beta)


def _gdn_core_bwd(qkv, g, beta, states, tinv, u, w, do, *, Bl, S, KW, VW, name):
    HQ = KW // LANES
    H = VW // LANES
    NC = S // CHUNK
    scale = float(LANES) ** -0.5

    def body(q_ref, k_ref, v_ref, g_ref, beta_ref, st_ref, t_ref, u_ref, w_ref, do_ref,
             dq_ref, dk_ref, dv_ref, dg_ref, dbeta_ref):
        def step(i, dstates):
            n = NC - 1 - i
            rows = pl.ds(pl.multiple_of(n * CHUNK, CHUNK), CHUNK)
            q = q_ref[rows, :] * scale
            k = k_ref[rows, :]
            out = []
            dq_sum = dk_sum = None
            for e in range(2):
                cols = slice(e * LANES, (e + 1) * LANES)
                dq, dk, dv, dbeta, dg, dS = _chunk_bwd(q, k, v_ref[rows, cols], beta_ref[e, rows, :],
                                                       g_ref[e, rows, :], st_ref[e, n], t_ref[e, rows, :],
                                                       u_ref[e, rows, :], w_ref[e, rows, :],
                                                       do_ref[rows, cols], dstates[e])
                dv_ref[rows, cols] = dv
                dg_ref[e, rows, :] = dg
                dbeta_ref[e, rows, :] = dbeta
                dq_sum = dq if dq_sum is None else dq_sum + dq
                dk_sum = dk if dk_sum is None else dk_sum + dk
                out.append(dS)
            dq_ref[rows, :] = dq_sum * scale
            dk_ref[rows, :] = dk_sum
            return tuple(out)

        z = jnp.zeros((LANES, LANES), F32)
        lax.fori_loop(0, NC, step, (z, z))

    rep = pl.BlockSpec((2, S, LANES), lambda b, h: (h, b, 0))
    seq = pl.BlockSpec((S, LANES), lambda b, h: (b, h))
    seq2 = pl.BlockSpec((S, 2 * LANES), lambda b, h: (b, h))
    return pl.pallas_call(
        body, name=name, grid=(Bl, HQ),
        in_specs=[seq, pl.BlockSpec((S, LANES), lambda b, h: (b, HQ + h)),
                  pl.BlockSpec((S, 2 * LANES), lambda b, h: (b, HQ + h)), rep, rep,
                  pl.BlockSpec((None, 2, NC, LANES, LANES), lambda b, h: (b, h, 0, 0, 0)), rep, rep, rep, seq2],
        out_specs=[seq, seq, seq2, rep, rep],
        out_shape=[jax.ShapeDtypeStruct((Bl * S, KW), F32), jax.ShapeDtypeStruct((Bl * S, KW), F32),
                   jax.ShapeDtypeStruct((Bl * S, VW), F32),
                   jax.ShapeDtypeStruct((H, Bl * S, LANES), F32), jax.ShapeDtypeStruct((H, Bl * S, LANES), F32)],
        compiler_params=_cparams(("parallel", "parallel")),
    )(qkv, qkv, qkv, g, beta, states, tinv, u, w, do)


def _gdn_out_fwd(o, p, norm_g, out_a, *, CC, VW, name, tr=256):
    T = o.shape[0]
    tr = min(tr, T)
    H = VW // LANES
    zoff = p.shape[1] // VW - 1

    def body(o_ref, z_ref, ng_ref, a_ref, mix_ref):
        mix_ref[:, :CC] = a_ref[...]
        for h in range(H):
            cols = slice(h * LANES, (h + 1) * LANES)
            ov = o_ref[:, cols]
            r = lax.rsqrt(jnp.mean(ov * ov, axis=-1, keepdims=True) + EPS)
            mix_ref[:, CC + h * LANES:CC + (h + 1) * LANES] = (ov * r * ng_ref[...] * _silu(z_ref[:, cols])).astype(BF16)

    return pl.pallas_call(
        body, name=name, grid=(T // tr,),
        in_specs=[pl.BlockSpec((tr, VW), lambda i: (i, 0)), pl.BlockSpec((tr, VW), lambda i: (i, zoff)),
                  pl.BlockSpec((1, LANES), lambda i: (0, 0)), pl.BlockSpec((tr, CC), lambda i: (i, 0))],
        out_specs=pl.BlockSpec((tr, CC + VW), lambda i: (i, 0)),
        out_shape=jax.ShapeDtypeStruct((T, CC + VW), BF16),
        compiler_params=_cparams(("parallel",)),
    )(o, p, norm_g.reshape(1, LANES), out_a)


def _gdn_out_bwd(o, p, norm_g, dmix, *, CC, VW, name, tr=256):
    T = o.shape[0]
    tr = min(tr, T)
    H = VW // LANES
    zoff = p.shape[1] // VW - 1

    def body(o_ref, z_ref, ng_ref, dmix_ref, do_ref, dz_ref, da_ref, dng_ref, dpb_ref):
        first = pl.program_id(0) == 0
        da = dmix_ref[:, :CC]
        da_ref[...] = da.astype(BF16)
        _acc_row(dpb_ref, 0, jnp.sum(da, axis=0, keepdims=True), first)
        ng = ng_ref[...]
        dng = jnp.zeros((1, LANES), F32)
        for h in range(H):
            cols = slice(h * LANES, (h + 1) * LANES)
            ov = o_ref[:, cols]
            zv = z_ref[:, cols]
            dout = dmix_ref[:, CC + h * LANES:CC + (h + 1) * LANES]
            r = lax.rsqrt(jnp.mean(ov * ov, axis=-1, keepdims=True) + EPS)
            on = ov * r * ng
            don = dout * _silu(zv)
            dz_ref[:, cols] = (dout * on * _dsilu(zv)).astype(BF16)
            dng = dng + jnp.sum(don * ov * r, axis=0, keepdims=True)
            dong = don * ng
            do_ref[:, cols] = r * dong - ov * (r * r * r) * jnp.mean(dong * ov, axis=-1, keepdims=True)
        _acc_row(dng_ref, 0, dng, first)

    return pl.pallas_call(
        body, name=name, grid=(T // tr,),
        in_specs=[pl.BlockSpec((tr, VW), lambda i: (i, 0)), pl.BlockSpec((tr, VW), lambda i: (i, zoff)),
                  pl.BlockSpec((1, LANES), lambda i: (0, 0)), pl.BlockSpec((tr, CC + VW), lambda i: (i, 0))],
        out_specs=[pl.BlockSpec((tr, VW), lambda i: (i, 0)), pl.BlockSpec((tr, VW), lambda i: (i, 0)),
                   pl.BlockSpec((tr, CC), lambda i: (i, 0)), pl.BlockSpec((1, LANES), lambda i: (0, 0)),
                   pl.BlockSpec((1, CC), lambda i: (0, 0))],
        out_shape=[jax.ShapeDtypeStruct((T, VW), F32), jax.ShapeDtypeStruct((T, VW), BF16),
                   jax.ShapeDtypeStruct((T, CC), BF16), jax.ShapeDtypeStruct((1, LANES), F32),
                   jax.ShapeDtypeStruct((1, CC), F32)],
        compiler_params=_cparams(("arbitrary",)),
    )(o, p, norm_g.reshape(1, LANES), dmix)


FFN_CW = 256


def _ffn_act_fwd(gu, conv_w, conv_b, *, Bl, S, name):
    FF = gu.shape[2]
    cw = min(FFN_CW, FF)

    def body(g_ref, u_ref, w_ref, b_ref, a_ref):
        gc = _conv_fwd(g_ref[...], w_ref, FFN_CONV_K) + b_ref[...]
        a_ref[...] = (_silu(gc) * u_ref[...]).astype(BF16)

    return pl.pallas_call(
        body, name=name, grid=(Bl, FF // cw),
        in_specs=[pl.BlockSpec((None, S, cw), lambda b, j: (0, b, j)),
                  pl.BlockSpec((None, S, cw), lambda b, j: (1, b, j)),
                  pl.BlockSpec((FFN_CONV_K, cw), lambda b, j: (0, j)),
                  pl.BlockSpec((1, cw), lambda b, j: (0, j))],
        out_specs=pl.BlockSpec((S, cw), lambda b, j: (b, j)),
        out_shape=jax.ShapeDtypeStruct((Bl * S, FF), BF16),
        compiler_params=_cparams(("parallel", "parallel")),
    )(gu, gu, conv_w, conv_b.reshape(1, FF))


def _ffn_act_bwd(gu, conv_w, conv_b, da, *, Bl, S, name):
    FF = gu.shape[2]
    cw = min(FFN_CW, FF)

    def body(g_ref, u_ref, w_ref, b_ref, da_ref, dgu_ref, dw_ref, db_ref):
        first = pl.program_id(1) == 0
        gate = g_ref[...]
        gc = _conv_fwd(gate, w_ref, FFN_CONV_K) + b_ref[...]
        dav = da_ref[...]
        dgu_ref[1] = (dav * _silu(gc)).astype(BF16)
        dgc = dav * u_ref[...] * _dsilu(gc)
        _acc_row(db_ref, 0, jnp.sum(dgc, axis=0, keepdims=True), first)
        _conv_bwd_w(dgc, gate, dw_ref, FFN_CONV_K, first)
        dgu_ref[0] = _conv_bwd_in(dgc, w_ref, FFN_CONV_K).astype(BF16)

    return pl.pallas_call(
        body, name=name, grid=(FF // cw, Bl),
        in_specs=[pl.BlockSpec((None, S, cw), lambda j, b: (0, b, j)),
                  pl.BlockSpec((None, S, cw), lambda j, b: (1, b, j)),
                  pl.BlockSpec((FFN_CONV_K, cw), lambda j, b: (0, j)),
                  pl.BlockSpec((1, cw), lambda j, b: (0, j)),
                  pl.BlockSpec((S, cw), lambda j, b: (b, j))],
        out_specs=[pl.BlockSpec((2, S, cw), lambda j, b: (0, b, j)),
                   pl.BlockSpec((FFN_CONV_K, cw), lambda j, b: (0, j)),
                   pl.BlockSpec((1, cw), lambda j, b: (0, j))],
        out_shape=[jax.ShapeDtypeStruct((2, Bl * S, FF), BF16),
                   jax.ShapeDtypeStruct((FFN_CONV_K, FF), F32), jax.ShapeDtypeStruct((1, FF), F32)],
        compiler_params=_cparams(("parallel", "arbitrary")),
    )(gu, gu, conv_w, conv_b.reshape(1, FF), da)


def _layer_dims(W):
    CC = W["conv_pw_b"].shape[0]
    VW = W["mix_norm_g"].shape[0] - CC
    KW = (W["gdn_conv_w"].shape[1] - VW) // 2
    return CC, KW, VW


def _layer_fwd(l, x, W, Bl, S, fetch):
    CC, KW, VW = _layer_dims(W)
    H = VW // LANES
    w_in_t, w_in_ba = fetch(l, "w_in", x)
    n_main = (w_in_t.shape[0] // LANES) * LANES
    h1 = _rms_fwd(x, W["mix_norm_g"], name="rms1_fwd")
    p = _mm(h1, w_in_t, tb=True, b_rows=n_main, name="mm_in")
    pba = _mm(h1, w_in_ba, tb=True, name="mm_in_ba")
    u3, u1 = _conf_fwd(p, W["conv_dw_w"], W["conv_dw_b"], W["conv_ln_g"], W["conv_ln_b"], Bl=Bl, S=S, CC=CC,
                       name="conf_fwd")
    conv_pw_w = fetch(l, "conv_pw_w", u3)
    out_a = _mm(u3, conv_pw_w, bias=W["conv_pw_b"], out_dtype=BF16, name="mm_pw")
    qkv = _gdn_pre_fwd(p, W["gdn_conv_w"], Bl=Bl, S=S, CC=CC, KW=KW, VW=VW, name="gdn_pre_fwd")
    g, beta = _gdn_gate_fwd(pba, W["gdn_a_log"], W["gdn_dt_bias"], Bl=Bl, S=S, H=H, name="gdn_gate_fwd")
    o, states, tinv, gdn_u, gdn_w = _gdn_core_fwd(qkv, g, beta, Bl=Bl, S=S, KW=KW, VW=VW, name="gdn_core_fwd")
    mix = _gdn_out_fwd(o, p, W["gdn_norm_g"], out_a, CC=CC, VW=VW, name="gdn_out_fwd")
    w_out = fetch(l, "w_out", mix)
    x1 = _mm(mix, w_out, res=x, name="mm_out")
    h2 = _rms_fwd(x1, W["ffn_norm_g"], name="rms2_fwd")
    w_up = fetch(l, "w_up", h2)
    gu = _mm(h2, w_up, out_blocks=2, tn=w_up.shape[2], name="mm_up")
    a = _ffn_act_fwd(gu, W["ffn_conv_w"], W["ffn_conv_b"], Bl=Bl, S=S, name="ffn_act_fwd")
    w_down = fetch(l, "w_down", a)
    x2 = _mm(a, w_down, res=x1, name="mm_down")
    saved = dict(x=x, h1=h1, p=p, pba=pba, u1=u1, u3=u3, qkv=qkv, g=g, beta=beta, o=o, states=states, tinv=tinv,
                 gdn_u=gdn_u, gdn_w=gdn_w, mix=mix, x1=x1, h2=h2, gu=gu, a=a, w_in_t=w_in_t, w_in_ba=w_in_ba, conv_pw_w=conv_pw_w,
                 w_out=w_out, w_up=w_up, w_down=w_down)
    return x2, saved


def _layer_bwd(l, dx2, dx2b, W, A, Bl, S, sink):
    CC, KW, VW = _layer_dims(W)
    H = VW // LANES
    G = {}
    upw = A["w_up"].shape[2]
    da = _mm(dx2b, A["w_down"], tb=True, tn=upw, name="mm_down_dx")
    da = sink(l, "w_down", _mm(A["a"], dx2b, ta=True, out_dtype=BF16, tm=upw, name="mm_down_dw"), da)
    dgu, G["ffn_conv_w"], G["ffn_conv_b"] = _ffn_act_bwd(A["gu"], W["ffn_conv_w"], W["ffn_conv_b"], da,
                                                         Bl=Bl, S=S, name="ffn_act_bwd")
    dh2 = _mm(dgu, A["w_up"], tb=True, tk=upw, tn=2048, name="mm_up_dx")
    dh2 = sink(l, "w_up", _mm(A["h2"], dgu, ta=True, out_dtype=BF16, out_blocks=N_DEV, tn=upw, name="mm_up_dw"),
               dh2)
    dx1, dx1b, G["ffn_norm_g"] = _rms_bwd(A["x1"], W["ffn_norm_g"], dh2, dx2, name="rms2_bwd")
    dmix = _mm(dx1b, A["w_out"], tb=True, name="mm_out_dx")
    dmix = sink(l, "w_out", _mm(A["mix"], dx1b, ta=True, out_dtype=BF16, name="mm_out_dw"), dmix)
    do, dz, dout_a, G["gdn_norm_g"], G["conv_pw_b"] = _gdn_out_bwd(A["o"], A["p"], W["gdn_norm_g"], dmix,
                                                                   CC=CC, VW=VW, name="gdn_out_bwd")
    dq, dk, dv, dg, dbeta = _gdn_core_bwd(A["qkv"], A["g"], A["beta"], A["states"], A["tinv"], A["gdn_u"],
                                          A["gdn_w"], do, Bl=Bl, S=S, KW=KW, VW=VW, name="gdn_core_bwd")
    dpba, dalog, ddtb = _gdn_gate_bwd(A["pba"], W["gdn_a_log"], W["gdn_dt_bias"], dg, dbeta, Bl=Bl, S=S, H=H,
                                      name="gdn_gate_bwd")
    G["gdn_a_log"], G["gdn_dt_bias"] = dalog[:H, 0], ddtb[:H, 0]
    dqkv, G["gdn_conv_w"] = _gdn_pre_bwd(A["p"], W["gdn_conv_w"], dq, dk, dv, Bl=Bl, S=S, CC=CC, KW=KW, VW=VW,
                                         name="gdn_pre_bwd")
    du3 = _mm(dout_a, A["conv_pw_w"], tb=True, name="mm_pw_dx")
    du3 = sink(l, "conv_pw_w", _mm(A["u3"], dout_a, ta=True, out_dtype=BF16, name="mm_pw_dw"), du3)
    dav, dag, G["conv_dw_w"], G["conv_dw_b"], G["conv_ln_g"], G["conv_ln_b"] = _conf_bwd(
        A["p"], A["u1"], W["conv_dw_w"], W["conv_ln_g"], W["conv_ln_b"], du3, Bl=Bl, S=S, CC=CC, name="conf_bwd")
    dp = jnp.concatenate([dav, dag, dqkv, dz], axis=1)
    dp = sink(l, "w_in", (_mm(dp, A["h1"], ta=True, out_dtype=BF16, name="mm_in_dw"),
                          _mm(dpba, A["h1"], ta=True, out_dtype=BF16, name="mm_in_ba_dw")), dp)
    dh1 = _mm(dpba, A["w_in_ba"], name="mm_in_ba_dx")
    dh1 = _mm(dp, A["w_in_t"], b_rows=dp.shape[1], res=dh1, name="mm_in_dx")
    dx, dxb, G["mix_norm_g"] = _rms_bwd(A["x"], W["mix_norm_g"], dh1, dx1, name="rms1_bwd")
    return dx, dxb, G


def _local_step(x, target, Ws, final_norm_g, fetch, sink):
    Bl, S, D = x.shape
    xt = x.reshape(Bl * S, D)
    acts = []
    for l, W in enumerate(Ws):
        xt, A = _layer_fwd(l, xt, W, Bl, S, fetch)
        acts.append(A)
    loss, dx, dxb, dgf = _loss_head(xt, final_norm_g, target.reshape(Bl * S, D), name="loss_head")
    grads = [None] * len(Ws)
    for l in reversed(range(len(Ws))):
        dx, dxb, grads[l] = _layer_bwd(l, dx, dxb, Ws[l], acts[l], Bl, S, sink)
    return loss[0, 0], dx.reshape(Bl, S, D), grads, dgf.reshape(D)


def _mesh_pos():
    return lax.axis_index("x"), lax.axis_index("y"), lax.axis_index("c")


def _dev_index(px, py, pc):
    return 4 * px + 2 * py + pc


_ANY = pl.BlockSpec(memory_space=pl.ANY)


def _all_gather(arrs, *, name):
    n = len(arrs)

    def body(*refs):
        ins, outs = refs[:n], refs[n:2 * n]
        send_sems, recv_sems, local_sems = refs[2 * n:]
        x, y, c = _mesh_pos()
        me, sibling = (x, y, c), (x, y, 1 - c)
        chips = [(1 - x, y), (x, 1 - y), (1 - x, 1 - y)]

        def copy(a, k, block, to, src=None):
            dst = outs[a].at[_dev_index(*block)]
            return pltpu.make_async_remote_copy(
                src_ref=dst if src is None else src, dst_ref=dst,
                send_sem=send_sems.at[a, k], recv_sem=recv_sems.at[a, k],
                device_id=to, device_id_type=MESH)

        mine = [pltpu.make_async_copy(ins[a], outs[a].at[_dev_index(*me)], local_sems.at[a]) for a in range(n)]
        for cp in mine:
            cp.start()
        first = []
        for a in range(n):
            first.append(copy(a, 0, me, sibling, src=ins[a]))
            first += [copy(a, 1 + j, me, (*chip, c), src=ins[a]) for j, chip in enumerate(chips)]
        for cp in first:
            cp.start()
        passed = []
        for a in range(n):
            for j, chip in enumerate(chips):
                copy(a, 1 + j, (*chip, c), me).wait_recv()
                fwd = copy(a, 4 + j, (*chip, c), sibling)
                fwd.start()
                passed.append(fwd)
        for a in range(n):
            copy(a, 0, sibling, me).wait_recv()
            for j, chip in enumerate(chips):
                copy(a, 4 + j, (*chip, 1 - c), me).wait_recv()
        for cp in first + passed:
            cp.wait_send()
        for cp in mine:
            cp.wait()

    return pl.pallas_call(
        body, name=name,
        in_specs=[_ANY] * n, out_specs=[_ANY] * n,
        out_shape=[jax.ShapeDtypeStruct((N_DEV,) + a.shape, a.dtype) for a in arrs],
        scratch_shapes=[pltpu.SemaphoreType.DMA((n, N_DEV - 1)), pltpu.SemaphoreType.DMA((n, N_DEV - 1)),
                        pltpu.SemaphoreType.DMA((n,))],
    )(*arrs)


def _peers(x, y, c):
    flip = lambda v, f: 1 - v if f else v
    return [(flip(x, p & 4), flip(y, p & 2), flip(c, p & 1)) for p in range(1, N_DEV)]


GATHER_ID, SCATTER_ID = 1, 2
_SEQUENCER = dict(axis_name="sequencer", num_cores=1)


def _handshake(peers):
    barrier = pltpu.get_barrier_semaphore()
    for peer in peers:
        pl.semaphore_signal(barrier, inc=1, device_id=peer, device_id_type=MESH)
    pl.semaphore_wait(barrier, len(peers))


def _sc_gather(src, *, name):
    def body(src_ref, zone_ref, send_sems, recv_sems, local_sem):
        x, y, c = _mesh_pos()
        me, sibling = (x, y, c), (x, y, 1 - c)
        chips = [(1 - x, y), (x, 1 - y), (1 - x, 1 - y)]
        _handshake([sibling] + [(*chip, c) for chip in chips])

        def copy(k, block, to, from_src=False):
            dst = zone_ref.at[_dev_index(*block)]
            return pltpu.make_async_remote_copy(
                src_ref=src_ref if from_src else dst, dst_ref=dst, send_sem=send_sems.at[k], recv_sem=recv_sems.at[k],
                device_id=to, device_id_type=MESH)

        mine = pltpu.make_async_copy(src_ref, zone_ref.at[_dev_index(*me)], local_sem)
        mine.start()
        first = [copy(1 + j, me, (*chip, c), from_src=True) for j, chip in enumerate(chips)]
        first.append(copy(0, me, sibling, from_src=True))
        for cp in first:
            cp.start()
        passed = []
        for j, chip in enumerate(chips):
            copy(1 + j, (*chip, c), me).wait_recv()
            fwd = copy(4 + j, (*chip, c), sibling)
            fwd.start()
            passed.append(fwd)
        copy(0, sibling, me).wait_recv()
        for j, chip in enumerate(chips):
            copy(4 + j, (*chip, 1 - c), me).wait_recv()
        for cp in first + passed:
            cp.wait_send()
        mine.wait()

    return pl.kernel(
        body, name=name,
        out_type=jax.ShapeDtypeStruct((N_DEV,) + src.shape, src.dtype),
        mesh=plsc.ScalarSubcoreMesh(**_SEQUENCER),
        scratch_types=[pltpu.SemaphoreType.DMA((N_DEV - 1,)), pltpu.SemaphoreType.DMA((N_DEV - 1,)),
                       pltpu.SemaphoreType.DMA],
        compiler_params=pltpu.CompilerParams(collective_id=GATHER_ID),
    )(src)


def _sc_scatter(part, *, name):
    def body(src_ref, zone_ref, send_sems, recv_sems, local_sem):
        x, y, c = _mesh_pos()
        me = _dev_index(x, y, c)
        peers = _peers(x, y, c)
        _handshake(peers)
        mine = pltpu.make_async_copy(src_ref.at[me], zone_ref.at[me], local_sem)
        mine.start()
        sends = [pltpu.make_async_remote_copy(
            src_ref=src_ref.at[_dev_index(*peer)], dst_ref=zone_ref.at[me], send_sem=send_sems.at[k],
            recv_sem=recv_sems.at[k], device_id=peer, device_id_type=MESH) for k, peer in enumerate(peers)]
        for cp in sends:
            cp.start()
        for k, peer in enumerate(peers):
            pltpu.make_async_remote_copy(
                src_ref=src_ref.at[me], dst_ref=zone_ref.at[_dev_index(*peer)], send_sem=send_sems.at[k],
                recv_sem=recv_sems.at[k], device_id=peer, device_id_type=MESH).wait_recv()
        for cp in sends:
            cp.wait_send()
        mine.wait()

    return pl.kernel(
        body, name=name,
        out_type=jax.ShapeDtypeStruct(part.shape, part.dtype),
        mesh=plsc.ScalarSubcoreMesh(**_SEQUENCER),
        scratch_types=[pltpu.SemaphoreType.DMA((N_DEV - 1,)), pltpu.SemaphoreType.DMA((N_DEV - 1,)),
                       pltpu.SemaphoreType.DMA],
        compiler_params=pltpu.CompilerParams(collective_id=SCATTER_ID),
    )(part)


def _adamw_math(w, g, m, v):
    m2 = ADAM_B1 * m + (1.0 - ADAM_B1) * g
    v2 = ADAM_B2 * v + (1.0 - ADAM_B2) * (g * g)
    m_hat = m2 / (1.0 - ADAM_B1 ** ADAM_STEP)
    v_hat = v2 / (1.0 - ADAM_B2 ** ADAM_STEP)
    delta = -ADAM_LR * (m_hat / (jnp.sqrt(v_hat) + ADAM_EPS) + ADAM_WD * w)
    return delta, m2, v2


def _adamw_big(l, w, m, v, recv, prev, *, summed=False, name, tr=128):
    L, R, C = w.shape
    tr = next(t for t in range(min(tr, R), 0, -16) if R % t == 0)

    def body(w_ref, m_ref, v_ref, r_ref, *rest):
        g_ref, d_ref, m2_ref, v2_ref = rest[-4:]
        if summed:
            g = r_ref[...]
        else:
            g = r_ref[0].astype(F32)
            for s in range(1, N_DEV):
                g = g + r_ref[s].astype(F32)
        g_ref[...] = g
        d_ref[...], m2_ref[...], v2_ref[...] = _adamw_math(w_ref[...], g, m_ref[...], v_ref[...])

    wspec = pl.BlockSpec((None, tr, C), lambda i: (l, i, 0))
    rspec = pl.BlockSpec((tr, C), lambda i: (i, 0)) if summed else pl.BlockSpec((N_DEV, tr, C), lambda i: (0, i, 0))
    prev = list(prev) if prev is not None else []
    return pl.pallas_call(
        body, name=name, grid=(R // tr,),
        in_specs=[wspec, wspec, wspec, rspec] + [_ANY] * len(prev),
        out_specs=[wspec] * 4,
        out_shape=[jax.ShapeDtypeStruct((L, R, C), F32)] * 4,
        input_output_aliases={4 + j: j for j in range(len(prev))},
        compiler_params=_cparams(("parallel",)),
    )(w, m, v, recv if summed else recv.reshape(N_DEV, R, C), *prev)


def _sum_slots_wide(recv, *, name, tc=512):
    _, R, C = recv.shape
    tc = _tile(C, tc)

    def body(r_ref, o_ref):
        g = r_ref[0].astype(F32)
        for s in range(1, N_DEV):
            g = g + r_ref[s].astype(F32)
        o_ref[...] = g

    return pl.pallas_call(
        body, name=name, grid=(C // tc,),
        in_specs=[pl.BlockSpec((N_DEV, R, tc), lambda j: (0, 0, j))],
        out_specs=pl.BlockSpec((R, tc), lambda j: (0, j)),
        out_shape=jax.ShapeDtypeStruct((R, C), F32),
        compiler_params=_cparams(("parallel",)),
    )(recv)


def _sum_slots(gathered, *, name):
    _, R, C = gathered.shape

    def body(r_ref, o_ref):
        g = r_ref[0]
        for s in range(1, N_DEV):
            g = g + r_ref[s]
        o_ref[...] = g

    return pl.pallas_call(body, name=name, out_shape=jax.ShapeDtypeStruct((R, C), F32))(gathered)


def _adamw_small(w, g, m, v, *, name):
    def body(w_ref, g_ref, m_ref, v_ref, d_ref, m2_ref, v2_ref):
        d_ref[...], m2_ref[...], v2_ref[...] = _adamw_math(w_ref[...], g_ref[...], m_ref[...], v_ref[...])

    return pl.pallas_call(body, name=name, out_shape=[jax.ShapeDtypeStruct(w.shape, F32)] * 3)(w, g, m, v)


def _pack(arrs):
    flat = []
    for a in arrs:
        a = a.reshape(-1).astype(F32)
        flat.append(jnp.pad(a, (0, (-a.shape[0]) % LANES)))
    out = jnp.concatenate(flat)
    out = jnp.pad(out, (0, (-out.shape[0]) % (8 * LANES)))
    return out.reshape(-1, LANES)


def _unpack(packed, shapes):
    flat = packed.reshape(-1)
    out, pos = [], 0
    for s in shapes:
        size = math.prod(s)
        out.append(flat[pos:pos + size].reshape(s))
        pos += size + (-size) % LANES
    return out


BIG = ("w_in", "conv_pw_w", "w_out", "w_up", "w_down")
SMALL_SHARDED = ("conv_dw_w", "gdn_conv_w", "ffn_conv_w")
SMALL_REPLICATED = ("mix_norm_g", "conv_dw_b", "conv_ln_g", "conv_ln_b", "conv_pw_b", "gdn_a_log", "gdn_dt_bias",
                    "gdn_norm_g", "ffn_norm_g", "ffn_conv_b")
WEIGHTS = ("mix_norm_g", "w_in", "conv_dw_w", "conv_dw_b", "conv_ln_g", "conv_ln_b", "conv_pw_w", "conv_pw_b",
           "gdn_conv_w", "gdn_a_log", "gdn_dt_bias", "gdn_norm_g", "w_out", "ffn_norm_g", "w_up", "ffn_conv_w",
           "ffn_conv_b", "w_down", "final_norm_g")


def _train_step(x, target, w, m, v):
    L = w["w_in"].shape[0]
    D = x.shape[-1]
    xi, yi, ci = _mesh_pos()
    me = _dev_index(xi, yi, ci)

    gathered = {}

    def launch(l, after=None):
        for n in BIG:
            src = (w[n][l].T if n == "w_in" else w[n][l]).astype(BF16)
            if after is not None:
                src = lax.optimization_barrier((src, after))[0]
            gathered[n, l] = _sc_gather(src, name=f"gather_{n}_{l}")

    launch(0)
    small_full = {}
    for n, g_ in zip(SMALL_SHARDED, _all_gather([w[n] for n in SMALL_SHARDED], name="all_gather_conv_taps")):
        small_full[n] = jnp.moveaxis(g_, 0, 2).reshape(L, g_.shape[2], N_DEV * g_.shape[3])
    Ws = []
    for l in range(L):
        W = {n: w[n][l] for n in SMALL_REPLICATED}
        W.update({n: small_full[n][l] for n in SMALL_SHARDED})
        Ws.append(W)

    def fetch(l, n, after):
        if n == "conv_pw_w" and l + 1 < L:
            launch(l + 1, after)
        g_ = lax.optimization_barrier((gathered[n, l], after))[0]
        if n == "w_up":
            return g_
        g_ = g_.reshape(g_.shape[0] * g_.shape[1], g_.shape[2])
        if n == "w_in":
            n_main = (g_.shape[0] // LANES) * LANES
            return g_, jnp.pad(g_[n_main:], ((0, LANES - (g_.shape[0] - n_main)), (0, 0)))
        return g_

    started = []
    res = {}
    SCATTERS_IN_FLIGHT = 2

    def consume(chain):
        n, l, recv = started.pop(0)
        if chain is not None:
            recv, chain = lax.optimization_barrier((recv, chain))
        if n == "w_in":
            recv = _sum_slots_wide(recv, name="sum_w_in_grad").T
        res[n] = _adamw_big(l, w[n], m[n], v[n], recv, res.get(n), summed=(n == "w_in"), name=f"adamw_{n}")
        if chain is None:
            return None
        tied = lax.optimization_barrier((chain, *res[n]))
        res[n] = list(tied[1:])
        return tied[0]

    def sink(l, n, g_, chain):
        g_, chain = lax.optimization_barrier((g_, chain))
        if len(started) >= SCATTERS_IN_FLIGHT:
            chain = consume(chain)
        if n == "w_in":
            g_main, g_ba = g_
            g_ = jnp.concatenate([g_main, g_ba[:w["w_in"].shape[2] * N_DEV - g_main.shape[0]]], axis=0)
            part = g_.reshape(N_DEV, -1, D)
        elif n == "w_up":
            part = g_
        else:
            part = g_.reshape(N_DEV, -1, g_.shape[1])
        started.append((n, l, _sc_scatter(part, name=f"scatter_{n}_{l}")))
        return chain

    loss, grad_x, G, d_final = _local_step(x, target, Ws, w["final_norm_g"], fetch, sink)

    small_names = [n for n in WEIGHTS if n not in BIG]
    partial = []
    for n in small_names:
        if n == "final_norm_g":
            partial.append(d_final)
        else:
            partial.append(jnp.stack([G[l][n].reshape(Ws[l][n].shape) for l in range(L)]))
    partial.append(loss.reshape(1))
    packed = _pack(partial)
    if started:
        packed = lax.optimization_barrier((packed, started[-1][2]))[0]
    small_gathered = _sc_gather(packed, name="gather_small_grads")

    out = {k: {} for k in ("grad", "delta", "new_m", "new_v")}
    while started:
        grad_x = consume(grad_x)
    for n in BIG:
        for j, k in enumerate(("grad", "delta", "new_m", "new_v")):
            out[k][n] = res[n][j]

    summed = _unpack(_sum_slots(small_gathered, name="sum_small_grads"), [p_.shape for p_ in partial])
    full = dict(zip(small_names, summed))
    loss = summed[-1][0]
    for n in SMALL_SHARDED:
        width = w[n].shape[-1]
        full[n] = lax.dynamic_slice_in_dim(full[n], me * width, width, axis=2)
    loc_shapes = [w[n].shape for n in small_names]
    g_pack = _pack([full[n] for n in small_names])
    res = _adamw_small(_pack([w[n] for n in small_names]), g_pack, _pack([m[n] for n in small_names]),
                       _pack([v[n] for n in small_names]), name="adamw_small")
    for k, packed in zip(("grad", "delta", "new_m", "new_v"), (g_pack,) + tuple(res)):
        out[k].update(dict(zip(small_names, _unpack(packed, loc_shapes))))
    return loss, grad_x, out


def kernel(x, mix_norm_g, w_in, conv_dw_w, conv_dw_b, conv_ln_g, conv_ln_b, conv_pw_w, conv_pw_b, gdn_conv_w, gdn_a_log, gdn_dt_bias, gdn_norm_g, w_out, ffn_norm_g, w_up, ffn_conv_w, ffn_conv_b, w_down, final_norm_g, loss_target, m_mix_norm_g, m_w_in, m_conv_dw_w, m_conv_dw_b, m_conv_ln_g, m_conv_ln_b, m_conv_pw_w, m_conv_pw_b, m_gdn_conv_w, m_gdn_a_log, m_gdn_dt_bias, m_gdn_norm_g, m_w_out, m_ffn_norm_g, m_w_up, m_ffn_conv_w, m_ffn_conv_b, m_w_down, m_final_norm_g, v_mix_norm_g, v_w_in, v_conv_dw_w, v_conv_dw_b, v_conv_ln_g, v_conv_ln_b, v_conv_pw_w, v_conv_pw_b, v_gdn_conv_w, v_gdn_a_log, v_gdn_dt_bias, v_gdn_norm_g, v_w_out, v_ffn_norm_g, v_w_up, v_ffn_conv_w, v_ffn_conv_b, v_w_down, v_final_norm_g):
    w = dict(zip(WEIGHTS, (mix_norm_g, w_in, conv_dw_w, conv_dw_b, conv_ln_g, conv_ln_b, conv_pw_w, conv_pw_b, gdn_conv_w,
                           gdn_a_log, gdn_dt_bias, gdn_norm_g, w_out, ffn_norm_g, w_up, ffn_conv_w, ffn_conv_b, w_down,
                           final_norm_g)))
    m = dict(zip(WEIGHTS, (m_mix_norm_g, m_w_in, m_conv_dw_w, m_conv_dw_b, m_conv_ln_g, m_conv_ln_b, m_conv_pw_w,
                           m_conv_pw_b, m_gdn_conv_w, m_gdn_a_log, m_gdn_dt_bias, m_gdn_norm_g, m_w_out, m_ffn_norm_g,
                           m_w_up, m_ffn_conv_w, m_ffn_conv_b, m_w_down, m_final_norm_g)))
    v = dict(zip(WEIGHTS, (v_mix_norm_g, v_w_in, v_conv_dw_w, v_conv_dw_b, v_conv_ln_g, v_conv_ln_b, v_conv_pw_w,
                           v_conv_pw_b, v_gdn_conv_w, v_gdn_a_log, v_gdn_dt_bias, v_gdn_norm_g, v_w_out, v_ffn_norm_g,
                           v_w_up, v_ffn_conv_w, v_ffn_conv_b, v_w_down, v_final_norm_g)))
    loss, grad_x, out = _train_step(x, loss_target, w, m, v)
    return (loss, grad_x, *[out["grad"][n] for n in WEIGHTS], *[out["delta"][n] for n in WEIGHTS],
            *[out["new_m"][n] for n in WEIGHTS], *[out["new_v"][n] for n in WEIGHTS])
```

```python
import math

import jax
import jax.numpy as jnp
from jax import lax
from jax.experimental import pallas as pl
from jax.experimental.pallas import tpu as pltpu
from jax.experimental.pallas import tpu_sc as plsc

F32 = jnp.float32
BF16 = jnp.bfloat16
MESH = pl.DeviceIdType.MESH

EPS = 1e-6
LANES = 128
CHUNK = 128
NEAR_BLOCK = 32
CONV_K = 31
SHORT_CONV_K = 4
FFN_CONV_K = 3
N_DEV = 8
VMEM_LIMIT = 56 * 1024 * 1024

ADAM_LR = 0.001
ADAM_B1 = 0.9
ADAM_B2 = 0.999
ADAM_EPS = 1e-08
ADAM_WD = 0.01
ADAM_STEP = 10


def _cparams(sem):
    return pltpu.CompilerParams(dimension_semantics=sem, vmem_limit_bytes=VMEM_LIMIT)


def _sig(x):
    return 1.0 / (1.0 + jnp.exp(-x))


def _silu(x):
    return x * _sig(x)


def _dsilu(x):
    s = _sig(x)
    return s * (1.0 + x * (1.0 - s))


def _softplus(x):
    return jnp.maximum(x, 0.0) + jnp.log1p(jnp.exp(-jnp.abs(x)))


def _dot(a, b):
    return jnp.dot(a, b, preferred_element_type=F32)


def _dot_nt(a, b):
    return lax.dot_general(a, b, (((1,), (1,)), ((), ())), preferred_element_type=F32)


def _dot_tn(a, b):
    return lax.dot_general(a, b, (((0,), (0,)), ((), ())), preferred_element_type=F32)


def _bf(x):
    return x.astype(BF16)


_NN = (((1,), (0,)), ((), ()))
_TN = (((0,), (0,)), ((), ()))


def _split2(x):
    hi = _bf(x)
    return hi, _bf(x - hi.astype(F32))


def _dot_x3(a, b, dn=_NN):
    ah, al = _split2(a)
    bh, bl = _split2(b)
    f = lambda p, q: lax.dot_general(p, q, dn, preferred_element_type=F32)
    return f(ah, bh) + (f(al, bh) + f(ah, bl))


def _dot_mask(mask, x, dn=_NN):
    mb = _bf(mask)
    hi, lo = _split2(x)
    lo2 = _bf(x - hi.astype(F32) - lo.astype(F32))
    f = lambda q: lax.dot_general(mb, q, dn, preferred_element_type=F32)
    return f(hi) + (f(lo) + f(lo2))


def _shift_down(u, s):
    if s == 0:
        return u
    row = lax.broadcasted_iota(jnp.int32, u.shape, 0)
    return jnp.where(row >= s, pltpu.roll(u, s, 0), 0.0)


def _shift_up(u, s):
    if s == 0:
        return u
    n = u.shape[0]
    row = lax.broadcasted_iota(jnp.int32, u.shape, 0)
    return jnp.where(row < n - s, pltpu.roll(u, n - s, 0), 0.0)


def _conv_fwd(u, w_ref, K):
    acc = None
    for k in range(K):
        term = w_ref[k:k + 1, :] * _shift_down(u, K - 1 - k)
        acc = term if acc is None else acc + term
    return acc


def _conv_bwd_in(do, w_ref, K):
    acc = None
    for k in range(K):
        term = w_ref[k:k + 1, :] * _shift_up(do, K - 1 - k)
        acc = term if acc is None else acc + term
    return acc


def _conv_bwd_w(do, u, dw_ref, K, first):
    for k in range(K):
        row = jnp.sum(do * _shift_down(u, K - 1 - k), axis=0, keepdims=True)
        _acc_row(dw_ref, k, row, first)


def _acc_row(ref, k, row, first):
    @pl.when(first)
    def _():
        ref[k:k + 1, :] = row

    @pl.when(jnp.logical_not(first))
    def _():
        ref[k:k + 1, :] += row


def _logical(arr):
    if arr.ndim == 2:
        return arr.shape
    return (arr.shape[1], arr.shape[0] * arr.shape[2])


def _tile(dim, pref, *col_widths):
    if dim % LANES:
        assert not col_widths
        return dim
    t = (min(pref, dim) // LANES) * LANES
    while t > LANES and (dim % t or any(c % t for c in col_widths)):
        t -= LANES
    assert dim % t == 0 and all(c % t == 0 for c in col_widths), (dim, pref, col_widths)
    return t


def _spec(shape, rt, ct, rfn, cfn):
    if len(shape) == 2:
        return pl.BlockSpec((rt, ct), lambda i, j, k: (rfn(i, j, k), cfn(i, j, k)))
    per = shape[2] // ct
    return pl.BlockSpec((None, rt, ct),
                        lambda i, j, k: (cfn(i, j, k) // per, rfn(i, j, k), cfn(i, j, k) % per))


def _mm(a, b, *, name, ta=False, tb=False, out_dtype=F32, out_blocks=None, bias=None, res=None, b_rows=None,
        tm=1024, tn=1024, tk=2816):
    ra, ca = _logical(a)
    rb, cb = _logical(b)
    if b_rows is not None:
        assert b.ndim == 2 and b_rows <= rb
        rb = b_rows
    M, K = (ca, ra) if ta else (ra, ca)
    N, K2 = (rb, cb) if tb else (cb, rb)
    assert K == K2, (a.shape, b.shape, ta, tb)
    out_shape = (M, N) if out_blocks is None else (out_blocks, M, N // out_blocks)
    cw = lambda arr: [arr.shape[2]] if arr.ndim == 3 else []
    m_c = cw(a) if ta else []
    k_c = (cw(a) if not ta else []) + (cw(b) if tb else [])
    n_c = (cw(b) if not tb else []) + ([out_shape[2]] if out_blocks else []) + (cw(res) if res is not None else [])
    tm, tn, tk = _tile(M, tm, *m_c), _tile(N, tn, *n_c), _tile(K, tk, *k_c)
    nk = K // tk
    im, jn, kk = (lambda i, j, k: i), (lambda i, j, k: j), (lambda i, j, k: k)
    in_specs = [
        _spec(a.shape, tk, tm, kk, im) if ta else _spec(a.shape, tm, tk, im, kk),
        _spec(b.shape, tn, tk, jn, kk) if tb else _spec(b.shape, tk, tn, kk, jn),
    ]
    args = [a, b]
    if bias is not None:
        in_specs.append(pl.BlockSpec((1, tn), lambda i, j, k: (0, j)))
        args.append(bias.reshape(1, N).astype(F32))
    if res is not None:
        in_specs.append(_spec(res.shape, tm, tn, im, jn))
        args.append(res)
    dn = (((0 if ta else 1,), (1 if tb else 0,)), ((), ()))

    def body(*refs):
        a_ref, b_ref = refs[0], refs[1]
        pos = 2
        bias_ref = res_ref = None
        if bias is not None:
            bias_ref = refs[pos]
            pos += 1
        if res is not None:
            res_ref = refs[pos]
            pos += 1
        o_ref = refs[pos]
        k = pl.program_id(2)
        part = lax.dot_general(_bf(a_ref[...]), _bf(b_ref[...]), dn, preferred_element_type=F32)

        def finish(r):
            if bias_ref is not None:
                r = r + bias_ref[...]
            if res_ref is not None:
                r = r + res_ref[...].astype(F32)
            o_ref[...] = r.astype(out_dtype)

        if nk == 1:
            finish(part)
            return
        acc_ref = refs[pos + 1]

        @pl.when(k == 0)
        def _():
            acc_ref[...] = part

        @pl.when((k > 0) & (k < nk - 1))
        def _():
            acc_ref[...] += part

        @pl.when(k == nk - 1)
        def _():
            finish(acc_ref[...] + part)

    return pl.pallas_call(
        body, name=name,
        grid=(M // tm, N // tn, nk),
        in_specs=in_specs,
        out_specs=_spec(out_shape, tm, tn, im, jn),
        out_shape=jax.ShapeDtypeStruct(out_shape, out_dtype),
        scratch_shapes=[pltpu.VMEM((tm, tn), F32)] if nk > 1 else [],
        compiler_params=_cparams(("parallel", "parallel", "arbitrary")),
    )(*args)


def _rms_fwd(x, g, *, name, tr=512):
    T, D = x.shape
    tr = min(tr, T)

    def body(x_ref, g_ref, h_ref):
        xv = x_ref[...]
        r = lax.rsqrt(jnp.mean(xv * xv, axis=-1, keepdims=True) + EPS)
        h_ref[...] = (xv * r * g_ref[...]).astype(BF16)

    return pl.pallas_call(
        body, name=name, grid=(T // tr,),
        in_specs=[pl.BlockSpec((tr, D), lambda i: (i, 0)), pl.BlockSpec((1, D), lambda i: (0, 0))],
        out_specs=pl.BlockSpec((tr, D), lambda i: (i, 0)),
        out_shape=jax.ShapeDtypeStruct((T, D), BF16),
        compiler_params=_cparams(("parallel",)),
    )(x, g.reshape(1, D))


def _rms_bwd(x, g, dh, dres, *, name, tr=512):
    T, D = x.shape
    tr = min(tr, T)

    def body(x_ref, g_ref, dh_ref, dres_ref, dx_ref, dxb_ref, dg_ref):
        i = pl.program_id(0)
        xv = x_ref[...]
        dy = dh_ref[...].astype(F32)
        r = lax.rsqrt(jnp.mean(xv * xv, axis=-1, keepdims=True) + EPS)
        dyg = dy * g_ref[...]
        dot = jnp.mean(dyg * xv, axis=-1, keepdims=True)
        dx = dres_ref[...] + r * dyg - xv * (r * r * r) * dot
        dx_ref[...] = dx
        dxb_ref[...] = dx.astype(BF16)
        part = jnp.sum(dy * xv * r, axis=0, keepdims=True)
        _acc_row(dg_ref, 0, part, i == 0)

    row = pl.BlockSpec((tr, D), lambda i: (i, 0))
    vec = pl.BlockSpec((1, D), lambda i: (0, 0))
    return pl.pallas_call(
        body, name=name, grid=(T // tr,),
        in_specs=[row, vec, row, row],
        out_specs=[row, row, vec],
        out_shape=[jax.ShapeDtypeStruct((T, D), F32), jax.ShapeDtypeStruct((T, D), BF16),
                   jax.ShapeDtypeStruct((1, D), F32)],
        compiler_params=_cparams(("arbitrary",)),
    )(x, g.reshape(1, D), dh, dres)


def _loss_head(x, g, target, *, name, tr=512):
    T, D = x.shape
    tr = min(tr, T)

    def body(x_ref, g_ref, t_ref, loss_ref, dx_ref, dxb_ref, dg_ref):
        i = pl.program_id(0)
        xv = x_ref[...]
        gv = g_ref[...]
        r = lax.rsqrt(jnp.mean(xv * xv, axis=-1, keepdims=True) + EPS)
        y = xv * r * gv
        err = y - t_ref[...]
        lpart = 0.5 * jnp.sum(jnp.mean(err * err, axis=-1, keepdims=True), axis=0, keepdims=True)
        dy = err * (1.0 / D)
        dyg = dy * gv
        dot = jnp.mean(dyg * xv, axis=-1, keepdims=True)
        dx = r * dyg - xv * (r * r * r) * dot
        dx_ref[...] = dx
        dxb_ref[...] = dx.astype(BF16)
        _acc_row(dg_ref, 0, jnp.sum(dy * xv * r, axis=0, keepdims=True), i == 0)
        _acc_row(loss_ref, 0, jnp.broadcast_to(lpart, (1, LANES)), i == 0)

    row = pl.BlockSpec((tr, D), lambda i: (i, 0))
    return pl.pallas_call(
        body, name=name, grid=(T // tr,),
        in_specs=[row, pl.BlockSpec((1, D), lambda i: (0, 0)), row],
        out_specs=[pl.BlockSpec((1, LANES), lambda i: (0, 0)), row, row, pl.BlockSpec((1, D), lambda i: (0, 0))],
        out_shape=[jax.ShapeDtypeStruct((1, LANES), F32), jax.ShapeDtypeStruct((T, D), F32),
                   jax.ShapeDtypeStruct((T, D), BF16), jax.ShapeDtypeStruct((1, D), F32)],
        compiler_params=_cparams(("arbitrary",)),
    )(x, g.reshape(1, D), target)


HALO = 32
SUBLANES = 8


def _conv_dw_blocks(do, u, dw_ref, K, first, u_s, do_p):
    S, C = u.shape
    zeros = jnp.zeros((HALO, C), F32)
    for r in range(SUBLANES):
        u_s[r, 0:HALO, :] = zeros
        u_s[r, HALO:HALO + S, :] = _shift_down(u, r)
    do_p[0:HALO, :] = zeros
    do_p[HALO:HALO + S, :] = do
    do_p[HALO + S:2 * HALO + S, :] = zeros
    n_a = (K - 1) // SUBLANES + 1

    def block(i, accs):
        i0 = pl.multiple_of(i * SUBLANES, SUBLANES)
        us = [u_s[r, pl.ds(i0, SUBLANES), :] for r in range(SUBLANES)]
        ds = [do_p[pl.ds(i0 + SUBLANES * a, SUBLANES), :] for a in range(n_a)]
        out = list(accs)
        for a in range(n_a):
            for r in range(SUBLANES):
                s = SUBLANES * a + r
                if s < K:
                    out[K - 1 - s] = out[K - 1 - s] + ds[a] * us[r]
        return tuple(out)

    accs = lax.fori_loop(HALO // SUBLANES, (S + HALO) // SUBLANES, block, (jnp.zeros((SUBLANES, C), F32),) * K)
    for k in range(K):
        _acc_row(dw_ref, k, jnp.sum(accs[k], axis=0, keepdims=True), first)


def _conf_norm(u1, lg_ref, lb_ref):
    mu = jnp.mean(u1, axis=-1, keepdims=True)
    xc = u1 - mu
    r = lax.rsqrt(jnp.mean(xc * xc, axis=-1, keepdims=True) + EPS)
    n = xc * r
    return r, n, n * lg_ref[...] + lb_ref[...]


def _conf_fwd(p, dw_w, dw_b, ln_g, ln_b, *, Bl, S, CC, name):
    G = CC // LANES

    def body(av_ref, ag_ref, w_ref, b_ref, lg_ref, lb_ref, o_ref, u1_ref):
        u0 = av_ref[...] * _sig(ag_ref[...])
        u1 = _conv_fwd(u0, w_ref, CONV_K) + b_ref[...]
        u1_ref[...] = u1
        _, _, u2 = _conf_norm(u1, lg_ref, lb_ref)
        o_ref[...] = _silu(u2).astype(BF16)

    vec = pl.BlockSpec((1, LANES), lambda b, j: (0, j))
    seq = pl.BlockSpec((S, LANES), lambda b, j: (b, j))
    return pl.pallas_call(
        body, name=name, grid=(Bl, G),
        in_specs=[seq, pl.BlockSpec((S, LANES), lambda b, j: (b, G + j)),
                  pl.BlockSpec((CONV_K, LANES), lambda b, j: (0, j)), vec, vec, vec],
        out_specs=[seq, seq],
        out_shape=[jax.ShapeDtypeStruct((Bl * S, CC), BF16), jax.ShapeDtypeStruct((Bl * S, CC), F32)],
        compiler_params=_cparams(("parallel", "parallel")),
    )(p, p, dw_w, dw_b.reshape(1, CC), ln_g.reshape(1, CC), ln_b.reshape(1, CC))


def _conf_bwd(p, u1, dw_w, ln_g, ln_b, du3, *, Bl, S, CC, name):
    G = CC // LANES

    def body(av_ref, ag_ref, u1_ref, w_ref, lg_ref, lb_ref, du3_ref,
             dav_ref, dag_ref, dw_ref, db_ref, dlg_ref, dlb_ref, u_s, do_p):
        first = pl.program_id(1) == 0
        av = av_ref[...]
        sg = _sig(ag_ref[...])
        r, n, u2 = _conf_norm(u1_ref[...], lg_ref, lb_ref)
        du2 = du3_ref[...] * _dsilu(u2)
        _acc_row(dlg_ref, 0, jnp.sum(du2 * n, axis=0, keepdims=True), first)
        _acc_row(dlb_ref, 0, jnp.sum(du2, axis=0, keepdims=True), first)
        dn = du2 * lg_ref[...]
        du1 = r * (dn - jnp.mean(dn, axis=-1, keepdims=True) - n * jnp.mean(dn * n, axis=-1, keepdims=True))
        _acc_row(db_ref, 0, jnp.sum(du1, axis=0, keepdims=True), first)
        _conv_dw_blocks(du1, av * sg, dw_ref, CONV_K, first, u_s, do_p)
        du0 = _conv_bwd_in(du1, w_ref, CONV_K)
        dav_ref[...] = (du0 * sg).astype(BF16)
        dag_ref[...] = (du0 * av * sg * (1.0 - sg)).astype(BF16)

    vec = pl.BlockSpec((1, LANES), lambda j, b: (0, j))
    seq = pl.BlockSpec((S, LANES), lambda j, b: (b, j))
    return pl.pallas_call(
        body, name=name, grid=(G, Bl),
        in_specs=[seq, pl.BlockSpec((S, LANES), lambda j, b: (b, G + j)), seq,
                  pl.BlockSpec((CONV_K, LANES), lambda j, b: (0, j)), vec, vec, seq],
        out_specs=[seq, seq, pl.BlockSpec((CONV_K, LANES), lambda j, b: (0, j)), vec, vec, vec],
        out_shape=[jax.ShapeDtypeStruct((Bl * S, CC), BF16), jax.ShapeDtypeStruct((Bl * S, CC), BF16),
                   jax.ShapeDtypeStruct((CONV_K, CC), F32), jax.ShapeDtypeStruct((1, CC), F32),
                   jax.ShapeDtypeStruct((1, CC), F32), jax.ShapeDtypeStruct((1, CC), F32)],
        scratch_shapes=[pltpu.VMEM((SUBLANES, S + HALO, LANES), F32), pltpu.VMEM((S + 2 * HALO, LANES), F32)],
        compiler_params=_cparams(("parallel", "arbitrary")),
    )(p, p, u1, dw_w, ln_g.reshape(1, CC), ln_b.reshape(1, CC), du3)


def _gdn_pre_fwd(p, conv_w, *, Bl, S, CC, KW, VW, name):
    NQK = 2 * KW // LANES
    NB = NQK + VW // LANES
    off = 2 * CC // LANES

    def body(x_ref, w_ref, o_ref):
        j = pl.program_id(1)
        s = _silu(_conv_fwd(x_ref[...], w_ref, SHORT_CONV_K))
        r = lax.rsqrt(jnp.sum(s * s, axis=-1, keepdims=True) + EPS)
        o_ref[...] = jnp.where(j < NQK, s * r, s)

    return pl.pallas_call(
        body, name=name, grid=(Bl, NB),
        in_specs=[pl.BlockSpec((S, LANES), lambda b, j: (b, off + j)),
                  pl.BlockSpec((SHORT_CONV_K, LANES), lambda b, j: (0, j))],
        out_specs=pl.BlockSpec((S, LANES), lambda b, j: (b, j)),
        out_shape=jax.ShapeDtypeStruct((Bl * S, NB * LANES), F32),
        compiler_params=_cparams(("parallel", "parallel")),
    )(p, conv_w)


def _gdn_pre_bwd(p, conv_w, dq, dk, dv, *, Bl, S, CC, KW, VW, name):
    HQ = KW // LANES
    H = VW // LANES
    NQK = 2 * HQ
    NB = NQK + H
    off = 2 * CC // LANES

    def body(x_ref, w_ref, dq_ref, dk_ref, dv_ref, dx_ref, dw_ref):
        j = pl.program_id(0)
        first = pl.program_id(1) == 0
        xv = x_ref[...]
        c = _conv_fwd(xv, w_ref, SHORT_CONV_K)
        s = _silu(c)
        r = lax.rsqrt(jnp.sum(s * s, axis=-1, keepdims=True) + EPS)
        dy = jnp.where(j < HQ, dq_ref[...], jnp.where(j < NQK, dk_ref[...], dv_ref[...]))
        ds_norm = r * dy - s * (r * r * r) * jnp.sum(s * dy, axis=-1, keepdims=True)
        ds = jnp.where(j < NQK, ds_norm, dy)
        dc = ds * _dsilu(c)
        _conv_bwd_w(dc, xv, dw_ref, SHORT_CONV_K, first)
        dx_ref[...] = _conv_bwd_in(dc, w_ref, SHORT_CONV_K).astype(BF16)

    return pl.pallas_call(
        body, name=name, grid=(NB, Bl),
        in_specs=[pl.BlockSpec((S, LANES), lambda j, b: (b, off + j)),
                  pl.BlockSpec((SHORT_CONV_K, LANES), lambda j, b: (0, j)),
                  pl.BlockSpec((S, LANES), lambda j, b: (b, jnp.minimum(j, HQ - 1))),
                  pl.BlockSpec((S, LANES), lambda j, b: (b, jnp.clip(j - HQ, 0, HQ - 1))),
                  pl.BlockSpec((S, LANES), lambda j, b: (b, jnp.clip(j - NQK, 0, H - 1)))],
        out_specs=[pl.BlockSpec((S, LANES), lambda j, b: (b, j)),
                   pl.BlockSpec((SHORT_CONV_K, LANES), lambda j, b: (0, j))],
        out_shape=[jax.ShapeDtypeStruct((Bl * S, NB * LANES), BF16),
                   jax.ShapeDtypeStruct((SHORT_CONV_K, NB * LANES), F32)],
        compiler_params=_cparams(("parallel", "arbitrary")),
    )(p, conv_w, dq, dk, dv)


def _split3(x):
    hi, lo = _split2(x)
    return hi, lo, _bf(x - hi.astype(F32) - lo.astype(F32))


def _lane_replicate(parts, h):
    row = lax.broadcasted_iota(jnp.int32, (LANES, LANES), 0)
    E = jnp.where(row == h, 1.0, 0.0).astype(BF16)
    return _dot(parts[0], E) + (_dot(parts[1], E) + _dot(parts[2], E))


def _gdn_gate_fwd(pba, a_log, dt_bias, *, Bl, S, H, name):
    def body(alog_ref, dtb_ref, x_ref, g_ref, beta_ref):
        parts = _split3(x_ref[...])
        for h in range(H):
            b_raw = _lane_replicate(parts, h)
            a_raw = _lane_replicate(parts, H + h)
            beta_ref[h] = _sig(b_raw)
            ea = jnp.exp(jnp.zeros((1, LANES), F32) + alog_ref[h])
            g_ref[h] = -ea * _softplus(a_raw + dtb_ref[h])

    smem = pl.BlockSpec(memory_space=pltpu.SMEM)
    rep = pl.BlockSpec((H, S, LANES), lambda b: (0, b, 0))
    return pl.pallas_call(
        body, name=name, grid=(Bl,),
        in_specs=[smem, smem, pl.BlockSpec((S, LANES), lambda b: (b, 0))],
        out_specs=[rep, rep],
        out_shape=[jax.ShapeDtypeStruct((H, Bl * S, LANES), F32)] * 2,
        compiler_params=_cparams(("parallel",)),
    )(a_log, dt_bias, pba)


def _gdn_gate_bwd(pba, a_log, dt_bias, dg, dbeta, *, Bl, S, H, name):
    HP = 8 * ((H + 7) // 8)

    def body(alog_ref, dtb_ref, x_ref, dg_ref, dbeta_ref, dx_ref, dalog_ref, ddtb_ref):
        first = pl.program_id(0) == 0
        parts = _split3(x_ref[...])
        lane = lax.broadcasted_iota(jnp.int32, (S, LANES), 1)
        acc = jnp.zeros((S, LANES), F32)

        @pl.when(first)
        def _():
            dalog_ref[...] = jnp.zeros_like(dalog_ref)
            ddtb_ref[...] = jnp.zeros_like(ddtb_ref)

        for h in range(H):
            b_raw = _lane_replicate(parts, h)
            a_raw = _lane_replicate(parts, H + h)
            beta = _sig(b_raw)
            db_raw = dbeta_ref[h] * beta * (1.0 - beta)
            z = a_raw + dtb_ref[h]
            ea = jnp.exp(jnp.zeros((1, LANES), F32) + alog_ref[h])
            dgv = dg_ref[h]
            da_raw = dgv * (-ea) * _sig(z)
            g = -ea * _softplus(z)
            dalog_ref[h:h + 1, :] += jnp.sum(dgv * g, axis=0, keepdims=True)
            ddtb_ref[h:h + 1, :] += jnp.sum(da_raw, axis=0, keepdims=True)
            acc = acc + jnp.where(lane == h, db_raw, 0.0) + jnp.where(lane == H + h, da_raw, 0.0)
        dx_ref[...] = acc.astype(BF16)

    smem = pl.BlockSpec(memory_space=pltpu.SMEM)
    rep = pl.BlockSpec((H, S, LANES), lambda b: (0, b, 0))
    small = pl.BlockSpec((HP, LANES), lambda b: (0, 0))
    return pl.pallas_call(
        body, name=name, grid=(Bl,),
        in_specs=[smem, smem, pl.BlockSpec((S, LANES), lambda b: (b, 0)), rep, rep],
        out_specs=[pl.BlockSpec((S, LANES), lambda b: (b, 0)), small, small],
        out_shape=[jax.ShapeDtypeStruct((Bl * S, LANES), BF16),
                   jax.ShapeDtypeStruct((HP, LANES), F32), jax.ShapeDtypeStruct((HP, LANES), F32)],
        compiler_params=_cparams(("arbitrary",)),
    )(a_log, dt_bias, pba, dg, dbeta)


def _tri_masks():
    ri = lax.broadcasted_iota(jnp.int32, (CHUNK, CHUNK), 0)
    ci = lax.broadcasted_iota(jnp.int32, (CHUNK, CHUNK), 1)
    return ri >= ci, ri > ci, ri == CHUNK - 1


def _tri_inv(L):
    ri = lax.broadcasted_iota(jnp.int32, (CHUNK, CHUNK), 0)
    ci = lax.broadcasted_iota(jnp.int32, (CHUNK, CHUNK), 1)
    T = jnp.where(ri == ci, 1.0, 0.0) - jnp.where((ri >> 1) == (ci >> 1), L, 0.0)
    for lv in range(2, int(math.log2(CHUNK)) + 1):
        O = jnp.where(((ri >> lv) == (ci >> lv)) & ((ri >> (lv - 1)) != (ci >> (lv - 1))), L, 0.0)
        if (1 << lv) <= NEAR_BLOCK:
            T = T - _dot_x3(T, _dot_x3(O, T))
        else:
            Tb = _bf(T)
            T = T - _dot(Tb, _bf(_dot(_bf(O), Tb)))
    return T


def _chunk_local(q, k, v, beta, g):
    ge, gt, last = _tri_masks()
    gam = _dot_mask(ge, g)
    D = jnp.where(ge, jnp.exp(jnp.where(ge, gam - gam.T, 0.0)), 0.0)
    kb = k * beta
    vb = v * beta
    M = _dot_nt(_bf(kb), _bf(k))
    L = jnp.where(gt, M * D, 0.0)
    eg = jnp.exp(gam)
    kbg = kb * eg
    P = _dot_nt(_bf(q), _bf(k))
    QK = jnp.where(ge, P * D, 0.0)
    gl = jnp.sum(jnp.where(last, gam, 0.0), axis=0, keepdims=True)
    el = jnp.exp(gl - gam)
    return dict(ge=ge, gt=gt, last=last, gam=gam, D=D, kb=kb, vb=vb, L=L, eg=eg, kbg=kbg, QK=QK, gl=gl,
                el=el, kd=k * el, qg=q * eg)


def _rowsum(x):
    return jnp.sum(x, axis=-1, keepdims=True)


def _chunk_bwd(q, k, v, beta, g, S, T, u, w, do, dS2):
    c = _chunk_local(q, k, v, beta, g)
    ge, gt, last = c["ge"], c["gt"], c["last"]
    Sb = _bf(S)
    vn = u - _dot(w, Sb)
    dob, vnb, dS2b = _bf(do), _bf(vn), _bf(dS2)
    e_last = jnp.exp(c["gl"])
    dqg = _dot_nt(dob, Sb)
    dS = _dot_tn(_bf(c["qg"]), dob)
    dQK = jnp.where(ge, _dot_nt(dob, vnb), 0.0)
    dvn = _dot_tn(_bf(c["QK"]), dob)
    dS = dS + dS2 * e_last
    de_last = jnp.sum(jnp.sum(dS2 * S, axis=0, keepdims=True), axis=1, keepdims=True)
    dkd = _dot_nt(vnb, dS2b)
    dvn = dvn + _dot(_bf(c["kd"]), dS2b)
    dvnb = _bf(dvn)
    dw = -_dot_nt(dvnb, Sb)
    dS = dS - _dot_tn(w, dvnb)
    dsol = _dot_x3(T, jnp.concatenate([dvn, dw], axis=1), _TN)
    dvb, dkbg = dsol[:, :LANES], dsol[:, LANES:]
    dA = -(_dot_nt(_bf(dvb), _bf(u)) + _dot_nt(_bf(dkbg), w))
    dL = jnp.where(gt, dA, 0.0)
    dM = dL * c["D"]
    dP = dQK * c["D"]
    E = dL * c["L"] + dQK * c["QK"]
    kbf = _bf(k)
    dkb = _dot(_bf(dM), kbf) + dkbg * c["eg"]
    dk = _dot_tn(_bf(dM), _bf(c["kb"])) + _dot_tn(_bf(dP), _bf(q)) + dkd * c["el"] + dkb * beta
    dq = _dot(_bf(dP), kbf) + dqg * c["eg"]
    s_kd = _rowsum(dkd * c["kd"])
    dgam = (_rowsum(E) - _rowsum(E.T) + _rowsum(dqg * c["qg"]) - s_kd + _rowsum(dkbg * c["kbg"]))
    dgl = jnp.sum(s_kd, axis=0, keepdims=True) + de_last * e_last
    dgam_rep = jnp.broadcast_to(dgam, (CHUNK, LANES)) + jnp.where(last, jnp.broadcast_to(dgl, (CHUNK, LANES)), 0.0)
    dg_rep = _dot_mask(ge, dgam_rep, _TN)
    dbeta = _rowsum(dkb * k) + _rowsum(dvb * v)
    dv = dvb * beta
    return dq, dk, dv, jnp.broadcast_to(dbeta, (CHUNK, LANES)), dg_rep, dS


def _gdn_core_fwd(qkv, g, beta, *, Bl, S, KW, VW, name):
    HQ = KW // LANES
    H = VW // LANES
    NC = S // CHUNK
    scale = float(LANES) ** -0.5

    PAIR = 2 if NC % 2 == 0 else 1

    def body(q_ref, k_ref, v_ref, g_ref, beta_ref, o_ref, st_ref, t_ref, u_s, w_s,
             qk_s, qg_s, kd_s, el_s):
        def local(n2, carry):
            for half in range(PAIR):
                n = n2 * PAIR + half
                rows = pl.ds(pl.multiple_of(n * CHUNK, CHUNK), CHUNK)
                q = q_ref[rows, :] * scale
                k = k_ref[rows, :]
                for e in range(2):
                    c = _chunk_local(q, k, v_ref[rows, e * LANES:(e + 1) * LANES], beta_ref[e, rows, :],
                                     g_ref[e, rows, :])
                    T = _tri_inv(c["L"])
                    t_ref[e, rows, :] = T
                    uw = _dot_x3(T, jnp.concatenate([c["vb"], c["kbg"]], axis=1))
                    u_s[e, rows, :] = uw[:, :LANES]
                    w_s[e, rows, :] = _bf(uw[:, LANES:])
                    qk_s[e, rows, :] = _bf(c["QK"])
                    qg_s[e, rows, :] = _bf(c["qg"])
                    kd_s[e, rows, :] = _bf(c["kd"])
                    el_s[e, pl.ds(pl.multiple_of(n * 8, 8), 8), :] = jnp.broadcast_to(jnp.exp(c["gl"]), (8, LANES))
            return carry

        lax.fori_loop(0, NC // PAIR, local, 0)

        def scan(n, states):
            rows = pl.ds(pl.multiple_of(n * CHUNK, CHUNK), CHUNK)
            out = []
            for e in range(2):
                S_in = states[e]
                st_ref[e, n] = S_in
                Sb = _bf(S_in)
                vn = u_s[e, rows, :] - _dot(w_s[e, rows, :], Sb)
                vnb = _bf(vn)
                o_ref[rows, e * LANES:(e + 1) * LANES] = _dot(qg_s[e, rows, :], Sb) + _dot(qk_s[e, rows, :], vnb)
                e_last = el_s[e, pl.ds(pl.multiple_of(n * 8, 8), 1), :]
                out.append(S_in * e_last + _dot_tn(kd_s[e, rows, :], vnb))
            return tuple(out)

        z = jnp.zeros((LANES, LANES), F32)
        lax.fori_loop(0, NC, scan, (z, z))

    rep = pl.BlockSpec((2, S, LANES), lambda b, h: (h, b, 0))
    return pl.pallas_call(
        body, name=name, grid=(Bl, HQ),
        in_specs=[pl.BlockSpec((S, LANES), lambda b, h: (b, h)),
                  pl.BlockSpec((S, LANES), lambda b, h: (b, HQ + h)),
                  pl.BlockSpec((S, 2 * LANES), lambda b, h: (b, HQ + h)), rep, rep],
        out_specs=[pl.BlockSpec((S, 2 * LANES), lambda b, h: (b, h)),
                   pl.BlockSpec((None, 2, NC, LANES, LANES), lambda b, h: (b, h, 0, 0, 0)), rep, rep, rep],
        out_shape=[jax.ShapeDtypeStruct((Bl * S, VW), F32),
                   jax.ShapeDtypeStruct((Bl, H, NC, LANES, LANES), F32),
                   jax.ShapeDtypeStruct((H, Bl * S, LANES), F32),
                   jax.ShapeDtypeStruct((H, Bl * S, LANES), F32),
                   jax.ShapeDtypeStruct((H, Bl * S, LANES), BF16)],
        scratch_shapes=[pltpu.VMEM((2, S, LANES), BF16)] * 3 + [pltpu.VMEM((2, NC * 8, LANES), F32)],
        compiler_params=_cparams(("parallel", "parallel")),
    )(qkv, qkv, qkv, g, beta)


def _gdn_core_bwd(qkv, g, beta, states, tinv, u, w, do, *, Bl, S, KW, VW, name):
    HQ = KW // LANES
    H = VW // LANES
    NC = S // CHUNK
    scale = float(LANES) ** -0.5

    def body(q_ref, k_ref, v_ref, g_ref, beta_ref, st_ref, t_ref, u_ref, w_ref, do_ref,
             dq_ref, dk_ref, dv_ref, dg_ref, dbeta_ref):
        def step(i, dstates):
            n = NC - 1 - i
            rows = pl.ds(pl.multiple_of(n * CHUNK, CHUNK), CHUNK)
            q = q_ref[rows, :] * scale
            k = k_ref[rows, :]
            out = []
            dq_sum = dk_sum = None
            for e in range(2):
                cols = slice(e * LANES, (e + 1) * LANES)
                dq, dk, dv, dbeta, dg, dS = _chunk_bwd(q, k, v_ref[rows, cols], beta_ref[e, rows, :],
                                                       g_ref[e, rows, :], st_ref[e, n], t_ref[e, rows, :],
                                                       u_ref[e, rows, :], w_ref[e, rows, :],
                                                       do_ref[rows, cols], dstates[e])
                dv_ref[rows, cols] = dv
                dg_ref[e, rows, :] = dg
                dbeta_ref[e, rows, :] = dbeta
                dq_sum = dq if dq_sum is None else dq_sum + dq
                dk_sum = dk if dk_sum is None else dk_sum + dk
                out.append(dS)
            dq_ref[rows, :] = dq_sum * scale
            dk_ref[rows, :] = dk_sum
            return tuple(out)

        z = jnp.zeros((LANES, LANES), F32)
        lax.fori_loop(0, NC, step, (z, z))

    rep = pl.BlockSpec((2, S, LANES), lambda b, h: (h, b, 0))
    seq = pl.BlockSpec((S, LANES), lambda b, h: (b, h))
    seq2 = pl.BlockSpec((S, 2 * LANES), lambda b, h: (b, h))
    return pl.pallas_call(
        body, name=name, grid=(Bl, HQ),
        in_specs=[seq, pl.BlockSpec((S, LANES), lambda b, h: (b, HQ + h)),
                  pl.BlockSpec((S, 2 * LANES), lambda b, h: (b, HQ + h)), rep, rep,
                  pl.BlockSpec((None, 2, NC, LANES, LANES), lambda b, h: (b, h, 0, 0, 0)), rep, rep, rep, seq2],
        out_specs=[seq, seq, seq2, rep, rep],
        out_shape=[jax.ShapeDtypeStruct((Bl * S, KW), F32), jax.ShapeDtypeStruct((Bl * S, KW), F32),
                   jax.ShapeDtypeStruct((Bl * S, VW), F32),
                   jax.ShapeDtypeStruct((H, Bl * S, LANES), F32), jax.ShapeDtypeStruct((H, Bl * S, LANES), F32)],
        compiler_params=_cparams(("parallel", "parallel")),
    )(qkv, qkv, qkv, g, beta, states, tinv, u, w, do)


def _gdn_out_fwd(o, p, norm_g, out_a, *, CC, VW, name, tr=256):
    T = o.shape[0]
    tr = min(tr, T)
    H = VW // LANES
    zoff = p.shape[1] // VW - 1

    def body(o_ref, z_ref, ng_ref, a_ref, mix_ref):
        mix_ref[:, :CC] = a_ref[...]
        for h in range(H):
            cols = slice(h * LANES, (h + 1) * LANES)
            ov = o_ref[:, cols]
            r = lax.rsqrt(jnp.mean(ov * ov, axis=-1, keepdims=True) + EPS)
            mix_ref[:, CC + h * LANES:CC + (h + 1) * LANES] = (ov * r * ng_ref[...] * _silu(z_ref[:, cols])).astype(BF16)

    return pl.pallas_call(
        body, name=name, grid=(T // tr,),
        in_specs=[pl.BlockSpec((tr, VW), lambda i: (i, 0)), pl.BlockSpec((tr, VW), lambda i: (i, zoff)),
                  pl.BlockSpec((1, LANES), lambda i: (0, 0)), pl.BlockSpec((tr, CC), lambda i: (i, 0))],
        out_specs=pl.BlockSpec((tr, CC + VW), lambda i: (i, 0)),
        out_shape=jax.ShapeDtypeStruct((T, CC + VW), BF16),
        compiler_params=_cparams(("parallel",)),
    )(o, p, norm_g.reshape(1, LANES), out_a)


def _gdn_out_bwd(o, p, norm_g, dmix, *, CC, VW, name, tr=256):
    T = o.shape[0]
    tr = min(tr, T)
    H = VW // LANES
    zoff = p.shape[1] // VW - 1

    def body(o_ref, z_ref, ng_ref, dmix_ref, do_ref, dz_ref, da_ref, dng_ref, dpb_ref):
        first = pl.program_id(0) == 0
        da = dmix_ref[:, :CC]
        da_ref[...] = da.astype(BF16)
        _acc_row(dpb_ref, 0, jnp.sum(da, axis=0, keepdims=True), first)
        ng = ng_ref[...]
        dng = jnp.zeros((1, LANES), F32)
        for h in range(H):
            cols = slice(h * LANES, (h + 1) * LANES)
            ov = o_ref[:, cols]
            zv = z_ref[:, cols]
            dout = dmix_ref[:, CC + h * LANES:CC + (h + 1) * LANES]
            r = lax.rsqrt(jnp.mean(ov * ov, axis=-1, keepdims=True) + EPS)
            on = ov * r * ng
            don = dout * _silu(zv)
            dz_ref[:, cols] = (dout * on * _dsilu(zv)).astype(BF16)
            dng = dng + jnp.sum(don * ov * r, axis=0, keepdims=True)
            dong = don * ng
            do_ref[:, cols] = r * dong - ov * (r * r * r) * jnp.mean(dong * ov, axis=-1, keepdims=True)
        _acc_row(dng_ref, 0, dng, first)

    return pl.pallas_call(
        body, name=name, grid=(T // tr,),
        in_specs=[pl.BlockSpec((tr, VW), lambda i: (i, 0)), pl.BlockSpec((tr, VW), lambda i: (i, zoff)),
                  pl.BlockSpec((1, LANES), lambda i: (0, 0)), pl.BlockSpec((tr, CC + VW), lambda i: (i, 0))],
        out_specs=[pl.BlockSpec((tr, VW), lambda i: (i, 0)), pl.BlockSpec((tr, VW), lambda i: (i, 0)),
                   pl.BlockSpec((tr, CC), lambda i: (i, 0)), pl.BlockSpec((1, LANES), lambda i: (0, 0)),
                   pl.BlockSpec((1, CC), lambda i: (0, 0))],
        out_shape=[jax.ShapeDtypeStruct((T, VW), F32), jax.ShapeDtypeStruct((T, VW), BF16),
                   jax.ShapeDtypeStruct((T, CC), BF16), jax.ShapeDtypeStruct((1, LANES), F32),
                   jax.ShapeDtypeStruct((1, CC), F32)],
        compiler_params=_cparams(("arbitrary",)),
    )(o, p, norm_g.reshape(1, LANES), dmix)


FFN_CW = 256


def _ffn_act_fwd(gu, conv_w, conv_b, *, Bl, S, name):
    FF = gu.shape[2]
    cw = min(FFN_CW, FF)

    def body(g_ref, u_ref, w_ref, b_ref, a_ref):
        gc = _conv_fwd(g_ref[...], w_ref, FFN_CONV_K) + b_ref[...]
        a_ref[...] = (_silu(gc) * u_ref[...]).astype(BF16)

    return pl.pallas_call(
        body, name=name, grid=(Bl, FF // cw),
        in_specs=[pl.BlockSpec((None, S, cw), lambda b, j: (0, b, j)),
                  pl.BlockSpec((None, S, cw), lambda b, j: (1, b, j)),
                  pl.BlockSpec((FFN_CONV_K, cw), lambda b, j: (0, j)),
                  pl.BlockSpec((1, cw), lambda b, j: (0, j))],
        out_specs=pl.BlockSpec((S, cw), lambda b, j: (b, j)),
        out_shape=jax.ShapeDtypeStruct((Bl * S, FF), BF16),
        compiler_params=_cparams(("parallel", "parallel")),
    )(gu, gu, conv_w, conv_b.reshape(1, FF))


def _ffn_act_bwd(gu, conv_w, conv_b, da, *, Bl, S, name):
    FF = gu.shape[2]
    cw = min(FFN_CW, FF)

    def body(g_ref, u_ref, w_ref, b_ref, da_ref, dgu_ref, dw_ref, db_ref):
        first = pl.program_id(1) == 0
        gate = g_ref[...]
        gc = _conv_fwd(gate, w_ref, FFN_CONV_K) + b_ref[...]
        dav = da_ref[...]
        dgu_ref[1] = (dav * _silu(gc)).astype(BF16)
        dgc = dav * u_ref[...] * _dsilu(gc)
        _acc_row(db_ref, 0, jnp.sum(dgc, axis=0, keepdims=True), first)
        _conv_bwd_w(dgc, gate, dw_ref, FFN_CONV_K, first)
        dgu_ref[0] = _conv_bwd_in(dgc, w_ref, FFN_CONV_K).astype(BF16)

    return pl.pallas_call(
        body, name=name, grid=(FF // cw, Bl),
        in_specs=[pl.BlockSpec((None, S, cw), lambda j, b: (0, b, j)),
                  pl.BlockSpec((None, S, cw), lambda j, b: (1, b, j)),
                  pl.BlockSpec((FFN_CONV_K, cw), lambda j, b: (0, j)),
                  pl.BlockSpec((1, cw), lambda j, b: (0, j)),
                  pl.BlockSpec((S, cw), lambda j, b: (b, j))],
        out_specs=[pl.BlockSpec((2, S, cw), lambda j, b: (0, b, j)),
                   pl.BlockSpec((FFN_CONV_K, cw), lambda j, b: (0, j)),
                   pl.BlockSpec((1, cw), lambda j, b: (0, j))],
        out_shape=[jax.ShapeDtypeStruct((2, Bl * S, FF), BF16),
                   jax.ShapeDtypeStruct((FFN_CONV_K, FF), F32), jax.ShapeDtypeStruct((1, FF), F32)],
        compiler_params=_cparams(("parallel", "arbitrary")),
    )(gu, gu, conv_w, conv_b.reshape(1, FF), da)


def _layer_dims(W):
    CC = W["conv_pw_b"].shape[0]
    VW = W["mix_norm_g"].shape[0] - CC
    KW = (W["gdn_conv_w"].shape[1] - VW) // 2
    return CC, KW, VW


def _layer_fwd(l, x, W, Bl, S, fetch):
    CC, KW, VW = _layer_dims(W)
    H = VW // LANES
    w_in_t, w_in_ba = fetch(l, "w_in", x)
    n_main = (w_in_t.shape[0] // LANES) * LANES
    h1 = _rms_fwd(x, W["mix_norm_g"], name="rms1_fwd")
    p = _mm(h1, w_in_t, tb=True, b_rows=n_main, name="mm_in")
    pba = _mm(h1, w_in_ba, tb=True, name="mm_in_ba")
    u3, u1 = _conf_fwd(p, W["conv_dw_w"], W["conv_dw_b"], W["conv_ln_g"], W["conv_ln_b"], Bl=Bl, S=S, CC=CC,
                       name="conf_fwd")
    conv_pw_w = fetch(l, "conv_pw_w", u3)
    out_a = _mm(u3, conv_pw_w, bias=W["conv_pw_b"], out_dtype=BF16, name="mm_pw")
    qkv = _gdn_pre_fwd(p, W["gdn_conv_w"], Bl=Bl, S=S, CC=CC, KW=KW, VW=VW, name="gdn_pre_fwd")
    g, beta = _gdn_gate_fwd(pba, W["gdn_a_log"], W["gdn_dt_bias"], Bl=Bl, S=S, H=H, name="gdn_gate_fwd")
    o, states, tinv, gdn_u, gdn_w = _gdn_core_fwd(qkv, g, beta, Bl=Bl, S=S, KW=KW, VW=VW, name="gdn_core_fwd")
    mix = _gdn_out_fwd(o, p, W["gdn_norm_g"], out_a, CC=CC, VW=VW, name="gdn_out_fwd")
    w_out = fetch(l, "w_out", mix)
    x1 = _mm(mix, w_out, res=x, name="mm_out")
    h2 = _rms_fwd(x1, W["ffn_norm_g"], name="rms2_fwd")
    w_up = fetch(l, "w_up", h2)
    gu = _mm(h2, w_up, out_blocks=2, tn=w_up.shape[2], name="mm_up")
    a = _ffn_act_fwd(gu, W["ffn_conv_w"], W["ffn_conv_b"], Bl=Bl, S=S, name="ffn_act_fwd")
    w_down = fetch(l, "w_down", a)
    x2 = _mm(a, w_down, res=x1, name="mm_down")
    saved = dict(x=x, h1=h1, p=p, pba=pba, u1=u1, u3=u3, qkv=qkv, g=g, beta=beta, o=o, states=states, tinv=tinv,
                 gdn_u=gdn_u, gdn_w=gdn_w, mix=mix, x1=x1, h2=h2, gu=gu, a=a, w_in_t=w_in_t, w_in_ba=w_in_ba, conv_pw_w=conv_pw_w,
                 w_out=w_out, w_up=w_up, w_down=w_down)
    return x2, saved


def _layer_bwd(l, dx2, dx2b, W, A, Bl, S, sink):
    CC, KW, VW = _layer_dims(W)
    H = VW // LANES
    G = {}
    upw = A["w_up"].shape[2]
    da = _mm(dx2b, A["w_down"], tb=True, tn=upw, name="mm_down_dx")
    da = sink(l, "w_down", _mm(A["a"], dx2b, ta=True, out_dtype=BF16, tm=upw, name="mm_down_dw"), da)
    dgu, G["ffn_conv_w"], G["ffn_conv_b"] = _ffn_act_bwd(A["gu"], W["ffn_conv_w"], W["ffn_conv_b"], da,
                                                         Bl=Bl, S=S, name="ffn_act_bwd")
    dh2 = _mm(dgu, A["w_up"], tb=True, tk=upw, tn=2048, name="mm_up_dx")
    dh2 = sink(l, "w_up", _mm(A["h2"], dgu, ta=True, out_dtype=BF16, out_blocks=N_DEV, tn=upw, name="mm_up_dw"),
               dh2)
    dx1, dx1b, G["ffn_norm_g"] = _rms_bwd(A["x1"], W["ffn_norm_g"], dh2, dx2, name="rms2_bwd")
    dmix = _mm(dx1b, A["w_out"], tb=True, name="mm_out_dx")
    dmix = sink(l, "w_out", _mm(A["mix"], dx1b, ta=True, out_dtype=BF16, name="mm_out_dw"), dmix)
    do, dz, dout_a, G["gdn_norm_g"], G["conv_pw_b"] = _gdn_out_bwd(A["o"], A["p"], W["gdn_norm_g"], dmix,
                                                                   CC=CC, VW=VW, name="gdn_out_bwd")
    dq, dk, dv, dg, dbeta = _gdn_core_bwd(A["qkv"], A["g"], A["beta"], A["states"], A["tinv"], A["gdn_u"],
                                          A["gdn_w"], do, Bl=Bl, S=S, KW=KW, VW=VW, name="gdn_core_bwd")
    dpba, dalog, ddtb = _gdn_gate_bwd(A["pba"], W["gdn_a_log"], W["gdn_dt_bias"], dg, dbeta, Bl=Bl, S=S, H=H,
                                      name="gdn_gate_bwd")
    G["gdn_a_log"], G["gdn_dt_bias"] = dalog[:H, 0], ddtb[:H, 0]
    dqkv, G["gdn_conv_w"] = _gdn_pre_bwd(A["p"], W["gdn_conv_w"], dq, dk, dv, Bl=Bl, S=S, CC=CC, KW=KW, VW=VW,
                                         name="gdn_pre_bwd")
    du3 = _mm(dout_a, A["conv_pw_w"], tb=True, name="mm_pw_dx")
    du3 = sink(l, "conv_pw_w", _mm(A["u3"], dout_a, ta=True, out_dtype=BF16, name="mm_pw_dw"), du3)
    dav, dag, G["conv_dw_w"], G["conv_dw_b"], G["conv_ln_g"], G["conv_ln_b"] = _conf_bwd(
        A["p"], A["u1"], W["conv_dw_w"], W["conv_ln_g"], W["conv_ln_b"], du3, Bl=Bl, S=S, CC=CC, name="conf_bwd")
    dp = jnp.concatenate([dav, dag, dqkv, dz], axis=1)
    dp = sink(l, "w_in", (_mm(dp, A["h1"], ta=True, out_dtype=BF16, name="mm_in_dw"),
                          _mm(dpba, A["h1"], ta=True, out_dtype=BF16, name="mm_in_ba_dw")), dp)
    dh1 = _mm(dpba, A["w_in_ba"], name="mm_in_ba_dx")
    dh1 = _mm(dp, A["w_in_t"], b_rows=dp.shape[1], res=dh1, name="mm_in_dx")
    dx, dxb, G["mix_norm_g"] = _rms_bwd(A["x"], W["mix_norm_g"], dh1, dx1, name="rms1_bwd")
    return dx, dxb, G


def _local_step(x, target, Ws, final_norm_g, fetch, sink):
    Bl, S, D = x.shape
    xt = x.reshape(Bl * S, D)
    acts = []
    for l, W in enumerate(Ws):
        xt, A = _layer_fwd(l, xt, W, Bl, S, fetch)
        acts.append(A)
    loss, dx, dxb, dgf = _loss_head(xt, final_norm_g, target.reshape(Bl * S, D), name="loss_head")
    grads = [None] * len(Ws)
    for l in reversed(range(len(Ws))):
        dx, dxb, grads[l] = _layer_bwd(l, dx, dxb, Ws[l], acts[l], Bl, S, sink)
    return loss[0, 0], dx.reshape(Bl, S, D), grads, dgf.reshape(D)


def _mesh_pos():
    return lax.axis_index("x"), lax.axis_index("y"), lax.axis_index("c")


def _dev_index(px, py, pc):
    return 4 * px + 2 * py + pc


_ANY = pl.BlockSpec(memory_space=pl.ANY)


def _all_gather(arrs, *, name):
    n = len(arrs)

    def body(*refs):
        ins, outs = refs[:n], refs[n:2 * n]
        send_sems, recv_sems, local_sems = refs[2 * n:]
        x, y, c = _mesh_pos()
        me, sibling = (x, y, c), (x, y, 1 - c)
        chips = [(1 - x, y), (x, 1 - y), (1 - x, 1 - y)]

        def copy(a, k, block, to, src=None):
            dst = outs[a].at[_dev_index(*block)]
            return pltpu.make_async_remote_copy(
                src_ref=dst if src is None else src, dst_ref=dst,
                send_sem=send_sems.at[a, k], recv_sem=recv_sems.at[a, k],
                device_id=to, device_id_type=MESH)

        mine = [pltpu.make_async_copy(ins[a], outs[a].at[_dev_index(*me)], local_sems.at[a]) for a in range(n)]
        for cp in mine:
            cp.start()
        first = []
        for a in range(n):
            first.append(copy(a, 0, me, sibling, src=ins[a]))
            first += [copy(a, 1 + j, me, (*chip, c), src=ins[a]) for j, chip in enumerate(chips)]
        for cp in first:
            cp.start()
        passed = []
        for a in range(n):
            for j, chip in enumerate(chips):
                copy(a, 1 + j, (*chip, c), me).wait_recv()
                fwd = copy(a, 4 + j, (*chip, c), sibling)
                fwd.start()
                passed.append(fwd)
        for a in range(n):
            copy(a, 0, sibling, me).wait_recv()
            for j, chip in enumerate(chips):
                copy(a, 4 + j, (*chip, 1 - c), me).wait_recv()
        for cp in first + passed:
            cp.wait_send()
        for cp in mine:
            cp.wait()

    return pl.pallas_call(
        body, name=name,
        in_specs=[_ANY] * n, out_specs=[_ANY] * n,
        out_shape=[jax.ShapeDtypeStruct((N_DEV,) + a.shape, a.dtype) for a in arrs],
        scratch_shapes=[pltpu.SemaphoreType.DMA((n, N_DEV - 1)), pltpu.SemaphoreType.DMA((n, N_DEV - 1)),
                        pltpu.SemaphoreType.DMA((n,))],
    )(*arrs)


def _peers(x, y, c):
    flip = lambda v, f: 1 - v if f else v
    return [(flip(x, p & 4), flip(y, p & 2), flip(c, p & 1)) for p in range(1, N_DEV)]


GATHER_ID, SCATTER_ID = 1, 2
_SEQUENCER = dict(axis_name="sequencer", num_cores=1)


def _handshake(peers):
    barrier = pltpu.get_barrier_semaphore()
    for peer in peers:
        pl.semaphore_signal(barrier, inc=1, device_id=peer, device_id_type=MESH)
    pl.semaphore_wait(barrier, len(peers))


def _sc_gather(src, *, name):
    def body(src_ref, zone_ref, send_sems, recv_sems, local_sem):
        x, y, c = _mesh_pos()
        me, sibling = (x, y, c), (x, y, 1 - c)
        chips = [(1 - x, y), (x, 1 - y), (1 - x, 1 - y)]
        _handshake([sibling] + [(*chip, c) for chip in chips])

        def copy(k, block, to, from_src=False):
            dst = zone_ref.at[_dev_index(*block)]
            return pltpu.make_async_remote_copy(
                src_ref=src_ref if from_src else dst, dst_ref=dst, send_sem=send_sems.at[k], recv_sem=recv_sems.at[k],
                device_id=to, device_id_type=MESH)

        mine = pltpu.make_async_copy(src_ref, zone_ref.at[_dev_index(*me)], local_sem)
        mine.start()
        first = [copy(1 + j, me, (*chip, c), from_src=True) for j, chip in enumerate(chips)]
        first.append(copy(0, me, sibling, from_src=True))
        for cp in first:
            cp.start()
        passed = []
        for j, chip in enumerate(chips):
            copy(1 + j, (*chip, c), me).wait_recv()
            fwd = copy(4 + j, (*chip, c), sibling)
            fwd.start()
            passed.append(fwd)
        copy(0, sibling, me).wait_recv()
        for j, chip in enumerate(chips):
            copy(4 + j, (*chip, 1 - c), me).wait_recv()
        for cp in first + passed:
            cp.wait_send()
        mine.wait()

    return pl.kernel(
        body, name=name,
        out_type=jax.ShapeDtypeStruct((N_DEV,) + src.shape, src.dtype),
        mesh=plsc.ScalarSubcoreMesh(**_SEQUENCER),
        scratch_types=[pltpu.SemaphoreType.DMA((N_DEV - 1,)), pltpu.SemaphoreType.DMA((N_DEV - 1,)),
                       pltpu.SemaphoreType.DMA],
        compiler_params=pltpu.CompilerParams(collective_id=GATHER_ID),
    )(src)


def _sc_scatter(part, *, name):
    def body(src_ref, zone_ref, send_sems, recv_sems, local_sem):
        x, y, c = _mesh_pos()
        me = _dev_index(x, y, c)
        peers = _peers(x, y, c)
        _handshake(peers)
        mine = pltpu.make_async_copy(src_ref.at[me], zone_ref.at[me], local_sem)
        mine.start()
        sends = [pltpu.make_async_remote_copy(
            src_ref=src_ref.at[_dev_index(*peer)], dst_ref=zone_ref.at[me], send_sem=send_sems.at[k],
            recv_sem=recv_sems.at[k], device_id=peer, device_id_type=MESH) for k, peer in enumerate(peers)]
        for cp in sends:
            cp.start()
        for k, peer in enumerate(peers):
            pltpu.make_async_remote_copy(
                src_ref=src_ref.at[me], dst_ref=zone_ref.at[_dev_index(*peer)], send_sem=send_sems.at[k],
                recv_sem=recv_sems.at[k], device_id=peer, device_id_type=MESH).wait_recv()
        for cp in sends:
            cp.wait_send()
        mine.wait()

    return pl.kernel(
        body, name=name,
        out_type=jax.ShapeDtypeStruct(part.shape, part.dtype),
        mesh=plsc.ScalarSubcoreMesh(**_SEQUENCER),
        scratch_types=[pltpu.SemaphoreType.DMA((N_DEV - 1,)), pltpu.SemaphoreType.DMA((N_DEV - 1,)),
                       pltpu.SemaphoreType.DMA],
        compiler_params=pltpu.CompilerParams(collective_id=SCATTER_ID),
    )(part)


def _adamw_math(w, g, m, v):
    m2 = ADAM_B1 * m + (1.0 - ADAM_B1) * g
    v2 = ADAM_B2 * v + (1.0 - ADAM_B2) * (g * g)
    m_hat = m2 / (1.0 - ADAM_B1 ** ADAM_STEP)
    v_hat = v2 / (1.0 - ADAM_B2 ** ADAM_STEP)
    delta = -ADAM_LR * (m_hat / (jnp.sqrt(v_hat) + ADAM_EPS) + ADAM_WD * w)
    return delta, m2, v2


def _adamw_big(l, w, m, v, recv, prev, *, summed=False, name, tr=128):
    L, R, C = w.shape
    tr = next(t for t in range(min(tr, R), 0, -16) if R % t == 0)

    def body(w_ref, m_ref, v_ref, r_ref, *rest):
        g_ref, d_ref, m2_ref, v2_ref = rest[-4:]
        if summed:
            g = r_ref[...]
        else:
            g = r_ref[0].astype(F32)
            for s in range(1, N_DEV):
                g = g + r_ref[s].astype(F32)
        g_ref[...] = g
        d_ref[...], m2_ref[...], v2_ref[...] = _adamw_math(w_ref[...], g, m_ref[...], v_ref[...])

    wspec = pl.BlockSpec((None, tr, C), lambda i: (l, i, 0))
    rspec = pl.BlockSpec((tr, C), lambda i: (i, 0)) if summed else pl.BlockSpec((N_DEV, tr, C), lambda i: (0, i, 0))
    prev = list(prev) if prev is not None else []
    return pl.pallas_call(
        body, name=name, grid=(R // tr,),
        in_specs=[wspec, wspec, wspec, rspec] + [_ANY] * len(prev),
        out_specs=[wspec] * 4,
        out_shape=[jax.ShapeDtypeStruct((L, R, C), F32)] * 4,
        input_output_aliases={4 + j: j for j in range(len(prev))},
        compiler_params=_cparams(("parallel",)),
    )(w, m, v, recv if summed else recv.reshape(N_DEV, R, C), *prev)


def _sum_slots_wide(recv, *, name, tc=512):
    _, R, C = recv.shape
    tc = _tile(C, tc)

    def body(r_ref, o_ref):
        g = r_ref[0].astype(F32)
        for s in range(1, N_DEV):
            g = g + r_ref[s].astype(F32)
        o_ref[...] = g

    return pl.pallas_call(
        body, name=name, grid=(C // tc,),
        in_specs=[pl.BlockSpec((N_DEV, R, tc), lambda j: (0, 0, j))],
        out_specs=pl.BlockSpec((R, tc), lambda j: (0, j)),
        out_shape=jax.ShapeDtypeStruct((R, C), F32),
        compiler_params=_cparams(("parallel",)),
    )(recv)


def _sum_slots(gathered, *, name):
    _, R, C = gathered.shape

    def body(r_ref, o_ref):
        g = r_ref[0]
        for s in range(1, N_DEV):
            g = g + r_ref[s]
        o_ref[...] = g

    return pl.pallas_call(body, name=name, out_shape=jax.ShapeDtypeStruct((R, C), F32))(gathered)


def _adamw_small(w, g, m, v, *, name):
    def body(w_ref, g_ref, m_ref, v_ref, d_ref, m2_ref, v2_ref):
        d_ref[...], m2_ref[...], v2_ref[...] = _adamw_math(w_ref[...], g_ref[...], m_ref[...], v_ref[...])

    return pl.pallas_call(body, name=name, out_shape=[jax.ShapeDtypeStruct(w.shape, F32)] * 3)(w, g, m, v)


def _pack(arrs):
    flat = []
    for a in arrs:
        a = a.reshape(-1).astype(F32)
        flat.append(jnp.pad(a, (0, (-a.shape[0]) % LANES)))
    out = jnp.concatenate(flat)
    out = jnp.pad(out, (0, (-out.shape[0]) % (8 * LANES)))
    return out.reshape(-1, LANES)


def _unpack(packed, shapes):
    flat = packed.reshape(-1)
    out, pos = [], 0
    for s in shapes:
        size = math.prod(s)
        out.append(flat[pos:pos + size].reshape(s))
        pos += size + (-size) % LANES
    return out


BIG = ("w_in", "conv_pw_w", "w_out", "w_up", "w_down")
SMALL_SHARDED = ("conv_dw_w", "gdn_conv_w", "ffn_conv_w")
SMALL_REPLICATED = ("mix_norm_g", "conv_dw_b", "conv_ln_g", "conv_ln_b", "conv_pw_b", "gdn_a_log", "gdn_dt_bias",
                    "gdn_norm_g", "ffn_norm_g", "ffn_conv_b")
WEIGHTS = ("mix_norm_g", "w_in", "conv_dw_w", "conv_dw_b", "conv_ln_g", "conv_ln_b", "conv_pw_w", "conv_pw_b",
           "gdn_conv_w", "gdn_a_log", "gdn_dt_bias", "gdn_norm_g", "w_out", "ffn_norm_g", "w_up", "ffn_conv_w",
           "ffn_conv_b", "w_down", "final_norm_g")


def _train_step(x, target, w, m, v):
    L = w["w_in"].shape[0]
    D = x.shape[-1]
    xi, yi, ci = _mesh_pos()
    me = _dev_index(xi, yi, ci)

    gathered = {}

    def launch(l, after=None):
        for n in BIG:
            src = (w[n][l].T if n == "w_in" else w[n][l]).astype(BF16)
            if after is not None:
                src = lax.optimization_barrier((src, after))[0]
            gathered[n, l] = _sc_gather(src, name=f"gather_{n}_{l}")

    launch(0)
    small_full = {}
    for n, g_ in zip(SMALL_SHARDED, _all_gather([w[n] for n in SMALL_SHARDED], name="all_gather_conv_taps")):
        small_full[n] = jnp.moveaxis(g_, 0, 2).reshape(L, g_.shape[2], N_DEV * g_.shape[3])
    Ws = []
    for l in range(L):
        W = {n: w[n][l] for n in SMALL_REPLICATED}
        W.update({n: small_full[n][l] for n in SMALL_SHARDED})
        Ws.append(W)

    def fetch(l, n, after):
        if n == "conv_pw_w" and l + 1 < L:
            launch(l + 1, after)
        g_ = lax.optimization_barrier((gathered[n, l], after))[0]
        if n == "w_up":
            return g_
        g_ = g_.reshape(g_.shape[0] * g_.shape[1], g_.shape[2])
        if n == "w_in":
            n_main = (g_.shape[0] // LANES) * LANES
            return g_, jnp.pad(g_[n_main:], ((0, LANES - (g_.shape[0] - n_main)), (0, 0)))
        return g_

    started = []
    res = {}
    SCATTERS_IN_FLIGHT = 2

    def consume(chain):
        n, l, recv = started.pop(0)
        if chain is not None:
            recv, chain = lax.optimization_barrier((recv, chain))
        if n == "w_in":
            recv = _sum_slots_wide(recv, name="sum_w_in_grad").T
        res[n] = _adamw_big(l, w[n], m[n], v[n], recv, res.get(n), summed=(n == "w_in"), name=f"adamw_{n}")
        if chain is None:
            return None
        tied = lax.optimization_barrier((chain, *res[n]))
        res[n] = list(tied[1:])
        return tied[0]

    def sink(l, n, g_, chain):
        g_, chain = lax.optimization_barrier((g_, chain))
        if len(started) >= SCATTERS_IN_FLIGHT:
            chain = consume(chain)
        if n == "w_in":
            g_main, g_ba = g_
            g_ = jnp.concatenate([g_main, g_ba[:w["w_in"].shape[2] * N_DEV - g_main.shape[0]]], axis=0)
            part = g_.reshape(N_DEV, -1, D)
        elif n == "w_up":
            part = g_
        else:
            part = g_.reshape(N_DEV, -1, g_.shape[1])
        started.append((n, l, _sc_scatter(part, name=f"scatter_{n}_{l}")))
        return chain

    loss, grad_x, G, d_final = _local_step(x, target, Ws, w["final_norm_g"], fetch, sink)

    small_names = [n for n in WEIGHTS if n not in BIG]
    partial = []
    for n in small_names:
        if n == "final_norm_g":
            partial.append(d_final)
        else:
            partial.append(jnp.stack([G[l][n].reshape(Ws[l][n].shape) for l in range(L)]))
    partial.append(loss.reshape(1))
    packed = _pack(partial)
    if started:
        packed = lax.optimization_barrier((packed, started[-1][2]))[0]
    small_gathered = _sc_gather(packed, name="gather_small_grads")

    out = {k: {} for k in ("grad", "delta", "new_m", "new_v")}
    while started:
        grad_x = consume(grad_x)
    for n in BIG:
        for j, k in enumerate(("grad", "delta", "new_m", "new_v")):
            out[k][n] = res[n][j]

    summed = _unpack(_sum_slots(small_gathered, name="sum_small_grads"), [p_.shape for p_ in partial])
    full = dict(zip(small_names, summed))
    loss = summed[-1][0]
    for n in SMALL_SHARDED:
        width = w[n].shape[-1]
        full[n] = lax.dynamic_slice_in_dim(full[n], me * width, width, axis=2)
    loc_shapes = [w[n].shape for n in small_names]
    g_pack = _pack([full[n] for n in small_names])
    res = _adamw_small(_pack([w[n] for n in small_names]), g_pack, _pack([m[n] for n in small_names]),
                       _pack([v[n] for n in small_names]), name="adamw_small")
    for k, packed in zip(("grad", "delta", "new_m", "new_v"), (g_pack,) + tuple(res)):
        out[k].update(dict(zip(small_names, _unpack(packed, loc_shapes))))
    return loss, grad_x, out


def kernel(x, mix_norm_g, w_in, conv_dw_w, conv_dw_b, conv_ln_g, conv_ln_b, conv_pw_w, conv_pw_b, gdn_conv_w, gdn_a_log, gdn_dt_bias, gdn_norm_g, w_out, ffn_norm_g, w_up, ffn_conv_w, ffn_conv_b, w_down, final_norm_g, loss_target, m_mix_norm_g, m_w_in, m_conv_dw_w, m_conv_dw_b, m_conv_ln_g, m_conv_ln_b, m_conv_pw_w, m_conv_pw_b, m_gdn_conv_w, m_gdn_a_log, m_gdn_dt_bias, m_gdn_norm_g, m_w_out, m_ffn_norm_g, m_w_up, m_ffn_conv_w, m_ffn_conv_b, m_w_down, m_final_norm_g, v_mix_norm_g, v_w_in, v_conv_dw_w, v_conv_dw_b, v_conv_ln_g, v_conv_ln_b, v_conv_pw_w, v_conv_pw_b, v_gdn_conv_w, v_gdn_a_log, v_gdn_dt_bias, v_gdn_norm_g, v_w_out, v_ffn_norm_g, v_w_up, v_ffn_conv_w, v_ffn_conv_b, v_w_down, v_final_norm_g):
    w = dict(zip(WEIGHTS, (mix_norm_g, w_in, conv_dw_w, conv_dw_b, conv_ln_g, conv_ln_b, conv_pw_w, conv_pw_b, gdn_conv_w,
                           gdn_a_log, gdn_dt_bias, gdn_norm_g, w_out, ffn_norm_g, w_up, ffn_conv_w, ffn_conv_b, w_down,
                           final_norm_g)))
    m = dict(zip(WEIGHTS, (m_mix_norm_g, m_w_in, m_conv_dw_w, m_conv_dw_b, m_conv_ln_g, m_conv_ln_b, m_conv_pw_w,
                           m_conv_pw_b, m_gdn_conv_w, m_gdn_a_log, m_gdn_dt_bias, m_gdn_norm_g, m_w_out, m_ffn_norm_g,
                           m_w_up, m_ffn_conv_w, m_ffn_conv_b, m_w_down, m_final_norm_g)))
    v = dict(zip(WEIGHTS, (v_mix_norm_g, v_w_in, v_conv_dw_w, v_conv_dw_b, v_conv_ln_g, v_conv_ln_b, v_conv_pw_w,
                           v_conv_pw_b, v_gdn_conv_w, v_gdn_a_log, v_gdn_dt_bias, v_gdn_norm_g, v_w_out, v_ffn_norm_g,
                           v_w_up, v_ffn_conv_w, v_ffn_conv_b, v_w_down, v_final_norm_g)))
    loss, grad_x, out = _train_step(x, loss_target, w, m, v)
    return (loss, grad_x, *[out["grad"][n] for n in WEIGHTS], *[out["delta"][n] for n in WEIGHTS],
            *[out["new_m"][n] for n in WEIGHTS], *[out["new_v"][n] for n in WEIGHTS])
```

```python
import math

import jax
import jax.numpy as jnp
from jax import lax
from jax.experimental import pallas as pl
from jax.experimental.pallas import tpu as pltpu
from jax.experimental.pallas import tpu_sc as plsc

F32 = jnp.float32
BF16 = jnp.bfloat16
MESH = pl.DeviceIdType.MESH

EPS = 1e-6
LANES = 128
CHUNK = 128
NEAR_BLOCK = 32
CONV_K = 31
SHORT_CONV_K = 4
FFN_CONV_K = 3
N_DEV = 8
VMEM_LIMIT = 56 * 1024 * 1024

ADAM_LR = 0.001
ADAM_B1 = 0.9
ADAM_B2 = 0.999
ADAM_EPS = 1e-08
ADAM_WD = 0.01
ADAM_STEP = 10


def _cparams(sem):
    return pltpu.CompilerParams(dimension_semantics=sem, vmem_limit_bytes=VMEM_LIMIT)


def _sig(x):
    return 1.0 / (1.0 + jnp.exp(-x))


def _silu(x):
    return x * _sig(x)


def _dsilu(x):
    s = _sig(x)
    return s * (1.0 + x * (1.0 - s))


def _softplus(x):
    return jnp.maximum(x, 0.0) + jnp.log1p(jnp.exp(-jnp.abs(x)))


def _dot(a, b):
    return jnp.dot(a, b, preferred_element_type=F32)


def _dot_nt(a, b):
    return lax.dot_general(a, b, (((1,), (1,)), ((), ())), preferred_element_type=F32)


def _dot_tn(a, b):
    return lax.dot_general(a, b, (((0,), (0,)), ((), ())), preferred_element_type=F32)


def _bf(x):
    return x.astype(BF16)


_NN = (((1,), (0,)), ((), ()))
_TN = (((0,), (0,)), ((), ()))


def _split2(x):
    hi = _bf(x)
    return hi, _bf(x - hi.astype(F32))


def _dot_x3(a, b, dn=_NN):
    ah, al = _split2(a)
    bh, bl = _split2(b)
    f = lambda p, q: lax.dot_general(p, q, dn, preferred_element_type=F32)
    return f(ah, bh) + (f(al, bh) + f(ah, bl))


def _dot_mask(mask, x, dn=_NN):
    mb = _bf(mask)
    hi, lo = _split2(x)
    lo2 = _bf(x - hi.astype(F32) - lo.astype(F32))
    f = lambda q: lax.dot_general(mb, q, dn, preferred_element_type=F32)
    return f(hi) + (f(lo) + f(lo2))


def _shift_down(u, s):
    if s == 0:
        return u
    row = lax.broadcasted_iota(jnp.int32, u.shape, 0)
    return jnp.where(row >= s, pltpu.roll(u, s, 0), 0.0)


def _shift_up(u, s):
    if s == 0:
        return u
    n = u.shape[0]
    row = lax.broadcasted_iota(jnp.int32, u.shape, 0)
    return jnp.where(row < n - s, pltpu.roll(u, n - s, 0), 0.0)


def _conv_fwd(u, w_ref, K):
    acc = None
    for k in range(K):
        term = w_ref[k:k + 1, :] * _shift_down(u, K - 1 - k)
        acc = term if acc is None else acc + term
    return acc


def _conv_bwd_in(do, w_ref, K):
    acc = None
    for k in range(K):
        term = w_ref[k:k + 1, :] * _shift_up(do, K - 1 - k)
        acc = term if acc is None else acc + term
    return acc


def _conv_bwd_w(do, u, dw_ref, K, first):
    for k in range(K):
        row = jnp.sum(do * _shift_down(u, K - 1 - k), axis=0, keepdims=True)
        _acc_row(dw_ref, k, row, first)


def _acc_row(ref, k, row, first):
    @pl.when(first)
    def _():
        ref[k:k + 1, :] = row

    @pl.when(jnp.logical_not(first))
    def _():
        ref[k:k + 1, :] += row


def _logical(arr):
    if arr.ndim == 2:
        return arr.shape
    return (arr.shape[1], arr.shape[0] * arr.shape[2])


def _tile(dim, pref, *col_widths):
    if dim % LANES:
        assert not col_widths
        return dim
    t = (min(pref, dim) // LANES) * LANES
    while t > LANES and (dim % t or any(c % t for c in col_widths)):
        t -= LANES
    assert dim % t == 0 and all(c % t == 0 for c in col_widths), (dim, pref, col_widths)
    return t


def _spec(shape, rt, ct, rfn, cfn):
    if len(shape) == 2:
        return pl.BlockSpec((rt, ct), lambda i, j, k: (rfn(i, j, k), cfn(i, j, k)))
    per = shape[2] // ct
    return pl.BlockSpec((None, rt, ct),
                        lambda i, j, k: (cfn(i, j, k) // per, rfn(i, j, k), cfn(i, j, k) % per))


def _mm(a, b, *, name, ta=False, tb=False, out_dtype=F32, out_blocks=None, bias=None, res=None, b_rows=None,
        tm=1024, tn=1024, tk=2816):
    ra, ca = _logical(a)
    rb, cb = _logical(b)
    if b_rows is not None:
        assert b.ndim == 2 and b_rows <= rb
        rb = b_rows
    M, K = (ca, ra) if ta else (ra, ca)
    N, K2 = (rb, cb) if tb else (cb, rb)
    assert K == K2, (a.shape, b.shape, ta, tb)
    out_shape = (M, N) if out_blocks is None else (out_blocks, M, N // out_blocks)
    cw = lambda arr: [arr.shape[2]] if arr.ndim == 3 else []
    m_c = cw(a) if ta else []
    k_c = (cw(a) if not ta else []) + (cw(b) if tb else [])
    n_c = (cw(b) if not tb else []) + ([out_shape[2]] if out_blocks else []) + (cw(res) if res is not None else [])
    tm, tn, tk = _tile(M, tm, *m_c), _tile(N, tn, *n_c), _tile(K, tk, *k_c)
    nk = K // tk
    im, jn, kk = (lambda i, j, k: i), (lambda i, j, k: j), (lambda i, j, k: k)
    in_specs = [
        _spec(a.shape, tk, tm, kk, im) if ta else _spec(a.shape, tm, tk, im, kk),
        _spec(b.shape, tn, tk, jn, kk) if tb else _spec(b.shape, tk, tn, kk, jn),
    ]
    args = [a, b]
    if bias is not None:
        in_specs.append(pl.BlockSpec((1, tn), lambda i, j, k: (0, j)))
        args.append(bias.reshape(1, N).astype(F32))
    if res is not None:
        in_specs.append(_spec(res.shape, tm, tn, im, jn))
        args.append(res)
    dn = (((0 if ta else 1,), (1 if tb else 0,)), ((), ()))

    def body(*refs):
        a_ref, b_ref = refs[0], refs[1]
        pos = 2
        bias_ref = res_ref = None
        if bias is not None:
            bias_ref = refs[pos]
            pos += 1
        if res is not None:
            res_ref = refs[pos]
            pos += 1
        o_ref = refs[pos]
        k = pl.program_id(2)
        part = lax.dot_general(_bf(a_ref[...]), _bf(b_ref[...]), dn, preferred_element_type=F32)

        def finish(r):
            if bias_ref is not None:
                r = r + bias_ref[...]
            if res_ref is not None:
                r = r + res_ref[...].astype(F32)
            o_ref[...] = r.astype(out_dtype)

        if nk == 1:
            finish(part)
            return
        acc_ref = refs[pos + 1]

        @pl.when(k == 0)
        def _():
            acc_ref[...] = part

        @pl.when((k > 0) & (k < nk - 1))
        def _():
            acc_ref[...] += part

        @pl.when(k == nk - 1)
        def _():
            finish(acc_ref[...] + part)

    return pl.pallas_call(
        body, name=name,
        grid=(M // tm, N // tn, nk),
        in_specs=in_specs,
        out_specs=_spec(out_shape, tm, tn, im, jn),
        out_shape=jax.ShapeDtypeStruct(out_shape, out_dtype),
        scratch_shapes=[pltpu.VMEM((tm, tn), F32)] if nk > 1 else [],
        compiler_params=_cparams(("parallel", "parallel", "arbitrary")),
    )(*args)


def _rms_fwd(x, g, *, name, tr=512):
    T, D = x.shape
    tr = min(tr, T)

    def body(x_ref, g_ref, h_ref):
        xv = x_ref[...]
        r = lax.rsqrt(jnp.mean(xv * xv, axis=-1, keepdims=True) + EPS)
        h_ref[...] = (xv * r * g_ref[...]).astype(BF16)

    return pl.pallas_call(
        body, name=name, grid=(T // tr,),
        in_specs=[pl.BlockSpec((tr, D), lambda i: (i, 0)), pl.BlockSpec((1, D), lambda i: (0, 0))],
        out_specs=pl.BlockSpec((tr, D), lambda i: (i, 0)),
        out_shape=jax.ShapeDtypeStruct((T, D), BF16),
        compiler_params=_cparams(("parallel",)),
    )(x, g.reshape(1, D))


def _rms_bwd(x, g, dh, dres, *, name, tr=512):
    T, D = x.shape
    tr = min(tr, T)

    def body(x_ref, g_ref, dh_ref, dres_ref, dx_ref, dxb_ref, dg_ref):
        i = pl.program_id(0)
        xv = x_ref[...]
        dy = dh_ref[...].astype(F32)
        r = lax.rsqrt(jnp.mean(xv * xv, axis=-1, keepdims=True) + EPS)
        dyg = dy * g_ref[...]
        dot = jnp.mean(dyg * xv, axis=-1, keepdims=True)
        dx = dres_ref[...] + r * dyg - xv * (r * r * r) * dot
        dx_ref[...] = dx
        dxb_ref[...] = dx.astype(BF16)
        part = jnp.sum(dy * xv * r, axis=0, keepdims=True)
        _acc_row(dg_ref, 0, part, i == 0)

    row = pl.BlockSpec((tr, D), lambda i: (i, 0))
    vec = pl.BlockSpec((1, D), lambda i: (0, 0))
    return pl.pallas_call(
        body, name=name, grid=(T // tr,),
        in_specs=[row, vec, row, row],
        out_specs=[row, row, vec],
        out_shape=[jax.ShapeDtypeStruct((T, D), F32), jax.ShapeDtypeStruct((T, D), BF16),
                   jax.ShapeDtypeStruct((1, D), F32)],
        compiler_params=_cparams(("arbitrary",)),
    )(x, g.reshape(1, D), dh, dres)


def _loss_head(x, g, target, *, name, tr=512):
    T, D = x.shape
    tr = min(tr, T)

    def body(x_ref, g_ref, t_ref, loss_ref, dx_ref, dxb_ref, dg_ref):
        i = pl.program_id(0)
        xv = x_ref[...]
        gv = g_ref[...]
        r = lax.rsqrt(jnp.mean(xv * xv, axis=-1, keepdims=True) + EPS)
        y = xv * r * gv
        err = y - t_ref[...]
        lpart = 0.5 * jnp.sum(jnp.mean(err * err, axis=-1, keepdims=True), axis=0, keepdims=True)
        dy = err * (1.0 / D)
        dyg = dy * gv
        dot = jnp.mean(dyg * xv, axis=-1, keepdims=True)
        dx = r * dyg - xv * (r * r * r) * dot
        dx_ref[...] = dx
        dxb_ref[...] = dx.astype(BF16)
        _acc_row(dg_ref, 0, jnp.sum(dy * xv * r, axis=0, keepdims=True), i == 0)
        _acc_row(loss_ref, 0, jnp.broadcast_to(lpart, (1, LANES)), i == 0)

    row = pl.BlockSpec((tr, D), lambda i: (i, 0))
    return pl.pallas_call(
        body, name=name, grid=(T // tr,),
        in_specs=[row, pl.BlockSpec((1, D), lambda i: (0, 0)), row],
        out_specs=[pl.BlockSpec((1, LANES), lambda i: (0, 0)), row, row, pl.BlockSpec((1, D), lambda i: (0, 0))],
        out_shape=[jax.ShapeDtypeStruct((1, LANES), F32), jax.ShapeDtypeStruct((T, D), F32),
                   jax.ShapeDtypeStruct((T, D), BF16), jax.ShapeDtypeStruct((1, D), F32)],
        compiler_params=_cparams(("arbitrary",)),
    )(x, g.reshape(1, D), target)


HALO = 32
SUBLANES = 8


def _conv_dw_blocks(do, u, dw_ref, K, first, u_s, do_p):
    S, C = u.shape
    zeros = jnp.zeros((HALO, C), F32)
    for r in range(SUBLANES):
        u_s[r, 0:HALO, :] = zeros
        u_s[r, HALO:HALO + S, :] = _shift_down(u, r)
    do_p[0:HALO, :] = zeros
    do_p[HALO:HALO + S, :] = do
    do_p[HALO + S:2 * HALO + S, :] = zeros
    n_a = (K - 1) // SUBLANES + 1

    def block(i, accs):
        i0 = pl.multiple_of(i * SUBLANES, SUBLANES)
        us = [u_s[r, pl.ds(i0, SUBLANES), :] for r in range(SUBLANES)]
        ds = [do_p[pl.ds(i0 + SUBLANES * a, SUBLANES), :] for a in range(n_a)]
        out = list(accs)
        for a in range(n_a):
            for r in range(SUBLANES):
                s = SUBLANES * a + r
                if s < K:
                    out[K - 1 - s] = out[K - 1 - s] + ds[a] * us[r]
        return tuple(out)

    accs = lax.fori_loop(HALO // SUBLANES, (S + HALO) // SUBLANES, block, (jnp.zeros((SUBLANES, C), F32),) * K)
    for k in range(K):
        _acc_row(dw_ref, k, jnp.sum(accs[k], axis=0, keepdims=True), first)


def _conf_norm(u1, lg_ref, lb_ref):
    mu = jnp.mean(u1, axis=-1, keepdims=True)
    xc = u1 - mu
    r = lax.rsqrt(jnp.mean(xc * xc, axis=-1, keepdims=True) + EPS)
    n = xc * r
    return r, n, n * lg_ref[...] + lb_ref[...]


def _conf_fwd(p, dw_w, dw_b, ln_g, ln_b, *, Bl, S, CC, name):
    G = CC // LANES

    def body(av_ref, ag_ref, w_ref, b_ref, lg_ref, lb_ref, o_ref, u1_ref):
        u0 = av_ref[...] * _sig(ag_ref[...])
        u1 = _conv_fwd(u0, w_ref, CONV_K) + b_ref[...]
        u1_ref[...] = u1
        _, _, u2 = _conf_norm(u1, lg_ref, lb_ref)
        o_ref[...] = _silu(u2).astype(BF16)

    vec = pl.BlockSpec((1, LANES), lambda b, j: (0, j))
    seq = pl.BlockSpec((S, LANES), lambda b, j: (b, j))
    return pl.pallas_call(
        body, name=name, grid=(Bl, G),
        in_specs=[seq, pl.BlockSpec((S, LANES), lambda b, j: (b, G + j)),
                  pl.BlockSpec((CONV_K, LANES), lambda b, j: (0, j)), vec, vec, vec],
        out_specs=[seq, seq],
        out_shape=[jax.ShapeDtypeStruct((Bl * S, CC), BF16), jax.ShapeDtypeStruct((Bl * S, CC), F32)],
        compiler_params=_cparams(("parallel", "parallel")),
    )(p, p, dw_w, dw_b.reshape(1, CC), ln_g.reshape(1, CC), ln_b.reshape(1, CC))


def _conf_bwd(p, u1, dw_w, ln_g, ln_b, du3, *, Bl, S, CC, name):
    G = CC // LANES

    def body(av_ref, ag_ref, u1_ref, w_ref, lg_ref, lb_ref, du3_ref,
             dav_ref, dag_ref, dw_ref, db_ref, dlg_ref, dlb_ref, u_s, do_p):
        first = pl.program_id(1) == 0
        av = av_ref[...]
        sg = _sig(ag_ref[...])
        r, n, u2 = _conf_norm(u1_ref[...], lg_ref, lb_ref)
        du2 = du3_ref[...] * _dsilu(u2)
        _acc_row(dlg_ref, 0, jnp.sum(du2 * n, axis=0, keepdims=True), first)
        _acc_row(dlb_ref, 0, jnp.sum(du2, axis=0, keepdims=True), first)
        dn = du2 * lg_ref[...]
        du1 = r * (dn - jnp.mean(dn, axis=-1, keepdims=True) - n * jnp.mean(dn * n, axis=-1, keepdims=True))
        _acc_row(db_ref, 0, jnp.sum(du1, axis=0, keepdims=True), first)
        _conv_dw_blocks(du1, av * sg, dw_ref, CONV_K, first, u_s, do_p)
        du0 = _conv_bwd_in(du1, w_ref, CONV_K)
        dav_ref[...] = (du0 * sg).astype(BF16)
        dag_ref[...] = (du0 * av * sg * (1.0 - sg)).astype(BF16)

    vec = pl.BlockSpec((1, LANES), lambda j, b: (0, j))
    seq = pl.BlockSpec((S, LANES), lambda j, b: (b, j))
    return pl.pallas_call(
        body, name=name, grid=(G, Bl),
        in_specs=[seq, pl.BlockSpec((S, LANES), lambda j, b: (b, G + j)), seq,
                  pl.BlockSpec((CONV_K, LANES), lambda j, b: (0, j)), vec, vec, seq],
        out_specs=[seq, seq, pl.BlockSpec((CONV_K, LANES), lambda j, b: (0, j)), vec, vec, vec],
        out_shape=[jax.ShapeDtypeStruct((Bl * S, CC), BF16), jax.ShapeDtypeStruct((Bl * S, CC), BF16),
                   jax.ShapeDtypeStruct((CONV_K, CC), F32), jax.ShapeDtypeStruct((1, CC), F32),
                   jax.ShapeDtypeStruct((1, CC), F32), jax.ShapeDtypeStruct((1, CC), F32)],
        scratch_shapes=[pltpu.VMEM((SUBLANES, S + HALO, LANES), F32), pltpu.VMEM((S + 2 * HALO, LANES), F32)],
        compiler_params=_cparams(("parallel", "arbitrary")),
    )(p, p, u1, dw_w, ln_g.reshape(1, CC), ln_b.reshape(1, CC), du3)


def _gdn_pre_fwd(p, conv_w, *, Bl, S, CC, KW, VW, name):
    NQK = 2 * KW // LANES
    NB = NQK + VW // LANES
    off = 2 * CC // LANES

    def body(x_ref, w_ref, o_ref):
        j = pl.program_id(1)
        s = _silu(_conv_fwd(x_ref[...], w_ref, SHORT_CONV_K))
        r = lax.rsqrt(jnp.sum(s * s, axis=-1, keepdims=True) + EPS)
        o_ref[...] = jnp.where(j < NQK, s * r, s)

    return pl.pallas_call(
        body, name=name, grid=(Bl, NB),
        in_specs=[pl.BlockSpec((S, LANES), lambda b, j: (b, off + j)),
                  pl.BlockSpec((SHORT_CONV_K, LANES), lambda b, j: (0, j))],
        out_specs=pl.BlockSpec((S, LANES), lambda b, j: (b, j)),
        out_shape=jax.ShapeDtypeStruct((Bl * S, NB * LANES), F32),
        compiler_params=_cparams(("parallel", "parallel")),
    )(p, conv_w)


def _gdn_pre_bwd(p, conv_w, dq, dk, dv, *, Bl, S, CC, KW, VW, name):
    HQ = KW // LANES
    H = VW // LANES
    NQK = 2 * HQ
    NB = NQK + H
    off = 2 * CC // LANES

    def body(x_ref, w_ref, dq_ref, dk_ref, dv_ref, dx_ref, dw_ref):
        j = pl.program_id(0)
        first = pl.program_id(1) == 0
        xv = x_ref[...]
        c = _conv_fwd(xv, w_ref, SHORT_CONV_K)
        s = _silu(c)
        r = lax.rsqrt(jnp.sum(s * s, axis=-1, keepdims=True) + EPS)
        dy = jnp.where(j < HQ, dq_ref[...], jnp.where(j < NQK, dk_ref[...], dv_ref[...]))
        ds_norm = r * dy - s * (r * r * r) * jnp.sum(s * dy, axis=-1, keepdims=True)
        ds = jnp.where(j < NQK, ds_norm, dy)
        dc = ds * _dsilu(c)
        _conv_bwd_w(dc, xv, dw_ref, SHORT_CONV_K, first)
        dx_ref[...] = _conv_bwd_in(dc, w_ref, SHORT_CONV_K).astype(BF16)

    return pl.pallas_call(
        body, name=name, grid=(NB, Bl),
        in_specs=[pl.BlockSpec((S, LANES), lambda j, b: (b, off + j)),
                  pl.BlockSpec((SHORT_CONV_K, LANES), lambda j, b: (0, j)),
                  pl.BlockSpec((S, LANES), lambda j, b: (b, jnp.minimum(j, HQ - 1))),
                  pl.BlockSpec((S, LANES), lambda j, b: (b, jnp.clip(j - HQ, 0, HQ - 1))),
                  pl.BlockSpec((S, LANES), lambda j, b: (b, jnp.clip(j - NQK, 0, H - 1)))],
        out_specs=[pl.BlockSpec((S, LANES), lambda j, b: (b, j)),
                   pl.BlockSpec((SHORT_CONV_K, LANES), lambda j, b: (0, j))],
        out_shape=[jax.ShapeDtypeStruct((Bl * S, NB * LANES), BF16),
                   jax.ShapeDtypeStruct((SHORT_CONV_K, NB * LANES), F32)],
        compiler_params=_cparams(("parallel", "arbitrary")),
    )(p, conv_w, dq, dk, dv)


def _split3(x):
    hi, lo = _split2(x)
    return hi, lo, _bf(x - hi.astype(F32) - lo.astype(F32))


def _lane_replicate(parts, h):
    row = lax.broadcasted_iota(jnp.int32, (LANES, LANES), 0)
    E = jnp.where(row == h, 1.0, 0.0).astype(BF16)
    return _dot(parts[0], E) + (_dot(parts[1], E) + _dot(parts[2], E))


def _gdn_gate_fwd(pba, a_log, dt_bias, *, Bl, S, H, name):
    def body(alog_ref, dtb_ref, x_ref, g_ref, beta_ref):
        parts = _split3(x_ref[...])
        for h in range(H):
            b_raw = _lane_replicate(parts, h)
            a_raw = _lane_replicate(parts, H + h)
            beta_ref[h] = _sig(b_raw)
            ea = jnp.exp(jnp.zeros((1, LANES), F32) + alog_ref[h])
            g_ref[h] = -ea * _softplus(a_raw + dtb_ref[h])

    smem = pl.BlockSpec(memory_space=pltpu.SMEM)
    rep = pl.BlockSpec((H, S, LANES), lambda b: (0, b, 0))
    return pl.pallas_call(
        body, name=name, grid=(Bl,),
        in_specs=[smem, smem, pl.BlockSpec((S, LANES), lambda b: (b, 0))],
        out_specs=[rep, rep],
        out_shape=[jax.ShapeDtypeStruct((H, Bl * S, LANES), F32)] * 2,
        compiler_params=_cparams(("parallel",)),
    )(a_log, dt_bias, pba)


def _gdn_gate_bwd(pba, a_log, dt_bias, dg, dbeta, *, Bl, S, H, name):
    HP = 8 * ((H + 7) // 8)

    def body(alog_ref, dtb_ref, x_ref, dg_ref, dbeta_ref, dx_ref, dalog_ref, ddtb_ref):
        first = pl.program_id(0) == 0
        parts = _split3(x_ref[...])
        lane = lax.broadcasted_iota(jnp.int32, (S, LANES), 1)
        acc = jnp.zeros((S, LANES), F32)

        @pl.when(first)
        def _():
            dalog_ref[...] = jnp.zeros_like(dalog_ref)
            ddtb_ref[...] = jnp.zeros_like(ddtb_ref)

        for h in range(H):
            b_raw = _lane_replicate(parts, h)
            a_raw = _lane_replicate(parts, H + h)
            beta = _sig(b_raw)
            db_raw = dbeta_ref[h] * beta * (1.0 - beta)
            z = a_raw + dtb_ref[h]
            ea = jnp.exp(jnp.zeros((1, LANES), F32) + alog_ref[h])
            dgv = dg_ref[h]
            da_raw = dgv * (-ea) * _sig(z)
            g = -ea * _softplus(z)
            dalog_ref[h:h + 1, :] += jnp.sum(dgv * g, axis=0, keepdims=True)
            ddtb_ref[h:h + 1, :] += jnp.sum(da_raw, axis=0, keepdims=True)
            acc = acc + jnp.where(lane == h, db_raw, 0.0) + jnp.where(lane == H + h, da_raw, 0.0)
        dx_ref[...] = acc.astype(BF16)

    smem = pl.BlockSpec(memory_space=pltpu.SMEM)
    rep = pl.BlockSpec((H, S, LANES), lambda b: (0, b, 0))
    small = pl.BlockSpec((HP, LANES), lambda b: (0, 0))
    return pl.pallas_call(
        body, name=name, grid=(Bl,),
        in_specs=[smem, smem, pl.BlockSpec((S, LANES), lambda b: (b, 0)), rep, rep],
        out_specs=[pl.BlockSpec((S, LANES), lambda b: (b, 0)), small, small],
        out_shape=[jax.ShapeDtypeStruct((Bl * S, LANES), BF16),
                   jax.ShapeDtypeStruct((HP, LANES), F32), jax.ShapeDtypeStruct((HP, LANES), F32)],
        compiler_params=_cparams(("arbitrary",)),
    )(a_log, dt_bias, pba, dg, dbeta)


def _tri_masks():
    ri = lax.broadcasted_iota(jnp.int32, (CHUNK, CHUNK), 0)
    ci = lax.broadcasted_iota(jnp.int32, (CHUNK, CHUNK), 1)
    return ri >= ci, ri > ci, ri == CHUNK - 1


def _tri_inv(L):
    ri = lax.broadcasted_iota(jnp.int32, (CHUNK, CHUNK), 0)
    ci = lax.broadcasted_iota(jnp.int32, (CHUNK, CHUNK), 1)
    T = jnp.where(ri == ci, 1.0, 0.0) - jnp.where((ri >> 1) == (ci >> 1), L, 0.0)
    for lv in range(2, int(math.log2(CHUNK)) + 1):
        O = jnp.where(((ri >> lv) == (ci >> lv)) & ((ri >> (lv - 1)) != (ci >> (lv - 1))), L, 0.0)
        if (1 << lv) <= NEAR_BLOCK:
            T = T - _dot_x3(T, _dot_x3(O, T))
        else:
            Tb = _bf(T)
            T = T - _dot(Tb, _bf(_dot(_bf(O), Tb)))
    return T


def _chunk_local(q, k, v, beta, g):
    ge, gt, last = _tri_masks()
    gam = _dot_mask(ge, g)
    D = jnp.where(ge, jnp.exp(jnp.where(ge, gam - gam.T, 0.0)), 0.0)
    kb = k * beta
    vb = v * beta
    M = _dot_nt(_bf(kb), _bf(k))
    L = jnp.where(gt, M * D, 0.0)
    eg = jnp.exp(gam)
    kbg = kb * eg
    P = _dot_nt(_bf(q), _bf(k))
    QK = jnp.where(ge, P * D, 0.0)
    gl = jnp.sum(jnp.where(last, gam, 0.0), axis=0, keepdims=True)
    el = jnp.exp(gl - gam)
    return dict(ge=ge, gt=gt, last=last, gam=gam, D=D, kb=kb, vb=vb, L=L, eg=eg, kbg=kbg, QK=QK, gl=gl,
                el=el, kd=k * el, qg=q * eg)


def _rowsum(x):
    return jnp.sum(x, axis=-1, keepdims=True)


def _chunk_bwd(q, k, v, beta, g, S, T, u, w, do, dS2):
    c = _chunk_local(q, k, v, beta, g)
    ge, gt, last = c["ge"], c["gt"], c["last"]
    Sb = _bf(S)
    vn = u - _dot(w, Sb)
    dob, vnb, dS2b = _bf(do), _bf(vn), _bf(dS2)
    e_last = jnp.exp(c["gl"])
    dqg = _dot_nt(dob, Sb)
    dS = _dot_tn(_bf(c["qg"]), dob)
    dQK = jnp.where(ge, _dot_nt(dob, vnb), 0.0)
    dvn = _dot_tn(_bf(c["QK"]), dob)
    dS = dS + dS2 * e_last
    de_last = jnp.sum(jnp.sum(dS2 * S, axis=0, keepdims=True), axis=1, keepdims=True)
    dkd = _dot_nt(vnb, dS2b)
    dvn = dvn + _dot(_bf(c["kd"]), dS2b)
    dvnb = _bf(dvn)
    dw = -_dot_nt(dvnb, Sb)
    dS = dS - _dot_tn(w, dvnb)
    dsol = _dot_x3(T, jnp.concatenate([dvn, dw], axis=1), _TN)
    dvb, dkbg = dsol[:, :LANES], dsol[:, LANES:]
    dA = -(_dot_nt(_bf(dvb), _bf(u)) + _dot_nt(_bf(dkbg), w))
    dL = jnp.where(gt, dA, 0.0)
    dM = dL * c["D"]
    dP = dQK * c["D"]
    E = dL * c["L"] + dQK * c["QK"]
    kbf = _bf(k)
    dkb = _dot(_bf(dM), kbf) + dkbg * c["eg"]
    dk = _dot_tn(_bf(dM), _bf(c["kb"])) + _dot_tn(_bf(dP), _bf(q)) + dkd * c["el"] + dkb * beta
    dq = _dot(_bf(dP), kbf) + dqg * c["eg"]
    s_kd = _rowsum(dkd * c["kd"])
    dgam = (_rowsum(E) - _rowsum(E.T) + _rowsum(dqg * c["qg"]) - s_kd + _rowsum(dkbg * c["kbg"]))
    dgl = jnp.sum(s_kd, axis=0, keepdims=True) + de_last * e_last
    dgam_rep = jnp.broadcast_to(dgam, (CHUNK, LANES)) + jnp.where(last, jnp.broadcast_to(dgl, (CHUNK, LANES)), 0.0)
    dg_rep = _dot_mask(ge, dgam_rep, _TN)
    dbeta = _rowsum(dkb * k) + _rowsum(dvb * v)
    dv = dvb * beta
    return dq, dk, dv, jnp.broadcast_to(dbeta, (CHUNK, LANES)), dg_rep, dS


def _gdn_core_fwd(qkv, g, beta, *, Bl, S, KW, VW, name):
    HQ = KW // LANES
    H = VW // LANES
    NC = S // CHUNK
    scale = float(LANES) ** -0.5

    PAIR = next(p for p in (4, 2, 1) if NC % p == 0)

    def body(q_ref, k_ref, v_ref, g_ref, beta_ref, o_ref, st_ref, t_ref, u_s, w_s,
             qk_s, qg_s, kd_s, el_s):
        def local(n2, carry):
            for half in range(PAIR):
                n = n2 * PAIR + half
                rows = pl.ds(pl.multiple_of(n * CHUNK, CHUNK), CHUNK)
                q = q_ref[rows, :] * scale
                k = k_ref[rows, :]
                for e in range(2):
                    c = _chunk_local(q, k, v_ref[rows, e * LANES:(e + 1) * LANES], beta_ref[e, rows, :],
                                     g_ref[e, rows, :])
                    T = _tri_inv(c["L"])
                    t_ref[e, rows, :] = T
                    uw = _dot_x3(T, jnp.concatenate([c["vb"], c["kbg"]], axis=1))
                    u_s[e, rows, :] = uw[:, :LANES]
                    w_s[e, rows, :] = _bf(uw[:, LANES:])
                    qk_s[e, rows, :] = _bf(c["QK"])
                    qg_s[e, rows, :] = _bf(c["qg"])
                    kd_s[e, rows, :] = _bf(c["kd"])
                    el_s[e, pl.ds(pl.multiple_of(n * 8, 8), 8), :] = jnp.broadcast_to(jnp.exp(c["gl"]), (8, LANES))
            return carry

        lax.fori_loop(0, NC // PAIR, local, 0)

        def scan(n, states):
            rows = pl.ds(pl.multiple_of(n * CHUNK, CHUNK), CHUNK)
            out = []
            for e in range(2):
                S_in = states[e]
                st_ref[e, n] = S_in
                Sb = _bf(S_in)
                vn = u_s[e, rows, :] - _dot(w_s[e, rows, :], Sb)
                vnb = _bf(vn)
                o_ref[rows, e * LANES:(e + 1) * LANES] = _dot(qg_s[e, rows, :], Sb) + _dot(qk_s[e, rows, :], vnb)
                e_last = el_s[e, pl.ds(pl.multiple_of(n * 8, 8), 1), :]
                out.append(S_in * e_last + _dot_tn(kd_s[e, rows, :], vnb))
            return tuple(out)

        z = jnp.zeros((LANES, LANES), F32)
        lax.fori_loop(0, NC, scan, (z, z))

    rep = pl.BlockSpec((2, S, LANES), lambda b, h: (h, b, 0))
    return pl.pallas_call(
        body, name=name, grid=(Bl, HQ),
        in_specs=[pl.BlockSpec((S, LANES), lambda b, h: (b, h)),
                  pl.BlockSpec((S, LANES), lambda b, h: (b, HQ + h)),
                  pl.BlockSpec((S, 2 * LANES), lambda b, h: (b, HQ + h)), rep, rep],
        out_specs=[pl.BlockSpec((S, 2 * LANES), lambda b, h: (b, h)),
                   pl.BlockSpec((None, 2, NC, LANES, LANES), lambda b, h: (b, h, 0, 0, 0)), rep, rep, rep],
        out_shape=[jax.ShapeDtypeStruct((Bl * S, VW), F32),
                   jax.ShapeDtypeStruct((Bl, H, NC, LANES, LANES), F32),
                   jax.ShapeDtypeStruct((H, Bl * S, LANES), F32),
                   jax.ShapeDtypeStruct((H, Bl * S, LANES), F32),
                   jax.ShapeDtypeStruct((H, Bl * S, LANES), BF16)],
        scratch_shapes=[pltpu.VMEM((2, S, LANES), BF16)] * 3 + [pltpu.VMEM((2, NC * 8, LANES), F32)],
        compiler_params=_cparams(("parallel", "parallel")),
    )(qkv, qkv, qkv, g, beta)


def _gdn_core_bwd(qkv, g, beta, states, tinv, u, w, do, *, Bl, S, KW, VW, name):
    HQ = KW // LANES
    H = VW // LANES
    NC = S // CHUNK
    scale = float(LANES) ** -0.5

    def body(q_ref, k_ref, v_ref, g_ref, beta_ref, st_ref, t_ref, u_ref, w_ref, do_ref,
             dq_ref, dk_ref, dv_ref, dg_ref, dbeta_ref):
        def step(i, dstates):
            n = NC - 1 - i
            rows = pl.ds(pl.multiple_of(n * CHUNK, CHUNK), CHUNK)
            q = q_ref[rows, :] * scale
            k = k_ref[rows, :]
            out = []
            dq_sum = dk_sum = None
            for e in range(2):
                cols = slice(e * LANES, (e + 1) * LANES)
                dq, dk, dv, dbeta, dg, dS = _chunk_bwd(q, k, v_ref[rows, cols], beta_ref[e, rows, :],
                                                       g_ref[e, rows, :], st_ref[e, n], t_ref[e, rows, :],
                                                       u_ref[e, rows, :], w_ref[e, rows, :],
                                                       do_ref[rows, cols], dstates[e])
                dv_ref[rows, cols] = dv
                dg_ref[e, rows, :] = dg
                dbeta_ref[e, rows, :] = dbeta
                dq_sum = dq if dq_sum is None else dq_sum + dq
                dk_sum = dk if dk_sum is None else dk_sum + dk
                out.append(dS)
            dq_ref[rows, :] = dq_sum * scale
            dk_ref[rows, :] = dk_sum
            return tuple(out)

        z = jnp.zeros((LANES, LANES), F32)
        lax.fori_loop(0, NC, step, (z, z))

    rep = pl.BlockSpec((2, S, LANES), lambda b, h: (h, b, 0))
    seq = pl.BlockSpec((S, LANES), lambda b, h: (b, h))
    seq2 = pl.BlockSpec((S, 2 * LANES), lambda b, h: (b, h))
    return pl.pallas_call(
        body, name=name, grid=(Bl, HQ),
        in_specs=[seq, pl.BlockSpec((S, LANES), lambda b, h: (b, HQ + h)),
                  pl.BlockSpec((S, 2 * LANES), lambda b, h: (b, HQ + h)), rep, rep,
                  pl.BlockSpec((None, 2, NC, LANES, LANES), lambda b, h: (b, h, 0, 0, 0)), rep, rep, rep, seq2],
        out_specs=[seq, seq, seq2, rep, rep],
        out_shape=[jax.ShapeDtypeStruct((Bl * S, KW), F32), jax.ShapeDtypeStruct((Bl * S, KW), F32),
                   jax.ShapeDtypeStruct((Bl * S, VW), F32),
                   jax.ShapeDtypeStruct((H, Bl * S, LANES), F32), jax.ShapeDtypeStruct((H, Bl * S, LANES), F32)],
        compiler_params=_cparams(("parallel", "parallel")),
    )(qkv, qkv, qkv, g, beta, states, tinv, u, w, do)


def _gdn_out_fwd(o, p, norm_g, out_a, *, CC, VW, name, tr=256):
    T = o.shape[0]
    tr = min(tr, T)
    H = VW // LANES
    zoff = p.shape[1] // VW - 1

    def body(o_ref, z_ref, ng_ref, a_ref, mix_ref):
        mix_ref[:, :CC] = a_ref[...]
        for h in range(H):
            cols = slice(h * LANES, (h + 1) * LANES)
            ov = o_ref[:, cols]
            r = lax.rsqrt(jnp.mean(ov * ov, axis=-1, keepdims=True) + EPS)
            mix_ref[:, CC + h * LANES:CC + (h + 1) * LANES] = (ov * r * ng_ref[...] * _silu(z_ref[:, cols])).astype(BF16)

    return pl.pallas_call(
        body, name=name, grid=(T // tr,),
        in_specs=[pl.BlockSpec((tr, VW), lambda i: (i, 0)), pl.BlockSpec((tr, VW), lambda i: (i, zoff)),
                  pl.BlockSpec((1, LANES), lambda i: (0, 0)), pl.BlockSpec((tr, CC), lambda i: (i, 0))],
        out_specs=pl.BlockSpec((tr, CC + VW), lambda i: (i, 0)),
        out_shape=jax.ShapeDtypeStruct((T, CC + VW), BF16),
        compiler_params=_cparams(("parallel",)),
    )(o, p, norm_g.reshape(1, LANES), out_a)


def _gdn_out_bwd(o, p, norm_g, dmix, *, CC, VW, name, tr=256):
    T = o.shape[0]
    tr = min(tr, T)
    H = VW // LANES
    zoff = p.shape[1] // VW - 1

    def body(o_ref, z_ref, ng_ref, dmix_ref, do_ref, dz_ref, da_ref, dng_ref, dpb_ref):
        first = pl.program_id(0) == 0
        da = dmix_ref[:, :CC]
        da_ref[...] = da.astype(BF16)
        _acc_row(dpb_ref, 0, jnp.sum(da, axis=0, keepdims=True), first)
        ng = ng_ref[...]
        dng = jnp.zeros((1, LANES), F32)
        for h in range(H):
            cols = slice(h * LANES, (h + 1) * LANES)
            ov = o_ref[:, cols]
            zv = z_ref[:, cols]
            dout = dmix_ref[:, CC + h * LANES:CC + (h + 1) * LANES]
            r = lax.rsqrt(jnp.mean(ov * ov, axis=-1, keepdims=True) + EPS)
            on = ov * r * ng
            don = dout * _silu(zv)
            dz_ref[:, cols] = (dout * on * _dsilu(zv)).astype(BF16)
            dng = dng + jnp.sum(don * ov * r, axis=0, keepdims=True)
            dong = don * ng
            do_ref[:, cols] = r * dong - ov * (r * r * r) * jnp.mean(dong * ov, axis=-1, keepdims=True)
        _acc_row(dng_ref, 0, dng, first)

    return pl.pallas_call(
        body, name=name, grid=(T // tr,),
        in_specs=[pl.BlockSpec((tr, VW), lambda i: (i, 0)), pl.BlockSpec((tr, VW), lambda i: (i, zoff)),
                  pl.BlockSpec((1, LANES), lambda i: (0, 0)), pl.BlockSpec((tr, CC + VW), lambda i: (i, 0))],
        out_specs=[pl.BlockSpec((tr, VW), lambda i: (i, 0)), pl.BlockSpec((tr, VW), lambda i: (i, 0)),
                   pl.BlockSpec((tr, CC), lambda i: (i, 0)), pl.BlockSpec((1, LANES), lambda i: (0, 0)),
                   pl.BlockSpec((1, CC), lambda i: (0, 0))],
        out_shape=[jax.ShapeDtypeStruct((T, VW), F32), jax.ShapeDtypeStruct((T, VW), BF16),
                   jax.ShapeDtypeStruct((T, CC), BF16), jax.ShapeDtypeStruct((1, LANES), F32),
                   jax.ShapeDtypeStruct((1, CC), F32)],
        compiler_params=_cparams(("arbitrary",)),
    )(o, p, norm_g.reshape(1, LANES), dmix)


FFN_CW = 256


def _ffn_act_fwd(gu, conv_w, conv_b, *, Bl, S, name):
    FF = gu.shape[2]
    cw = min(FFN_CW, FF)

    def body(g_ref, u_ref, w_ref, b_ref, a_ref):
        gc = _conv_fwd(g_ref[...], w_ref, FFN_CONV_K) + b_ref[...]
        a_ref[...] = (_silu(gc) * u_ref[...]).astype(BF16)

    return pl.pallas_call(
        body, name=name, grid=(Bl, FF // cw),
        in_specs=[pl.BlockSpec((None, S, cw), lambda b, j: (0, b, j)),
                  pl.BlockSpec((None, S, cw), lambda b, j: (1, b, j)),
                  pl.BlockSpec((FFN_CONV_K, cw), lambda b, j: (0, j)),
                  pl.BlockSpec((1, cw), lambda b, j: (0, j))],
        out_specs=pl.BlockSpec((S, cw), lambda b, j: (b, j)),
        out_shape=jax.ShapeDtypeStruct((Bl * S, FF), BF16),
        compiler_params=_cparams(("parallel", "parallel")),
    )(gu, gu, conv_w, conv_b.reshape(1, FF))


def _ffn_act_bwd(gu, conv_w, conv_b, da, *, Bl, S, name):
    FF = gu.shape[2]
    cw = min(FFN_CW, FF)

    def body(g_ref, u_ref, w_ref, b_ref, da_ref, dgu_ref, dw_ref, db_ref):
        first = pl.program_id(1) == 0
        gate = g_ref[...]
        gc = _conv_fwd(gate, w_ref, FFN_CONV_K) + b_ref[...]
        dav = da_ref[...]
        dgu_ref[1] = (dav * _silu(gc)).astype(BF16)
        dgc = dav * u_ref[...] * _dsilu(gc)
        _acc_row(db_ref, 0, jnp.sum(dgc, axis=0, keepdims=True), first)
        _conv_bwd_w(dgc, gate, dw_ref, FFN_CONV_K, first)
        dgu_ref[0] = _conv_bwd_in(dgc, w_ref, FFN_CONV_K).astype(BF16)

    return pl.pallas_call(
        body, name=name, grid=(FF // cw, Bl),
        in_specs=[pl.BlockSpec((None, S, cw), lambda j, b: (0, b, j)),
                  pl.BlockSpec((None, S, cw), lambda j, b: (1, b, j)),
                  pl.BlockSpec((FFN_CONV_K, cw), lambda j, b: (0, j)),
                  pl.BlockSpec((1, cw), lambda j, b: (0, j)),
                  pl.BlockSpec((S, cw), lambda j, b: (b, j))],
        out_specs=[pl.BlockSpec((2, S, cw), lambda j, b: (0, b, j)),
                   pl.BlockSpec((FFN_CONV_K, cw), lambda j, b: (0, j)),
                   pl.BlockSpec((1, cw), lambda j, b: (0, j))],
        out_shape=[jax.ShapeDtypeStruct((2, Bl * S, FF), BF16),
                   jax.ShapeDtypeStruct((FFN_CONV_K, FF), F32), jax.ShapeDtypeStruct((1, FF), F32)],
        compiler_params=_cparams(("parallel", "arbitrary")),
    )(gu, gu, conv_w, conv_b.reshape(1, FF), da)


def _layer_dims(W):
    CC = W["conv_pw_b"].shape[0]
    VW = W["mix_norm_g"].shape[0] - CC
    KW = (W["gdn_conv_w"].shape[1] - VW) // 2
    return CC, KW, VW


def _layer_fwd(l, x, W, Bl, S, fetch):
    CC, KW, VW = _layer_dims(W)
    H = VW // LANES
    w_in_t, w_in_ba = fetch(l, "w_in", x)
    n_main = (w_in_t.shape[0] // LANES) * LANES
    h1 = _rms_fwd(x, W["mix_norm_g"], name="rms1_fwd")
    p = _mm(h1, w_in_t, tb=True, b_rows=n_main, name="mm_in")
    pba = _mm(h1, w_in_ba, tb=True, name="mm_in_ba")
    u3, u1 = _conf_fwd(p, W["conv_dw_w"], W["conv_dw_b"], W["conv_ln_g"], W["conv_ln_b"], Bl=Bl, S=S, CC=CC,
                       name="conf_fwd")
    conv_pw_w = fetch(l, "conv_pw_w", u3)
    out_a = _mm(u3, conv_pw_w, bias=W["conv_pw_b"], out_dtype=BF16, name="mm_pw")
    qkv = _gdn_pre_fwd(p, W["gdn_conv_w"], Bl=Bl, S=S, CC=CC, KW=KW, VW=VW, name="gdn_pre_fwd")
    g, beta = _gdn_gate_fwd(pba, W["gdn_a_log"], W["gdn_dt_bias"], Bl=Bl, S=S, H=H, name="gdn_gate_fwd")
    o, states, tinv, gdn_u, gdn_w = _gdn_core_fwd(qkv, g, beta, Bl=Bl, S=S, KW=KW, VW=VW, name="gdn_core_fwd")
    mix = _gdn_out_fwd(o, p, W["gdn_norm_g"], out_a, CC=CC, VW=VW, name="gdn_out_fwd")
    w_out = fetch(l, "w_out", mix)
    x1 = _mm(mix, w_out, res=x, name="mm_out")
    h2 = _rms_fwd(x1, W["ffn_norm_g"], name="rms2_fwd")
    w_up = fetch(l, "w_up", h2)
    gu = _mm(h2, w_up, out_blocks=2, tn=w_up.shape[2], name="mm_up")
    a = _ffn_act_fwd(gu, W["ffn_conv_w"], W["ffn_conv_b"], Bl=Bl, S=S, name="ffn_act_fwd")
    w_down = fetch(l, "w_down", a)
    x2 = _mm(a, w_down, res=x1, name="mm_down")
    saved = dict(x=x, h1=h1, p=p, pba=pba, u1=u1, u3=u3, qkv=qkv, g=g, beta=beta, o=o, states=states, tinv=tinv,
                 gdn_u=gdn_u, gdn_w=gdn_w, mix=mix, x1=x1, h2=h2, gu=gu, a=a, w_in_t=w_in_t, w_in_ba=w_in_ba, conv_pw_w=conv_pw_w,
                 w_out=w_out, w_up=w_up, w_down=w_down)
    return x2, saved


def _layer_bwd(l, dx2, dx2b, W, A, Bl, S, sink):
    CC, KW, VW = _layer_dims(W)
    H = VW // LANES
    G = {}
    upw = A["w_up"].shape[2]
    da = _mm(dx2b, A["w_down"], tb=True, tn=upw, name="mm_down_dx")
    da = sink(l, "w_down", _mm(A["a"], dx2b, ta=True, out_dtype=BF16, tm=upw, name="mm_down_dw"), da)
    dgu, G["ffn_conv_w"], G["ffn_conv_b"] = _ffn_act_bwd(A["gu"], W["ffn_conv_w"], W["ffn_conv_b"], da,
                                                         Bl=Bl, S=S, name="ffn_act_bwd")
    dh2 = _mm(dgu, A["w_up"], tb=True, tk=upw, tn=2048, name="mm_up_dx")
    dh2 = sink(l, "w_up", _mm(A["h2"], dgu, ta=True, out_dtype=BF16, out_blocks=N_DEV, tn=upw, name="mm_up_dw"),
               dh2)
    dx1, dx1b, G["ffn_norm_g"] = _rms_bwd(A["x1"], W["ffn_norm_g"], dh2, dx2, name="rms2_bwd")
    dmix = _mm(dx1b, A["w_out"], tb=True, name="mm_out_dx")
    dmix = sink(l, "w_out", _mm(A["mix"], dx1b, ta=True, out_dtype=BF16, name="mm_out_dw"), dmix)
    do, dz, dout_a, G["gdn_norm_g"], G["conv_pw_b"] = _gdn_out_bwd(A["o"], A["p"], W["gdn_norm_g"], dmix,
                                                                   CC=CC, VW=VW, name="gdn_out_bwd")
    dq, dk, dv, dg, dbeta = _gdn_core_bwd(A["qkv"], A["g"], A["beta"], A["states"], A["tinv"], A["gdn_u"],
                                          A["gdn_w"], do, Bl=Bl, S=S, KW=KW, VW=VW, name="gdn_core_bwd")
    dpba, dalog, ddtb = _gdn_gate_bwd(A["pba"], W["gdn_a_log"], W["gdn_dt_bias"], dg, dbeta, Bl=Bl, S=S, H=H,
                                      name="gdn_gate_bwd")
    G["gdn_a_log"], G["gdn_dt_bias"] = dalog[:H, 0], ddtb[:H, 0]
    dqkv, G["gdn_conv_w"] = _gdn_pre_bwd(A["p"], W["gdn_conv_w"], dq, dk, dv, Bl=Bl, S=S, CC=CC, KW=KW, VW=VW,
                                         name="gdn_pre_bwd")
    du3 = _mm(dout_a, A["conv_pw_w"], tb=True, name="mm_pw_dx")
    du3 = sink(l, "conv_pw_w", _mm(A["u3"], dout_a, ta=True, out_dtype=BF16, name="mm_pw_dw"), du3)
    dav, dag, G["conv_dw_w"], G["conv_dw_b"], G["conv_ln_g"], G["conv_ln_b"] = _conf_bwd(
        A["p"], A["u1"], W["conv_dw_w"], W["conv_ln_g"], W["conv_ln_b"], du3, Bl=Bl, S=S, CC=CC, name="conf_bwd")
    dp = jnp.concatenate([dav, dag, dqkv, dz], axis=1)
    dp = sink(l, "w_in", (_mm(dp, A["h1"], ta=True, out_dtype=BF16, name="mm_in_dw"),
                          _mm(dpba, A["h1"], ta=True, out_dtype=BF16, name="mm_in_ba_dw")), dp)
    dh1 = _mm(dpba, A["w_in_ba"], name="mm_in_ba_dx")
    dh1 = _mm(dp, A["w_in_t"], b_rows=dp.shape[1], res=dh1, name="mm_in_dx")
    dx, dxb, G["mix_norm_g"] = _rms_bwd(A["x"], W["mix_norm_g"], dh1, dx1, name="rms1_bwd")
    return dx, dxb, G


def _local_step(x, target, Ws, final_norm_g, fetch, sink):
    Bl, S, D = x.shape
    xt = x.reshape(Bl * S, D)
    acts = []
    for l, W in enumerate(Ws):
        xt, A = _layer_fwd(l, xt, W, Bl, S, fetch)
        acts.append(A)
    loss, dx, dxb, dgf = _loss_head(xt, final_norm_g, target.reshape(Bl * S, D), name="loss_head")
    grads = [None] * len(Ws)
    for l in reversed(range(len(Ws))):
        dx, dxb, grads[l] = _layer_bwd(l, dx, dxb, Ws[l], acts[l], Bl, S, sink)
    return loss[0, 0], dx.reshape(Bl, S, D), grads, dgf.reshape(D)


def _mesh_pos():
    return lax.axis_index("x"), lax.axis_index("y"), lax.axis_index("c")


def _dev_index(px, py, pc):
    return 4 * px + 2 * py + pc


_ANY = pl.BlockSpec(memory_space=pl.ANY)


def _all_gather(arrs, *, name):
    n = len(arrs)

    def body(*refs):
        ins, outs = refs[:n], refs[n:2 * n]
        send_sems, recv_sems, local_sems = refs[2 * n:]
        x, y, c = _mesh_pos()
        me, sibling = (x, y, c), (x, y, 1 - c)
        chips = [(1 - x, y), (x, 1 - y), (1 - x, 1 - y)]

        def copy(a, k, block, to, src=None):
            dst = outs[a].at[_dev_index(*block)]
            return pltpu.make_async_remote_copy(
                src_ref=dst if src is None else src, dst_ref=dst,
                send_sem=send_sems.at[a, k], recv_sem=recv_sems.at[a, k],
                device_id=to, device_id_type=MESH)

        mine = [pltpu.make_async_copy(ins[a], outs[a].at[_dev_index(*me)], local_sems.at[a]) for a in range(n)]
        for cp in mine:
            cp.start()
        first = []
        for a in range(n):
            first.append(copy(a, 0, me, sibling, src=ins[a]))
            first += [copy(a, 1 + j, me, (*chip, c), src=ins[a]) for j, chip in enumerate(chips)]
        for cp in first:
            cp.start()
        passed = []
        for a in range(n):
            for j, chip in enumerate(chips):
                copy(a, 1 + j, (*chip, c), me).wait_recv()
                fwd = copy(a, 4 + j, (*chip, c), sibling)
                fwd.start()
                passed.append(fwd)
        for a in range(n):
            copy(a, 0, sibling, me).wait_recv()
            for j, chip in enumerate(chips):
                copy(a, 4 + j, (*chip, 1 - c), me).wait_recv()
        for cp in first + passed:
            cp.wait_send()
        for cp in mine:
            cp.wait()

    return pl.pallas_call(
        body, name=name,
        in_specs=[_ANY] * n, out_specs=[_ANY] * n,
        out_shape=[jax.ShapeDtypeStruct((N_DEV,) + a.shape, a.dtype) for a in arrs],
        scratch_shapes=[pltpu.SemaphoreType.DMA((n, N_DEV - 1)), pltpu.SemaphoreType.DMA((n, N_DEV - 1)),
                        pltpu.SemaphoreType.DMA((n,))],
    )(*arrs)


def _peers(x, y, c):
    flip = lambda v, f: 1 - v if f else v
    return [(flip(x, p & 4), flip(y, p & 2), flip(c, p & 1)) for p in range(1, N_DEV)]


GATHER_ID, SCATTER_ID = 1, 2
_SEQUENCER = dict(axis_name="sequencer", num_cores=1)


def _handshake(peers):
    barrier = pltpu.get_barrier_semaphore()
    for peer in peers:
        pl.semaphore_signal(barrier, inc=1, device_id=peer, device_id_type=MESH)
    pl.semaphore_wait(barrier, len(peers))


def _sc_gather(src, *, name):
    def body(src_ref, zone_ref, send_sems, recv_sems, local_sem):
        x, y, c = _mesh_pos()
        me, sibling = (x, y, c), (x, y, 1 - c)
        chips = [(1 - x, y), (x, 1 - y), (1 - x, 1 - y)]
        _handshake([sibling] + [(*chip, c) for chip in chips])

        def copy(k, block, to, from_src=False):
            dst = zone_ref.at[_dev_index(*block)]
            return pltpu.make_async_remote_copy(
                src_ref=src_ref if from_src else dst, dst_ref=dst, send_sem=send_sems.at[k], recv_sem=recv_sems.at[k],
                device_id=to, device_id_type=MESH)

        mine = pltpu.make_async_copy(src_ref, zone_ref.at[_dev_index(*me)], local_sem)
        mine.start()
        first = [copy(1 + j, me, (*chip, c), from_src=True) for j, chip in enumerate(chips)]
        first.append(copy(0, me, sibling, from_src=True))
        for cp in first:
            cp.start()
        passed = []
        for j, chip in enumerate(chips):
            copy(1 + j, (*chip, c), me).wait_recv()
            fwd = copy(4 + j, (*chip, c), sibling)
            fwd.start()
            passed.append(fwd)
        copy(0, sibling, me).wait_recv()
        for j, chip in enumerate(chips):
            copy(4 + j, (*chip, 1 - c), me).wait_recv()
        for cp in first + passed:
            cp.wait_send()
        mine.wait()

    return pl.kernel(
        body, name=name,
        out_type=jax.ShapeDtypeStruct((N_DEV,) + src.shape, src.dtype),
        mesh=plsc.ScalarSubcoreMesh(**_SEQUENCER),
        scratch_types=[pltpu.SemaphoreType.DMA((N_DEV - 1,)), pltpu.SemaphoreType.DMA((N_DEV - 1,)),
                       pltpu.SemaphoreType.DMA],
        compiler_params=pltpu.CompilerParams(collective_id=GATHER_ID),
    )(src)


def _sc_scatter(part, *, name):
    def body(src_ref, zone_ref, send_sems, recv_sems, local_sem):
        x, y, c = _mesh_pos()
        me = _dev_index(x, y, c)
        peers = _peers(x, y, c)
        _handshake(peers)
        mine = pltpu.make_async_copy(src_ref.at[me], zone_ref.at[me], local_sem)
        mine.start()
        sends = [pltpu.make_async_remote_copy(
            src_ref=src_ref.at[_dev_index(*peer)], dst_ref=zone_ref.at[me], send_sem=send_sems.at[k],
            recv_sem=recv_sems.at[k], device_id=peer, device_id_type=MESH) for k, peer in enumerate(peers)]
        for cp in sends:
            cp.start()
        for k, peer in enumerate(peers):
            pltpu.make_async_remote_copy(
                src_ref=src_ref.at[me], dst_ref=zone_ref.at[_dev_index(*peer)], send_sem=send_sems.at[k],
                recv_sem=recv_sems.at[k], device_id=peer, device_id_type=MESH).wait_recv()
        for cp in sends:
            cp.wait_send()
        mine.wait()

    return pl.kernel(
        body, name=name,
        out_type=jax.ShapeDtypeStruct(part.shape, part.dtype),
        mesh=plsc.ScalarSubcoreMesh(**_SEQUENCER),
        scratch_types=[pltpu.SemaphoreType.DMA((N_DEV - 1,)), pltpu.SemaphoreType.DMA((N_DEV - 1,)),
                       pltpu.SemaphoreType.DMA],
        compiler_params=pltpu.CompilerParams(collective_id=SCATTER_ID),
    )(part)


def _adamw_math(w, g, m, v):
    m2 = ADAM_B1 * m + (1.0 - ADAM_B1) * g
    v2 = ADAM_B2 * v + (1.0 - ADAM_B2) * (g * g)
    m_hat = m2 / (1.0 - ADAM_B1 ** ADAM_STEP)
    v_hat = v2 / (1.0 - ADAM_B2 ** ADAM_STEP)
    delta = -ADAM_LR * (m_hat / (jnp.sqrt(v_hat) + ADAM_EPS) + ADAM_WD * w)
    return delta, m2, v2


def _adamw_big(l, w, m, v, recv, prev, *, summed=False, name, tr=128):
    L, R, C = w.shape
    tr = next(t for t in range(min(tr, R), 0, -16) if R % t == 0)

    def body(w_ref, m_ref, v_ref, r_ref, *rest):
        g_ref, d_ref, m2_ref, v2_ref = rest[-4:]
        if summed:
            g = r_ref[...]
        else:
            g = r_ref[0].astype(F32)
            for s in range(1, N_DEV):
                g = g + r_ref[s].astype(F32)
        g_ref[...] = g
        d_ref[...], m2_ref[...], v2_ref[...] = _adamw_math(w_ref[...], g, m_ref[...], v_ref[...])

    wspec = pl.BlockSpec((None, tr, C), lambda i: (l, i, 0))
    rspec = pl.BlockSpec((tr, C), lambda i: (i, 0)) if summed else pl.BlockSpec((N_DEV, tr, C), lambda i: (0, i, 0))
    prev = list(prev) if prev is not None else []
    return pl.pallas_call(
        body, name=name, grid=(R // tr,),
        in_specs=[wspec, wspec, wspec, rspec] + [_ANY] * len(prev),
        out_specs=[wspec] * 4,
        out_shape=[jax.ShapeDtypeStruct((L, R, C), F32)] * 4,
        input_output_aliases={4 + j: j for j in range(len(prev))},
        compiler_params=_cparams(("parallel",)),
    )(w, m, v, recv if summed else recv.reshape(N_DEV, R, C), *prev)


def _sum_slots_wide(recv, *, name, tc=512):
    _, R, C = recv.shape
    tc = _tile(C, tc)

    def body(r_ref, o_ref):
        g = r_ref[0].astype(F32)
        for s in range(1, N_DEV):
            g = g + r_ref[s].astype(F32)
        o_ref[...] = g

    return pl.pallas_call(
        body, name=name, grid=(C // tc,),
        in_specs=[pl.BlockSpec((N_DEV, R, tc), lambda j: (0, 0, j))],
        out_specs=pl.BlockSpec((R, tc), lambda j: (0, j)),
        out_shape=jax.ShapeDtypeStruct((R, C), F32),
        compiler_params=_cparams(("parallel",)),
    )(recv)


def _sum_slots(gathered, *, name):
    _, R, C = gathered.shape

    def body(r_ref, o_ref):
        g = r_ref[0]
        for s in range(1, N_DEV):
            g = g + r_ref[s]
        o_ref[...] = g

    return pl.pallas_call(body, name=name, out_shape=jax.ShapeDtypeStruct((R, C), F32))(gathered)


def _adamw_small(w, g, m, v, *, name):
    def body(w_ref, g_ref, m_ref, v_ref, d_ref, m2_ref, v2_ref):
        d_ref[...], m2_ref[...], v2_ref[...] = _adamw_math(w_ref[...], g_ref[...], m_ref[...], v_ref[...])

    return pl.pallas_call(body, name=name, out_shape=[jax.ShapeDtypeStruct(w.shape, F32)] * 3)(w, g, m, v)


def _pack(arrs):
    flat = []
    for a in arrs:
        a = a.reshape(-1).astype(F32)
        flat.append(jnp.pad(a, (0, (-a.shape[0]) % LANES)))
    out = jnp.concatenate(flat)
    out = jnp.pad(out, (0, (-out.shape[0]) % (8 * LANES)))
    return out.reshape(-1, LANES)


def _unpack(packed, shapes):
    flat = packed.reshape(-1)
    out, pos = [], 0
    for s in shapes:
        size = math.prod(s)
        out.append(flat[pos:pos + size].reshape(s))
        pos += size + (-size) % LANES
    return out


BIG = ("w_in", "conv_pw_w", "w_out", "w_up", "w_down")
SMALL_SHARDED = ("conv_dw_w", "gdn_conv_w", "ffn_conv_w")
SMALL_REPLICATED = ("mix_norm_g", "conv_dw_b", "conv_ln_g", "conv_ln_b", "conv_pw_b", "gdn_a_log", "gdn_dt_bias",
                    "gdn_norm_g", "ffn_norm_g", "ffn_conv_b")
WEIGHTS = ("mix_norm_g", "w_in", "conv_dw_w", "conv_dw_b", "conv_ln_g", "conv_ln_b", "conv_pw_w", "conv_pw_b",
           "gdn_conv_w", "gdn_a_log", "gdn_dt_bias", "gdn_norm_g", "w_out", "ffn_norm_g", "w_up", "ffn_conv_w",
           "ffn_conv_b", "w_down", "final_norm_g")


def _train_step(x, target, w, m, v):
    L = w["w_in"].shape[0]
    D = x.shape[-1]
    xi, yi, ci = _mesh_pos()
    me = _dev_index(xi, yi, ci)

    gathered = {}

    def launch(l, after=None):
        for n in BIG:
            src = (w[n][l].T if n == "w_in" else w[n][l]).astype(BF16)
            if after is not None:
                src = lax.optimization_barrier((src, after))[0]
            gathered[n, l] = _sc_gather(src, name=f"gather_{n}_{l}")

    launch(0)
    small_full = {}
    for n, g_ in zip(SMALL_SHARDED, _all_gather([w[n] for n in SMALL_SHARDED], name="all_gather_conv_taps")):
        small_full[n] = jnp.moveaxis(g_, 0, 2).reshape(L, g_.shape[2], N_DEV * g_.shape[3])
    Ws = []
    for l in range(L):
        W = {n: w[n][l] for n in SMALL_REPLICATED}
        W.update({n: small_full[n][l] for n in SMALL_SHARDED})
        Ws.append(W)

    def fetch(l, n, after):
        if n == "conv_pw_w" and l + 1 < L:
            launch(l + 1, after)
        g_ = lax.optimization_barrier((gathered[n, l], after))[0]
        if n == "w_up":
            return g_
        g_ = g_.reshape(g_.shape[0] * g_.shape[1], g_.shape[2])
        if n == "w_in":
            n_main = (g_.shape[0] // LANES) * LANES
            return g_, jnp.pad(g_[n_main:], ((0, LANES - (g_.shape[0] - n_main)), (0, 0)))
        return g_

    started = []
    res = {}
    SCATTERS_IN_FLIGHT = 2

    def consume(chain):
        n, l, recv = started.pop(0)
        if chain is not None:
            recv, chain = lax.optimization_barrier((recv, chain))
        if n == "w_in":
            recv = _sum_slots_wide(recv, name="sum_w_in_grad").T
        res[n] = _adamw_big(l, w[n], m[n], v[n], recv, res.get(n), summed=(n == "w_in"), name=f"adamw_{n}")
        if chain is None:
            return None
        tied = lax.optimization_barrier((chain, *res[n]))
        res[n] = list(tied[1:])
        return tied[0]

    def sink(l, n, g_, chain):
        g_, chain = lax.optimization_barrier((g_, chain))
        if len(started) >= SCATTERS_IN_FLIGHT:
            chain = consume(chain)
        if n == "w_in":
            g_main, g_ba = g_
            g_ = jnp.concatenate([g_main, g_ba[:w["w_in"].shape[2] * N_DEV - g_main.shape[0]]], axis=0)
            part = g_.reshape(N_DEV, -1, D)
        elif n == "w_up":
            part = g_
        else:
            part = g_.reshape(N_DEV, -1, g_.shape[1])
        started.append((n, l, _sc_scatter(part, name=f"scatter_{n}_{l}")))
        return chain

    loss, grad_x, G, d_final = _local_step(x, target, Ws, w["final_norm_g"], fetch, sink)

    small_names = [n for n in WEIGHTS if n not in BIG]
    partial = []
    for n in small_names:
        if n == "final_norm_g":
            partial.append(d_final)
        else:
            partial.append(jnp.stack([G[l][n].reshape(Ws[l][n].shape) for l in range(L)]))
    partial.append(loss.reshape(1))
    packed = _pack(partial)
    if started:
        packed = lax.optimization_barrier((packed, started[-1][2]))[0]
    small_gathered = _sc_gather(packed, name="gather_small_grads")

    out = {k: {} for k in ("grad", "delta", "new_m", "new_v")}
    while started:
        grad_x = consume(grad_x)
    for n in BIG:
        for j, k in enumerate(("grad", "delta", "new_m", "new_v")):
            out[k][n] = res[n][j]

    summed = _unpack(_sum_slots(small_gathered, name="sum_small_grads"), [p_.shape for p_ in partial])
    full = dict(zip(small_names, summed))
    loss = summed[-1][0]
    for n in SMALL_SHARDED:
        width = w[n].shape[-1]
        full[n] = lax.dynamic_slice_in_dim(full[n], me * width, width, axis=2)
    loc_shapes = [w[n].shape for n in small_names]
    g_pack = _pack([full[n] for n in small_names])
    res = _adamw_small(_pack([w[n] for n in small_names]), g_pack, _pack([m[n] for n in small_names]),
                       _pack([v[n] for n in small_names]), name="adamw_small")
    for k, packed in zip(("grad", "delta", "new_m", "new_v"), (g_pack,) + tuple(res)):
        out[k].update(dict(zip(small_names, _unpack(packed, loc_shapes))))
    return loss, grad_x, out


def kernel(x, mix_norm_g, w_in, conv_dw_w, conv_dw_b, conv_ln_g, conv_ln_b, conv_pw_w, conv_pw_b, gdn_conv_w, gdn_a_log, gdn_dt_bias, gdn_norm_g, w_out, ffn_norm_g, w_up, ffn_conv_w, ffn_conv_b, w_down, final_norm_g, loss_target, m_mix_norm_g, m_w_in, m_conv_dw_w, m_conv_dw_b, m_conv_ln_g, m_conv_ln_b, m_conv_pw_w, m_conv_pw_b, m_gdn_conv_w, m_gdn_a_log, m_gdn_dt_bias, m_gdn_norm_g, m_w_out, m_ffn_norm_g, m_w_up, m_ffn_conv_w, m_ffn_conv_b, m_w_down, m_final_norm_g, v_mix_norm_g, v_w_in, v_conv_dw_w, v_conv_dw_b, v_conv_ln_g, v_conv_ln_b, v_conv_pw_w, v_conv_pw_b, v_gdn_conv_w, v_gdn_a_log, v_gdn_dt_bias, v_gdn_norm_g, v_w_out, v_ffn_norm_g, v_w_up, v_ffn_conv_w, v_ffn_conv_b, v_w_down, v_final_norm_g):
    w = dict(zip(WEIGHTS, (mix_norm_g, w_in, conv_dw_w, conv_dw_b, conv_ln_g, conv_ln_b, conv_pw_w, conv_pw_b, gdn_conv_w,
                           gdn_a_log, gdn_dt_bias, gdn_norm_g, w_out, ffn_norm_g, w_up, ffn_conv_w, ffn_conv_b, w_down,
                           final_norm_g)))
    m = dict(zip(WEIGHTS, (m_mix_norm_g, m_w_in, m_conv_dw_w, m_conv_dw_b, m_conv_ln_g, m_conv_ln_b, m_conv_pw_w,
                           m_conv_pw_b, m_gdn_conv_w, m_gdn_a_log, m_gdn_dt_bias, m_gdn_norm_g, m_w_out, m_ffn_norm_g,
                           m_w_up, m_ffn_conv_w, m_ffn_conv_b, m_w_down, m_final_norm_g)))
    v = dict(zip(WEIGHTS, (v_mix_norm_g, v_w_in, v_conv_dw_w, v_conv_dw_b, v_conv_ln_g, v_conv_ln_b, v_conv_pw_w,
                           v_conv_pw_b, v_gdn_conv_w, v_gdn_a_log, v_gdn_dt_bias, v_gdn_norm_g, v_w_out, v_ffn_norm_g,
                           v_w_up, v_ffn_conv_w, v_ffn_conv_b, v_w_down, v_final_norm_g)))
    loss, grad_x, out = _train_step(x, loss_target, w, m, v)
    return (loss, grad_x, *[out["grad"][n] for n in WEIGHTS], *[out["delta"][n] for n in WEIGHTS],
            *[out["new_m"][n] for n in WEIGHTS], *[out["new_v"][n] for n in WEIGHTS])
```

```python
import math

import jax
import jax.numpy as jnp
from jax import lax
from jax.experimental import pallas as pl
from jax.experimental.pallas import tpu as pltpu
from jax.experimental.pallas import tpu_sc as plsc

F32 = jnp.float32
BF16 = jnp.bfloat16
MESH = pl.DeviceIdType.MESH

EPS = 1e-6
LANES = 128
CHUNK = 128
NEAR_BLOCK = 32
CONV_K = 31
SHORT_CONV_K = 4
FFN_CONV_K = 3
N_DEV = 8
VMEM_LIMIT = 56 * 1024 * 1024

ADAM_LR = 0.001
ADAM_B1 = 0.9
ADAM_B2 = 0.999
ADAM_EPS = 1e-08
ADAM_WD = 0.01
ADAM_STEP = 10


def _cparams(sem):
    return pltpu.CompilerParams(dimension_semantics=sem, vmem_limit_bytes=VMEM_LIMIT)


def _sig(x):
    return 1.0 / (1.0 + jnp.exp(-x))


def _silu(x):
    return x * _sig(x)


def _dsilu(x):
    s = _sig(x)
    return s * (1.0 + x * (1.0 - s))


def _softplus(x):
    return jnp.maximum(x, 0.0) + jnp.log1p(jnp.exp(-jnp.abs(x)))


def _dot(a, b):
    return jnp.dot(a, b, preferred_element_type=F32)


def _dot_nt(a, b):
    return lax.dot_general(a, b, (((1,), (1,)), ((), ())), preferred_element_type=F32)


def _dot_tn(a, b):
    return lax.dot_general(a, b, (((0,), (0,)), ((), ())), preferred_element_type=F32)


def _bf(x):
    return x.astype(BF16)


_NN = (((1,), (0,)), ((), ()))
_TN = (((0,), (0,)), ((), ()))


def _split2(x):
    hi = _bf(x)
    return hi, _bf(x - hi.astype(F32))


def _dot_x3(a, b, dn=_NN):
    ah, al = _split2(a)
    bh, bl = _split2(b)
    f = lambda p, q: lax.dot_general(p, q, dn, preferred_element_type=F32)
    return f(ah, bh) + (f(al, bh) + f(ah, bl))


def _dot_mask(mask, x, dn=_NN):
    mb = _bf(mask)
    hi, lo = _split2(x)
    lo2 = _bf(x - hi.astype(F32) - lo.astype(F32))
    f = lambda q: lax.dot_general(mb, q, dn, preferred_element_type=F32)
    return f(hi) + (f(lo) + f(lo2))


def _shift_down(u, s):
    if s == 0:
        return u
    row = lax.broadcasted_iota(jnp.int32, u.shape, 0)
    return jnp.where(row >= s, pltpu.roll(u, s, 0), 0.0)


def _shift_up(u, s):
    if s == 0:
        return u
    n = u.shape[0]
    row = lax.broadcasted_iota(jnp.int32, u.shape, 0)
    return jnp.where(row < n - s, pltpu.roll(u, n - s, 0), 0.0)


def _conv_fwd(u, w_ref, K):
    acc = None
    for k in range(K):
        term = w_ref[k:k + 1, :] * _shift_down(u, K - 1 - k)
        acc = term if acc is None else acc + term
    return acc


def _conv_bwd_in(do, w_ref, K):
    acc = None
    for k in range(K):
        term = w_ref[k:k + 1, :] * _shift_up(do, K - 1 - k)
        acc = term if acc is None else acc + term
    return acc


def _conv_bwd_w(do, u, dw_ref, K, first):
    for k in range(K):
        row = jnp.sum(do * _shift_down(u, K - 1 - k), axis=0, keepdims=True)
        _acc_row(dw_ref, k, row, first)


def _acc_row(ref, k, row, first):
    @pl.when(first)
    def _():
        ref[k:k + 1, :] = row

    @pl.when(jnp.logical_not(first))
    def _():
        ref[k:k + 1, :] += row


def _logical(arr):
    if arr.ndim == 2:
        return arr.shape
    return (arr.shape[1], arr.shape[0] * arr.shape[2])


def _tile(dim, pref, *col_widths):
    if dim % LANES:
        assert not col_widths
        return dim
    t = (min(pref, dim) // LANES) * LANES
    while t > LANES and (dim % t or any(c % t for c in col_widths)):
        t -= LANES
    assert dim % t == 0 and all(c % t == 0 for c in col_widths), (dim, pref, col_widths)
    return t


def _spec(shape, rt, ct, rfn, cfn):
    if len(shape) == 2:
        return pl.BlockSpec((rt, ct), lambda i, j, k: (rfn(i, j, k), cfn(i, j, k)))
    per = shape[2] // ct
    return pl.BlockSpec((None, rt, ct),
                        lambda i, j, k: (cfn(i, j, k) // per, rfn(i, j, k), cfn(i, j, k) % per))


def _mm(a, b, *, name, ta=False, tb=False, out_dtype=F32, out_blocks=None, bias=None, res=None, b_rows=None,
        tm=1024, tn=1024, tk=2816):
    ra, ca = _logical(a)
    rb, cb = _logical(b)
    if b_rows is not None:
        assert b.ndim == 2 and b_rows <= rb
        rb = b_rows
    M, K = (ca, ra) if ta else (ra, ca)
    N, K2 = (rb, cb) if tb else (cb, rb)
    assert K == K2, (a.shape, b.shape, ta, tb)
    out_shape = (M, N) if out_blocks is None else (out_blocks, M, N // out_blocks)
    cw = lambda arr: [arr.shape[2]] if arr.ndim == 3 else []
    m_c = cw(a) if ta else []
    k_c = (cw(a) if not ta else []) + (cw(b) if tb else [])
    n_c = (cw(b) if not tb else []) + ([out_shape[2]] if out_blocks else []) + (cw(res) if res is not None else [])
    tm, tn, tk = _tile(M, tm, *m_c), _tile(N, tn, *n_c), _tile(K, tk, *k_c)
    nk = K // tk
    im, jn, kk = (lambda i, j, k: i), (lambda i, j, k: j), (lambda i, j, k: k)
    in_specs = [
        _spec(a.shape, tk, tm, kk, im) if ta else _spec(a.shape, tm, tk, im, kk),
        _spec(b.shape, tn, tk, jn, kk) if tb else _spec(b.shape, tk, tn, kk, jn),
    ]
    args = [a, b]
    if bias is not None:
        in_specs.append(pl.BlockSpec((1, tn), lambda i, j, k: (0, j)))
        args.append(bias.reshape(1, N).astype(F32))
    if res is not None:
        in_specs.append(_spec(res.shape, tm, tn, im, jn))
        args.append(res)
    dn = (((0 if ta else 1,), (1 if tb else 0,)), ((), ()))

    def body(*refs):
        a_ref, b_ref = refs[0], refs[1]
        pos = 2
        bias_ref = res_ref = None
        if bias is not None:
            bias_ref = refs[pos]
            pos += 1
        if res is not None:
            res_ref = refs[pos]
            pos += 1
        o_ref = refs[pos]
        k = pl.program_id(2)
        part = lax.dot_general(_bf(a_ref[...]), _bf(b_ref[...]), dn, preferred_element_type=F32)

        def finish(r):
            if bias_ref is not None:
                r = r + bias_ref[...]
            if res_ref is not None:
                r = r + res_ref[...].astype(F32)
            o_ref[...] = r.astype(out_dtype)

        if nk == 1:
            finish(part)
            return
        acc_ref = refs[pos + 1]

        @pl.when(k == 0)
        def _():
            acc_ref[...] = part

        @pl.when((k > 0) & (k < nk - 1))
        def _():
            acc_ref[...] += part

        @pl.when(k == nk - 1)
        def _():
            finish(acc_ref[...] + part)

    return pl.pallas_call(
        body, name=name,
        grid=(M // tm, N // tn, nk),
        in_specs=in_specs,
        out_specs=_spec(out_shape, tm, tn, im, jn),
        out_shape=jax.ShapeDtypeStruct(out_shape, out_dtype),
        scratch_shapes=[pltpu.VMEM((tm, tn), F32)] if nk > 1 else [],
        compiler_params=_cparams(("parallel", "parallel", "arbitrary")),
    )(*args)


def _rms_fwd(x, g, *, name, tr=512):
    T, D = x.shape
    tr = min(tr, T)

    def body(x_ref, g_ref, h_ref):
        xv = x_ref[...]
        r = lax.rsqrt(jnp.mean(xv * xv, axis=-1, keepdims=True) + EPS)
        h_ref[...] = (xv * r * g_ref[...]).astype(BF16)

    return pl.pallas_call(
        body, name=name, grid=(T // tr,),
        in_specs=[pl.BlockSpec((tr, D), lambda i: (i, 0)), pl.BlockSpec((1, D), lambda i: (0, 0))],
        out_specs=pl.BlockSpec((tr, D), lambda i: (i, 0)),
        out_shape=jax.ShapeDtypeStruct((T, D), BF16),
        compiler_params=_cparams(("parallel",)),
    )(x, g.reshape(1, D))


def _rms_bwd(x, g, dh, dres, *, name, tr=512):
    T, D = x.shape
    tr = min(tr, T)

    def body(x_ref, g_ref, dh_ref, dres_ref, dx_ref, dxb_ref, dg_ref):
        i = pl.program_id(0)
        xv = x_ref[...]
        dy = dh_ref[...].astype(F32)
        r = lax.rsqrt(jnp.mean(xv * xv, axis=-1, keepdims=True) + EPS)
        dyg = dy * g_ref[...]
        dot = jnp.mean(dyg * xv, axis=-1, keepdims=True)
        dx = dres_ref[...] + r * dyg - xv * (r * r * r) * dot
        dx_ref[...] = dx
        dxb_ref[...] = dx.astype(BF16)
        part = jnp.sum(dy * xv * r, axis=0, keepdims=True)
        _acc_row(dg_ref, 0, part, i == 0)

    row = pl.BlockSpec((tr, D), lambda i: (i, 0))
    vec = pl.BlockSpec((1, D), lambda i: (0, 0))
    return pl.pallas_call(
        body, name=name, grid=(T // tr,),
        in_specs=[row, vec, row, row],
        out_specs=[row, row, vec],
        out_shape=[jax.ShapeDtypeStruct((T, D), F32), jax.ShapeDtypeStruct((T, D), BF16),
                   jax.ShapeDtypeStruct((1, D), F32)],
        compiler_params=_cparams(("arbitrary",)),
    )(x, g.reshape(1, D), dh, dres)


def _loss_head(x, g, target, *, name, tr=512):
    T, D = x.shape
    tr = min(tr, T)

    def body(x_ref, g_ref, t_ref, loss_ref, dx_ref, dxb_ref, dg_ref):
        i = pl.program_id(0)
        xv = x_ref[...]
        gv = g_ref[...]
        r = lax.rsqrt(jnp.mean(xv * xv, axis=-1, keepdims=True) + EPS)
        y = xv * r * gv
        err = y - t_ref[...]
        lpart = 0.5 * jnp.sum(jnp.mean(err * err, axis=-1, keepdims=True), axis=0, keepdims=True)
        dy = err * (1.0 / D)
        dyg = dy * gv
        dot = jnp.mean(dyg * xv, axis=-1, keepdims=True)
        dx = r * dyg - xv * (r * r * r) * dot
        dx_ref[...] = dx
        dxb_ref[...] = dx.astype(BF16)
        _acc_row(dg_ref, 0, jnp.sum(dy * xv * r, axis=0, keepdims=True), i == 0)
        _acc_row(loss_ref, 0, jnp.broadcast_to(lpart, (1, LANES)), i == 0)

    row = pl.BlockSpec((tr, D), lambda i: (i, 0))
    return pl.pallas_call(
        body, name=name, grid=(T // tr,),
        in_specs=[row, pl.BlockSpec((1, D), lambda i: (0, 0)), row],
        out_specs=[pl.BlockSpec((1, LANES), lambda i: (0, 0)), row, row, pl.BlockSpec((1, D), lambda i: (0, 0))],
        out_shape=[jax.ShapeDtypeStruct((1, LANES), F32), jax.ShapeDtypeStruct((T, D), F32),
                   jax.ShapeDtypeStruct((T, D), BF16), jax.ShapeDtypeStruct((1, D), F32)],
        compiler_params=_cparams(("arbitrary",)),
    )(x, g.reshape(1, D), target)


HALO = 32
SUBLANES = 8


def _conv_dw_blocks(do, u, dw_ref, K, first, u_s, do_p):
    S, C = u.shape
    zeros = jnp.zeros((HALO, C), F32)
    for r in range(SUBLANES):
        u_s[r, 0:HALO, :] = zeros
        u_s[r, HALO:HALO + S, :] = _shift_down(u, r)
    do_p[0:HALO, :] = zeros
    do_p[HALO:HALO + S, :] = do
    do_p[HALO + S:2 * HALO + S, :] = zeros
    n_a = (K - 1) // SUBLANES + 1

    def block(i, accs):
        i0 = pl.multiple_of(i * SUBLANES, SUBLANES)
        us = [u_s[r, pl.ds(i0, SUBLANES), :] for r in range(SUBLANES)]
        ds = [do_p[pl.ds(i0 + SUBLANES * a, SUBLANES), :] for a in range(n_a)]
        out = list(accs)
        for a in range(n_a):
            for r in range(SUBLANES):
                s = SUBLANES * a + r
                if s < K:
                    out[K - 1 - s] = out[K - 1 - s] + ds[a] * us[r]
        return tuple(out)

    accs = lax.fori_loop(HALO // SUBLANES, (S + HALO) // SUBLANES, block, (jnp.zeros((SUBLANES, C), F32),) * K)
    for k in range(K):
        _acc_row(dw_ref, k, jnp.sum(accs[k], axis=0, keepdims=True), first)


def _conf_norm(u1, lg_ref, lb_ref):
    mu = jnp.mean(u1, axis=-1, keepdims=True)
    xc = u1 - mu
    r = lax.rsqrt(jnp.mean(xc * xc, axis=-1, keepdims=True) + EPS)
    n = xc * r
    return r, n, n * lg_ref[...] + lb_ref[...]


def _conf_fwd(p, dw_w, dw_b, ln_g, ln_b, *, Bl, S, CC, name):
    G = CC // LANES

    def body(av_ref, ag_ref, w_ref, b_ref, lg_ref, lb_ref, o_ref, u1_ref):
        u0 = av_ref[...] * _sig(ag_ref[...])
        u1 = _conv_fwd(u0, w_ref, CONV_K) + b_ref[...]
        u1_ref[...] = u1
        _, _, u2 = _conf_norm(u1, lg_ref, lb_ref)
        o_ref[...] = _silu(u2).astype(BF16)

    vec = pl.BlockSpec((1, LANES), lambda b, j: (0, j))
    seq = pl.BlockSpec((S, LANES), lambda b, j: (b, j))
    return pl.pallas_call(
        body, name=name, grid=(Bl, G),
        in_specs=[seq, pl.BlockSpec((S, LANES), lambda b, j: (b, G + j)),
                  pl.BlockSpec((CONV_K, LANES), lambda b, j: (0, j)), vec, vec, vec],
        out_specs=[seq, seq],
        out_shape=[jax.ShapeDtypeStruct((Bl * S, CC), BF16), jax.ShapeDtypeStruct((Bl * S, CC), F32)],
        compiler_params=_cparams(("parallel", "parallel")),
    )(p, p, dw_w, dw_b.reshape(1, CC), ln_g.reshape(1, CC), ln_b.reshape(1, CC))


def _conf_bwd(p, u1, dw_w, ln_g, ln_b, du3, *, Bl, S, CC, name):
    G = CC // LANES

    def body(av_ref, ag_ref, u1_ref, w_ref, lg_ref, lb_ref, du3_ref,
             dav_ref, dag_ref, dw_ref, db_ref, dlg_ref, dlb_ref, u_s, do_p):
        first = pl.program_id(1) == 0
        av = av_ref[...]
        sg = _sig(ag_ref[...])
        r, n, u2 = _conf_norm(u1_ref[...], lg_ref, lb_ref)
        du2 = du3_ref[...] * _dsilu(u2)
        _acc_row(dlg_ref, 0, jnp.sum(du2 * n, axis=0, keepdims=True), first)
        _acc_row(dlb_ref, 0, jnp.sum(du2, axis=0, keepdims=True), first)
        dn = du2 * lg_ref[...]
        du1 = r * (dn - jnp.mean(dn, axis=-1, keepdims=True) - n * jnp.mean(dn * n, axis=-1, keepdims=True))
        _acc_row(db_ref, 0, jnp.sum(du1, axis=0, keepdims=True), first)
        _conv_dw_blocks(du1, av * sg, dw_ref, CONV_K, first, u_s, do_p)
        du0 = _conv_bwd_in(du1, w_ref, CONV_K)
        dav_ref[...] = (du0 * sg).astype(BF16)
        dag_ref[...] = (du0 * av * sg * (1.0 - sg)).astype(BF16)

    vec = pl.BlockSpec((1, LANES), lambda j, b: (0, j))
    seq = pl.BlockSpec((S, LANES), lambda j, b: (b, j))
    return pl.pallas_call(
        body, name=name, grid=(G, Bl),
        in_specs=[seq, pl.BlockSpec((S, LANES), lambda j, b: (b, G + j)), seq,
                  pl.BlockSpec((CONV_K, LANES), lambda j, b: (0, j)), vec, vec, seq],
        out_specs=[seq, seq, pl.BlockSpec((CONV_K, LANES), lambda j, b: (0, j)), vec, vec, vec],
        out_shape=[jax.ShapeDtypeStruct((Bl * S, CC), BF16), jax.ShapeDtypeStruct((Bl * S, CC), BF16),
                   jax.ShapeDtypeStruct((CONV_K, CC), F32), jax.ShapeDtypeStruct((1, CC), F32),
                   jax.ShapeDtypeStruct((1, CC), F32), jax.ShapeDtypeStruct((1, CC), F32)],
        scratch_shapes=[pltpu.VMEM((SUBLANES, S + HALO, LANES), F32), pltpu.VMEM((S + 2 * HALO, LANES), F32)],
        compiler_params=_cparams(("parallel", "arbitrary")),
    )(p, p, u1, dw_w, ln_g.reshape(1, CC), ln_b.reshape(1, CC), du3)


def _gdn_pre_fwd(p, conv_w, *, Bl, S, CC, KW, VW, name):
    NQK = 2 * KW // LANES
    NB = NQK + VW // LANES
    off = 2 * CC // LANES

    def body(x_ref, w_ref, o_ref):
        j = pl.program_id(1)
        s = _silu(_conv_fwd(x_ref[...], w_ref, SHORT_CONV_K))
        r = lax.rsqrt(jnp.sum(s * s, axis=-1, keepdims=True) + EPS)
        o_ref[...] = jnp.where(j < NQK, s * r, s)

    return pl.pallas_call(
        body, name=name, grid=(Bl, NB),
        in_specs=[pl.BlockSpec((S, LANES), lambda b, j: (b, off + j)),
                  pl.BlockSpec((SHORT_CONV_K, LANES), lambda b, j: (0, j))],
        out_specs=pl.BlockSpec((S, LANES), lambda b, j: (b, j)),
        out_shape=jax.ShapeDtypeStruct((Bl * S, NB * LANES), F32),
        compiler_params=_cparams(("parallel", "parallel")),
    )(p, conv_w)


def _gdn_pre_bwd(p, conv_w, dq, dk, dv, *, Bl, S, CC, KW, VW, name):
    HQ = KW // LANES
    H = VW // LANES
    NQK = 2 * HQ
    NB = NQK + H
    off = 2 * CC // LANES

    def body(x_ref, w_ref, dq_ref, dk_ref, dv_ref, dx_ref, dw_ref):
        j = pl.program_id(0)
        first = pl.program_id(1) == 0
        xv = x_ref[...]
        c = _conv_fwd(xv, w_ref, SHORT_CONV_K)
        s = _silu(c)
        r = lax.rsqrt(jnp.sum(s * s, axis=-1, keepdims=True) + EPS)
        dy = jnp.where(j < HQ, dq_ref[...], jnp.where(j < NQK, dk_ref[...], dv_ref[...]))
        ds_norm = r * dy - s * (r * r * r) * jnp.sum(s * dy, axis=-1, keepdims=True)
        ds = jnp.where(j < NQK, ds_norm, dy)
        dc = ds * _dsilu(c)
        _conv_bwd_w(dc, xv, dw_ref, SHORT_CONV_K, first)
        dx_ref[...] = _conv_bwd_in(dc, w_ref, SHORT_CONV_K).astype(BF16)

    return pl.pallas_call(
        body, name=name, grid=(NB, Bl),
        in_specs=[pl.BlockSpec((S, LANES), lambda j, b: (b, off + j)),
                  pl.BlockSpec((SHORT_CONV_K, LANES), lambda j, b: (0, j)),
                  pl.BlockSpec((S, LANES), lambda j, b: (b, jnp.minimum(j, HQ - 1))),
                  pl.BlockSpec((S, LANES), lambda j, b: (b, jnp.clip(j - HQ, 0, HQ - 1))),
                  pl.BlockSpec((S, LANES), lambda j, b: (b, jnp.clip(j - NQK, 0, H - 1)))],
        out_specs=[pl.BlockSpec((S, LANES), lambda j, b: (b, j)),
                   pl.BlockSpec((SHORT_CONV_K, LANES), lambda j, b: (0, j))],
        out_shape=[jax.ShapeDtypeStruct((Bl * S, NB * LANES), BF16),
                   jax.ShapeDtypeStruct((SHORT_CONV_K, NB * LANES), F32)],
        compiler_params=_cparams(("parallel", "arbitrary")),
    )(p, conv_w, dq, dk, dv)


def _split3(x):
    hi, lo = _split2(x)
    return hi, lo, _bf(x - hi.astype(F32) - lo.astype(F32))


def _lane_replicate(parts, h):
    row = lax.broadcasted_iota(jnp.int32, (LANES, LANES), 0)
    E = jnp.where(row == h, 1.0, 0.0).astype(BF16)
    return _dot(parts[0], E) + (_dot(parts[1], E) + _dot(parts[2], E))


def _gdn_gate_fwd(pba, a_log, dt_bias, *, Bl, S, H, name):
    def body(alog_ref, dtb_ref, x_ref, g_ref, beta_ref):
        parts = _split3(x_ref[...])
        for h in range(H):
            b_raw = _lane_replicate(parts, h)
            a_raw = _lane_replicate(parts, H + h)
            beta_ref[h] = _sig(b_raw)
            ea = jnp.exp(jnp.zeros((1, LANES), F32) + alog_ref[h])
            g_ref[h] = -ea * _softplus(a_raw + dtb_ref[h])

    smem = pl.BlockSpec(memory_space=pltpu.SMEM)
    rep = pl.BlockSpec((H, S, LANES), lambda b: (0, b, 0))
    return pl.pallas_call(
        body, name=name, grid=(Bl,),
        in_specs=[smem, smem, pl.BlockSpec((S, LANES), lambda b: (b, 0))],
        out_specs=[rep, rep],
        out_shape=[jax.ShapeDtypeStruct((H, Bl * S, LANES), F32)] * 2,
        compiler_params=_cparams(("parallel",)),
    )(a_log, dt_bias, pba)


def _gdn_gate_bwd(pba, a_log, dt_bias, dg, dbeta, *, Bl, S, H, name):
    HP = 8 * ((H + 7) // 8)

    def body(alog_ref, dtb_ref, x_ref, dg_ref, dbeta_ref, dx_ref, dalog_ref, ddtb_ref):
        first = pl.program_id(0) == 0
        parts = _split3(x_ref[...])
        lane = lax.broadcasted_iota(jnp.int32, (S, LANES), 1)
        acc = jnp.zeros((S, LANES), F32)

        @pl.when(first)
        def _():
            dalog_ref[...] = jnp.zeros_like(dalog_ref)
            ddtb_ref[...] = jnp.zeros_like(ddtb_ref)

        for h in range(H):
            b_raw = _lane_replicate(parts, h)
            a_raw = _lane_replicate(parts, H + h)
            beta = _sig(b_raw)
            db_raw = dbeta_ref[h] * beta * (1.0 - beta)
            z = a_raw + dtb_ref[h]
            ea = jnp.exp(jnp.zeros((1, LANES), F32) + alog_ref[h])
            dgv = dg_ref[h]
            da_raw = dgv * (-ea) * _sig(z)
            g = -ea * _softplus(z)
            dalog_ref[h:h + 1, :] += jnp.sum(dgv * g, axis=0, keepdims=True)
            ddtb_ref[h:h + 1, :] += jnp.sum(da_raw, axis=0, keepdims=True)
            acc = acc + jnp.where(lane == h, db_raw, 0.0) + jnp.where(lane == H + h, da_raw, 0.0)
        dx_ref[...] = acc.astype(BF16)

    smem = pl.BlockSpec(memory_space=pltpu.SMEM)
    rep = pl.BlockSpec((H, S, LANES), lambda b: (0, b, 0))
    small = pl.BlockSpec((HP, LANES), lambda b: (0, 0))
    return pl.pallas_call(
        body, name=name, grid=(Bl,),
        in_specs=[smem, smem, pl.BlockSpec((S, LANES), lambda b: (b, 0)), rep, rep],
        out_specs=[pl.BlockSpec((S, LANES), lambda b: (b, 0)), small, small],
        out_shape=[jax.ShapeDtypeStruct((Bl * S, LANES), BF16),
                   jax.ShapeDtypeStruct((HP, LANES), F32), jax.ShapeDtypeStruct((HP, LANES), F32)],
        compiler_params=_cparams(("arbitrary",)),
    )(a_log, dt_bias, pba, dg, dbeta)


def _tri_masks():
    ri = lax.broadcasted_iota(jnp.int32, (CHUNK, CHUNK), 0)
    ci = lax.broadcasted_iota(jnp.int32, (CHUNK, CHUNK), 1)
    return ri >= ci, ri > ci, ri == CHUNK - 1


def _tri_inv(L):
    ri = lax.broadcasted_iota(jnp.int32, (CHUNK, CHUNK), 0)
    ci = lax.broadcasted_iota(jnp.int32, (CHUNK, CHUNK), 1)
    T = jnp.where(ri == ci, 1.0, 0.0) - jnp.where((ri >> 1) == (ci >> 1), L, 0.0)
    for lv in range(2, int(math.log2(CHUNK)) + 1):
        O = jnp.where(((ri >> lv) == (ci >> lv)) & ((ri >> (lv - 1)) != (ci >> (lv - 1))), L, 0.0)
        if (1 << lv) <= NEAR_BLOCK:
            T = T - _dot_x3(T, _dot_x3(O, T))
        else:
            Tb = _bf(T)
            T = T - _dot(Tb, _bf(_dot(_bf(O), Tb)))
    return T


def _chunk_local(q, k, v, beta, g):
    ge, gt, last = _tri_masks()
    gam = _dot_mask(ge, g)
    D = jnp.where(ge, jnp.exp(jnp.where(ge, gam - gam.T, 0.0)), 0.0)
    kb = k * beta
    vb = v * beta
    M = _dot_nt(_bf(kb), _bf(k))
    L = jnp.where(gt, M * D, 0.0)
    eg = jnp.exp(gam)
    kbg = kb * eg
    P = _dot_nt(_bf(q), _bf(k))
    QK = jnp.where(ge, P * D, 0.0)
    gl = jnp.sum(jnp.where(last, gam, 0.0), axis=0, keepdims=True)
    el = jnp.exp(gl - gam)
    return dict(ge=ge, gt=gt, last=last, gam=gam, D=D, kb=kb, vb=vb, L=L, eg=eg, kbg=kbg, QK=QK, gl=gl,
                el=el, kd=k * el, qg=q * eg)


def _rowsum(x):
    return jnp.sum(x, axis=-1, keepdims=True)


def _chunk_bwd(q, k, v, beta, g, S, T, u, w, do, dS2):
    c = _chunk_local(q, k, v, beta, g)
    ge, gt, last = c["ge"], c["gt"], c["last"]
    Sb = _bf(S)
    vn = u - _dot(w, Sb)
    dob, vnb, dS2b = _bf(do), _bf(vn), _bf(dS2)
    e_last = jnp.exp(c["gl"])
    dqg = _dot_nt(dob, Sb)
    dS = _dot_tn(_bf(c["qg"]), dob)
    dQK = jnp.where(ge, _dot_nt(dob, vnb), 0.0)
    dvn = _dot_tn(_bf(c["QK"]), dob)
    dS = dS + dS2 * e_last
    de_last = jnp.sum(jnp.sum(dS2 * S, axis=0, keepdims=True), axis=1, keepdims=True)
    dkd = _dot_nt(vnb, dS2b)
    dvn = dvn + _dot(_bf(c["kd"]), dS2b)
    dvnb = _bf(dvn)
    dw = -_dot_nt(dvnb, Sb)
    dS = dS - _dot_tn(w, dvnb)
    dsol = _dot_x3(T, jnp.concatenate([dvn, dw], axis=1), _TN)
    dvb, dkbg = dsol[:, :LANES], dsol[:, LANES:]
    dA = -(_dot_nt(_bf(dvb), _bf(u)) + _dot_nt(_bf(dkbg), w))
    dL = jnp.where(gt, dA, 0.0)
    dM = dL * c["D"]
    dP = dQK * c["D"]
    E = dL * c["L"] + dQK * c["QK"]
    kbf = _bf(k)
    dkb = _dot(_bf(dM), kbf) + dkbg * c["eg"]
    dk = _dot_tn(_bf(dM), _bf(c["kb"])) + _dot_tn(_bf(dP), _bf(q)) + dkd * c["el"] + dkb * beta
    dq = _dot(_bf(dP), kbf) + dqg * c["eg"]
    s_kd = _rowsum(dkd * c["kd"])
    dgam = (_rowsum(E) - _rowsum(E.T) + _rowsum(dqg * c["qg"]) - s_kd + _rowsum(dkbg * c["kbg"]))
    dgl = jnp.sum(s_kd, axis=0, keepdims=True) + de_last * e_last
    dgam_rep = jnp.broadcast_to(dgam, (CHUNK, LANES)) + jnp.where(last, jnp.broadcast_to(dgl, (CHUNK, LANES)), 0.0)
    dg_rep = _dot_mask(ge, dgam_rep, _TN)
    dbeta = _rowsum(dkb * k) + _rowsum(dvb * v)
    dv = dvb * beta
    return dq, dk, dv, jnp.broadcast_to(dbeta, (CHUNK, LANES)), dg_rep, dS


def _gdn_core_fwd(qkv, g, beta, *, Bl, S, KW, VW, name):
    HQ = KW // LANES
    H = VW // LANES
    NC = S // CHUNK
    scale = float(LANES) ** -0.5

    PAIR = next(p for p in (4, 2, 1) if NC % p == 0)

    def body(q_ref, k_ref, v_ref, g_ref, beta_ref, o_ref, st_ref, t_ref, u_s, w_s,
             qk_s, qg_s, kd_s, el_s):
        def local(n2, carry):
            for half in range(PAIR):
                n = n2 * PAIR + half
                rows = pl.ds(pl.multiple_of(n * CHUNK, CHUNK), CHUNK)
                q = q_ref[rows, :] * scale
                k = k_ref[rows, :]
                for e in range(2):
                    c = _chunk_local(q, k, v_ref[rows, e * LANES:(e + 1) * LANES], beta_ref[e, rows, :],
                                     g_ref[e, rows, :])
                    T = _tri_inv(c["L"])
                    t_ref[e, rows, :] = T
                    uw = _dot_x3(T, jnp.concatenate([c["vb"], c["kbg"]], axis=1))
                    u_s[e, rows, :] = uw[:, :LANES]
                    w_s[e, rows, :] = _bf(uw[:, LANES:])
                    qk_s[e, rows, :] = _bf(c["QK"])
                    qg_s[e, rows, :] = _bf(c["qg"])
                    kd_s[e, rows, :] = _bf(c["kd"])
                    el_s[e, pl.ds(pl.multiple_of(n * 8, 8), 8), :] = jnp.broadcast_to(jnp.exp(c["gl"]), (8, LANES))
            return carry

        lax.fori_loop(0, NC // PAIR, local, 0)

        def scan(n, states):
            rows = pl.ds(pl.multiple_of(n * CHUNK, CHUNK), CHUNK)
            out = []
            for e in range(2):
                S_in = states[e]
                st_ref[e, n] = S_in
                Sb = _bf(S_in)
                vn = u_s[e, rows, :] - _dot(w_s[e, rows, :], Sb)
                vnb = _bf(vn)
                o_ref[rows, e * LANES:(e + 1) * LANES] = _dot(qg_s[e, rows, :], Sb) + _dot(qk_s[e, rows, :], vnb)
                e_last = el_s[e, pl.ds(pl.multiple_of(n * 8, 8), 1), :]
                out.append(S_in * e_last + _dot_tn(kd_s[e, rows, :], vnb))
            return tuple(out)

        z = jnp.zeros((LANES, LANES), F32)
        lax.fori_loop(0, NC, scan, (z, z))

    rep = pl.BlockSpec((2, S, LANES), lambda b, h: (h, b, 0))
    return pl.pallas_call(
        body, name=name, grid=(Bl, HQ),
        in_specs=[pl.BlockSpec((S, LANES), lambda b, h: (b, h)),
                  pl.BlockSpec((S, LANES), lambda b, h: (b, HQ + h)),
                  pl.BlockSpec((S, 2 * LANES), lambda b, h: (b, HQ + h)), rep, rep],
        out_specs=[pl.BlockSpec((S, 2 * LANES), lambda b, h: (b, h)),
                   pl.BlockSpec((None, 2, NC, LANES, LANES), lambda b, h: (b, h, 0, 0, 0)), rep, rep, rep],
        out_shape=[jax.ShapeDtypeStruct((Bl * S, VW), F32),
                   jax.ShapeDtypeStruct((Bl, H, NC, LANES, LANES), F32),
                   jax.ShapeDtypeStruct((H, Bl * S, LANES), F32),
                   jax.ShapeDtypeStruct((H, Bl * S, LANES), F32),
                   jax.ShapeDtypeStruct((H, Bl * S, LANES), BF16)],
        scratch_shapes=[pltpu.VMEM((2, S, LANES), BF16)] * 3 + [pltpu.VMEM((2, NC * 8, LANES), F32)],
        compiler_params=_cparams(("parallel", "parallel")),
    )(qkv, qkv, qkv, g, beta)


def _gdn_core_bwd(qkv, g, beta, states, tinv, u, w, do, *, Bl, S, KW, VW, name):
    HQ = KW // LANES
    H = VW // LANES
    NC = S // CHUNK
    scale = float(LANES) ** -0.5

    def body(q_ref, k_ref, v_ref, g_ref, beta_ref, st_ref, t_ref, u_ref, w_ref, do_ref,
             dq_ref, dk_ref, dv_ref, dg_ref, dbeta_ref):
        def step(i, dstates):
            n = NC - 1 - i
            rows = pl.ds(pl.multiple_of(n * CHUNK, CHUNK), CHUNK)
            q = q_ref[rows, :] * scale
            k = k_ref[rows, :]
            out = []
            dq_sum = dk_sum = None
            for e in range(2):
                cols = slice(e * LANES, (e + 1) * LANES)
                dq, dk, dv, dbeta, dg, dS = _chunk_bwd(q, k, v_ref[rows, cols], beta_ref[e, rows, :],
                                                       g_ref[e, rows, :], st_ref[e, n], t_ref[e, rows, :],
                                                       u_ref[e, rows, :], w_ref[e, rows, :],
                                                       do_ref[rows, cols], dstates[e])
                dv_ref[rows, cols] = dv
                dg_ref[e, rows, :] = dg
                dbeta_ref[e, rows, :] = dbeta
                dq_sum = dq if dq_sum is None else dq_sum + dq
                dk_sum = dk if dk_sum is None else dk_sum + dk
                out.append(dS)
            dq_ref[rows, :] = dq_sum * scale
            dk_ref[rows, :] = dk_sum
            return tuple(out)

        z = jnp.zeros((LANES, LANES), F32)
        lax.fori_loop(0, NC, step, (z, z))

    rep = pl.BlockSpec((2, S, LANES), lambda b, h: (h, b, 0))
    seq = pl.BlockSpec((S, LANES), lambda b, h: (b, h))
    seq2 = pl.BlockSpec((S, 2 * LANES), lambda b, h: (b, h))
    return pl.pallas_call(
        body, name=name, grid=(Bl, HQ),
        in_specs=[seq, pl.BlockSpec((S, LANES), lambda b, h: (b, HQ + h)),
                  pl.BlockSpec((S, 2 * LANES), lambda b, h: (b, HQ + h)), rep, rep,
                  pl.BlockSpec((None, 2, NC, LANES, LANES), lambda b, h: (b, h, 0, 0, 0)), rep, rep, rep, seq2],
        out_specs=[seq, seq, seq2, rep, rep],
        out_shape=[jax.ShapeDtypeStruct((Bl * S, KW), F32), jax.ShapeDtypeStruct((Bl * S, KW), F32),
                   jax.ShapeDtypeStruct((Bl * S, VW), F32),
                   jax.ShapeDtypeStruct((H, Bl * S, LANES), F32), jax.ShapeDtypeStruct((H, Bl * S, LANES), F32)],
        compiler_params=_cparams(("parallel", "parallel")),
    )(qkv, qkv, qkv, g, beta, states, tinv, u, w, do)


def _gdn_out_fwd(o, p, norm_g, out_a, *, CC, VW, name, tr=256):
    T = o.shape[0]
    tr = min(tr, T)
    H = VW // LANES
    zoff = p.shape[1] // VW - 1

    def body(o_ref, z_ref, ng_ref, a_ref, mix_ref):
        mix_ref[:, :CC] = a_ref[...]
        for h in range(H):
            cols = slice(h * LANES, (h + 1) * LANES)
            ov = o_ref[:, cols]
            r = lax.rsqrt(jnp.mean(ov * ov, axis=-1, keepdims=True) + EPS)
            mix_ref[:, CC + h * LANES:CC + (h + 1) * LANES] = (ov * r * ng_ref[...] * _silu(z_ref[:, cols])).astype(BF16)

    return pl.pallas_call(
        body, name=name, grid=(T // tr,),
        in_specs=[pl.BlockSpec((tr, VW), lambda i: (i, 0)), pl.BlockSpec((tr, VW), lambda i: (i, zoff)),
                  pl.BlockSpec((1, LANES), lambda i: (0, 0)), pl.BlockSpec((tr, CC), lambda i: (i, 0))],
        out_specs=pl.BlockSpec((tr, CC + VW), lambda i: (i, 0)),
        out_shape=jax.ShapeDtypeStruct((T, CC + VW), BF16),
        compiler_params=_cparams(("parallel",)),
    )(o, p, norm_g.reshape(1, LANES), out_a)


def _gdn_out_bwd(o, p, norm_g, dmix, *, CC, VW, name, tr=256):
    T = o.shape[0]
    tr = min(tr, T)
    H = VW // LANES
    zoff = p.shape[1] // VW - 1

    def body(o_ref, z_ref, ng_ref, dmix_ref, do_ref, dz_ref, da_ref, dng_ref, dpb_ref):
        first = pl.program_id(0) == 0
        da = dmix_ref[:, :CC]
        da_ref[...] = da.astype(BF16)
        _acc_row(dpb_ref, 0, jnp.sum(da, axis=0, keepdims=True), first)
        ng = ng_ref[...]
        dng = jnp.zeros((1, LANES), F32)
        for h in range(H):
            cols = slice(h * LANES, (h + 1) * LANES)
            ov = o_ref[:, cols]
            zv = z_ref[:, cols]
            dout = dmix_ref[:, CC + h * LANES:CC + (h + 1) * LANES]
            r = lax.rsqrt(jnp.mean(ov * ov, axis=-1, keepdims=True) + EPS)
            on = ov * r * ng
            don = dout * _silu(zv)
            dz_ref[:, cols] = (dout * on * _dsilu(zv)).astype(BF16)
            dng = dng + jnp.sum(don * ov * r, axis=0, keepdims=True)
            dong = don * ng
            do_ref[:, cols] = r * dong - ov * (r * r * r) * jnp.mean(dong * ov, axis=-1, keepdims=True)
        _acc_row(dng_ref, 0, dng, first)

    return pl.pallas_call(
        body, name=name, grid=(T // tr,),
        in_specs=[pl.BlockSpec((tr, VW), lambda i: (i, 0)), pl.BlockSpec((tr, VW), lambda i: (i, zoff)),
                  pl.BlockSpec((1, LANES), lambda i: (0, 0)), pl.BlockSpec((tr, CC + VW), lambda i: (i, 0))],
        out_specs=[pl.BlockSpec((tr, VW), lambda i: (i, 0)), pl.BlockSpec((tr, VW), lambda i: (i, 0)),
                   pl.BlockSpec((tr, CC), lambda i: (i, 0)), pl.BlockSpec((1, LANES), lambda i: (0, 0)),
                   pl.BlockSpec((1, CC), lambda i: (0, 0))],
        out_shape=[jax.ShapeDtypeStruct((T, VW), F32), jax.ShapeDtypeStruct((T, VW), BF16),
                   jax.ShapeDtypeStruct((T, CC), BF16), jax.ShapeDtypeStruct((1, LANES), F32),
                   jax.ShapeDtypeStruct((1, CC), F32)],
        compiler_params=_cparams(("arbitrary",)),
    )(o, p, norm_g.reshape(1, LANES), dmix)


FFN_CW = 256


def _ffn_act_fwd(gu, conv_w, conv_b, *, Bl, S, name):
    FF = gu.shape[2]
    cw = min(FFN_CW, FF)

    def body(g_ref, u_ref, w_ref, b_ref, a_ref):
        gc = _conv_fwd(g_ref[...], w_ref, FFN_CONV_K) + b_ref[...]
        a_ref[...] = (_silu(gc) * u_ref[...]).astype(BF16)

    return pl.pallas_call(
        body, name=name, grid=(Bl, FF // cw),
        in_specs=[pl.BlockSpec((None, S, cw), lambda b, j: (0, b, j)),
                  pl.BlockSpec((None, S, cw), lambda b, j: (1, b, j)),
                  pl.BlockSpec((FFN_CONV_K, cw), lambda b, j: (0, j)),
                  pl.BlockSpec((1, cw), lambda b, j: (0, j))],
        out_specs=pl.BlockSpec((S, cw), lambda b, j: (b, j)),
        out_shape=jax.ShapeDtypeStruct((Bl * S, FF), BF16),
        compiler_params=_cparams(("parallel", "parallel")),
    )(gu, gu, conv_w, conv_b.reshape(1, FF))


def _ffn_act_bwd(gu, conv_w, conv_b, da, *, Bl, S, name):
    FF = gu.shape[2]
    cw = min(FFN_CW, FF)

    def body(g_ref, u_ref, w_ref, b_ref, da_ref, dgu_ref, dw_ref, db_ref):
        first = pl.program_id(1) == 0
        gate = g_ref[...]
        gc = _conv_fwd(gate, w_ref, FFN_CONV_K) + b_ref[...]
        dav = da_ref[...]
        dgu_ref[1] = (dav * _silu(gc)).astype(BF16)
        dgc = dav * u_ref[...] * _dsilu(gc)
        _acc_row(db_ref, 0, jnp.sum(dgc, axis=0, keepdims=True), first)
        _conv_bwd_w(dgc, gate, dw_ref, FFN_CONV_K, first)
        dgu_ref[0] = _conv_bwd_in(dgc, w_ref, FFN_CONV_K).astype(BF16)

    return pl.pallas_call(
        body, name=name, grid=(FF // cw, Bl),
        in_specs=[pl.BlockSpec((None, S, cw), lambda j, b: (0, b, j)),
                  pl.BlockSpec((None, S, cw), lambda j, b: (1, b, j)),
                  pl.BlockSpec((FFN_CONV_K, cw), lambda j, b: (0, j)),
                  pl.BlockSpec((1, cw), lambda j, b: (0, j)),
                  pl.BlockSpec((S, cw), lambda j, b: (b, j))],
        out_specs=[pl.BlockSpec((2, S, cw), lambda j, b: (0, b, j)),
                   pl.BlockSpec((FFN_CONV_K, cw), lambda j, b: (0, j)),
                   pl.BlockSpec((1, cw), lambda j, b: (0, j))],
        out_shape=[jax.ShapeDtypeStruct((2, Bl * S, FF), BF16),
                   jax.ShapeDtypeStruct((FFN_CONV_K, FF), F32), jax.ShapeDtypeStruct((1, FF), F32)],
        compiler_params=_cparams(("parallel", "arbitrary")),
    )(gu, gu, conv_w, conv_b.reshape(1, FF), da)


def _layer_dims(W):
    CC = W["conv_pw_b"].shape[0]
    VW = W["mix_norm_g"].shape[0] - CC
    KW = (W["gdn_conv_w"].shape[1] - VW) // 2
    return CC, KW, VW


def _layer_fwd(l, x, W, Bl, S, fetch):
    CC, KW, VW = _layer_dims(W)
    H = VW // LANES
    w_in_t, w_in_ba = fetch(l, "w_in", x)
    n_main = (w_in_t.shape[0] // LANES) * LANES
    h1 = _rms_fwd(x, W["mix_norm_g"], name="rms1_fwd")
    p = _mm(h1, w_in_t, tb=True, b_rows=n_main, name="mm_in")
    pba = _mm(h1, w_in_ba, tb=True, name="mm_in_ba")
    u3, u1 = _conf_fwd(p, W["conv_dw_w"], W["conv_dw_b"], W["conv_ln_g"], W["conv_ln_b"], Bl=Bl, S=S, CC=CC,
                       name="conf_fwd")
    conv_pw_w = fetch(l, "conv_pw_w", u3)
    out_a = _mm(u3, conv_pw_w, bias=W["conv_pw_b"], out_dtype=BF16, name="mm_pw")
    qkv = _gdn_pre_fwd(p, W["gdn_conv_w"], Bl=Bl, S=S, CC=CC, KW=KW, VW=VW, name="gdn_pre_fwd")
    g, beta = _gdn_gate_fwd(pba, W["gdn_a_log"], W["gdn_dt_bias"], Bl=Bl, S=S, H=H, name="gdn_gate_fwd")
    o, states, tinv, gdn_u, gdn_w = _gdn_core_fwd(qkv, g, beta, Bl=Bl, S=S, KW=KW, VW=VW, name="gdn_core_fwd")
    mix = _gdn_out_fwd(o, p, W["gdn_norm_g"], out_a, CC=CC, VW=VW, name="gdn_out_fwd")
    w_out = fetch(l, "w_out", mix)
    x1 = _mm(mix, w_out, res=x, name="mm_out")
    h2 = _rms_fwd(x1, W["ffn_norm_g"], name="rms2_fwd")
    w_up = fetch(l, "w_up", h2)
    gu = _mm(h2, w_up, out_blocks=2, tn=w_up.shape[2], name="mm_up")
    a = _ffn_act_fwd(gu, W["ffn_conv_w"], W["ffn_conv_b"], Bl=Bl, S=S, name="ffn_act_fwd")
    w_down = fetch(l, "w_down", a)
    x2 = _mm(a, w_down, res=x1, name="mm_down")
    saved = dict(x=x, h1=h1, p=p, pba=pba, u1=u1, u3=u3, qkv=qkv, g=g, beta=beta, o=o, states=states, tinv=tinv,
                 gdn_u=gdn_u, gdn_w=gdn_w, mix=mix, x1=x1, h2=h2, gu=gu, a=a, w_in_t=w_in_t, w_in_ba=w_in_ba, conv_pw_w=conv_pw_w,
                 w_out=w_out, w_up=w_up, w_down=w_down)
    return x2, saved


def _layer_bwd(l, dx2, dx2b, W, A, Bl, S, sink):
    CC, KW, VW = _layer_dims(W)
    H = VW // LANES
    G = {}
    upw = A["w_up"].shape[2]
    da = _mm(dx2b, A["w_down"], tb=True, tn=upw, name="mm_down_dx")
    da = sink(l, "w_down", _mm(A["a"], dx2b, ta=True, out_dtype=BF16, tm=upw, name="mm_down_dw"), da)
    dgu, G["ffn_conv_w"], G["ffn_conv_b"] = _ffn_act_bwd(A["gu"], W["ffn_conv_w"], W["ffn_conv_b"], da,
                                                         Bl=Bl, S=S, name="ffn_act_bwd")
    dh2 = _mm(dgu, A["w_up"], tb=True, tk=upw, tn=2048, name="mm_up_dx")
    dh2 = sink(l, "w_up", _mm(A["h2"], dgu, ta=True, out_dtype=BF16, out_blocks=N_DEV, tn=upw, name="mm_up_dw"),
               dh2)
    dx1, dx1b, G["ffn_norm_g"] = _rms_bwd(A["x1"], W["ffn_norm_g"], dh2, dx2, name="rms2_bwd")
    dmix = _mm(dx1b, A["w_out"], tb=True, name="mm_out_dx")
    dmix = sink(l, "w_out", _mm(A["mix"], dx1b, ta=True, out_dtype=BF16, name="mm_out_dw"), dmix)
    do, dz, dout_a, G["gdn_norm_g"], G["conv_pw_b"] = _gdn_out_bwd(A["o"], A["p"], W["gdn_norm_g"], dmix,
                                                                   CC=CC, VW=VW, name="gdn_out_bwd")
    dq, dk, dv, dg, dbeta = _gdn_core_bwd(A["qkv"], A["g"], A["beta"], A["states"], A["tinv"], A["gdn_u"],
                                          A["gdn_w"], do, Bl=Bl, S=S, KW=KW, VW=VW, name="gdn_core_bwd")
    dpba, dalog, ddtb = _gdn_gate_bwd(A["pba"], W["gdn_a_log"], W["gdn_dt_bias"], dg, dbeta, Bl=Bl, S=S, H=H,
                                      name="gdn_gate_bwd")
    G["gdn_a_log"], G["gdn_dt_bias"] = dalog[:H, 0], ddtb[:H, 0]
    dqkv, G["gdn_conv_w"] = _gdn_pre_bwd(A["p"], W["gdn_conv_w"], dq, dk, dv, Bl=Bl, S=S, CC=CC, KW=KW, VW=VW,
                                         name="gdn_pre_bwd")
    du3 = _mm(dout_a, A["conv_pw_w"], tb=True, name="mm_pw_dx")
    du3 = sink(l, "conv_pw_w", _mm(A["u3"], dout_a, ta=True, out_dtype=BF16, name="mm_pw_dw"), du3)
    dav, dag, G["conv_dw_w"], G["conv_dw_b"], G["conv_ln_g"], G["conv_ln_b"] = _conf_bwd(
        A["p"], A["u1"], W["conv_dw_w"], W["conv_ln_g"], W["conv_ln_b"], du3, Bl=Bl, S=S, CC=CC, name="conf_bwd")
    dp = jnp.concatenate([dav, dag, dqkv, dz], axis=1)
    dp = sink(l, "w_in", (_mm(dp, A["h1"], ta=True, out_dtype=BF16, name="mm_in_dw"),
                          _mm(dpba, A["h1"], ta=True, out_dtype=BF16, name="mm_in_ba_dw")), dp)
    dh1 = _mm(dpba, A["w_in_ba"], name="mm_in_ba_dx")
    dh1 = _mm(dp, A["w_in_t"], b_rows=dp.shape[1], res=dh1, name="mm_in_dx")
    dx, dxb, G["mix_norm_g"] = _rms_bwd(A["x"], W["mix_norm_g"], dh1, dx1, name="rms1_bwd")
    return dx, dxb, G


def _local_step(x, target, Ws, final_norm_g, fetch, sink):
    Bl, S, D = x.shape
    xt = x.reshape(Bl * S, D)
    acts = []
    for l, W in enumerate(Ws):
        xt, A = _layer_fwd(l, xt, W, Bl, S, fetch)
        acts.append(A)
    loss, dx, dxb, dgf = _loss_head(xt, final_norm_g, target.reshape(Bl * S, D), name="loss_head")
    grads = [None] * len(Ws)
    for l in reversed(range(len(Ws))):
        dx, dxb, grads[l] = _layer_bwd(l, dx, dxb, Ws[l], acts[l], Bl, S, sink)
    return loss[0, 0], dx.reshape(Bl, S, D), grads, dgf.reshape(D)


def _mesh_pos():
    return lax.axis_index("x"), lax.axis_index("y"), lax.axis_index("c")


def _dev_index(px, py, pc):
    return 4 * px + 2 * py + pc


_ANY = pl.BlockSpec(memory_space=pl.ANY)


def _all_gather(arrs, *, name):
    n = len(arrs)

    def body(*refs):
        ins, outs = refs[:n], refs[n:2 * n]
        send_sems, recv_sems, local_sems = refs[2 * n:]
        x, y, c = _mesh_pos()
        me, sibling = (x, y, c), (x, y, 1 - c)
        chips = [(1 - x, y), (x, 1 - y), (1 - x, 1 - y)]

        def copy(a, k, block, to, src=None):
            dst = outs[a].at[_dev_index(*block)]
            return pltpu.make_async_remote_copy(
                src_ref=dst if src is None else src, dst_ref=dst,
                send_sem=send_sems.at[a, k], recv_sem=recv_sems.at[a, k],
                device_id=to, device_id_type=MESH)

        mine = [pltpu.make_async_copy(ins[a], outs[a].at[_dev_index(*me)], local_sems.at[a]) for a in range(n)]
        for cp in mine:
            cp.start()
        first = []
        for a in range(n):
            first.append(copy(a, 0, me, sibling, src=ins[a]))
            first += [copy(a, 1 + j, me, (*chip, c), src=ins[a]) for j, chip in enumerate(chips)]
        for cp in first:
            cp.start()
        passed = []
        for a in range(n):
            for j, chip in enumerate(chips):
                copy(a, 1 + j, (*chip, c), me).wait_recv()
                fwd = copy(a, 4 + j, (*chip, c), sibling)
                fwd.start()
                passed.append(fwd)
        for a in range(n):
            copy(a, 0, sibling, me).wait_recv()
            for j, chip in enumerate(chips):
                copy(a, 4 + j, (*chip, 1 - c), me).wait_recv()
        for cp in first + passed:
            cp.wait_send()
        for cp in mine:
            cp.wait()

    return pl.pallas_call(
        body, name=name,
        in_specs=[_ANY] * n, out_specs=[_ANY] * n,
        out_shape=[jax.ShapeDtypeStruct((N_DEV,) + a.shape, a.dtype) for a in arrs],
        scratch_shapes=[pltpu.SemaphoreType.DMA((n, N_DEV - 1)), pltpu.SemaphoreType.DMA((n, N_DEV - 1)),
                        pltpu.SemaphoreType.DMA((n,))],
    )(*arrs)


def _peers(x, y, c):
    flip = lambda v, f: 1 - v if f else v
    return [(flip(x, p & 4), flip(y, p & 2), flip(c, p & 1)) for p in range(1, N_DEV)]


GATHER_ID, SCATTER_ID = 1, 2
_SEQUENCER = dict(axis_name="sequencer", num_cores=1)


def _handshake(peers):
    barrier = pltpu.get_barrier_semaphore()
    for peer in peers:
        pl.semaphore_signal(barrier, inc=1, device_id=peer, device_id_type=MESH)
    pl.semaphore_wait(barrier, len(peers))


def _sc_gather(src, *, name):
    def body(src_ref, zone_ref, send_sems, recv_sems, local_sem):
        x, y, c = _mesh_pos()
        me, sibling = (x, y, c), (x, y, 1 - c)
        chips = [(1 - x, y), (x, 1 - y), (1 - x, 1 - y)]
        _handshake([sibling] + [(*chip, c) for chip in chips])

        def copy(k, block, to, from_src=False):
            dst = zone_ref.at[_dev_index(*block)]
            return pltpu.make_async_remote_copy(
                src_ref=src_ref if from_src else dst, dst_ref=dst, send_sem=send_sems.at[k], recv_sem=recv_sems.at[k],
                device_id=to, device_id_type=MESH)

        mine = pltpu.make_async_copy(src_ref, zone_ref.at[_dev_index(*me)], local_sem)
        mine.start()
        first = [copy(1 + j, me, (*chip, c), from_src=True) for j, chip in enumerate(chips)]
        first.append(copy(0, me, sibling, from_src=True))
        for cp in first:
            cp.start()
        passed = []
        for j, chip in enumerate(chips):
            copy(1 + j, (*chip, c), me).wait_recv()
            fwd = copy(4 + j, (*chip, c), sibling)
            fwd.start()
            passed.append(fwd)
        copy(0, sibling, me).wait_recv()
        for j, chip in enumerate(chips):
            copy(4 + j, (*chip, 1 - c), me).wait_recv()
        for cp in first + passed:
            cp.wait_send()
        mine.wait()

    return pl.kernel(
        body, name=name,
        out_type=jax.ShapeDtypeStruct((N_DEV,) + src.shape, src.dtype),
        mesh=plsc.ScalarSubcoreMesh(**_SEQUENCER),
        scratch_types=[pltpu.SemaphoreType.DMA((N_DEV - 1,)), pltpu.SemaphoreType.DMA((N_DEV - 1,)),
                       pltpu.SemaphoreType.DMA],
        compiler_params=pltpu.CompilerParams(collective_id=GATHER_ID),
    )(src)


def _sc_scatter(part, *, name):
    def body(src_ref, zone_ref, send_sems, recv_sems, local_sem):
        x, y, c = _mesh_pos()
        me = _dev_index(x, y, c)
        peers = _peers(x, y, c)
        _handshake(peers)
        mine = pltpu.make_async_copy(src_ref.at[me], zone_ref.at[me], local_sem)
        mine.start()
        sends = [pltpu.make_async_remote_copy(
            src_ref=src_ref.at[_dev_index(*peer)], dst_ref=zone_ref.at[me], send_sem=send_sems.at[k],
            recv_sem=recv_sems.at[k], device_id=peer, device_id_type=MESH) for k, peer in enumerate(peers)]
        for cp in sends:
            cp.start()
        for k, peer in enumerate(peers):
            pltpu.make_async_remote_copy(
                src_ref=src_ref.at[me], dst_ref=zone_ref.at[_dev_index(*peer)], send_sem=send_sems.at[k],
                recv_sem=recv_sems.at[k], device_id=peer, device_id_type=MESH).wait_recv()
        for cp in sends:
            cp.wait_send()
        mine.wait()

    return pl.kernel(
        body, name=name,
        out_type=jax.ShapeDtypeStruct(part.shape, part.dtype),
        mesh=plsc.ScalarSubcoreMesh(**_SEQUENCER),
        scratch_types=[pltpu.SemaphoreType.DMA((N_DEV - 1,)), pltpu.SemaphoreType.DMA((N_DEV - 1,)),
                       pltpu.SemaphoreType.DMA],
        compiler_params=pltpu.CompilerParams(collective_id=SCATTER_ID),
    )(part)


def _adamw_math(w, g, m, v):
    m2 = ADAM_B1 * m + (1.0 - ADAM_B1) * g
    v2 = ADAM_B2 * v + (1.0 - ADAM_B2) * (g * g)
    m_hat = m2 / (1.0 - ADAM_B1 ** ADAM_STEP)
    v_hat = v2 / (1.0 - ADAM_B2 ** ADAM_STEP)
    delta = -ADAM_LR * (m_hat / (jnp.sqrt(v_hat) + ADAM_EPS) + ADAM_WD * w)
    return delta, m2, v2


def _adamw_big(l, w, m, v, recv, prev, *, summed=False, name, tr=128):
    L, R, C = w.shape
    tr = next(t for t in range(min(tr, R), 0, -16) if R % t == 0)

    def body(w_ref, m_ref, v_ref, r_ref, *rest):
        g_ref, d_ref, m2_ref, v2_ref = rest[-4:]
        if summed:
            g = r_ref[...]
        else:
            g = r_ref[0].astype(F32)
            for s in range(1, N_DEV):
                g = g + r_ref[s].astype(F32)
        g_ref[...] = g
        d_ref[...], m2_ref[...], v2_ref[...] = _adamw_math(w_ref[...], g, m_ref[...], v_ref[...])

    wspec = pl.BlockSpec((None, tr, C), lambda i: (l, i, 0))
    rspec = pl.BlockSpec((tr, C), lambda i: (i, 0)) if summed else pl.BlockSpec((N_DEV, tr, C), lambda i: (0, i, 0))
    prev = list(prev) if prev is not None else []
    return pl.pallas_call(
        body, name=name, grid=(R // tr,),
        in_specs=[wspec, wspec, wspec, rspec] + [_ANY] * len(prev),
        out_specs=[wspec] * 4,
        out_shape=[jax.ShapeDtypeStruct((L, R, C), F32)] * 4,
        input_output_aliases={4 + j: j for j in range(len(prev))},
        compiler_params=_cparams(("parallel",)),
    )(w, m, v, recv if summed else recv.reshape(N_DEV, R, C), *prev)


def _sum_slots_wide(recv, *, name, tc=512):
    _, R, C = recv.shape
    tc = _tile(C, tc)

    def body(r_ref, o_ref):
        g = r_ref[0].astype(F32)
        for s in range(1, N_DEV):
            g = g + r_ref[s].astype(F32)
        o_ref[...] = g

    return pl.pallas_call(
        body, name=name, grid=(C // tc,),
        in_specs=[pl.BlockSpec((N_DEV, R, tc), lambda j: (0, 0, j))],
        out_specs=pl.BlockSpec((R, tc), lambda j: (0, j)),
        out_shape=jax.ShapeDtypeStruct((R, C), F32),
        compiler_params=_cparams(("parallel",)),
    )(recv)


def _sum_slots(gathered, *, name):
    _, R, C = gathered.shape

    def body(r_ref, o_ref):
        g = r_ref[0]
        for s in range(1, N_DEV):
            g = g + r_ref[s]
        o_ref[...] = g

    return pl.pallas_call(body, name=name, out_shape=jax.ShapeDtypeStruct((R, C), F32))(gathered)


def _adamw_small(ws, gs, ms, vs, *, name):
    n = len(ws)

    def body(*refs):
        for i in range(n):
            w_ref, g_ref, m_ref, v_ref = (refs[k * n + i] for k in range(4))
            d_ref, m2_ref, v2_ref = (refs[(4 + k) * n + i] for k in range(3))
            d_ref[...], m2_ref[...], v2_ref[...] = _adamw_math(w_ref[...], g_ref[...], m_ref[...], v_ref[...])

    out = pl.pallas_call(body, name=name,
                         out_shape=[jax.ShapeDtypeStruct(a.shape, F32) for a in ws] * 3)(*ws, *gs, *ms, *vs)
    return out[:n], out[n:2 * n], out[2 * n:]


def _pack(arrs):
    flat = []
    for a in arrs:
        a = a.reshape(-1).astype(F32)
        flat.append(jnp.pad(a, (0, (-a.shape[0]) % LANES)))
    out = jnp.concatenate(flat)
    out = jnp.pad(out, (0, (-out.shape[0]) % (8 * LANES)))
    return out.reshape(-1, LANES)


def _unpack(packed, shapes):
    flat = packed.reshape(-1)
    out, pos = [], 0
    for s in shapes:
        size = math.prod(s)
        out.append(flat[pos:pos + size].reshape(s))
        pos += size + (-size) % LANES
    return out


BIG = ("w_in", "conv_pw_w", "w_out", "w_up", "w_down")
SMALL_SHARDED = ("conv_dw_w", "gdn_conv_w", "ffn_conv_w")
SMALL_REPLICATED = ("mix_norm_g", "conv_dw_b", "conv_ln_g", "conv_ln_b", "conv_pw_b", "gdn_a_log", "gdn_dt_bias",
                    "gdn_norm_g", "ffn_norm_g", "ffn_conv_b")
WEIGHTS = ("mix_norm_g", "w_in", "conv_dw_w", "conv_dw_b", "conv_ln_g", "conv_ln_b", "conv_pw_w", "conv_pw_b",
           "gdn_conv_w", "gdn_a_log", "gdn_dt_bias", "gdn_norm_g", "w_out", "ffn_norm_g", "w_up", "ffn_conv_w",
           "ffn_conv_b", "w_down", "final_norm_g")


def _train_step(x, target, w, m, v):
    L = w["w_in"].shape[0]
    D = x.shape[-1]
    xi, yi, ci = _mesh_pos()
    me = _dev_index(xi, yi, ci)

    gathered = {}

    def launch(l, after=None):
        for n in BIG:
            src = (w[n][l].T if n == "w_in" else w[n][l]).astype(BF16)
            if after is not None:
                src = lax.optimization_barrier((src, after))[0]
            gathered[n, l] = _sc_gather(src, name=f"gather_{n}_{l}")

    launch(0)
    small_full = {}
    for n, g_ in zip(SMALL_SHARDED, _all_gather([w[n] for n in SMALL_SHARDED], name="all_gather_conv_taps")):
        small_full[n] = jnp.moveaxis(g_, 0, 2).reshape(L, g_.shape[2], N_DEV * g_.shape[3])
    Ws = []
    for l in range(L):
        W = {n: w[n][l] for n in SMALL_REPLICATED}
        W.update({n: small_full[n][l] for n in SMALL_SHARDED})
        Ws.append(W)

    def fetch(l, n, after):
        if n == "conv_pw_w" and l + 1 < L:
            launch(l + 1, after)
        g_ = lax.optimization_barrier((gathered[n, l], after))[0]
        if n == "w_up":
            return g_
        g_ = g_.reshape(g_.shape[0] * g_.shape[1], g_.shape[2])
        if n == "w_in":
            n_main = (g_.shape[0] // LANES) * LANES
            return g_, jnp.pad(g_[n_main:], ((0, LANES - (g_.shape[0] - n_main)), (0, 0)))
        return g_

    started = []
    res = {}
    SCATTERS_IN_FLIGHT = 2

    def consume(chain):
        n, l, recv = started.pop(0)
        if chain is not None:
            recv, chain = lax.optimization_barrier((recv, chain))
        if n == "w_in":
            recv = _sum_slots_wide(recv, name="sum_w_in_grad").T
        res[n] = _adamw_big(l, w[n], m[n], v[n], recv, res.get(n), summed=(n == "w_in"), name=f"adamw_{n}")
        if chain is None:
            return None
        tied = lax.optimization_barrier((chain, *res[n]))
        res[n] = list(tied[1:])
        return tied[0]

    def sink(l, n, g_, chain):
        g_, chain = lax.optimization_barrier((g_, chain))
        if len(started) >= SCATTERS_IN_FLIGHT:
            chain = consume(chain)
        if n == "w_in":
            g_main, g_ba = g_
            g_ = jnp.concatenate([g_main, g_ba[:w["w_in"].shape[2] * N_DEV - g_main.shape[0]]], axis=0)
            part = g_.reshape(N_DEV, -1, D)
        elif n == "w_up":
            part = g_
        else:
            part = g_.reshape(N_DEV, -1, g_.shape[1])
        started.append((n, l, _sc_scatter(part, name=f"scatter_{n}_{l}")))
        return chain

    loss, grad_x, G, d_final = _local_step(x, target, Ws, w["final_norm_g"], fetch, sink)

    small_names = [n for n in WEIGHTS if n not in BIG]
    partial = []
    for n in small_names:
        if n == "final_norm_g":
            partial.append(d_final)
        else:
            partial.append(jnp.stack([G[l][n].reshape(Ws[l][n].shape) for l in range(L)]))
    partial.append(loss.reshape(1))
    packed = _pack(partial)
    if started:
        packed = lax.optimization_barrier((packed, started[-1][2]))[0]
    small_gathered = _sc_gather(packed, name="gather_small_grads")

    out = {k: {} for k in ("grad", "delta", "new_m", "new_v")}
    while started:
        grad_x = consume(grad_x)
    for n in BIG:
        for j, k in enumerate(("grad", "delta", "new_m", "new_v")):
            out[k][n] = res[n][j]

    summed = _unpack(_sum_slots(small_gathered, name="sum_small_grads"), [p_.shape for p_ in partial])
    full = dict(zip(small_names, summed))
    loss = summed[-1][0]
    for n in SMALL_SHARDED:
        width = w[n].shape[-1]
        full[n] = lax.dynamic_slice_in_dim(full[n], me * width, width, axis=2)
    flat = lambda a: a.reshape(-1, a.shape[-1])
    res = _adamw_small(*[[flat(src[n]) for n in small_names] for src in (w, full, m, v)], name="adamw_small")
    out["grad"].update({n: full[n] for n in small_names})
    for k, arrs in zip(("delta", "new_m", "new_v"), res):
        out[k].update({n: a.reshape(w[n].shape) for n, a in zip(small_names, arrs)})
    return loss, grad_x, out


def kernel(x, mix_norm_g, w_in, conv_dw_w, conv_dw_b, conv_ln_g, conv_ln_b, conv_pw_w, conv_pw_b, gdn_conv_w, gdn_a_log, gdn_dt_bias, gdn_norm_g, w_out, ffn_norm_g, w_up, ffn_conv_w, ffn_conv_b, w_down, final_norm_g, loss_target, m_mix_norm_g, m_w_in, m_conv_dw_w, m_conv_dw_b, m_conv_ln_g, m_conv_ln_b, m_conv_pw_w, m_conv_pw_b, m_gdn_conv_w, m_gdn_a_log, m_gdn_dt_bias, m_gdn_norm_g, m_w_out, m_ffn_norm_g, m_w_up, m_ffn_conv_w, m_ffn_conv_b, m_w_down, m_final_norm_g, v_mix_norm_g, v_w_in, v_conv_dw_w, v_conv_dw_b, v_conv_ln_g, v_conv_ln_b, v_conv_pw_w, v_conv_pw_b, v_gdn_conv_w, v_gdn_a_log, v_gdn_dt_bias, v_gdn_norm_g, v_w_out, v_ffn_norm_g, v_w_up, v_ffn_conv_w, v_ffn_conv_b, v_w_down, v_final_norm_g):
    w = dict(zip(WEIGHTS, (mix_norm_g, w_in, conv_dw_w, conv_dw_b, conv_ln_g, conv_ln_b, conv_pw_w, conv_pw_b, gdn_conv_w,
                           gdn_a_log, gdn_dt_bias, gdn_norm_g, w_out, ffn_norm_g, w_up, ffn_conv_w, ffn_conv_b, w_down,
                           final_norm_g)))
    m = dict(zip(WEIGHTS, (m_mix_norm_g, m_w_in, m_conv_dw_w, m_conv_dw_b, m_conv_ln_g, m_conv_ln_b, m_conv_pw_w,
                           m_conv_pw_b, m_gdn_conv_w, m_gdn_a_log, m_gdn_dt_bias, m_gdn_norm_g, m_w_out, m_ffn_norm_g,
                           m_w_up, m_ffn_conv_w, m_ffn_conv_b, m_w_down, m_final_norm_g)))
    v = dict(zip(WEIGHTS, (v_mix_norm_g, v_w_in, v_conv_dw_w, v_conv_dw_b, v_conv_ln_g, v_conv_ln_b, v_conv_pw_w,
                           v_conv_pw_b, v_gdn_conv_w, v_gdn_a_log, v_gdn_dt_bias, v_gdn_norm_g, v_w_out, v_ffn_norm_g,
                           v_w_up, v_ffn_conv_w, v_ffn_conv_b, v_w_down, v_final_norm_g)))
    loss, grad_x, out = _train_step(x, loss_target, w, m, v)
    return (loss, grad_x, *[out["grad"][n] for n in WEIGHTS], *[out["delta"][n] for n in WEIGHTS],
            *[out["new_m"][n] for n in WEIGHTS], *[out["new_v"][n] for n in WEIGHTS])
```

```python
import math

import jax
import jax.numpy as jnp
from jax import lax
from jax.experimental import pallas as pl
from jax.experimental.pallas import tpu as pltpu
from jax.experimental.pallas import tpu_sc as plsc

F32 = jnp.float32
BF16 = jnp.bfloat16
MESH = pl.DeviceIdType.MESH

EPS = 1e-6
LANES = 128
CHUNK = 128
NEAR_BLOCK = 32
CONV_K = 31
SHORT_CONV_K = 4
FFN_CONV_K = 3
N_DEV = 8
VMEM_LIMIT = 56 * 1024 * 1024

ADAM_LR = 0.001
ADAM_B1 = 0.9
ADAM_B2 = 0.999
ADAM_EPS = 1e-08
ADAM_WD = 0.01
ADAM_STEP = 10


def _cparams(sem):
    return pltpu.CompilerParams(dimension_semantics=sem, vmem_limit_bytes=VMEM_LIMIT)


def _sig(x):
    return 1.0 / (1.0 + jnp.exp(-x))


def _silu(x):
    return x * _sig(x)


def _dsilu(x):
    s = _sig(x)
    return s * (1.0 + x * (1.0 - s))


def _softplus(x):
    return jnp.maximum(x, 0.0) + jnp.log1p(jnp.exp(-jnp.abs(x)))


def _dot(a, b):
    return jnp.dot(a, b, preferred_element_type=F32)


def _dot_nt(a, b):
    return lax.dot_general(a, b, (((1,), (1,)), ((), ())), preferred_element_type=F32)


def _dot_tn(a, b):
    return lax.dot_general(a, b, (((0,), (0,)), ((), ())), preferred_element_type=F32)


def _bf(x):
    return x.astype(BF16)


_NN = (((1,), (0,)), ((), ()))
_TN = (((0,), (0,)), ((), ()))


def _split2(x):
    hi = _bf(x)
    return hi, _bf(x - hi.astype(F32))


def _dot_x3(a, b, dn=_NN):
    ah, al = _split2(a)
    bh, bl = _split2(b)
    f = lambda p, q: lax.dot_general(p, q, dn, preferred_element_type=F32)
    return f(ah, bh) + (f(al, bh) + f(ah, bl))


def _dot_mask(mask, x, dn=_NN):
    mb = _bf(mask)
    hi, lo = _split2(x)
    lo2 = _bf(x - hi.astype(F32) - lo.astype(F32))
    f = lambda q: lax.dot_general(mb, q, dn, preferred_element_type=F32)
    return f(hi) + (f(lo) + f(lo2))


def _shift_down(u, s):
    if s == 0:
        return u
    row = lax.broadcasted_iota(jnp.int32, u.shape, 0)
    return jnp.where(row >= s, pltpu.roll(u, s, 0), 0.0)


def _shift_up(u, s):
    if s == 0:
        return u
    n = u.shape[0]
    row = lax.broadcasted_iota(jnp.int32, u.shape, 0)
    return jnp.where(row < n - s, pltpu.roll(u, n - s, 0), 0.0)


def _conv_fwd(u, w_ref, K):
    acc = None
    for k in range(K):
        term = w_ref[k:k + 1, :] * _shift_down(u, K - 1 - k)
        acc = term if acc is None else acc + term
    return acc


def _conv_bwd_in(do, w_ref, K):
    acc = None
    for k in range(K):
        term = w_ref[k:k + 1, :] * _shift_up(do, K - 1 - k)
        acc = term if acc is None else acc + term
    return acc


def _conv_bwd_w(do, u, dw_ref, K, first):
    for k in range(K):
        row = jnp.sum(do * _shift_down(u, K - 1 - k), axis=0, keepdims=True)
        _acc_row(dw_ref, k, row, first)


def _acc_row(ref, k, row, first):
    @pl.when(first)
    def _():
        ref[k:k + 1, :] = row

    @pl.when(jnp.logical_not(first))
    def _():
        ref[k:k + 1, :] += row


def _logical(arr):
    if arr.ndim == 2:
        return arr.shape
    return (arr.shape[1], arr.shape[0] * arr.shape[2])


def _tile(dim, pref, *col_widths):
    if dim % LANES:
        assert not col_widths
        return dim
    t = (min(pref, dim) // LANES) * LANES
    while t > LANES and (dim % t or any(c % t for c in col_widths)):
        t -= LANES
    assert dim % t == 0 and all(c % t == 0 for c in col_widths), (dim, pref, col_widths)
    return t


def _spec(shape, rt, ct, rfn, cfn):
    if len(shape) == 2:
        return pl.BlockSpec((rt, ct), lambda i, j, k: (rfn(i, j, k), cfn(i, j, k)))
    per = shape[2] // ct
    return pl.BlockSpec((None, rt, ct),
                        lambda i, j, k: (cfn(i, j, k) // per, rfn(i, j, k), cfn(i, j, k) % per))


def _mm(a, b, *, name, ta=False, tb=False, out_dtype=F32, out_blocks=None, bias=None, res=None, b_rows=None,
        tm=1024, tn=1024, tk=2816):
    ra, ca = _logical(a)
    rb, cb = _logical(b)
    if b_rows is not None:
        assert b.ndim == 2 and b_rows <= rb
        rb = b_rows
    M, K = (ca, ra) if ta else (ra, ca)
    N, K2 = (rb, cb) if tb else (cb, rb)
    assert K == K2, (a.shape, b.shape, ta, tb)
    out_shape = (M, N) if out_blocks is None else (out_blocks, M, N // out_blocks)
    cw = lambda arr: [arr.shape[2]] if arr.ndim == 3 else []
    m_c = cw(a) if ta else []
    k_c = (cw(a) if not ta else []) + (cw(b) if tb else [])
    n_c = (cw(b) if not tb else []) + ([out_shape[2]] if out_blocks else []) + (cw(res) if res is not None else [])
    tm, tn, tk = _tile(M, tm, *m_c), _tile(N, tn, *n_c), _tile(K, tk, *k_c)
    nk = K // tk
    im, jn, kk = (lambda i, j, k: i), (lambda i, j, k: j), (lambda i, j, k: k)
    in_specs = [
        _spec(a.shape, tk, tm, kk, im) if ta else _spec(a.shape, tm, tk, im, kk),
        _spec(b.shape, tn, tk, jn, kk) if tb else _spec(b.shape, tk, tn, kk, jn),
    ]
    args = [a, b]
    if bias is not None:
        in_specs.append(pl.BlockSpec((1, tn), lambda i, j, k: (0, j)))
        args.append(bias.reshape(1, N).astype(F32))
    if res is not None:
        in_specs.append(_spec(res.shape, tm, tn, im, jn))
        args.append(res)
    dn = (((0 if ta else 1,), (1 if tb else 0,)), ((), ()))

    def body(*refs):
        a_ref, b_ref = refs[0], refs[1]
        pos = 2
        bias_ref = res_ref = None
        if bias is not None:
            bias_ref = refs[pos]
            pos += 1
        if res is not None:
            res_ref = refs[pos]
            pos += 1
        o_ref = refs[pos]
        k = pl.program_id(2)
        part = lax.dot_general(_bf(a_ref[...]), _bf(b_ref[...]), dn, preferred_element_type=F32)

        def finish(r):
            if bias_ref is not None:
                r = r + bias_ref[...]
            if res_ref is not None:
                r = r + res_ref[...].astype(F32)
            o_ref[...] = r.astype(out_dtype)

        if nk == 1:
            finish(part)
            return
        acc_ref = refs[pos + 1]

        @pl.when(k == 0)
        def _():
            acc_ref[...] = part

        @pl.when((k > 0) & (k < nk - 1))
        def _():
            acc_ref[...] += part

        @pl.when(k == nk - 1)
        def _():
            finish(acc_ref[...] + part)

    return pl.pallas_call(
        body, name=name,
        grid=(M // tm, N // tn, nk),
        in_specs=in_specs,
        out_specs=_spec(out_shape, tm, tn, im, jn),
        out_shape=jax.ShapeDtypeStruct(out_shape, out_dtype),
        scratch_shapes=[pltpu.VMEM((tm, tn), F32)] if nk > 1 else [],
        compiler_params=_cparams(("parallel", "parallel", "arbitrary")),
    )(*args)


def _mm_res_norm(a, b, res, g, *, name, tm=512):
    M, K = a.shape
    N = b.shape[1]
    tm = min(tm, M)

    def body(a_ref, b_ref, res_ref, g_ref, x_ref, h_ref):
        xv = res_ref[...] + _dot(a_ref[...], b_ref[...])
        x_ref[...] = xv
        r = lax.rsqrt(jnp.mean(xv * xv, axis=-1, keepdims=True) + EPS)
        h_ref[...] = (xv * r * g_ref[...]).astype(BF16)

    row = lambda n: pl.BlockSpec((tm, n), lambda i: (i, 0))
    return pl.pallas_call(
        body, name=name, grid=(M // tm,),
        in_specs=[row(K), pl.BlockSpec((K, N), lambda i: (0, 0)), row(N), pl.BlockSpec((1, N), lambda i: (0, 0))],
        out_specs=[row(N), row(N)],
        out_shape=[jax.ShapeDtypeStruct((M, N), F32), jax.ShapeDtypeStruct((M, N), BF16)],
        compiler_params=_cparams(("parallel",)),
    )(a, b, res, g.reshape(1, N))


def _rms_fwd(x, g, *, name, tr=512):
    T, D = x.shape
    tr = min(tr, T)

    def body(x_ref, g_ref, h_ref):
        xv = x_ref[...]
        r = lax.rsqrt(jnp.mean(xv * xv, axis=-1, keepdims=True) + EPS)
        h_ref[...] = (xv * r * g_ref[...]).astype(BF16)

    return pl.pallas_call(
        body, name=name, grid=(T // tr,),
        in_specs=[pl.BlockSpec((tr, D), lambda i: (i, 0)), pl.BlockSpec((1, D), lambda i: (0, 0))],
        out_specs=pl.BlockSpec((tr, D), lambda i: (i, 0)),
        out_shape=jax.ShapeDtypeStruct((T, D), BF16),
        compiler_params=_cparams(("parallel",)),
    )(x, g.reshape(1, D))


def _rms_bwd(x, g, dh, dres, *, name, tr=512):
    T, D = x.shape
    tr = min(tr, T)

    def body(x_ref, g_ref, dh_ref, dres_ref, dx_ref, dxb_ref, dg_ref):
        i = pl.program_id(0)
        xv = x_ref[...]
        dy = dh_ref[...].astype(F32)
        r = lax.rsqrt(jnp.mean(xv * xv, axis=-1, keepdims=True) + EPS)
        dyg = dy * g_ref[...]
        dot = jnp.mean(dyg * xv, axis=-1, keepdims=True)
        dx = dres_ref[...] + r * dyg - xv * (r * r * r) * dot
        dx_ref[...] = dx
        dxb_ref[...] = dx.astype(BF16)
        part = jnp.sum(dy * xv * r, axis=0, keepdims=True)
        _acc_row(dg_ref, 0, part, i == 0)

    row = pl.BlockSpec((tr, D), lambda i: (i, 0))
    vec = pl.BlockSpec((1, D), lambda i: (0, 0))
    return pl.pallas_call(
        body, name=name, grid=(T // tr,),
        in_specs=[row, vec, row, row],
        out_specs=[row, row, vec],
        out_shape=[jax.ShapeDtypeStruct((T, D), F32), jax.ShapeDtypeStruct((T, D), BF16),
                   jax.ShapeDtypeStruct((1, D), F32)],
        compiler_params=_cparams(("arbitrary",)),
    )(x, g.reshape(1, D), dh, dres)


def _loss_head(x, g, target, *, name, tr=512):
    T, D = x.shape
    tr = min(tr, T)

    def body(x_ref, g_ref, t_ref, loss_ref, dx_ref, dxb_ref, dg_ref):
        i = pl.program_id(0)
        xv = x_ref[...]
        gv = g_ref[...]
        r = lax.rsqrt(jnp.mean(xv * xv, axis=-1, keepdims=True) + EPS)
        y = xv * r * gv
        err = y - t_ref[...]
        lpart = 0.5 * jnp.sum(jnp.mean(err * err, axis=-1, keepdims=True), axis=0, keepdims=True)
        dy = err * (1.0 / D)
        dyg = dy * gv
        dot = jnp.mean(dyg * xv, axis=-1, keepdims=True)
        dx = r * dyg - xv * (r * r * r) * dot
        dx_ref[...] = dx
        dxb_ref[...] = dx.astype(BF16)
        _acc_row(dg_ref, 0, jnp.sum(dy * xv * r, axis=0, keepdims=True), i == 0)
        _acc_row(loss_ref, 0, jnp.broadcast_to(lpart, (1, LANES)), i == 0)

    row = pl.BlockSpec((tr, D), lambda i: (i, 0))
    return pl.pallas_call(
        body, name=name, grid=(T // tr,),
        in_specs=[row, pl.BlockSpec((1, D), lambda i: (0, 0)), row],
        out_specs=[pl.BlockSpec((1, LANES), lambda i: (0, 0)), row, row, pl.BlockSpec((1, D), lambda i: (0, 0))],
        out_shape=[jax.ShapeDtypeStruct((1, LANES), F32), jax.ShapeDtypeStruct((T, D), F32),
                   jax.ShapeDtypeStruct((T, D), BF16), jax.ShapeDtypeStruct((1, D), F32)],
        compiler_params=_cparams(("arbitrary",)),
    )(x, g.reshape(1, D), target)


HALO = 32
SUBLANES = 8


def _conv_dw_blocks(do, u, dw_ref, K, first, u_s, do_p):
    S, C = u.shape
    zeros = jnp.zeros((HALO, C), F32)
    for r in range(SUBLANES):
        u_s[r, 0:HALO, :] = zeros
        u_s[r, HALO:HALO + S, :] = _shift_down(u, r)
    do_p[0:HALO, :] = zeros
    do_p[HALO:HALO + S, :] = do
    do_p[HALO + S:2 * HALO + S, :] = zeros
    n_a = (K - 1) // SUBLANES + 1

    def block(i, accs):
        i0 = pl.multiple_of(i * SUBLANES, SUBLANES)
        us = [u_s[r, pl.ds(i0, SUBLANES), :] for r in range(SUBLANES)]
        ds = [do_p[pl.ds(i0 + SUBLANES * a, SUBLANES), :] for a in range(n_a)]
        out = list(accs)
        for a in range(n_a):
            for r in range(SUBLANES):
                s = SUBLANES * a + r
                if s < K:
                    out[K - 1 - s] = out[K - 1 - s] + ds[a] * us[r]
        return tuple(out)

    accs = lax.fori_loop(HALO // SUBLANES, (S + HALO) // SUBLANES, block, (jnp.zeros((SUBLANES, C), F32),) * K)
    for k in range(K):
        _acc_row(dw_ref, k, jnp.sum(accs[k], axis=0, keepdims=True), first)


def _conf_norm(u1, lg_ref, lb_ref):
    mu = jnp.mean(u1, axis=-1, keepdims=True)
    xc = u1 - mu
    r = lax.rsqrt(jnp.mean(xc * xc, axis=-1, keepdims=True) + EPS)
    n = xc * r
    return r, n, n * lg_ref[...] + lb_ref[...]


def _conf_fwd(p, dw_w, dw_b, ln_g, ln_b, *, Bl, S, CC, name):
    G = CC // LANES

    def body(av_ref, ag_ref, w_ref, b_ref, lg_ref, lb_ref, o_ref, u1_ref):
        u0 = av_ref[...] * _sig(ag_ref[...])
        u1 = _conv_fwd(u0, w_ref, CONV_K) + b_ref[...]
        u1_ref[...] = u1
        _, _, u2 = _conf_norm(u1, lg_ref, lb_ref)
        o_ref[...] = _silu(u2).astype(BF16)

    vec = pl.BlockSpec((1, LANES), lambda b, j: (0, j))
    seq = pl.BlockSpec((S, LANES), lambda b, j: (b, j))
    return pl.pallas_call(
        body, name=name, grid=(Bl, G),
        in_specs=[seq, pl.BlockSpec((S, LANES), lambda b, j: (b, G + j)),
                  pl.BlockSpec((CONV_K, LANES), lambda b, j: (0, j)), vec, vec, vec],
        out_specs=[seq, seq],
        out_shape=[jax.ShapeDtypeStruct((Bl * S, CC), BF16), jax.ShapeDtypeStruct((Bl * S, CC), F32)],
        compiler_params=_cparams(("parallel", "parallel")),
    )(p, p, dw_w, dw_b.reshape(1, CC), ln_g.reshape(1, CC), ln_b.reshape(1, CC))


def _conf_bwd(p, u1, dw_w, ln_g, ln_b, du3, *, Bl, S, CC, name):
    G = CC // LANES

    def body(av_ref, ag_ref, u1_ref, w_ref, lg_ref, lb_ref, du3_ref,
             dav_ref, dag_ref, dw_ref, db_ref, dlg_ref, dlb_ref, u_s, do_p):
        first = pl.program_id(1) == 0
        av = av_ref[...]
        sg = _sig(ag_ref[...])
        r, n, u2 = _conf_norm(u1_ref[...], lg_ref, lb_ref)
        du2 = du3_ref[...] * _dsilu(u2)
        _acc_row(dlg_ref, 0, jnp.sum(du2 * n, axis=0, keepdims=True), first)
        _acc_row(dlb_ref, 0, jnp.sum(du2, axis=0, keepdims=True), first)
        dn = du2 * lg_ref[...]
        du1 = r * (dn - jnp.mean(dn, axis=-1, keepdims=True) - n * jnp.mean(dn * n, axis=-1, keepdims=True))
        _acc_row(db_ref, 0, jnp.sum(du1, axis=0, keepdims=True), first)
        _conv_dw_blocks(du1, av * sg, dw_ref, CONV_K, first, u_s, do_p)
        du0 = _conv_bwd_in(du1, w_ref, CONV_K)
        dav_ref[...] = (du0 * sg).astype(BF16)
        dag_ref[...] = (du0 * av * sg * (1.0 - sg)).astype(BF16)

    vec = pl.BlockSpec((1, LANES), lambda j, b: (0, j))
    seq = pl.BlockSpec((S, LANES), lambda j, b: (b, j))
    return pl.pallas_call(
        body, name=name, grid=(G, Bl),
        in_specs=[seq, pl.BlockSpec((S, LANES), lambda j, b: (b, G + j)), seq,
                  pl.BlockSpec((CONV_K, LANES), lambda j, b: (0, j)), vec, vec, seq],
        out_specs=[seq, seq, pl.BlockSpec((CONV_K, LANES), lambda j, b: (0, j)), vec, vec, vec],
        out_shape=[jax.ShapeDtypeStruct((Bl * S, CC), BF16), jax.ShapeDtypeStruct((Bl * S, CC), BF16),
                   jax.ShapeDtypeStruct((CONV_K, CC), F32), jax.ShapeDtypeStruct((1, CC), F32),
                   jax.ShapeDtypeStruct((1, CC), F32), jax.ShapeDtypeStruct((1, CC), F32)],
        scratch_shapes=[pltpu.VMEM((SUBLANES, S + HALO, LANES), F32), pltpu.VMEM((S + 2 * HALO, LANES), F32)],
        compiler_params=_cparams(("parallel", "arbitrary")),
    )(p, p, u1, dw_w, ln_g.reshape(1, CC), ln_b.reshape(1, CC), du3)


def _gdn_pre_fwd(p, conv_w, *, Bl, S, CC, KW, VW, name):
    NQK = 2 * KW // LANES
    NB = NQK + VW // LANES
    off = 2 * CC // LANES

    def body(x_ref, w_ref, o_ref):
        j = pl.program_id(1)
        s = _silu(_conv_fwd(x_ref[...], w_ref, SHORT_CONV_K))
        r = lax.rsqrt(jnp.sum(s * s, axis=-1, keepdims=True) + EPS)
        o_ref[...] = jnp.where(j < NQK, s * r, s)

    return pl.pallas_call(
        body, name=name, grid=(Bl, NB),
        in_specs=[pl.BlockSpec((S, LANES), lambda b, j: (b, off + j)),
                  pl.BlockSpec((SHORT_CONV_K, LANES), lambda b, j: (0, j))],
        out_specs=pl.BlockSpec((S, LANES), lambda b, j: (b, j)),
        out_shape=jax.ShapeDtypeStruct((Bl * S, NB * LANES), F32),
        compiler_params=_cparams(("parallel", "parallel")),
    )(p, conv_w)


def _gdn_pre_bwd(p, conv_w, dq, dk, dv, *, Bl, S, CC, KW, VW, name):
    HQ = KW // LANES
    H = VW // LANES
    NQK = 2 * HQ
    NB = NQK + H
    off = 2 * CC // LANES

    def body(x_ref, w_ref, dq_ref, dk_ref, dv_ref, dx_ref, dw_ref):
        j = pl.program_id(0)
        first = pl.program_id(1) == 0
        xv = x_ref[...]
        c = _conv_fwd(xv, w_ref, SHORT_CONV_K)
        s = _silu(c)
        r = lax.rsqrt(jnp.sum(s * s, axis=-1, keepdims=True) + EPS)
        dy = jnp.where(j < HQ, dq_ref[...], jnp.where(j < NQK, dk_ref[...], dv_ref[...]))
        ds_norm = r * dy - s * (r * r * r) * jnp.sum(s * dy, axis=-1, keepdims=True)
        ds = jnp.where(j < NQK, ds_norm, dy)
        dc = ds * _dsilu(c)
        _conv_bwd_w(dc, xv, dw_ref, SHORT_CONV_K, first)
        dx_ref[...] = _conv_bwd_in(dc, w_ref, SHORT_CONV_K).astype(BF16)

    return pl.pallas_call(
        body, name=name, grid=(NB, Bl),
        in_specs=[pl.BlockSpec((S, LANES), lambda j, b: (b, off + j)),
                  pl.BlockSpec((SHORT_CONV_K, LANES), lambda j, b: (0, j)),
                  pl.BlockSpec((S, LANES), lambda j, b: (b, jnp.minimum(j, HQ - 1))),
                  pl.BlockSpec((S, LANES), lambda j, b: (b, jnp.clip(j - HQ, 0, HQ - 1))),
                  pl.BlockSpec((S, LANES), lambda j, b: (b, jnp.clip(j - NQK, 0, H - 1)))],
        out_specs=[pl.BlockSpec((S, LANES), lambda j, b: (b, j)),
                   pl.BlockSpec((SHORT_CONV_K, LANES), lambda j, b: (0, j))],
        out_shape=[jax.ShapeDtypeStruct((Bl * S, NB * LANES), BF16),
                   jax.ShapeDtypeStruct((SHORT_CONV_K, NB * LANES), F32)],
        compiler_params=_cparams(("parallel", "arbitrary")),
    )(p, conv_w, dq, dk, dv)


def _split3(x):
    hi, lo = _split2(x)
    return hi, lo, _bf(x - hi.astype(F32) - lo.astype(F32))


def _lane_replicate(parts, h):
    row = lax.broadcasted_iota(jnp.int32, (LANES, LANES), 0)
    E = jnp.where(row == h, 1.0, 0.0).astype(BF16)
    return _dot(parts[0], E) + (_dot(parts[1], E) + _dot(parts[2], E))


def _gdn_gate_fwd(pba, a_log, dt_bias, *, Bl, S, H, name):
    def body(alog_ref, dtb_ref, x_ref, g_ref, beta_ref):
        parts = _split3(x_ref[...])
        for h in range(H):
            b_raw = _lane_replicate(parts, h)
            a_raw = _lane_replicate(parts, H + h)
            beta_ref[h] = _sig(b_raw)
            ea = jnp.exp(jnp.zeros((1, LANES), F32) + alog_ref[h])
            g_ref[h] = -ea * _softplus(a_raw + dtb_ref[h])

    smem = pl.BlockSpec(memory_space=pltpu.SMEM)
    rep = pl.BlockSpec((H, S, LANES), lambda b: (0, b, 0))
    return pl.pallas_call(
        body, name=name, grid=(Bl,),
        in_specs=[smem, smem, pl.BlockSpec((S, LANES), lambda b: (b, 0))],
        out_specs=[rep, rep],
        out_shape=[jax.ShapeDtypeStruct((H, Bl * S, LANES), F32)] * 2,
        compiler_params=_cparams(("parallel",)),
    )(a_log, dt_bias, pba)


def _gdn_gate_bwd(pba, a_log, dt_bias, dg, dbeta, *, Bl, S, H, name):
    HP = 8 * ((H + 7) // 8)

    def body(alog_ref, dtb_ref, x_ref, dg_ref, dbeta_ref, dx_ref, dalog_ref, ddtb_ref):
        first = pl.program_id(0) == 0
        parts = _split3(x_ref[...])
        lane = lax.broadcasted_iota(jnp.int32, (S, LANES), 1)
        acc = jnp.zeros((S, LANES), F32)

        @pl.when(first)
        def _():
            dalog_ref[...] = jnp.zeros_like(dalog_ref)
            ddtb_ref[...] = jnp.zeros_like(ddtb_ref)

        for h in range(H):
            b_raw = _lane_replicate(parts, h)
            a_raw = _lane_replicate(parts, H + h)
            beta = _sig(b_raw)
            db_raw = dbeta_ref[h] * beta * (1.0 - beta)
            z = a_raw + dtb_ref[h]
            ea = jnp.exp(jnp.zeros((1, LANES), F32) + alog_ref[h])
            dgv = dg_ref[h]
            da_raw = dgv * (-ea) * _sig(z)
            g = -ea * _softplus(z)
            dalog_ref[h:h + 1, :] += jnp.sum(dgv * g, axis=0, keepdims=True)
            ddtb_ref[h:h + 1, :] += jnp.sum(da_raw, axis=0, keepdims=True)
            acc = acc + jnp.where(lane == h, db_raw, 0.0) + jnp.where(lane == H + h, da_raw, 0.0)
        dx_ref[...] = acc.astype(BF16)

    smem = pl.BlockSpec(memory_space=pltpu.SMEM)
    rep = pl.BlockSpec((H, S, LANES), lambda b: (0, b, 0))
    small = pl.BlockSpec((HP, LANES), lambda b: (0, 0))
    return pl.pallas_call(
        body, name=name, grid=(Bl,),
        in_specs=[smem, smem, pl.BlockSpec((S, LANES), lambda b: (b, 0)), rep, rep],
        out_specs=[pl.BlockSpec((S, LANES), lambda b: (b, 0)), small, small],
        out_shape=[jax.ShapeDtypeStruct((Bl * S, LANES), BF16),
                   jax.ShapeDtypeStruct((HP, LANES), F32), jax.ShapeDtypeStruct((HP, LANES), F32)],
        compiler_params=_cparams(("arbitrary",)),
    )(a_log, dt_bias, pba, dg, dbeta)


def _tri_masks():
    ri = lax.broadcasted_iota(jnp.int32, (CHUNK, CHUNK), 0)
    ci = lax.broadcasted_iota(jnp.int32, (CHUNK, CHUNK), 1)
    return ri >= ci, ri > ci, ri == CHUNK - 1


def _tri_inv(L):
    ri = lax.broadcasted_iota(jnp.int32, (CHUNK, CHUNK), 0)
    ci = lax.broadcasted_iota(jnp.int32, (CHUNK, CHUNK), 1)
    T = jnp.where(ri == ci, 1.0, 0.0) - jnp.where((ri >> 1) == (ci >> 1), L, 0.0)
    for lv in range(2, int(math.log2(CHUNK)) + 1):
        O = jnp.where(((ri >> lv) == (ci >> lv)) & ((ri >> (lv - 1)) != (ci >> (lv - 1))), L, 0.0)
        if (1 << lv) <= NEAR_BLOCK:
            T = T - _dot_x3(T, _dot_x3(O, T))
        else:
            Tb = _bf(T)
            T = T - _dot(Tb, _bf(_dot(_bf(O), Tb)))
    return T


def _chunk_local(q, k, v, beta, g):
    ge, gt, last = _tri_masks()
    gam = _dot_mask(ge, g)
    D = jnp.where(ge, jnp.exp(jnp.where(ge, gam - gam.T, 0.0)), 0.0)
    kb = k * beta
    vb = v * beta
    M = _dot_nt(_bf(kb), _bf(k))
    L = jnp.where(gt, M * D, 0.0)
    eg = jnp.exp(gam)
    kbg = kb * eg
    P = _dot_nt(_bf(q), _bf(k))
    QK = jnp.where(ge, P * D, 0.0)
    gl = jnp.sum(jnp.where(last, gam, 0.0), axis=0, keepdims=True)
    el = jnp.exp(gl - gam)
    return dict(ge=ge, gt=gt, last=last, gam=gam, D=D, kb=kb, vb=vb, L=L, eg=eg, kbg=kbg, QK=QK, gl=gl,
                el=el, kd=k * el, qg=q * eg)


def _rowsum(x):
    return jnp.sum(x, axis=-1, keepdims=True)


def _chunk_bwd(q, k, v, beta, g, S, T, u, w, do, dS2):
    c = _chunk_local(q, k, v, beta, g)
    ge, gt, last = c["ge"], c["gt"], c["last"]
    Sb = _bf(S)
    vn = u - _dot(w, Sb)
    dob, vnb, dS2b = _bf(do), _bf(vn), _bf(dS2)
    e_last = jnp.exp(c["gl"])
    dqg = _dot_nt(dob, Sb)
    dS = _dot_tn(_bf(c["qg"]), dob)
    dQK = jnp.where(ge, _dot_nt(dob, vnb), 0.0)
    dvn = _dot_tn(_bf(c["QK"]), dob)
    dS = dS + dS2 * e_last
    de_last = jnp.sum(jnp.sum(dS2 * S, axis=0, keepdims=True), axis=1, keepdims=True)
    dkd = _dot_nt(vnb, dS2b)
    dvn = dvn + _dot(_bf(c["kd"]), dS2b)
    dvnb = _bf(dvn)
    dw = -_dot_nt(dvnb, Sb)
    dS = dS - _dot_tn(w, dvnb)
    dsol = _dot_x3(T, jnp.concatenate([dvn, dw], axis=1), _TN)
    dvb, dkbg = dsol[:, :LANES], dsol[:, LANES:]
    dA = -(_dot_nt(_bf(dvb), _bf(u)) + _dot_nt(_bf(dkbg), w))
    dL = jnp.where(gt, dA, 0.0)
    dM = dL * c["D"]
    dP = dQK * c["D"]
    E = dL * c["L"] + dQK * c["QK"]
    kbf = _bf(k)
    dkb = _dot(_bf(dM), kbf) + dkbg * c["eg"]
    dk = _dot_tn(_bf(dM), _bf(c["kb"])) + _dot_tn(_bf(dP), _bf(q)) + dkd * c["el"] + dkb * beta
    dq = _dot(_bf(dP), kbf) + dqg * c["eg"]
    s_kd = _rowsum(dkd * c["kd"])
    dgam = (_rowsum(E) - _rowsum(E.T) + _rowsum(dqg * c["qg"]) - s_kd + _rowsum(dkbg * c["kbg"]))
    dgl = jnp.sum(s_kd, axis=0, keepdims=True) + de_last * e_last
    dgam_rep = jnp.broadcast_to(dgam, (CHUNK, LANES)) + jnp.where(last, jnp.broadcast_to(dgl, (CHUNK, LANES)), 0.0)
    dg_rep = _dot_mask(ge, dgam_rep, _TN)
    dbeta = _rowsum(dkb * k) + _rowsum(dvb * v)
    dv = dvb * beta
    return dq, dk, dv, jnp.broadcast_to(dbeta, (CHUNK, LANES)), dg_rep, dS


def _gdn_core_fwd(qkv, g, beta, *, Bl, S, KW, VW, name):
    HQ = KW // LANES
    H = VW // LANES
    NC = S // CHUNK
    scale = float(LANES) ** -0.5

    PAIR = next(p for p in (4, 2, 1) if NC % p == 0)

    def body(q_ref, k_ref, v_ref, g_ref, beta_ref, o_ref, st_ref, t_ref, u_s, w_s,
             qk_s, qg_s, kd_s, el_s):
        def local(n2, carry):
            for half in range(PAIR):
                n = n2 * PAIR + half
                rows = pl.ds(pl.multiple_of(n * CHUNK, CHUNK), CHUNK)
                q = q_ref[rows, :] * scale
                k = k_ref[rows, :]
                for e in range(2):
                    c = _chunk_local(q, k, v_ref[rows, e * LANES:(e + 1) * LANES], beta_ref[e, rows, :],
                                     g_ref[e, rows, :])
                    T = _tri_inv(c["L"])
                    t_ref[e, rows, :] = T
                    uw = _dot_x3(T, jnp.concatenate([c["vb"], c["kbg"]], axis=1))
                    u_s[e, rows, :] = uw[:, :LANES]
                    w_s[e, rows, :] = _bf(uw[:, LANES:])
                    qk_s[e, rows, :] = _bf(c["QK"])
                    qg_s[e, rows, :] = _bf(c["qg"])
                    kd_s[e, rows, :] = _bf(c["kd"])
                    el_s[e, pl.ds(pl.multiple_of(n * 8, 8), 8), :] = jnp.broadcast_to(jnp.exp(c["gl"]), (8, LANES))
            return carry

        lax.fori_loop(0, NC // PAIR, local, 0)

        def scan(n, states):
            rows = pl.ds(pl.multiple_of(n * CHUNK, CHUNK), CHUNK)
            out = []
            for e in range(2):
                S_in = states[e]
                st_ref[e, n] = S_in
                Sb = _bf(S_in)
                vn = u_s[e, rows, :] - _dot(w_s[e, rows, :], Sb)
                vnb = _bf(vn)
                o_ref[rows, e * LANES:(e + 1) * LANES] = _dot(qg_s[e, rows, :], Sb) + _dot(qk_s[e, rows, :], vnb)
                e_last = el_s[e, pl.ds(pl.multiple_of(n * 8, 8), 1), :]
                out.append(S_in * e_last + _dot_tn(kd_s[e, rows, :], vnb))
            return tuple(out)

        z = jnp.zeros((LANES, LANES), F32)
        lax.fori_loop(0, NC, scan, (z, z))

    rep = pl.BlockSpec((2, S, LANES), lambda b, h: (h, b, 0))
    return pl.pallas_call(
        body, name=name, grid=(Bl, HQ),
        in_specs=[pl.BlockSpec((S, LANES), lambda b, h: (b, h)),
                  pl.BlockSpec((S, LANES), lambda b, h: (b, HQ + h)),
                  pl.BlockSpec((S, 2 * LANES), lambda b, h: (b, HQ + h)), rep, rep],
        out_specs=[pl.BlockSpec((S, 2 * LANES), lambda b, h: (b, h)),
                   pl.BlockSpec((None, 2, NC, LANES, LANES), lambda b, h: (b, h, 0, 0, 0)), rep, rep, rep],
        out_shape=[jax.ShapeDtypeStruct((Bl * S, VW), F32),
                   jax.ShapeDtypeStruct((Bl, H, NC, LANES, LANES), F32),
                   jax.ShapeDtypeStruct((H, Bl * S, LANES), F32),
                   jax.ShapeDtypeStruct((H, Bl * S, LANES), F32),
                   jax.ShapeDtypeStruct((H, Bl * S, LANES), BF16)],
        scratch_shapes=[pltpu.VMEM((2, S, LANES), BF16)] * 3 + [pltpu.VMEM((2, NC * 8, LANES), F32)],
        compiler_params=_cparams(("parallel", "parallel")),
    )(qkv, qkv, qkv, g, beta)


def _gdn_core_bwd(qkv, g, beta, states, tinv, u, w, do, *, Bl, S, KW, VW, name):
    HQ = KW // LANES
    H = VW // LANES
    NC = S // CHUNK
    scale = float(LANES) ** -0.5

    def body(q_ref, k_ref, v_ref, g_ref, beta_ref, st_ref, t_ref, u_ref, w_ref, do_ref,
             dq_ref, dk_ref, dv_ref, dg_ref, dbeta_ref):
        def step(i, dstates):
            n = NC - 1 - i
            rows = pl.ds(pl.multiple_of(n * CHUNK, CHUNK), CHUNK)
            q = q_ref[rows, :] * scale
            k = k_ref[rows, :]
            out = []
            dq_sum = dk_sum = None
            for e in range(2):
                cols = slice(e * LANES, (e + 1) * LANES)
                dq, dk, dv, dbeta, dg, dS = _chunk_bwd(q, k, v_ref[rows, cols], beta_ref[e, rows, :],
                                                       g_ref[e, rows, :], st_ref[e, n], t_ref[e, rows, :],
                                                       u_ref[e, rows, :], w_ref[e, rows, :],
                                                       do_ref[rows, cols], dstates[e])
                dv_ref[rows, cols] = dv
                dg_ref[e, rows, :] = dg
                dbeta_ref[e, rows, :] = dbeta
                dq_sum = dq if dq_sum is None else dq_sum + dq
                dk_sum = dk if dk_sum is None else dk_sum + dk
                out.append(dS)
            dq_ref[rows, :] = dq_sum * scale
            dk_ref[rows, :] = dk_sum
            return tuple(out)

        z = jnp.zeros((LANES, LANES), F32)
        lax.fori_loop(0, NC, step, (z, z))

    rep = pl.BlockSpec((2, S, LANES), lambda b, h: (h, b, 0))
    seq = pl.BlockSpec((S, LANES), lambda b, h: (b, h))
    seq2 = pl.BlockSpec((S, 2 * LANES), lambda b, h: (b, h))
    return pl.pallas_call(
        body, name=name, grid=(Bl, HQ),
        in_specs=[seq, pl.BlockSpec((S, LANES), lambda b, h: (b, HQ + h)),
                  pl.BlockSpec((S, 2 * LANES), lambda b, h: (b, HQ + h)), rep, rep,
                  pl.BlockSpec((None, 2, NC, LANES, LANES), lambda b, h: (b, h, 0, 0, 0)), rep, rep, rep, seq2],
        out_specs=[seq, seq, seq2, rep, rep],
        out_shape=[jax.ShapeDtypeStruct((Bl * S, KW), F32), jax.ShapeDtypeStruct((Bl * S, KW), F32),
                   jax.ShapeDtypeStruct((Bl * S, VW), F32),
                   jax.ShapeDtypeStruct((H, Bl * S, LANES), F32), jax.ShapeDtypeStruct((H, Bl * S, LANES), F32)],
        compiler_params=_cparams(("parallel", "parallel")),
    )(qkv, qkv, qkv, g, beta, states, tinv, u, w, do)


def _gdn_out_fwd(o, p, norm_g, out_a, *, CC, VW, name, tr=256):
    T = o.shape[0]
    tr = min(tr, T)
    H = VW // LANES
    zoff = p.shape[1] // VW - 1

    def body(o_ref, z_ref, ng_ref, a_ref, mix_ref):
        mix_ref[:, :CC] = a_ref[...]
        for h in range(H):
            cols = slice(h * LANES, (h + 1) * LANES)
            ov = o_ref[:, cols]
            r = lax.rsqrt(jnp.mean(ov * ov, axis=-1, keepdims=True) + EPS)
            mix_ref[:, CC + h * LANES:CC + (h + 1) * LANES] = (ov * r * ng_ref[...] * _silu(z_ref[:, cols])).astype(BF16)

    return pl.pallas_call(
        body, name=name, grid=(T // tr,),
        in_specs=[pl.BlockSpec((tr, VW), lambda i: (i, 0)), pl.BlockSpec((tr, VW), lambda i: (i, zoff)),
                  pl.BlockSpec((1, LANES), lambda i: (0, 0)), pl.BlockSpec((tr, CC), lambda i: (i, 0))],
        out_specs=pl.BlockSpec((tr, CC + VW), lambda i: (i, 0)),
        out_shape=jax.ShapeDtypeStruct((T, CC + VW), BF16),
        compiler_params=_cparams(("parallel",)),
    )(o, p, norm_g.reshape(1, LANES), out_a)


def _gdn_out_bwd(o, p, norm_g, dmix, *, CC, VW, name, tr=256):
    T = o.shape[0]
    tr = min(tr, T)
    H = VW // LANES
    zoff = p.shape[1] // VW - 1

    def body(o_ref, z_ref, ng_ref, dmix_ref, do_ref, dz_ref, da_ref, dng_ref, dpb_ref):
        first = pl.program_id(0) == 0
        da = dmix_ref[:, :CC]
        da_ref[...] = da.astype(BF16)
        _acc_row(dpb_ref, 0, jnp.sum(da, axis=0, keepdims=True), first)
        ng = ng_ref[...]
        dng = jnp.zeros((1, LANES), F32)
        for h in range(H):
            cols = slice(h * LANES, (h + 1) * LANES)
            ov = o_ref[:, cols]
            zv = z_ref[:, cols]
            dout = dmix_ref[:, CC + h * LANES:CC + (h + 1) * LANES]
            r = lax.rsqrt(jnp.mean(ov * ov, axis=-1, keepdims=True) + EPS)
            on = ov * r * ng
            don = dout * _silu(zv)
            dz_ref[:, cols] = (dout * on * _dsilu(zv)).astype(BF16)
            dng = dng + jnp.sum(don * ov * r, axis=0, keepdims=True)
            dong = don * ng
            do_ref[:, cols] = r * dong - ov * (r * r * r) * jnp.mean(dong * ov, axis=-1, keepdims=True)
        _acc_row(dng_ref, 0, dng, first)

    return pl.pallas_call(
        body, name=name, grid=(T // tr,),
        in_specs=[pl.BlockSpec((tr, VW), lambda i: (i, 0)), pl.BlockSpec((tr, VW), lambda i: (i, zoff)),
                  pl.BlockSpec((1, LANES), lambda i: (0, 0)), pl.BlockSpec((tr, CC + VW), lambda i: (i, 0))],
        out_specs=[pl.BlockSpec((tr, VW), lambda i: (i, 0)), pl.BlockSpec((tr, VW), lambda i: (i, 0)),
                   pl.BlockSpec((tr, CC), lambda i: (i, 0)), pl.BlockSpec((1, LANES), lambda i: (0, 0)),
                   pl.BlockSpec((1, CC), lambda i: (0, 0))],
        out_shape=[jax.ShapeDtypeStruct((T, VW), F32), jax.ShapeDtypeStruct((T, VW), BF16),
                   jax.ShapeDtypeStruct((T, CC), BF16), jax.ShapeDtypeStruct((1, LANES), F32),
                   jax.ShapeDtypeStruct((1, CC), F32)],
        compiler_params=_cparams(("arbitrary",)),
    )(o, p, norm_g.reshape(1, LANES), dmix)


FFN_CW = 256


def _ffn_act_fwd(gu, conv_w, conv_b, *, Bl, S, name):
    FF = gu.shape[2]
    cw = min(FFN_CW, FF)

    def body(g_ref, u_ref, w_ref, b_ref, a_ref):
        gc = _conv_fwd(g_ref[...], w_ref, FFN_CONV_K) + b_ref[...]
        a_ref[...] = (_silu(gc) * u_ref[...]).astype(BF16)

    return pl.pallas_call(
        body, name=name, grid=(Bl, FF // cw),
        in_specs=[pl.BlockSpec((None, S, cw), lambda b, j: (0, b, j)),
                  pl.BlockSpec((None, S, cw), lambda b, j: (1, b, j)),
                  pl.BlockSpec((FFN_CONV_K, cw), lambda b, j: (0, j)),
                  pl.BlockSpec((1, cw), lambda b, j: (0, j))],
        out_specs=pl.BlockSpec((S, cw), lambda b, j: (b, j)),
        out_shape=jax.ShapeDtypeStruct((Bl * S, FF), BF16),
        compiler_params=_cparams(("parallel", "parallel")),
    )(gu, gu, conv_w, conv_b.reshape(1, FF))


def _ffn_act_bwd(gu, conv_w, conv_b, da, *, Bl, S, name):
    FF = gu.shape[2]
    cw = min(FFN_CW, FF)

    def body(g_ref, u_ref, w_ref, b_ref, da_ref, dgu_ref, dw_ref, db_ref):
        first = pl.program_id(1) == 0
        gate = g_ref[...]
        gc = _conv_fwd(gate, w_ref, FFN_CONV_K) + b_ref[...]
        dav = da_ref[...]
        dgu_ref[1] = (dav * _silu(gc)).astype(BF16)
        dgc = dav * u_ref[...] * _dsilu(gc)
        _acc_row(db_ref, 0, jnp.sum(dgc, axis=0, keepdims=True), first)
        _conv_bwd_w(dgc, gate, dw_ref, FFN_CONV_K, first)
        dgu_ref[0] = _conv_bwd_in(dgc, w_ref, FFN_CONV_K).astype(BF16)

    return pl.pallas_call(
        body, name=name, grid=(FF // cw, Bl),
        in_specs=[pl.BlockSpec((None, S, cw), lambda j, b: (0, b, j)),
                  pl.BlockSpec((None, S, cw), lambda j, b: (1, b, j)),
                  pl.BlockSpec((FFN_CONV_K, cw), lambda j, b: (0, j)),
                  pl.BlockSpec((1, cw), lambda j, b: (0, j)),
                  pl.BlockSpec((S, cw), lambda j, b: (b, j))],
        out_specs=[pl.BlockSpec((2, S, cw), lambda j, b: (0, b, j)),
                   pl.BlockSpec((FFN_CONV_K, cw), lambda j, b: (0, j)),
                   pl.BlockSpec((1, cw), lambda j, b: (0, j))],
        out_shape=[jax.ShapeDtypeStruct((2, Bl * S, FF), BF16),
                   jax.ShapeDtypeStruct((FFN_CONV_K, FF), F32), jax.ShapeDtypeStruct((1, FF), F32)],
        compiler_params=_cparams(("parallel", "arbitrary")),
    )(gu, gu, conv_w, conv_b.reshape(1, FF), da)


def _layer_dims(W):
    CC = W["conv_pw_b"].shape[0]
    VW = W["mix_norm_g"].shape[0] - CC
    KW = (W["gdn_conv_w"].shape[1] - VW) // 2
    return CC, KW, VW


def _layer_fwd(l, x, W, Bl, S, fetch):
    CC, KW, VW = _layer_dims(W)
    H = VW // LANES
    w_in_t, w_in_ba = fetch(l, "w_in", x)
    n_main = (w_in_t.shape[0] // LANES) * LANES
    h1 = _rms_fwd(x, W["mix_norm_g"], name="rms1_fwd")
    p = _mm(h1, w_in_t, tb=True, b_rows=n_main, name="mm_in")
    pba = _mm(h1, w_in_ba, tb=True, name="mm_in_ba")
    u3, u1 = _conf_fwd(p, W["conv_dw_w"], W["conv_dw_b"], W["conv_ln_g"], W["conv_ln_b"], Bl=Bl, S=S, CC=CC,
                       name="conf_fwd")
    conv_pw_w = fetch(l, "conv_pw_w", u3)
    out_a = _mm(u3, conv_pw_w, bias=W["conv_pw_b"], out_dtype=BF16, name="mm_pw")
    qkv = _gdn_pre_fwd(p, W["gdn_conv_w"], Bl=Bl, S=S, CC=CC, KW=KW, VW=VW, name="gdn_pre_fwd")
    g, beta = _gdn_gate_fwd(pba, W["gdn_a_log"], W["gdn_dt_bias"], Bl=Bl, S=S, H=H, name="gdn_gate_fwd")
    o, states, tinv, gdn_u, gdn_w = _gdn_core_fwd(qkv, g, beta, Bl=Bl, S=S, KW=KW, VW=VW, name="gdn_core_fwd")
    mix = _gdn_out_fwd(o, p, W["gdn_norm_g"], out_a, CC=CC, VW=VW, name="gdn_out_fwd")
    w_out = fetch(l, "w_out", mix)
    x1, h2 = _mm_res_norm(mix, w_out, x, W["ffn_norm_g"], name="mm_out_norm")
    w_up = fetch(l, "w_up", h2)
    gu = _mm(h2, w_up, out_blocks=2, tn=w_up.shape[2], name="mm_up")
    a = _ffn_act_fwd(gu, W["ffn_conv_w"], W["ffn_conv_b"], Bl=Bl, S=S, name="ffn_act_fwd")
    w_down = fetch(l, "w_down", a)
    x2 = _mm(a, w_down, res=x1, name="mm_down")
    saved = dict(x=x, h1=h1, p=p, pba=pba, u1=u1, u3=u3, qkv=qkv, g=g, beta=beta, o=o, states=states, tinv=tinv,
                 gdn_u=gdn_u, gdn_w=gdn_w, mix=mix, x1=x1, h2=h2, gu=gu, a=a, w_in_t=w_in_t, w_in_ba=w_in_ba, conv_pw_w=conv_pw_w,
                 w_out=w_out, w_up=w_up, w_down=w_down)
    return x2, saved


def _layer_bwd(l, dx2, dx2b, W, A, Bl, S, sink):
    CC, KW, VW = _layer_dims(W)
    H = VW // LANES
    G = {}
    upw = A["w_up"].shape[2]
    da = _mm(dx2b, A["w_down"], tb=True, tn=upw, name="mm_down_dx")
    da = sink(l, "w_down", _mm(A["a"], dx2b, ta=True, out_dtype=BF16, tm=upw, name="mm_down_dw"), da)
    dgu, G["ffn_conv_w"], G["ffn_conv_b"] = _ffn_act_bwd(A["gu"], W["ffn_conv_w"], W["ffn_conv_b"], da,
                                                         Bl=Bl, S=S, name="ffn_act_bwd")
    dh2 = _mm(dgu, A["w_up"], tb=True, tk=upw, tn=2048, name="mm_up_dx")
    dh2 = sink(l, "w_up", _mm(A["h2"], dgu, ta=True, out_dtype=BF16, out_blocks=N_DEV, tn=upw, name="mm_up_dw"),
               dh2)
    dx1, dx1b, G["ffn_norm_g"] = _rms_bwd(A["x1"], W["ffn_norm_g"], dh2, dx2, name="rms2_bwd")
    dmix = _mm(dx1b, A["w_out"], tb=True, name="mm_out_dx")
    dmix = sink(l, "w_out", _mm(A["mix"], dx1b, ta=True, out_dtype=BF16, name="mm_out_dw"), dmix)
    do, dz, dout_a, G["gdn_norm_g"], G["conv_pw_b"] = _gdn_out_bwd(A["o"], A["p"], W["gdn_norm_g"], dmix,
                                                                   CC=CC, VW=VW, name="gdn_out_bwd")
    dq, dk, dv, dg, dbeta = _gdn_core_bwd(A["qkv"], A["g"], A["beta"], A["states"], A["tinv"], A["gdn_u"],
                                          A["gdn_w"], do, Bl=Bl, S=S, KW=KW, VW=VW, name="gdn_core_bwd")
    dpba, dalog, ddtb = _gdn_gate_bwd(A["pba"], W["gdn_a_log"], W["gdn_dt_bias"], dg, dbeta, Bl=Bl, S=S, H=H,
                                      name="gdn_gate_bwd")
    G["gdn_a_log"], G["gdn_dt_bias"] = dalog[:H, 0], ddtb[:H, 0]
    dqkv, G["gdn_conv_w"] = _gdn_pre_bwd(A["p"], W["gdn_conv_w"], dq, dk, dv, Bl=Bl, S=S, CC=CC, KW=KW, VW=VW,
                                         name="gdn_pre_bwd")
    du3 = _mm(dout_a, A["conv_pw_w"], tb=True, name="mm_pw_dx")
    du3 = sink(l, "conv_pw_w", _mm(A["u3"], dout_a, ta=True, out_dtype=BF16, name="mm_pw_dw"), du3)
    dav, dag, G["conv_dw_w"], G["conv_dw_b"], G["conv_ln_g"], G["conv_ln_b"] = _conf_bwd(
        A["p"], A["u1"], W["conv_dw_w"], W["conv_ln_g"], W["conv_ln_b"], du3, Bl=Bl, S=S, CC=CC, name="conf_bwd")
    dp = jnp.concatenate([dav, dag, dqkv, dz], axis=1)
    dp = sink(l, "w_in", (_mm(dp, A["h1"], ta=True, out_dtype=BF16, name="mm_in_dw"),
                          _mm(dpba, A["h1"], ta=True, out_dtype=BF16, name="mm_in_ba_dw")), dp)
    dh1 = _mm(dpba, A["w_in_ba"], name="mm_in_ba_dx")
    dh1 = _mm(dp, A["w_in_t"], b_rows=dp.shape[1], res=dh1, name="mm_in_dx")
    dx, dxb, G["mix_norm_g"] = _rms_bwd(A["x"], W["mix_norm_g"], dh1, dx1, name="rms1_bwd")
    return dx, dxb, G


def _local_step(x, target, Ws, final_norm_g, fetch, sink):
    Bl, S, D = x.shape
    xt = x.reshape(Bl * S, D)
    acts = []
    for l, W in enumerate(Ws):
        xt, A = _layer_fwd(l, xt, W, Bl, S, fetch)
        acts.append(A)
    loss, dx, dxb, dgf = _loss_head(xt, final_norm_g, target.reshape(Bl * S, D), name="loss_head")
    grads = [None] * len(Ws)
    for l in reversed(range(len(Ws))):
        dx, dxb, grads[l] = _layer_bwd(l, dx, dxb, Ws[l], acts[l], Bl, S, sink)
    return loss[0, 0], dx.reshape(Bl, S, D), grads, dgf.reshape(D)


def _mesh_pos():
    return lax.axis_index("x"), lax.axis_index("y"), lax.axis_index("c")


def _dev_index(px, py, pc):
    return 4 * px + 2 * py + pc


_ANY = pl.BlockSpec(memory_space=pl.ANY)


def _all_gather(arrs, *, name):
    n = len(arrs)

    def body(*refs):
        ins, outs = refs[:n], refs[n:2 * n]
        send_sems, recv_sems, local_sems = refs[2 * n:]
        x, y, c = _mesh_pos()
        me, sibling = (x, y, c), (x, y, 1 - c)
        chips = [(1 - x, y), (x, 1 - y), (1 - x, 1 - y)]

        def copy(a, k, block, to, src=None):
            dst = outs[a].at[_dev_index(*block)]
            return pltpu.make_async_remote_copy(
                src_ref=dst if src is None else src, dst_ref=dst,
                send_sem=send_sems.at[a, k], recv_sem=recv_sems.at[a, k],
                device_id=to, device_id_type=MESH)

        mine = [pltpu.make_async_copy(ins[a], outs[a].at[_dev_index(*me)], local_sems.at[a]) for a in range(n)]
        for cp in mine:
            cp.start()
        first = []
        for a in range(n):
            first.append(copy(a, 0, me, sibling, src=ins[a]))
            first += [copy(a, 1 + j, me, (*chip, c), src=ins[a]) for j, chip in enumerate(chips)]
        for cp in first:
            cp.start()
        passed = []
        for a in range(n):
            for j, chip in enumerate(chips):
                copy(a, 1 + j, (*chip, c), me).wait_recv()
                fwd = copy(a, 4 + j, (*chip, c), sibling)
                fwd.start()
                passed.append(fwd)
        for a in range(n):
            copy(a, 0, sibling, me).wait_recv()
            for j, chip in enumerate(chips):
                copy(a, 4 + j, (*chip, 1 - c), me).wait_recv()
        for cp in first + passed:
            cp.wait_send()
        for cp in mine:
            cp.wait()

    return pl.pallas_call(
        body, name=name,
        in_specs=[_ANY] * n, out_specs=[_ANY] * n,
        out_shape=[jax.ShapeDtypeStruct((N_DEV,) + a.shape, a.dtype) for a in arrs],
        scratch_shapes=[pltpu.SemaphoreType.DMA((n, N_DEV - 1)), pltpu.SemaphoreType.DMA((n, N_DEV - 1)),
                        pltpu.SemaphoreType.DMA((n,))],
    )(*arrs)


def _peers(x, y, c):
    flip = lambda v, f: 1 - v if f else v
    return [(flip(x, p & 4), flip(y, p & 2), flip(c, p & 1)) for p in range(1, N_DEV)]


GATHER_ID, SCATTER_ID = 1, 2
_SEQUENCER = dict(axis_name="sequencer", num_cores=1)


def _handshake(peers):
    barrier = pltpu.get_barrier_semaphore()
    for peer in peers:
        pl.semaphore_signal(barrier, inc=1, device_id=peer, device_id_type=MESH)
    pl.semaphore_wait(barrier, len(peers))


def _sc_gather(src, *, name):
    def body(src_ref, zone_ref, send_sems, recv_sems, local_sem):
        x, y, c = _mesh_pos()
        me, sibling = (x, y, c), (x, y, 1 - c)
        chips = [(1 - x, y), (x, 1 - y), (1 - x, 1 - y)]
        _handshake([sibling] + [(*chip, c) for chip in chips])

        def copy(k, block, to, from_src=False):
            dst = zone_ref.at[_dev_index(*block)]
            return pltpu.make_async_remote_copy(
                src_ref=src_ref if from_src else dst, dst_ref=dst, send_sem=send_sems.at[k], recv_sem=recv_sems.at[k],
                device_id=to, device_id_type=MESH)

        mine = pltpu.make_async_copy(src_ref, zone_ref.at[_dev_index(*me)], local_sem)
        mine.start()
        first = [copy(1 + j, me, (*chip, c), from_src=True) for j, chip in enumerate(chips)]
        first.append(copy(0, me, sibling, from_src=True))
        for cp in first:
            cp.start()
        passed = []
        for j, chip in enumerate(chips):
            copy(1 + j, (*chip, c), me).wait_recv()
            fwd = copy(4 + j, (*chip, c), sibling)
            fwd.start()
            passed.append(fwd)
        copy(0, sibling, me).wait_recv()
        for j, chip in enumerate(chips):
            copy(4 + j, (*chip, 1 - c), me).wait_recv()
        for cp in first + passed:
            cp.wait_send()
        mine.wait()

    return pl.kernel(
        body, name=name,
        out_type=jax.ShapeDtypeStruct((N_DEV,) + src.shape, src.dtype),
        mesh=plsc.ScalarSubcoreMesh(**_SEQUENCER),
        scratch_types=[pltpu.SemaphoreType.DMA((N_DEV - 1,)), pltpu.SemaphoreType.DMA((N_DEV - 1,)),
                       pltpu.SemaphoreType.DMA],
        compiler_params=pltpu.CompilerParams(collective_id=GATHER_ID),
    )(src)


def _sc_scatter(part, *, name):
    def body(src_ref, zone_ref, send_sems, recv_sems, local_sem):
        x, y, c = _mesh_pos()
        me = _dev_index(x, y, c)
        peers = _peers(x, y, c)
        _handshake(peers)
        mine = pltpu.make_async_copy(src_ref.at[me], zone_ref.at[me], local_sem)
        mine.start()
        sends = [pltpu.make_async_remote_copy(
            src_ref=src_ref.at[_dev_index(*peer)], dst_ref=zone_ref.at[me], send_sem=send_sems.at[k],
            recv_sem=recv_sems.at[k], device_id=peer, device_id_type=MESH) for k, peer in enumerate(peers)]
        for cp in sends:
            cp.start()
        for k, peer in enumerate(peers):
            pltpu.make_async_remote_copy(
                src_ref=src_ref.at[me], dst_ref=zone_ref.at[_dev_index(*peer)], send_sem=send_sems.at[k],
                recv_sem=recv_sems.at[k], device_id=peer, device_id_type=MESH).wait_recv()
        for cp in sends:
            cp.wait_send()
        mine.wait()

    return pl.kernel(
        body, name=name,
        out_type=jax.ShapeDtypeStruct(part.shape, part.dtype),
        mesh=plsc.ScalarSubcoreMesh(**_SEQUENCER),
        scratch_types=[pltpu.SemaphoreType.DMA((N_DEV - 1,)), pltpu.SemaphoreType.DMA((N_DEV - 1,)),
                       pltpu.SemaphoreType.DMA],
        compiler_params=pltpu.CompilerParams(collective_id=SCATTER_ID),
    )(part)


def _adamw_math(w, g, m, v):
    m2 = ADAM_B1 * m + (1.0 - ADAM_B1) * g
    v2 = ADAM_B2 * v + (1.0 - ADAM_B2) * (g * g)
    m_hat = m2 / (1.0 - ADAM_B1 ** ADAM_STEP)
    v_hat = v2 / (1.0 - ADAM_B2 ** ADAM_STEP)
    delta = -ADAM_LR * (m_hat / (jnp.sqrt(v_hat) + ADAM_EPS) + ADAM_WD * w)
    return delta, m2, v2


def _adamw_big(l, w, m, v, recv, prev, *, summed=False, name, tr=128):
    L, R, C = w.shape
    tr = next(t for t in range(min(tr, R), 0, -16) if R % t == 0)

    def body(w_ref, m_ref, v_ref, r_ref, *rest):
        g_ref, d_ref, m2_ref, v2_ref = rest[-4:]
        if summed:
            g = r_ref[...]
        else:
            g = r_ref[0].astype(F32)
            for s in range(1, N_DEV):
                g = g + r_ref[s].astype(F32)
        g_ref[...] = g
        d_ref[...], m2_ref[...], v2_ref[...] = _adamw_math(w_ref[...], g, m_ref[...], v_ref[...])

    wspec = pl.BlockSpec((None, tr, C), lambda i: (l, i, 0))
    rspec = pl.BlockSpec((tr, C), lambda i: (i, 0)) if summed else pl.BlockSpec((N_DEV, tr, C), lambda i: (0, i, 0))
    prev = list(prev) if prev is not None else []
    return pl.pallas_call(
        body, name=name, grid=(R // tr,),
        in_specs=[wspec, wspec, wspec, rspec] + [_ANY] * len(prev),
        out_specs=[wspec] * 4,
        out_shape=[jax.ShapeDtypeStruct((L, R, C), F32)] * 4,
        input_output_aliases={4 + j: j for j in range(len(prev))},
        compiler_params=_cparams(("parallel",)),
    )(w, m, v, recv if summed else recv.reshape(N_DEV, R, C), *prev)


def _sum_slots_wide(recv, *, name, tc=512):
    _, R, C = recv.shape
    tc = _tile(C, tc)

    def body(r_ref, o_ref):
        g = r_ref[0].astype(F32)
        for s in range(1, N_DEV):
            g = g + r_ref[s].astype(F32)
        o_ref[...] = g

    return pl.pallas_call(
        body, name=name, grid=(C // tc,),
        in_specs=[pl.BlockSpec((N_DEV, R, tc), lambda j: (0, 0, j))],
        out_specs=pl.BlockSpec((R, tc), lambda j: (0, j)),
        out_shape=jax.ShapeDtypeStruct((R, C), F32),
        compiler_params=_cparams(("parallel",)),
    )(recv)


def _sum_slots(gathered, *, name):
    _, R, C = gathered.shape

    def body(r_ref, o_ref):
        g = r_ref[0]
        for s in range(1, N_DEV):
            g = g + r_ref[s]
        o_ref[...] = g

    return pl.pallas_call(body, name=name, out_shape=jax.ShapeDtypeStruct((R, C), F32))(gathered)


def _adamw_small(ws, gs, ms, vs, *, name):
    n = len(ws)

    def body(*refs):
        for i in range(n):
            w_ref, g_ref, m_ref, v_ref = (refs[k * n + i] for k in range(4))
            d_ref, m2_ref, v2_ref = (refs[(4 + k) * n + i] for k in range(3))
            d_ref[...], m2_ref[...], v2_ref[...] = _adamw_math(w_ref[...], g_ref[...], m_ref[...], v_ref[...])

    out = pl.pallas_call(body, name=name,
                         out_shape=[jax.ShapeDtypeStruct(a.shape, F32) for a in ws] * 3)(*ws, *gs, *ms, *vs)
    return out[:n], out[n:2 * n], out[2 * n:]


def _pack(arrs):
    flat = []
    for a in arrs:
        a = a.reshape(-1).astype(F32)
        flat.append(jnp.pad(a, (0, (-a.shape[0]) % LANES)))
    out = jnp.concatenate(flat)
    out = jnp.pad(out, (0, (-out.shape[0]) % (8 * LANES)))
    return out.reshape(-1, LANES)


def _unpack(packed, shapes):
    flat = packed.reshape(-1)
    out, pos = [], 0
    for s in shapes:
        size = math.prod(s)
        out.append(flat[pos:pos + size].reshape(s))
        pos += size + (-size) % LANES
    return out


BIG = ("w_in", "conv_pw_w", "w_out", "w_up", "w_down")
SMALL_SHARDED = ("conv_dw_w", "gdn_conv_w", "ffn_conv_w")
SMALL_REPLICATED = ("mix_norm_g", "conv_dw_b", "conv_ln_g", "conv_ln_b", "conv_pw_b", "gdn_a_log", "gdn_dt_bias",
                    "gdn_norm_g", "ffn_norm_g", "ffn_conv_b")
WEIGHTS = ("mix_norm_g", "w_in", "conv_dw_w", "conv_dw_b", "conv_ln_g", "conv_ln_b", "conv_pw_w", "conv_pw_b",
           "gdn_conv_w", "gdn_a_log", "gdn_dt_bias", "gdn_norm_g", "w_out", "ffn_norm_g", "w_up", "ffn_conv_w",
           "ffn_conv_b", "w_down", "final_norm_g")


def _train_step(x, target, w, m, v):
    L = w["w_in"].shape[0]
    D = x.shape[-1]
    xi, yi, ci = _mesh_pos()
    me = _dev_index(xi, yi, ci)

    gathered = {}

    def launch(l, after=None):
        for n in BIG:
            src = (w[n][l].T if n == "w_in" else w[n][l]).astype(BF16)
            if after is not None:
                src = lax.optimization_barrier((src, after))[0]
            gathered[n, l] = _sc_gather(src, name=f"gather_{n}_{l}")

    launch(0)
    small_full = {}
    for n, g_ in zip(SMALL_SHARDED, _all_gather([w[n] for n in SMALL_SHARDED], name="all_gather_conv_taps")):
        small_full[n] = jnp.moveaxis(g_, 0, 2).reshape(L, g_.shape[2], N_DEV * g_.shape[3])
    Ws = []
    for l in range(L):
        W = {n: w[n][l] for n in SMALL_REPLICATED}
        W.update({n: small_full[n][l] for n in SMALL_SHARDED})
        Ws.append(W)

    def fetch(l, n, after):
        if n == "conv_pw_w" and l + 1 < L:
            launch(l + 1, after)
        g_ = lax.optimization_barrier((gathered[n, l], after))[0]
        if n == "w_up":
            return g_
        g_ = g_.reshape(g_.shape[0] * g_.shape[1], g_.shape[2])
        if n == "w_in":
            n_main = (g_.shape[0] // LANES) * LANES
            return g_, jnp.pad(g_[n_main:], ((0, LANES - (g_.shape[0] - n_main)), (0, 0)))
        return g_

    started = []
    res = {}
    SCATTERS_IN_FLIGHT = 2

    def consume(chain):
        n, l, recv = started.pop(0)
        if chain is not None:
            recv, chain = lax.optimization_barrier((recv, chain))
        if n == "w_in":
            recv = _sum_slots_wide(recv, name="sum_w_in_grad").T
        res[n] = _adamw_big(l, w[n], m[n], v[n], recv, res.get(n), summed=(n == "w_in"), name=f"adamw_{n}")
        if chain is None:
            return None
        tied = lax.optimization_barrier((chain, *res[n]))
        res[n] = list(tied[1:])
        return tied[0]

    def sink(l, n, g_, chain):
        g_, chain = lax.optimization_barrier((g_, chain))
        if len(started) >= SCATTERS_IN_FLIGHT:
            chain = consume(chain)
        if n == "w_in":
            g_main, g_ba = g_
            g_ = jnp.concatenate([g_main, g_ba[:w["w_in"].shape[2] * N_DEV - g_main.shape[0]]], axis=0)
            part = g_.reshape(N_DEV, -1, D)
        elif n == "w_up":
            part = g_
        else:
            part = g_.reshape(N_DEV, -1, g_.shape[1])
        started.append((n, l, _sc_scatter(part, name=f"scatter_{n}_{l}")))
        return chain

    loss, grad_x, G, d_final = _local_step(x, target, Ws, w["final_norm_g"], fetch, sink)

    small_names = [n for n in WEIGHTS if n not in BIG]
    partial = []
    for n in small_names:
        if n == "final_norm_g":
            partial.append(d_final)
        else:
            partial.append(jnp.stack([G[l][n].reshape(Ws[l][n].shape) for l in range(L)]))
    partial.append(loss.reshape(1))
    packed = _pack(partial)
    if started:
        packed = lax.optimization_barrier((packed, started[-1][2]))[0]
    small_gathered = _sc_gather(packed, name="gather_small_grads")

    out = {k: {} for k in ("grad", "delta", "new_m", "new_v")}
    while started:
        grad_x = consume(grad_x)
    for n in BIG:
        for j, k in enumerate(("grad", "delta", "new_m", "new_v")):
            out[k][n] = res[n][j]

    summed = _unpack(_sum_slots(small_gathered, name="sum_small_grads"), [p_.shape for p_ in partial])
    full = dict(zip(small_names, summed))
    loss = summed[-1][0]
    for n in SMALL_SHARDED:
        width = w[n].shape[-1]
        full[n] = lax.dynamic_slice_in_dim(full[n], me * width, width, axis=2)
    flat = lambda a: a.reshape(-1, a.shape[-1])
    res = _adamw_small(*[[flat(src[n]) for n in small_names] for src in (w, full, m, v)], name="adamw_small")
    out["grad"].update({n: full[n] for n in small_names})
    for k, arrs in zip(("delta", "new_m", "new_v"), res):
        out[k].update({n: a.reshape(w[n].shape) for n, a in zip(small_names, arrs)})
    return loss, grad_x, out


def kernel(x, mix_norm_g, w_in, conv_dw_w, conv_dw_b, conv_ln_g, conv_ln_b, conv_pw_w, conv_pw_b, gdn_conv_w, gdn_a_log, gdn_dt_bias, gdn_norm_g, w_out, ffn_norm_g, w_up, ffn_conv_w, ffn_conv_b, w_down, final_norm_g, loss_target, m_mix_norm_g, m_w_in, m_conv_dw_w, m_conv_dw_b, m_conv_ln_g, m_conv_ln_b, m_conv_pw_w, m_conv_pw_b, m_gdn_conv_w, m_gdn_a_log, m_gdn_dt_bias, m_gdn_norm_g, m_w_out, m_ffn_norm_g, m_w_up, m_ffn_conv_w, m_ffn_conv_b, m_w_down, m_final_norm_g, v_mix_norm_g, v_w_in, v_conv_dw_w, v_conv_dw_b, v_conv_ln_g, v_conv_ln_b, v_conv_pw_w, v_conv_pw_b, v_gdn_conv_w, v_gdn_a_log, v_gdn_dt_bias, v_gdn_norm_g, v_w_out, v_ffn_norm_g, v_w_up, v_ffn_conv_w, v_ffn_conv_b, v_w_down, v_final_norm_g):
    w = dict(zip(WEIGHTS, (mix_norm_g, w_in, conv_dw_w, conv_dw_b, conv_ln_g, conv_ln_b, conv_pw_w, conv_pw_b, gdn_conv_w,
                           gdn_a_log, gdn_dt_bias, gdn_norm_g, w_out, ffn_norm_g, w_up, ffn_conv_w, ffn_conv_b, w_down,
                           final_norm_g)))
    m = dict(zip(WEIGHTS, (m_mix_norm_g, m_w_in, m_conv_dw_w, m_conv_dw_b, m_conv_ln_g, m_conv_ln_b, m_conv_pw_w,
                           m_conv_pw_b, m_gdn_conv_w, m_gdn_a_log, m_gdn_dt_bias, m_gdn_norm_g, m_w_out, m_ffn_norm_g,
                           m_w_up, m_ffn_conv_w, m_ffn_conv_b, m_w_down, m_final_norm_g)))
    v = dict(zip(WEIGHTS, (v_mix_norm_g, v_w_in, v_conv_dw_w, v_conv_dw_b, v_conv_ln_g, v_conv_ln_b, v_conv_pw_w,
                           v_conv_pw_b, v_gdn_conv_w, v_gdn_a_log, v_gdn_dt_bias, v_gdn_norm_g, v_w_out, v_ffn_norm_g,
                           v_w_up, v_ffn_conv_w, v_ffn_conv_b, v_w_down, v_final_norm_g)))
    loss, grad_x, out = _train_step(x, loss_target, w, m, v)
    return (loss, grad_x, *[out["grad"][n] for n in WEIGHTS], *[out["delta"][n] for n in WEIGHTS],
            *[out["new_m"][n] for n in WEIGHTS], *[out["new_v"][n] for n in WEIGHTS])
```
